```python
import math
import jax, jax.numpy as jnp
from jax import lax
import numpy as np

D_MODEL = 1024
BATCH = 8
SEQ = 8192
DEPTH = 1

CONV_A_WIDTH = 512
CONV_A_K = 3
DN_HEADS = 8
DN_DK = 128
DN_DV = 128
DN_CONV_K = 5
CHUNK = 64
DN_QK = DN_HEADS * DN_DK
DN_VW = DN_HEADS * DN_DV
N_BRANCH = 2
D_FF = 2816
N_MOD = 9
EPS = 1e-6

SPLIT_SIZES = (
    CONV_A_WIDTH,
    CONV_A_WIDTH,
    CONV_A_WIDTH,
    2 * DN_QK + DN_VW,
    DN_VW,
    4 * DN_HEADS,
    N_BRANCH * D_MODEL,
)
IN_COLS = sum(SPLIT_SIZES)
SPLIT_OFFSETS = tuple(int(v) for v in np.cumsum(SPLIT_SIZES)[:-1])

kernel_name = "bidir_hybrid_conv_gdn_macaron_adaln"


def _rmsnorm(x, w):
    xf = x.astype(jnp.float32)
    y = xf * lax.rsqrt(jnp.mean(xf * xf, axis=-1, keepdims=True) + EPS)
    return (y * w.astype(jnp.float32)).astype(x.dtype)


def _l2norm(x):
    return x * lax.rsqrt(jnp.sum(x * x, axis=-1, keepdims=True) + EPS)


def _modulate(x, shift, scale):
    return x * (1.0 + scale) + shift


def _swiglu(x, w_up, w_down):
    a, b = jnp.split(x @ w_up, 2, axis=-1)
    return (jax.nn.silu(a) * b) @ w_down


def _dwconv(x, w):
    k_taps = w.shape[0]
    pad = k_taps // 2
    s = x.shape[1]
    xp = jnp.pad(x, ((0, 0), (pad, pad), (0, 0)))
    y = xp[:, 0:s] * w[0]
    for i in range(1, k_taps):
        y = y + xp[:, i:i + s] * w[i]
    return y


def _chunk_gated_delta(q, k, v, beta, g):
    bn, s, h, dk = q.shape
    dv = v.shape[-1]
    n = s // CHUNK

    def to_chunks(t):
        t = t.reshape((bn, n, CHUNK, h) + t.shape[3:])
        return jnp.moveaxis(t, 3, 1)

    q, k, v, beta, g = (to_chunks(t) for t in (q, k, v, beta, g))
    g = jnp.cumsum(g, axis=-1)
    idx = jnp.arange(CHUNK)
    incl = idx[:, None] >= idx[None, :]
    strict = idx[:, None] > idx[None, :]
    decay = jnp.exp(jnp.where(incl, g[..., :, None] - g[..., None, :], -jnp.inf))
    k_beta = k * beta[..., None]
    a_mat = jnp.where(strict, jnp.einsum('bhnid,bhnjd->bhnij', k_beta, k) * decay, 0.0)
    a_mat = a_mat + jnp.eye(CHUNK, dtype=q.dtype)
    rhs = jnp.concatenate([v * beta[..., None], k_beta * jnp.exp(g)[..., None]], axis=-1)
    sol = lax.linalg.triangular_solve(a_mat, rhs, left_side=True, lower=True)
    u, w = sol[..., :dv], sol[..., dv:]
    attn = jnp.einsum('bhnid,bhnjd->bhnij', q, k) * decay

    xs = tuple(jnp.moveaxis(t, 2, 0) for t in (q, k, u, w, g, attn))

    def step(state, inp):
        q_c, k_c, u_c, w_c, g_c, attn_c = inp
        v_new = u_c - jnp.einsum('bhcd,bhde->bhce', w_c, state)
        o_c = (jnp.einsum('bhcd,bhde->bhce', q_c * jnp.exp(g_c)[..., None], state)
               + jnp.einsum('bhij,bhje->bhie', attn_c, v_new))
        g_last = g_c[..., -1:]
        state = (state * jnp.exp(g_last)[..., None]
                 + jnp.einsum('bhcd,bhce->bhde', k_c * jnp.exp(g_last - g_c)[..., None], v_new))
        return state, o_c

    state0 = jnp.zeros((bn, h, dk, dv), q.dtype)
    _, o = lax.scan(step, state0, xs)
    return o.transpose(1, 0, 3, 2, 4).reshape(bn, s, h, dv)


def _hybrid_mixer(u, w_in, conv_a, conv_dn, a_log_fwd, dt_bias_fwd, a_log_bwd, dt_bias_bwd,
                  dn_norm, w_a_out, w_b_out, w_out):
    bn, s, _ = u.shape
    f32 = jnp.float32
    proj = u @ w_in
    ca_b, ca_c, ca_v, dn_qkv, dn_z, dn_ba, gate_logits = jnp.split(proj, SPLIT_OFFSETS, axis=-1)

    y_a = ca_b * _dwconv(ca_c * ca_v, conv_a)

    qkv = jax.nn.silu(_dwconv(dn_qkv, conv_dn))
    q, k, v = jnp.split(qkv, [DN_QK, 2 * DN_QK], axis=-1)
    q = (_l2norm(q.reshape(bn, s, DN_HEADS, DN_DK).astype(f32)) * (DN_DK ** -0.5))
    k = _l2norm(k.reshape(bn, s, DN_HEADS, DN_DK).astype(f32))
    v = v.reshape(bn, s, DN_HEADS, DN_DV).astype(f32)
    b_f, b_b, al_f, al_b = jnp.split(dn_ba.astype(f32), 4, axis=-1)
    beta_f, beta_b = jax.nn.sigmoid(b_f), jax.nn.sigmoid(b_b)
    g_f = -jnp.exp(a_log_fwd.astype(f32)) * jax.nn.softplus(al_f + dt_bias_fwd.astype(f32))
    g_b = -jnp.exp(a_log_bwd.astype(f32)) * jax.nn.softplus(al_b + dt_bias_bwd.astype(f32))
    o_fwd = _chunk_gated_delta(q, k, v, beta_f, g_f)
    rev = lambda t: t[:, ::-1]
    o_bwd = rev(_chunk_gated_delta(rev(q), rev(k), rev(v), rev(beta_b), rev(g_b)))
    o = _rmsnorm(o_fwd + o_bwd, dn_norm) * jax.nn.silu(dn_z.reshape(bn, s, DN_HEADS, DN_DV).astype(f32))
    y_b = o.reshape(bn, s, DN_VW).astype(u.dtype)

    gate_a, gate_b = jnp.split(gate_logits, N_BRANCH, axis=-1)
    merged = jax.nn.sigmoid(gate_a) * (y_a @ w_a_out) + jax.nn.sigmoid(gate_b) * (y_b @ w_b_out)
    return merged @ w_out


def _fwd_setup_inputs(seed: int = 0) -> dict:
    key = jax.random.key(seed)
    ks = jax.random.split(key, 24)
    L = DEPTH
    f32 = jnp.float32

    def nrm(k, shape, fan_in):
        return jax.random.normal(k, shape, f32) * (fan_in ** -0.5)

    def gain(k, shape):
        return 1.0 + 0.02 * jax.random.normal(k, shape, f32)

    def a_log(k):
        return jnp.log(jax.random.uniform(k, (L, DN_HEADS), f32, minval=1.0, maxval=16.0))

    def dt_bias(k):
        dt = jnp.exp(jax.random.uniform(k, (L, DN_HEADS), f32,
                                        minval=math.log(1e-3), maxval=math.log(1e-1)))
        return dt + jnp.log(-jnp.expm1(-dt))

    return {
        "x": jax.random.normal(ks[0], (BATCH, SEQ, D_MODEL), f32),
        "c": jax.random.normal(ks[1], (BATCH, D_MODEL), f32),
        "w_ada": nrm(ks[2], (L, D_MODEL, N_MOD * D_MODEL), D_MODEL),
        "b_ada": 0.02 * jax.random.normal(ks[3], (L, N_MOD * D_MODEL), f32),
        "norm_ffn1": gain(ks[4], (L, D_MODEL)),
        "w_ffn1_up": nrm(ks[5], (L, D_MODEL, 2 * D_FF), D_MODEL),
        "w_ffn1_down": nrm(ks[6], (L, D_FF, D_MODEL), D_FF),
        "norm_mix": gain(ks[7], (L, D_MODEL)),
        "w_in": nrm(ks[8], (L, D_MODEL, IN_COLS), D_MODEL),
        "conv_a": nrm(ks[9], (L, CONV_A_K, CONV_A_WIDTH), CONV_A_K),
        "conv_dn": nrm(ks[10], (L, DN_CONV_K, 2 * DN_QK + DN_VW), DN_CONV_K),
        "a_log_fwd": a_log(ks[11]),
        "dt_bias_fwd": dt_bias(ks[12]),
        "a_log_bwd": a_log(ks[13]),
        "dt_bias_bwd": dt_bias(ks[14]),
        "dn_norm": gain(ks[15], (L, DN_DV)),
        "w_a_out": nrm(ks[16], (L, CONV_A_WIDTH, D_MODEL), CONV_A_WIDTH),
        "w_b_out": nrm(ks[17], (L, DN_VW, D_MODEL), DN_VW),
        "w_out": nrm(ks[18], (L, D_MODEL, D_MODEL), D_MODEL),
        "norm_ffn2": gain(ks[19], (L, D_MODEL)),
        "w_ffn2_up": nrm(ks[20], (L, D_MODEL, 2 * D_FF), D_MODEL),
        "w_ffn2_down": nrm(ks[21], (L, D_FF, D_MODEL), D_FF),
        "norm_final": gain(ks[22], (D_MODEL,)),
    }


def _fwd_reference(x, c, w_ada, b_ada, norm_ffn1, w_ffn1_up, w_ffn1_down, norm_mix, w_in, conv_a,
              conv_dn, a_log_fwd, dt_bias_fwd, a_log_bwd, dt_bias_bwd, dn_norm, w_a_out, w_b_out,
              w_out, norm_ffn2, w_ffn2_up, w_ffn2_down, norm_final):
    h = x
    c_act = jax.nn.silu(c)
    for l in range(DEPTH):
        mod = c_act @ w_ada[l] + b_ada[l]
        sh1, sc1, g1, sh2, sc2, g2, sh3, sc3, g3 = [m[:, None, :] for m in jnp.split(mod, N_MOD, axis=-1)]
        u = _modulate(_rmsnorm(h, norm_ffn1[l]), sh1, sc1)
        h = h + 0.5 * g1 * _swiglu(u, w_ffn1_up[l], w_ffn1_down[l])
        u = _modulate(_rmsnorm(h, norm_mix[l]), sh2, sc2)
        h = h + g2 * _hybrid_mixer(u, w_in[l], conv_a[l], conv_dn[l], a_log_fwd[l], dt_bias_fwd[l],
                                   a_log_bwd[l], dt_bias_bwd[l], dn_norm[l], w_a_out[l],
                                   w_b_out[l], w_out[l])
        u = _modulate(_rmsnorm(h, norm_ffn2[l]), sh3, sc3)
        h = h + 0.5 * g3 * _swiglu(u, w_ffn2_up[l], w_ffn2_down[l])
    return _rmsnorm(h, norm_final)


import jax as _jax
import jax.numpy as _jnp

TWIN_FORMAT = 'train_step'
FWD_PARAMS = ['x', 'c', 'w_ada', 'b_ada', 'norm_ffn1', 'w_ffn1_up', 'w_ffn1_down', 'norm_mix', 'w_in', 'conv_a', 'conv_dn', 'a_log_fwd', 'dt_bias_fwd', 'a_log_bwd', 'dt_bias_bwd', 'dn_norm', 'w_a_out', 'w_b_out', 'w_out', 'norm_ffn2', 'w_ffn2_up', 'w_ffn2_down', 'norm_final']
TWIN_WEIGHTS = ['w_ada', 'b_ada', 'norm_ffn1', 'w_ffn1_up', 'w_ffn1_down', 'norm_mix', 'w_in', 'conv_a', 'conv_dn', 'a_log_fwd', 'dt_bias_fwd', 'a_log_bwd', 'dt_bias_bwd', 'dn_norm', 'w_a_out', 'w_b_out', 'w_out', 'norm_ffn2', 'w_ffn2_up', 'w_ffn2_down', 'norm_final']
TWIN_DIFF_INPUT = 'x'
TWIN_INPUTS = ['x', 'c', 'w_ada', 'b_ada', 'norm_ffn1', 'w_ffn1_up', 'w_ffn1_down', 'norm_mix', 'w_in', 'conv_a', 'conv_dn', 'a_log_fwd', 'dt_bias_fwd', 'a_log_bwd', 'dt_bias_bwd', 'dn_norm', 'w_a_out', 'w_b_out', 'w_out', 'norm_ffn2', 'w_ffn2_up', 'w_ffn2_down', 'norm_final', 'loss_target', 'm_w_ada', 'm_b_ada', 'm_norm_ffn1', 'm_w_ffn1_up', 'm_w_ffn1_down', 'm_norm_mix', 'm_w_in', 'm_conv_a', 'm_conv_dn', 'm_a_log_fwd', 'm_dt_bias_fwd', 'm_a_log_bwd', 'm_dt_bias_bwd', 'm_dn_norm', 'm_w_a_out', 'm_w_b_out', 'm_w_out', 'm_norm_ffn2', 'm_w_ffn2_up', 'm_w_ffn2_down', 'm_norm_final', 'v_w_ada', 'v_b_ada', 'v_norm_ffn1', 'v_w_ffn1_up', 'v_w_ffn1_down', 'v_norm_mix', 'v_w_in', 'v_conv_a', 'v_conv_dn', 'v_a_log_fwd', 'v_dt_bias_fwd', 'v_a_log_bwd', 'v_dt_bias_bwd', 'v_dn_norm', 'v_w_a_out', 'v_w_b_out', 'v_w_out', 'v_norm_ffn2', 'v_w_ffn2_up', 'v_w_ffn2_down', 'v_norm_final']
TWIN_OUTPUTS = ['loss', 'grad_x', 'grad_w_ada', 'grad_b_ada', 'grad_norm_ffn1', 'grad_w_ffn1_up', 'grad_w_ffn1_down', 'grad_norm_mix', 'grad_w_in', 'grad_conv_a', 'grad_conv_dn', 'grad_a_log_fwd', 'grad_dt_bias_fwd', 'grad_a_log_bwd', 'grad_dt_bias_bwd', 'grad_dn_norm', 'grad_w_a_out', 'grad_w_b_out', 'grad_w_out', 'grad_norm_ffn2', 'grad_w_ffn2_up', 'grad_w_ffn2_down', 'grad_norm_final', 'delta_w_ada', 'delta_b_ada', 'delta_norm_ffn1', 'delta_w_ffn1_up', 'delta_w_ffn1_down', 'delta_norm_mix', 'delta_w_in', 'delta_conv_a', 'delta_conv_dn', 'delta_a_log_fwd', 'delta_dt_bias_fwd', 'delta_a_log_bwd', 'delta_dt_bias_bwd', 'delta_dn_norm', 'delta_w_a_out', 'delta_w_b_out', 'delta_w_out', 'delta_norm_ffn2', 'delta_w_ffn2_up', 'delta_w_ffn2_down', 'delta_norm_final', 'new_m_w_ada', 'new_m_b_ada', 'new_m_norm_ffn1', 'new_m_w_ffn1_up', 'new_m_w_ffn1_down', 'new_m_norm_mix', 'new_m_w_in', 'new_m_conv_a', 'new_m_conv_dn', 'new_m_a_log_fwd', 'new_m_dt_bias_fwd', 'new_m_a_log_bwd', 'new_m_dt_bias_bwd', 'new_m_dn_norm', 'new_m_w_a_out', 'new_m_w_b_out', 'new_m_w_out', 'new_m_norm_ffn2', 'new_m_w_ffn2_up', 'new_m_w_ffn2_down', 'new_m_norm_final', 'new_v_w_ada', 'new_v_b_ada', 'new_v_norm_ffn1', 'new_v_w_ffn1_up', 'new_v_w_ffn1_down', 'new_v_norm_mix', 'new_v_w_in', 'new_v_conv_a', 'new_v_conv_dn', 'new_v_a_log_fwd', 'new_v_dt_bias_fwd', 'new_v_a_log_bwd', 'new_v_dt_bias_bwd', 'new_v_dn_norm', 'new_v_w_a_out', 'new_v_w_b_out', 'new_v_w_out', 'new_v_norm_ffn2', 'new_v_w_ffn2_up', 'new_v_w_ffn2_down', 'new_v_norm_final']
TWIN_LEAF_KINDS = {'loss': 'loss', 'grad_x': 'grad_x', 'grad_w_ada': 'grad_w', 'grad_b_ada': 'grad_w', 'grad_norm_ffn1': 'grad_w', 'grad_w_ffn1_up': 'grad_w', 'grad_w_ffn1_down': 'grad_w', 'grad_norm_mix': 'grad_w', 'grad_w_in': 'grad_w', 'grad_conv_a': 'grad_w', 'grad_conv_dn': 'grad_w', 'grad_a_log_fwd': 'grad_w', 'grad_dt_bias_fwd': 'grad_w', 'grad_a_log_bwd': 'grad_w', 'grad_dt_bias_bwd': 'grad_w', 'grad_dn_norm': 'grad_w', 'grad_w_a_out': 'grad_w', 'grad_w_b_out': 'grad_w', 'grad_w_out': 'grad_w', 'grad_norm_ffn2': 'grad_w', 'grad_w_ffn2_up': 'grad_w', 'grad_w_ffn2_down': 'grad_w', 'grad_norm_final': 'grad_w', 'delta_w_ada': 'delta_w', 'delta_b_ada': 'delta_w', 'delta_norm_ffn1': 'delta_w', 'delta_w_ffn1_up': 'delta_w', 'delta_w_ffn1_down': 'delta_w', 'delta_norm_mix': 'delta_w', 'delta_w_in': 'delta_w', 'delta_conv_a': 'delta_w', 'delta_conv_dn': 'delta_w', 'delta_a_log_fwd': 'delta_w', 'delta_dt_bias_fwd': 'delta_w', 'delta_a_log_bwd': 'delta_w', 'delta_dt_bias_bwd': 'delta_w', 'delta_dn_norm': 'delta_w', 'delta_w_a_out': 'delta_w', 'delta_w_b_out': 'delta_w', 'delta_w_out': 'delta_w', 'delta_norm_ffn2': 'delta_w', 'delta_w_ffn2_up': 'delta_w', 'delta_w_ffn2_down': 'delta_w', 'delta_norm_final': 'delta_w', 'new_m_w_ada': 'new_m', 'new_m_b_ada': 'new_m', 'new_m_norm_ffn1': 'new_m', 'new_m_w_ffn1_up': 'new_m', 'new_m_w_ffn1_down': 'new_m', 'new_m_norm_mix': 'new_m', 'new_m_w_in': 'new_m', 'new_m_conv_a': 'new_m', 'new_m_conv_dn': 'new_m', 'new_m_a_log_fwd': 'new_m', 'new_m_dt_bias_fwd': 'new_m', 'new_m_a_log_bwd': 'new_m', 'new_m_dt_bias_bwd': 'new_m', 'new_m_dn_norm': 'new_m', 'new_m_w_a_out': 'new_m', 'new_m_w_b_out': 'new_m', 'new_m_w_out': 'new_m', 'new_m_norm_ffn2': 'new_m', 'new_m_w_ffn2_up': 'new_m', 'new_m_w_ffn2_down': 'new_m', 'new_m_norm_final': 'new_m', 'new_v_w_ada': 'new_v', 'new_v_b_ada': 'new_v', 'new_v_norm_ffn1': 'new_v', 'new_v_w_ffn1_up': 'new_v', 'new_v_w_ffn1_down': 'new_v', 'new_v_norm_mix': 'new_v', 'new_v_w_in': 'new_v', 'new_v_conv_a': 'new_v', 'new_v_conv_dn': 'new_v', 'new_v_a_log_fwd': 'new_v', 'new_v_dt_bias_fwd': 'new_v', 'new_v_a_log_bwd': 'new_v', 'new_v_dt_bias_bwd': 'new_v', 'new_v_dn_norm': 'new_v', 'new_v_w_a_out': 'new_v', 'new_v_w_b_out': 'new_v', 'new_v_w_out': 'new_v', 'new_v_norm_ffn2': 'new_v', 'new_v_w_ffn2_up': 'new_v', 'new_v_w_ffn2_down': 'new_v', 'new_v_norm_final': 'new_v'}


def _forward(args):
    return _fwd_reference(*[args[k] for k in FWD_PARAMS])


def _output_shape():
    def fwd():
        inp = _fwd_setup_inputs(0)
        return _fwd_reference(*[inp[k] for k in FWD_PARAMS])
    out = _jax.eval_shape(fwd)
    return out.shape, out.dtype

N_MICROBATCH = 1
ADAM_LR = 0.001
ADAM_B1 = 0.9
ADAM_B2 = 0.999
ADAM_EPS = 1e-08
ADAM_WD = 0.01
ADAM_STEP = 10
PER_EXAMPLE_BATCH_AXIS = {'x': 0, 'c': 0, 'loss_target': 0}
SHARED_INPUTS = []
_WEIGHT_DTYPES = {'w_ada': _jnp.float32, 'b_ada': _jnp.float32, 'norm_ffn1': _jnp.float32, 'w_ffn1_up': _jnp.float32, 'w_ffn1_down': _jnp.float32, 'norm_mix': _jnp.float32, 'w_in': _jnp.float32, 'conv_a': _jnp.float32, 'conv_dn': _jnp.float32, 'a_log_fwd': _jnp.float32, 'dt_bias_fwd': _jnp.float32, 'a_log_bwd': _jnp.float32, 'dt_bias_bwd': _jnp.float32, 'dn_norm': _jnp.float32, 'w_a_out': _jnp.float32, 'w_b_out': _jnp.float32, 'w_out': _jnp.float32, 'norm_ffn2': _jnp.float32, 'w_ffn2_up': _jnp.float32, 'w_ffn2_down': _jnp.float32, 'norm_final': _jnp.float32}
MOMENT_SCALE = {'w_ada': 1.169723e-01, 'b_ada': 2.142657e-01, 'norm_ffn1': 1.379392e-01, 'w_ffn1_up': 5.987660e-02, 'w_ffn1_down': 9.732249e-02, 'norm_mix': 2.863785e-01, 'w_in': 1.044469e-01, 'conv_a': 2.032358e-01, 'conv_dn': 3.507945e-02, 'a_log_fwd': 1.001098e-01, 'dt_bias_fwd': 9.826536e-02, 'a_log_bwd': 8.890858e-02, 'dt_bias_bwd': 8.714415e-02, 'dn_norm': 1.528880e-01, 'w_a_out': 1.532107e-01, 'w_b_out': 5.328551e-02, 'w_out': 1.629713e-01, 'norm_ffn2': 7.614220e-02, 'w_ffn2_up': 3.606321e-02, 'w_ffn2_down': 5.891253e-02, 'norm_final': 6.526518e+01}


def _to_microbatches(a, axis):
    t = _jnp.moveaxis(a, axis, 0)
    t = t.reshape((N_MICROBATCH, t.shape[0] // N_MICROBATCH) + t.shape[1:])
    return _jnp.moveaxis(t, 1, axis + 1)


def setup_inputs(seed: int = 0) -> dict:
    inp = _fwd_setup_inputs(seed)
    key = _jax.random.fold_in(_jax.random.key(seed), 7919)
    shape, _ = _output_shape()
    out = dict(inp)
    out["loss_target"] = _jax.random.normal(_jax.random.fold_in(key, 0), shape, _jnp.float32)
    for i, name in enumerate(TWIN_WEIGHTS):
        w = inp[name].astype(_jnp.float32)
        if MOMENT_SCALE is None:
            s = _jnp.sqrt(_jnp.mean(_jnp.square(w)) + 1e-30)
        else:
            s = MOMENT_SCALE[name]
        km, kv = _jax.random.split(_jax.random.fold_in(key, i + 1))
        out[name] = w
        out["m_" + name] = s * _jax.random.normal(km, w.shape, _jnp.float32)
        out["v_" + name] = (s * s) * _jax.random.uniform(kv, w.shape, _jnp.float32, 0.5, 1.5)
    if N_MICROBATCH > 1:
        for name, axis in PER_EXAMPLE_BATCH_AXIS.items():
            out[name] = _to_microbatches(out[name], axis)
    return {'x': out['x'], 'c': out['c'], 'w_ada': out['w_ada'], 'b_ada': out['b_ada'], 'norm_ffn1': out['norm_ffn1'], 'w_ffn1_up': out['w_ffn1_up'], 'w_ffn1_down': out['w_ffn1_down'], 'norm_mix': out['norm_mix'], 'w_in': out['w_in'], 'conv_a': out['conv_a'], 'conv_dn': out['conv_dn'], 'a_log_fwd': out['a_log_fwd'], 'dt_bias_fwd': out['dt_bias_fwd'], 'a_log_bwd': out['a_log_bwd'], 'dt_bias_bwd': out['dt_bias_bwd'], 'dn_norm': out['dn_norm'], 'w_a_out': out['w_a_out'], 'w_b_out': out['w_b_out'], 'w_out': out['w_out'], 'norm_ffn2': out['norm_ffn2'], 'w_ffn2_up': out['w_ffn2_up'], 'w_ffn2_down': out['w_ffn2_down'], 'norm_final': out['norm_final'], 'loss_target': out['loss_target'], 'm_w_ada': out['m_w_ada'], 'm_b_ada': out['m_b_ada'], 'm_norm_ffn1': out['m_norm_ffn1'], 'm_w_ffn1_up': out['m_w_ffn1_up'], 'm_w_ffn1_down': out['m_w_ffn1_down'], 'm_norm_mix': out['m_norm_mix'], 'm_w_in': out['m_w_in'], 'm_conv_a': out['m_conv_a'], 'm_conv_dn': out['m_conv_dn'], 'm_a_log_fwd': out['m_a_log_fwd'], 'm_dt_bias_fwd': out['m_dt_bias_fwd'], 'm_a_log_bwd': out['m_a_log_bwd'], 'm_dt_bias_bwd': out['m_dt_bias_bwd'], 'm_dn_norm': out['m_dn_norm'], 'm_w_a_out': out['m_w_a_out'], 'm_w_b_out': out['m_w_b_out'], 'm_w_out': out['m_w_out'], 'm_norm_ffn2': out['m_norm_ffn2'], 'm_w_ffn2_up': out['m_w_ffn2_up'], 'm_w_ffn2_down': out['m_w_ffn2_down'], 'm_norm_final': out['m_norm_final'], 'v_w_ada': out['v_w_ada'], 'v_b_ada': out['v_b_ada'], 'v_norm_ffn1': out['v_norm_ffn1'], 'v_w_ffn1_up': out['v_w_ffn1_up'], 'v_w_ffn1_down': out['v_w_ffn1_down'], 'v_norm_mix': out['v_norm_mix'], 'v_w_in': out['v_w_in'], 'v_conv_a': out['v_conv_a'], 'v_conv_dn': out['v_conv_dn'], 'v_a_log_fwd': out['v_a_log_fwd'], 'v_dt_bias_fwd': out['v_dt_bias_fwd'], 'v_a_log_bwd': out['v_a_log_bwd'], 'v_dt_bias_bwd': out['v_dt_bias_bwd'], 'v_dn_norm': out['v_dn_norm'], 'v_w_a_out': out['v_w_a_out'], 'v_w_b_out': out['v_w_b_out'], 'v_w_out': out['v_w_out'], 'v_norm_ffn2': out['v_norm_ffn2'], 'v_w_ffn2_up': out['v_w_ffn2_up'], 'v_w_ffn2_down': out['v_w_ffn2_down'], 'v_norm_final': out['v_norm_final']}


def _loss(weights, diff, rest, loss_target):
    with _jax.named_scope("forward"):
        args = {**rest, TWIN_DIFF_INPUT: diff, **{k: w.astype(_WEIGHT_DTYPES[k]) for k, w in weights.items()}}
        y = _forward(args)
    with _jax.named_scope("loss_head"):
        err = _jnp.square(y.astype(_jnp.float32) - loss_target)
        return 0.5 * _jnp.sum(_jnp.mean(err, axis=-1)) if err.ndim else 0.5 * err


def _adamw(w, g, m, v):
    m = ADAM_B1 * m + (1.0 - ADAM_B1) * g
    v = ADAM_B2 * v + (1.0 - ADAM_B2) * _jnp.square(g)
    m_hat = m / (1.0 - ADAM_B1 ** ADAM_STEP)
    v_hat = v / (1.0 - ADAM_B2 ** ADAM_STEP)
    delta = -ADAM_LR * (m_hat / (_jnp.sqrt(v_hat) + ADAM_EPS) + ADAM_WD * w)
    return delta, m, v


def reference(x, c, w_ada, b_ada, norm_ffn1, w_ffn1_up, w_ffn1_down, norm_mix, w_in, conv_a, conv_dn, a_log_fwd, dt_bias_fwd, a_log_bwd, dt_bias_bwd, dn_norm, w_a_out, w_b_out, w_out, norm_ffn2, w_ffn2_up, w_ffn2_down, norm_final, loss_target, m_w_ada, m_b_ada, m_norm_ffn1, m_w_ffn1_up, m_w_ffn1_down, m_norm_mix, m_w_in, m_conv_a, m_conv_dn, m_a_log_fwd, m_dt_bias_fwd, m_a_log_bwd, m_dt_bias_bwd, m_dn_norm, m_w_a_out, m_w_b_out, m_w_out, m_norm_ffn2, m_w_ffn2_up, m_w_ffn2_down, m_norm_final, v_w_ada, v_b_ada, v_norm_ffn1, v_w_ffn1_up, v_w_ffn1_down, v_norm_mix, v_w_in, v_conv_a, v_conv_dn, v_a_log_fwd, v_dt_bias_fwd, v_a_log_bwd, v_dt_bias_bwd, v_dn_norm, v_w_a_out, v_w_b_out, v_w_out, v_norm_ffn2, v_w_ffn2_up, v_w_ffn2_down, v_norm_final):
    given = dict(x=x, c=c, w_ada=w_ada, b_ada=b_ada, norm_ffn1=norm_ffn1, w_ffn1_up=w_ffn1_up, w_ffn1_down=w_ffn1_down, norm_mix=norm_mix, w_in=w_in, conv_a=conv_a, conv_dn=conv_dn, a_log_fwd=a_log_fwd, dt_bias_fwd=dt_bias_fwd, a_log_bwd=a_log_bwd, dt_bias_bwd=dt_bias_bwd, dn_norm=dn_norm, w_a_out=w_a_out, w_b_out=w_b_out, w_out=w_out, norm_ffn2=norm_ffn2, w_ffn2_up=w_ffn2_up, w_ffn2_down=w_ffn2_down, norm_final=norm_final, loss_target=loss_target, m_w_ada=m_w_ada, m_b_ada=m_b_ada, m_norm_ffn1=m_norm_ffn1, m_w_ffn1_up=m_w_ffn1_up, m_w_ffn1_down=m_w_ffn1_down, m_norm_mix=m_norm_mix, m_w_in=m_w_in, m_conv_a=m_conv_a, m_conv_dn=m_conv_dn, m_a_log_fwd=m_a_log_fwd, m_dt_bias_fwd=m_dt_bias_fwd, m_a_log_bwd=m_a_log_bwd, m_dt_bias_bwd=m_dt_bias_bwd, m_dn_norm=m_dn_norm, m_w_a_out=m_w_a_out, m_w_b_out=m_w_b_out, m_w_out=m_w_out, m_norm_ffn2=m_norm_ffn2, m_w_ffn2_up=m_w_ffn2_up, m_w_ffn2_down=m_w_ffn2_down, m_norm_final=m_norm_final, v_w_ada=v_w_ada, v_b_ada=v_b_ada, v_norm_ffn1=v_norm_ffn1, v_w_ffn1_up=v_w_ffn1_up, v_w_ffn1_down=v_w_ffn1_down, v_norm_mix=v_norm_mix, v_w_in=v_w_in, v_conv_a=v_conv_a, v_conv_dn=v_conv_dn, v_a_log_fwd=v_a_log_fwd, v_dt_bias_fwd=v_dt_bias_fwd, v_a_log_bwd=v_a_log_bwd, v_dt_bias_bwd=v_dt_bias_bwd, v_dn_norm=v_dn_norm, v_w_a_out=v_w_a_out, v_w_b_out=v_w_b_out, v_w_out=v_w_out, v_norm_ffn2=v_norm_ffn2, v_w_ffn2_up=v_w_ffn2_up, v_w_ffn2_down=v_w_ffn2_down, v_norm_final=v_norm_final)
    weights = {n: given[n] for n in TWIN_WEIGHTS}
    shared = {n: given[n] for n in SHARED_INPUTS}
    per_example = {n: given[n] for n in ['x', 'c']}
    grad_fn = _jax.value_and_grad(_loss, argnums=(0, 1))

    def one_microbatch(ex, loss_target):
        ex = dict(ex)
        diff = ex.pop(TWIN_DIFF_INPUT)
        return grad_fn(weights, diff, {**shared, **ex}, loss_target)

    if N_MICROBATCH == 1:
        loss, (grad_w, grad_x) = one_microbatch(per_example, given["loss_target"])
    else:
        def body(carry, xs):
            loss_sum, grad_sum = carry
            l_k, (gw_k, gx_k) = one_microbatch(xs[0], xs[1])
            with _jax.named_scope("update"):
                return (loss_sum + l_k, _jax.tree.map(_jnp.add, grad_sum, gw_k)), gx_k

        init = (_jnp.zeros((), _jnp.float32), _jax.tree.map(_jnp.zeros_like, weights))
        (loss, grad_w), grad_x = _jax.lax.scan(body, init, (per_example, given["loss_target"]))
    with _jax.named_scope("update"):
        delta_w, new_m, new_v = {}, {}, {}
        for n in TWIN_WEIGHTS:
            delta_w[n], new_m[n], new_v[n] = _adamw(weights[n], grad_w[n], given["m_" + n], given["v_" + n])
    return (loss, grad_x, *[grad_w[n] for n in TWIN_WEIGHTS], *[delta_w[n] for n in TWIN_WEIGHTS],
            *[new_m[n] for n in TWIN_WEIGHTS], *[new_v[n] for n in TWIN_WEIGHTS])
```

```python
import functools

import jax
import jax.numpy as jnp
from jax import lax
from jax.experimental import pallas as pl
from jax.experimental.pallas import tpu as pltpu

F32 = jnp.float32
BF16 = jnp.bfloat16
SDS = jax.ShapeDtypeStruct
MESH = pl.DeviceIdType.MESH
HI = lax.Precision.HIGHEST

EPS = 1e-6
HEAD = 128
CHUNK = 64
LANES = 128
N_CHIPS = 4
N_DEV = 8
VMEM_LIMIT = 56 * 1024 * 1024

ADAM_LR = 0.001
ADAM_B1 = 0.9
ADAM_B2 = 0.999
ADAM_EPS = 1e-08
ADAM_WD = 0.01
ADAM_STEP = 10


def _params(*sem):
    return pltpu.CompilerParams(dimension_semantics=sem, vmem_limit_bytes=VMEM_LIMIT)


def _tile(n, cap, mult=LANES):
    t = min(n, cap) // mult * mult
    while t >= mult:
        if n % t == 0:
            return t
        t -= mult
    return n


def _row(tr, w, cb=0):
    return pl.BlockSpec((tr, w), lambda i: (i, cb))


def _vec(r, w):
    return pl.BlockSpec((r, w), lambda i: (0, 0))


def _nn(a, b, **kw):
    return jnp.dot(a, b, preferred_element_type=F32, **kw)


def _nt(a, b, **kw):
    return lax.dot_general(a, b, (((1,), (1,)), ((), ())), preferred_element_type=F32, **kw)


def _tn(a, b, **kw):
    return lax.dot_general(a, b, (((0,), (0,)), ((), ())), preferred_element_type=F32, **kw)


def _silu_grad(x):
    s = jax.nn.sigmoid(x)
    return s * (1.0 + x * (1.0 - s))


def _matmul(name, a, b, mode, out_dtype=F32, tm=512, tn=1024, tk=512):
    if mode == "nn":
        (m, k), (_, n) = a.shape, b.shape
    elif mode == "nt":
        (m, k), (n, _) = a.shape, b.shape
    else:
        (k, m), (_, n) = a.shape, b.shape
    tm, tn, tk = _tile(m, tm), _tile(n, tn), _tile(k, tk)
    nk = k // tk
    a_spec = {"nn": pl.BlockSpec((tm, tk), lambda i, j, l: (i, l)),
              "nt": pl.BlockSpec((tm, tk), lambda i, j, l: (i, l)),
              "tn": pl.BlockSpec((tk, tm), lambda i, j, l: (l, i))}[mode]
    b_spec = {"nn": pl.BlockSpec((tk, tn), lambda i, j, l: (l, j)),
              "nt": pl.BlockSpec((tn, tk), lambda i, j, l: (j, l)),
              "tn": pl.BlockSpec((tk, tn), lambda i, j, l: (l, j))}[mode]
    dot = {"nn": _nn, "nt": _nt, "tn": _tn}[mode]

    def body(a_ref, b_ref, o_ref, acc):
        l = pl.program_id(2)

        @pl.when(l == 0)
        def _():
            acc[...] = jnp.zeros_like(acc)

        acc[...] += dot(a_ref[...].astype(BF16), b_ref[...].astype(BF16))

        @pl.when(l == nk - 1)
        def _():
            o_ref[...] = acc[...].astype(o_ref.dtype)

    return pl.pallas_call(
        body, grid=(m // tm, n // tn, nk), in_specs=[a_spec, b_spec],
        out_specs=pl.BlockSpec((tm, tn), lambda i, j, l: (i, j)),
        out_shape=SDS((m, n), out_dtype), scratch_shapes=[pltpu.VMEM((tm, tn), F32)],
        compiler_params=_params("parallel", "parallel", "arbitrary"), name=name)(a, b)


def _norm_mod(name, h, nw, sh, sc, tr=512):
    s, d = h.shape
    tr = _tile(s, tr, 8)

    def body(h_ref, nw_ref, sh_ref, sc_ref, u_ref):
        x = h_ref[...]
        r = lax.rsqrt(jnp.mean(x * x, axis=-1, keepdims=True) + EPS)
        u_ref[...] = (x * r * nw_ref[...] * (1.0 + sc_ref[...]) + sh_ref[...]).astype(BF16)

    return pl.pallas_call(
        body, grid=(s // tr,), in_specs=[_row(tr, d), _vec(1, d), _vec(1, d), _vec(1, d)],
        out_specs=_row(tr, d), out_shape=SDS((s, d), BF16),
        compiler_params=_params("parallel"), name=name)(h, nw, sh, sc)


def _norm_mod_bwd(name, h, du, dh, nw, sc, tr=512):
    s, d = h.shape
    tr = _tile(s, tr, 8)

    def body(h_ref, du_ref, dh_ref, nw_ref, sc_ref, o_ref, acc_ref):
        @pl.when(pl.program_id(0) == 0)
        def _():
            acc_ref[...] = jnp.zeros_like(acc_ref)

        x, g = h_ref[...], du_ref[...]
        r = lax.rsqrt(jnp.mean(x * x, axis=-1, keepdims=True) + EPS)
        n = x * r
        nw, sc1 = nw_ref[...], 1.0 + sc_ref[...]
        dn = g * sc1 * nw
        o_ref[...] = dh_ref[...] + r * (dn - n * jnp.mean(dn * n, axis=-1, keepdims=True))
        gn = g * n
        acc_ref[0:1, :] += jnp.sum(g, axis=0, keepdims=True)
        acc_ref[1:2, :] += jnp.sum(gn * nw, axis=0, keepdims=True)
        acc_ref[2:3, :] += jnp.sum(gn * sc1, axis=0, keepdims=True)

    return pl.pallas_call(
        body, grid=(s // tr,),
        in_specs=[_row(tr, d), _row(tr, d), _row(tr, d), _vec(1, d), _vec(1, d)],
        out_specs=[_row(tr, d), _vec(8, d)], out_shape=[SDS((s, d), F32), SDS((8, d), F32)],
        compiler_params=_params("arbitrary"), name=name)(h, du, dh, nw, sc)


def _swiglu(name, ab, tr=256):
    s, f2 = ab.shape
    f = f2 // 2
    tr = _tile(s, tr, 8)

    def body(a_ref, b_ref, o_ref):
        o_ref[...] = (jax.nn.silu(a_ref[...]) * b_ref[...]).astype(BF16)

    return pl.pallas_call(
        body, grid=(s // tr,), in_specs=[_row(tr, f, 0), _row(tr, f, 1)], out_specs=_row(tr, f),
        out_shape=SDS((s, f), BF16), compiler_params=_params("parallel"), name=name)(ab, ab)


def _swiglu_bwd(name, ab, dhm, tr=256):
    s, f2 = ab.shape
    f = f2 // 2
    tr = _tile(s, tr, 8)

    def body(a_ref, b_ref, d_ref, o_ref):
        a, d = a_ref[...], d_ref[...]
        o_ref[:, 0:f] = (d * b_ref[...] * _silu_grad(a)).astype(BF16)
        o_ref[:, f:f2] = (d * jax.nn.silu(a)).astype(BF16)

    return pl.pallas_call(
        body, grid=(s // tr,), in_specs=[_row(tr, f, 0), _row(tr, f, 1), _row(tr, f)],
        out_specs=_row(tr, f2), out_shape=SDS((s, f2), BF16),
        compiler_params=_params("parallel"), name=name)(ab, ab, dhm)


def _resid(name, h, f, g, scale, tr=512):
    s, d = h.shape
    tr = _tile(s, tr, 8)

    def body(h_ref, f_ref, g_ref, o_ref):
        o_ref[...] = h_ref[...] + (scale * g_ref[...]) * f_ref[...]

    return pl.pallas_call(
        body, grid=(s // tr,), in_specs=[_row(tr, d), _row(tr, d), _vec(1, d)], out_specs=_row(tr, d),
        out_shape=SDS((s, d), F32), compiler_params=_params("parallel"), name=name)(h, f, g)


def _resid_bwd(name, dh, f, g, scale, tr=512):
    s, d = dh.shape
    tr = _tile(s, tr, 8)

    def body(dh_ref, f_ref, g_ref, o_ref, acc_ref):
        @pl.when(pl.program_id(0) == 0)
        def _():
            acc_ref[...] = jnp.zeros_like(acc_ref)

        x = dh_ref[...]
        o_ref[...] = ((scale * g_ref[...]) * x).astype(BF16)
        acc_ref[0:1, :] += jnp.sum(scale * x * f_ref[...], axis=0, keepdims=True)

    return pl.pallas_call(
        body, grid=(s // tr,), in_specs=[_row(tr, d), _row(tr, d), _vec(1, d)],
        out_specs=[_row(tr, d), _vec(8, d)], out_shape=[SDS((s, d), BF16), SDS((8, d), F32)],
        compiler_params=_params("arbitrary"), name=name)(dh, f, g)


def _final_loss(name, h, tgt, nw, tr=512):
    s, d = h.shape
    tr = _tile(s, tr, 8)

    def body(h_ref, t_ref, nw_ref, o_ref, acc_ref):
        @pl.when(pl.program_id(0) == 0)
        def _():
            acc_ref[...] = jnp.zeros_like(acc_ref)

        x, nw = h_ref[...], nw_ref[...]
        r = lax.rsqrt(jnp.mean(x * x, axis=-1, keepdims=True) + EPS)
        n = x * r
        diff = n * nw - t_ref[...]
        dy = diff * (1.0 / d)
        dn = dy * nw
        o_ref[...] = r * (dn - n * jnp.mean(dn * n, axis=-1, keepdims=True))
        acc_ref[0:1, :] += jnp.sum(dy * n, axis=0, keepdims=True)
        acc_ref[1:2, :] += jnp.sum(diff * diff, axis=0, keepdims=True) * (0.5 / d)

    return pl.pallas_call(
        body, grid=(s // tr,), in_specs=[_row(tr, d), _row(tr, d), _vec(1, d)],
        out_specs=[_row(tr, d), _vec(8, d)], out_shape=[SDS((s, d), F32), SDS((8, d), F32)],
        compiler_params=_params("arbitrary"), name=name)(h, tgt, nw)


class _Layout:
    def __init__(self, d, ca, nh):
        self.d, self.ca, self.nh = d, ca, nh
        self.qk = nh * HEAD
        self.qkv = 3 * self.qk
        self.z = self.qkv
        self.ga = self.z + self.qk
        self.cab = self.ga + 2 * d
        self.ba = self.cab + 3 * ca
        self.tail = _tile(self.ba, 512)
        self.total = self.ba + self.tail
        assert self.qkv % self.qk == 0 and self.ga % (2 * d) == 0 and self.cab % (3 * ca) == 0
        assert self.ba % self.tail == 0 and 4 * nh <= LANES

    def perm_cols(self, w):
        ca, qkv, qk, d, nh = self.ca, self.qkv, self.qk, self.d, self.nh
        o = [0, ca, 2 * ca, 3 * ca, 3 * ca + qkv, 3 * ca + qkv + qk, 3 * ca + qkv + qk + 4 * nh]
        cb, cc, cv = (w[..., o[i]:o[i + 1]] for i in range(3))
        x_qkv, x_z, x_ba = w[..., o[3]:o[4]], w[..., o[4]:o[5]], w[..., o[5]:o[6]]
        gates = w[..., o[6]:o[6] + 2 * d]
        pad = jnp.zeros(w.shape[:-1] + (self.tail - 4 * nh,), w.dtype)
        return jnp.concatenate([x_qkv, x_z, gates, cb, cc, cv, x_ba, pad], axis=-1)

    def unperm_cols(self, w):
        ca, nh = self.ca, self.nh
        cb, cc, cv = (w[..., self.cab + i * ca:self.cab + (i + 1) * ca] for i in range(3))
        return jnp.concatenate([cb, cc, cv, w[..., 0:self.qkv], w[..., self.z:self.ga],
                                w[..., self.ba:self.ba + 4 * nh], w[..., self.ga:self.cab]], axis=-1)


def _halo_specs(tr, w, cb, s):
    nb8 = s // 8
    return [pl.BlockSpec((8, w), lambda i: (jnp.maximum(i * (tr // 8) - 1, 0), cb)),
            pl.BlockSpec((tr, w), lambda i: (i, cb)),
            pl.BlockSpec((8, w), lambda i: (jnp.minimum((i + 1) * (tr // 8), nb8 - 1), cb))]


def _ext(prev_ref, main_ref, next_ref, i, nt):
    p = jnp.where(i > 0, prev_ref[...].astype(F32), 0.0)
    n = jnp.where(i < nt - 1, next_ref[...].astype(F32), 0.0)
    return jnp.concatenate([p, main_ref[...].astype(F32), n], axis=0)


def _shift(x, k):
    return x if k == 0 else pltpu.roll(x, (-k) % x.shape[0], 0)


def _conv_taps(x_ext, w, tr):
    kt = w.shape[0]
    acc = None
    for t in range(kt):
        term = _shift(x_ext, t - kt // 2)[8:8 + tr] * w[t:t + 1, :]
        acc = term if acc is None else acc + term
    return acc


def _prep_a(name, proj, conv_a, lay, tr=256):
    s, ca = proj.shape[0], lay.ca
    tr = _tile(s, tr, 8)
    nt, w = s // tr, 3 * ca

    def body(p_ref, m_ref, n_ref, w_ref, o_ref):
        x = _ext(p_ref, m_ref, n_ref, pl.program_id(0), nt)
        xv = x[:, ca:2 * ca] * x[:, 2 * ca:w]
        y = _conv_taps(xv, w_ref[...], tr)
        o_ref[...] = (m_ref[:, 0:ca] * y).astype(BF16)

    return pl.pallas_call(
        body, grid=(nt,), in_specs=_halo_specs(tr, w, lay.cab // w, s) + [_vec(conv_a.shape[0], ca)],
        out_specs=_row(tr, ca), out_shape=SDS((s, ca), BF16),
        compiler_params=_params("parallel"), name=name)(proj, proj, proj, conv_a)


def _prep_a_bwd(name, dya, proj, conv_a, dproj, lay, tr=256):
    s, ca = proj.shape[0], lay.ca
    tr = _tile(s, tr, 8)
    nt, w, kt = s // tr, 3 * ca, conv_a.shape[0]

    def body(p_ref, m_ref, n_ref, dp_ref, dm_ref, dn_ref, w_ref, _, o_ref, acc_ref):
        i = pl.program_id(0)

        @pl.when(i == 0)
        def _():
            acc_ref[...] = jnp.zeros_like(acc_ref)

        x = _ext(p_ref, m_ref, n_ref, i, nt)
        d_ext = _ext(dp_ref, dm_ref, dn_ref, i, nt)
        cb, cc, cv = x[:, 0:ca], x[:, ca:2 * ca], x[:, 2 * ca:w]
        xv = cc * cv
        wv = w_ref[...]
        dy_ext = d_ext * cb
        dx = None
        for t in range(kt):
            term = _shift(dy_ext, kt // 2 - t)[8:8 + tr] * wv[t:t + 1, :]
            dx = term if dx is None else dx + term
            acc_ref[t:t + 1, :] += jnp.sum(dy_ext[8:8 + tr] * _shift(xv, t - kt // 2)[8:8 + tr],
                                           axis=0, keepdims=True)
        y = _conv_taps(xv, wv, tr)
        o_ref[:, 0:ca] = (dm_ref[...] * y).astype(BF16)
        o_ref[:, ca:2 * ca] = (dx * cv[8:8 + tr]).astype(BF16)
        o_ref[:, 2 * ca:w] = (dx * cc[8:8 + tr]).astype(BF16)

    return pl.pallas_call(
        body, grid=(nt,),
        in_specs=_halo_specs(tr, w, lay.cab // w, s) + _halo_specs(tr, ca, 0, s)
        + [_vec(kt, ca), pl.BlockSpec(memory_space=pl.ANY)],
        out_specs=[_row(tr, w, lay.cab // w), _vec(8, ca)],
        out_shape=[SDS(dproj.shape, dproj.dtype), SDS((8, ca), F32)], input_output_aliases={7: 0},
        compiler_params=_params("arbitrary"), name=name)(proj, proj, proj, dya, dya, dya, conv_a, dproj)


def _qkv_act(c, nh, tr_rows):
    sact = jax.nn.silu(c)
    outs, inv = [], []
    for hd in range(3 * nh):
        sl = sact[:, hd * HEAD:(hd + 1) * HEAD]
        if hd < 2 * nh:
            r = lax.rsqrt(jnp.sum(sl * sl, axis=-1, keepdims=True) + EPS)
            inv.append(r)
            outs.append(sl * (r * (HEAD ** -0.5 if hd < nh else 1.0)))
        else:
            outs.append(sl)
    return jnp.concatenate(outs, axis=-1), sact, inv


def _prep_b(name, proj, conv_dn, lay, tr=256):
    s, w, nh = proj.shape[0], lay.qkv, lay.nh
    tr = _tile(s, tr, 8)
    nt = s // tr

    def body(p_ref, m_ref, n_ref, w_ref, o_ref):
        x = _ext(p_ref, m_ref, n_ref, pl.program_id(0), nt)
        c = _conv_taps(x, w_ref[...], tr)
        o_ref[...] = _qkv_act(c, nh, tr)[0]

    return pl.pallas_call(
        body, grid=(nt,), in_specs=_halo_specs(tr, w, 0, s) + [_vec(conv_dn.shape[0], w)],
        out_specs=_row(tr, w), out_shape=SDS((s, w), F32),
        compiler_params=_params("parallel"), name=name)(proj, proj, proj, conv_dn)


def _prep_b_bwd(name, dq, dk, dv, proj, conv_dn, dproj, lay, tr=256):
    s, w, nh, qk = proj.shape[0], lay.qkv, lay.nh, lay.qk
    tr = _tile(s, tr, 8)
    nt, kt = s // tr, conv_dn.shape[0]
    n_ext = tr + 16

    def body(*refs):
        x_refs, g_refs = refs[0:3], refs[3:12]
        w_ref, o_ref, acc_ref = refs[12], refs[14], refs[15]
        i = pl.program_id(0)

        @pl.when(i == 0)
        def _():
            acc_ref[...] = jnp.zeros_like(acc_ref)

        x = _ext(*x_refs, i, nt)
        wv = w_ref[...]
        c = None
        for t in range(kt):
            term = _shift(x, t - kt // 2) * wv[t:t + 1, :]
            c = term if c is None else c + term
        sact = jax.nn.silu(c)
        ds = []
        for part in range(3):
            g = _ext(*g_refs[3 * part:3 * part + 3], i, nt)
            for hd in range(nh):
                sl = sact[:, part * qk + hd * HEAD:part * qk + (hd + 1) * HEAD]
                gh = g[:, hd * HEAD:(hd + 1) * HEAD]
                if part < 2:
                    r = lax.rsqrt(jnp.sum(sl * sl, axis=-1, keepdims=True) + EPS)
                    sc = HEAD ** -0.5 if part == 0 else 1.0
                    ds.append(sc * r * (gh - sl * (r * r) * jnp.sum(gh * sl, axis=-1, keepdims=True)))
                else:
                    ds.append(gh)
        dc = jnp.concatenate(ds, axis=-1) * _silu_grad(c)
        rows = lax.broadcasted_iota(jnp.int32, (n_ext, 1), 0)
        dc = jnp.where((rows >= 2) & (rows < n_ext - 2), dc, 0.0)
        dx = None
        for t in range(kt):
            term = _shift(dc, kt // 2 - t)[8:8 + tr] * wv[t:t + 1, :]
            dx = term if dx is None else dx + term
            acc_ref[t:t + 1, :] += jnp.sum(dc[8:8 + tr] * _shift(x, t - kt // 2)[8:8 + tr],
                                           axis=0, keepdims=True)
        o_ref[...] = dx.astype(BF16)

    return pl.pallas_call(
        body, grid=(nt,),
        in_specs=_halo_specs(tr, w, 0, s) + _halo_specs(tr, qk, 0, s) * 3
        + [_vec(kt, w), pl.BlockSpec(memory_space=pl.ANY)],
        out_specs=[_row(tr, w, 0), _vec(8, w)],
        out_shape=[SDS(dproj.shape, dproj.dtype), SDS((8, w), F32)], input_output_aliases={13: 0},
        compiler_params=_params("arbitrary"), name=name)(
            proj, proj, proj, dq, dq, dq, dk, dk, dk, dv, dv, dv, conv_dn, dproj)


def _softplus(x):
    return jnp.maximum(x, 0.0) + jnp.log(1.0 + jnp.exp(-jnp.abs(x)))


def _prep_c(name, proj, pvec, lay, tr=512):
    s, nh = proj.shape[0], lay.nh
    tr = _tile(s, tr, 8)

    def body(x_ref, p_ref, o_ref):
        x = x_ref[...]
        lane = lax.broadcasted_iota(jnp.int32, x.shape, 1)
        g = -jnp.exp(p_ref[0:1, :]) * _softplus(x + p_ref[1:2, :])
        o_ref[...] = jnp.where(lane < 2 * nh, jax.nn.sigmoid(x), jnp.where(lane < 4 * nh, g, 0.0))

    return pl.pallas_call(
        body, grid=(s // tr,), in_specs=[_row(tr, LANES, lay.ba // LANES), _vec(8, LANES)],
        out_specs=_row(tr, LANES), out_shape=SDS((s, LANES), F32),
        compiler_params=_params("parallel"), name=name)(proj, pvec)


def _prep_c_bwd(name, dbg, proj, pvec, dproj, lay, tr=512):
    s, nh, tail = proj.shape[0], lay.nh, lay.tail
    tr = _tile(s, tr, 8)

    def body(x_ref, d_ref, p_ref, _, o_ref, acc_ref):
        @pl.when(pl.program_id(0) == 0)
        def _():
            acc_ref[...] = jnp.zeros_like(acc_ref)

        x, d = x_ref[...], d_ref[...]
        lane = lax.broadcasted_iota(jnp.int32, x.shape, 1)
        is_b, is_g = lane < 2 * nh, (lane >= 2 * nh) & (lane < 4 * nh)
        sb = jax.nn.sigmoid(x)
        na = -jnp.exp(p_ref[0:1, :])
        xs = x + p_ref[1:2, :]
        dsp = d * na * jax.nn.sigmoid(xs)
        dx = jnp.where(is_b, d * sb * (1.0 - sb), jnp.where(is_g, dsp, 0.0))
        o_ref[...] = jnp.zeros_like(o_ref)
        o_ref[:, 0:LANES] = dx.astype(BF16)
        acc_ref[0:1, :] += jnp.sum(jnp.where(is_g, d * na * _softplus(xs), 0.0), axis=0, keepdims=True)
        acc_ref[1:2, :] += jnp.sum(jnp.where(is_g, dsp, 0.0), axis=0, keepdims=True)

    return pl.pallas_call(
        body, grid=(s // tr,),
        in_specs=[_row(tr, LANES, lay.ba // LANES), _row(tr, LANES), _vec(8, LANES),
                  pl.BlockSpec(memory_space=pl.ANY)],
        out_specs=[_row(tr, tail, lay.ba // tail), _vec(8, LANES)],
        out_shape=[SDS(dproj.shape, dproj.dtype), SDS((8, LANES), F32)], input_output_aliases={3: 0},
        compiler_params=_params("arbitrary"), name=name)(proj, dbg, pvec, dproj)


def _post(name, o_f, o_b, proj, dn_w, lay, tr=256):
    s, qk, nh = o_f.shape[0], lay.qk, lay.nh
    tr = _tile(s, tr, 8)

    def body(f_ref, b_ref, z_ref, w_ref, o_ref):
        o = f_ref[...] + b_ref[...]
        gate = jax.nn.silu(z_ref[...])
        for hd in range(nh):
            sl = slice(hd * HEAD, (hd + 1) * HEAD)
            oh = o[:, sl]
            r = lax.rsqrt(jnp.mean(oh * oh, axis=-1, keepdims=True) + EPS)
            o_ref[:, sl] = (oh * r * w_ref[...] * gate[:, sl]).astype(BF16)

    return pl.pallas_call(
        body, grid=(s // tr,),
        in_specs=[_row(tr, qk), _row(tr, qk), _row(tr, qk, lay.z // qk), _vec(1, HEAD)],
        out_specs=_row(tr, qk), out_shape=SDS((s, qk), BF16),
        compiler_params=_params("parallel"), name=name)(o_f, o_b, proj, dn_w)


def _post_bwd(name, dyb, o_f, o_b, proj, dn_w, dproj, lay, tr=256):
    s, qk, nh = o_f.shape[0], lay.qk, lay.nh
    tr = _tile(s, tr, 8)

    def body(d_ref, f_ref, b_ref, z_ref, w_ref, _, do_ref, dz_ref, acc_ref):
        @pl.when(pl.program_id(0) == 0)
        def _():
            acc_ref[...] = jnp.zeros_like(acc_ref)

        o, z, d, wv = f_ref[...] + b_ref[...], z_ref[...], d_ref[...], w_ref[...]
        gate = jax.nn.silu(z)
        dgate = _silu_grad(z)
        for hd in range(nh):
            sl = slice(hd * HEAD, (hd + 1) * HEAD)
            oh, dh = o[:, sl], d[:, sl]
            r = lax.rsqrt(jnp.mean(oh * oh, axis=-1, keepdims=True) + EPS)
            n = oh * r
            dz_ref[:, sl] = (dh * n * wv * dgate[:, sl]).astype(BF16)
            don = dh * gate[:, sl]
            acc_ref[0:1, :] += jnp.sum(don * n, axis=0, keepdims=True)
            dn = don * wv
            do_ref[:, sl] = r * (dn - n * jnp.mean(dn * n, axis=-1, keepdims=True))

    return pl.pallas_call(
        body, grid=(s // tr,),
        in_specs=[_row(tr, qk), _row(tr, qk), _row(tr, qk), _row(tr, qk, lay.z // qk), _vec(1, HEAD),
                  pl.BlockSpec(memory_space=pl.ANY)],
        out_specs=[_row(tr, qk), _row(tr, qk, lay.z // qk), _vec(8, HEAD)],
        out_shape=[SDS((s, qk), F32), SDS(dproj.shape, dproj.dtype), SDS((8, HEAD), F32)],
        input_output_aliases={5: 1},
        compiler_params=_params("arbitrary"), name=name)(dyb, o_f, o_b, proj, dn_w, dproj)


def _merge(name, pa, pb, proj, lay, tr=512):
    s, d = pa.shape
    tr = _tile(s, tr, 8)

    def body(a_ref, b_ref, g_ref, o_ref):
        o_ref[...] = (jax.nn.sigmoid(g_ref[:, 0:d]) * a_ref[...]
                      + jax.nn.sigmoid(g_ref[:, d:2 * d]) * b_ref[...]).astype(BF16)

    return pl.pallas_call(
        body, grid=(s // tr,), in_specs=[_row(tr, d), _row(tr, d), _row(tr, 2 * d, lay.ga // (2 * d))],
        out_specs=_row(tr, d), out_shape=SDS((s, d), BF16),
        compiler_params=_params("parallel"), name=name)(pa, pb, proj)


def _merge_bwd(name, dmg, pa, pb, proj, lay, tr=512):
    s, d = pa.shape
    tr = _tile(s, tr, 8)

    def body(d_ref, a_ref, b_ref, g_ref, da_ref, db_ref, dg_ref):
        dm = d_ref[...]
        sa, sb = jax.nn.sigmoid(g_ref[:, 0:d]), jax.nn.sigmoid(g_ref[:, d:2 * d])
        da_ref[...] = (sa * dm).astype(BF16)
        db_ref[...] = (sb * dm).astype(BF16)
        dg_ref[:, 0:d] = (dm * a_ref[...] * sa * (1.0 - sa)).astype(BF16)
        dg_ref[:, d:2 * d] = (dm * b_ref[...] * sb * (1.0 - sb)).astype(BF16)

    return pl.pallas_call(
        body, grid=(s // tr,),
        in_specs=[_row(tr, d), _row(tr, d), _row(tr, d), _row(tr, 2 * d, lay.ga // (2 * d))],
        out_specs=[_row(tr, d), _row(tr, d), _row(tr, 2 * d, lay.ga // (2 * d))],
        out_shape=[SDS((s, d), BF16), SDS((s, d), BF16), SDS((s, lay.total), BF16)],
        compiler_params=_params("parallel"), name=name)(dmg, pa, pb, proj)


def _tri_inverse(a_mat, ri, ci):
    def same(shift):
        return (ri >> shift) == (ci >> shift)

    x = -jnp.where(same(3), a_mat, 0.0)
    t_mat = (ri == ci).astype(F32) + x
    for _ in range(2):
        x = _nn(x, x)
        t_mat = t_mat + _nn(t_mat, x)
    for shift in (3, 4, 5):
        b = jnp.where(same(shift + 1) & ~same(shift), a_mat, 0.0)
        t_mat = t_mat - _nn(_nn(t_mat, b), t_mat)
    return t_mat


def _chunk_terms(q, k, v, beta, g, reverse):
    c = CHUNK
    ri = lax.broadcasted_iota(jnp.int32, (c, c), 0)
    ci = lax.broadcasted_iota(jnp.int32, (c, c), 1)
    if reverse:
        incl, strict = ri <= ci, ri < ci
    else:
        incl, strict = ri >= ci, ri > ci
    incl_f = incl.astype(F32)
    incl_t = ((ri >= ci) if reverse else (ri <= ci)).astype(F32)
    gb = jnp.broadcast_to(g, (c, HEAD))
    gc = _nn(incl_f, gb, precision=HI)
    g_row = _nn(jnp.ones((c, c), F32), gb[:, 0:c] * incl_t, precision=HI)
    decay = jnp.where(incl, jnp.exp(jnp.where(incl, gc[:, 0:c] - g_row, 0.0)), 0.0)
    g_tot = jnp.sum(gb, axis=0, keepdims=True)
    e = jnp.exp(gc)
    ed = jnp.exp(g_tot - gc)
    el = jnp.exp(g_tot)
    kb = k * beta
    a_mat = jnp.where(strict, _nt(kb, k) * decay, 0.0)
    t_mat = _tri_inverse(a_mat, ri, ci)
    u = _nn(t_mat, v * beta)
    w = _nn(t_mat, kb * e)
    p_mat = jnp.where(incl, _nt(q, k) * decay, 0.0)
    return dict(incl=incl, strict=strict, incl_f=incl_f, decay=decay, e=e, ed=ed, el=el, kb=kb,
                a=a_mat, t=t_mat, u=u, w=w, p=p_mat)


def _delta_specs(nh, tb, nb, reverse):
    tok = (lambda i: nb - 1 - i) if reverse else (lambda i: i)
    qkv = [pl.BlockSpec((tb, HEAD), functools.partial(lambda hd, i, part: (tok(i), part * nh + hd), part=p))
           for p in range(3)]
    head = pl.BlockSpec((tb, HEAD), lambda hd, i: (tok(i), hd))
    bg = pl.BlockSpec((None, tb, 4), lambda hd, i: (hd, tok(i), 0))
    st = pl.BlockSpec((None, tb // CHUNK, HEAD, HEAD), lambda hd, i: (hd, tok(i), 0, 0))
    dbg = pl.BlockSpec((None, tb, 2), lambda hd, i: (hd, tok(i), 0))
    return qkv, head, bg, st, dbg


def _delta_fwd(name, qkvn, bg_t, nh, reverse, tb=256):
    s = qkvn.shape[0]
    tb = _tile(s, tb, CHUNK)
    nb, cpb = s // tb, tb // CHUNK
    qkv, head, bg, st, _ = _delta_specs(nh, tb, nb, reverse)
    cb, cg = (1, 3) if reverse else (0, 2)

    def body(q_ref, k_ref, v_ref, bg_ref, o_ref, st_ref, state):
        @pl.when(pl.program_id(1) == 0)
        def _():
            state[...] = jnp.zeros_like(state)

        for cj in (range(cpb - 1, -1, -1) if reverse else range(cpb)):
            rows = pl.ds(cj * CHUNK, CHUNK)
            q, k, v = q_ref[rows, :], k_ref[rows, :], v_ref[rows, :]
            tm = _chunk_terms(q, k, v, bg_ref[rows, cb:cb + 1], bg_ref[rows, cg:cg + 1], reverse)
            s_in = state[...]
            st_ref[cj] = s_in
            vn = tm["u"] - _nn(tm["w"], s_in)
            o_ref[rows, :] = _nn(q * tm["e"], s_in) + _nn(tm["p"], vn)
            state[...] = s_in * tm["el"] + _tn(k * tm["ed"], vn)

    return pl.pallas_call(
        body, grid=(nh, nb), in_specs=qkv + [bg], out_specs=[head, st],
        out_shape=[SDS((s, nh * HEAD), F32), SDS((nh, s // CHUNK, HEAD, HEAD), F32)],
        scratch_shapes=[pltpu.VMEM((HEAD, HEAD), F32)],
        compiler_params=_params("parallel", "arbitrary"), name=name)(qkvn, qkvn, qkvn, bg_t)


def _delta_bwd(name, qkvn, bg_t, do, states, nh, reverse, add=None, tb=256):
    s = qkvn.shape[0]
    tb = _tile(s, tb, CHUNK)
    nb, cpb = s // tb, tb // CHUNK
    qkv, head, bg, st, dbg = _delta_specs(nh, tb, nb, not reverse)
    cb, cg = (1, 3) if reverse else (0, 2)
    n_add = 0 if add is None else 3

    def body(*refs):
        q_ref, k_ref, v_ref, bg_ref, do_ref, st_ref = refs[0:6]
        add_refs = refs[6:6 + n_add]
        dq_ref, dk_ref, dv_ref, dbg_ref, dstate = refs[6 + n_add:]

        @pl.when(pl.program_id(1) == 0)
        def _():
            dstate[...] = jnp.zeros_like(dstate)

        ones = jnp.ones((CHUNK, HEAD), F32)
        for cj in (range(cpb) if reverse else range(cpb - 1, -1, -1)):
            rows = pl.ds(cj * CHUNK, CHUNK)
            q, k, v, d_o = q_ref[rows, :], k_ref[rows, :], v_ref[rows, :], do_ref[rows, :]
            beta = bg_ref[rows, cb:cb + 1]
            tm = _chunk_terms(q, k, v, beta, bg_ref[rows, cg:cg + 1], reverse)
            incl, strict, e, ed, el, kb = tm["incl"], tm["strict"], tm["e"], tm["ed"], tm["el"], tm["kb"]
            t_mat, u, w, p_mat, decay = tm["t"], tm["u"], tm["w"], tm["p"], tm["decay"]
            s_in, ds_out = st_ref[cj], dstate[...]
            vn = u - _nn(w, s_in)
            qe, kd, ke = q * e, k * ed, kb * e
            dvn = _tn(p_mat, d_o) + _nn(kd, ds_out)
            dqe = _nt(d_o, s_in)
            dq = dqe * e
            dgc = jnp.sum(dqe * qe, axis=1, keepdims=True)
            dp = jnp.where(incl, _nt(d_o, vn), 0.0)
            dkd = _nt(vn, ds_out)
            dk = dkd * ed
            r = jnp.sum(dkd * kd, axis=1, keepdims=True)
            dgc = dgc - r
            dg_tot = jnp.sum(r) + jnp.sum(ds_out * s_in) * el
            dw = -_nt(dvn, s_in)
            dbv = _tn(t_mat, dvn)
            dke = _tn(t_mat, dw)
            da = -jnp.where(strict, _nt(dbv, u) + _nt(dke, w), 0.0)
            m_mat, n_mat = da * decay, dp * decay
            dkb = _nn(m_mat, k) + dke * e
            dk = dk + _tn(m_mat, kb) + _tn(n_mat, q)
            dq = dq + _nn(n_mat, k)
            g_mat = da * tm["a"] + dp * p_mat
            col = _tn(g_mat, ones, precision=HI)[:, 0:1]
            dgc = dgc + jnp.sum(g_mat, axis=1, keepdims=True) - col + jnp.sum(dke * ke, axis=1, keepdims=True)
            dv = dbv * beta
            dbeta = jnp.sum(dbv * v, axis=1, keepdims=True) + jnp.sum(dkb * k, axis=1, keepdims=True)
            dk = dk + dkb * beta
            dstate[...] = el * ds_out + _tn(qe, d_o) - _tn(w, dvn)
            dg = _tn(tm["incl_f"], jnp.broadcast_to(dgc, (CHUNK, HEAD)), precision=HI) + dg_tot
            if n_add:
                dq = dq + add_refs[0][rows, :]
                dk = dk + add_refs[1][rows, :]
                dv = dv + add_refs[2][rows, :]
            dq_ref[rows, :], dk_ref[rows, :], dv_ref[rows, :] = dq, dk, dv
            dbg_ref[rows, 0:1] = dbeta
            dbg_ref[rows, 1:2] = dg[:, 0:1]

    out3 = SDS((s, nh * HEAD), F32)
    return pl.pallas_call(
        body, grid=(nh, nb), in_specs=qkv + [bg, head, st] + [head] * n_add,
        out_specs=[head, head, head, dbg], out_shape=[out3, out3, out3, SDS((nh, s, 2), F32)],
        scratch_shapes=[pltpu.VMEM((HEAD, HEAD), F32)],
        compiler_params=_params("parallel", "arbitrary"), name=name)(
            qkvn, qkvn, qkvn, bg_t, do, states, *(add or ()))


def _ffn_fwd(tag, h, nw, sh, sc, g, w_up, w_down):
    u = _norm_mod(tag + "_norm", h, nw, sh, sc)
    ab = _matmul(tag + "_up", u, w_up, "nn")
    hm = _swiglu(tag + "_act", ab)
    f = _matmul(tag + "_down", hm, w_down, "nn")
    return _resid(tag + "_res", h, f, g, 0.5), (h, u, ab, hm, f)


def _ffn_bwd(tag, dh, saved, nw, sc, g, w_up, w_down):
    h, u, ab, hm, f = saved
    df, acc_g = _resid_bwd(tag + "_res_bwd", dh, f, g, 0.5)
    gw_down = _matmul(tag + "_gw_down", hm, df, "tn")
    dhm = _matmul(tag + "_dhm", df, w_down, "nt")
    dab = _swiglu_bwd(tag + "_act_bwd", ab, dhm)
    gw_up = _matmul(tag + "_gw_up", u, dab, "tn")
    du = _matmul(tag + "_du", dab, w_up, "nt")
    dh_in, acc = _norm_mod_bwd(tag + "_norm_bwd", h, du, dh, nw, sc)
    return dh_in, gw_up, gw_down, (acc[0], acc[1], acc_g[0], acc[2])


def _bg_transpose(bg, nh):
    s = bg.shape[0]
    return bg[:, 0:4 * nh].reshape(s, 4, nh).transpose(2, 0, 1)


def _mixer_fwd(h, nw, sh, sc, g, wt, lay):
    nh = lay.nh
    u = _norm_mod("mix_norm", h, nw, sh, sc)
    proj = _matmul("mix_in", u, wt["w_in"], "nn", tn=512)
    qkvn = _prep_b("mix_prep_b", proj, wt["conv_dn"], lay)
    ya = _prep_a("mix_prep_a", proj, wt["conv_a"], lay)
    bg_t = _bg_transpose(_prep_c("mix_prep_c", proj, wt["pvec"], lay), nh)
    o_f, st_f = _delta_fwd("delta_fwd_l2r", qkvn, bg_t, nh, False)
    o_b, st_b = _delta_fwd("delta_fwd_r2l", qkvn, bg_t, nh, True)
    yb = _post("mix_post", o_f, o_b, proj, wt["dn_norm"], lay)
    pa = _matmul("mix_a_out", ya, wt["w_a_out"], "nn")
    pb = _matmul("mix_b_out", yb, wt["w_b_out"], "nn")
    mg = _merge("mix_merge", pa, pb, proj, lay)
    y = _matmul("mix_out", mg, wt["w_out"], "nn")
    h2 = _resid("mix_res", h, y, g, 1.0)
    return h2, (h, u, proj, qkvn, ya, bg_t, o_f, o_b, st_f, st_b, yb, pa, pb, mg, y)


def _mixer_bwd(dh, saved, nw, sc, g, wt, lay):
    h, u, proj, qkvn, ya, bg_t, o_f, o_b, st_f, st_b, yb, pa, pb, mg, y = saved
    nh, s = lay.nh, h.shape[0]
    dy, acc_g = _resid_bwd("mix_res_bwd", dh, y, g, 1.0)
    gw_out = _matmul("mix_gw_out", mg, dy, "tn")
    dmg = _matmul("mix_dmg", dy, wt["w_out"], "nt")
    dpa, dpb, dproj = _merge_bwd("mix_merge_bwd", dmg, pa, pb, proj, lay)
    gw_a = _matmul("mix_gw_a", ya, dpa, "tn")
    gw_b = _matmul("mix_gw_b", yb, dpb, "tn")
    dya = _matmul("mix_dya", dpa, wt["w_a_out"], "nt")
    dyb = _matmul("mix_dyb", dpb, wt["w_b_out"], "nt")
    do, dproj, acc_dn = _post_bwd("mix_post_bwd", dyb, o_f, o_b, proj, wt["dn_norm"], dproj, lay)
    dq, dk, dv, dbg_f = _delta_bwd("delta_bwd_l2r", qkvn, bg_t, do, st_f, nh, False)
    dq, dk, dv, dbg_b = _delta_bwd("delta_bwd_r2l", qkvn, bg_t, do, st_b, nh, True, add=(dq, dk, dv))
    dproj, acc_ca = _prep_a_bwd("mix_prep_a_bwd", dya, proj, wt["conv_a"], dproj, lay)
    dproj, acc_cd = _prep_b_bwd("mix_prep_b_bwd", dq, dk, dv, proj, wt["conv_dn"], dproj, lay)
    dbg = jnp.concatenate([dbg_f[:, :, 0].T, dbg_b[:, :, 0].T, dbg_f[:, :, 1].T, dbg_b[:, :, 1].T,
                           jnp.zeros((s, LANES - 4 * nh), F32)], axis=1)
    dproj, acc_pc = _prep_c_bwd("mix_prep_c_bwd", dbg, proj, wt["pvec"], dproj, lay)
    gw_in = _matmul("mix_gw_in", u, dproj, "tn", tn=512)
    du = _matmul("mix_du", dproj, wt["w_in"], "nt")
    dh_in, acc = _norm_mod_bwd("mix_norm_bwd", h, du, dh, nw, sc)
    small = dict(conv_a=acc_ca[0:wt["conv_a"].shape[0]], conv_dn=acc_cd[0:wt["conv_dn"].shape[0]],
                 dn_norm=acc_dn[0:1], a_log=acc_pc[0], dt_bias=acc_pc[1])
    return dh_in, dict(w_in=gw_in, w_a_out=gw_a, w_b_out=gw_b, w_out=gw_out), small, (acc[0], acc[1], acc_g[0], acc[2])


def _local_step(x, tgt, modv, wt, lay):
    m = [modv[i:i + 1] for i in range(9)]
    h1, sv1 = _ffn_fwd("ffn1", x, wt["norm_ffn1"], m[0], m[1], m[2], wt["w_ffn1_up"], wt["w_ffn1_down"])
    h2, sv2 = _mixer_fwd(h1, wt["norm_mix"], m[3], m[4], m[5], wt, lay)
    h3, sv3 = _ffn_fwd("ffn2", h2, wt["norm_ffn2"], m[6], m[7], m[8], wt["w_ffn2_up"], wt["w_ffn2_down"])
    dh3, acc_f = _final_loss("final_loss", h3, tgt, wt["norm_final"])
    loss = jnp.sum(acc_f[1])
    dh2, gu2, gd2, dm3 = _ffn_bwd("ffn2", dh3, sv3, wt["norm_ffn2"], m[7], m[8], wt["w_ffn2_up"], wt["w_ffn2_down"])
    dh1, gmix, small, dm2 = _mixer_bwd(dh2, sv2, wt["norm_mix"], m[4], m[5], wt, lay)
    dx, gu1, gd1, dm1 = _ffn_bwd("ffn1", dh1, sv1, wt["norm_ffn1"], m[1], m[2], wt["w_ffn1_up"], wt["w_ffn1_down"])
    dmod = jnp.stack([dm1[0], dm1[1], dm1[2], dm2[0], dm2[1], dm2[2], dm3[0], dm3[1], dm3[2]])
    big = dict(w_ffn1_up=gu1, w_ffn1_down=gd1, w_ffn2_up=gu2, w_ffn2_down=gd2, **gmix)
    small = dict(small, norm_ffn1=dm1[3], norm_mix=dm2[3], norm_ffn2=dm3[3], norm_final=acc_f[0])
    return loss, dx, dmod, big, small


def _position():
    return lax.axis_index("x"), lax.axis_index("y"), lax.axis_index("c")


_ANY = pl.BlockSpec(memory_space=pl.ANY)
_VMEM = pl.BlockSpec(memory_space=pltpu.VMEM)


def _allgather8(name, v):
    r = v.shape[0]

    def body(v_ref, out_ref, send_sems, recv_sems):
        x, y, c = _position()
        me = 4 * x + 2 * y + c
        out_ref[me] = v_ref[...]
        copies = []
        for mask in range(1, N_DEV):
            peer = tuple(1 - p if mask >> b & 1 else p for p, b in ((x, 2), (y, 1), (c, 0)))
            cp = pltpu.make_async_remote_copy(
                src_ref=v_ref, dst_ref=out_ref.at[me], send_sem=send_sems.at[mask - 1],
                recv_sem=recv_sems.at[mask - 1], device_id=peer, device_id_type=MESH)
            cp.start()
            copies.append(cp)
        for cp in copies:
            cp.wait()

    return pl.pallas_call(
        body, in_specs=[_VMEM], out_specs=_VMEM, out_shape=SDS((N_DEV, r, LANES), F32),
        scratch_shapes=[pltpu.SemaphoreType.DMA((N_DEV - 1,)), pltpu.SemaphoreType.DMA((N_DEV - 1,))],
        name=name)(v)


def _other_chips(x, y):
    return [(1 - x, y), (x, 1 - y), (1 - x, 1 - y)]


def _allgather_weights(name, pack):
    r, w = pack.shape
    hh = r // 2

    def body(p_ref, out_ref, send_sems, recv_sems, local_sem):
        x, y, c = _position()
        p = 2 * x + y
        half = pl.ds(pl.multiple_of(c * hh, 16), hh)
        other = pl.ds(pl.multiple_of((1 - c) * hh, 16), hh)
        chips = _other_chips(x, y)

        def copy(k, chip_index, rows, to, src=None):
            dst = out_ref.at[chip_index, rows]
            return pltpu.make_async_remote_copy(
                src_ref=dst if src is None else src, dst_ref=dst, send_sem=send_sems.at[k],
                recv_sem=recv_sems.at[k], device_id=to, device_id_type=MESH)

        mine = pltpu.make_async_copy(p_ref, out_ref.at[p], local_sem)
        mine.start()
        first = [copy(j, p, half, (cx, cy, c), src=p_ref.at[half]) for j, (cx, cy) in enumerate(chips)]
        for cp in first:
            cp.start()
        passed = []
        for j, (cx, cy) in enumerate(chips):
            copy(j, 2 * cx + cy, half, (x, y, c)).wait_recv()
            cp = copy(3 + j, 2 * cx + cy, half, (x, y, 1 - c))
            cp.start()
            passed.append(cp)
        for j, (cx, cy) in enumerate(chips):
            copy(3 + j, 2 * cx + cy, other, (x, y, c)).wait_recv()
        for cp in first + passed:
            cp.wait_send()
        mine.wait()

    return pl.pallas_call(
        body, in_specs=[_ANY], out_specs=_ANY, out_shape=SDS((N_CHIPS, r, w), pack.dtype),
        scratch_shapes=[pltpu.SemaphoreType.DMA((6,)), pltpu.SemaphoreType.DMA((6,)), pltpu.SemaphoreType.DMA],
        name=name)(pack)


def _swap_sibling(name, v):
    def body(v_ref, out_ref, send_sem, recv_sem):
        x, y, c = _position()
        cp = pltpu.make_async_remote_copy(src_ref=v_ref, dst_ref=out_ref, send_sem=send_sem, recv_sem=recv_sem,
                                          device_id=(x, y, 1 - c), device_id_type=MESH)
        cp.start()
        cp.wait()

    return pl.pallas_call(
        body, in_specs=[_ANY], out_specs=_ANY, out_shape=SDS(v.shape, v.dtype),
        scratch_shapes=[pltpu.SemaphoreType.DMA, pltpu.SemaphoreType.DMA], name=name)(v)


def _scatter_chips(name, v):
    _, r, w = v.shape

    def body(v_ref, out_ref, send_sems, recv_sems):
        x, y, c = _position()
        copies = []
        for j, (cx, cy) in enumerate(_other_chips(x, y)):
            cp = pltpu.make_async_remote_copy(
                src_ref=v_ref.at[2 * cx + cy], dst_ref=out_ref.at[j], send_sem=send_sems.at[j],
                recv_sem=recv_sems.at[j], device_id=(cx, cy, c), device_id_type=MESH)
            cp.start()
            copies.append(cp)
        for cp in copies:
            cp.wait()

    return pl.pallas_call(
        body, in_specs=[_ANY], out_specs=_ANY, out_shape=SDS((N_CHIPS - 1, r, w), v.dtype),
        scratch_shapes=[pltpu.SemaphoreType.DMA((3,)), pltpu.SemaphoreType.DMA((3,))], name=name)(v)


def _add2(name, a, b, tr=512):
    r, w = a.shape
    tr = _tile(r, tr, 16)

    def body(a_ref, b_ref, o_ref):
        o_ref[...] = (a_ref[...].astype(F32) + b_ref[...].astype(F32)).astype(o_ref.dtype)

    return pl.pallas_call(body, grid=(r // tr,), in_specs=[_row(tr, w), _row(tr, w)], out_specs=_row(tr, w),
                          out_shape=SDS((r, w), a.dtype), compiler_params=_params("parallel"), name=name)(a, b)


def _add5(name, a, b, rc, tr=512):
    r, w = a.shape
    tr = _tile(r, tr, 16)
    three = pl.BlockSpec((N_CHIPS - 1, tr, w), lambda i: (0, i, 0))

    def body(a_ref, b_ref, rc_ref, o_ref):
        acc = a_ref[...].astype(F32) + b_ref[...].astype(F32)
        for j in range(N_CHIPS - 1):
            acc = acc + rc_ref[j].astype(F32)
        o_ref[...] = acc

    return pl.pallas_call(body, grid=(r // tr,), in_specs=[_row(tr, w), _row(tr, w), three], out_specs=_row(tr, w),
                          out_shape=SDS((r, w), F32), compiler_params=_params("parallel"), name=name)(a, b, rc)


def _sum8(name, v):
    _, r, w = v.shape

    def body(v_ref, o_ref):
        acc = v_ref[0]
        for j in range(1, N_DEV):
            acc = acc + v_ref[j]
        o_ref[...] = acc

    return pl.pallas_call(body, in_specs=[_VMEM], out_specs=_VMEM, out_shape=SDS((r, w), F32), name=name)(v)


def _adamw(name, w, g, m, v, tr=256):
    r, cdim = w.shape
    tr = _tile(r, tr, 8)
    bc1, bc2 = 1.0 - ADAM_B1 ** ADAM_STEP, 1.0 - ADAM_B2 ** ADAM_STEP

    def body(w_ref, g_ref, m_ref, v_ref, d_ref, nm_ref, nv_ref):
        g = g_ref[...]
        m2 = ADAM_B1 * m_ref[...] + (1.0 - ADAM_B1) * g
        v2 = ADAM_B2 * v_ref[...] + (1.0 - ADAM_B2) * (g * g)
        d_ref[...] = -ADAM_LR * ((m2 / bc1) / (jnp.sqrt(v2 / bc2) + ADAM_EPS) + ADAM_WD * w_ref[...])
        nm_ref[...] = m2
        nv_ref[...] = v2

    spec = _row(tr, cdim)
    out = SDS((r, cdim), F32)
    return pl.pallas_call(body, grid=(r // tr,), in_specs=[spec] * 4, out_specs=[spec] * 3, out_shape=[out] * 3,
                          compiler_params=_params("parallel"), name=name)(w, g, m, v)


def _pack_rows(arrays, width, row_mult, dtype):
    parts, spans, row = [], [], 0
    for a in arrays:
        n = a.size
        rows = -(-n // width)
        flat = a.reshape(-1).astype(dtype)
        if rows * width != n:
            flat = jnp.concatenate([flat, jnp.zeros((rows * width - n,), dtype)])
        parts.append(flat.reshape(rows, width))
        spans.append((row, rows, n, a.shape))
        row += rows
    pad = -row % row_mult
    if pad:
        parts.append(jnp.zeros((pad, width), dtype))
    return jnp.concatenate(parts, axis=0), spans


def _unpack_rows(packed, spans):
    return [packed[r0:r0 + rows].reshape(-1)[0:n].reshape(shape) for r0, rows, n, shape in spans]


BIG = ("w_ffn1_up", "w_ffn1_down", "w_in", "w_a_out", "w_b_out", "w_out", "w_ffn2_up", "w_ffn2_down")
COL_SHARDED = ("w_ffn1_up", "w_in", "w_a_out", "w_ffn2_up")
SMALL = ("b_ada", "norm_ffn1", "norm_mix", "conv_a", "conv_dn", "a_log_fwd", "dt_bias_fwd", "a_log_bwd",
         "dt_bias_bwd", "dn_norm", "norm_ffn2", "norm_final")
WEIGHTS = ("w_ada", "b_ada", "norm_ffn1", "w_ffn1_up", "w_ffn1_down", "norm_mix", "w_in", "conv_a", "conv_dn",
           "a_log_fwd", "dt_bias_fwd", "a_log_bwd", "dt_bias_bwd", "dn_norm", "w_a_out", "w_b_out", "w_out",
           "norm_ffn2", "w_ffn2_up", "w_ffn2_down", "norm_final")
PACK_W = 1024
PACK_ROW_MULT = 32


def _join_shards(name, pieces):
    return jnp.concatenate(pieces, axis=1 if name in COL_SHARDED else 0)


def _split_shards(name, full):
    return jnp.split(full, N_CHIPS, axis=1 if name in COL_SHARDED else 0)


def kernel(x, c, w_ada, b_ada, norm_ffn1, w_ffn1_up, w_ffn1_down, norm_mix, w_in, conv_a, conv_dn, a_log_fwd, dt_bias_fwd, a_log_bwd, dt_bias_bwd, dn_norm, w_a_out, w_b_out, w_out, norm_ffn2, w_ffn2_up, w_ffn2_down, norm_final, loss_target, m_w_ada, m_b_ada, m_norm_ffn1, m_w_ffn1_up, m_w_ffn1_down, m_norm_mix, m_w_in, m_conv_a, m_conv_dn, m_a_log_fwd, m_dt_bias_fwd, m_a_log_bwd, m_dt_bias_bwd, m_dn_norm, m_w_a_out, m_w_b_out, m_w_out, m_norm_ffn2, m_w_ffn2_up, m_w_ffn2_down, m_norm_final, v_w_ada, v_b_ada, v_norm_ffn1, v_w_ffn1_up, v_w_ffn1_down, v_norm_mix, v_w_in, v_conv_a, v_conv_dn, v_a_log_fwd, v_dt_bias_fwd, v_a_log_bwd, v_dt_bias_bwd, v_dn_norm, v_w_a_out, v_w_b_out, v_w_out, v_norm_ffn2, v_w_ffn2_up, v_w_ffn2_down, v_norm_final):
    given = dict(locals())
    wsh = {n: given[n] for n in WEIGHTS}
    msh = {n: given["m_" + n] for n in WEIGHTS}
    vsh = {n: given["v_" + n] for n in WEIGHTS}
    d = x.shape[-1]
    ca = conv_a.shape[-1] * N_CHIPS
    nh = conv_dn.shape[-1] * N_CHIPS // (3 * HEAD)
    lay = _Layout(d, ca, nh)
    xi, yi, ci = _position()
    chip = 2 * xi + yi
    me = 2 * chip + ci

    c_act = jax.nn.silu(c)
    g1, g1_spans = _pack_rows([c_act, conv_a[0], conv_dn[0]], LANES, 8, F32)
    g1_all = _allgather8("gather_cond", g1)
    per_dev = [_unpack_rows(g1_all[k], g1_spans) for k in range(N_DEV)]
    c_all = jnp.concatenate([p[0] for p in per_dev], axis=0)
    conv_a_full = jnp.concatenate([per_dev[2 * k][1] for k in range(N_CHIPS)], axis=1)
    conv_dn_full = jnp.concatenate([per_dev[2 * k][2] for k in range(N_CHIPS)], axis=1)

    mod_sh = _matmul("ada_mod", c_all, w_ada[0], "nn")
    b_sh = lax.dynamic_slice_in_dim(b_ada, chip * mod_sh.shape[1], mod_sh.shape[1], axis=1)
    g2, g2_spans = _pack_rows([mod_sh + b_sh], LANES, 8, F32)
    g2_all = _allgather8("gather_mod", g2)
    mod_all = jnp.concatenate([_unpack_rows(g2_all[2 * k], g2_spans)[0] for k in range(N_CHIPS)], axis=1)
    modv = lax.dynamic_index_in_dim(mod_all, me, 0, keepdims=False).reshape(9, d)

    wpack, w_spans = _pack_rows([wsh[n][0] for n in BIG], PACK_W, PACK_ROW_MULT, BF16)
    wall = _allgather_weights("gather_weights", wpack)
    shards = [_unpack_rows(wall[k], w_spans) for k in range(N_CHIPS)]
    wt = {n: _join_shards(n, [shards[k][i] for k in range(N_CHIPS)]) for i, n in enumerate(BIG)}
    wt["w_in"] = lay.perm_cols(wt["w_in"])
    lane_pad = (jnp.zeros((2 * nh,), F32), jnp.zeros((LANES - 4 * nh,), F32))
    pvec = jnp.stack([jnp.concatenate([lane_pad[0], a_log_fwd[0], a_log_bwd[0], lane_pad[1]]),
                      jnp.concatenate([lane_pad[0], dt_bias_fwd[0], dt_bias_bwd[0], lane_pad[1]])]
                     + [jnp.zeros((LANES,), F32)] * 6)
    wt.update(conv_a=conv_a_full, conv_dn=conv_dn_full, pvec=pvec, dn_norm=dn_norm, norm_ffn1=norm_ffn1,
              norm_mix=norm_mix, norm_ffn2=norm_ffn2, norm_final=norm_final.reshape(1, d))

    loss, dx, dmod, big, small = _local_step(x[0], loss_target[0], modv, wt, lay)
    loss = lax.psum(loss, ("x", "y", "c"))
    big["w_in"] = lay.unperm_cols(big["w_in"])

    small_list = [dmod.reshape(1, 9 * d), small["norm_ffn1"], small["norm_mix"], small["conv_a"], small["conv_dn"],
                  small["a_log"][2 * nh:3 * nh], small["dt_bias"][2 * nh:3 * nh], small["a_log"][3 * nh:4 * nh],
                  small["dt_bias"][3 * nh:4 * nh], small["dn_norm"], small["norm_ffn2"], small["norm_final"]]
    g3, g3_spans = _pack_rows(small_list, LANES, 8, F32)
    g3_all = _allgather8("gather_small_grads", g3)
    g_small = dict(zip(SMALL, _unpack_rows(_sum8("sum_small_grads", g3_all), g3_spans)))
    dmod_all = jnp.concatenate([_unpack_rows(g3_all[k], g3_spans)[0] for k in range(N_DEV)], axis=0)
    ncol = w_ada.shape[-1]
    dmod_sh = lax.dynamic_slice_in_dim(dmod_all, chip * ncol, ncol, axis=1)
    grads = {"w_ada": _matmul("ada_grad", c_all, dmod_sh, "tn")[None]}
    for n in SMALL:
        g = g_small[n]
        if n in ("conv_a", "conv_dn"):
            wloc = wsh[n].shape[-1]
            g = lax.dynamic_slice_in_dim(g, chip * wloc, wloc, axis=1)
        grads[n] = g.reshape(wsh[n].shape)

    packs = []
    for k in range(N_CHIPS):
        pk, g_spans = _pack_rows([_split_shards(n, big[n])[k] for n in BIG], PACK_W, PACK_ROW_MULT, BF16)
        packs.append(pk)
    rp = packs[0].shape[0]
    gpack = jnp.stack(packs).reshape(N_CHIPS, 2, rp // 2, PACK_W)
    mine = lax.dynamic_index_in_dim(gpack, ci, 1, keepdims=False)
    theirs = lax.dynamic_index_in_dim(gpack, 1 - ci, 1, keepdims=False)
    from_sib = _swap_sibling("grads_to_sibling", theirs)
    chip_sum = _add2("grads_chip_sum", mine.reshape(-1, PACK_W), from_sib.reshape(-1, PACK_W))
    from_chips = _scatter_chips("grads_to_chips", chip_sum.reshape(N_CHIPS, rp // 2, PACK_W))
    own_a = lax.dynamic_index_in_dim(mine, chip, 0, keepdims=False)
    own_b = lax.dynamic_index_in_dim(from_sib, chip, 0, keepdims=False)
    my_half = _add5("grads_total", own_a, own_b, from_chips)
    sib_half = _swap_sibling("grads_share_total", my_half)
    halves = jnp.stack([my_half, sib_half])
    lo = lax.dynamic_index_in_dim(halves, ci, 0, keepdims=False)
    hi = lax.dynamic_index_in_dim(halves, 1 - ci, 0, keepdims=False)
    for n, g in zip(BIG, _unpack_rows(jnp.concatenate([lo, hi], axis=0), g_spans)):
        grads[n] = g[None]

    delta, new_m, new_v = {}, {}, {}
    for n in ("w_ada",) + BIG:
        shp = wsh[n].shape
        outs = _adamw("adamw_" + n, *(t.reshape(shp[-2], shp[-1]) for t in (wsh[n], grads[n], msh[n], vsh[n])))
        delta[n], new_m[n], new_v[n] = (o.reshape(shp) for o in outs)
    packed = []
    for src in (wsh, grads, msh, vsh):
        pk, s_spans = _pack_rows([src[n] for n in SMALL], LANES, 8, F32)
        packed.append(pk)
    outs = _adamw("adamw_small", *packed)
    for dst, o in zip((delta, new_m, new_v), outs):
        dst.update(zip(SMALL, _unpack_rows(o, s_spans)))

    return (loss, dx[None], *[grads[n] for n in WEIGHTS], *[delta[n] for n in WEIGHTS],
            *[new_m[n] for n in WEIGHTS], *[new_v[n] for n in WEIGHTS])
```

```python
import functools

import jax
import jax.numpy as jnp
from jax import lax
from jax.experimental import pallas as pl
from jax.experimental.pallas import tpu as pltpu

F32 = jnp.float32
BF16 = jnp.bfloat16
SDS = jax.ShapeDtypeStruct
MESH = pl.DeviceIdType.MESH
HI = lax.Precision.HIGHEST

EPS = 1e-6
HEAD = 128
CHUNK = 64
LANES = 128
N_CHIPS = 4
N_DEV = 8
VMEM_LIMIT = 56 * 1024 * 1024

ADAM_LR = 0.001
ADAM_B1 = 0.9
ADAM_B2 = 0.999
ADAM_EPS = 1e-08
ADAM_WD = 0.01
ADAM_STEP = 10


def _params(*sem):
    return pltpu.CompilerParams(dimension_semantics=sem, vmem_limit_bytes=VMEM_LIMIT)


def _tile(n, cap, mult=LANES):
    t = min(n, cap) // mult * mult
    while t >= mult:
        if n % t == 0:
            return t
        t -= mult
    return n


def _row(tr, w, cb=0):
    return pl.BlockSpec((tr, w), lambda i: (i, cb))


def _vec(r, w):
    return pl.BlockSpec((r, w), lambda i: (0, 0))


def _nn(a, b, **kw):
    return jnp.dot(a, b, preferred_element_type=F32, **kw)


def _nt(a, b, **kw):
    return lax.dot_general(a, b, (((1,), (1,)), ((), ())), preferred_element_type=F32, **kw)


def _tn(a, b, **kw):
    return lax.dot_general(a, b, (((0,), (0,)), ((), ())), preferred_element_type=F32, **kw)


def _bnn(a, b):
    return lax.dot_general(a, b, (((2,), (1,)), ((0,), (0,))), preferred_element_type=F32)


def _bnt(a, b):
    return lax.dot_general(a, b, (((2,), (2,)), ((0,), (0,))), preferred_element_type=F32)


def _btn(a, b):
    return lax.dot_general(a, b, (((1,), (1,)), ((0,), (0,))), preferred_element_type=F32)


def _silu_grad(x):
    s = jax.nn.sigmoid(x)
    return s * (1.0 + x * (1.0 - s))


def _matmul(name, a, b, mode, out_dtype=F32, tm=512, tn=1024, tk=512):
    if mode == "nn":
        (m, k), (_, n) = a.shape, b.shape
    elif mode == "nt":
        (m, k), (n, _) = a.shape, b.shape
    else:
        (k, m), (_, n) = a.shape, b.shape
    tm, tn, tk = _tile(m, tm), _tile(n, tn), _tile(k, tk)
    nk = k // tk
    a_spec = {"nn": pl.BlockSpec((tm, tk), lambda i, j, l: (i, l)),
              "nt": pl.BlockSpec((tm, tk), lambda i, j, l: (i, l)),
              "tn": pl.BlockSpec((tk, tm), lambda i, j, l: (l, i))}[mode]
    b_spec = {"nn": pl.BlockSpec((tk, tn), lambda i, j, l: (l, j)),
              "nt": pl.BlockSpec((tn, tk), lambda i, j, l: (j, l)),
              "tn": pl.BlockSpec((tk, tn), lambda i, j, l: (l, j))}[mode]
    dot = {"nn": _nn, "nt": _nt, "tn": _tn}[mode]

    def body(a_ref, b_ref, o_ref, acc):
        l = pl.program_id(2)

        @pl.when(l == 0)
        def _():
            acc[...] = jnp.zeros_like(acc)

        acc[...] += dot(a_ref[...].astype(BF16), b_ref[...].astype(BF16))

        @pl.when(l == nk - 1)
        def _():
            o_ref[...] = acc[...].astype(o_ref.dtype)

    return pl.pallas_call(
        body, grid=(m // tm, n // tn, nk), in_specs=[a_spec, b_spec],
        out_specs=pl.BlockSpec((tm, tn), lambda i, j, l: (i, j)),
        out_shape=SDS((m, n), out_dtype), scratch_shapes=[pltpu.VMEM((tm, tn), F32)],
        compiler_params=_params("parallel", "parallel", "arbitrary"), name=name)(a, b)


def _norm_mod(name, h, nw, sh, sc, tr=512):
    s, d = h.shape
    tr = _tile(s, tr, 8)

    def body(h_ref, nw_ref, sh_ref, sc_ref, u_ref):
        x = h_ref[...]
        r = lax.rsqrt(jnp.mean(x * x, axis=-1, keepdims=True) + EPS)
        u_ref[...] = (x * r * nw_ref[...] * (1.0 + sc_ref[...]) + sh_ref[...]).astype(BF16)

    return pl.pallas_call(
        body, grid=(s // tr,), in_specs=[_row(tr, d), _vec(1, d), _vec(1, d), _vec(1, d)],
        out_specs=_row(tr, d), out_shape=SDS((s, d), BF16),
        compiler_params=_params("parallel"), name=name)(h, nw, sh, sc)


def _norm_mod_bwd(name, h, du, dh, nw, sc, tr=512):
    s, d = h.shape
    tr = _tile(s, tr, 8)

    def body(h_ref, du_ref, dh_ref, nw_ref, sc_ref, o_ref, acc_ref):
        @pl.when(pl.program_id(0) == 0)
        def _():
            acc_ref[...] = jnp.zeros_like(acc_ref)

        x, g = h_ref[...], du_ref[...]
        r = lax.rsqrt(jnp.mean(x * x, axis=-1, keepdims=True) + EPS)
        n = x * r
        nw, sc1 = nw_ref[...], 1.0 + sc_ref[...]
        dn = g * sc1 * nw
        o_ref[...] = dh_ref[...] + r * (dn - n * jnp.mean(dn * n, axis=-1, keepdims=True))
        gn = g * n
        acc_ref[0:1, :] += jnp.sum(g, axis=0, keepdims=True)
        acc_ref[1:2, :] += jnp.sum(gn * nw, axis=0, keepdims=True)
        acc_ref[2:3, :] += jnp.sum(gn * sc1, axis=0, keepdims=True)

    return pl.pallas_call(
        body, grid=(s // tr,),
        in_specs=[_row(tr, d), _row(tr, d), _row(tr, d), _vec(1, d), _vec(1, d)],
        out_specs=[_row(tr, d), _vec(8, d)], out_shape=[SDS((s, d), F32), SDS((8, d), F32)],
        compiler_params=_params("arbitrary"), name=name)(h, du, dh, nw, sc)


def _swiglu(name, ab, tr=256):
    s, f2 = ab.shape
    f = f2 // 2
    tr = _tile(s, tr, 8)

    def body(a_ref, b_ref, o_ref):
        o_ref[...] = (jax.nn.silu(a_ref[...]) * b_ref[...]).astype(BF16)

    return pl.pallas_call(
        body, grid=(s // tr,), in_specs=[_row(tr, f, 0), _row(tr, f, 1)], out_specs=_row(tr, f),
        out_shape=SDS((s, f), BF16), compiler_params=_params("parallel"), name=name)(ab, ab)


def _swiglu_bwd(name, ab, dhm, tr=256):
    s, f2 = ab.shape
    f = f2 // 2
    tr = _tile(s, tr, 8)

    def body(a_ref, b_ref, d_ref, o_ref):
        a, d = a_ref[...], d_ref[...]
        o_ref[:, 0:f] = (d * b_ref[...] * _silu_grad(a)).astype(BF16)
        o_ref[:, f:f2] = (d * jax.nn.silu(a)).astype(BF16)

    return pl.pallas_call(
        body, grid=(s // tr,), in_specs=[_row(tr, f, 0), _row(tr, f, 1), _row(tr, f)],
        out_specs=_row(tr, f2), out_shape=SDS((s, f2), BF16),
        compiler_params=_params("parallel"), name=name)(ab, ab, dhm)


def _resid(name, h, f, g, scale, tr=512):
    s, d = h.shape
    tr = _tile(s, tr, 8)

    def body(h_ref, f_ref, g_ref, o_ref):
        o_ref[...] = h_ref[...] + (scale * g_ref[...]) * f_ref[...]

    return pl.pallas_call(
        body, grid=(s // tr,), in_specs=[_row(tr, d), _row(tr, d), _vec(1, d)], out_specs=_row(tr, d),
        out_shape=SDS((s, d), F32), compiler_params=_params("parallel"), name=name)(h, f, g)


def _resid_bwd(name, dh, f, g, scale, tr=512):
    s, d = dh.shape
    tr = _tile(s, tr, 8)

    def body(dh_ref, f_ref, g_ref, o_ref, acc_ref):
        @pl.when(pl.program_id(0) == 0)
        def _():
            acc_ref[...] = jnp.zeros_like(acc_ref)

        x = dh_ref[...]
        o_ref[...] = ((scale * g_ref[...]) * x).astype(BF16)
        acc_ref[0:1, :] += jnp.sum(scale * x * f_ref[...], axis=0, keepdims=True)

    return pl.pallas_call(
        body, grid=(s // tr,), in_specs=[_row(tr, d), _row(tr, d), _vec(1, d)],
        out_specs=[_row(tr, d), _vec(8, d)], out_shape=[SDS((s, d), BF16), SDS((8, d), F32)],
        compiler_params=_params("arbitrary"), name=name)(dh, f, g)


def _final_loss(name, h, tgt, nw, tr=512):
    s, d = h.shape
    tr = _tile(s, tr, 8)

    def body(h_ref, t_ref, nw_ref, o_ref, acc_ref):
        @pl.when(pl.program_id(0) == 0)
        def _():
            acc_ref[...] = jnp.zeros_like(acc_ref)

        x, nw = h_ref[...], nw_ref[...]
        r = lax.rsqrt(jnp.mean(x * x, axis=-1, keepdims=True) + EPS)
        n = x * r
        diff = n * nw - t_ref[...]
        dy = diff * (1.0 / d)
        dn = dy * nw
        o_ref[...] = r * (dn - n * jnp.mean(dn * n, axis=-1, keepdims=True))
        acc_ref[0:1, :] += jnp.sum(dy * n, axis=0, keepdims=True)
        acc_ref[1:2, :] += jnp.sum(diff * diff, axis=0, keepdims=True) * (0.5 / d)

    return pl.pallas_call(
        body, grid=(s // tr,), in_specs=[_row(tr, d), _row(tr, d), _vec(1, d)],
        out_specs=[_row(tr, d), _vec(8, d)], out_shape=[SDS((s, d), F32), SDS((8, d), F32)],
        compiler_params=_params("arbitrary"), name=name)(h, tgt, nw)


class _Layout:
    def __init__(self, d, ca, nh):
        self.d, self.ca, self.nh = d, ca, nh
        self.qk = nh * HEAD
        self.qkv = 3 * self.qk
        self.z = self.qkv
        self.ga = self.z + self.qk
        self.cab = self.ga + 2 * d
        self.ba = self.cab + 3 * ca
        self.tail = _tile(self.ba, 512)
        self.total = self.ba + self.tail
        assert self.qkv % self.qk == 0 and self.ga % (2 * d) == 0 and self.cab % (3 * ca) == 0
        assert self.ba % self.tail == 0 and 4 * nh <= LANES

    def perm_cols(self, w):
        ca, qkv, qk, d, nh = self.ca, self.qkv, self.qk, self.d, self.nh
        o = [0, ca, 2 * ca, 3 * ca, 3 * ca + qkv, 3 * ca + qkv + qk, 3 * ca + qkv + qk + 4 * nh]
        cb, cc, cv = (w[..., o[i]:o[i + 1]] for i in range(3))
        x_qkv, x_z, x_ba = w[..., o[3]:o[4]], w[..., o[4]:o[5]], w[..., o[5]:o[6]]
        gates = w[..., o[6]:o[6] + 2 * d]
        pad = jnp.zeros(w.shape[:-1] + (self.tail - 4 * nh,), w.dtype)
        return jnp.concatenate([x_qkv, x_z, gates, cb, cc, cv, x_ba, pad], axis=-1)

    def unperm_cols(self, w):
        ca, nh = self.ca, self.nh
        cb, cc, cv = (w[..., self.cab + i * ca:self.cab + (i + 1) * ca] for i in range(3))
        return jnp.concatenate([cb, cc, cv, w[..., 0:self.qkv], w[..., self.z:self.ga],
                                w[..., self.ba:self.ba + 4 * nh], w[..., self.ga:self.cab]], axis=-1)


def _halo_specs(tr, w, cb, s):
    nb8 = s // 8
    return [pl.BlockSpec((8, w), lambda i: (jnp.maximum(i * (tr // 8) - 1, 0), cb)),
            pl.BlockSpec((tr, w), lambda i: (i, cb)),
            pl.BlockSpec((8, w), lambda i: (jnp.minimum((i + 1) * (tr // 8), nb8 - 1), cb))]


def _ext(prev_ref, main_ref, next_ref, i, nt):
    p = jnp.where(i > 0, prev_ref[...].astype(F32), 0.0)
    n = jnp.where(i < nt - 1, next_ref[...].astype(F32), 0.0)
    return jnp.concatenate([p, main_ref[...].astype(F32), n], axis=0)


def _shift(x, k):
    return x if k == 0 else pltpu.roll(x, (-k) % x.shape[0], 0)


def _conv_taps(x_ext, w, tr):
    kt = w.shape[0]
    acc = None
    for t in range(kt):
        term = _shift(x_ext, t - kt // 2)[8:8 + tr] * w[t:t + 1, :]
        acc = term if acc is None else acc + term
    return acc


def _prep_a(name, proj, conv_a, lay, tr=256):
    s, ca = proj.shape[0], lay.ca
    tr = _tile(s, tr, 8)
    nt, w = s // tr, 3 * ca

    def body(p_ref, m_ref, n_ref, w_ref, o_ref):
        x = _ext(p_ref, m_ref, n_ref, pl.program_id(0), nt)
        xv = x[:, ca:2 * ca] * x[:, 2 * ca:w]
        y = _conv_taps(xv, w_ref[...], tr)
        o_ref[...] = (m_ref[:, 0:ca] * y).astype(BF16)

    return pl.pallas_call(
        body, grid=(nt,), in_specs=_halo_specs(tr, w, lay.cab // w, s) + [_vec(conv_a.shape[0], ca)],
        out_specs=_row(tr, ca), out_shape=SDS((s, ca), BF16),
        compiler_params=_params("parallel"), name=name)(proj, proj, proj, conv_a)


def _prep_a_bwd(name, dya, proj, conv_a, dproj, lay, tr=256):
    s, ca = proj.shape[0], lay.ca
    tr = _tile(s, tr, 8)
    nt, w, kt = s // tr, 3 * ca, conv_a.shape[0]

    def body(p_ref, m_ref, n_ref, dp_ref, dm_ref, dn_ref, w_ref, _, o_ref, acc_ref):
        i = pl.program_id(0)

        @pl.when(i == 0)
        def _():
            acc_ref[...] = jnp.zeros_like(acc_ref)

        x = _ext(p_ref, m_ref, n_ref, i, nt)
        d_ext = _ext(dp_ref, dm_ref, dn_ref, i, nt)
        cb, cc, cv = x[:, 0:ca], x[:, ca:2 * ca], x[:, 2 * ca:w]
        xv = cc * cv
        wv = w_ref[...]
        dy_ext = d_ext * cb
        dx = None
        for t in range(kt):
            term = _shift(dy_ext, kt // 2 - t)[8:8 + tr] * wv[t:t + 1, :]
            dx = term if dx is None else dx + term
            acc_ref[t:t + 1, :] += jnp.sum(dy_ext[8:8 + tr] * _shift(xv, t - kt // 2)[8:8 + tr],
                                           axis=0, keepdims=True)
        y = _conv_taps(xv, wv, tr)
        o_ref[:, 0:ca] = (dm_ref[...] * y).astype(BF16)
        o_ref[:, ca:2 * ca] = (dx * cv[8:8 + tr]).astype(BF16)
        o_ref[:, 2 * ca:w] = (dx * cc[8:8 + tr]).astype(BF16)

    return pl.pallas_call(
        body, grid=(nt,),
        in_specs=_halo_specs(tr, w, lay.cab // w, s) + _halo_specs(tr, ca, 0, s)
        + [_vec(kt, ca), pl.BlockSpec(memory_space=pl.ANY)],
        out_specs=[_row(tr, w, lay.cab // w), _vec(8, ca)],
        out_shape=[SDS(dproj.shape, dproj.dtype), SDS((8, ca), F32)], input_output_aliases={7: 0},
        compiler_params=_params("arbitrary"), name=name)(proj, proj, proj, dya, dya, dya, conv_a, dproj)


def _qkv_act(c, nh, tr_rows):
    sact = jax.nn.silu(c)
    outs, inv = [], []
    for hd in range(3 * nh):
        sl = sact[:, hd * HEAD:(hd + 1) * HEAD]
        if hd < 2 * nh:
            r = lax.rsqrt(jnp.sum(sl * sl, axis=-1, keepdims=True) + EPS)
            inv.append(r)
            outs.append(sl * (r * (HEAD ** -0.5 if hd < nh else 1.0)))
        else:
            outs.append(sl)
    return jnp.concatenate(outs, axis=-1), sact, inv


def _prep_b(name, proj, conv_dn, lay, tr=256):
    s, w, nh = proj.shape[0], lay.qkv, lay.nh
    tr = _tile(s, tr, 8)
    nt = s // tr

    def body(p_ref, m_ref, n_ref, w_ref, o_ref):
        x = _ext(p_ref, m_ref, n_ref, pl.program_id(0), nt)
        c = _conv_taps(x, w_ref[...], tr)
        o_ref[...] = _qkv_act(c, nh, tr)[0]

    return pl.pallas_call(
        body, grid=(nt,), in_specs=_halo_specs(tr, w, 0, s) + [_vec(conv_dn.shape[0], w)],
        out_specs=_row(tr, w), out_shape=SDS((s, w), F32),
        compiler_params=_params("parallel"), name=name)(proj, proj, proj, conv_dn)


def _prep_b_bwd(name, dq, dk, dv, proj, conv_dn, dproj, lay, tr=256):
    s, w, nh, qk = proj.shape[0], lay.qkv, lay.nh, lay.qk
    tr = _tile(s, tr, 8)
    nt, kt = s // tr, conv_dn.shape[0]
    n_ext = tr + 16

    def body(*refs):
        x_refs, g_refs = refs[0:3], refs[3:12]
        w_ref, o_ref, acc_ref = refs[12], refs[14], refs[15]
        i = pl.program_id(0)

        @pl.when(i == 0)
        def _():
            acc_ref[...] = jnp.zeros_like(acc_ref)

        x = _ext(*x_refs, i, nt)
        wv = w_ref[...]
        c = None
        for t in range(kt):
            term = _shift(x, t - kt // 2) * wv[t:t + 1, :]
            c = term if c is None else c + term
        sact = jax.nn.silu(c)
        ds = []
        for part in range(3):
            g = _ext(*g_refs[3 * part:3 * part + 3], i, nt)
            for hd in range(nh):
                sl = sact[:, part * qk + hd * HEAD:part * qk + (hd + 1) * HEAD]
                gh = g[:, hd * HEAD:(hd + 1) * HEAD]
                if part < 2:
                    r = lax.rsqrt(jnp.sum(sl * sl, axis=-1, keepdims=True) + EPS)
                    sc = HEAD ** -0.5 if part == 0 else 1.0
                    ds.append(sc * r * (gh - sl * (r * r) * jnp.sum(gh * sl, axis=-1, keepdims=True)))
                else:
                    ds.append(gh)
        dc = jnp.concatenate(ds, axis=-1) * _silu_grad(c)
        rows = lax.broadcasted_iota(jnp.int32, (n_ext, 1), 0)
        dc = jnp.where((rows >= 2) & (rows < n_ext - 2), dc, 0.0)
        dx = None
        for t in range(kt):
            term = _shift(dc, kt // 2 - t)[8:8 + tr] * wv[t:t + 1, :]
            dx = term if dx is None else dx + term
            acc_ref[t:t + 1, :] += jnp.sum(dc[8:8 + tr] * _shift(x, t - kt // 2)[8:8 + tr],
                                           axis=0, keepdims=True)
        o_ref[...] = dx.astype(BF16)

    return pl.pallas_call(
        body, grid=(nt,),
        in_specs=_halo_specs(tr, w, 0, s) + _halo_specs(tr, qk, 0, s) * 3
        + [_vec(kt, w), pl.BlockSpec(memory_space=pl.ANY)],
        out_specs=[_row(tr, w, 0), _vec(8, w)],
        out_shape=[SDS(dproj.shape, dproj.dtype), SDS((8, w), F32)], input_output_aliases={13: 0},
        compiler_params=_params("arbitrary"), name=name)(
            proj, proj, proj, dq, dq, dq, dk, dk, dk, dv, dv, dv, conv_dn, dproj)


def _softplus(x):
    return jnp.maximum(x, 0.0) + jnp.log(1.0 + jnp.exp(-jnp.abs(x)))


def _split3(x):
    hi = x.astype(BF16)
    r = x - hi.astype(F32)
    mid = r.astype(BF16)
    return hi, mid, (r - mid.astype(F32)).astype(BF16)


def _exact_nn(m, x):
    m = m.astype(BF16)
    hi, mid, lo = _split3(x)
    return _nn(m, hi) + _nn(m, mid) + _nn(m, lo)


def _chunk_cumsum_masks(tr):
    ri = lax.broadcasted_iota(jnp.int32, (tr, tr), 0)
    ci = lax.broadcasted_iota(jnp.int32, (tr, tr), 1)
    same = (ri // CHUNK) == (ci // CHUNK)
    return (same & (ci <= ri)).astype(F32), (same & (ci >= ri)).astype(F32)


def _prep_c(name, proj, pvec, lay, tr=512):
    s, nh = proj.shape[0], lay.nh
    tr = _tile(s, tr, CHUNK)
    assert 6 * nh <= LANES

    def body(x_ref, p_ref, o_ref):
        x = x_ref[...]
        lane = lax.broadcasted_iota(jnp.int32, x.shape, 1)
        is_g = (lane >= 2 * nh) & (lane < 4 * nh)
        g = jnp.where(is_g, -jnp.exp(p_ref[0:1, :]) * _softplus(x + p_ref[1:2, :]), 0.0)
        m_f, m_b = _chunk_cumsum_masks(tr)
        gc = jnp.where(lane < 3 * nh, _exact_nn(m_f, g), _exact_nn(m_b, g))
        gc = pltpu.roll(gc, 2 * nh, 1)
        o_ref[...] = jnp.where(lane < 2 * nh, jax.nn.sigmoid(x), jnp.where(lane < 4 * nh, g, gc))

    return pl.pallas_call(
        body, grid=(s // tr,), in_specs=[_row(tr, LANES, lay.ba // LANES), _vec(8, LANES)],
        out_specs=_row(tr, LANES), out_shape=SDS((s, LANES), F32),
        compiler_params=_params("parallel"), name=name)(proj, pvec)


def _prep_c_bwd(name, dbg_f, dbg_b, proj, pvec, dproj, lay, tr=512):
    s, nh, tail = proj.shape[0], lay.nh, lay.tail
    tr = _tile(s, tr, CHUNK)

    def body(x_ref, df_ref, db_ref, p_ref, _, o_ref, acc_ref):
        @pl.when(pl.program_id(0) == 0)
        def _():
            acc_ref[...] = jnp.zeros_like(acc_ref)

        x = x_ref[...]
        lane = lax.broadcasted_iota(jnp.int32, x.shape, 1)
        is_b, is_g = lane < 2 * nh, (lane >= 2 * nh) & (lane < 4 * nh)
        fwd_lane = (lane < nh) | ((lane >= 2 * nh) & (lane < 3 * nh))
        d = jnp.where(lane < 4 * nh, jnp.where(fwd_lane, df_ref[...], db_ref[...]), 0.0)
        m_f, m_b = _chunk_cumsum_masks(tr)
        dgc = jnp.where(is_g, d, 0.0)
        dg = jnp.where(fwd_lane, _exact_nn(m_b, dgc), _exact_nn(m_f, dgc))
        sb = jax.nn.sigmoid(x)
        na = -jnp.exp(p_ref[0:1, :])
        xs = x + p_ref[1:2, :]
        dsp = dg * na * jax.nn.sigmoid(xs)
        dx = jnp.where(is_b, d * sb * (1.0 - sb), jnp.where(is_g, dsp, 0.0))
        o_ref[...] = jnp.zeros_like(o_ref)
        o_ref[:, 0:LANES] = dx.astype(BF16)
        acc_ref[0:1, :] += jnp.sum(jnp.where(is_g, dg * na * _softplus(xs), 0.0), axis=0, keepdims=True)
        acc_ref[1:2, :] += jnp.sum(jnp.where(is_g, dsp, 0.0), axis=0, keepdims=True)

    return pl.pallas_call(
        body, grid=(s // tr,),
        in_specs=[_row(tr, LANES, lay.ba // LANES), _row(tr, LANES), _row(tr, LANES), _vec(8, LANES),
                  pl.BlockSpec(memory_space=pl.ANY)],
        out_specs=[_row(tr, tail, lay.ba // tail), _vec(8, LANES)],
        out_shape=[SDS(dproj.shape, dproj.dtype), SDS((8, LANES), F32)], input_output_aliases={4: 0},
        compiler_params=_params("arbitrary"), name=name)(proj, dbg_f, dbg_b, pvec, dproj)


def _post(name, o_f, o_b, proj, dn_w, lay, tr=256):
    s, qk, nh = o_f.shape[0], lay.qk, lay.nh
    tr = _tile(s, tr, 8)

    def body(f_ref, b_ref, z_ref, w_ref, o_ref):
        o = f_ref[...] + b_ref[...]
        gate = jax.nn.silu(z_ref[...])
        for hd in range(nh):
            sl = slice(hd * HEAD, (hd + 1) * HEAD)
            oh = o[:, sl]
            r = lax.rsqrt(jnp.mean(oh * oh, axis=-1, keepdims=True) + EPS)
            o_ref[:, sl] = (oh * r * w_ref[...] * gate[:, sl]).astype(BF16)

    return pl.pallas_call(
        body, grid=(s // tr,),
        in_specs=[_row(tr, qk), _row(tr, qk), _row(tr, qk, lay.z // qk), _vec(1, HEAD)],
        out_specs=_row(tr, qk), out_shape=SDS((s, qk), BF16),
        compiler_params=_params("parallel"), name=name)(o_f, o_b, proj, dn_w)


def _post_bwd(name, dyb, o_f, o_b, proj, dn_w, dproj, lay, tr=256):
    s, qk, nh = o_f.shape[0], lay.qk, lay.nh
    tr = _tile(s, tr, 8)

    def body(d_ref, f_ref, b_ref, z_ref, w_ref, _, do_ref, dz_ref, acc_ref):
        @pl.when(pl.program_id(0) == 0)
        def _():
            acc_ref[...] = jnp.zeros_like(acc_ref)

        o, z, d, wv = f_ref[...] + b_ref[...], z_ref[...], d_ref[...], w_ref[...]
        gate = jax.nn.silu(z)
        dgate = _silu_grad(z)
        for hd in range(nh):
            sl = slice(hd * HEAD, (hd + 1) * HEAD)
            oh, dh = o[:, sl], d[:, sl]
            r = lax.rsqrt(jnp.mean(oh * oh, axis=-1, keepdims=True) + EPS)
            n = oh * r
            dz_ref[:, sl] = (dh * n * wv * dgate[:, sl]).astype(BF16)
            don = dh * gate[:, sl]
            acc_ref[0:1, :] += jnp.sum(don * n, axis=0, keepdims=True)
            dn = don * wv
            do_ref[:, sl] = r * (dn - n * jnp.mean(dn * n, axis=-1, keepdims=True))

    return pl.pallas_call(
        body, grid=(s // tr,),
        in_specs=[_row(tr, qk), _row(tr, qk), _row(tr, qk), _row(tr, qk, lay.z // qk), _vec(1, HEAD),
                  pl.BlockSpec(memory_space=pl.ANY)],
        out_specs=[_row(tr, qk), _row(tr, qk, lay.z // qk), _vec(8, HEAD)],
        out_shape=[SDS((s, qk), F32), SDS(dproj.shape, dproj.dtype), SDS((8, HEAD), F32)],
        input_output_aliases={5: 1},
        compiler_params=_params("arbitrary"), name=name)(dyb, o_f, o_b, proj, dn_w, dproj)


def _merge(name, pa, pb, proj, lay, tr=512):
    s, d = pa.shape
    tr = _tile(s, tr, 8)

    def body(a_ref, b_ref, g_ref, o_ref):
        o_ref[...] = (jax.nn.sigmoid(g_ref[:, 0:d]) * a_ref[...]
                      + jax.nn.sigmoid(g_ref[:, d:2 * d]) * b_ref[...]).astype(BF16)

    return pl.pallas_call(
        body, grid=(s // tr,), in_specs=[_row(tr, d), _row(tr, d), _row(tr, 2 * d, lay.ga // (2 * d))],
        out_specs=_row(tr, d), out_shape=SDS((s, d), BF16),
        compiler_params=_params("parallel"), name=name)(pa, pb, proj)


def _merge_bwd(name, dmg, pa, pb, proj, lay, tr=512):
    s, d = pa.shape
    tr = _tile(s, tr, 8)

    def body(d_ref, a_ref, b_ref, g_ref, da_ref, db_ref, dg_ref):
        dm = d_ref[...]
        sa, sb = jax.nn.sigmoid(g_ref[:, 0:d]), jax.nn.sigmoid(g_ref[:, d:2 * d])
        da_ref[...] = (sa * dm).astype(BF16)
        db_ref[...] = (sb * dm).astype(BF16)
        dg_ref[:, 0:d] = (dm * a_ref[...] * sa * (1.0 - sa)).astype(BF16)
        dg_ref[:, d:2 * d] = (dm * b_ref[...] * sb * (1.0 - sb)).astype(BF16)

    return pl.pallas_call(
        body, grid=(s // tr,),
        in_specs=[_row(tr, d), _row(tr, d), _row(tr, d), _row(tr, 2 * d, lay.ga // (2 * d))],
        out_specs=[_row(tr, d), _row(tr, d), _row(tr, 2 * d, lay.ga // (2 * d))],
        out_shape=[SDS((s, d), BF16), SDS((s, d), BF16), SDS((s, lay.total), BF16)],
        compiler_params=_params("parallel"), name=name)(dmg, pa, pb, proj)


def _tri_inverse(a_mat, ri, ci):
    def same(shift):
        return (ri >> shift) == (ci >> shift)

    x = -jnp.where(same(3), a_mat, 0.0)
    t_mat = (ri == ci).astype(F32) + x
    for _ in range(2):
        x = _bnn(x, x)
        t_mat = t_mat + _bnn(t_mat, x)
    for shift in (3, 4, 5):
        b = jnp.where(same(shift + 1) & ~same(shift), a_mat, 0.0)
        t_mat = t_mat - _bnn(_bnn(t_mat, b), t_mat)
    return t_mat


def _chunk_terms(q, k, v, beta, gc, g_row, g_last, reverse):
    c = CHUNK
    ri = lax.broadcasted_iota(jnp.int32, (c, c), 0)
    ci = lax.broadcasted_iota(jnp.int32, (c, c), 1)
    if reverse:
        incl, strict = ri <= ci, ri < ci
    else:
        incl, strict = ri >= ci, ri > ci
    decay = jnp.where(incl, jnp.exp(jnp.where(incl, gc - g_row, 0.0)), 0.0)
    e = jnp.exp(gc)
    ed = jnp.exp(g_last - gc)
    el = jnp.exp(g_last)
    kb = k * beta
    a_mat = jnp.where(strict, _bnt(kb, k) * decay, 0.0)
    t_mat = _tri_inverse(a_mat, ri, ci)
    u = _bnn(t_mat, v * beta)
    w = _bnn(t_mat, kb * e)
    p_mat = jnp.where(incl, _bnt(q, k) * decay, 0.0)
    return dict(incl=incl, strict=strict, decay=decay, e=e, ed=ed, el=el, kb=kb,
                a=a_mat, t=t_mat, u=u, w=w, p=p_mat)


def _delta_specs(nh, tb, nb, reverse):
    tok = (lambda i: nb - 1 - i) if reverse else (lambda i: i)
    hw = nh * HEAD
    qkv = [pl.BlockSpec((tb, hw), functools.partial(lambda i, part: (tok(i), part), part=p)) for p in range(3)]
    rows = pl.BlockSpec((tb, hw), lambda i: (tok(i), 0))
    bg = pl.BlockSpec((tb, LANES), lambda i: (tok(i), 0))
    gct = pl.BlockSpec((2 * nh, tb), lambda i: (0, tok(i)))
    st = pl.BlockSpec((nh, tb // CHUNK, HEAD, HEAD), lambda i: (0, tok(i), 0, 0))
    return qkv, rows, bg, gct, st


def _heads(ref, rows, nh):
    return jnp.stack([ref[rows, hd * HEAD:(hd + 1) * HEAD] for hd in range(nh)])


def _chunk_scalars(bg_ref, gct_ref, cj, nh, tb, reverse):
    rows = pl.ds(cj * CHUNK, CHUNK)
    lb = nh if reverse else 0
    lc = 4 * nh + lb
    last = cj * CHUNK + (0 if reverse else CHUNK - 1)
    g_lanes = gct_ref[lb:lb + nh, :]
    if cj:
        g_lanes = pltpu.roll(g_lanes, tb - cj * CHUNK, 1)
    col = lambda l0, r: jnp.stack([bg_ref[r, l0 + hd:l0 + hd + 1] for hd in range(nh)])
    return col(lb, rows), col(lc, rows), g_lanes[:, 0:CHUNK][:, None, :], col(lc, pl.ds(last, 1))


def _delta_fwd(name, qkvn, bg, gct, nh, reverse, tb=128):
    s = qkvn.shape[0]
    tb = _tile(s, tb, LANES)
    nb, cpb = s // tb, tb // CHUNK
    qkv, rows_spec, bg_spec, gct_spec, st = _delta_specs(nh, tb, nb, reverse)

    def body(q_ref, k_ref, v_ref, bg_ref, gct_ref, o_ref, st_ref, state):
        @pl.when(pl.program_id(0) == 0)
        def _():
            state[...] = jnp.zeros_like(state)

        for cj in (range(cpb - 1, -1, -1) if reverse else range(cpb)):
            rows = pl.ds(cj * CHUNK, CHUNK)
            q, k, v = _heads(q_ref, rows, nh), _heads(k_ref, rows, nh), _heads(v_ref, rows, nh)
            tm = _chunk_terms(q, k, v, *_chunk_scalars(bg_ref, gct_ref, cj, nh, tb, reverse), reverse)
            s_in = state[...]
            st_ref[:, cj] = s_in
            vn = tm["u"] - _bnn(tm["w"], s_in)
            o = _bnn(q * tm["e"], s_in) + _bnn(tm["p"], vn)
            for hd in range(nh):
                o_ref[rows, hd * HEAD:(hd + 1) * HEAD] = o[hd]
            state[...] = s_in * tm["el"] + _btn(k * tm["ed"], vn)

    return pl.pallas_call(
        body, grid=(nb,), in_specs=qkv + [bg_spec, gct_spec], out_specs=[rows_spec, st],
        out_shape=[SDS((s, nh * HEAD), F32), SDS((nh, s // CHUNK, HEAD, HEAD), F32)],
        scratch_shapes=[pltpu.VMEM((nh, HEAD, HEAD), F32)],
        compiler_params=_params("arbitrary"), name=name)(qkvn, qkvn, qkvn, bg, gct)


def _delta_bwd(name, qkvn, bg, gct, do, states, nh, reverse, add=None, tb=128):
    s = qkvn.shape[0]
    tb = _tile(s, tb, LANES)
    nb, cpb = s // tb, tb // CHUNK
    qkv, rows_spec, bg_spec, gct_spec, st = _delta_specs(nh, tb, nb, not reverse)
    n_add = 0 if add is None else 3

    def body(*refs):
        q_ref, k_ref, v_ref, bg_ref, gct_ref, do_ref, st_ref = refs[0:7]
        add_refs = refs[7:7 + n_add]
        dq_ref, dk_ref, dv_ref, dbg_ref, dstate = refs[7 + n_add:]

        @pl.when(pl.program_id(0) == 0)
        def _():
            dstate[...] = jnp.zeros_like(dstate)

        ones = jnp.ones((nh, CHUNK, HEAD), BF16)
        row_id = lax.broadcasted_iota(jnp.int32, (CHUNK, 1), 0)
        rsum = lambda x: jnp.sum(x, axis=2, keepdims=True)
        for cj in (range(cpb) if reverse else range(cpb - 1, -1, -1)):
            rows = pl.ds(cj * CHUNK, CHUNK)
            q, k, v, d_o = (_heads(r, rows, nh) for r in (q_ref, k_ref, v_ref, do_ref))
            beta, gc, g_row, g_last = _chunk_scalars(bg_ref, gct_ref, cj, nh, tb, reverse)
            tm = _chunk_terms(q, k, v, beta, gc, g_row, g_last, reverse)
            incl, strict, e, ed, el, kb = tm["incl"], tm["strict"], tm["e"], tm["ed"], tm["el"], tm["kb"]
            t_mat, u, w, p_mat, decay = tm["t"], tm["u"], tm["w"], tm["p"], tm["decay"]
            s_in, ds_out = st_ref[:, cj], dstate[...]
            vn = u - _bnn(w, s_in)
            qe, kd, ke = q * e, k * ed, kb * e
            dvn = _btn(p_mat, d_o) + _bnn(kd, ds_out)
            dqe = _bnt(d_o, s_in)
            dq = dqe * e
            dgc = rsum(dqe * qe)
            dp = jnp.where(incl, _bnt(d_o, vn), 0.0)
            dkd = _bnt(vn, ds_out)
            dk = dkd * ed
            r = rsum(dkd * kd)
            dgc = dgc - r
            dg_last = (jnp.sum(r, axis=1, keepdims=True)
                       + jnp.sum(rsum(ds_out * s_in), axis=1, keepdims=True) * el)
            dw = -_bnt(dvn, s_in)
            dbv = _btn(t_mat, dvn)
            dke = _btn(t_mat, dw)
            da = -jnp.where(strict, _bnt(dbv, u) + _bnt(dke, w), 0.0)
            m_mat, n_mat = da * decay, dp * decay
            dkb = _bnn(m_mat, k) + dke * e
            dk = dk + _btn(m_mat, kb) + _btn(n_mat, q)
            dq = dq + _bnn(n_mat, k)
            g_mat = da * tm["a"] + dp * p_mat
            g_hi, g_mid, g_lo = _split3(g_mat)
            col = (_btn(g_hi, ones) + _btn(g_mid, ones) + _btn(g_lo, ones))[:, :, 0:1]
            dgc = dgc + rsum(g_mat) - col + rsum(dke * ke)
            dgc = dgc + jnp.where(row_id == (0 if reverse else CHUNK - 1), dg_last, 0.0)
            dv = dbv * beta
            dbeta = rsum(dbv * v) + rsum(dkb * k)
            dk = dk + dkb * beta
            dstate[...] = el * ds_out + _btn(qe, d_o) - _btn(w, dvn)
            lb = nh if reverse else 0
            for hd in range(nh):
                cols = slice(hd * HEAD, (hd + 1) * HEAD)
                extra = [a[rows, cols] for a in add_refs] if n_add else [0.0, 0.0, 0.0]
                dq_ref[rows, cols] = dq[hd] + extra[0]
                dk_ref[rows, cols] = dk[hd] + extra[1]
                dv_ref[rows, cols] = dv[hd] + extra[2]
                dbg_ref[rows, lb + hd:lb + hd + 1] = dbeta[hd]
                dbg_ref[rows, 2 * nh + lb + hd:2 * nh + lb + hd + 1] = dgc[hd]

    out3 = SDS((s, nh * HEAD), F32)
    return pl.pallas_call(
        body, grid=(nb,), in_specs=qkv + [bg_spec, gct_spec, rows_spec, st] + [rows_spec] * n_add,
        out_specs=[rows_spec, rows_spec, rows_spec, bg_spec], out_shape=[out3, out3, out3, SDS((s, LANES), F32)],
        scratch_shapes=[pltpu.VMEM((nh, HEAD, HEAD), F32)],
        compiler_params=_params("arbitrary"), name=name)(qkvn, qkvn, qkvn, bg, gct, do, states, *(add or ()))


def _ffn_fwd(tag, h, nw, sh, sc, g, w_up, w_down):
    u = _norm_mod(tag + "_norm", h, nw, sh, sc)
    ab = _matmul(tag + "_up", u, w_up, "nn")
    hm = _swiglu(tag + "_act", ab)
    f = _matmul(tag + "_down", hm, w_down, "nn")
    return _resid(tag + "_res", h, f, g, 0.5), (h, u, ab, hm, f)


def _ffn_bwd(tag, dh, saved, nw, sc, g, w_up, w_down):
    h, u, ab, hm, f = saved
    df, acc_g = _resid_bwd(tag + "_res_bwd", dh, f, g, 0.5)
    gw_down = _matmul(tag + "_gw_down", hm, df, "tn")
    dhm = _matmul(tag + "_dhm", df, w_down, "nt")
    dab = _swiglu_bwd(tag + "_act_bwd", ab, dhm)
    gw_up = _matmul(tag + "_gw_up", u, dab, "tn")
    du = _matmul(tag + "_du", dab, w_up, "nt")
    dh_in, acc = _norm_mod_bwd(tag + "_norm_bwd", h, du, dh, nw, sc)
    return dh_in, gw_up, gw_down, (acc[0], acc[1], acc_g[0], acc[2])


def _mixer_fwd(h, nw, sh, sc, g, wt, lay):
    nh = lay.nh
    u = _norm_mod("mix_norm", h, nw, sh, sc)
    proj = _matmul("mix_in", u, wt["w_in"], "nn", tn=512)
    qkvn = _prep_b("mix_prep_b", proj, wt["conv_dn"], lay)
    ya = _prep_a("mix_prep_a", proj, wt["conv_a"], lay)
    bg = _prep_c("mix_prep_c", proj, wt["pvec"], lay)
    gct = bg[:, 4 * nh:6 * nh].T
    o_f, st_f = _delta_fwd("delta_fwd_l2r", qkvn, bg, gct, nh, False)
    o_b, st_b = _delta_fwd("delta_fwd_r2l", qkvn, bg, gct, nh, True)
    yb = _post("mix_post", o_f, o_b, proj, wt["dn_norm"], lay)
    pa = _matmul("mix_a_out", ya, wt["w_a_out"], "nn")
    pb = _matmul("mix_b_out", yb, wt["w_b_out"], "nn")
    mg = _merge("mix_merge", pa, pb, proj, lay)
    y = _matmul("mix_out", mg, wt["w_out"], "nn")
    h2 = _resid("mix_res", h, y, g, 1.0)
    return h2, (h, u, proj, qkvn, ya, bg, gct, o_f, o_b, st_f, st_b, yb, pa, pb, mg, y)


def _mixer_bwd(dh, saved, nw, sc, g, wt, lay):
    h, u, proj, qkvn, ya, bg, gct, o_f, o_b, st_f, st_b, yb, pa, pb, mg, y = saved
    nh = lay.nh
    dy, acc_g = _resid_bwd("mix_res_bwd", dh, y, g, 1.0)
    gw_out = _matmul("mix_gw_out", mg, dy, "tn")
    dmg = _matmul("mix_dmg", dy, wt["w_out"], "nt")
    dpa, dpb, dproj = _merge_bwd("mix_merge_bwd", dmg, pa, pb, proj, lay)
    gw_a = _matmul("mix_gw_a", ya, dpa, "tn")
    gw_b = _matmul("mix_gw_b", yb, dpb, "tn")
    dya = _matmul("mix_dya", dpa, wt["w_a_out"], "nt")
    dyb = _matmul("mix_dyb", dpb, wt["w_b_out"], "nt")
    do, dproj, acc_dn = _post_bwd("mix_post_bwd", dyb, o_f, o_b, proj, wt["dn_norm"], dproj, lay)
    dq, dk, dv, dbg_f = _delta_bwd("delta_bwd_l2r", qkvn, bg, gct, do, st_f, nh, False)
    dq, dk, dv, dbg_b = _delta_bwd("delta_bwd_r2l", qkvn, bg, gct, do, st_b, nh, True, add=(dq, dk, dv))
    dproj, acc_ca = _prep_a_bwd("mix_prep_a_bwd", dya, proj, wt["conv_a"], dproj, lay)
    dproj, acc_cd = _prep_b_bwd("mix_prep_b_bwd", dq, dk, dv, proj, wt["conv_dn"], dproj, lay)
    dproj, acc_pc = _prep_c_bwd("mix_prep_c_bwd", dbg_f, dbg_b, proj, wt["pvec"], dproj, lay)
    gw_in = _matmul("mix_gw_in", u, dproj, "tn", tn=512)
    du = _matmul("mix_du", dproj, wt["w_in"], "nt")
    dh_in, acc = _norm_mod_bwd("mix_norm_bwd", h, du, dh, nw, sc)
    small = dict(conv_a=acc_ca[0:wt["conv_a"].shape[0]], conv_dn=acc_cd[0:wt["conv_dn"].shape[0]],
                 dn_norm=acc_dn[0:1], a_log=acc_pc[0], dt_bias=acc_pc[1])
    return dh_in, dict(w_in=gw_in, w_a_out=gw_a, w_b_out=gw_b, w_out=gw_out), small, (acc[0], acc[1], acc_g[0], acc[2])


def _local_step(x, tgt, modv, wt, lay):
    m = [modv[i:i + 1] for i in range(9)]
    h1, sv1 = _ffn_fwd("ffn1", x, wt["norm_ffn1"], m[0], m[1], m[2], wt["w_ffn1_up"], wt["w_ffn1_down"])
    h2, sv2 = _mixer_fwd(h1, wt["norm_mix"], m[3], m[4], m[5], wt, lay)
    h3, sv3 = _ffn_fwd("ffn2", h2, wt["norm_ffn2"], m[6], m[7], m[8], wt["w_ffn2_up"], wt["w_ffn2_down"])
    dh3, acc_f = _final_loss("final_loss", h3, tgt, wt["norm_final"])
    loss = jnp.sum(acc_f[1])
    dh2, gu2, gd2, dm3 = _ffn_bwd("ffn2", dh3, sv3, wt["norm_ffn2"], m[7], m[8], wt["w_ffn2_up"], wt["w_ffn2_down"])
    dh1, gmix, small, dm2 = _mixer_bwd(dh2, sv2, wt["norm_mix"], m[4], m[5], wt, lay)
    dx, gu1, gd1, dm1 = _ffn_bwd("ffn1", dh1, sv1, wt["norm_ffn1"], m[1], m[2], wt["w_ffn1_up"], wt["w_ffn1_down"])
    dmod = jnp.stack([dm1[0], dm1[1], dm1[2], dm2[0], dm2[1], dm2[2], dm3[0], dm3[1], dm3[2]])
    big = dict(w_ffn1_up=gu1, w_ffn1_down=gd1, w_ffn2_up=gu2, w_ffn2_down=gd2, **gmix)
    small = dict(small, norm_ffn1=dm1[3], norm_mix=dm2[3], norm_ffn2=dm3[3], norm_final=acc_f[0])
    return loss, dx, dmod, big, small


def _position():
    return lax.axis_index("x"), lax.axis_index("y"), lax.axis_index("c")


_ANY = pl.BlockSpec(memory_space=pl.ANY)
_VMEM = pl.BlockSpec(memory_space=pltpu.VMEM)


def _allgather8(name, v):
    r = v.shape[0]

    def body(v_ref, out_ref, send_sems, recv_sems):
        x, y, c = _position()
        me = 4 * x + 2 * y + c
        out_ref[me] = v_ref[...]
        copies = []
        for mask in range(1, N_DEV):
            peer = tuple(1 - p if mask >> b & 1 else p for p, b in ((x, 2), (y, 1), (c, 0)))
            cp = pltpu.make_async_remote_copy(
                src_ref=v_ref, dst_ref=out_ref.at[me], send_sem=send_sems.at[mask - 1],
                recv_sem=recv_sems.at[mask - 1], device_id=peer, device_id_type=MESH)
            cp.start()
            copies.append(cp)
        for cp in copies:
            cp.wait()

    return pl.pallas_call(
        body, in_specs=[_VMEM], out_specs=_VMEM, out_shape=SDS((N_DEV, r, LANES), F32),
        scratch_shapes=[pltpu.SemaphoreType.DMA((N_DEV - 1,)), pltpu.SemaphoreType.DMA((N_DEV - 1,))],
        name=name)(v)


def _other_chips(x, y):
    return [(1 - x, y), (x, 1 - y), (1 - x, 1 - y)]


def _allgather_weights(name, pack):
    r, w = pack.shape
    hh = r // 2

    def body(p_ref, out_ref, send_sems, recv_sems, local_sem):
        x, y, c = _position()
        p = 2 * x + y
        half = pl.ds(pl.multiple_of(c * hh, 16), hh)
        other = pl.ds(pl.multiple_of((1 - c) * hh, 16), hh)
        chips = _other_chips(x, y)

        def copy(k, chip_index, rows, to, src=None):
            dst = out_ref.at[chip_index, rows]
            return pltpu.make_async_remote_copy(
                src_ref=dst if src is None else src, dst_ref=dst, send_sem=send_sems.at[k],
                recv_sem=recv_sems.at[k], device_id=to, device_id_type=MESH)

        mine = pltpu.make_async_copy(p_ref, out_ref.at[p], local_sem)
        mine.start()
        first = [copy(j, p, half, (cx, cy, c), src=p_ref.at[half]) for j, (cx, cy) in enumerate(chips)]
        for cp in first:
            cp.start()
        passed = []
        for j, (cx, cy) in enumerate(chips):
            copy(j, 2 * cx + cy, half, (x, y, c)).wait_recv()
            cp = copy(3 + j, 2 * cx + cy, half, (x, y, 1 - c))
            cp.start()
            passed.append(cp)
        for j, (cx, cy) in enumerate(chips):
            copy(3 + j, 2 * cx + cy, other, (x, y, c)).wait_recv()
        for cp in first + passed:
            cp.wait_send()
        mine.wait()

    return pl.pallas_call(
        body, in_specs=[_ANY], out_specs=_ANY, out_shape=SDS((N_CHIPS, r, w), pack.dtype),
        scratch_shapes=[pltpu.SemaphoreType.DMA((6,)), pltpu.SemaphoreType.DMA((6,)), pltpu.SemaphoreType.DMA],
        name=name)(pack)


def _swap_sibling(name, v):
    def body(v_ref, out_ref, send_sem, recv_sem):
        x, y, c = _position()
        cp = pltpu.make_async_remote_copy(src_ref=v_ref, dst_ref=out_ref, send_sem=send_sem, recv_sem=recv_sem,
                                          device_id=(x, y, 1 - c), device_id_type=MESH)
        cp.start()
        cp.wait()

    return pl.pallas_call(
        body, in_specs=[_ANY], out_specs=_ANY, out_shape=SDS(v.shape, v.dtype),
        scratch_shapes=[pltpu.SemaphoreType.DMA, pltpu.SemaphoreType.DMA], name=name)(v)


def _scatter_chips(name, v):
    _, r, w = v.shape

    def body(v_ref, out_ref, send_sems, recv_sems):
        x, y, c = _position()
        copies = []
        for j, (cx, cy) in enumerate(_other_chips(x, y)):
            cp = pltpu.make_async_remote_copy(
                src_ref=v_ref.at[2 * cx + cy], dst_ref=out_ref.at[j], send_sem=send_sems.at[j],
                recv_sem=recv_sems.at[j], device_id=(cx, cy, c), device_id_type=MESH)
            cp.start()
            copies.append(cp)
        for cp in copies:
            cp.wait()

    return pl.pallas_call(
        body, in_specs=[_ANY], out_specs=_ANY, out_shape=SDS((N_CHIPS - 1, r, w), v.dtype),
        scratch_shapes=[pltpu.SemaphoreType.DMA((3,)), pltpu.SemaphoreType.DMA((3,))], name=name)(v)


def _add2(name, a, b, tr=512):
    r, w = a.shape
    tr = _tile(r, tr, 16)

    def body(a_ref, b_ref, o_ref):
        o_ref[...] = (a_ref[...].astype(F32) + b_ref[...].astype(F32)).astype(o_ref.dtype)

    return pl.pallas_call(body, grid=(r // tr,), in_specs=[_row(tr, w), _row(tr, w)], out_specs=_row(tr, w),
                          out_shape=SDS((r, w), a.dtype), compiler_params=_params("parallel"), name=name)(a, b)


def _add5(name, a, b, rc, tr=512):
    r, w = a.shape
    tr = _tile(r, tr, 16)
    three = pl.BlockSpec((N_CHIPS - 1, tr, w), lambda i: (0, i, 0))

    def body(a_ref, b_ref, rc_ref, o_ref):
        acc = a_ref[...].astype(F32) + b_ref[...].astype(F32)
        for j in range(N_CHIPS - 1):
            acc = acc + rc_ref[j].astype(F32)
        o_ref[...] = acc

    return pl.pallas_call(body, grid=(r // tr,), in_specs=[_row(tr, w), _row(tr, w), three], out_specs=_row(tr, w),
                          out_shape=SDS((r, w), F32), compiler_params=_params("parallel"), name=name)(a, b, rc)


def _sum8(name, v):
    _, r, w = v.shape

    def body(v_ref, o_ref):
        acc = v_ref[0]
        for j in range(1, N_DEV):
            acc = acc + v_ref[j]
        o_ref[...] = acc

    return pl.pallas_call(body, in_specs=[_VMEM], out_specs=_VMEM, out_shape=SDS((r, w), F32), name=name)(v)


def _adamw(name, w, g, m, v, tr=256):
    r, cdim = w.shape
    tr = _tile(r, tr, 8)
    bc1, bc2 = 1.0 - ADAM_B1 ** ADAM_STEP, 1.0 - ADAM_B2 ** ADAM_STEP

    def body(w_ref, g_ref, m_ref, v_ref, d_ref, nm_ref, nv_ref):
        g = g_ref[...]
        m2 = ADAM_B1 * m_ref[...] + (1.0 - ADAM_B1) * g
        v2 = ADAM_B2 * v_ref[...] + (1.0 - ADAM_B2) * (g * g)
        d_ref[...] = -ADAM_LR * ((m2 / bc1) / (jnp.sqrt(v2 / bc2) + ADAM_EPS) + ADAM_WD * w_ref[...])
        nm_ref[...] = m2
        nv_ref[...] = v2

    spec = _row(tr, cdim)
    out = SDS((r, cdim), F32)
    return pl.pallas_call(body, grid=(r // tr,), in_specs=[spec] * 4, out_specs=[spec] * 3, out_shape=[out] * 3,
                          compiler_params=_params("parallel"), name=name)(w, g, m, v)


def _pack_rows(arrays, width, row_mult, dtype):
    parts, spans, row = [], [], 0
    for a in arrays:
        n = a.size
        rows = -(-n // width)
        flat = a.reshape(-1).astype(dtype)
        if rows * width != n:
            flat = jnp.concatenate([flat, jnp.zeros((rows * width - n,), dtype)])
        parts.append(flat.reshape(rows, width))
        spans.append((row, rows, n, a.shape))
        row += rows
    pad = -row % row_mult
    if pad:
        parts.append(jnp.zeros((pad, width), dtype))
    return jnp.concatenate(parts, axis=0), spans


def _unpack_rows(packed, spans):
    return [packed[r0:r0 + rows].reshape(-1)[0:n].reshape(shape) for r0, rows, n, shape in spans]


BIG = ("w_ffn1_up", "w_ffn1_down", "w_in", "w_a_out", "w_b_out", "w_out", "w_ffn2_up", "w_ffn2_down")
COL_SHARDED = ("w_ffn1_up", "w_in", "w_a_out", "w_ffn2_up")
SMALL = ("b_ada", "norm_ffn1", "norm_mix", "conv_a", "conv_dn", "a_log_fwd", "dt_bias_fwd", "a_log_bwd",
         "dt_bias_bwd", "dn_norm", "norm_ffn2", "norm_final")
WEIGHTS = ("w_ada", "b_ada", "norm_ffn1", "w_ffn1_up", "w_ffn1_down", "norm_mix", "w_in", "conv_a", "conv_dn",
           "a_log_fwd", "dt_bias_fwd", "a_log_bwd", "dt_bias_bwd", "dn_norm", "w_a_out", "w_b_out", "w_out",
           "norm_ffn2", "w_ffn2_up", "w_ffn2_down", "norm_final")
PACK_W = 1024
PACK_ROW_MULT = 32


def _join_shards(name, pieces):
    return jnp.concatenate(pieces, axis=1 if name in COL_SHARDED else 0)


def _split_shards(name, full):
    return jnp.split(full, N_CHIPS, axis=1 if name in COL_SHARDED else 0)


def kernel(x, c, w_ada, b_ada, norm_ffn1, w_ffn1_up, w_ffn1_down, norm_mix, w_in, conv_a, conv_dn, a_log_fwd, dt_bias_fwd, a_log_bwd, dt_bias_bwd, dn_norm, w_a_out, w_b_out, w_out, norm_ffn2, w_ffn2_up, w_ffn2_down, norm_final, loss_target, m_w_ada, m_b_ada, m_norm_ffn1, m_w_ffn1_up, m_w_ffn1_down, m_norm_mix, m_w_in, m_conv_a, m_conv_dn, m_a_log_fwd, m_dt_bias_fwd, m_a_log_bwd, m_dt_bias_bwd, m_dn_norm, m_w_a_out, m_w_b_out, m_w_out, m_norm_ffn2, m_w_ffn2_up, m_w_ffn2_down, m_norm_final, v_w_ada, v_b_ada, v_norm_ffn1, v_w_ffn1_up, v_w_ffn1_down, v_norm_mix, v_w_in, v_conv_a, v_conv_dn, v_a_log_fwd, v_dt_bias_fwd, v_a_log_bwd, v_dt_bias_bwd, v_dn_norm, v_w_a_out, v_w_b_out, v_w_out, v_norm_ffn2, v_w_ffn2_up, v_w_ffn2_down, v_norm_final):
    given = dict(locals())
    wsh = {n: given[n] for n in WEIGHTS}
    msh = {n: given["m_" + n] for n in WEIGHTS}
    vsh = {n: given["v_" + n] for n in WEIGHTS}
    d = x.shape[-1]
    ca = conv_a.shape[-1] * N_CHIPS
    nh = conv_dn.shape[-1] * N_CHIPS // (3 * HEAD)
    lay = _Layout(d, ca, nh)
    xi, yi, ci = _position()
    chip = 2 * xi + yi
    me = 2 * chip + ci

    c_act = jax.nn.silu(c)
    g1, g1_spans = _pack_rows([c_act, conv_a[0], conv_dn[0]], LANES, 8, F32)
    g1_all = _allgather8("gather_cond", g1)
    per_dev = [_unpack_rows(g1_all[k], g1_spans) for k in range(N_DEV)]
    c_all = jnp.concatenate([p[0] for p in per_dev], axis=0)
    conv_a_full = jnp.concatenate([per_dev[2 * k][1] for k in range(N_CHIPS)], axis=1)
    conv_dn_full = jnp.concatenate([per_dev[2 * k][2] for k in range(N_CHIPS)], axis=1)

    mod_sh = _matmul("ada_mod", c_all, w_ada[0], "nn")
    b_sh = lax.dynamic_slice_in_dim(b_ada, chip * mod_sh.shape[1], mod_sh.shape[1], axis=1)
    g2, g2_spans = _pack_rows([mod_sh + b_sh], LANES, 8, F32)
    g2_all = _allgather8("gather_mod", g2)
    mod_all = jnp.concatenate([_unpack_rows(g2_all[2 * k], g2_spans)[0] for k in range(N_CHIPS)], axis=1)
    modv = lax.dynamic_index_in_dim(mod_all, me, 0, keepdims=False).reshape(9, d)

    wpack, w_spans = _pack_rows([wsh[n][0] for n in BIG], PACK_W, PACK_ROW_MULT, BF16)
    wall = _allgather_weights("gather_weights", wpack)
    shards = [_unpack_rows(wall[k], w_spans) for k in range(N_CHIPS)]
    wt = {n: _join_shards(n, [shards[k][i] for k in range(N_CHIPS)]) for i, n in enumerate(BIG)}
    wt["w_in"] = lay.perm_cols(wt["w_in"])
    lane_pad = (jnp.zeros((2 * nh,), F32), jnp.zeros((LANES - 4 * nh,), F32))
    pvec = jnp.stack([jnp.concatenate([lane_pad[0], a_log_fwd[0], a_log_bwd[0], lane_pad[1]]),
                      jnp.concatenate([lane_pad[0], dt_bias_fwd[0], dt_bias_bwd[0], lane_pad[1]])]
                     + [jnp.zeros((LANES,), F32)] * 6)
    wt.update(conv_a=conv_a_full, conv_dn=conv_dn_full, pvec=pvec, dn_norm=dn_norm, norm_ffn1=norm_ffn1,
              norm_mix=norm_mix, norm_ffn2=norm_ffn2, norm_final=norm_final.reshape(1, d))

    loss, dx, dmod, big, small = _local_step(x[0], loss_target[0], modv, wt, lay)
    loss = lax.psum(loss, ("x", "y", "c"))
    big["w_in"] = lay.unperm_cols(big["w_in"])

    small_list = [dmod.reshape(1, 9 * d), small["norm_ffn1"], small["norm_mix"], small["conv_a"], small["conv_dn"],
                  small["a_log"][2 * nh:3 * nh], small["dt_bias"][2 * nh:3 * nh], small["a_log"][3 * nh:4 * nh],
                  small["dt_bias"][3 * nh:4 * nh], small["dn_norm"], small["norm_ffn2"], small["norm_final"]]
    g3, g3_spans = _pack_rows(small_list, LANES, 8, F32)
    g3_all = _allgather8("gather_small_grads", g3)
    g_small = dict(zip(SMALL, _unpack_rows(_sum8("sum_small_grads", g3_all), g3_spans)))
    dmod_all = jnp.concatenate([_unpack_rows(g3_all[k], g3_spans)[0] for k in range(N_DEV)], axis=0)
    ncol = w_ada.shape[-1]
    dmod_sh = lax.dynamic_slice_in_dim(dmod_all, chip * ncol, ncol, axis=1)
    grads = {"w_ada": _matmul("ada_grad", c_all, dmod_sh, "tn")[None]}
    for n in SMALL:
        g = g_small[n]
        if n in ("conv_a", "conv_dn"):
            wloc = wsh[n].shape[-1]
            g = lax.dynamic_slice_in_dim(g, chip * wloc, wloc, axis=1)
        grads[n] = g.reshape(wsh[n].shape)

    packs = []
    for k in range(N_CHIPS):
        pk, g_spans = _pack_rows([_split_shards(n, big[n])[k] for n in BIG], PACK_W, PACK_ROW_MULT, BF16)
        packs.append(pk)
    rp = packs[0].shape[0]
    gpack = jnp.stack(packs).reshape(N_CHIPS, 2, rp // 2, PACK_W)
    mine = lax.dynamic_index_in_dim(gpack, ci, 1, keepdims=False)
    theirs = lax.dynamic_index_in_dim(gpack, 1 - ci, 1, keepdims=False)
    from_sib = _swap_sibling("grads_to_sibling", theirs)
    chip_sum = _add2("grads_chip_sum", mine.reshape(-1, PACK_W), from_sib.reshape(-1, PACK_W))
    from_chips = _scatter_chips("grads_to_chips", chip_sum.reshape(N_CHIPS, rp // 2, PACK_W))
    own_a = lax.dynamic_index_in_dim(mine, chip, 0, keepdims=False)
    own_b = lax.dynamic_index_in_dim(from_sib, chip, 0, keepdims=False)
    my_half = _add5("grads_total", own_a, own_b, from_chips)
    sib_half = _swap_sibling("grads_share_total", my_half)
    halves = jnp.stack([my_half, sib_half])
    lo = lax.dynamic_index_in_dim(halves, ci, 0, keepdims=False)
    hi = lax.dynamic_index_in_dim(halves, 1 - ci, 0, keepdims=False)
    for n, g in zip(BIG, _unpack_rows(jnp.concatenate([lo, hi], axis=0), g_spans)):
        grads[n] = g[None]

    delta, new_m, new_v = {}, {}, {}
    for n in ("w_ada",) + BIG:
        shp = wsh[n].shape
        outs = _adamw("adamw_" + n, *(t.reshape(shp[-2], shp[-1]) for t in (wsh[n], grads[n], msh[n], vsh[n])))
        delta[n], new_m[n], new_v[n] = (o.reshape(shp) for o in outs)
    packed = []
    for src in (wsh, grads, msh, vsh):
        pk, s_spans = _pack_rows([src[n] for n in SMALL], LANES, 8, F32)
        packed.append(pk)
    outs = _adamw("adamw_small", *packed)
    for dst, o in zip((delta, new_m, new_v), outs):
        dst.update(zip(SMALL, _unpack_rows(o, s_spans)))

    return (loss, dx[None], *[grads[n] for n in WEIGHTS], *[delta[n] for n in WEIGHTS],
            *[new_m[n] for n in WEIGHTS], *[new_v[n] for n in WEIGHTS])
```

```python
import functools

import jax
import jax.numpy as jnp
from jax import lax
from jax.experimental import pallas as pl
from jax.experimental.pallas import tpu as pltpu

F32 = jnp.float32
BF16 = jnp.bfloat16
SDS = jax.ShapeDtypeStruct
MESH = pl.DeviceIdType.MESH
HI = lax.Precision.HIGHEST

EPS = 1e-6
HEAD = 128
CHUNK = 64
LANES = 128
N_CHIPS = 4
N_DEV = 8
VMEM_LIMIT = 56 * 1024 * 1024

ADAM_LR = 0.001
ADAM_B1 = 0.9
ADAM_B2 = 0.999
ADAM_EPS = 1e-08
ADAM_WD = 0.01
ADAM_STEP = 10


def _params(*sem):
    return pltpu.CompilerParams(dimension_semantics=sem, vmem_limit_bytes=VMEM_LIMIT)


def _tile(n, cap, mult=LANES):
    t = min(n, cap) // mult * mult
    while t >= mult:
        if n % t == 0:
            return t
        t -= mult
    return n


def _row(tr, w, cb=0):
    return pl.BlockSpec((tr, w), lambda i: (i, cb))


def _vec(r, w):
    return pl.BlockSpec((r, w), lambda i: (0, 0))


def _nn(a, b, **kw):
    return jnp.dot(a, b, preferred_element_type=F32, **kw)


def _nt(a, b, **kw):
    return lax.dot_general(a, b, (((1,), (1,)), ((), ())), preferred_element_type=F32, **kw)


def _tn(a, b, **kw):
    return lax.dot_general(a, b, (((0,), (0,)), ((), ())), preferred_element_type=F32, **kw)


def _bnn(a, b):
    return lax.dot_general(a, b, (((2,), (1,)), ((0,), (0,))), preferred_element_type=F32)


def _bnt(a, b):
    return lax.dot_general(a, b, (((2,), (2,)), ((0,), (0,))), preferred_element_type=F32)


def _btn(a, b):
    return lax.dot_general(a, b, (((1,), (1,)), ((0,), (0,))), preferred_element_type=F32)


def _silu_grad(x):
    s = jax.nn.sigmoid(x)
    return s * (1.0 + x * (1.0 - s))


def _matmul(name, a, b, mode, out_dtype=F32, tm=1024, tn=1024, tk=2048, full_k=2816):
    if mode == "nn":
        (m, k), (_, n) = a.shape, b.shape
    elif mode == "nt":
        (m, k), (n, _) = a.shape, b.shape
    else:
        (k, m), (_, n) = a.shape, b.shape
    tm, tn = _tile(m, tm), _tile(n, tn)
    tk = k if k <= full_k else _tile(k, tk)
    nk = k // tk
    a_bytes, b_bytes = a.size * a.dtype.itemsize, b.size * b.dtype.itemsize
    j_outer = nk == 1 and b_bytes + a_bytes * (n // tn) < a_bytes + b_bytes * (m // tm)
    ij = (lambda g0, g1: (g1, g0)) if j_outer else (lambda g0, g1: (g0, g1))

    def spec(shape, pick):
        return pl.BlockSpec(shape, lambda g0, g1, l: pick(*ij(g0, g1), l))

    a_spec = {"nn": spec((tm, tk), lambda i, j, l: (i, l)), "nt": spec((tm, tk), lambda i, j, l: (i, l)),
              "tn": spec((tk, tm), lambda i, j, l: (l, i))}[mode]
    b_spec = {"nn": spec((tk, tn), lambda i, j, l: (l, j)), "nt": spec((tn, tk), lambda i, j, l: (j, l)),
              "tn": spec((tk, tn), lambda i, j, l: (l, j))}[mode]
    dot = {"nn": _nn, "nt": _nt, "tn": _tn}[mode]

    def body_one(a_ref, b_ref, o_ref):
        o_ref[...] = dot(a_ref[...].astype(BF16), b_ref[...].astype(BF16)).astype(o_ref.dtype)

    def body_acc(a_ref, b_ref, o_ref, acc):
        l = pl.program_id(2)
        part = dot(a_ref[...].astype(BF16), b_ref[...].astype(BF16))

        @pl.when(l == 0)
        def _():
            acc[...] = part

        @pl.when((l > 0) & (l < nk - 1))
        def _():
            acc[...] += part

        @pl.when(l == nk - 1)
        def _():
            o_ref[...] = (acc[...] + part).astype(o_ref.dtype)

    return pl.pallas_call(
        body_one if nk == 1 else body_acc, grid=(n // tn, m // tm, nk) if j_outer else (m // tm, n // tn, nk),
        in_specs=[a_spec, b_spec], out_specs=spec((tm, tn), lambda i, j, l: (i, j)),
        out_shape=SDS((m, n), out_dtype), scratch_shapes=[] if nk == 1 else [pltpu.VMEM((tm, tn), F32)],
        compiler_params=_params("parallel", "parallel", "arbitrary"), name=name)(a, b)


def _norm_mod(name, h, nw, sh, sc, tr=512):
    s, d = h.shape
    tr = _tile(s, tr, 8)

    def body(h_ref, nw_ref, sh_ref, sc_ref, u_ref):
        x = h_ref[...]
        r = lax.rsqrt(jnp.mean(x * x, axis=-1, keepdims=True) + EPS)
        u_ref[...] = (x * r * nw_ref[...] * (1.0 + sc_ref[...]) + sh_ref[...]).astype(BF16)

    return pl.pallas_call(
        body, grid=(s // tr,), in_specs=[_row(tr, d), _vec(1, d), _vec(1, d), _vec(1, d)],
        out_specs=_row(tr, d), out_shape=SDS((s, d), BF16),
        compiler_params=_params("parallel"), name=name)(h, nw, sh, sc)


def _norm_mod_bwd(name, h, du, dh, nw, sc, tr=512):
    s, d = h.shape
    tr = _tile(s, tr, 8)

    def body(h_ref, du_ref, dh_ref, nw_ref, sc_ref, o_ref, acc_ref):
        @pl.when(pl.program_id(0) == 0)
        def _():
            acc_ref[...] = jnp.zeros_like(acc_ref)

        x, g = h_ref[...], du_ref[...]
        r = lax.rsqrt(jnp.mean(x * x, axis=-1, keepdims=True) + EPS)
        n = x * r
        nw, sc1 = nw_ref[...], 1.0 + sc_ref[...]
        dn = g * sc1 * nw
        o_ref[...] = dh_ref[...] + r * (dn - n * jnp.mean(dn * n, axis=-1, keepdims=True))
        gn = g * n
        acc_ref[0:1, :] += jnp.sum(g, axis=0, keepdims=True)
        acc_ref[1:2, :] += jnp.sum(gn * nw, axis=0, keepdims=True)
        acc_ref[2:3, :] += jnp.sum(gn * sc1, axis=0, keepdims=True)

    return pl.pallas_call(
        body, grid=(s // tr,),
        in_specs=[_row(tr, d), _row(tr, d), _row(tr, d), _vec(1, d), _vec(1, d)],
        out_specs=[_row(tr, d), _vec(8, d)], out_shape=[SDS((s, d), F32), SDS((8, d), F32)],
        compiler_params=_params("arbitrary"), name=name)(h, du, dh, nw, sc)


def _swiglu(name, ab, tr=256):
    s, f2 = ab.shape
    f = f2 // 2
    tr = _tile(s, tr, 8)

    def body(a_ref, b_ref, o_ref):
        o_ref[...] = (jax.nn.silu(a_ref[...].astype(F32)) * b_ref[...].astype(F32)).astype(BF16)

    return pl.pallas_call(
        body, grid=(s // tr,), in_specs=[_row(tr, f, 0), _row(tr, f, 1)], out_specs=_row(tr, f),
        out_shape=SDS((s, f), BF16), compiler_params=_params("parallel"), name=name)(ab, ab)


def _swiglu_bwd(name, ab, dhm, tr=256):
    s, f2 = ab.shape
    f = f2 // 2
    tr = _tile(s, tr, 8)

    def body(a_ref, b_ref, d_ref, o_ref):
        a, d = a_ref[...].astype(F32), d_ref[...].astype(F32)
        o_ref[:, 0:f] = (d * b_ref[...].astype(F32) * _silu_grad(a)).astype(BF16)
        o_ref[:, f:f2] = (d * jax.nn.silu(a)).astype(BF16)

    return pl.pallas_call(
        body, grid=(s // tr,), in_specs=[_row(tr, f, 0), _row(tr, f, 1), _row(tr, f)],
        out_specs=_row(tr, f2), out_shape=SDS((s, f2), BF16),
        compiler_params=_params("parallel"), name=name)(ab, ab, dhm)


def _resid(name, h, f, g, scale, tr=512):
    s, d = h.shape
    tr = _tile(s, tr, 8)

    def body(h_ref, f_ref, g_ref, o_ref):
        o_ref[...] = h_ref[...] + (scale * g_ref[...]) * f_ref[...]

    return pl.pallas_call(
        body, grid=(s // tr,), in_specs=[_row(tr, d), _row(tr, d), _vec(1, d)], out_specs=_row(tr, d),
        out_shape=SDS((s, d), F32), compiler_params=_params("parallel"), name=name)(h, f, g)


def _resid_bwd(name, dh, f, g, scale, tr=512):
    s, d = dh.shape
    tr = _tile(s, tr, 8)

    def body(dh_ref, f_ref, g_ref, o_ref, acc_ref):
        @pl.when(pl.program_id(0) == 0)
        def _():
            acc_ref[...] = jnp.zeros_like(acc_ref)

        x = dh_ref[...]
        o_ref[...] = ((scale * g_ref[...]) * x).astype(BF16)
        acc_ref[0:1, :] += jnp.sum(scale * x * f_ref[...], axis=0, keepdims=True)

    return pl.pallas_call(
        body, grid=(s // tr,), in_specs=[_row(tr, d), _row(tr, d), _vec(1, d)],
        out_specs=[_row(tr, d), _vec(8, d)], out_shape=[SDS((s, d), BF16), SDS((8, d), F32)],
        compiler_params=_params("arbitrary"), name=name)(dh, f, g)


def _final_loss(name, h, tgt, nw, tr=512):
    s, d = h.shape
    tr = _tile(s, tr, 8)

    def body(h_ref, t_ref, nw_ref, o_ref, acc_ref):
        @pl.when(pl.program_id(0) == 0)
        def _():
            acc_ref[...] = jnp.zeros_like(acc_ref)

        x, nw = h_ref[...], nw_ref[...]
        r = lax.rsqrt(jnp.mean(x * x, axis=-1, keepdims=True) + EPS)
        n = x * r
        diff = n * nw - t_ref[...]
        dy = diff * (1.0 / d)
        dn = dy * nw
        o_ref[...] = r * (dn - n * jnp.mean(dn * n, axis=-1, keepdims=True))
        acc_ref[0:1, :] += jnp.sum(dy * n, axis=0, keepdims=True)
        acc_ref[1:2, :] += jnp.sum(diff * diff, axis=0, keepdims=True) * (0.5 / d)

    return pl.pallas_call(
        body, grid=(s // tr,), in_specs=[_row(tr, d), _row(tr, d), _vec(1, d)],
        out_specs=[_row(tr, d), _vec(8, d)], out_shape=[SDS((s, d), F32), SDS((8, d), F32)],
        compiler_params=_params("arbitrary"), name=name)(h, tgt, nw)


class _Layout:
    def __init__(self, d, ca, nh):
        self.d, self.ca, self.nh = d, ca, nh
        self.qk = nh * HEAD
        self.qkv = 3 * self.qk
        self.z = self.qkv
        self.ga = self.z + self.qk
        self.cab = self.ga + 2 * d
        self.ba = self.cab + 3 * ca
        self.tail = _tile(self.ba, 512)
        self.total = self.ba + self.tail
        assert self.qkv % self.qk == 0 and self.ga % (2 * d) == 0 and self.cab % (3 * ca) == 0
        assert self.ba % self.tail == 0 and 4 * nh <= LANES

    def perm_cols(self, w):
        ca, qkv, qk, d, nh = self.ca, self.qkv, self.qk, self.d, self.nh
        o = [0, ca, 2 * ca, 3 * ca, 3 * ca + qkv, 3 * ca + qkv + qk, 3 * ca + qkv + qk + 4 * nh]
        cb, cc, cv = (w[..., o[i]:o[i + 1]] for i in range(3))
        x_qkv, x_z, x_ba = w[..., o[3]:o[4]], w[..., o[4]:o[5]], w[..., o[5]:o[6]]
        gates = w[..., o[6]:o[6] + 2 * d]
        pad = jnp.zeros(w.shape[:-1] + (self.tail - 4 * nh,), w.dtype)
        return jnp.concatenate([x_qkv, x_z, gates, cb, cc, cv, x_ba, pad], axis=-1)

    def unperm_cols(self, w):
        ca, nh = self.ca, self.nh
        cb, cc, cv = (w[..., self.cab + i * ca:self.cab + (i + 1) * ca] for i in range(3))
        return jnp.concatenate([cb, cc, cv, w[..., 0:self.qkv], w[..., self.z:self.ga],
                                w[..., self.ba:self.ba + 4 * nh], w[..., self.ga:self.cab]], axis=-1)


def _halo_specs(tr, w, cb, s):
    nb8 = s // 8
    return [pl.BlockSpec((8, w), lambda i: (jnp.maximum(i * (tr // 8) - 1, 0), cb)),
            pl.BlockSpec((tr, w), lambda i: (i, cb)),
            pl.BlockSpec((8, w), lambda i: (jnp.minimum((i + 1) * (tr // 8), nb8 - 1), cb))]


def _ext(prev_ref, main_ref, next_ref, i, nt):
    p = jnp.where(i > 0, prev_ref[...].astype(F32), 0.0)
    n = jnp.where(i < nt - 1, next_ref[...].astype(F32), 0.0)
    return jnp.concatenate([p, main_ref[...].astype(F32), n], axis=0)


def _shift(x, k):
    return x if k == 0 else pltpu.roll(x, (-k) % x.shape[0], 0)


def _conv_taps(x_ext, w, tr):
    kt = w.shape[0]
    acc = None
    for t in range(kt):
        term = _shift(x_ext, t - kt // 2)[8:8 + tr] * w[t:t + 1, :]
        acc = term if acc is None else acc + term
    return acc


def _prep_a(name, proj, conv_a, lay, tr=256):
    s, ca = proj.shape[0], lay.ca
    tr = _tile(s, tr, 8)
    nt, w = s // tr, 3 * ca

    def body(p_ref, m_ref, n_ref, w_ref, o_ref):
        x = _ext(p_ref, m_ref, n_ref, pl.program_id(0), nt)
        xv = x[:, ca:2 * ca] * x[:, 2 * ca:w]
        y = _conv_taps(xv, w_ref[...], tr)
        o_ref[...] = (m_ref[:, 0:ca] * y).astype(BF16)

    return pl.pallas_call(
        body, grid=(nt,), in_specs=_halo_specs(tr, w, lay.cab // w, s) + [_vec(conv_a.shape[0], ca)],
        out_specs=_row(tr, ca), out_shape=SDS((s, ca), BF16),
        compiler_params=_params("parallel"), name=name)(proj, proj, proj, conv_a)


def _prep_a_bwd(name, dya, proj, conv_a, dproj, lay, tr=256):
    s, ca = proj.shape[0], lay.ca
    tr = _tile(s, tr, 8)
    nt, w, kt = s // tr, 3 * ca, conv_a.shape[0]

    def body(p_ref, m_ref, n_ref, dp_ref, dm_ref, dn_ref, w_ref, _, o_ref, acc_ref):
        i = pl.program_id(0)

        @pl.when(i == 0)
        def _():
            acc_ref[...] = jnp.zeros_like(acc_ref)

        x = _ext(p_ref, m_ref, n_ref, i, nt)
        d_ext = _ext(dp_ref, dm_ref, dn_ref, i, nt)
        cb, cc, cv = x[:, 0:ca], x[:, ca:2 * ca], x[:, 2 * ca:w]
        xv = cc * cv
        wv = w_ref[...]
        dy_ext = d_ext * cb
        dx = None
        for t in range(kt):
            term = _shift(dy_ext, kt // 2 - t)[8:8 + tr] * wv[t:t + 1, :]
            dx = term if dx is None else dx + term
            acc_ref[t:t + 1, :] += jnp.sum(dy_ext[8:8 + tr] * _shift(xv, t - kt // 2)[8:8 + tr],
                                           axis=0, keepdims=True)
        y = _conv_taps(xv, wv, tr)
        o_ref[:, 0:ca] = (dm_ref[...] * y).astype(BF16)
        o_ref[:, ca:2 * ca] = (dx * cv[8:8 + tr]).astype(BF16)
        o_ref[:, 2 * ca:w] = (dx * cc[8:8 + tr]).astype(BF16)

    return pl.pallas_call(
        body, grid=(nt,),
        in_specs=_halo_specs(tr, w, lay.cab // w, s) + _halo_specs(tr, ca, 0, s)
        + [_vec(kt, ca), pl.BlockSpec(memory_space=pl.ANY)],
        out_specs=[_row(tr, w, lay.cab // w), _vec(8, ca)],
        out_shape=[SDS(dproj.shape, dproj.dtype), SDS((8, ca), F32)], input_output_aliases={7: 0},
        compiler_params=_params("arbitrary"), name=name)(proj, proj, proj, dya, dya, dya, conv_a, dproj)


def _qkv_act(c, nh, tr_rows):
    sact = jax.nn.silu(c)
    outs, inv = [], []
    for hd in range(3 * nh):
        sl = sact[:, hd * HEAD:(hd + 1) * HEAD]
        if hd < 2 * nh:
            r = lax.rsqrt(jnp.sum(sl * sl, axis=-1, keepdims=True) + EPS)
            inv.append(r)
            outs.append(sl * (r * (HEAD ** -0.5 if hd < nh else 1.0)))
        else:
            outs.append(sl)
    return jnp.concatenate(outs, axis=-1), sact, inv


def _prep_b(name, proj, conv_dn, lay, tr=256):
    s, w, nh = proj.shape[0], lay.qkv, lay.nh
    tr = _tile(s, tr, 8)
    nt = s // tr

    def body(p_ref, m_ref, n_ref, w_ref, o_ref):
        x = _ext(p_ref, m_ref, n_ref, pl.program_id(0), nt)
        c = _conv_taps(x, w_ref[...], tr)
        o_ref[...] = _qkv_act(c, nh, tr)[0]

    return pl.pallas_call(
        body, grid=(nt,), in_specs=_halo_specs(tr, w, 0, s) + [_vec(conv_dn.shape[0], w)],
        out_specs=_row(tr, w), out_shape=SDS((s, w), F32),
        compiler_params=_params("parallel"), name=name)(proj, proj, proj, conv_dn)


def _prep_b_bwd(name, dq, dk, dv, proj, conv_dn, dproj, lay, tr=256):
    s, w, nh, qk = proj.shape[0], lay.qkv, lay.nh, lay.qk
    tr = _tile(s, tr, 8)
    nt, kt = s // tr, conv_dn.shape[0]
    n_ext = tr + 16

    def body(*refs):
        x_refs, g_refs = refs[0:3], refs[3:12]
        w_ref, o_ref, acc_ref = refs[12], refs[14], refs[15]
        i = pl.program_id(0)

        @pl.when(i == 0)
        def _():
            acc_ref[...] = jnp.zeros_like(acc_ref)

        x = _ext(*x_refs, i, nt)
        wv = w_ref[...]
        c = None
        for t in range(kt):
            term = _shift(x, t - kt // 2) * wv[t:t + 1, :]
            c = term if c is None else c + term
        sact = jax.nn.silu(c)
        ds = []
        for part in range(3):
            g = _ext(*g_refs[3 * part:3 * part + 3], i, nt)
            for hd in range(nh):
                sl = sact[:, part * qk + hd * HEAD:part * qk + (hd + 1) * HEAD]
                gh = g[:, hd * HEAD:(hd + 1) * HEAD]
                if part < 2:
                    r = lax.rsqrt(jnp.sum(sl * sl, axis=-1, keepdims=True) + EPS)
                    sc = HEAD ** -0.5 if part == 0 else 1.0
                    ds.append(sc * r * (gh - sl * (r * r) * jnp.sum(gh * sl, axis=-1, keepdims=True)))
                else:
                    ds.append(gh)
        dc = jnp.concatenate(ds, axis=-1) * _silu_grad(c)
        rows = lax.broadcasted_iota(jnp.int32, (n_ext, 1), 0)
        dc = jnp.where((rows >= 2) & (rows < n_ext - 2), dc, 0.0)
        dx = None
        for t in range(kt):
            term = _shift(dc, kt // 2 - t)[8:8 + tr] * wv[t:t + 1, :]
            dx = term if dx is None else dx + term
            acc_ref[t:t + 1, :] += jnp.sum(dc[8:8 + tr] * _shift(x, t - kt // 2)[8:8 + tr],
                                           axis=0, keepdims=True)
        o_ref[...] = dx.astype(BF16)

    return pl.pallas_call(
        body, grid=(nt,),
        in_specs=_halo_specs(tr, w, 0, s) + _halo_specs(tr, qk, 0, s) * 3
        + [_vec(kt, w), pl.BlockSpec(memory_space=pl.ANY)],
        out_specs=[_row(tr, w, 0), _vec(8, w)],
        out_shape=[SDS(dproj.shape, dproj.dtype), SDS((8, w), F32)], input_output_aliases={13: 0},
        compiler_params=_params("arbitrary"), name=name)(
            proj, proj, proj, dq, dq, dq, dk, dk, dk, dv, dv, dv, conv_dn, dproj)


def _softplus(x):
    return jnp.maximum(x, 0.0) + jnp.log(1.0 + jnp.exp(-jnp.abs(x)))


def _split3(x):
    hi = x.astype(BF16)
    r = x - hi.astype(F32)
    mid = r.astype(BF16)
    return hi, mid, (r - mid.astype(F32)).astype(BF16)


def _exact_nn(m, x):
    m = m.astype(BF16)
    hi, mid, lo = _split3(x)
    return _nn(m, hi) + _nn(m, mid) + _nn(m, lo)


def _chunk_cumsum_masks(tr):
    ri = lax.broadcasted_iota(jnp.int32, (tr, tr), 0)
    ci = lax.broadcasted_iota(jnp.int32, (tr, tr), 1)
    same = (ri // CHUNK) == (ci // CHUNK)
    return (same & (ci <= ri)).astype(F32), (same & (ci >= ri)).astype(F32)


def _prep_c(name, proj, pvec, lay, tr=512):
    s, nh = proj.shape[0], lay.nh
    tr = _tile(s, tr, CHUNK)
    assert 6 * nh <= LANES

    def body(x_ref, p_ref, o_ref):
        x = x_ref[...]
        lane = lax.broadcasted_iota(jnp.int32, x.shape, 1)
        is_g = (lane >= 2 * nh) & (lane < 4 * nh)
        g = jnp.where(is_g, -jnp.exp(p_ref[0:1, :]) * _softplus(x + p_ref[1:2, :]), 0.0)
        m_f, m_b = _chunk_cumsum_masks(tr)
        gc = jnp.where(lane < 3 * nh, _exact_nn(m_f, g), _exact_nn(m_b, g))
        gc = pltpu.roll(gc, 2 * nh, 1)
        o_ref[...] = jnp.where(lane < 2 * nh, jax.nn.sigmoid(x), jnp.where(lane < 4 * nh, g, gc))

    return pl.pallas_call(
        body, grid=(s // tr,), in_specs=[_row(tr, LANES, lay.ba // LANES), _vec(8, LANES)],
        out_specs=_row(tr, LANES), out_shape=SDS((s, LANES), F32),
        compiler_params=_params("parallel"), name=name)(proj, pvec)


def _prep_c_bwd(name, dbg_f, dbg_b, proj, pvec, dproj, lay, tr=512):
    s, nh, tail = proj.shape[0], lay.nh, lay.tail
    tr = _tile(s, tr, CHUNK)

    def body(x_ref, df_ref, db_ref, p_ref, _, o_ref, acc_ref):
        @pl.when(pl.program_id(0) == 0)
        def _():
            acc_ref[...] = jnp.zeros_like(acc_ref)

        x = x_ref[...]
        lane = lax.broadcasted_iota(jnp.int32, x.shape, 1)
        is_b, is_g = lane < 2 * nh, (lane >= 2 * nh) & (lane < 4 * nh)
        fwd_lane = (lane < nh) | ((lane >= 2 * nh) & (lane < 3 * nh))
        d = jnp.where(lane < 4 * nh, jnp.where(fwd_lane, df_ref[...], db_ref[...]), 0.0)
        m_f, m_b = _chunk_cumsum_masks(tr)
        dgc = jnp.where(is_g, d, 0.0)
        dg = jnp.where(fwd_lane, _exact_nn(m_b, dgc), _exact_nn(m_f, dgc))
        sb = jax.nn.sigmoid(x)
        na = -jnp.exp(p_ref[0:1, :])
        xs = x + p_ref[1:2, :]
        dsp = dg * na * jax.nn.sigmoid(xs)
        dx = jnp.where(is_b, d * sb * (1.0 - sb), jnp.where(is_g, dsp, 0.0))
        o_ref[...] = jnp.zeros_like(o_ref)
        o_ref[:, 0:LANES] = dx.astype(BF16)
        acc_ref[0:1, :] += jnp.sum(jnp.where(is_g, dg * na * _softplus(xs), 0.0), axis=0, keepdims=True)
        acc_ref[1:2, :] += jnp.sum(jnp.where(is_g, dsp, 0.0), axis=0, keepdims=True)

    return pl.pallas_call(
        body, grid=(s // tr,),
        in_specs=[_row(tr, LANES, lay.ba // LANES), _row(tr, LANES), _row(tr, LANES), _vec(8, LANES),
                  pl.BlockSpec(memory_space=pl.ANY)],
        out_specs=[_row(tr, tail, lay.ba // tail), _vec(8, LANES)],
        out_shape=[SDS(dproj.shape, dproj.dtype), SDS((8, LANES), F32)], input_output_aliases={4: 0},
        compiler_params=_params("arbitrary"), name=name)(proj, dbg_f, dbg_b, pvec, dproj)


def _post(name, o_f, o_b, proj, dn_w, lay, tr=256):
    s, qk, nh = o_f.shape[0], lay.qk, lay.nh
    tr = _tile(s, tr, 8)

    def body(f_ref, b_ref, z_ref, w_ref, o_ref):
        o = f_ref[...] + b_ref[...]
        gate = jax.nn.silu(z_ref[...])
        for hd in range(nh):
            sl = slice(hd * HEAD, (hd + 1) * HEAD)
            oh = o[:, sl]
            r = lax.rsqrt(jnp.mean(oh * oh, axis=-1, keepdims=True) + EPS)
            o_ref[:, sl] = (oh * r * w_ref[...] * gate[:, sl]).astype(BF16)

    return pl.pallas_call(
        body, grid=(s // tr,),
        in_specs=[_row(tr, qk), _row(tr, qk), _row(tr, qk, lay.z // qk), _vec(1, HEAD)],
        out_specs=_row(tr, qk), out_shape=SDS((s, qk), BF16),
        compiler_params=_params("parallel"), name=name)(o_f, o_b, proj, dn_w)


def _post_bwd(name, dyb, o_f, o_b, proj, dn_w, dproj, lay, tr=256):
    s, qk, nh = o_f.shape[0], lay.qk, lay.nh
    tr = _tile(s, tr, 8)

    def body(d_ref, f_ref, b_ref, z_ref, w_ref, _, do_ref, dz_ref, acc_ref):
        @pl.when(pl.program_id(0) == 0)
        def _():
            acc_ref[...] = jnp.zeros_like(acc_ref)

        o, z, d, wv = f_ref[...] + b_ref[...], z_ref[...], d_ref[...], w_ref[...]
        gate = jax.nn.silu(z)
        dgate = _silu_grad(z)
        for hd in range(nh):
            sl = slice(hd * HEAD, (hd + 1) * HEAD)
            oh, dh = o[:, sl], d[:, sl]
            r = lax.rsqrt(jnp.mean(oh * oh, axis=-1, keepdims=True) + EPS)
            n = oh * r
            dz_ref[:, sl] = (dh * n * wv * dgate[:, sl]).astype(BF16)
            don = dh * gate[:, sl]
            acc_ref[0:1, :] += jnp.sum(don * n, axis=0, keepdims=True)
            dn = don * wv
            do_ref[:, sl] = r * (dn - n * jnp.mean(dn * n, axis=-1, keepdims=True))

    return pl.pallas_call(
        body, grid=(s // tr,),
        in_specs=[_row(tr, qk), _row(tr, qk), _row(tr, qk), _row(tr, qk, lay.z // qk), _vec(1, HEAD),
                  pl.BlockSpec(memory_space=pl.ANY)],
        out_specs=[_row(tr, qk), _row(tr, qk, lay.z // qk), _vec(8, HEAD)],
        out_shape=[SDS((s, qk), F32), SDS(dproj.shape, dproj.dtype), SDS((8, HEAD), F32)],
        input_output_aliases={5: 1},
        compiler_params=_params("arbitrary"), name=name)(dyb, o_f, o_b, proj, dn_w, dproj)


def _merge(name, pa, pb, proj, lay, tr=512):
    s, d = pa.shape
    tr = _tile(s, tr, 8)

    def body(a_ref, b_ref, g_ref, o_ref):
        o_ref[...] = (jax.nn.sigmoid(g_ref[:, 0:d]) * a_ref[...]
                      + jax.nn.sigmoid(g_ref[:, d:2 * d]) * b_ref[...]).astype(BF16)

    return pl.pallas_call(
        body, grid=(s // tr,), in_specs=[_row(tr, d), _row(tr, d), _row(tr, 2 * d, lay.ga // (2 * d))],
        out_specs=_row(tr, d), out_shape=SDS((s, d), BF16),
        compiler_params=_params("parallel"), name=name)(pa, pb, proj)


def _merge_bwd(name, dmg, pa, pb, proj, lay, tr=512):
    s, d = pa.shape
    tr = _tile(s, tr, 8)

    def body(d_ref, a_ref, b_ref, g_ref, da_ref, db_ref, dg_ref):
        dm = d_ref[...]
        sa, sb = jax.nn.sigmoid(g_ref[:, 0:d]), jax.nn.sigmoid(g_ref[:, d:2 * d])
        da_ref[...] = (sa * dm).astype(BF16)
        db_ref[...] = (sb * dm).astype(BF16)
        dg_ref[:, 0:d] = (dm * a_ref[...] * sa * (1.0 - sa)).astype(BF16)
        dg_ref[:, d:2 * d] = (dm * b_ref[...] * sb * (1.0 - sb)).astype(BF16)

    return pl.pallas_call(
        body, grid=(s // tr,),
        in_specs=[_row(tr, d), _row(tr, d), _row(tr, d), _row(tr, 2 * d, lay.ga // (2 * d))],
        out_specs=[_row(tr, d), _row(tr, d), _row(tr, 2 * d, lay.ga // (2 * d))],
        out_shape=[SDS((s, d), BF16), SDS((s, d), BF16), SDS((s, lay.total), BF16)],
        compiler_params=_params("parallel"), name=name)(dmg, pa, pb, proj)


def _tri_inverse(a_mat, ri, ci):
    def same(shift):
        return (ri >> shift) == (ci >> shift)

    x = -jnp.where(same(3), a_mat, 0.0)
    t_mat = (ri == ci).astype(F32) + x
    for _ in range(2):
        x = _bnn(x, x)
        t_mat = t_mat + _bnn(t_mat, x)
    for shift in (3, 4, 5):
        b = jnp.where(same(shift + 1) & ~same(shift), a_mat, 0.0)
        t_mat = t_mat - _bnn(_bnn(t_mat, b), t_mat)
    return t_mat


def _chunk_terms(q, k, v, beta, gc, g_row, g_last, reverse):
    c = CHUNK
    ri = lax.broadcasted_iota(jnp.int32, (c, c), 0)
    ci = lax.broadcasted_iota(jnp.int32, (c, c), 1)
    if reverse:
        incl, strict = ri <= ci, ri < ci
    else:
        incl, strict = ri >= ci, ri > ci
    decay = jnp.where(incl, jnp.exp(jnp.where(incl, gc - g_row, 0.0)), 0.0)
    e = jnp.exp(gc)
    ed = jnp.exp(g_last - gc)
    el = jnp.exp(g_last)
    kb = k * beta
    a_mat = jnp.where(strict, _bnt(kb, k) * decay, 0.0)
    t_mat = _tri_inverse(a_mat, ri, ci)
    u = _bnn(t_mat, v * beta)
    w = _bnn(t_mat, kb * e)
    p_mat = jnp.where(incl, _bnt(q, k) * decay, 0.0)
    return dict(incl=incl, strict=strict, decay=decay, e=e, ed=ed, el=el, kb=kb,
                a=a_mat, t=t_mat, u=u, w=w, p=p_mat)


def _delta_specs(nh, tb, nb, reverse):
    tok = (lambda i: nb - 1 - i) if reverse else (lambda i: i)
    hw = nh * HEAD
    qkv = [pl.BlockSpec((tb, hw), functools.partial(lambda i, part: (tok(i), part), part=p)) for p in range(3)]
    rows = pl.BlockSpec((tb, hw), lambda i: (tok(i), 0))
    bg = pl.BlockSpec((tb, LANES), lambda i: (tok(i), 0))
    gct = pl.BlockSpec((2 * nh, tb), lambda i: (0, tok(i)))
    st = pl.BlockSpec((nh, tb // CHUNK, HEAD, HEAD), lambda i: (0, tok(i), 0, 0))
    return qkv, rows, bg, gct, st


def _heads(ref, rows, nh):
    return jnp.stack([ref[rows, hd * HEAD:(hd + 1) * HEAD] for hd in range(nh)])


def _chunk_scalars(bg_ref, gct_ref, cj, nh, tb, reverse):
    rows = pl.ds(cj * CHUNK, CHUNK)
    lb = nh if reverse else 0
    lc = 4 * nh + lb
    last = cj * CHUNK + (0 if reverse else CHUNK - 1)
    g_lanes = gct_ref[lb:lb + nh, :]
    if cj:
        g_lanes = pltpu.roll(g_lanes, tb - cj * CHUNK, 1)
    col = lambda l0, r: jnp.stack([bg_ref[r, l0 + hd:l0 + hd + 1] for hd in range(nh)])
    return col(lb, rows), col(lc, rows), g_lanes[:, 0:CHUNK][:, None, :], col(lc, pl.ds(last, 1))


def _delta_fwd(name, qkvn, bg, gct, nh, reverse, tb=128):
    s = qkvn.shape[0]
    tb = _tile(s, tb, LANES)
    nb, cpb = s // tb, tb // CHUNK
    qkv, rows_spec, bg_spec, gct_spec, st = _delta_specs(nh, tb, nb, reverse)

    def body(q_ref, k_ref, v_ref, bg_ref, gct_ref, o_ref, st_ref, state):
        @pl.when(pl.program_id(0) == 0)
        def _():
            state[...] = jnp.zeros_like(state)

        for cj in (range(cpb - 1, -1, -1) if reverse else range(cpb)):
            rows = pl.ds(cj * CHUNK, CHUNK)
            q, k, v = _heads(q_ref, rows, nh), _heads(k_ref, rows, nh), _heads(v_ref, rows, nh)
            tm = _chunk_terms(q, k, v, *_chunk_scalars(bg_ref, gct_ref, cj, nh, tb, reverse), reverse)
            s_in = state[...]
            st_ref[:, cj] = s_in
            vn = tm["u"] - _bnn(tm["w"], s_in)
            o = _bnn(q * tm["e"], s_in) + _bnn(tm["p"], vn)
            for hd in range(nh):
                o_ref[rows, hd * HEAD:(hd + 1) * HEAD] = o[hd]
            state[...] = s_in * tm["el"] + _btn(k * tm["ed"], vn)

    return pl.pallas_call(
        body, grid=(nb,), in_specs=qkv + [bg_spec, gct_spec], out_specs=[rows_spec, st],
        out_shape=[SDS((s, nh * HEAD), F32), SDS((nh, s // CHUNK, HEAD, HEAD), F32)],
        scratch_shapes=[pltpu.VMEM((nh, HEAD, HEAD), F32)],
        compiler_params=_params("arbitrary"), name=name)(qkvn, qkvn, qkvn, bg, gct)


def _delta_bwd(name, qkvn, bg, gct, do, states, nh, reverse, add=None, tb=128):
    s = qkvn.shape[0]
    tb = _tile(s, tb, LANES)
    nb, cpb = s // tb, tb // CHUNK
    qkv, rows_spec, bg_spec, gct_spec, st = _delta_specs(nh, tb, nb, not reverse)
    n_add = 0 if add is None else 3

    def body(*refs):
        q_ref, k_ref, v_ref, bg_ref, gct_ref, do_ref, st_ref = refs[0:7]
        add_refs = refs[7:7 + n_add]
        dq_ref, dk_ref, dv_ref, dbg_ref, dstate = refs[7 + n_add:]

        @pl.when(pl.program_id(0) == 0)
        def _():
            dstate[...] = jnp.zeros_like(dstate)

        ones = jnp.ones((nh, CHUNK, HEAD), BF16)
        row_id = lax.broadcasted_iota(jnp.int32, (CHUNK, 1), 0)
        rsum = lambda x: jnp.sum(x, axis=2, keepdims=True)
        for cj in (range(cpb) if reverse else range(cpb - 1, -1, -1)):
            rows = pl.ds(cj * CHUNK, CHUNK)
            q, k, v, d_o = (_heads(r, rows, nh) for r in (q_ref, k_ref, v_ref, do_ref))
            beta, gc, g_row, g_last = _chunk_scalars(bg_ref, gct_ref, cj, nh, tb, reverse)
            tm = _chunk_terms(q, k, v, beta, gc, g_row, g_last, reverse)
            incl, strict, e, ed, el, kb = tm["incl"], tm["strict"], tm["e"], tm["ed"], tm["el"], tm["kb"]
            t_mat, u, w, p_mat, decay = tm["t"], tm["u"], tm["w"], tm["p"], tm["decay"]
            s_in, ds_out = st_ref[:, cj], dstate[...]
            vn = u - _bnn(w, s_in)
            qe, kd, ke = q * e, k * ed, kb * e
            dvn = _btn(p_mat, d_o) + _bnn(kd, ds_out)
            dqe = _bnt(d_o, s_in)
            dq = dqe * e
            dgc = rsum(dqe * qe)
            dp = jnp.where(incl, _bnt(d_o, vn), 0.0)
            dkd = _bnt(vn, ds_out)
            dk = dkd * ed
            r = rsum(dkd * kd)
            dgc = dgc - r
            dg_last = (jnp.sum(r, axis=1, keepdims=True)
                       + jnp.sum(rsum(ds_out * s_in), axis=1, keepdims=True) * el)
            dw = -_bnt(dvn, s_in)
            dbv = _btn(t_mat, dvn)
            dke = _btn(t_mat, dw)
            da = -jnp.where(strict, _bnt(dbv, u) + _bnt(dke, w), 0.0)
            m_mat, n_mat = da * decay, dp * decay
            dkb = _bnn(m_mat, k) + dke * e
            dk = dk + _btn(m_mat, kb) + _btn(n_mat, q)
            dq = dq + _bnn(n_mat, k)
            g_mat = da * tm["a"] + dp * p_mat
            g_hi, g_mid, g_lo = _split3(g_mat)
            col = (_btn(g_hi, ones) + _btn(g_mid, ones) + _btn(g_lo, ones))[:, :, 0:1]
            dgc = dgc + rsum(g_mat) - col + rsum(dke * ke)
            dgc = dgc + jnp.where(row_id == (0 if reverse else CHUNK - 1), dg_last, 0.0)
            dv = dbv * beta
            dbeta = rsum(dbv * v) + rsum(dkb * k)
            dk = dk + dkb * beta
            dstate[...] = el * ds_out + _btn(qe, d_o) - _btn(w, dvn)
            lb = nh if reverse else 0
            for hd in range(nh):
                cols = slice(hd * HEAD, (hd + 1) * HEAD)
                extra = [a[rows, cols] for a in add_refs] if n_add else [0.0, 0.0, 0.0]
                dq_ref[rows, cols] = dq[hd] + extra[0]
                dk_ref[rows, cols] = dk[hd] + extra[1]
                dv_ref[rows, cols] = dv[hd] + extra[2]
                dbg_ref[rows, lb + hd:lb + hd + 1] = dbeta[hd]
                dbg_ref[rows, 2 * nh + lb + hd:2 * nh + lb + hd + 1] = dgc[hd]

    out3 = SDS((s, nh * HEAD), F32)
    return pl.pallas_call(
        body, grid=(nb,), in_specs=qkv + [bg_spec, gct_spec, rows_spec, st] + [rows_spec] * n_add,
        out_specs=[rows_spec, rows_spec, rows_spec, bg_spec], out_shape=[out3, out3, out3, SDS((s, LANES), F32)],
        scratch_shapes=[pltpu.VMEM((nh, HEAD, HEAD), F32)],
        compiler_params=_params("arbitrary"), name=name)(qkvn, qkvn, qkvn, bg, gct, do, states, *(add or ()))


def _ffn_fwd(tag, h, nw, sh, sc, g, w_up, w_down):
    u = _norm_mod(tag + "_norm", h, nw, sh, sc)
    ab = _matmul(tag + "_up", u, w_up, "nn", out_dtype=BF16, tn=2816, tm=512)
    hm = _swiglu(tag + "_act", ab)
    f = _matmul(tag + "_down", hm, w_down, "nn")
    return _resid(tag + "_res", h, f, g, 0.5), (h, u, ab, hm, f)


def _ffn_bwd(tag, dh, saved, nw, sc, g, w_up, w_down):
    h, u, ab, hm, f = saved
    df, acc_g = _resid_bwd(tag + "_res_bwd", dh, f, g, 0.5)
    gw_down = _matmul(tag + "_gw_down", hm, df, "tn", tm=1408)
    dhm = _matmul(tag + "_dhm", df, w_down, "nt", out_dtype=BF16, tn=2816)
    dab = _swiglu_bwd(tag + "_act_bwd", ab, dhm)
    gw_up = _matmul(tag + "_gw_up", u, dab, "tn", tn=1408)
    du = _matmul(tag + "_du", dab, w_up, "nt")
    dh_in, acc = _norm_mod_bwd(tag + "_norm_bwd", h, du, dh, nw, sc)
    return dh_in, gw_up, gw_down, (acc[0], acc[1], acc_g[0], acc[2])


def _mixer_fwd(h, nw, sh, sc, g, wt, lay):
    nh = lay.nh
    u = _norm_mod("mix_norm", h, nw, sh, sc)
    proj = _matmul("mix_in", u, wt["w_in"], "nn")
    qkvn = _prep_b("mix_prep_b", proj, wt["conv_dn"], lay)
    ya = _prep_a("mix_prep_a", proj, wt["conv_a"], lay)
    bg = _prep_c("mix_prep_c", proj, wt["pvec"], lay)
    gct = bg[:, 4 * nh:6 * nh].T
    o_f, st_f = _delta_fwd("delta_fwd_l2r", qkvn, bg, gct, nh, False)
    o_b, st_b = _delta_fwd("delta_fwd_r2l", qkvn, bg, gct, nh, True)
    yb = _post("mix_post", o_f, o_b, proj, wt["dn_norm"], lay)
    pa = _matmul("mix_a_out", ya, wt["w_a_out"], "nn")
    pb = _matmul("mix_b_out", yb, wt["w_b_out"], "nn")
    mg = _merge("mix_merge", pa, pb, proj, lay)
    y = _matmul("mix_out", mg, wt["w_out"], "nn")
    h2 = _resid("mix_res", h, y, g, 1.0)
    return h2, (h, u, proj, qkvn, ya, bg, gct, o_f, o_b, st_f, st_b, yb, pa, pb, mg, y)


def _mixer_bwd(dh, saved, nw, sc, g, wt, lay):
    h, u, proj, qkvn, ya, bg, gct, o_f, o_b, st_f, st_b, yb, pa, pb, mg, y = saved
    nh = lay.nh
    dy, acc_g = _resid_bwd("mix_res_bwd", dh, y, g, 1.0)
    gw_out = _matmul("mix_gw_out", mg, dy, "tn")
    dmg = _matmul("mix_dmg", dy, wt["w_out"], "nt")
    dpa, dpb, dproj = _merge_bwd("mix_merge_bwd", dmg, pa, pb, proj, lay)
    gw_a = _matmul("mix_gw_a", ya, dpa, "tn")
    gw_b = _matmul("mix_gw_b", yb, dpb, "tn")
    dya = _matmul("mix_dya", dpa, wt["w_a_out"], "nt")
    dyb = _matmul("mix_dyb", dpb, wt["w_b_out"], "nt")
    do, dproj, acc_dn = _post_bwd("mix_post_bwd", dyb, o_f, o_b, proj, wt["dn_norm"], dproj, lay)
    dq, dk, dv, dbg_f = _delta_bwd("delta_bwd_l2r", qkvn, bg, gct, do, st_f, nh, False)
    dq, dk, dv, dbg_b = _delta_bwd("delta_bwd_r2l", qkvn, bg, gct, do, st_b, nh, True, add=(dq, dk, dv))
    dproj, acc_ca = _prep_a_bwd("mix_prep_a_bwd", dya, proj, wt["conv_a"], dproj, lay)
    dproj, acc_cd = _prep_b_bwd("mix_prep_b_bwd", dq, dk, dv, proj, wt["conv_dn"], dproj, lay)
    dproj, acc_pc = _prep_c_bwd("mix_prep_c_bwd", dbg_f, dbg_b, proj, wt["pvec"], dproj, lay)
    gw_in = _matmul("mix_gw_in", u, dproj, "tn")
    du = _matmul("mix_du", dproj, wt["w_in"], "nt")
    dh_in, acc = _norm_mod_bwd("mix_norm_bwd", h, du, dh, nw, sc)
    small = dict(conv_a=acc_ca[0:wt["conv_a"].shape[0]], conv_dn=acc_cd[0:wt["conv_dn"].shape[0]],
                 dn_norm=acc_dn[0:1], a_log=acc_pc[0], dt_bias=acc_pc[1])
    return dh_in, dict(w_in=gw_in, w_a_out=gw_a, w_b_out=gw_b, w_out=gw_out), small, (acc[0], acc[1], acc_g[0], acc[2])


def _local_step(x, tgt, modv, wt, lay):
    m = [modv[i:i + 1] for i in range(9)]
    h1, sv1 = _ffn_fwd("ffn1", x, wt["norm_ffn1"], m[0], m[1], m[2], wt["w_ffn1_up"], wt["w_ffn1_down"])
    h2, sv2 = _mixer_fwd(h1, wt["norm_mix"], m[3], m[4], m[5], wt, lay)
    h3, sv3 = _ffn_fwd("ffn2", h2, wt["norm_ffn2"], m[6], m[7], m[8], wt["w_ffn2_up"], wt["w_ffn2_down"])
    dh3, acc_f = _final_loss("final_loss", h3, tgt, wt["norm_final"])
    loss = jnp.sum(acc_f[1])
    dh2, gu2, gd2, dm3 = _ffn_bwd("ffn2", dh3, sv3, wt["norm_ffn2"], m[7], m[8], wt["w_ffn2_up"], wt["w_ffn2_down"])
    dh1, gmix, small, dm2 = _mixer_bwd(dh2, sv2, wt["norm_mix"], m[4], m[5], wt, lay)
    dx, gu1, gd1, dm1 = _ffn_bwd("ffn1", dh1, sv1, wt["norm_ffn1"], m[1], m[2], wt["w_ffn1_up"], wt["w_ffn1_down"])
    dmod = jnp.stack([dm1[0], dm1[1], dm1[2], dm2[0], dm2[1], dm2[2], dm3[0], dm3[1], dm3[2]])
    big = dict(w_ffn1_up=gu1, w_ffn1_down=gd1, w_ffn2_up=gu2, w_ffn2_down=gd2, **gmix)
    small = dict(small, norm_ffn1=dm1[3], norm_mix=dm2[3], norm_ffn2=dm3[3], norm_final=acc_f[0])
    return loss, dx, dmod, big, small


def _position():
    return lax.axis_index("x"), lax.axis_index("y"), lax.axis_index("c")


_ANY = pl.BlockSpec(memory_space=pl.ANY)
_VMEM = pl.BlockSpec(memory_space=pltpu.VMEM)


def _allgather8(name, v):
    r = v.shape[0]

    def body(v_ref, out_ref, send_sems, recv_sems):
        x, y, c = _position()
        me = 4 * x + 2 * y + c
        out_ref[me] = v_ref[...]
        copies = []
        for mask in range(1, N_DEV):
            peer = tuple(1 - p if mask >> b & 1 else p for p, b in ((x, 2), (y, 1), (c, 0)))
            cp = pltpu.make_async_remote_copy(
                src_ref=v_ref, dst_ref=out_ref.at[me], send_sem=send_sems.at[mask - 1],
                recv_sem=recv_sems.at[mask - 1], device_id=peer, device_id_type=MESH)
            cp.start()
            copies.append(cp)
        for cp in copies:
            cp.wait()

    return pl.pallas_call(
        body, in_specs=[_VMEM], out_specs=_VMEM, out_shape=SDS((N_DEV, r, LANES), F32),
        scratch_shapes=[pltpu.SemaphoreType.DMA((N_DEV - 1,)), pltpu.SemaphoreType.DMA((N_DEV - 1,))],
        name=name)(v)


def _other_chips(x, y):
    return [(1 - x, y), (x, 1 - y), (1 - x, 1 - y)]


def _allgather_weights(name, pack):
    r, w = pack.shape
    hh = r // 2

    def body(p_ref, out_ref, send_sems, recv_sems, local_sem):
        x, y, c = _position()
        p = 2 * x + y
        half = pl.ds(pl.multiple_of(c * hh, 16), hh)
        other = pl.ds(pl.multiple_of((1 - c) * hh, 16), hh)
        chips = _other_chips(x, y)

        def copy(k, chip_index, rows, to, src=None):
            dst = out_ref.at[chip_index, rows]
            return pltpu.make_async_remote_copy(
                src_ref=dst if src is None else src, dst_ref=dst, send_sem=send_sems.at[k],
                recv_sem=recv_sems.at[k], device_id=to, device_id_type=MESH)

        mine = pltpu.make_async_copy(p_ref, out_ref.at[p], local_sem)
        mine.start()
        first = [copy(j, p, half, (cx, cy, c), src=p_ref.at[half]) for j, (cx, cy) in enumerate(chips)]
        for cp in first:
            cp.start()
        passed = []
        for j, (cx, cy) in enumerate(chips):
            copy(j, 2 * cx + cy, half, (x, y, c)).wait_recv()
            cp = copy(3 + j, 2 * cx + cy, half, (x, y, 1 - c))
            cp.start()
            passed.append(cp)
        for j, (cx, cy) in enumerate(chips):
            copy(3 + j, 2 * cx + cy, other, (x, y, c)).wait_recv()
        for cp in first + passed:
            cp.wait_send()
        mine.wait()

    return pl.pallas_call(
        body, in_specs=[_ANY], out_specs=_ANY, out_shape=SDS((N_CHIPS, r, w), pack.dtype),
        scratch_shapes=[pltpu.SemaphoreType.DMA((6,)), pltpu.SemaphoreType.DMA((6,)), pltpu.SemaphoreType.DMA],
        name=name)(pack)


def _swap_sibling(name, v):
    def body(v_ref, out_ref, send_sem, recv_sem):
        x, y, c = _position()
        cp = pltpu.make_async_remote_copy(src_ref=v_ref, dst_ref=out_ref, send_sem=send_sem, recv_sem=recv_sem,
                                          device_id=(x, y, 1 - c), device_id_type=MESH)
        cp.start()
        cp.wait()

    return pl.pallas_call(
        body, in_specs=[_ANY], out_specs=_ANY, out_shape=SDS(v.shape, v.dtype),
        scratch_shapes=[pltpu.SemaphoreType.DMA, pltpu.SemaphoreType.DMA], name=name)(v)


def _scatter_chips(name, v):
    _, r, w = v.shape

    def body(v_ref, out_ref, send_sems, recv_sems):
        x, y, c = _position()
        copies = []
        for j, (cx, cy) in enumerate(_other_chips(x, y)):
            cp = pltpu.make_async_remote_copy(
                src_ref=v_ref.at[2 * cx + cy], dst_ref=out_ref.at[j], send_sem=send_sems.at[j],
                recv_sem=recv_sems.at[j], device_id=(cx, cy, c), device_id_type=MESH)
            cp.start()
            copies.append(cp)
        for cp in copies:
            cp.wait()

    return pl.pallas_call(
        body, in_specs=[_ANY], out_specs=_ANY, out_shape=SDS((N_CHIPS - 1, r, w), v.dtype),
        scratch_shapes=[pltpu.SemaphoreType.DMA((3,)), pltpu.SemaphoreType.DMA((3,))], name=name)(v)


def _add2(name, a, b, tr=512):
    r, w = a.shape
    tr = _tile(r, tr, 16)

    def body(a_ref, b_ref, o_ref):
        o_ref[...] = (a_ref[...].astype(F32) + b_ref[...].astype(F32)).astype(o_ref.dtype)

    return pl.pallas_call(body, grid=(r // tr,), in_specs=[_row(tr, w), _row(tr, w)], out_specs=_row(tr, w),
                          out_shape=SDS((r, w), a.dtype), compiler_params=_params("parallel"), name=name)(a, b)


def _add5(name, a, b, rc, tr=512):
    r, w = a.shape
    tr = _tile(r, tr, 16)
    three = pl.BlockSpec((N_CHIPS - 1, tr, w), lambda i: (0, i, 0))

    def body(a_ref, b_ref, rc_ref, o_ref):
        acc = a_ref[...].astype(F32) + b_ref[...].astype(F32)
        for j in range(N_CHIPS - 1):
            acc = acc + rc_ref[j].astype(F32)
        o_ref[...] = acc

    return pl.pallas_call(body, grid=(r // tr,), in_specs=[_row(tr, w), _row(tr, w), three], out_specs=_row(tr, w),
                          out_shape=SDS((r, w), F32), compiler_params=_params("parallel"), name=name)(a, b, rc)


def _sum8(name, v):
    _, r, w = v.shape

    def body(v_ref, o_ref):
        acc = v_ref[0]
        for j in range(1, N_DEV):
            acc = acc + v_ref[j]
        o_ref[...] = acc

    return pl.pallas_call(body, in_specs=[_VMEM], out_specs=_VMEM, out_shape=SDS((r, w), F32), name=name)(v)


def _adamw(name, w, g, m, v, tr=256):
    r, cdim = w.shape
    tr = _tile(r, tr, 8)
    bc1, bc2 = 1.0 - ADAM_B1 ** ADAM_STEP, 1.0 - ADAM_B2 ** ADAM_STEP

    def body(w_ref, g_ref, m_ref, v_ref, d_ref, nm_ref, nv_ref):
        g = g_ref[...]
        m2 = ADAM_B1 * m_ref[...] + (1.0 - ADAM_B1) * g
        v2 = ADAM_B2 * v_ref[...] + (1.0 - ADAM_B2) * (g * g)
        d_ref[...] = -ADAM_LR * ((m2 / bc1) / (jnp.sqrt(v2 / bc2) + ADAM_EPS) + ADAM_WD * w_ref[...])
        nm_ref[...] = m2
        nv_ref[...] = v2

    spec = _row(tr, cdim)
    out = SDS((r, cdim), F32)
    return pl.pallas_call(body, grid=(r // tr,), in_specs=[spec] * 4, out_specs=[spec] * 3, out_shape=[out] * 3,
                          compiler_params=_params("parallel"), name=name)(w, g, m, v)


def _pack_rows(arrays, width, row_mult, dtype):
    parts, spans, row = [], [], 0
    for a in arrays:
        n = a.size
        rows = -(-n // width)
        flat = a.reshape(-1).astype(dtype)
        if rows * width != n:
            flat = jnp.concatenate([flat, jnp.zeros((rows * width - n,), dtype)])
        parts.append(flat.reshape(rows, width))
        spans.append((row, rows, n, a.shape))
        row += rows
    pad = -row % row_mult
    if pad:
        parts.append(jnp.zeros((pad, width), dtype))
    return jnp.concatenate(parts, axis=0), spans


def _unpack_rows(packed, spans):
    return [packed[r0:r0 + rows].reshape(-1)[0:n].reshape(shape) for r0, rows, n, shape in spans]


BIG = ("w_ffn1_up", "w_ffn1_down", "w_in", "w_a_out", "w_b_out", "w_out", "w_ffn2_up", "w_ffn2_down")
COL_SHARDED = ("w_ffn1_up", "w_in", "w_a_out", "w_ffn2_up")
SMALL = ("b_ada", "norm_ffn1", "norm_mix", "conv_a", "conv_dn", "a_log_fwd", "dt_bias_fwd", "a_log_bwd",
         "dt_bias_bwd", "dn_norm", "norm_ffn2", "norm_final")
WEIGHTS = ("w_ada", "b_ada", "norm_ffn1", "w_ffn1_up", "w_ffn1_down", "norm_mix", "w_in", "conv_a", "conv_dn",
           "a_log_fwd", "dt_bias_fwd", "a_log_bwd", "dt_bias_bwd", "dn_norm", "w_a_out", "w_b_out", "w_out",
           "norm_ffn2", "w_ffn2_up", "w_ffn2_down", "norm_final")
PACK_W = 1024
PACK_ROW_MULT = 32


def _join_shards(name, pieces):
    return jnp.concatenate(pieces, axis=1 if name in COL_SHARDED else 0)


def _split_shards(name, full):
    return jnp.split(full, N_CHIPS, axis=1 if name in COL_SHARDED else 0)


def kernel(x, c, w_ada, b_ada, norm_ffn1, w_ffn1_up, w_ffn1_down, norm_mix, w_in, conv_a, conv_dn, a_log_fwd, dt_bias_fwd, a_log_bwd, dt_bias_bwd, dn_norm, w_a_out, w_b_out, w_out, norm_ffn2, w_ffn2_up, w_ffn2_down, norm_final, loss_target, m_w_ada, m_b_ada, m_norm_ffn1, m_w_ffn1_up, m_w_ffn1_down, m_norm_mix, m_w_in, m_conv_a, m_conv_dn, m_a_log_fwd, m_dt_bias_fwd, m_a_log_bwd, m_dt_bias_bwd, m_dn_norm, m_w_a_out, m_w_b_out, m_w_out, m_norm_ffn2, m_w_ffn2_up, m_w_ffn2_down, m_norm_final, v_w_ada, v_b_ada, v_norm_ffn1, v_w_ffn1_up, v_w_ffn1_down, v_norm_mix, v_w_in, v_conv_a, v_conv_dn, v_a_log_fwd, v_dt_bias_fwd, v_a_log_bwd, v_dt_bias_bwd, v_dn_norm, v_w_a_out, v_w_b_out, v_w_out, v_norm_ffn2, v_w_ffn2_up, v_w_ffn2_down, v_norm_final):
    given = dict(locals())
    wsh = {n: given[n] for n in WEIGHTS}
    msh = {n: given["m_" + n] for n in WEIGHTS}
    vsh = {n: given["v_" + n] for n in WEIGHTS}
    d = x.shape[-1]
    ca = conv_a.shape[-1] * N_CHIPS
    nh = conv_dn.shape[-1] * N_CHIPS // (3 * HEAD)
    lay = _Layout(d, ca, nh)
    xi, yi, ci = _position()
    chip = 2 * xi + yi
    me = 2 * chip + ci

    c_act = jax.nn.silu(c)
    g1, g1_spans = _pack_rows([c_act, conv_a[0], conv_dn[0]], LANES, 8, F32)
    g1_all = _allgather8("gather_cond", g1)
    per_dev = [_unpack_rows(g1_all[k], g1_spans) for k in range(N_DEV)]
    c_all = jnp.concatenate([p[0] for p in per_dev], axis=0)
    conv_a_full = jnp.concatenate([per_dev[2 * k][1] for k in range(N_CHIPS)], axis=1)
    conv_dn_full = jnp.concatenate([per_dev[2 * k][2] for k in range(N_CHIPS)], axis=1)

    mod_sh = _matmul("ada_mod", c_all, w_ada[0], "nn")
    b_sh = lax.dynamic_slice_in_dim(b_ada, chip * mod_sh.shape[1], mod_sh.shape[1], axis=1)
    g2, g2_spans = _pack_rows([mod_sh + b_sh], LANES, 8, F32)
    g2_all = _allgather8("gather_mod", g2)
    mod_all = jnp.concatenate([_unpack_rows(g2_all[2 * k], g2_spans)[0] for k in range(N_CHIPS)], axis=1)
    modv = lax.dynamic_index_in_dim(mod_all, me, 0, keepdims=False).reshape(9, d)

    wpack, w_spans = _pack_rows([wsh[n][0] for n in BIG], PACK_W, PACK_ROW_MULT, BF16)
    wall = _allgather_weights("gather_weights", wpack)
    shards = [_unpack_rows(wall[k], w_spans) for k in range(N_CHIPS)]
    wt = {n: _join_shards(n, [shards[k][i] for k in range(N_CHIPS)]) for i, n in enumerate(BIG)}
    wt["w_in"] = lay.perm_cols(wt["w_in"])
    lane_pad = (jnp.zeros((2 * nh,), F32), jnp.zeros((LANES - 4 * nh,), F32))
    pvec = jnp.stack([jnp.concatenate([lane_pad[0], a_log_fwd[0], a_log_bwd[0], lane_pad[1]]),
                      jnp.concatenate([lane_pad[0], dt_bias_fwd[0], dt_bias_bwd[0], lane_pad[1]])]
                     + [jnp.zeros((LANES,), F32)] * 6)
    wt.update(conv_a=conv_a_full, conv_dn=conv_dn_full, pvec=pvec, dn_norm=dn_norm, norm_ffn1=norm_ffn1,
              norm_mix=norm_mix, norm_ffn2=norm_ffn2, norm_final=norm_final.reshape(1, d))

    loss, dx, dmod, big, small = _local_step(x[0], loss_target[0], modv, wt, lay)
    loss = lax.psum(loss, ("x", "y", "c"))
    big["w_in"] = lay.unperm_cols(big["w_in"])

    small_list = [dmod.reshape(1, 9 * d), small["norm_ffn1"], small["norm_mix"], small["conv_a"], small["conv_dn"],
                  small["a_log"][2 * nh:3 * nh], small["dt_bias"][2 * nh:3 * nh], small["a_log"][3 * nh:4 * nh],
                  small["dt_bias"][3 * nh:4 * nh], small["dn_norm"], small["norm_ffn2"], small["norm_final"]]
    g3, g3_spans = _pack_rows(small_list, LANES, 8, F32)
    g3_all = _allgather8("gather_small_grads", g3)
    g_small = dict(zip(SMALL, _unpack_rows(_sum8("sum_small_grads", g3_all), g3_spans)))
    dmod_all = jnp.concatenate([_unpack_rows(g3_all[k], g3_spans)[0] for k in range(N_DEV)], axis=0)
    ncol = w_ada.shape[-1]
    dmod_sh = lax.dynamic_slice_in_dim(dmod_all, chip * ncol, ncol, axis=1)
    grads = {"w_ada": _matmul("ada_grad", c_all, dmod_sh, "tn")[None]}
    for n in SMALL:
        g = g_small[n]
        if n in ("conv_a", "conv_dn"):
            wloc = wsh[n].shape[-1]
            g = lax.dynamic_slice_in_dim(g, chip * wloc, wloc, axis=1)
        grads[n] = g.reshape(wsh[n].shape)

    packs = []
    for k in range(N_CHIPS):
        pk, g_spans = _pack_rows([_split_shards(n, big[n])[k] for n in BIG], PACK_W, PACK_ROW_MULT, BF16)
        packs.append(pk)
    rp = packs[0].shape[0]
    gpack = jnp.stack(packs).reshape(N_CHIPS, 2, rp // 2, PACK_W)
    mine = lax.dynamic_index_in_dim(gpack, ci, 1, keepdims=False)
    theirs = lax.dynamic_index_in_dim(gpack, 1 - ci, 1, keepdims=False)
    from_sib = _swap_sibling("grads_to_sibling", theirs)
    chip_sum = _add2("grads_chip_sum", mine.reshape(-1, PACK_W), from_sib.reshape(-1, PACK_W))
    from_chips = _scatter_chips("grads_to_chips", chip_sum.reshape(N_CHIPS, rp // 2, PACK_W))
    own_a = lax.dynamic_index_in_dim(mine, chip, 0, keepdims=False)
    own_b = lax.dynamic_index_in_dim(from_sib, chip, 0, keepdims=False)
    my_half = _add5("grads_total", own_a, own_b, from_chips)
    sib_half = _swap_sibling("grads_share_total", my_half)
    halves = jnp.stack([my_half, sib_half])
    lo = lax.dynamic_index_in_dim(halves, ci, 0, keepdims=False)
    hi = lax.dynamic_index_in_dim(halves, 1 - ci, 0, keepdims=False)
    for n, g in zip(BIG, _unpack_rows(jnp.concatenate([lo, hi], axis=0), g_spans)):
        grads[n] = g[None]

    delta, new_m, new_v = {}, {}, {}
    for n in ("w_ada",) + BIG:
        shp = wsh[n].shape
        outs = _adamw("adamw_" + n, *(t.reshape(shp[-2], shp[-1]) for t in (wsh[n], grads[n], msh[n], vsh[n])))
        delta[n], new_m[n], new_v[n] = (o.reshape(shp) for o in outs)
    packed = []
    for src in (wsh, grads, msh, vsh):
        pk, s_spans = _pack_rows([src[n] for n in SMALL], LANES, 8, F32)
        packed.append(pk)
    outs = _adamw("adamw_small", *packed)
    for dst, o in zip((delta, new_m, new_v), outs):
        dst.update(zip(SMALL, _unpack_rows(o, s_spans)))

    return (loss, dx[None], *[grads[n] for n in WEIGHTS], *[delta[n] for n in WEIGHTS],
            *[new_m[n] for n in WEIGHTS], *[new_v[n] for n in WEIGHTS])
```

```python
import functools

import jax
import jax.numpy as jnp
from jax import lax
from jax.experimental import pallas as pl
from jax.experimental.pallas import tpu as pltpu

F32 = jnp.float32
BF16 = jnp.bfloat16
SDS = jax.ShapeDtypeStruct
MESH = pl.DeviceIdType.MESH
HI = lax.Precision.HIGHEST

EPS = 1e-6
HEAD = 128
CHUNK = 64
LANES = 128
N_CHIPS = 4
N_DEV = 8
VMEM_LIMIT = 56 * 1024 * 1024

ADAM_LR = 0.001
ADAM_B1 = 0.9
ADAM_B2 = 0.999
ADAM_EPS = 1e-08
ADAM_WD = 0.01
ADAM_STEP = 10


def _params(*sem):
    return pltpu.CompilerParams(dimension_semantics=sem, vmem_limit_bytes=VMEM_LIMIT)


def _tile(n, cap, mult=LANES):
    t = min(n, cap) // mult * mult
    while t >= mult:
        if n % t == 0:
            return t
        t -= mult
    return n


def _row(tr, w, cb=0):
    return pl.BlockSpec((tr, w), lambda i: (i, cb))


def _vec(r, w):
    return pl.BlockSpec((r, w), lambda i: (0, 0))


def _nn(a, b, **kw):
    return jnp.dot(a, b, preferred_element_type=F32, **kw)


def _nt(a, b, **kw):
    return lax.dot_general(a, b, (((1,), (1,)), ((), ())), preferred_element_type=F32, **kw)


def _tn(a, b, **kw):
    return lax.dot_general(a, b, (((0,), (0,)), ((), ())), preferred_element_type=F32, **kw)


def _bnn(a, b):
    return lax.dot_general(a, b, (((2,), (1,)), ((0,), (0,))), preferred_element_type=F32)


def _bnt(a, b):
    return lax.dot_general(a, b, (((2,), (2,)), ((0,), (0,))), preferred_element_type=F32)


def _btn(a, b):
    return lax.dot_general(a, b, (((1,), (1,)), ((0,), (0,))), preferred_element_type=F32)


def _silu_grad(x):
    s = jax.nn.sigmoid(x)
    return s * (1.0 + x * (1.0 - s))


def _matmul(name, a, b, mode, out_dtype=F32, tm=1024, tn=1024, tk=2048, full_k=2816, out_pieces=0):
    pieces_b = b.shape[0] if b.ndim == 3 else 0
    b2 = b.shape[1:] if pieces_b else b.shape
    if mode == "nn":
        (m, k), n = a.shape, b2[1] * max(pieces_b, 1)
    elif mode == "nt":
        (m, _), n, k = a.shape, b2[0], b2[1] * max(pieces_b, 1)
    else:
        (k, m), n = a.shape, b2[1] * max(pieces_b, 1)
    n_unit = n // max(out_pieces, 1) if mode == "nt" or not pieces_b else n // pieces_b
    if out_pieces and pieces_b and mode != "nt":
        assert out_pieces == pieces_b
    k_unit = k // pieces_b if (pieces_b and mode == "nt") else k
    tm, tn = _tile(m, tm), _tile(n_unit, tn)
    tk = k_unit if k_unit <= full_k else _tile(k_unit, tk)
    nk = k // tk
    n_per, k_per = n_unit // tn, k_unit // tk
    a_bytes, b_bytes = a.size * a.dtype.itemsize, b.size * b.dtype.itemsize
    j_outer = nk == 1 and b_bytes + a_bytes * (n // tn) < a_bytes + b_bytes * (m // tm)
    ij = (lambda g0, g1: (g1, g0)) if j_outer else (lambda g0, g1: (g0, g1))

    def spec(shape, pick):
        return pl.BlockSpec(shape, lambda g0, g1, l: pick(*ij(g0, g1), l))

    a_spec = {"nn": spec((tm, tk), lambda i, j, l: (i, l)), "nt": spec((tm, tk), lambda i, j, l: (i, l)),
              "tn": spec((tk, tm), lambda i, j, l: (l, i))}[mode]
    if not pieces_b:
        b_spec = {"nn": spec((tk, tn), lambda i, j, l: (l, j)), "nt": spec((tn, tk), lambda i, j, l: (j, l)),
                  "tn": spec((tk, tn), lambda i, j, l: (l, j))}[mode]
    elif mode == "nt":
        b_spec = spec((None, tn, tk), lambda i, j, l: (l // k_per, j, l % k_per))
    else:
        b_spec = spec((None, tk, tn), lambda i, j, l: (j // n_per, l, j % n_per))
    if out_pieces:
        o_spec = spec((None, tm, tn), lambda i, j, l: (j // n_per, i, j % n_per))
        o_shape = SDS((out_pieces, m, n // out_pieces), out_dtype)
    else:
        o_spec, o_shape = spec((tm, tn), lambda i, j, l: (i, j)), SDS((m, n), out_dtype)
    dot = {"nn": _nn, "nt": _nt, "tn": _tn}[mode]

    def body_one(a_ref, b_ref, o_ref):
        o_ref[...] = dot(a_ref[...].astype(BF16), b_ref[...].astype(BF16)).astype(o_ref.dtype)

    def body_acc(a_ref, b_ref, o_ref, acc):
        l = pl.program_id(2)
        part = dot(a_ref[...].astype(BF16), b_ref[...].astype(BF16))

        @pl.when(l == 0)
        def _():
            acc[...] = part

        @pl.when((l > 0) & (l < nk - 1))
        def _():
            acc[...] += part

        @pl.when(l == nk - 1)
        def _():
            o_ref[...] = (acc[...] + part).astype(o_ref.dtype)

    return pl.pallas_call(
        body_one if nk == 1 else body_acc, grid=(n // tn, m // tm, nk) if j_outer else (m // tm, n // tn, nk),
        in_specs=[a_spec, b_spec], out_specs=o_spec, out_shape=o_shape,
        scratch_shapes=[] if nk == 1 else [pltpu.VMEM((tm, tn), F32)],
        compiler_params=_params("parallel", "parallel", "arbitrary"), name=name)(a, b)


def _norm_mod(name, h, nw, sh, sc, tr=512):
    s, d = h.shape
    tr = _tile(s, tr, 8)

    def body(h_ref, nw_ref, sh_ref, sc_ref, u_ref):
        x = h_ref[...]
        r = lax.rsqrt(jnp.mean(x * x, axis=-1, keepdims=True) + EPS)
        u_ref[...] = (x * r * nw_ref[...] * (1.0 + sc_ref[...]) + sh_ref[...]).astype(BF16)

    return pl.pallas_call(
        body, grid=(s // tr,), in_specs=[_row(tr, d), _vec(1, d), _vec(1, d), _vec(1, d)],
        out_specs=_row(tr, d), out_shape=SDS((s, d), BF16),
        compiler_params=_params("parallel"), name=name)(h, nw, sh, sc)


def _norm_mod_bwd(name, h, du, dh, nw, sc, tr=512):
    s, d = h.shape
    tr = _tile(s, tr, 8)

    def body(h_ref, du_ref, dh_ref, nw_ref, sc_ref, o_ref, acc_ref):
        @pl.when(pl.program_id(0) == 0)
        def _():
            acc_ref[...] = jnp.zeros_like(acc_ref)

        x, g = h_ref[...], du_ref[...]
        r = lax.rsqrt(jnp.mean(x * x, axis=-1, keepdims=True) + EPS)
        n = x * r
        nw, sc1 = nw_ref[...], 1.0 + sc_ref[...]
        dn = g * sc1 * nw
        o_ref[...] = dh_ref[...] + r * (dn - n * jnp.mean(dn * n, axis=-1, keepdims=True))
        gn = g * n
        acc_ref[0:1, :] += jnp.sum(g, axis=0, keepdims=True)
        acc_ref[1:2, :] += jnp.sum(gn * nw, axis=0, keepdims=True)
        acc_ref[2:3, :] += jnp.sum(gn * sc1, axis=0, keepdims=True)

    return pl.pallas_call(
        body, grid=(s // tr,),
        in_specs=[_row(tr, d), _row(tr, d), _row(tr, d), _vec(1, d), _vec(1, d)],
        out_specs=[_row(tr, d), _vec(8, d)], out_shape=[SDS((s, d), F32), SDS((8, d), F32)],
        compiler_params=_params("arbitrary"), name=name)(h, du, dh, nw, sc)


def _swiglu(name, ab, tr=256):
    s, f2 = ab.shape
    f = f2 // 2
    tr = _tile(s, tr, 8)

    def body(a_ref, b_ref, o_ref):
        o_ref[...] = (jax.nn.silu(a_ref[...].astype(F32)) * b_ref[...].astype(F32)).astype(BF16)

    return pl.pallas_call(
        body, grid=(s // tr,), in_specs=[_row(tr, f, 0), _row(tr, f, 1)], out_specs=_row(tr, f),
        out_shape=SDS((s, f), BF16), compiler_params=_params("parallel"), name=name)(ab, ab)


def _swiglu_bwd(name, ab, dhm, tr=256):
    s, f2 = ab.shape
    f = f2 // 2
    tr = _tile(s, tr, 8)

    def body(a_ref, b_ref, d_ref, o_ref):
        a, d = a_ref[...].astype(F32), d_ref[...].astype(F32)
        o_ref[:, 0:f] = (d * b_ref[...].astype(F32) * _silu_grad(a)).astype(BF16)
        o_ref[:, f:f2] = (d * jax.nn.silu(a)).astype(BF16)

    return pl.pallas_call(
        body, grid=(s // tr,), in_specs=[_row(tr, f, 0), _row(tr, f, 1), _row(tr, f)],
        out_specs=_row(tr, f2), out_shape=SDS((s, f2), BF16),
        compiler_params=_params("parallel"), name=name)(ab, ab, dhm)


def _resid(name, h, f, g, scale, tr=512):
    s, d = h.shape
    tr = _tile(s, tr, 8)

    def body(h_ref, f_ref, g_ref, o_ref):
        o_ref[...] = h_ref[...] + (scale * g_ref[...]) * f_ref[...]

    return pl.pallas_call(
        body, grid=(s // tr,), in_specs=[_row(tr, d), _row(tr, d), _vec(1, d)], out_specs=_row(tr, d),
        out_shape=SDS((s, d), F32), compiler_params=_params("parallel"), name=name)(h, f, g)


def _resid_bwd(name, dh, f, g, scale, tr=512):
    s, d = dh.shape
    tr = _tile(s, tr, 8)

    def body(dh_ref, f_ref, g_ref, o_ref, acc_ref):
        @pl.when(pl.program_id(0) == 0)
        def _():
            acc_ref[...] = jnp.zeros_like(acc_ref)

        x = dh_ref[...]
        o_ref[...] = ((scale * g_ref[...]) * x).astype(BF16)
        acc_ref[0:1, :] += jnp.sum(scale * x * f_ref[...], axis=0, keepdims=True)

    return pl.pallas_call(
        body, grid=(s // tr,), in_specs=[_row(tr, d), _row(tr, d), _vec(1, d)],
        out_specs=[_row(tr, d), _vec(8, d)], out_shape=[SDS((s, d), BF16), SDS((8, d), F32)],
        compiler_params=_params("arbitrary"), name=name)(dh, f, g)


def _final_loss(name, h, tgt, nw, tr=512):
    s, d = h.shape
    tr = _tile(s, tr, 8)

    def body(h_ref, t_ref, nw_ref, o_ref, acc_ref):
        @pl.when(pl.program_id(0) == 0)
        def _():
            acc_ref[...] = jnp.zeros_like(acc_ref)

        x, nw = h_ref[...], nw_ref[...]
        r = lax.rsqrt(jnp.mean(x * x, axis=-1, keepdims=True) + EPS)
        n = x * r
        diff = n * nw - t_ref[...]
        dy = diff * (1.0 / d)
        dn = dy * nw
        o_ref[...] = r * (dn - n * jnp.mean(dn * n, axis=-1, keepdims=True))
        acc_ref[0:1, :] += jnp.sum(dy * n, axis=0, keepdims=True)
        acc_ref[1:2, :] += jnp.sum(diff * diff, axis=0, keepdims=True) * (0.5 / d)

    return pl.pallas_call(
        body, grid=(s // tr,), in_specs=[_row(tr, d), _row(tr, d), _vec(1, d)],
        out_specs=[_row(tr, d), _vec(8, d)], out_shape=[SDS((s, d), F32), SDS((8, d), F32)],
        compiler_params=_params("arbitrary"), name=name)(h, tgt, nw)


class _Layout:
    def __init__(self, d, ca, nh):
        self.d, self.ca, self.nh = d, ca, nh
        self.qk = nh * HEAD
        self.qkv = 3 * self.qk
        self.z = self.qkv
        self.ga = self.z + self.qk
        self.cab = self.ga + 2 * d
        self.ba = self.cab + 3 * ca
        self.tail = _tile(self.ba, 512)
        self.total = self.ba + self.tail
        assert self.qkv % self.qk == 0 and self.ga % (2 * d) == 0 and self.cab % (3 * ca) == 0
        assert self.ba % self.tail == 0 and 4 * nh <= LANES

    def perm_cols(self, w):
        ca, qkv, qk, d, nh = self.ca, self.qkv, self.qk, self.d, self.nh
        o = [0, ca, 2 * ca, 3 * ca, 3 * ca + qkv, 3 * ca + qkv + qk, 3 * ca + qkv + qk + 4 * nh]
        cb, cc, cv = (w[..., o[i]:o[i + 1]] for i in range(3))
        x_qkv, x_z, x_ba = w[..., o[3]:o[4]], w[..., o[4]:o[5]], w[..., o[5]:o[6]]
        gates = w[..., o[6]:o[6] + 2 * d]
        pad = jnp.zeros(w.shape[:-1] + (self.tail - 4 * nh,), w.dtype)
        return jnp.concatenate([x_qkv, x_z, gates, cb, cc, cv, x_ba, pad], axis=-1)

    def unperm_cols(self, w):
        ca, nh = self.ca, self.nh
        cb, cc, cv = (w[..., self.cab + i * ca:self.cab + (i + 1) * ca] for i in range(3))
        return jnp.concatenate([cb, cc, cv, w[..., 0:self.qkv], w[..., self.z:self.ga],
                                w[..., self.ba:self.ba + 4 * nh], w[..., self.ga:self.cab]], axis=-1)


def _halo_specs(tr, w, cb, s):
    nb8 = s // 8
    return [pl.BlockSpec((8, w), lambda i: (jnp.maximum(i * (tr // 8) - 1, 0), cb)),
            pl.BlockSpec((tr, w), lambda i: (i, cb)),
            pl.BlockSpec((8, w), lambda i: (jnp.minimum((i + 1) * (tr // 8), nb8 - 1), cb))]


def _ext(prev_ref, main_ref, next_ref, i, nt):
    p = jnp.where(i > 0, prev_ref[...].astype(F32), 0.0)
    n = jnp.where(i < nt - 1, next_ref[...].astype(F32), 0.0)
    return jnp.concatenate([p, main_ref[...].astype(F32), n], axis=0)


def _shift(x, k):
    return x if k == 0 else pltpu.roll(x, (-k) % x.shape[0], 0)


def _conv_taps(x_ext, w, tr):
    kt = w.shape[0]
    acc = None
    for t in range(kt):
        term = _shift(x_ext, t - kt // 2)[8:8 + tr] * w[t:t + 1, :]
        acc = term if acc is None else acc + term
    return acc


def _prep_a(name, proj, conv_a, lay, tr=256):
    s, ca = proj.shape[0], lay.ca
    tr = _tile(s, tr, 8)
    nt, w = s // tr, 3 * ca

    def body(p_ref, m_ref, n_ref, w_ref, o_ref):
        x = _ext(p_ref, m_ref, n_ref, pl.program_id(0), nt)
        xv = x[:, ca:2 * ca] * x[:, 2 * ca:w]
        y = _conv_taps(xv, w_ref[...], tr)
        o_ref[...] = (m_ref[:, 0:ca] * y).astype(BF16)

    return pl.pallas_call(
        body, grid=(nt,), in_specs=_halo_specs(tr, w, lay.cab // w, s) + [_vec(conv_a.shape[0], ca)],
        out_specs=_row(tr, ca), out_shape=SDS((s, ca), BF16),
        compiler_params=_params("parallel"), name=name)(proj, proj, proj, conv_a)


def _prep_a_bwd(name, dya, proj, conv_a, dproj, lay, tr=256):
    s, ca = proj.shape[0], lay.ca
    tr = _tile(s, tr, 8)
    nt, w, kt = s // tr, 3 * ca, conv_a.shape[0]

    def body(p_ref, m_ref, n_ref, dp_ref, dm_ref, dn_ref, w_ref, _, o_ref, acc_ref):
        i = pl.program_id(0)

        @pl.when(i == 0)
        def _():
            acc_ref[...] = jnp.zeros_like(acc_ref)

        x = _ext(p_ref, m_ref, n_ref, i, nt)
        d_ext = _ext(dp_ref, dm_ref, dn_ref, i, nt)
        cb, cc, cv = x[:, 0:ca], x[:, ca:2 * ca], x[:, 2 * ca:w]
        xv = cc * cv
        wv = w_ref[...]
        dy_ext = d_ext * cb
        dx = None
        for t in range(kt):
            term = _shift(dy_ext, kt // 2 - t)[8:8 + tr] * wv[t:t + 1, :]
            dx = term if dx is None else dx + term
            acc_ref[t:t + 1, :] += jnp.sum(dy_ext[8:8 + tr] * _shift(xv, t - kt // 2)[8:8 + tr],
                                           axis=0, keepdims=True)
        y = _conv_taps(xv, wv, tr)
        o_ref[:, 0:ca] = (dm_ref[...] * y).astype(BF16)
        o_ref[:, ca:2 * ca] = (dx * cv[8:8 + tr]).astype(BF16)
        o_ref[:, 2 * ca:w] = (dx * cc[8:8 + tr]).astype(BF16)

    return pl.pallas_call(
        body, grid=(nt,),
        in_specs=_halo_specs(tr, w, lay.cab // w, s) + _halo_specs(tr, ca, 0, s)
        + [_vec(kt, ca), pl.BlockSpec(memory_space=pl.ANY)],
        out_specs=[_row(tr, w, lay.cab // w), _vec(8, ca)],
        out_shape=[SDS(dproj.shape, dproj.dtype), SDS((8, ca), F32)], input_output_aliases={7: 0},
        compiler_params=_params("arbitrary"), name=name)(proj, proj, proj, dya, dya, dya, conv_a, dproj)


def _qkv_act(c, nh, tr_rows):
    sact = jax.nn.silu(c)
    outs, inv = [], []
    for hd in range(3 * nh):
        sl = sact[:, hd * HEAD:(hd + 1) * HEAD]
        if hd < 2 * nh:
            r = lax.rsqrt(jnp.sum(sl * sl, axis=-1, keepdims=True) + EPS)
            inv.append(r)
            outs.append(sl * (r * (HEAD ** -0.5 if hd < nh else 1.0)))
        else:
            outs.append(sl)
    return jnp.concatenate(outs, axis=-1), sact, inv


def _prep_b(name, proj, conv_dn, lay, tr=256):
    s, w, nh = proj.shape[0], lay.qkv, lay.nh
    tr = _tile(s, tr, 8)
    nt = s // tr

    def body(p_ref, m_ref, n_ref, w_ref, o_ref):
        x = _ext(p_ref, m_ref, n_ref, pl.program_id(0), nt)
        c = _conv_taps(x, w_ref[...], tr)
        o_ref[...] = _qkv_act(c, nh, tr)[0]

    return pl.pallas_call(
        body, grid=(nt,), in_specs=_halo_specs(tr, w, 0, s) + [_vec(conv_dn.shape[0], w)],
        out_specs=_row(tr, w), out_shape=SDS((s, w), F32),
        compiler_params=_params("parallel"), name=name)(proj, proj, proj, conv_dn)


def _prep_b_bwd(name, dq, dk, dv, proj, conv_dn, dproj, lay, tr=256):
    s, w, nh, qk = proj.shape[0], lay.qkv, lay.nh, lay.qk
    tr = _tile(s, tr, 8)
    nt, kt = s // tr, conv_dn.shape[0]
    n_ext = tr + 16

    def body(*refs):
        x_refs, g_refs = refs[0:3], refs[3:12]
        w_ref, o_ref, acc_ref = refs[12], refs[14], refs[15]
        i = pl.program_id(0)

        @pl.when(i == 0)
        def _():
            acc_ref[...] = jnp.zeros_like(acc_ref)

        x = _ext(*x_refs, i, nt)
        wv = w_ref[...]
        c = None
        for t in range(kt):
            term = _shift(x, t - kt // 2) * wv[t:t + 1, :]
            c = term if c is None else c + term
        sact = jax.nn.silu(c)
        ds = []
        for part in range(3):
            g = _ext(*g_refs[3 * part:3 * part + 3], i, nt)
            for hd in range(nh):
                sl = sact[:, part * qk + hd * HEAD:part * qk + (hd + 1) * HEAD]
                gh = g[:, hd * HEAD:(hd + 1) * HEAD]
                if part < 2:
                    r = lax.rsqrt(jnp.sum(sl * sl, axis=-1, keepdims=True) + EPS)
                    sc = HEAD ** -0.5 if part == 0 else 1.0
                    ds.append(sc * r * (gh - sl * (r * r) * jnp.sum(gh * sl, axis=-1, keepdims=True)))
                else:
                    ds.append(gh)
        dc = jnp.concatenate(ds, axis=-1) * _silu_grad(c)
        rows = lax.broadcasted_iota(jnp.int32, (n_ext, 1), 0)
        dc = jnp.where((rows >= 2) & (rows < n_ext - 2), dc, 0.0)
        dx = None
        for t in range(kt):
            term = _shift(dc, kt // 2 - t)[8:8 + tr] * wv[t:t + 1, :]
            dx = term if dx is None else dx + term
            acc_ref[t:t + 1, :] += jnp.sum(dc[8:8 + tr] * _shift(x, t - kt // 2)[8:8 + tr],
                                           axis=0, keepdims=True)
        o_ref[...] = dx.astype(BF16)

    return pl.pallas_call(
        body, grid=(nt,),
        in_specs=_halo_specs(tr, w, 0, s) + _halo_specs(tr, qk, 0, s) * 3
        + [_vec(kt, w), pl.BlockSpec(memory_space=pl.ANY)],
        out_specs=[_row(tr, w, 0), _vec(8, w)],
        out_shape=[SDS(dproj.shape, dproj.dtype), SDS((8, w), F32)], input_output_aliases={13: 0},
        compiler_params=_params("arbitrary"), name=name)(
            proj, proj, proj, dq, dq, dq, dk, dk, dk, dv, dv, dv, conv_dn, dproj)


def _softplus(x):
    return jnp.maximum(x, 0.0) + jnp.log(1.0 + jnp.exp(-jnp.abs(x)))


def _split3(x):
    hi = x.astype(BF16)
    r = x - hi.astype(F32)
    mid = r.astype(BF16)
    return hi, mid, (r - mid.astype(F32)).astype(BF16)


def _exact_nn(m, x):
    m = m.astype(BF16)
    hi, mid, lo = _split3(x)
    return _nn(m, hi) + _nn(m, mid) + _nn(m, lo)


def _chunk_cumsum_masks(tr):
    ri = lax.broadcasted_iota(jnp.int32, (tr, tr), 0)
    ci = lax.broadcasted_iota(jnp.int32, (tr, tr), 1)
    same = (ri // CHUNK) == (ci // CHUNK)
    return (same & (ci <= ri)).astype(F32), (same & (ci >= ri)).astype(F32)


def _prep_c(name, proj, pvec, lay, tr=512):
    s, nh = proj.shape[0], lay.nh
    tr = _tile(s, tr, CHUNK)
    assert 6 * nh <= LANES

    def body(x_ref, p_ref, o_ref):
        x = x_ref[...]
        lane = lax.broadcasted_iota(jnp.int32, x.shape, 1)
        is_g = (lane >= 2 * nh) & (lane < 4 * nh)
        g = jnp.where(is_g, -jnp.exp(p_ref[0:1, :]) * _softplus(x + p_ref[1:2, :]), 0.0)
        m_f, m_b = _chunk_cumsum_masks(tr)
        gc = jnp.where(lane < 3 * nh, _exact_nn(m_f, g), _exact_nn(m_b, g))
        gc = pltpu.roll(gc, 2 * nh, 1)
        o_ref[...] = jnp.where(lane < 2 * nh, jax.nn.sigmoid(x), jnp.where(lane < 4 * nh, g, gc))

    return pl.pallas_call(
        body, grid=(s // tr,), in_specs=[_row(tr, LANES, lay.ba // LANES), _vec(8, LANES)],
        out_specs=_row(tr, LANES), out_shape=SDS((s, LANES), F32),
        compiler_params=_params("parallel"), name=name)(proj, pvec)


def _prep_c_bwd(name, dbg_f, dbg_b, proj, pvec, dproj, lay, tr=512):
    s, nh, tail = proj.shape[0], lay.nh, lay.tail
    tr = _tile(s, tr, CHUNK)

    def body(x_ref, df_ref, db_ref, p_ref, _, o_ref, acc_ref):
        @pl.when(pl.program_id(0) == 0)
        def _():
            acc_ref[...] = jnp.zeros_like(acc_ref)

        x = x_ref[...]
        lane = lax.broadcasted_iota(jnp.int32, x.shape, 1)
        is_b, is_g = lane < 2 * nh, (lane >= 2 * nh) & (lane < 4 * nh)
        fwd_lane = (lane < nh) | ((lane >= 2 * nh) & (lane < 3 * nh))
        d = jnp.where(lane < 4 * nh, jnp.where(fwd_lane, df_ref[...], db_ref[...]), 0.0)
        m_f, m_b = _chunk_cumsum_masks(tr)
        dgc = jnp.where(is_g, d, 0.0)
        dg = jnp.where(fwd_lane, _exact_nn(m_b, dgc), _exact_nn(m_f, dgc))
        sb = jax.nn.sigmoid(x)
        na = -jnp.exp(p_ref[0:1, :])
        xs = x + p_ref[1:2, :]
        dsp = dg * na * jax.nn.sigmoid(xs)
        dx = jnp.where(is_b, d * sb * (1.0 - sb), jnp.where(is_g, dsp, 0.0))
        o_ref[...] = jnp.zeros_like(o_ref)
        o_ref[:, 0:LANES] = dx.astype(BF16)
        acc_ref[0:1, :] += jnp.sum(jnp.where(is_g, dg * na * _softplus(xs), 0.0), axis=0, keepdims=True)
        acc_ref[1:2, :] += jnp.sum(jnp.where(is_g, dsp, 0.0), axis=0, keepdims=True)

    return pl.pallas_call(
        body, grid=(s // tr,),
        in_specs=[_row(tr, LANES, lay.ba // LANES), _row(tr, LANES), _row(tr, LANES), _vec(8, LANES),
                  pl.BlockSpec(memory_space=pl.ANY)],
        out_specs=[_row(tr, tail, lay.ba // tail), _vec(8, LANES)],
        out_shape=[SDS(dproj.shape, dproj.dtype), SDS((8, LANES), F32)], input_output_aliases={4: 0},
        compiler_params=_params("arbitrary"), name=name)(proj, dbg_f, dbg_b, pvec, dproj)


def _post(name, o_f, o_b, proj, dn_w, lay, tr=256):
    s, qk, nh = o_f.shape[0], lay.qk, lay.nh
    tr = _tile(s, tr, 8)

    def body(f_ref, b_ref, z_ref, w_ref, o_ref):
        o = f_ref[...] + b_ref[...]
        gate = jax.nn.silu(z_ref[...])
        for hd in range(nh):
            sl = slice(hd * HEAD, (hd + 1) * HEAD)
            oh = o[:, sl]
            r = lax.rsqrt(jnp.mean(oh * oh, axis=-1, keepdims=True) + EPS)
            o_ref[:, sl] = (oh * r * w_ref[...] * gate[:, sl]).astype(BF16)

    return pl.pallas_call(
        body, grid=(s // tr,),
        in_specs=[_row(tr, qk), _row(tr, qk), _row(tr, qk, lay.z // qk), _vec(1, HEAD)],
        out_specs=_row(tr, qk), out_shape=SDS((s, qk), BF16),
        compiler_params=_params("parallel"), name=name)(o_f, o_b, proj, dn_w)


def _post_bwd(name, dyb, o_f, o_b, proj, dn_w, dproj, lay, tr=256):
    s, qk, nh = o_f.shape[0], lay.qk, lay.nh
    tr = _tile(s, tr, 8)

    def body(d_ref, f_ref, b_ref, z_ref, w_ref, _, do_ref, dz_ref, acc_ref):
        @pl.when(pl.program_id(0) == 0)
        def _():
            acc_ref[...] = jnp.zeros_like(acc_ref)

        o, z, d, wv = f_ref[...] + b_ref[...], z_ref[...], d_ref[...], w_ref[...]
        gate = jax.nn.silu(z)
        dgate = _silu_grad(z)
        for hd in range(nh):
            sl = slice(hd * HEAD, (hd + 1) * HEAD)
            oh, dh = o[:, sl], d[:, sl]
            r = lax.rsqrt(jnp.mean(oh * oh, axis=-1, keepdims=True) + EPS)
            n = oh * r
            dz_ref[:, sl] = (dh * n * wv * dgate[:, sl]).astype(BF16)
            don = dh * gate[:, sl]
            acc_ref[0:1, :] += jnp.sum(don * n, axis=0, keepdims=True)
            dn = don * wv
            do_ref[:, sl] = r * (dn - n * jnp.mean(dn * n, axis=-1, keepdims=True))

    return pl.pallas_call(
        body, grid=(s // tr,),
        in_specs=[_row(tr, qk), _row(tr, qk), _row(tr, qk), _row(tr, qk, lay.z // qk), _vec(1, HEAD),
                  pl.BlockSpec(memory_space=pl.ANY)],
        out_specs=[_row(tr, qk), _row(tr, qk, lay.z // qk), _vec(8, HEAD)],
        out_shape=[SDS((s, qk), F32), SDS(dproj.shape, dproj.dtype), SDS((8, HEAD), F32)],
        input_output_aliases={5: 1},
        compiler_params=_params("arbitrary"), name=name)(dyb, o_f, o_b, proj, dn_w, dproj)


def _merge(name, pa, pb, proj, lay, tr=512):
    s, d = pa.shape
    tr = _tile(s, tr, 8)

    def body(a_ref, b_ref, g_ref, o_ref):
        o_ref[...] = (jax.nn.sigmoid(g_ref[:, 0:d]) * a_ref[...]
                      + jax.nn.sigmoid(g_ref[:, d:2 * d]) * b_ref[...]).astype(BF16)

    return pl.pallas_call(
        body, grid=(s // tr,), in_specs=[_row(tr, d), _row(tr, d), _row(tr, 2 * d, lay.ga // (2 * d))],
        out_specs=_row(tr, d), out_shape=SDS((s, d), BF16),
        compiler_params=_params("parallel"), name=name)(pa, pb, proj)


def _merge_bwd(name, dmg, pa, pb, proj, lay, tr=512):
    s, d = pa.shape
    tr = _tile(s, tr, 8)

    def body(d_ref, a_ref, b_ref, g_ref, da_ref, db_ref, dg_ref):
        dm = d_ref[...]
        sa, sb = jax.nn.sigmoid(g_ref[:, 0:d]), jax.nn.sigmoid(g_ref[:, d:2 * d])
        da_ref[...] = (sa * dm).astype(BF16)
        db_ref[...] = (sb * dm).astype(BF16)
        dg_ref[:, 0:d] = (dm * a_ref[...] * sa * (1.0 - sa)).astype(BF16)
        dg_ref[:, d:2 * d] = (dm * b_ref[...] * sb * (1.0 - sb)).astype(BF16)

    return pl.pallas_call(
        body, grid=(s // tr,),
        in_specs=[_row(tr, d), _row(tr, d), _row(tr, d), _row(tr, 2 * d, lay.ga // (2 * d))],
        out_specs=[_row(tr, d), _row(tr, d), _row(tr, 2 * d, lay.ga // (2 * d))],
        out_shape=[SDS((s, d), BF16), SDS((s, d), BF16), SDS((s, lay.total), BF16)],
        compiler_params=_params("parallel"), name=name)(dmg, pa, pb, proj)


def _tri_inverse(a_mat, ri, ci):
    def same(shift):
        return (ri >> shift) == (ci >> shift)

    x = -jnp.where(same(3), a_mat, 0.0)
    t_mat = (ri == ci).astype(F32) + x
    for _ in range(2):
        x = _bnn(x, x)
        t_mat = t_mat + _bnn(t_mat, x)
    for shift in (3, 4, 5):
        b = jnp.where(same(shift + 1) & ~same(shift), a_mat, 0.0)
        t_mat = t_mat - _bnn(_bnn(t_mat, b), t_mat)
    return t_mat


def _chunk_terms(q, k, v, beta, gc, g_row, g_last, reverse, t_mat=None):
    c = CHUNK
    ri = lax.broadcasted_iota(jnp.int32, (c, c), 0)
    ci = lax.broadcasted_iota(jnp.int32, (c, c), 1)
    if reverse:
        incl, strict = ri <= ci, ri < ci
    else:
        incl, strict = ri >= ci, ri > ci
    decay = jnp.where(incl, jnp.exp(jnp.where(incl, gc - g_row, 0.0)), 0.0)
    e = jnp.exp(gc)
    ed = jnp.exp(g_last - gc)
    el = jnp.exp(g_last)
    kb = k * beta
    a_mat = jnp.where(strict, _bnt(kb, k) * decay, 0.0)
    if t_mat is None:
        t_mat = _tri_inverse(a_mat, ri, ci)
    u = _bnn(t_mat, v * beta)
    w = _bnn(t_mat, kb * e)
    p_mat = jnp.where(incl, _bnt(q, k) * decay, 0.0)
    return dict(incl=incl, strict=strict, decay=decay, e=e, ed=ed, el=el, kb=kb,
                a=a_mat, t=t_mat, u=u, w=w, p=p_mat)


def _delta_specs(nh, tb, nb, reverse):
    tok = (lambda i: nb - 1 - i) if reverse else (lambda i: i)
    hw = nh * HEAD
    qkv = [pl.BlockSpec((tb, hw), functools.partial(lambda i, part: (tok(i), part), part=p)) for p in range(3)]
    rows = pl.BlockSpec((tb, hw), lambda i: (tok(i), 0))
    bg = pl.BlockSpec((tb, LANES), lambda i: (tok(i), 0))
    gct = pl.BlockSpec((2 * nh, tb), lambda i: (0, tok(i)))
    st = pl.BlockSpec((nh, tb // CHUNK, HEAD, HEAD), lambda i: (0, tok(i), 0, 0))
    tri = pl.BlockSpec((nh, tb // CHUNK, CHUNK, CHUNK), lambda i: (0, tok(i), 0, 0))
    return qkv, rows, bg, gct, st, tri


def _heads(ref, rows, nh):
    return jnp.stack([ref[rows, hd * HEAD:(hd + 1) * HEAD] for hd in range(nh)])


def _chunk_scalars(bg_ref, gct_ref, cj, nh, tb, reverse):
    rows = pl.ds(cj * CHUNK, CHUNK)
    lb = nh if reverse else 0
    lc = 4 * nh + lb
    last = cj * CHUNK + (0 if reverse else CHUNK - 1)
    g_lanes = gct_ref[lb:lb + nh, :]
    if cj:
        g_lanes = pltpu.roll(g_lanes, tb - cj * CHUNK, 1)
    col = lambda l0, r: jnp.stack([bg_ref[r, l0 + hd:l0 + hd + 1] for hd in range(nh)])
    return col(lb, rows), col(lc, rows), g_lanes[:, 0:CHUNK][:, None, :], col(lc, pl.ds(last, 1))


def _delta_fwd(name, qkvn, bg, gct, nh, reverse, tb=128):
    s = qkvn.shape[0]
    tb = _tile(s, tb, LANES)
    nb, cpb = s // tb, tb // CHUNK
    qkv, rows_spec, bg_spec, gct_spec, st, tri = _delta_specs(nh, tb, nb, reverse)

    def body(q_ref, k_ref, v_ref, bg_ref, gct_ref, o_ref, st_ref, tri_ref, state):
        @pl.when(pl.program_id(0) == 0)
        def _():
            state[...] = jnp.zeros_like(state)

        for cj in (range(cpb - 1, -1, -1) if reverse else range(cpb)):
            rows = pl.ds(cj * CHUNK, CHUNK)
            q, k, v = _heads(q_ref, rows, nh), _heads(k_ref, rows, nh), _heads(v_ref, rows, nh)
            tm = _chunk_terms(q, k, v, *_chunk_scalars(bg_ref, gct_ref, cj, nh, tb, reverse), reverse)
            s_in = state[...]
            st_ref[:, cj] = s_in
            tri_ref[:, cj] = tm["t"]
            vn = tm["u"] - _bnn(tm["w"], s_in)
            o = _bnn(q * tm["e"], s_in) + _bnn(tm["p"], vn)
            for hd in range(nh):
                o_ref[rows, hd * HEAD:(hd + 1) * HEAD] = o[hd]
            state[...] = s_in * tm["el"] + _btn(k * tm["ed"], vn)

    return pl.pallas_call(
        body, grid=(nb,), in_specs=qkv + [bg_spec, gct_spec], out_specs=[rows_spec, st, tri],
        out_shape=[SDS((s, nh * HEAD), F32), SDS((nh, s // CHUNK, HEAD, HEAD), F32),
                   SDS((nh, s // CHUNK, CHUNK, CHUNK), F32)],
        scratch_shapes=[pltpu.VMEM((nh, HEAD, HEAD), F32)],
        compiler_params=_params("arbitrary"), name=name)(qkvn, qkvn, qkvn, bg, gct)


def _delta_bwd(name, qkvn, bg, gct, do, states, tris, nh, reverse, add=None, tb=128):
    s = qkvn.shape[0]
    tb = _tile(s, tb, LANES)
    nb, cpb = s // tb, tb // CHUNK
    qkv, rows_spec, bg_spec, gct_spec, st, tri = _delta_specs(nh, tb, nb, not reverse)
    n_add = 0 if add is None else 3

    def body(*refs):
        q_ref, k_ref, v_ref, bg_ref, gct_ref, do_ref, st_ref, tri_ref = refs[0:8]
        add_refs = refs[8:8 + n_add]
        dq_ref, dk_ref, dv_ref, dbg_ref, dstate = refs[8 + n_add:]

        @pl.when(pl.program_id(0) == 0)
        def _():
            dstate[...] = jnp.zeros_like(dstate)

        ones = jnp.ones((nh, CHUNK, HEAD), BF16)
        row_id = lax.broadcasted_iota(jnp.int32, (CHUNK, 1), 0)
        rsum = lambda x: jnp.sum(x, axis=2, keepdims=True)
        for cj in (range(cpb) if reverse else range(cpb - 1, -1, -1)):
            rows = pl.ds(cj * CHUNK, CHUNK)
            q, k, v, d_o = (_heads(r, rows, nh) for r in (q_ref, k_ref, v_ref, do_ref))
            beta, gc, g_row, g_last = _chunk_scalars(bg_ref, gct_ref, cj, nh, tb, reverse)
            tm = _chunk_terms(q, k, v, beta, gc, g_row, g_last, reverse, t_mat=tri_ref[:, cj])
            incl, strict, e, ed, el, kb = tm["incl"], tm["strict"], tm["e"], tm["ed"], tm["el"], tm["kb"]
            t_mat, u, w, p_mat, decay = tm["t"], tm["u"], tm["w"], tm["p"], tm["decay"]
            s_in, ds_out = st_ref[:, cj], dstate[...]
            vn = u - _bnn(w, s_in)
            qe, kd, ke = q * e, k * ed, kb * e
            dvn = _btn(p_mat, d_o) + _bnn(kd, ds_out)
            dqe = _bnt(d_o, s_in)
            dq = dqe * e
            dgc = rsum(dqe * qe)
            dp = jnp.where(incl, _bnt(d_o, vn), 0.0)
            dkd = _bnt(vn, ds_out)
            dk = dkd * ed
            r = rsum(dkd * kd)
            dgc = dgc - r
            dg_last = (jnp.sum(r, axis=1, keepdims=True)
                       + jnp.sum(rsum(ds_out * s_in), axis=1, keepdims=True) * el)
            dw = -_bnt(dvn, s_in)
            dbv = _btn(t_mat, dvn)
            dke = _btn(t_mat, dw)
            da = -jnp.where(strict, _bnt(dbv, u) + _bnt(dke, w), 0.0)
            m_mat, n_mat = da * decay, dp * decay
            dkb = _bnn(m_mat, k) + dke * e
            dk = dk + _btn(m_mat, kb) + _btn(n_mat, q)
            dq = dq + _bnn(n_mat, k)
            g_mat = da * tm["a"] + dp * p_mat
            g_hi, g_mid, g_lo = _split3(g_mat)
            col = (_btn(g_hi, ones) + _btn(g_mid, ones) + _btn(g_lo, ones))[:, :, 0:1]
            dgc = dgc + rsum(g_mat) - col + rsum(dke * ke)
            dgc = dgc + jnp.where(row_id == (0 if reverse else CHUNK - 1), dg_last, 0.0)
            dv = dbv * beta
            dbeta = rsum(dbv * v) + rsum(dkb * k)
            dk = dk + dkb * beta
            dstate[...] = el * ds_out + _btn(qe, d_o) - _btn(w, dvn)
            lb = nh if reverse else 0
            for hd in range(nh):
                cols = slice(hd * HEAD, (hd + 1) * HEAD)
                extra = [a[rows, cols] for a in add_refs] if n_add else [0.0, 0.0, 0.0]
                dq_ref[rows, cols] = dq[hd] + extra[0]
                dk_ref[rows, cols] = dk[hd] + extra[1]
                dv_ref[rows, cols] = dv[hd] + extra[2]
                dbg_ref[rows, lb + hd:lb + hd + 1] = dbeta[hd]
                dbg_ref[rows, 2 * nh + lb + hd:2 * nh + lb + hd + 1] = dgc[hd]

    out3 = SDS((s, nh * HEAD), F32)
    return pl.pallas_call(
        body, grid=(nb,), in_specs=qkv + [bg_spec, gct_spec, rows_spec, st, tri] + [rows_spec] * n_add,
        out_specs=[rows_spec, rows_spec, rows_spec, bg_spec], out_shape=[out3, out3, out3, SDS((s, LANES), F32)],
        scratch_shapes=[pltpu.VMEM((nh, HEAD, HEAD), F32)],
        compiler_params=_params("arbitrary"), name=name)(qkvn, qkvn, qkvn, bg, gct, do, states, tris, *(add or ()))


def _row_pieces(g):
    return g.reshape(N_CHIPS, g.shape[0] // N_CHIPS, g.shape[1])


def _ffn_fwd(tag, h, nw, sh, sc, g, w_up, w_down):
    u = _norm_mod(tag + "_norm", h, nw, sh, sc)
    ab = _matmul(tag + "_up", u, w_up, "nn", out_dtype=BF16, tn=1408)
    hm = _swiglu(tag + "_act", ab)
    f = _matmul(tag + "_down", hm, w_down, "nn")
    return _resid(tag + "_res", h, f, g, 0.5), (h, u, ab, hm, f)


def _ffn_bwd(tag, dh, saved, nw, sc, g, w_up, w_down):
    h, u, ab, hm, f = saved
    df, acc_g = _resid_bwd(tag + "_res_bwd", dh, f, g, 0.5)
    gw_down = _matmul(tag + "_gw_down", hm, df, "tn", out_dtype=BF16, tm=1408)
    dhm = _matmul(tag + "_dhm", df, w_down, "nt", out_dtype=BF16, tn=2816)
    dab = _swiglu_bwd(tag + "_act_bwd", ab, dhm)
    gw_up = _matmul(tag + "_gw_up", u, dab, "tn", out_dtype=BF16, tn=1408, out_pieces=N_CHIPS)
    du = _matmul(tag + "_du", dab, w_up, "nt")
    dh_in, acc = _norm_mod_bwd(tag + "_norm_bwd", h, du, dh, nw, sc)
    return dh_in, gw_up, _row_pieces(gw_down), (acc[0], acc[1], acc_g[0], acc[2])


def _mixer_fwd(h, nw, sh, sc, g, wt, lay):
    nh = lay.nh
    u = _norm_mod("mix_norm", h, nw, sh, sc)
    proj = _matmul("mix_in", u, wt["w_in"], "nn")
    qkvn = _prep_b("mix_prep_b", proj, wt["conv_dn"], lay)
    ya = _prep_a("mix_prep_a", proj, wt["conv_a"], lay)
    bg = _prep_c("mix_prep_c", proj, wt["pvec"], lay)
    gct = bg[:, 4 * nh:6 * nh].T
    o_f, *st_f = _delta_fwd("delta_fwd_l2r", qkvn, bg, gct, nh, False)
    o_b, *st_b = _delta_fwd("delta_fwd_r2l", qkvn, bg, gct, nh, True)
    yb = _post("mix_post", o_f, o_b, proj, wt["dn_norm"], lay)
    pa = _matmul("mix_a_out", ya, wt["w_a_out"], "nn")
    pb = _matmul("mix_b_out", yb, wt["w_b_out"], "nn")
    mg = _merge("mix_merge", pa, pb, proj, lay)
    y = _matmul("mix_out", mg, wt["w_out"], "nn")
    h2 = _resid("mix_res", h, y, g, 1.0)
    return h2, (h, u, proj, qkvn, ya, bg, gct, o_f, o_b, st_f, st_b, yb, pa, pb, mg, y)


def _mixer_bwd(dh, saved, nw, sc, g, wt, lay):
    h, u, proj, qkvn, ya, bg, gct, o_f, o_b, st_f, st_b, yb, pa, pb, mg, y = saved
    nh = lay.nh
    dy, acc_g = _resid_bwd("mix_res_bwd", dh, y, g, 1.0)
    gw_out = _matmul("mix_gw_out", mg, dy, "tn", out_dtype=BF16)
    dmg = _matmul("mix_dmg", dy, wt["w_out"], "nt")
    dpa, dpb, dproj = _merge_bwd("mix_merge_bwd", dmg, pa, pb, proj, lay)
    gw_a = _matmul("mix_gw_a", ya, dpa, "tn", out_dtype=BF16, out_pieces=N_CHIPS)
    gw_b = _matmul("mix_gw_b", yb, dpb, "tn", out_dtype=BF16)
    dya = _matmul("mix_dya", dpa, wt["w_a_out"], "nt")
    dyb = _matmul("mix_dyb", dpb, wt["w_b_out"], "nt")
    do, dproj, acc_dn = _post_bwd("mix_post_bwd", dyb, o_f, o_b, proj, wt["dn_norm"], dproj, lay)
    dq, dk, dv, dbg_f = _delta_bwd("delta_bwd_l2r", qkvn, bg, gct, do, *st_f, nh, False)
    dq, dk, dv, dbg_b = _delta_bwd("delta_bwd_r2l", qkvn, bg, gct, do, *st_b, nh, True, add=(dq, dk, dv))
    dproj, acc_ca = _prep_a_bwd("mix_prep_a_bwd", dya, proj, wt["conv_a"], dproj, lay)
    dproj, acc_cd = _prep_b_bwd("mix_prep_b_bwd", dq, dk, dv, proj, wt["conv_dn"], dproj, lay)
    dproj, acc_pc = _prep_c_bwd("mix_prep_c_bwd", dbg_f, dbg_b, proj, wt["pvec"], dproj, lay)
    gw_in = lay.unperm_cols(_matmul("mix_gw_in", u, dproj, "tn", out_dtype=BF16))
    gw_in = gw_in.reshape(gw_in.shape[0], N_CHIPS, -1).transpose(1, 0, 2)
    du = _matmul("mix_du", dproj, wt["w_in"], "nt")
    dh_in, acc = _norm_mod_bwd("mix_norm_bwd", h, du, dh, nw, sc)
    small = dict(conv_a=acc_ca[0:wt["conv_a"].shape[0]], conv_dn=acc_cd[0:wt["conv_dn"].shape[0]],
                 dn_norm=acc_dn[0:1], a_log=acc_pc[0], dt_bias=acc_pc[1])
    big = dict(w_in=gw_in, w_a_out=gw_a, w_b_out=_row_pieces(gw_b), w_out=_row_pieces(gw_out))
    return dh_in, big, small, (acc[0], acc[1], acc_g[0], acc[2])


def _local_step(x, tgt, modv, wt, lay):
    m = [modv[i:i + 1] for i in range(9)]
    h1, sv1 = _ffn_fwd("ffn1", x, wt["norm_ffn1"], m[0], m[1], m[2], wt["w_ffn1_up"], wt["w_ffn1_down"])
    h2, sv2 = _mixer_fwd(h1, wt["norm_mix"], m[3], m[4], m[5], wt, lay)
    h3, sv3 = _ffn_fwd("ffn2", h2, wt["norm_ffn2"], m[6], m[7], m[8], wt["w_ffn2_up"], wt["w_ffn2_down"])
    dh3, acc_f = _final_loss("final_loss", h3, tgt, wt["norm_final"])
    loss = jnp.sum(acc_f[1])
    dh2, gu2, gd2, dm3 = _ffn_bwd("ffn2", dh3, sv3, wt["norm_ffn2"], m[7], m[8], wt["w_ffn2_up"], wt["w_ffn2_down"])
    dh1, gmix, small, dm2 = _mixer_bwd(dh2, sv2, wt["norm_mix"], m[4], m[5], wt, lay)
    dx, gu1, gd1, dm1 = _ffn_bwd("ffn1", dh1, sv1, wt["norm_ffn1"], m[1], m[2], wt["w_ffn1_up"], wt["w_ffn1_down"])
    dmod = jnp.stack([dm1[0], dm1[1], dm1[2], dm2[0], dm2[1], dm2[2], dm3[0], dm3[1], dm3[2]])
    big = dict(w_ffn1_up=gu1, w_ffn1_down=gd1, w_ffn2_up=gu2, w_ffn2_down=gd2, **gmix)
    small = dict(small, norm_ffn1=dm1[3], norm_mix=dm2[3], norm_ffn2=dm3[3], norm_final=acc_f[0])
    return loss, dx, dmod, big, small


def _position():
    return lax.axis_index("x"), lax.axis_index("y"), lax.axis_index("c")


_ANY = pl.BlockSpec(memory_space=pl.ANY)
_VMEM = pl.BlockSpec(memory_space=pltpu.VMEM)


def _allgather8(name, v):
    r = v.shape[0]

    def body(v_ref, out_ref, send_sems, recv_sems):
        x, y, c = _position()
        me = 4 * x + 2 * y + c
        out_ref[me] = v_ref[...]
        copies = []
        for mask in range(1, N_DEV):
            peer = tuple(1 - p if mask >> b & 1 else p for p, b in ((x, 2), (y, 1), (c, 0)))
            cp = pltpu.make_async_remote_copy(
                src_ref=v_ref, dst_ref=out_ref.at[me], send_sem=send_sems.at[mask - 1],
                recv_sem=recv_sems.at[mask - 1], device_id=peer, device_id_type=MESH)
            cp.start()
            copies.append(cp)
        for cp in copies:
            cp.wait()

    return pl.pallas_call(
        body, in_specs=[_VMEM], out_specs=_VMEM, out_shape=SDS((N_DEV, r, LANES), F32),
        scratch_shapes=[pltpu.SemaphoreType.DMA((N_DEV - 1,)), pltpu.SemaphoreType.DMA((N_DEV - 1,))],
        name=name)(v)


def _other_chips(x, y):
    return [(1 - x, y), (x, 1 - y), (1 - x, 1 - y)]


def _half_rows(c, rows):
    hr = rows // 2
    assert hr % 16 == 0
    return pl.ds(pl.multiple_of(c * hr, 16), hr)


def _gather_shards(name, shards):
    nw = len(shards)

    def body(*refs):
        ins, outs = refs[0:nw], refs[nw:2 * nw]
        send_sems, recv_sems, local_sems = refs[2 * nw:]
        x, y, c = _position()
        p = 2 * x + y
        chips = _other_chips(x, y)

        def copy(w, k, chip_index, rows, to, src=None):
            dst = outs[w].at[chip_index, rows]
            return pltpu.make_async_remote_copy(
                src_ref=dst if src is None else src, dst_ref=dst, send_sem=send_sems.at[k * nw + w],
                recv_sem=recv_sems.at[k * nw + w], device_id=to, device_id_type=MESH)

        half = [_half_rows(c, r.shape[0]) for r in ins]
        other = [_half_rows(1 - c, r.shape[0]) for r in ins]
        mine = [pltpu.make_async_copy(ins[w], outs[w].at[p], local_sems.at[w]) for w in range(nw)]
        for cp in mine:
            cp.start()
        first = [copy(w, j, p, half[w], (cx, cy, c), src=ins[w].at[half[w]])
                 for j, (cx, cy) in enumerate(chips) for w in range(nw)]
        for cp in first:
            cp.start()
        passed = []
        for j, (cx, cy) in enumerate(chips):
            for w in range(nw):
                copy(w, j, 2 * cx + cy, half[w], (x, y, c)).wait_recv()
                cp = copy(w, 3 + j, 2 * cx + cy, half[w], (x, y, 1 - c))
                cp.start()
                passed.append(cp)
        for j, (cx, cy) in enumerate(chips):
            for w in range(nw):
                copy(w, 3 + j, 2 * cx + cy, other[w], (x, y, c)).wait_recv()
        for cp in first + passed:
            cp.wait_send()
        for cp in mine:
            cp.wait()

    return pl.pallas_call(
        body, in_specs=[_ANY] * nw, out_specs=[_ANY] * nw,
        out_shape=[SDS((N_CHIPS,) + v.shape, v.dtype) for v in shards],
        scratch_shapes=[pltpu.SemaphoreType.DMA((6 * nw,)), pltpu.SemaphoreType.DMA((6 * nw,)),
                        pltpu.SemaphoreType.DMA((nw,))],
        name=name)(*shards)


def _swap_halves(name, gs):
    nw = len(gs)

    def body(*refs):
        ins, outs = refs[0:nw], refs[nw:2 * nw]
        send_sems, recv_sems = refs[2 * nw:]
        x, y, c = _position()
        copies = []
        for w in range(nw):
            cp = pltpu.make_async_remote_copy(
                src_ref=ins[w].at[:, _half_rows(1 - c, ins[w].shape[1])], dst_ref=outs[w],
                send_sem=send_sems.at[w], recv_sem=recv_sems.at[w], device_id=(x, y, 1 - c), device_id_type=MESH)
            cp.start()
            copies.append(cp)
        for cp in copies:
            cp.wait()

    return pl.pallas_call(
        body, in_specs=[_ANY] * nw, out_specs=[_ANY] * nw,
        out_shape=[SDS((g.shape[0], g.shape[1] // 2, g.shape[2]), g.dtype) for g in gs],
        scratch_shapes=[pltpu.SemaphoreType.DMA((nw,)), pltpu.SemaphoreType.DMA((nw,))], name=name)(*gs)


def _scatter_chips(name, vs):
    nw = len(vs)

    def body(*refs):
        ins, outs = refs[0:nw], refs[nw:2 * nw]
        send_sems, recv_sems = refs[2 * nw:]
        x, y, c = _position()
        copies = []
        for j, (cx, cy) in enumerate(_other_chips(x, y)):
            for w in range(nw):
                cp = pltpu.make_async_remote_copy(
                    src_ref=ins[w].at[2 * cx + cy], dst_ref=outs[w].at[j], send_sem=send_sems.at[j * nw + w],
                    recv_sem=recv_sems.at[j * nw + w], device_id=(cx, cy, c), device_id_type=MESH)
                cp.start()
                copies.append(cp)
        for cp in copies:
            cp.wait()

    return pl.pallas_call(
        body, in_specs=[_ANY] * nw, out_specs=[_ANY] * nw,
        out_shape=[SDS((N_CHIPS - 1,) + v.shape[1:], v.dtype) for v in vs],
        scratch_shapes=[pltpu.SemaphoreType.DMA((3 * nw,)), pltpu.SemaphoreType.DMA((3 * nw,))], name=name)(*vs)


def _share_halves(name, halves):
    nw = len(halves)

    def body(*refs):
        ins, outs = refs[0:nw], refs[nw:2 * nw]
        send_sems, recv_sems, local_sems = refs[2 * nw:]
        x, y, c = _position()
        copies = []
        for w in range(nw):
            rows = _half_rows(c, outs[w].shape[0])
            loc = pltpu.make_async_copy(ins[w], outs[w].at[rows], local_sems.at[w])
            loc.start()
            cp = pltpu.make_async_remote_copy(
                src_ref=ins[w], dst_ref=outs[w].at[rows], send_sem=send_sems.at[w], recv_sem=recv_sems.at[w],
                device_id=(x, y, 1 - c), device_id_type=MESH)
            cp.start()
            copies += [loc, cp]
        for cp in copies:
            cp.wait()

    return pl.pallas_call(
        body, in_specs=[_ANY] * nw, out_specs=[_ANY] * nw,
        out_shape=[SDS((2 * h.shape[0], h.shape[1]), h.dtype) for h in halves],
        scratch_shapes=[pltpu.SemaphoreType.DMA((nw,)), pltpu.SemaphoreType.DMA((nw,)),
                        pltpu.SemaphoreType.DMA((nw,))], name=name)(*halves)


def _chip_sum(name, g, from_sib):
    _, r, cdim = g.shape
    hr = r // 2

    def body(g_ref, s_ref, o_ref):
        o_ref[...] = (g_ref[...].astype(F32) + s_ref[...].astype(F32)).astype(o_ref.dtype)

    blk = (None, hr, cdim)
    return pl.pallas_call(
        body, grid=(N_CHIPS,),
        in_specs=[pl.BlockSpec(blk, lambda j: (j, lax.axis_index("c"), 0)), pl.BlockSpec(blk, lambda j: (j, 0, 0))],
        out_specs=pl.BlockSpec(blk, lambda j: (j, 0, 0)),
        out_shape=SDS((N_CHIPS, hr, cdim), g.dtype), compiler_params=_params("parallel"), name=name)(g, from_sib)


def _total(name, g, from_sib, from_chips):
    _, r, cdim = g.shape
    hr = r // 2
    tr = _tile(hr, 256, 16)
    nt = hr // tr

    def body(g_ref, s_ref, rc_ref, o_ref):
        acc = g_ref[...].astype(F32) + s_ref[...].astype(F32)
        for j in range(N_CHIPS - 1):
            acc = acc + rc_ref[j].astype(F32)
        o_ref[...] = acc

    def chip():
        return 2 * lax.axis_index("x") + lax.axis_index("y")

    blk = (None, tr, cdim)
    return pl.pallas_call(
        body, grid=(nt,),
        in_specs=[pl.BlockSpec(blk, lambda i: (chip(), lax.axis_index("c") * nt + i, 0)),
                  pl.BlockSpec(blk, lambda i: (chip(), i, 0)),
                  pl.BlockSpec((N_CHIPS - 1, tr, cdim), lambda i: (0, i, 0))],
        out_specs=pl.BlockSpec((tr, cdim), lambda i: (i, 0)),
        out_shape=SDS((hr, cdim), F32), compiler_params=_params("parallel"), name=name)(g, from_sib, from_chips)


def _sum8(name, v):
    _, r, w = v.shape

    def body(v_ref, o_ref):
        acc = v_ref[0]
        for j in range(1, N_DEV):
            acc = acc + v_ref[j]
        o_ref[...] = acc

    return pl.pallas_call(body, in_specs=[_VMEM], out_specs=_VMEM, out_shape=SDS((r, w), F32), name=name)(v)


def _adamw(name, w, g, m, v, tr=256):
    r, cdim = w.shape
    tr = _tile(r, tr, 8)
    bc1, bc2 = 1.0 - ADAM_B1 ** ADAM_STEP, 1.0 - ADAM_B2 ** ADAM_STEP

    def body(w_ref, g_ref, m_ref, v_ref, d_ref, nm_ref, nv_ref):
        g = g_ref[...]
        m2 = ADAM_B1 * m_ref[...] + (1.0 - ADAM_B1) * g
        v2 = ADAM_B2 * v_ref[...] + (1.0 - ADAM_B2) * (g * g)
        d_ref[...] = -ADAM_LR * ((m2 / bc1) / (jnp.sqrt(v2 / bc2) + ADAM_EPS) + ADAM_WD * w_ref[...])
        nm_ref[...] = m2
        nv_ref[...] = v2

    spec = _row(tr, cdim)
    out = SDS((r, cdim), F32)
    return pl.pallas_call(body, grid=(r // tr,), in_specs=[spec] * 4, out_specs=[spec] * 3, out_shape=[out] * 3,
                          compiler_params=_params("parallel"), name=name)(w, g, m, v)


def _pack_rows(arrays, width, row_mult, dtype):
    parts, spans, row = [], [], 0
    for a in arrays:
        n = a.size
        rows = -(-n // width)
        flat = a.reshape(-1).astype(dtype)
        if rows * width != n:
            flat = jnp.concatenate([flat, jnp.zeros((rows * width - n,), dtype)])
        parts.append(flat.reshape(rows, width))
        spans.append((row, rows, n, a.shape))
        row += rows
    pad = -row % row_mult
    if pad:
        parts.append(jnp.zeros((pad, width), dtype))
    return jnp.concatenate(parts, axis=0), spans


def _unpack_rows(packed, spans):
    return [packed[r0:r0 + rows].reshape(-1)[0:n].reshape(shape) for r0, rows, n, shape in spans]


BIG = ("w_ffn1_up", "w_ffn1_down", "w_in", "w_a_out", "w_b_out", "w_out", "w_ffn2_up", "w_ffn2_down")
COL_SHARDED = ("w_ffn1_up", "w_in", "w_a_out", "w_ffn2_up")
SMALL = ("b_ada", "norm_ffn1", "norm_mix", "conv_a", "conv_dn", "a_log_fwd", "dt_bias_fwd", "a_log_bwd",
         "dt_bias_bwd", "dn_norm", "norm_ffn2", "norm_final")
WEIGHTS = ("w_ada", "b_ada", "norm_ffn1", "w_ffn1_up", "w_ffn1_down", "norm_mix", "w_in", "conv_a", "conv_dn",
           "a_log_fwd", "dt_bias_fwd", "a_log_bwd", "dt_bias_bwd", "dn_norm", "w_a_out", "w_b_out", "w_out",
           "norm_ffn2", "w_ffn2_up", "w_ffn2_down", "norm_final")


def kernel(x, c, w_ada, b_ada, norm_ffn1, w_ffn1_up, w_ffn1_down, norm_mix, w_in, conv_a, conv_dn, a_log_fwd, dt_bias_fwd, a_log_bwd, dt_bias_bwd, dn_norm, w_a_out, w_b_out, w_out, norm_ffn2, w_ffn2_up, w_ffn2_down, norm_final, loss_target, m_w_ada, m_b_ada, m_norm_ffn1, m_w_ffn1_up, m_w_ffn1_down, m_norm_mix, m_w_in, m_conv_a, m_conv_dn, m_a_log_fwd, m_dt_bias_fwd, m_a_log_bwd, m_dt_bias_bwd, m_dn_norm, m_w_a_out, m_w_b_out, m_w_out, m_norm_ffn2, m_w_ffn2_up, m_w_ffn2_down, m_norm_final, v_w_ada, v_b_ada, v_norm_ffn1, v_w_ffn1_up, v_w_ffn1_down, v_norm_mix, v_w_in, v_conv_a, v_conv_dn, v_a_log_fwd, v_dt_bias_fwd, v_a_log_bwd, v_dt_bias_bwd, v_dn_norm, v_w_a_out, v_w_b_out, v_w_out, v_norm_ffn2, v_w_ffn2_up, v_w_ffn2_down, v_norm_final):
    given = dict(locals())
    wsh = {n: given[n] for n in WEIGHTS}
    msh = {n: given["m_" + n] for n in WEIGHTS}
    vsh = {n: given["v_" + n] for n in WEIGHTS}
    d = x.shape[-1]
    ca = conv_a.shape[-1] * N_CHIPS
    nh = conv_dn.shape[-1] * N_CHIPS // (3 * HEAD)
    lay = _Layout(d, ca, nh)
    xi, yi, ci = _position()
    chip = 2 * xi + yi
    me = 2 * chip + ci

    c_act = jax.nn.silu(c)
    g1, g1_spans = _pack_rows([c_act, conv_a[0], conv_dn[0]], LANES, 8, F32)
    g1_all = _allgather8("gather_cond", g1)
    per_dev = [_unpack_rows(g1_all[k], g1_spans) for k in range(N_DEV)]
    c_all = jnp.concatenate([p[0] for p in per_dev], axis=0)
    conv_a_full = jnp.concatenate([per_dev[2 * k][1] for k in range(N_CHIPS)], axis=1)
    conv_dn_full = jnp.concatenate([per_dev[2 * k][2] for k in range(N_CHIPS)], axis=1)

    mod_sh = _matmul("ada_mod", c_all, w_ada[0], "nn")
    b_sh = lax.dynamic_slice_in_dim(b_ada, chip * mod_sh.shape[1], mod_sh.shape[1], axis=1)
    g2, g2_spans = _pack_rows([mod_sh + b_sh], LANES, 8, F32)
    g2_all = _allgather8("gather_mod", g2)
    mod_all = jnp.concatenate([_unpack_rows(g2_all[2 * k], g2_spans)[0] for k in range(N_CHIPS)], axis=1)
    modv = lax.dynamic_index_in_dim(mod_all, me, 0, keepdims=False).reshape(9, d)

    gathered = _gather_shards("gather_weights", [wsh[n][0].astype(BF16) for n in BIG])
    wt = {}
    for n, g in zip(BIG, gathered):
        if n == "w_in":
            wt[n] = lay.perm_cols(jnp.concatenate(list(g), axis=1))
        else:
            wt[n] = g if n in COL_SHARDED else g.reshape(-1, g.shape[-1])
    lane_pad = (jnp.zeros((2 * nh,), F32), jnp.zeros((LANES - 4 * nh,), F32))
    pvec = jnp.stack([jnp.concatenate([lane_pad[0], a_log_fwd[0], a_log_bwd[0], lane_pad[1]]),
                      jnp.concatenate([lane_pad[0], dt_bias_fwd[0], dt_bias_bwd[0], lane_pad[1]])]
                     + [jnp.zeros((LANES,), F32)] * 6)
    wt.update(conv_a=conv_a_full, conv_dn=conv_dn_full, pvec=pvec, dn_norm=dn_norm, norm_ffn1=norm_ffn1,
              norm_mix=norm_mix, norm_ffn2=norm_ffn2, norm_final=norm_final.reshape(1, d))

    loss, dx, dmod, big, small = _local_step(x[0], loss_target[0], modv, wt, lay)
    loss = lax.psum(loss, ("x", "y", "c"))

    small_list = [dmod.reshape(1, 9 * d), small["norm_ffn1"], small["norm_mix"], small["conv_a"], small["conv_dn"],
                  small["a_log"][2 * nh:3 * nh], small["dt_bias"][2 * nh:3 * nh], small["a_log"][3 * nh:4 * nh],
                  small["dt_bias"][3 * nh:4 * nh], small["dn_norm"], small["norm_ffn2"], small["norm_final"]]
    g3, g3_spans = _pack_rows(small_list, LANES, 8, F32)
    g3_all = _allgather8("gather_small_grads", g3)
    g_small = dict(zip(SMALL, _unpack_rows(_sum8("sum_small_grads", g3_all), g3_spans)))
    dmod_all = jnp.concatenate([_unpack_rows(g3_all[k], g3_spans)[0] for k in range(N_DEV)], axis=0)
    ncol = w_ada.shape[-1]
    dmod_sh = lax.dynamic_slice_in_dim(dmod_all, chip * ncol, ncol, axis=1)
    grads = {"w_ada": _matmul("ada_grad", c_all, dmod_sh, "tn")[None]}
    for n in SMALL:
        g = g_small[n]
        if n in ("conv_a", "conv_dn"):
            wloc = wsh[n].shape[-1]
            g = lax.dynamic_slice_in_dim(g, chip * wloc, wloc, axis=1)
        grads[n] = g.reshape(wsh[n].shape)

    parts = [big[n] for n in BIG]
    from_sib = _swap_halves("grads_to_sibling", parts)
    chip_sums = [_chip_sum("chip_sum_" + n, g, f) for n, g, f in zip(BIG, parts, from_sib)]
    from_chips = _scatter_chips("grads_to_chips", chip_sums)
    halves = [_total("total_" + n, g, f, r) for n, g, f, r in zip(BIG, parts, from_sib, from_chips)]
    for n, g in zip(BIG, _share_halves("grads_share_total", halves)):
        grads[n] = g[None]

    delta, new_m, new_v = {}, {}, {}
    for n in ("w_ada",) + BIG:
        shp = wsh[n].shape
        outs = _adamw("adamw_" + n, *(t.reshape(shp[-2], shp[-1]) for t in (wsh[n], grads[n], msh[n], vsh[n])))
        delta[n], new_m[n], new_v[n] = (o.reshape(shp) for o in outs)
    packed = []
    for src in (wsh, grads, msh, vsh):
        pk, s_spans = _pack_rows([src[n] for n in SMALL], LANES, 8, F32)
        packed.append(pk)
    outs = _adamw("adamw_small", *packed)
    for dst, o in zip((delta, new_m, new_v), outs):
        dst.update(zip(SMALL, _unpack_rows(o, s_spans)))

    return (loss, dx[None], *[grads[n] for n in WEIGHTS], *[delta[n] for n in WEIGHTS],
            *[new_m[n] for n in WEIGHTS], *[new_v[n] for n in WEIGHTS])
```

```python
import functools

import jax
import jax.numpy as jnp
from jax import lax
from jax.experimental import pallas as pl
from jax.experimental.pallas import tpu as pltpu

F32 = jnp.float32
BF16 = jnp.bfloat16
SDS = jax.ShapeDtypeStruct
MESH = pl.DeviceIdType.MESH
HI = lax.Precision.HIGHEST

EPS = 1e-6
HEAD = 128
CHUNK = 64
LANES = 128
N_CHIPS = 4
N_DEV = 8
VMEM_LIMIT = 56 * 1024 * 1024

ADAM_LR = 0.001
ADAM_B1 = 0.9
ADAM_B2 = 0.999
ADAM_EPS = 1e-08
ADAM_WD = 0.01
ADAM_STEP = 10


def _params(*sem):
    return pltpu.CompilerParams(dimension_semantics=sem, vmem_limit_bytes=VMEM_LIMIT)


def _tile(n, cap, mult=LANES):
    t = min(n, cap) // mult * mult
    while t >= mult:
        if n % t == 0:
            return t
        t -= mult
    return n


def _row(tr, w, cb=0):
    return pl.BlockSpec((tr, w), lambda i: (i, cb))


def _vec(r, w):
    return pl.BlockSpec((r, w), lambda i: (0, 0))


def _nn(a, b, **kw):
    return jnp.dot(a, b, preferred_element_type=F32, **kw)


def _nt(a, b, **kw):
    return lax.dot_general(a, b, (((1,), (1,)), ((), ())), preferred_element_type=F32, **kw)


def _tn(a, b, **kw):
    return lax.dot_general(a, b, (((0,), (0,)), ((), ())), preferred_element_type=F32, **kw)


def _bnn(a, b):
    return lax.dot_general(a, b, (((2,), (1,)), ((0,), (0,))), preferred_element_type=F32)


def _bnt(a, b):
    return lax.dot_general(a, b, (((2,), (2,)), ((0,), (0,))), preferred_element_type=F32)


def _btn(a, b):
    return lax.dot_general(a, b, (((1,), (1,)), ((0,), (0,))), preferred_element_type=F32)


def _silu_grad(x):
    s = jax.nn.sigmoid(x)
    return s * (1.0 + x * (1.0 - s))


def _matmul(name, a, b, mode, out_dtype=F32, tm=1024, tn=1024, tk=2048, full_k=2816, out_pieces=0):
    pieces_b = b.shape[0] if b.ndim == 3 else 0
    b2 = b.shape[1:] if pieces_b else b.shape
    if mode == "nn":
        (m, k), n = a.shape, b2[1] * max(pieces_b, 1)
    elif mode == "nt":
        (m, _), n, k = a.shape, b2[0], b2[1] * max(pieces_b, 1)
    else:
        (k, m), n = a.shape, b2[1] * max(pieces_b, 1)
    n_unit = n // max(out_pieces, 1) if mode == "nt" or not pieces_b else n // pieces_b
    if out_pieces and pieces_b and mode != "nt":
        assert out_pieces == pieces_b
    k_unit = k // pieces_b if (pieces_b and mode == "nt") else k
    tm, tn = _tile(m, tm), _tile(n_unit, tn)
    tk = k_unit if k_unit <= full_k else _tile(k_unit, tk)
    nk = k // tk
    n_per, k_per = n_unit // tn, k_unit // tk
    a_bytes, b_bytes = a.size * a.dtype.itemsize, b.size * b.dtype.itemsize
    j_outer = nk == 1 and b_bytes + a_bytes * (n // tn) < a_bytes + b_bytes * (m // tm)
    ij = (lambda g0, g1: (g1, g0)) if j_outer else (lambda g0, g1: (g0, g1))

    def spec(shape, pick):
        return pl.BlockSpec(shape, lambda g0, g1, l: pick(*ij(g0, g1), l))

    a_spec = {"nn": spec((tm, tk), lambda i, j, l: (i, l)), "nt": spec((tm, tk), lambda i, j, l: (i, l)),
              "tn": spec((tk, tm), lambda i, j, l: (l, i))}[mode]
    if not pieces_b:
        b_spec = {"nn": spec((tk, tn), lambda i, j, l: (l, j)), "nt": spec((tn, tk), lambda i, j, l: (j, l)),
                  "tn": spec((tk, tn), lambda i, j, l: (l, j))}[mode]
    elif mode == "nt":
        b_spec = spec((None, tn, tk), lambda i, j, l: (l // k_per, j, l % k_per))
    else:
        b_spec = spec((None, tk, tn), lambda i, j, l: (j // n_per, l, j % n_per))
    if out_pieces:
        o_spec = spec((None, tm, tn), lambda i, j, l: (j // n_per, i, j % n_per))
        o_shape = SDS((out_pieces, m, n // out_pieces), out_dtype)
    else:
        o_spec, o_shape = spec((tm, tn), lambda i, j, l: (i, j)), SDS((m, n), out_dtype)
    dot = {"nn": _nn, "nt": _nt, "tn": _tn}[mode]

    def body_one(a_ref, b_ref, o_ref):
        o_ref[...] = dot(a_ref[...].astype(BF16), b_ref[...].astype(BF16)).astype(o_ref.dtype)

    def body_acc(a_ref, b_ref, o_ref, acc):
        l = pl.program_id(2)
        part = dot(a_ref[...].astype(BF16), b_ref[...].astype(BF16))

        @pl.when(l == 0)
        def _():
            acc[...] = part

        @pl.when((l > 0) & (l < nk - 1))
        def _():
            acc[...] += part

        @pl.when(l == nk - 1)
        def _():
            o_ref[...] = (acc[...] + part).astype(o_ref.dtype)

    return pl.pallas_call(
        body_one if nk == 1 else body_acc, grid=(n // tn, m // tm, nk) if j_outer else (m // tm, n // tn, nk),
        in_specs=[a_spec, b_spec], out_specs=o_spec, out_shape=o_shape,
        scratch_shapes=[] if nk == 1 else [pltpu.VMEM((tm, tn), F32)],
        compiler_params=_params("parallel", "parallel", "arbitrary"), name=name)(a, b)


def _norm_mod(name, h, nw, sh, sc, tr=512):
    s, d = h.shape
    tr = _tile(s, tr, 8)

    def body(h_ref, nw_ref, sh_ref, sc_ref, u_ref):
        x = h_ref[...]
        r = lax.rsqrt(jnp.mean(x * x, axis=-1, keepdims=True) + EPS)
        u_ref[...] = (x * r * nw_ref[...] * (1.0 + sc_ref[...]) + sh_ref[...]).astype(BF16)

    return pl.pallas_call(
        body, grid=(s // tr,), in_specs=[_row(tr, d), _vec(1, d), _vec(1, d), _vec(1, d)],
        out_specs=_row(tr, d), out_shape=SDS((s, d), BF16),
        compiler_params=_params("parallel"), name=name)(h, nw, sh, sc)


def _norm_mod_bwd(name, h, du, dh, nw, sc, tr=512):
    s, d = h.shape
    tr = _tile(s, tr, 8)

    def body(h_ref, du_ref, dh_ref, nw_ref, sc_ref, o_ref, acc_ref):
        @pl.when(pl.program_id(0) == 0)
        def _():
            acc_ref[...] = jnp.zeros_like(acc_ref)

        x, g = h_ref[...], du_ref[...]
        r = lax.rsqrt(jnp.mean(x * x, axis=-1, keepdims=True) + EPS)
        n = x * r
        nw, sc1 = nw_ref[...], 1.0 + sc_ref[...]
        dn = g * sc1 * nw
        o_ref[...] = dh_ref[...] + r * (dn - n * jnp.mean(dn * n, axis=-1, keepdims=True))
        gn = g * n
        acc_ref[0:1, :] += jnp.sum(g, axis=0, keepdims=True)
        acc_ref[1:2, :] += jnp.sum(gn * nw, axis=0, keepdims=True)
        acc_ref[2:3, :] += jnp.sum(gn * sc1, axis=0, keepdims=True)

    return pl.pallas_call(
        body, grid=(s // tr,),
        in_specs=[_row(tr, d), _row(tr, d), _row(tr, d), _vec(1, d), _vec(1, d)],
        out_specs=[_row(tr, d), _vec(8, d)], out_shape=[SDS((s, d), F32), SDS((8, d), F32)],
        compiler_params=_params("arbitrary"), name=name)(h, du, dh, nw, sc)


def _swiglu(name, ab, tr=256):
    s, f2 = ab.shape
    f = f2 // 2
    tr = _tile(s, tr, 8)

    def body(a_ref, b_ref, o_ref):
        o_ref[...] = (jax.nn.silu(a_ref[...].astype(F32)) * b_ref[...].astype(F32)).astype(BF16)

    return pl.pallas_call(
        body, grid=(s // tr,), in_specs=[_row(tr, f, 0), _row(tr, f, 1)], out_specs=_row(tr, f),
        out_shape=SDS((s, f), BF16), compiler_params=_params("parallel"), name=name)(ab, ab)


def _swiglu_bwd(name, ab, dhm, tr=256):
    s, f2 = ab.shape
    f = f2 // 2
    tr = _tile(s, tr, 8)

    def body(a_ref, b_ref, d_ref, o_ref):
        a, d = a_ref[...].astype(F32), d_ref[...].astype(F32)
        o_ref[:, 0:f] = (d * b_ref[...].astype(F32) * _silu_grad(a)).astype(BF16)
        o_ref[:, f:f2] = (d * jax.nn.silu(a)).astype(BF16)

    return pl.pallas_call(
        body, grid=(s // tr,), in_specs=[_row(tr, f, 0), _row(tr, f, 1), _row(tr, f)],
        out_specs=_row(tr, f2), out_shape=SDS((s, f2), BF16),
        compiler_params=_params("parallel"), name=name)(ab, ab, dhm)


def _resid(name, h, f, g, scale, tr=512):
    s, d = h.shape
    tr = _tile(s, tr, 8)

    def body(h_ref, f_ref, g_ref, o_ref):
        o_ref[...] = h_ref[...] + (scale * g_ref[...]) * f_ref[...]

    return pl.pallas_call(
        body, grid=(s // tr,), in_specs=[_row(tr, d), _row(tr, d), _vec(1, d)], out_specs=_row(tr, d),
        out_shape=SDS((s, d), F32), compiler_params=_params("parallel"), name=name)(h, f, g)


def _resid_bwd(name, dh, f, g, scale, tr=512):
    s, d = dh.shape
    tr = _tile(s, tr, 8)

    def body(dh_ref, f_ref, g_ref, o_ref, acc_ref):
        @pl.when(pl.program_id(0) == 0)
        def _():
            acc_ref[...] = jnp.zeros_like(acc_ref)

        x = dh_ref[...]
        o_ref[...] = ((scale * g_ref[...]) * x).astype(BF16)
        acc_ref[0:1, :] += jnp.sum(scale * x * f_ref[...], axis=0, keepdims=True)

    return pl.pallas_call(
        body, grid=(s // tr,), in_specs=[_row(tr, d), _row(tr, d), _vec(1, d)],
        out_specs=[_row(tr, d), _vec(8, d)], out_shape=[SDS((s, d), BF16), SDS((8, d), F32)],
        compiler_params=_params("arbitrary"), name=name)(dh, f, g)


def _final_loss(name, h, tgt, nw, tr=512):
    s, d = h.shape
    tr = _tile(s, tr, 8)

    def body(h_ref, t_ref, nw_ref, o_ref, acc_ref):
        @pl.when(pl.program_id(0) == 0)
        def _():
            acc_ref[...] = jnp.zeros_like(acc_ref)

        x, nw = h_ref[...], nw_ref[...]
        r = lax.rsqrt(jnp.mean(x * x, axis=-1, keepdims=True) + EPS)
        n = x * r
        diff = n * nw - t_ref[...]
        dy = diff * (1.0 / d)
        dn = dy * nw
        o_ref[...] = r * (dn - n * jnp.mean(dn * n, axis=-1, keepdims=True))
        acc_ref[0:1, :] += jnp.sum(dy * n, axis=0, keepdims=True)
        acc_ref[1:2, :] += jnp.sum(diff * diff, axis=0, keepdims=True) * (0.5 / d)

    return pl.pallas_call(
        body, grid=(s // tr,), in_specs=[_row(tr, d), _row(tr, d), _vec(1, d)],
        out_specs=[_row(tr, d), _vec(8, d)], out_shape=[SDS((s, d), F32), SDS((8, d), F32)],
        compiler_params=_params("arbitrary"), name=name)(h, tgt, nw)


class _Layout:
    def __init__(self, d, ca, nh):
        self.d, self.ca, self.nh = d, ca, nh
        self.qk = nh * HEAD
        self.qkv = 3 * self.qk
        self.z = self.qkv
        self.ga = self.z + self.qk
        self.cab = self.ga + 2 * d
        self.ba = self.cab + 3 * ca
        self.tail = _tile(self.ba, 512)
        self.total = self.ba + self.tail
        assert self.qkv % self.qk == 0 and self.ga % (2 * d) == 0 and self.cab % (3 * ca) == 0
        assert self.ba % self.tail == 0 and 4 * nh <= LANES

    def perm_cols(self, w):
        ca, qkv, qk, d, nh = self.ca, self.qkv, self.qk, self.d, self.nh
        o = [0, ca, 2 * ca, 3 * ca, 3 * ca + qkv, 3 * ca + qkv + qk, 3 * ca + qkv + qk + 4 * nh]
        cb, cc, cv = (w[..., o[i]:o[i + 1]] for i in range(3))
        x_qkv, x_z, x_ba = w[..., o[3]:o[4]], w[..., o[4]:o[5]], w[..., o[5]:o[6]]
        gates = w[..., o[6]:o[6] + 2 * d]
        pad = jnp.zeros(w.shape[:-1] + (self.tail - 4 * nh,), w.dtype)
        return jnp.concatenate([x_qkv, x_z, gates, cb, cc, cv, x_ba, pad], axis=-1)

    def unperm_cols(self, w):
        ca, nh = self.ca, self.nh
        cb, cc, cv = (w[..., self.cab + i * ca:self.cab + (i + 1) * ca] for i in range(3))
        return jnp.concatenate([cb, cc, cv, w[..., 0:self.qkv], w[..., self.z:self.ga],
                                w[..., self.ba:self.ba + 4 * nh], w[..., self.ga:self.cab]], axis=-1)


def _halo_specs(tr, w, cb, s):
    nb8 = s // 8
    return [pl.BlockSpec((8, w), lambda i: (jnp.maximum(i * (tr // 8) - 1, 0), cb)),
            pl.BlockSpec((tr, w), lambda i: (i, cb)),
            pl.BlockSpec((8, w), lambda i: (jnp.minimum((i + 1) * (tr // 8), nb8 - 1), cb))]


def _ext(prev_ref, main_ref, next_ref, i, nt):
    p = jnp.where(i > 0, prev_ref[...].astype(F32), 0.0)
    n = jnp.where(i < nt - 1, next_ref[...].astype(F32), 0.0)
    return jnp.concatenate([p, main_ref[...].astype(F32), n], axis=0)


def _shift(x, k):
    return x if k == 0 else pltpu.roll(x, (-k) % x.shape[0], 0)


def _conv_taps(x_ext, w, tr):
    kt = w.shape[0]
    acc = None
    for t in range(kt):
        term = _shift(x_ext, t - kt // 2)[8:8 + tr] * w[t:t + 1, :]
        acc = term if acc is None else acc + term
    return acc


def _prep_a(name, proj, conv_a, lay, tr=256):
    s, ca = proj.shape[0], lay.ca
    tr = _tile(s, tr, 8)
    nt, w = s // tr, 3 * ca

    def body(p_ref, m_ref, n_ref, w_ref, o_ref):
        x = _ext(p_ref, m_ref, n_ref, pl.program_id(0), nt)
        xv = x[:, ca:2 * ca] * x[:, 2 * ca:w]
        y = _conv_taps(xv, w_ref[...], tr)
        o_ref[...] = (m_ref[:, 0:ca] * y).astype(BF16)

    return pl.pallas_call(
        body, grid=(nt,), in_specs=_halo_specs(tr, w, lay.cab // w, s) + [_vec(conv_a.shape[0], ca)],
        out_specs=_row(tr, ca), out_shape=SDS((s, ca), BF16),
        compiler_params=_params("parallel"), name=name)(proj, proj, proj, conv_a)


def _prep_a_bwd(name, dya, proj, conv_a, dproj, lay, tr=256):
    s, ca = proj.shape[0], lay.ca
    tr = _tile(s, tr, 8)
    nt, w, kt = s // tr, 3 * ca, conv_a.shape[0]

    def body(p_ref, m_ref, n_ref, dp_ref, dm_ref, dn_ref, w_ref, _, o_ref, acc_ref):
        i = pl.program_id(0)

        @pl.when(i == 0)
        def _():
            acc_ref[...] = jnp.zeros_like(acc_ref)

        x = _ext(p_ref, m_ref, n_ref, i, nt)
        d_ext = _ext(dp_ref, dm_ref, dn_ref, i, nt)
        cb, cc, cv = x[:, 0:ca], x[:, ca:2 * ca], x[:, 2 * ca:w]
        xv = cc * cv
        wv = w_ref[...]
        dy_ext = d_ext * cb
        dx = None
        for t in range(kt):
            term = _shift(dy_ext, kt // 2 - t)[8:8 + tr] * wv[t:t + 1, :]
            dx = term if dx is None else dx + term
            acc_ref[t:t + 1, :] += jnp.sum(dy_ext[8:8 + tr] * _shift(xv, t - kt // 2)[8:8 + tr],
                                           axis=0, keepdims=True)
        y = _conv_taps(xv, wv, tr)
        o_ref[:, 0:ca] = (dm_ref[...] * y).astype(BF16)
        o_ref[:, ca:2 * ca] = (dx * cv[8:8 + tr]).astype(BF16)
        o_ref[:, 2 * ca:w] = (dx * cc[8:8 + tr]).astype(BF16)

    return pl.pallas_call(
        body, grid=(nt,),
        in_specs=_halo_specs(tr, w, lay.cab // w, s) + _halo_specs(tr, ca, 0, s)
        + [_vec(kt, ca), pl.BlockSpec(memory_space=pl.ANY)],
        out_specs=[_row(tr, w, lay.cab // w), _vec(8, ca)],
        out_shape=[SDS(dproj.shape, dproj.dtype), SDS((8, ca), F32)], input_output_aliases={7: 0},
        compiler_params=_params("arbitrary"), name=name)(proj, proj, proj, dya, dya, dya, conv_a, dproj)


def _qkv_act(c, nh, tr_rows):
    sact = jax.nn.silu(c)
    outs, inv = [], []
    for hd in range(3 * nh):
        sl = sact[:, hd * HEAD:(hd + 1) * HEAD]
        if hd < 2 * nh:
            r = lax.rsqrt(jnp.sum(sl * sl, axis=-1, keepdims=True) + EPS)
            inv.append(r)
            outs.append(sl * (r * (HEAD ** -0.5 if hd < nh else 1.0)))
        else:
            outs.append(sl)
    return jnp.concatenate(outs, axis=-1), sact, inv


def _prep_b(name, proj, conv_dn, lay, tr=256):
    s, w, nh = proj.shape[0], lay.qkv, lay.nh
    tr = _tile(s, tr, 8)
    nt = s // tr

    def body(p_ref, m_ref, n_ref, w_ref, o_ref):
        x = _ext(p_ref, m_ref, n_ref, pl.program_id(0), nt)
        c = _conv_taps(x, w_ref[...], tr)
        o_ref[...] = _qkv_act(c, nh, tr)[0]

    return pl.pallas_call(
        body, grid=(nt,), in_specs=_halo_specs(tr, w, 0, s) + [_vec(conv_dn.shape[0], w)],
        out_specs=_row(tr, w), out_shape=SDS((s, w), F32),
        compiler_params=_params("parallel"), name=name)(proj, proj, proj, conv_dn)


def _prep_b_bwd(name, dq, dk, dv, proj, conv_dn, dproj, lay, tr=256):
    s, w, nh, qk = proj.shape[0], lay.qkv, lay.nh, lay.qk
    tr = _tile(s, tr, 8)
    nt, kt = s // tr, conv_dn.shape[0]
    n_ext = tr + 16

    def body(*refs):
        x_refs, g_refs = refs[0:3], refs[3:12]
        w_ref, o_ref, acc_ref = refs[12], refs[14], refs[15]
        i = pl.program_id(0)

        @pl.when(i == 0)
        def _():
            acc_ref[...] = jnp.zeros_like(acc_ref)

        x = _ext(*x_refs, i, nt)
        wv = w_ref[...]
        c = None
        for t in range(kt):
            term = _shift(x, t - kt // 2) * wv[t:t + 1, :]
            c = term if c is None else c + term
        sact = jax.nn.silu(c)
        ds = []
        for part in range(3):
            g = _ext(*g_refs[3 * part:3 * part + 3], i, nt)
            for hd in range(nh):
                sl = sact[:, part * qk + hd * HEAD:part * qk + (hd + 1) * HEAD]
                gh = g[:, hd * HEAD:(hd + 1) * HEAD]
                if part < 2:
                    r = lax.rsqrt(jnp.sum(sl * sl, axis=-1, keepdims=True) + EPS)
                    sc = HEAD ** -0.5 if part == 0 else 1.0
                    ds.append(sc * r * (gh - sl * (r * r) * jnp.sum(gh * sl, axis=-1, keepdims=True)))
                else:
                    ds.append(gh)
        dc = jnp.concatenate(ds, axis=-1) * _silu_grad(c)
        rows = lax.broadcasted_iota(jnp.int32, (n_ext, 1), 0)
        dc = jnp.where((rows >= 2) & (rows < n_ext - 2), dc, 0.0)
        dx = None
        for t in range(kt):
            term = _shift(dc, kt // 2 - t)[8:8 + tr] * wv[t:t + 1, :]
            dx = term if dx is None else dx + term
            acc_ref[t:t + 1, :] += jnp.sum(dc[8:8 + tr] * _shift(x, t - kt // 2)[8:8 + tr],
                                           axis=0, keepdims=True)
        o_ref[...] = dx.astype(BF16)

    return pl.pallas_call(
        body, grid=(nt,),
        in_specs=_halo_specs(tr, w, 0, s) + _halo_specs(tr, qk, 0, s) * 3
        + [_vec(kt, w), pl.BlockSpec(memory_space=pl.ANY)],
        out_specs=[_row(tr, w, 0), _vec(8, w)],
        out_shape=[SDS(dproj.shape, dproj.dtype), SDS((8, w), F32)], input_output_aliases={13: 0},
        compiler_params=_params("arbitrary"), name=name)(
            proj, proj, proj, dq, dq, dq, dk, dk, dk, dv, dv, dv, conv_dn, dproj)


def _softplus(x):
    return jnp.maximum(x, 0.0) + jnp.log(1.0 + jnp.exp(-jnp.abs(x)))


def _split3(x):
    hi = x.astype(BF16)
    r = x - hi.astype(F32)
    mid = r.astype(BF16)
    return hi, mid, (r - mid.astype(F32)).astype(BF16)


def _exact_nn(m, x):
    m = m.astype(BF16)
    hi, mid, lo = _split3(x)
    return _nn(m, hi) + _nn(m, mid) + _nn(m, lo)


def _chunk_cumsum_masks(tr):
    ri = lax.broadcasted_iota(jnp.int32, (tr, tr), 0)
    ci = lax.broadcasted_iota(jnp.int32, (tr, tr), 1)
    same = (ri // CHUNK) == (ci // CHUNK)
    return (same & (ci <= ri)).astype(F32), (same & (ci >= ri)).astype(F32)


def _prep_c(name, proj, pvec, lay, tr=512):
    s, nh = proj.shape[0], lay.nh
    tr = _tile(s, tr, CHUNK)
    assert 6 * nh <= LANES

    def body(x_ref, p_ref, o_ref):
        x = x_ref[...]
        lane = lax.broadcasted_iota(jnp.int32, x.shape, 1)
        is_g = (lane >= 2 * nh) & (lane < 4 * nh)
        g = jnp.where(is_g, -jnp.exp(p_ref[0:1, :]) * _softplus(x + p_ref[1:2, :]), 0.0)
        m_f, m_b = _chunk_cumsum_masks(tr)
        gc = jnp.where(lane < 3 * nh, _exact_nn(m_f, g), _exact_nn(m_b, g))
        gc = pltpu.roll(gc, 2 * nh, 1)
        o_ref[...] = jnp.where(lane < 2 * nh, jax.nn.sigmoid(x), jnp.where(lane < 4 * nh, g, gc))

    return pl.pallas_call(
        body, grid=(s // tr,), in_specs=[_row(tr, LANES, lay.ba // LANES), _vec(8, LANES)],
        out_specs=_row(tr, LANES), out_shape=SDS((s, LANES), F32),
        compiler_params=_params("parallel"), name=name)(proj, pvec)


def _prep_c_bwd(name, dbg_f, dbg_b, proj, pvec, dproj, lay, tr=512):
    s, nh, tail = proj.shape[0], lay.nh, lay.tail
    tr = _tile(s, tr, CHUNK)

    def body(x_ref, df_ref, db_ref, p_ref, _, o_ref, acc_ref):
        @pl.when(pl.program_id(0) == 0)
        def _():
            acc_ref[...] = jnp.zeros_like(acc_ref)

        x = x_ref[...]
        lane = lax.broadcasted_iota(jnp.int32, x.shape, 1)
        is_b, is_g = lane < 2 * nh, (lane >= 2 * nh) & (lane < 4 * nh)
        fwd_lane = (lane < nh) | ((lane >= 2 * nh) & (lane < 3 * nh))
        d = jnp.where(lane < 4 * nh, jnp.where(fwd_lane, df_ref[...], db_ref[...]), 0.0)
        m_f, m_b = _chunk_cumsum_masks(tr)
        dgc = jnp.where(is_g, d, 0.0)
        dg = jnp.where(fwd_lane, _exact_nn(m_b, dgc), _exact_nn(m_f, dgc))
        sb = jax.nn.sigmoid(x)
        na = -jnp.exp(p_ref[0:1, :])
        xs = x + p_ref[1:2, :]
        dsp = dg * na * jax.nn.sigmoid(xs)
        dx = jnp.where(is_b, d * sb * (1.0 - sb), jnp.where(is_g, dsp, 0.0))
        o_ref[...] = jnp.zeros_like(o_ref)
        o_ref[:, 0:LANES] = dx.astype(BF16)
        acc_ref[0:1, :] += jnp.sum(jnp.where(is_g, dg * na * _softplus(xs), 0.0), axis=0, keepdims=True)
        acc_ref[1:2, :] += jnp.sum(jnp.where(is_g, dsp, 0.0), axis=0, keepdims=True)

    return pl.pallas_call(
        body, grid=(s // tr,),
        in_specs=[_row(tr, LANES, lay.ba // LANES), _row(tr, LANES), _row(tr, LANES), _vec(8, LANES),
                  pl.BlockSpec(memory_space=pl.ANY)],
        out_specs=[_row(tr, tail, lay.ba // tail), _vec(8, LANES)],
        out_shape=[SDS(dproj.shape, dproj.dtype), SDS((8, LANES), F32)], input_output_aliases={4: 0},
        compiler_params=_params("arbitrary"), name=name)(proj, dbg_f, dbg_b, pvec, dproj)


def _post(name, o_f, o_b, proj, dn_w, lay, tr=256):
    s, qk, nh = o_f.shape[0], lay.qk, lay.nh
    tr = _tile(s, tr, 8)

    def body(f_ref, b_ref, z_ref, w_ref, o_ref):
        o = f_ref[...] + b_ref[...]
        gate = jax.nn.silu(z_ref[...])
        for hd in range(nh):
            sl = slice(hd * HEAD, (hd + 1) * HEAD)
            oh = o[:, sl]
            r = lax.rsqrt(jnp.mean(oh * oh, axis=-1, keepdims=True) + EPS)
            o_ref[:, sl] = (oh * r * w_ref[...] * gate[:, sl]).astype(BF16)

    return pl.pallas_call(
        body, grid=(s // tr,),
        in_specs=[_row(tr, qk), _row(tr, qk), _row(tr, qk, lay.z // qk), _vec(1, HEAD)],
        out_specs=_row(tr, qk), out_shape=SDS((s, qk), BF16),
        compiler_params=_params("parallel"), name=name)(o_f, o_b, proj, dn_w)


def _post_bwd(name, dyb, o_f, o_b, proj, dn_w, dproj, lay, tr=256):
    s, qk, nh = o_f.shape[0], lay.qk, lay.nh
    tr = _tile(s, tr, 8)

    def body(d_ref, f_ref, b_ref, z_ref, w_ref, _, do_ref, dz_ref, acc_ref):
        @pl.when(pl.program_id(0) == 0)
        def _():
            acc_ref[...] = jnp.zeros_like(acc_ref)

        o, z, d, wv = f_ref[...] + b_ref[...], z_ref[...], d_ref[...], w_ref[...]
        gate = jax.nn.silu(z)
        dgate = _silu_grad(z)
        for hd in range(nh):
            sl = slice(hd * HEAD, (hd + 1) * HEAD)
            oh, dh = o[:, sl], d[:, sl]
            r = lax.rsqrt(jnp.mean(oh * oh, axis=-1, keepdims=True) + EPS)
            n = oh * r
            dz_ref[:, sl] = (dh * n * wv * dgate[:, sl]).astype(BF16)
            don = dh * gate[:, sl]
            acc_ref[0:1, :] += jnp.sum(don * n, axis=0, keepdims=True)
            dn = don * wv
            do_ref[:, sl] = r * (dn - n * jnp.mean(dn * n, axis=-1, keepdims=True))

    return pl.pallas_call(
        body, grid=(s // tr,),
        in_specs=[_row(tr, qk), _row(tr, qk), _row(tr, qk), _row(tr, qk, lay.z // qk), _vec(1, HEAD),
                  pl.BlockSpec(memory_space=pl.ANY)],
        out_specs=[_row(tr, qk), _row(tr, qk, lay.z // qk), _vec(8, HEAD)],
        out_shape=[SDS((s, qk), F32), SDS(dproj.shape, dproj.dtype), SDS((8, HEAD), F32)],
        input_output_aliases={5: 1},
        compiler_params=_params("arbitrary"), name=name)(dyb, o_f, o_b, proj, dn_w, dproj)


def _merge(name, pa, pb, proj, lay, tr=512):
    s, d = pa.shape
    tr = _tile(s, tr, 8)

    def body(a_ref, b_ref, g_ref, o_ref):
        o_ref[...] = (jax.nn.sigmoid(g_ref[:, 0:d]) * a_ref[...]
                      + jax.nn.sigmoid(g_ref[:, d:2 * d]) * b_ref[...]).astype(BF16)

    return pl.pallas_call(
        body, grid=(s // tr,), in_specs=[_row(tr, d), _row(tr, d), _row(tr, 2 * d, lay.ga // (2 * d))],
        out_specs=_row(tr, d), out_shape=SDS((s, d), BF16),
        compiler_params=_params("parallel"), name=name)(pa, pb, proj)


def _merge_bwd(name, dmg, pa, pb, proj, lay, tr=512):
    s, d = pa.shape
    tr = _tile(s, tr, 8)

    def body(d_ref, a_ref, b_ref, g_ref, da_ref, db_ref, dg_ref):
        dm = d_ref[...]
        sa, sb = jax.nn.sigmoid(g_ref[:, 0:d]), jax.nn.sigmoid(g_ref[:, d:2 * d])
        da_ref[...] = (sa * dm).astype(BF16)
        db_ref[...] = (sb * dm).astype(BF16)
        dg_ref[:, 0:d] = (dm * a_ref[...] * sa * (1.0 - sa)).astype(BF16)
        dg_ref[:, d:2 * d] = (dm * b_ref[...] * sb * (1.0 - sb)).astype(BF16)

    return pl.pallas_call(
        body, grid=(s // tr,),
        in_specs=[_row(tr, d), _row(tr, d), _row(tr, d), _row(tr, 2 * d, lay.ga // (2 * d))],
        out_specs=[_row(tr, d), _row(tr, d), _row(tr, 2 * d, lay.ga // (2 * d))],
        out_shape=[SDS((s, d), BF16), SDS((s, d), BF16), SDS((s, lay.total), BF16)],
        compiler_params=_params("parallel"), name=name)(dmg, pa, pb, proj)


def _tri_inverse(a_mat, ri, ci):
    def same(shift):
        return (ri >> shift) == (ci >> shift)

    x = -jnp.where(same(3), a_mat, 0.0)
    t_mat = (ri == ci).astype(F32) + x
    for _ in range(2):
        x = _bnn(x, x)
        t_mat = t_mat + _bnn(t_mat, x)
    for shift in (3, 4, 5):
        b = jnp.where(same(shift + 1) & ~same(shift), a_mat, 0.0)
        t_mat = t_mat - _bnn(_bnn(t_mat, b), t_mat)
    return t_mat


def _chunk_terms(q, k, v, beta, gc, g_row, g_last, reverse, t_mat=None):
    c = CHUNK
    ri = lax.broadcasted_iota(jnp.int32, (c, c), 0)
    ci = lax.broadcasted_iota(jnp.int32, (c, c), 1)
    if reverse:
        incl, strict = ri <= ci, ri < ci
    else:
        incl, strict = ri >= ci, ri > ci
    decay = jnp.where(incl, jnp.exp(jnp.where(incl, gc - g_row, 0.0)), 0.0)
    e = jnp.exp(gc)
    ed = jnp.exp(g_last - gc)
    el = jnp.exp(g_last)
    kb = k * beta
    a_mat = jnp.where(strict, _bnt(kb, k) * decay, 0.0)
    if t_mat is None:
        t_mat = _tri_inverse(a_mat, ri, ci)
    u = _bnn(t_mat, v * beta)
    w = _bnn(t_mat, kb * e)
    p_mat = jnp.where(incl, _bnt(q, k) * decay, 0.0)
    return dict(incl=incl, strict=strict, decay=decay, e=e, ed=ed, el=el, kb=kb,
                a=a_mat, t=t_mat, u=u, w=w, p=p_mat)


def _delta_specs(nh, tb, nb, reverse):
    tok = (lambda i: nb - 1 - i) if reverse else (lambda i: i)
    hw = nh * HEAD
    qkv = [pl.BlockSpec((tb, hw), functools.partial(lambda i, part: (tok(i), part), part=p)) for p in range(3)]
    rows = pl.BlockSpec((tb, hw), lambda i: (tok(i), 0))
    bg = pl.BlockSpec((tb, LANES), lambda i: (tok(i), 0))
    gct = pl.BlockSpec((2 * nh, tb), lambda i: (0, tok(i)))
    st = pl.BlockSpec((nh, tb // CHUNK, HEAD, HEAD), lambda i: (0, tok(i), 0, 0))
    tri = pl.BlockSpec((nh, tb // CHUNK, CHUNK, CHUNK), lambda i: (0, tok(i), 0, 0))
    return qkv, rows, bg, gct, st, tri


def _heads(ref, rows, nh):
    return jnp.stack([ref[rows, hd * HEAD:(hd + 1) * HEAD] for hd in range(nh)])


def _chunk_scalars(bg_ref, gct_ref, cj, nh, tb, reverse):
    rows = pl.ds(cj * CHUNK, CHUNK)
    lb = nh if reverse else 0
    lc = 4 * nh + lb
    last = cj * CHUNK + (0 if reverse else CHUNK - 1)
    g_lanes = gct_ref[lb:lb + nh, :]
    if cj:
        g_lanes = pltpu.roll(g_lanes, tb - cj * CHUNK, 1)
    col = lambda l0, r: jnp.stack([bg_ref[r, l0 + hd:l0 + hd + 1] for hd in range(nh)])
    return col(lb, rows), col(lc, rows), g_lanes[:, 0:CHUNK][:, None, :], col(lc, pl.ds(last, 1))


def _delta_fwd(name, qkvn, bg, gct, nh, reverse, tb=128):
    s = qkvn.shape[0]
    tb = _tile(s, tb, LANES)
    nb, cpb = s // tb, tb // CHUNK
    qkv, rows_spec, bg_spec, gct_spec, st, tri = _delta_specs(nh, tb, nb, reverse)

    def body(q_ref, k_ref, v_ref, bg_ref, gct_ref, o_ref, st_ref, tri_ref, state):
        @pl.when(pl.program_id(0) == 0)
        def _():
            state[...] = jnp.zeros_like(state)

        for cj in (range(cpb - 1, -1, -1) if reverse else range(cpb)):
            rows = pl.ds(cj * CHUNK, CHUNK)
            q, k, v = _heads(q_ref, rows, nh), _heads(k_ref, rows, nh), _heads(v_ref, rows, nh)
            tm = _chunk_terms(q, k, v, *_chunk_scalars(bg_ref, gct_ref, cj, nh, tb, reverse), reverse)
            s_in = state[...]
            st_ref[:, cj] = s_in
            tri_ref[:, cj] = tm["t"]
            vn = tm["u"] - _bnn(tm["w"], s_in)
            o = _bnn(q * tm["e"], s_in) + _bnn(tm["p"], vn)
            for hd in range(nh):
                o_ref[rows, hd * HEAD:(hd + 1) * HEAD] = o[hd]
            state[...] = s_in * tm["el"] + _btn(k * tm["ed"], vn)

    return pl.pallas_call(
        body, grid=(nb,), in_specs=qkv + [bg_spec, gct_spec], out_specs=[rows_spec, st, tri],
        out_shape=[SDS((s, nh * HEAD), F32), SDS((nh, s // CHUNK, HEAD, HEAD), F32),
                   SDS((nh, s // CHUNK, CHUNK, CHUNK), F32)],
        scratch_shapes=[pltpu.VMEM((nh, HEAD, HEAD), F32)],
        compiler_params=_params("arbitrary"), name=name)(qkvn, qkvn, qkvn, bg, gct)


def _delta_bwd(name, qkvn, bg, gct, do, states, tris, nh, reverse, add=None, tb=128):
    s = qkvn.shape[0]
    tb = _tile(s, tb, LANES)
    nb, cpb = s // tb, tb // CHUNK
    qkv, rows_spec, bg_spec, gct_spec, st, tri = _delta_specs(nh, tb, nb, not reverse)
    n_add = 0 if add is None else 3

    def body(*refs):
        q_ref, k_ref, v_ref, bg_ref, gct_ref, do_ref, st_ref, tri_ref = refs[0:8]
        add_refs = refs[8:8 + n_add]
        dq_ref, dk_ref, dv_ref, dbg_ref, dstate = refs[8 + n_add:]

        @pl.when(pl.program_id(0) == 0)
        def _():
            dstate[...] = jnp.zeros_like(dstate)

        ones = jnp.ones((nh, CHUNK, HEAD), BF16)
        row_id = lax.broadcasted_iota(jnp.int32, (CHUNK, 1), 0)
        rsum = lambda x: jnp.sum(x, axis=2, keepdims=True)
        for cj in (range(cpb) if reverse else range(cpb - 1, -1, -1)):
            rows = pl.ds(cj * CHUNK, CHUNK)
            q, k, v, d_o = (_heads(r, rows, nh) for r in (q_ref, k_ref, v_ref, do_ref))
            beta, gc, g_row, g_last = _chunk_scalars(bg_ref, gct_ref, cj, nh, tb, reverse)
            tm = _chunk_terms(q, k, v, beta, gc, g_row, g_last, reverse, t_mat=tri_ref[:, cj])
            incl, strict, e, ed, el, kb = tm["incl"], tm["strict"], tm["e"], tm["ed"], tm["el"], tm["kb"]
            t_mat, u, w, p_mat, decay = tm["t"], tm["u"], tm["w"], tm["p"], tm["decay"]
            s_in, ds_out = st_ref[:, cj], dstate[...]
            vn = u - _bnn(w, s_in)
            qe, kd, ke = q * e, k * ed, kb * e
            dvn = _btn(p_mat, d_o) + _bnn(kd, ds_out)
            dqe = _bnt(d_o, s_in)
            dq = dqe * e
            dgc = rsum(dqe * qe)
            dp = jnp.where(incl, _bnt(d_o, vn), 0.0)
            dkd = _bnt(vn, ds_out)
            dk = dkd * ed
            r = rsum(dkd * kd)
            dgc = dgc - r
            dg_last = (jnp.sum(r, axis=1, keepdims=True)
                       + jnp.sum(rsum(ds_out * s_in), axis=1, keepdims=True) * el)
            dw = -_bnt(dvn, s_in)
            dbv = _btn(t_mat, dvn)
            dke = _btn(t_mat, dw)
            da = -jnp.where(strict, _bnt(dbv, u) + _bnt(dke, w), 0.0)
            m_mat, n_mat = da * decay, dp * decay
            dkb = _bnn(m_mat, k) + dke * e
            dk = dk + _btn(m_mat, kb) + _btn(n_mat, q)
            dq = dq + _bnn(n_mat, k)
            g_mat = da * tm["a"] + dp * p_mat
            g_hi, g_mid, g_lo = _split3(g_mat)
            col = (_btn(g_hi, ones) + _btn(g_mid, ones) + _btn(g_lo, ones))[:, :, 0:1]
            dgc = dgc + rsum(g_mat) - col + rsum(dke * ke)
            dgc = dgc + jnp.where(row_id == (0 if reverse else CHUNK - 1), dg_last, 0.0)
            dv = dbv * beta
            dbeta = rsum(dbv * v) + rsum(dkb * k)
            dk = dk + dkb * beta
            dstate[...] = el * ds_out + _btn(qe, d_o) - _btn(w, dvn)
            lb = nh if reverse else 0
            for hd in range(nh):
                cols = slice(hd * HEAD, (hd + 1) * HEAD)
                extra = [a[rows, cols] for a in add_refs] if n_add else [0.0, 0.0, 0.0]
                dq_ref[rows, cols] = dq[hd] + extra[0]
                dk_ref[rows, cols] = dk[hd] + extra[1]
                dv_ref[rows, cols] = dv[hd] + extra[2]
                dbg_ref[rows, lb + hd:lb + hd + 1] = dbeta[hd]
                dbg_ref[rows, 2 * nh + lb + hd:2 * nh + lb + hd + 1] = dgc[hd]

    out3 = SDS((s, nh * HEAD), F32)
    return pl.pallas_call(
        body, grid=(nb,), in_specs=qkv + [bg_spec, gct_spec, rows_spec, st, tri] + [rows_spec] * n_add,
        out_specs=[rows_spec, rows_spec, rows_spec, bg_spec], out_shape=[out3, out3, out3, SDS((s, LANES), F32)],
        scratch_shapes=[pltpu.VMEM((nh, HEAD, HEAD), F32)],
        compiler_params=_params("arbitrary"), name=name)(qkvn, qkvn, qkvn, bg, gct, do, states, tris, *(add or ()))


def _row_pieces(g):
    return g.reshape(N_CHIPS, g.shape[0] // N_CHIPS, g.shape[1])


def _ffn_fwd(tag, h, nw, sh, sc, g, w_up, w_down):
    u = _norm_mod(tag + "_norm", h, nw, sh, sc)
    ab = _matmul(tag + "_up", u, w_up, "nn", out_dtype=BF16, tn=1408)
    hm = _swiglu(tag + "_act", ab)
    f = _matmul(tag + "_down", hm, w_down, "nn")
    return _resid(tag + "_res", h, f, g, 0.5), (h, u, ab, hm, f)


def _ffn_bwd(tag, dh, saved, nw, sc, g, w_up, w_down):
    h, u, ab, hm, f = saved
    df, acc_g = _resid_bwd(tag + "_res_bwd", dh, f, g, 0.5)
    gw_down = _matmul(tag + "_gw_down", hm, df, "tn", out_dtype=BF16, tm=1408)
    dhm = _matmul(tag + "_dhm", df, w_down, "nt", out_dtype=BF16, tn=2816)
    dab = _swiglu_bwd(tag + "_act_bwd", ab, dhm)
    gw_up = _matmul(tag + "_gw_up", u, dab, "tn", out_dtype=BF16, tn=1408, out_pieces=N_CHIPS)
    du = _matmul(tag + "_du", dab, w_up, "nt")
    dh_in, acc = _norm_mod_bwd(tag + "_norm_bwd", h, du, dh, nw, sc)
    return dh_in, gw_up, _row_pieces(gw_down), (acc[0], acc[1], acc_g[0], acc[2])


def _mixer_fwd(h, nw, sh, sc, g, wt, lay):
    nh = lay.nh
    u = _norm_mod("mix_norm", h, nw, sh, sc)
    proj = _matmul("mix_in", u, wt["w_in"], "nn")
    qkvn = _prep_b("mix_prep_b", proj, wt["conv_dn"], lay)
    ya = _prep_a("mix_prep_a", proj, wt["conv_a"], lay)
    bg = _prep_c("mix_prep_c", proj, wt["pvec"], lay)
    gct = bg[:, 4 * nh:6 * nh].T
    o_f, *st_f = _delta_fwd("delta_fwd_l2r", qkvn, bg, gct, nh, False)
    o_b, *st_b = _delta_fwd("delta_fwd_r2l", qkvn, bg, gct, nh, True)
    yb = _post("mix_post", o_f, o_b, proj, wt["dn_norm"], lay)
    pa = _matmul("mix_a_out", ya, wt["w_a_out"], "nn")
    pb = _matmul("mix_b_out", yb, wt["w_b_out"], "nn")
    mg = _merge("mix_merge", pa, pb, proj, lay)
    y = _matmul("mix_out", mg, wt["w_out"], "nn")
    h2 = _resid("mix_res", h, y, g, 1.0)
    return h2, (h, u, proj, qkvn, ya, bg, gct, o_f, o_b, st_f, st_b, yb, pa, pb, mg, y)


def _mixer_bwd(dh, saved, nw, sc, g, wt, lay):
    h, u, proj, qkvn, ya, bg, gct, o_f, o_b, st_f, st_b, yb, pa, pb, mg, y = saved
    nh = lay.nh
    dy, acc_g = _resid_bwd("mix_res_bwd", dh, y, g, 1.0)
    gw_out = _matmul("mix_gw_out", mg, dy, "tn", out_dtype=BF16)
    dmg = _matmul("mix_dmg", dy, wt["w_out"], "nt")
    dpa, dpb, dproj = _merge_bwd("mix_merge_bwd", dmg, pa, pb, proj, lay)
    gw_a = _matmul("mix_gw_a", ya, dpa, "tn", out_dtype=BF16, out_pieces=N_CHIPS)
    gw_b = _matmul("mix_gw_b", yb, dpb, "tn", out_dtype=BF16)
    dya = _matmul("mix_dya", dpa, wt["w_a_out"], "nt")
    dyb = _matmul("mix_dyb", dpb, wt["w_b_out"], "nt")
    do, dproj, acc_dn = _post_bwd("mix_post_bwd", dyb, o_f, o_b, proj, wt["dn_norm"], dproj, lay)
    dq, dk, dv, dbg_f = _delta_bwd("delta_bwd_l2r", qkvn, bg, gct, do, *st_f, nh, False)
    dq, dk, dv, dbg_b = _delta_bwd("delta_bwd_r2l", qkvn, bg, gct, do, *st_b, nh, True, add=(dq, dk, dv))
    dproj, acc_ca = _prep_a_bwd("mix_prep_a_bwd", dya, proj, wt["conv_a"], dproj, lay)
    dproj, acc_cd = _prep_b_bwd("mix_prep_b_bwd", dq, dk, dv, proj, wt["conv_dn"], dproj, lay)
    dproj, acc_pc = _prep_c_bwd("mix_prep_c_bwd", dbg_f, dbg_b, proj, wt["pvec"], dproj, lay)
    gw_in = lay.unperm_cols(_matmul("mix_gw_in", u, dproj, "tn", out_dtype=BF16))
    gw_in = gw_in.reshape(gw_in.shape[0], N_CHIPS, -1).transpose(1, 0, 2)
    du = _matmul("mix_du", dproj, wt["w_in"], "nt")
    dh_in, acc = _norm_mod_bwd("mix_norm_bwd", h, du, dh, nw, sc)
    small = dict(conv_a=acc_ca[0:wt["conv_a"].shape[0]], conv_dn=acc_cd[0:wt["conv_dn"].shape[0]],
                 dn_norm=acc_dn[0:1], a_log=acc_pc[0], dt_bias=acc_pc[1])
    big = dict(w_in=gw_in, w_a_out=gw_a, w_b_out=_row_pieces(gw_b), w_out=_row_pieces(gw_out))
    return dh_in, big, small, (acc[0], acc[1], acc_g[0], acc[2])


def _local_step(x, tgt, modv, wt, lay):
    m = [modv[i:i + 1] for i in range(9)]
    h1, sv1 = _ffn_fwd("ffn1", x, wt["norm_ffn1"], m[0], m[1], m[2], wt["w_ffn1_up"], wt["w_ffn1_down"])
    h2, sv2 = _mixer_fwd(h1, wt["norm_mix"], m[3], m[4], m[5], wt, lay)
    h3, sv3 = _ffn_fwd("ffn2", h2, wt["norm_ffn2"], m[6], m[7], m[8], wt["w_ffn2_up"], wt["w_ffn2_down"])
    dh3, acc_f = _final_loss("final_loss", h3, tgt, wt["norm_final"])
    loss = jnp.sum(acc_f[1])
    dh2, gu2, gd2, dm3 = _ffn_bwd("ffn2", dh3, sv3, wt["norm_ffn2"], m[7], m[8], wt["w_ffn2_up"], wt["w_ffn2_down"])
    dh1, gmix, small, dm2 = _mixer_bwd(dh2, sv2, wt["norm_mix"], m[4], m[5], wt, lay)
    dx, gu1, gd1, dm1 = _ffn_bwd("ffn1", dh1, sv1, wt["norm_ffn1"], m[1], m[2], wt["w_ffn1_up"], wt["w_ffn1_down"])
    dmod = jnp.stack([dm1[0], dm1[1], dm1[2], dm2[0], dm2[1], dm2[2], dm3[0], dm3[1], dm3[2]])
    big = dict(w_ffn1_up=gu1, w_ffn1_down=gd1, w_ffn2_up=gu2, w_ffn2_down=gd2, **gmix)
    small = dict(small, norm_ffn1=dm1[3], norm_mix=dm2[3], norm_ffn2=dm3[3], norm_final=acc_f[0])
    return loss, dx, dmod, big, small


def _position():
    return lax.axis_index("x"), lax.axis_index("y"), lax.axis_index("c")


_ANY = pl.BlockSpec(memory_space=pl.ANY)
_VMEM = pl.BlockSpec(memory_space=pltpu.VMEM)


def _allgather8(name, v):
    r = v.shape[0]

    def body(v_ref, out_ref, send_sems, recv_sems):
        x, y, c = _position()
        me = 4 * x + 2 * y + c
        out_ref[me] = v_ref[...]
        copies = []
        for mask in range(1, N_DEV):
            peer = tuple(1 - p if mask >> b & 1 else p for p, b in ((x, 2), (y, 1), (c, 0)))
            cp = pltpu.make_async_remote_copy(
                src_ref=v_ref, dst_ref=out_ref.at[me], send_sem=send_sems.at[mask - 1],
                recv_sem=recv_sems.at[mask - 1], device_id=peer, device_id_type=MESH)
            cp.start()
            copies.append(cp)
        for cp in copies:
            cp.wait()

    return pl.pallas_call(
        body, in_specs=[_VMEM], out_specs=_VMEM, out_shape=SDS((N_DEV, r, LANES), F32),
        scratch_shapes=[pltpu.SemaphoreType.DMA((N_DEV - 1,)), pltpu.SemaphoreType.DMA((N_DEV - 1,))],
        name=name)(v)


def _other_chips(x, y):
    return [(1 - x, y), (x, 1 - y), (1 - x, 1 - y)]


def _half_rows(c, rows):
    hr = rows // 2
    assert hr % 16 == 0
    return pl.ds(pl.multiple_of(c * hr, 16), hr)


def _gather_shards(name, shards):
    nw = len(shards)

    def body(*refs):
        ins, outs = refs[0:nw], refs[nw:2 * nw]
        send_sems, recv_sems, local_sems = refs[2 * nw:]
        x, y, c = _position()
        p = 2 * x + y
        chips = _other_chips(x, y)

        def copy(w, k, chip_index, rows, to, src=None):
            dst = outs[w].at[chip_index, rows]
            return pltpu.make_async_remote_copy(
                src_ref=dst if src is None else src, dst_ref=dst, send_sem=send_sems.at[k * nw + w],
                recv_sem=recv_sems.at[k * nw + w], device_id=to, device_id_type=MESH)

        half = [_half_rows(c, r.shape[0]) for r in ins]
        other = [_half_rows(1 - c, r.shape[0]) for r in ins]
        mine = [pltpu.make_async_copy(ins[w], outs[w].at[p], local_sems.at[w]) for w in range(nw)]
        for cp in mine:
            cp.start()
        first = [copy(w, j, p, half[w], (cx, cy, c), src=ins[w].at[half[w]])
                 for j, (cx, cy) in enumerate(chips) for w in range(nw)]
        for cp in first:
            cp.start()
        passed = []
        for j, (cx, cy) in enumerate(chips):
            for w in range(nw):
                copy(w, j, 2 * cx + cy, half[w], (x, y, c)).wait_recv()
                cp = copy(w, 3 + j, 2 * cx + cy, half[w], (x, y, 1 - c))
                cp.start()
                passed.append(cp)
        for j, (cx, cy) in enumerate(chips):
            for w in range(nw):
                copy(w, 3 + j, 2 * cx + cy, other[w], (x, y, c)).wait_recv()
        for cp in first + passed:
            cp.wait_send()
        for cp in mine:
            cp.wait()

    return pl.pallas_call(
        body, in_specs=[_ANY] * nw, out_specs=[_ANY] * nw,
        out_shape=[SDS((N_CHIPS,) + v.shape, v.dtype) for v in shards],
        scratch_shapes=[pltpu.SemaphoreType.DMA((6 * nw,)), pltpu.SemaphoreType.DMA((6 * nw,)),
                        pltpu.SemaphoreType.DMA((nw,))],
        name=name)(*shards)


def _swap_halves(name, gs):
    nw = len(gs)

    def body(*refs):
        ins, outs = refs[0:nw], refs[nw:2 * nw]
        send_sems, recv_sems = refs[2 * nw:]
        x, y, c = _position()
        copies = []
        for w in range(nw):
            cp = pltpu.make_async_remote_copy(
                src_ref=ins[w].at[:, _half_rows(1 - c, ins[w].shape[1])], dst_ref=outs[w],
                send_sem=send_sems.at[w], recv_sem=recv_sems.at[w], device_id=(x, y, 1 - c), device_id_type=MESH)
            cp.start()
            copies.append(cp)
        for cp in copies:
            cp.wait()

    return pl.pallas_call(
        body, in_specs=[_ANY] * nw, out_specs=[_ANY] * nw,
        out_shape=[SDS((g.shape[0], g.shape[1] // 2, g.shape[2]), g.dtype) for g in gs],
        scratch_shapes=[pltpu.SemaphoreType.DMA((nw,)), pltpu.SemaphoreType.DMA((nw,))], name=name)(*gs)


def _scatter_chips(name, vs):
    nw = len(vs)

    def body(*refs):
        ins, outs = refs[0:nw], refs[nw:2 * nw]
        send_sems, recv_sems = refs[2 * nw:]
        x, y, c = _position()
        copies = []
        for j, (cx, cy) in enumerate(_other_chips(x, y)):
            for w in range(nw):
                cp = pltpu.make_async_remote_copy(
                    src_ref=ins[w].at[2 * cx + cy], dst_ref=outs[w].at[j], send_sem=send_sems.at[j * nw + w],
                    recv_sem=recv_sems.at[j * nw + w], device_id=(cx, cy, c), device_id_type=MESH)
                cp.start()
                copies.append(cp)
        for cp in copies:
            cp.wait()

    return pl.pallas_call(
        body, in_specs=[_ANY] * nw, out_specs=[_ANY] * nw,
        out_shape=[SDS((N_CHIPS - 1,) + v.shape[1:], v.dtype) for v in vs],
        scratch_shapes=[pltpu.SemaphoreType.DMA((3 * nw,)), pltpu.SemaphoreType.DMA((3 * nw,))], name=name)(*vs)


def _share_halves(name, fulls):
    nw = len(fulls)

    def body(*refs):
        outs = refs[nw:2 * nw]
        send_sems, recv_sems = refs[2 * nw:]
        x, y, c = _position()
        copies = []
        for w in range(nw):
            rows = outs[w].at[_half_rows(c, outs[w].shape[0])]
            cp = pltpu.make_async_remote_copy(
                src_ref=rows, dst_ref=rows, send_sem=send_sems.at[w], recv_sem=recv_sems.at[w],
                device_id=(x, y, 1 - c), device_id_type=MESH)
            cp.start()
            copies.append(cp)
        for cp in copies:
            cp.wait()

    return pl.pallas_call(
        body, in_specs=[_ANY] * nw, out_specs=[_ANY] * nw, out_shape=[SDS(f.shape, f.dtype) for f in fulls],
        input_output_aliases={w: w for w in range(nw)},
        scratch_shapes=[pltpu.SemaphoreType.DMA((nw,)), pltpu.SemaphoreType.DMA((nw,))], name=name)(*fulls)


def _chip_sum(name, g, from_sib):
    _, r, cdim = g.shape
    hr = r // 2

    def body(g_ref, s_ref, o_ref):
        o_ref[...] = (g_ref[...].astype(F32) + s_ref[...].astype(F32)).astype(o_ref.dtype)

    blk = (None, hr, cdim)
    return pl.pallas_call(
        body, grid=(N_CHIPS,),
        in_specs=[pl.BlockSpec(blk, lambda j: (j, lax.axis_index("c"), 0)), pl.BlockSpec(blk, lambda j: (j, 0, 0))],
        out_specs=pl.BlockSpec(blk, lambda j: (j, 0, 0)),
        out_shape=SDS((N_CHIPS, hr, cdim), g.dtype), compiler_params=_params("parallel"), name=name)(g, from_sib)


def _total(name, g, from_sib, from_chips):
    _, r, cdim = g.shape
    hr = r // 2
    tr = _tile(hr, 256, 16)
    nt = hr // tr

    def body(g_ref, s_ref, rc_ref, o_ref):
        acc = g_ref[...].astype(F32) + s_ref[...].astype(F32)
        for j in range(N_CHIPS - 1):
            acc = acc + rc_ref[j].astype(F32)
        o_ref[...] = acc

    def chip():
        return 2 * lax.axis_index("x") + lax.axis_index("y")

    blk = (None, tr, cdim)
    return pl.pallas_call(
        body, grid=(nt,),
        in_specs=[pl.BlockSpec(blk, lambda i: (chip(), lax.axis_index("c") * nt + i, 0)),
                  pl.BlockSpec(blk, lambda i: (chip(), i, 0)),
                  pl.BlockSpec((N_CHIPS - 1, tr, cdim), lambda i: (0, i, 0))],
        out_specs=pl.BlockSpec((tr, cdim), lambda i: (lax.axis_index("c") * nt + i, 0)),
        out_shape=SDS((r, cdim), F32), compiler_params=_params("parallel"), name=name)(g, from_sib, from_chips)


def _sum8(name, v):
    _, r, w = v.shape

    def body(v_ref, o_ref):
        acc = v_ref[0]
        for j in range(1, N_DEV):
            acc = acc + v_ref[j]
        o_ref[...] = acc

    return pl.pallas_call(body, in_specs=[_VMEM], out_specs=_VMEM, out_shape=SDS((r, w), F32), name=name)(v)


def _adamw(name, w, g, m, v, tr=256):
    r, cdim = w.shape
    tr = _tile(r, tr, 8)
    bc1, bc2 = 1.0 - ADAM_B1 ** ADAM_STEP, 1.0 - ADAM_B2 ** ADAM_STEP

    def body(w_ref, g_ref, m_ref, v_ref, d_ref, nm_ref, nv_ref):
        g = g_ref[...]
        m2 = ADAM_B1 * m_ref[...] + (1.0 - ADAM_B1) * g
        v2 = ADAM_B2 * v_ref[...] + (1.0 - ADAM_B2) * (g * g)
        d_ref[...] = -ADAM_LR * ((m2 / bc1) / (jnp.sqrt(v2 / bc2) + ADAM_EPS) + ADAM_WD * w_ref[...])
        nm_ref[...] = m2
        nv_ref[...] = v2

    spec = _row(tr, cdim)
    out = SDS((r, cdim), F32)
    return pl.pallas_call(body, grid=(r // tr,), in_specs=[spec] * 4, out_specs=[spec] * 3, out_shape=[out] * 3,
                          compiler_params=_params("parallel"), name=name)(w, g, m, v)


def _pack_rows(arrays, width, row_mult, dtype):
    parts, spans, row = [], [], 0
    for a in arrays:
        n = a.size
        rows = -(-n // width)
        flat = a.reshape(-1).astype(dtype)
        if rows * width != n:
            flat = jnp.concatenate([flat, jnp.zeros((rows * width - n,), dtype)])
        parts.append(flat.reshape(rows, width))
        spans.append((row, rows, n, a.shape))
        row += rows
    pad = -row % row_mult
    if pad:
        parts.append(jnp.zeros((pad, width), dtype))
    return jnp.concatenate(parts, axis=0), spans


def _unpack_rows(packed, spans):
    return [packed[r0:r0 + rows].reshape(-1)[0:n].reshape(shape) for r0, rows, n, shape in spans]


BIG = ("w_ffn1_up", "w_ffn1_down", "w_in", "w_a_out", "w_b_out", "w_out", "w_ffn2_up", "w_ffn2_down")
COL_SHARDED = ("w_ffn1_up", "w_in", "w_a_out", "w_ffn2_up")
SMALL = ("b_ada", "norm_ffn1", "norm_mix", "conv_a", "conv_dn", "a_log_fwd", "dt_bias_fwd", "a_log_bwd",
         "dt_bias_bwd", "dn_norm", "norm_ffn2", "norm_final")
WEIGHTS = ("w_ada", "b_ada", "norm_ffn1", "w_ffn1_up", "w_ffn1_down", "norm_mix", "w_in", "conv_a", "conv_dn",
           "a_log_fwd", "dt_bias_fwd", "a_log_bwd", "dt_bias_bwd", "dn_norm", "w_a_out", "w_b_out", "w_out",
           "norm_ffn2", "w_ffn2_up", "w_ffn2_down", "norm_final")


def kernel(x, c, w_ada, b_ada, norm_ffn1, w_ffn1_up, w_ffn1_down, norm_mix, w_in, conv_a, conv_dn, a_log_fwd, dt_bias_fwd, a_log_bwd, dt_bias_bwd, dn_norm, w_a_out, w_b_out, w_out, norm_ffn2, w_ffn2_up, w_ffn2_down, norm_final, loss_target, m_w_ada, m_b_ada, m_norm_ffn1, m_w_ffn1_up, m_w_ffn1_down, m_norm_mix, m_w_in, m_conv_a, m_conv_dn, m_a_log_fwd, m_dt_bias_fwd, m_a_log_bwd, m_dt_bias_bwd, m_dn_norm, m_w_a_out, m_w_b_out, m_w_out, m_norm_ffn2, m_w_ffn2_up, m_w_ffn2_down, m_norm_final, v_w_ada, v_b_ada, v_norm_ffn1, v_w_ffn1_up, v_w_ffn1_down, v_norm_mix, v_w_in, v_conv_a, v_conv_dn, v_a_log_fwd, v_dt_bias_fwd, v_a_log_bwd, v_dt_bias_bwd, v_dn_norm, v_w_a_out, v_w_b_out, v_w_out, v_norm_ffn2, v_w_ffn2_up, v_w_ffn2_down, v_norm_final):
    given = dict(locals())
    wsh = {n: given[n] for n in WEIGHTS}
    msh = {n: given["m_" + n] for n in WEIGHTS}
    vsh = {n: given["v_" + n] for n in WEIGHTS}
    d = x.shape[-1]
    ca = conv_a.shape[-1] * N_CHIPS
    nh = conv_dn.shape[-1] * N_CHIPS // (3 * HEAD)
    lay = _Layout(d, ca, nh)
    xi, yi, ci = _position()
    chip = 2 * xi + yi
    me = 2 * chip + ci

    c_act = jax.nn.silu(c)
    g1, g1_spans = _pack_rows([c_act, conv_a[0], conv_dn[0]], LANES, 8, F32)
    g1_all = _allgather8("gather_cond", g1)
    per_dev = [_unpack_rows(g1_all[k], g1_spans) for k in range(N_DEV)]
    c_all = jnp.concatenate([p[0] for p in per_dev], axis=0)
    conv_a_full = jnp.concatenate([per_dev[2 * k][1] for k in range(N_CHIPS)], axis=1)
    conv_dn_full = jnp.concatenate([per_dev[2 * k][2] for k in range(N_CHIPS)], axis=1)

    mod_sh = _matmul("ada_mod", c_all, w_ada[0], "nn")
    b_sh = lax.dynamic_slice_in_dim(b_ada, chip * mod_sh.shape[1], mod_sh.shape[1], axis=1)
    g2, g2_spans = _pack_rows([mod_sh + b_sh], LANES, 8, F32)
    g2_all = _allgather8("gather_mod", g2)
    mod_all = jnp.concatenate([_unpack_rows(g2_all[2 * k], g2_spans)[0] for k in range(N_CHIPS)], axis=1)
    modv = lax.dynamic_index_in_dim(mod_all, me, 0, keepdims=False).reshape(9, d)

    gathered = _gather_shards("gather_weights", [wsh[n][0].astype(BF16) for n in BIG])
    wt = {}
    for n, g in zip(BIG, gathered):
        if n == "w_in":
            wt[n] = lay.perm_cols(jnp.concatenate(list(g), axis=1))
        else:
            wt[n] = g if n in COL_SHARDED else g.reshape(-1, g.shape[-1])
    lane_pad = (jnp.zeros((2 * nh,), F32), jnp.zeros((LANES - 4 * nh,), F32))
    pvec = jnp.stack([jnp.concatenate([lane_pad[0], a_log_fwd[0], a_log_bwd[0], lane_pad[1]]),
                      jnp.concatenate([lane_pad[0], dt_bias_fwd[0], dt_bias_bwd[0], lane_pad[1]])]
                     + [jnp.zeros((LANES,), F32)] * 6)
    wt.update(conv_a=conv_a_full, conv_dn=conv_dn_full, pvec=pvec, dn_norm=dn_norm, norm_ffn1=norm_ffn1,
              norm_mix=norm_mix, norm_ffn2=norm_ffn2, norm_final=norm_final.reshape(1, d))

    loss, dx, dmod, big, small = _local_step(x[0], loss_target[0], modv, wt, lay)
    loss = lax.psum(loss, ("x", "y", "c"))

    small_list = [dmod.reshape(1, 9 * d), small["norm_ffn1"], small["norm_mix"], small["conv_a"], small["conv_dn"],
                  small["a_log"][2 * nh:3 * nh], small["dt_bias"][2 * nh:3 * nh], small["a_log"][3 * nh:4 * nh],
                  small["dt_bias"][3 * nh:4 * nh], small["dn_norm"], small["norm_ffn2"], small["norm_final"]]
    g3, g3_spans = _pack_rows(small_list, LANES, 8, F32)
    g3_all = _allgather8("gather_small_grads", g3)
    g_small = dict(zip(SMALL, _unpack_rows(_sum8("sum_small_grads", g3_all), g3_spans)))
    dmod_all = jnp.concatenate([_unpack_rows(g3_all[k], g3_spans)[0] for k in range(N_DEV)], axis=0)
    ncol = w_ada.shape[-1]
    dmod_sh = lax.dynamic_slice_in_dim(dmod_all, chip * ncol, ncol, axis=1)
    grads = {"w_ada": _matmul("ada_grad", c_all, dmod_sh, "tn")[None]}
    for n in SMALL:
        g = g_small[n]
        if n in ("conv_a", "conv_dn"):
            wloc = wsh[n].shape[-1]
            g = lax.dynamic_slice_in_dim(g, chip * wloc, wloc, axis=1)
        grads[n] = g.reshape(wsh[n].shape)

    parts = [big[n] for n in BIG]
    from_sib = _swap_halves("grads_to_sibling", parts)
    chip_sums = [_chip_sum("chip_sum_" + n, g, f) for n, g, f in zip(BIG, parts, from_sib)]
    from_chips = _scatter_chips("grads_to_chips", chip_sums)
    halves = [_total("total_" + n, g, f, r) for n, g, f, r in zip(BIG, parts, from_sib, from_chips)]
    for n, g in zip(BIG, _share_halves("grads_share_total", halves)):
        grads[n] = g[None]

    delta, new_m, new_v = {}, {}, {}
    for n in ("w_ada",) + BIG:
        shp = wsh[n].shape
        outs = _adamw("adamw_" + n, *(t.reshape(shp[-2], shp[-1]) for t in (wsh[n], grads[n], msh[n], vsh[n])))
        delta[n], new_m[n], new_v[n] = (o.reshape(shp) for o in outs)
    packed = []
    for src in (wsh, grads, msh, vsh):
        pk, s_spans = _pack_rows([src[n] for n in SMALL], LANES, 8, F32)
        packed.append(pk)
    outs = _adamw("adamw_small", *packed)
    for dst, o in zip((delta, new_m, new_v), outs):
        dst.update(zip(SMALL, _unpack_rows(o, s_spans)))

    return (loss, dx[None], *[grads[n] for n in WEIGHTS], *[delta[n] for n in WEIGHTS],
            *[new_m[n] for n in WEIGHTS], *[new_v[n] for n in WEIGHTS])
```

```python
import functools

import jax
import jax.numpy as jnp
from jax import lax
from jax.experimental import pallas as pl
from jax.experimental.pallas import tpu as pltpu

F32 = jnp.float32
BF16 = jnp.bfloat16
SDS = jax.ShapeDtypeStruct
MESH = pl.DeviceIdType.MESH
HI = lax.Precision.HIGHEST

EPS = 1e-6
HEAD = 128
CHUNK = 64
LANES = 128
N_CHIPS = 4
N_DEV = 8
VMEM_LIMIT = 56 * 1024 * 1024

ADAM_LR = 0.001
ADAM_B1 = 0.9
ADAM_B2 = 0.999
ADAM_EPS = 1e-08
ADAM_WD = 0.01
ADAM_STEP = 10


def _params(*sem):
    return pltpu.CompilerParams(dimension_semantics=sem, vmem_limit_bytes=VMEM_LIMIT)


def _tile(n, cap, mult=LANES):
    t = min(n, cap) // mult * mult
    while t >= mult:
        if n % t == 0:
            return t
        t -= mult
    return n


def _row(tr, w, cb=0):
    return pl.BlockSpec((tr, w), lambda i: (i, cb))


def _vec(r, w):
    return pl.BlockSpec((r, w), lambda i: (0, 0))


def _nn(a, b, **kw):
    return jnp.dot(a, b, preferred_element_type=F32, **kw)


def _nt(a, b, **kw):
    return lax.dot_general(a, b, (((1,), (1,)), ((), ())), preferred_element_type=F32, **kw)


def _tn(a, b, **kw):
    return lax.dot_general(a, b, (((0,), (0,)), ((), ())), preferred_element_type=F32, **kw)


def _bnn(a, b):
    return lax.dot_general(a, b, (((2,), (1,)), ((0,), (0,))), preferred_element_type=F32)


def _bnt(a, b):
    return lax.dot_general(a, b, (((2,), (2,)), ((0,), (0,))), preferred_element_type=F32)


def _btn(a, b):
    return lax.dot_general(a, b, (((1,), (1,)), ((0,), (0,))), preferred_element_type=F32)


def _silu_grad(x):
    s = jax.nn.sigmoid(x)
    return s * (1.0 + x * (1.0 - s))


def _matmul(name, a, b, mode, out_dtype=F32, tm=1024, tn=1024, tk=2048, full_k=2816, out_pieces=0, ex=None):
    pieces_b = b.shape[0] if b.ndim == 3 else 0
    b2 = b.shape[1:] if pieces_b else b.shape
    if mode == "nn":
        (m, k), n = a.shape, b2[1] * max(pieces_b, 1)
    elif mode == "nt":
        (m, _), n, k = a.shape, b2[0], b2[1] * max(pieces_b, 1)
    else:
        (k, m), n = a.shape, b2[1] * max(pieces_b, 1)
    n_unit = n // max(out_pieces, 1) if mode == "nt" or not pieces_b else n // pieces_b
    if out_pieces and pieces_b and mode != "nt":
        assert out_pieces == pieces_b
    k_unit = k // pieces_b if (pieces_b and mode == "nt") else k
    tm, tn = _tile(m, tm), _tile(n_unit, tn)
    tk = k_unit if k_unit <= full_k else _tile(k_unit, tk)
    nk = k // tk
    n_per, k_per = n_unit // tn, k_unit // tk
    a_bytes, b_bytes = a.size * a.dtype.itemsize, b.size * b.dtype.itemsize
    j_outer = nk == 1 and b_bytes + a_bytes * (n // tn) < a_bytes + b_bytes * (m // tm)
    ij = (lambda g0, g1: (g1, g0)) if j_outer else (lambda g0, g1: (g0, g1))

    def spec(shape, pick):
        return pl.BlockSpec(shape, lambda g0, g1, l: pick(*ij(g0, g1), l))

    a_spec = {"nn": spec((tm, tk), lambda i, j, l: (i, l)), "nt": spec((tm, tk), lambda i, j, l: (i, l)),
              "tn": spec((tk, tm), lambda i, j, l: (l, i))}[mode]
    if not pieces_b:
        b_spec = {"nn": spec((tk, tn), lambda i, j, l: (l, j)), "nt": spec((tn, tk), lambda i, j, l: (j, l)),
                  "tn": spec((tk, tn), lambda i, j, l: (l, j))}[mode]
    elif mode == "nt":
        b_spec = spec((None, tn, tk), lambda i, j, l: (l // k_per, j, l % k_per))
    else:
        b_spec = spec((None, tk, tn), lambda i, j, l: (j // n_per, l, j % n_per))
    if out_pieces:
        o_spec = spec((None, tm, tn), lambda i, j, l: (j // n_per, i, j % n_per))
        o_shape = SDS((out_pieces, m, n // out_pieces), out_dtype)
    else:
        o_spec, o_shape = spec((tm, tn), lambda i, j, l: (i, j)), SDS((m, n), out_dtype)
    dot = {"nn": _nn, "nt": _nt, "tn": _tn}[mode]
    grid = (n // tn, m // tm, nk) if j_outer else (m // tm, n // tn, nk)
    host = _Hosted(ex, name, grid)

    def body(a_ref, b_ref, *rest):
        (o_ref,), scratch = host.split(rest, 1)
        host.start()
        part = dot(a_ref[...].astype(BF16), b_ref[...].astype(BF16))
        if nk == 1:
            o_ref[...] = part.astype(o_ref.dtype)
        else:
            l, acc = pl.program_id(2), scratch[0]

            @pl.when(l == 0)
            def _():
                acc[...] = part

            @pl.when((l > 0) & (l < nk - 1))
            def _():
                acc[...] += part

            @pl.when(l == nk - 1)
            def _():
                o_ref[...] = (acc[...] + part).astype(o_ref.dtype)
        host.wait()

    out = pl.pallas_call(
        body, grid=grid, in_specs=[a_spec, b_spec] + host.in_specs, out_specs=[o_spec] + host.out_specs,
        out_shape=[o_shape] + host.out_shapes, input_output_aliases=host.aliases(2, 1),
        scratch_shapes=([] if nk == 1 else [pltpu.VMEM((tm, tn), F32)]) + host.sems,
        compiler_params=_params(*(("arbitrary",) * 3 if host.stage else ("parallel", "parallel", "arbitrary"))),
        name=name)(a, b, *host.arrays)
    return host.finish(out, 1)[0]


def _norm_mod(name, h, nw, sh, sc, tr=512):
    s, d = h.shape
    tr = _tile(s, tr, 8)

    def body(h_ref, nw_ref, sh_ref, sc_ref, u_ref):
        x = h_ref[...]
        r = lax.rsqrt(jnp.mean(x * x, axis=-1, keepdims=True) + EPS)
        u_ref[...] = (x * r * nw_ref[...] * (1.0 + sc_ref[...]) + sh_ref[...]).astype(BF16)

    return pl.pallas_call(
        body, grid=(s // tr,), in_specs=[_row(tr, d), _vec(1, d), _vec(1, d), _vec(1, d)],
        out_specs=_row(tr, d), out_shape=SDS((s, d), BF16),
        compiler_params=_params("parallel"), name=name)(h, nw, sh, sc)


def _norm_mod_bwd(name, h, du, dh, nw, sc, tr=512):
    s, d = h.shape
    tr = _tile(s, tr, 8)

    def body(h_ref, du_ref, dh_ref, nw_ref, sc_ref, o_ref, acc_ref):
        @pl.when(pl.program_id(0) == 0)
        def _():
            acc_ref[...] = jnp.zeros_like(acc_ref)

        x, g = h_ref[...], du_ref[...]
        r = lax.rsqrt(jnp.mean(x * x, axis=-1, keepdims=True) + EPS)
        n = x * r
        nw, sc1 = nw_ref[...], 1.0 + sc_ref[...]
        dn = g * sc1 * nw
        o_ref[...] = dh_ref[...] + r * (dn - n * jnp.mean(dn * n, axis=-1, keepdims=True))
        gn = g * n
        acc_ref[0:1, :] += jnp.sum(g, axis=0, keepdims=True)
        acc_ref[1:2, :] += jnp.sum(gn * nw, axis=0, keepdims=True)
        acc_ref[2:3, :] += jnp.sum(gn * sc1, axis=0, keepdims=True)

    return pl.pallas_call(
        body, grid=(s // tr,),
        in_specs=[_row(tr, d), _row(tr, d), _row(tr, d), _vec(1, d), _vec(1, d)],
        out_specs=[_row(tr, d), _vec(8, d)], out_shape=[SDS((s, d), F32), SDS((8, d), F32)],
        compiler_params=_params("arbitrary"), name=name)(h, du, dh, nw, sc)


def _swiglu(name, ab, tr=256):
    s, f2 = ab.shape
    f = f2 // 2
    tr = _tile(s, tr, 8)

    def body(a_ref, b_ref, o_ref):
        o_ref[...] = (jax.nn.silu(a_ref[...].astype(F32)) * b_ref[...].astype(F32)).astype(BF16)

    return pl.pallas_call(
        body, grid=(s // tr,), in_specs=[_row(tr, f, 0), _row(tr, f, 1)], out_specs=_row(tr, f),
        out_shape=SDS((s, f), BF16), compiler_params=_params("parallel"), name=name)(ab, ab)


def _swiglu_bwd(name, ab, dhm, tr=256):
    s, f2 = ab.shape
    f = f2 // 2
    tr = _tile(s, tr, 8)

    def body(a_ref, b_ref, d_ref, o_ref):
        a, d = a_ref[...].astype(F32), d_ref[...].astype(F32)
        o_ref[:, 0:f] = (d * b_ref[...].astype(F32) * _silu_grad(a)).astype(BF16)
        o_ref[:, f:f2] = (d * jax.nn.silu(a)).astype(BF16)

    return pl.pallas_call(
        body, grid=(s // tr,), in_specs=[_row(tr, f, 0), _row(tr, f, 1), _row(tr, f)],
        out_specs=_row(tr, f2), out_shape=SDS((s, f2), BF16),
        compiler_params=_params("parallel"), name=name)(ab, ab, dhm)


def _resid(name, h, f, g, scale, tr=512):
    s, d = h.shape
    tr = _tile(s, tr, 8)

    def body(h_ref, f_ref, g_ref, o_ref):
        o_ref[...] = h_ref[...] + (scale * g_ref[...]) * f_ref[...]

    return pl.pallas_call(
        body, grid=(s // tr,), in_specs=[_row(tr, d), _row(tr, d), _vec(1, d)], out_specs=_row(tr, d),
        out_shape=SDS((s, d), F32), compiler_params=_params("parallel"), name=name)(h, f, g)


def _resid_bwd(name, dh, f, g, scale, tr=512):
    s, d = dh.shape
    tr = _tile(s, tr, 8)

    def body(dh_ref, f_ref, g_ref, o_ref, acc_ref):
        @pl.when(pl.program_id(0) == 0)
        def _():
            acc_ref[...] = jnp.zeros_like(acc_ref)

        x = dh_ref[...]
        o_ref[...] = ((scale * g_ref[...]) * x).astype(BF16)
        acc_ref[0:1, :] += jnp.sum(scale * x * f_ref[...], axis=0, keepdims=True)

    return pl.pallas_call(
        body, grid=(s // tr,), in_specs=[_row(tr, d), _row(tr, d), _vec(1, d)],
        out_specs=[_row(tr, d), _vec(8, d)], out_shape=[SDS((s, d), BF16), SDS((8, d), F32)],
        compiler_params=_params("arbitrary"), name=name)(dh, f, g)


def _final_loss(name, h, tgt, nw, tr=512):
    s, d = h.shape
    tr = _tile(s, tr, 8)

    def body(h_ref, t_ref, nw_ref, o_ref, acc_ref):
        @pl.when(pl.program_id(0) == 0)
        def _():
            acc_ref[...] = jnp.zeros_like(acc_ref)

        x, nw = h_ref[...], nw_ref[...]
        r = lax.rsqrt(jnp.mean(x * x, axis=-1, keepdims=True) + EPS)
        n = x * r
        diff = n * nw - t_ref[...]
        dy = diff * (1.0 / d)
        dn = dy * nw
        o_ref[...] = r * (dn - n * jnp.mean(dn * n, axis=-1, keepdims=True))
        acc_ref[0:1, :] += jnp.sum(dy * n, axis=0, keepdims=True)
        acc_ref[1:2, :] += jnp.sum(diff * diff, axis=0, keepdims=True) * (0.5 / d)

    return pl.pallas_call(
        body, grid=(s // tr,), in_specs=[_row(tr, d), _row(tr, d), _vec(1, d)],
        out_specs=[_row(tr, d), _vec(8, d)], out_shape=[SDS((s, d), F32), SDS((8, d), F32)],
        compiler_params=_params("arbitrary"), name=name)(h, tgt, nw)


class _Layout:
    def __init__(self, d, ca, nh):
        self.d, self.ca, self.nh = d, ca, nh
        self.qk = nh * HEAD
        self.qkv = 3 * self.qk
        self.z = self.qkv
        self.ga = self.z + self.qk
        self.cab = self.ga + 2 * d
        self.ba = self.cab + 3 * ca
        self.tail = _tile(self.ba, 512)
        self.total = self.ba + self.tail
        assert self.qkv % self.qk == 0 and self.ga % (2 * d) == 0 and self.cab % (3 * ca) == 0
        assert self.ba % self.tail == 0 and 4 * nh <= LANES

    def perm_cols(self, w):
        ca, qkv, qk, d, nh = self.ca, self.qkv, self.qk, self.d, self.nh
        o = [0, ca, 2 * ca, 3 * ca, 3 * ca + qkv, 3 * ca + qkv + qk, 3 * ca + qkv + qk + 4 * nh]
        cb, cc, cv = (w[..., o[i]:o[i + 1]] for i in range(3))
        x_qkv, x_z, x_ba = w[..., o[3]:o[4]], w[..., o[4]:o[5]], w[..., o[5]:o[6]]
        gates = w[..., o[6]:o[6] + 2 * d]
        pad = jnp.zeros(w.shape[:-1] + (self.tail - 4 * nh,), w.dtype)
        return jnp.concatenate([x_qkv, x_z, gates, cb, cc, cv, x_ba, pad], axis=-1)

    def unperm_cols(self, w):
        ca, nh = self.ca, self.nh
        cb, cc, cv = (w[..., self.cab + i * ca:self.cab + (i + 1) * ca] for i in range(3))
        return jnp.concatenate([cb, cc, cv, w[..., 0:self.qkv], w[..., self.z:self.ga],
                                w[..., self.ba:self.ba + 4 * nh], w[..., self.ga:self.cab]], axis=-1)


def _halo_specs(tr, w, cb, s):
    nb8 = s // 8
    return [pl.BlockSpec((8, w), lambda i: (jnp.maximum(i * (tr // 8) - 1, 0), cb)),
            pl.BlockSpec((tr, w), lambda i: (i, cb)),
            pl.BlockSpec((8, w), lambda i: (jnp.minimum((i + 1) * (tr // 8), nb8 - 1), cb))]


def _ext(prev_ref, main_ref, next_ref, i, nt):
    p = jnp.where(i > 0, prev_ref[...].astype(F32), 0.0)
    n = jnp.where(i < nt - 1, next_ref[...].astype(F32), 0.0)
    return jnp.concatenate([p, main_ref[...].astype(F32), n], axis=0)


def _shift(x, k):
    return x if k == 0 else pltpu.roll(x, (-k) % x.shape[0], 0)


def _conv_taps(x_ext, w, tr):
    kt = w.shape[0]
    acc = None
    for t in range(kt):
        term = _shift(x_ext, t - kt // 2)[8:8 + tr] * w[t:t + 1, :]
        acc = term if acc is None else acc + term
    return acc


def _prep_a(name, proj, conv_a, lay, tr=256):
    s, ca = proj.shape[0], lay.ca
    tr = _tile(s, tr, 8)
    nt, w = s // tr, 3 * ca

    def body(p_ref, m_ref, n_ref, w_ref, o_ref):
        x = _ext(p_ref, m_ref, n_ref, pl.program_id(0), nt)
        xv = x[:, ca:2 * ca] * x[:, 2 * ca:w]
        y = _conv_taps(xv, w_ref[...], tr)
        o_ref[...] = (m_ref[:, 0:ca] * y).astype(BF16)

    return pl.pallas_call(
        body, grid=(nt,), in_specs=_halo_specs(tr, w, lay.cab // w, s) + [_vec(conv_a.shape[0], ca)],
        out_specs=_row(tr, ca), out_shape=SDS((s, ca), BF16),
        compiler_params=_params("parallel"), name=name)(proj, proj, proj, conv_a)


def _prep_a_bwd(name, dya, proj, conv_a, dproj, lay, tr=256):
    s, ca = proj.shape[0], lay.ca
    tr = _tile(s, tr, 8)
    nt, w, kt = s // tr, 3 * ca, conv_a.shape[0]

    def body(p_ref, m_ref, n_ref, dp_ref, dm_ref, dn_ref, w_ref, _, o_ref, acc_ref):
        i = pl.program_id(0)

        @pl.when(i == 0)
        def _():
            acc_ref[...] = jnp.zeros_like(acc_ref)

        x = _ext(p_ref, m_ref, n_ref, i, nt)
        d_ext = _ext(dp_ref, dm_ref, dn_ref, i, nt)
        cb, cc, cv = x[:, 0:ca], x[:, ca:2 * ca], x[:, 2 * ca:w]
        xv = cc * cv
        wv = w_ref[...]
        dy_ext = d_ext * cb
        dx = None
        for t in range(kt):
            term = _shift(dy_ext, kt // 2 - t)[8:8 + tr] * wv[t:t + 1, :]
            dx = term if dx is None else dx + term
            acc_ref[t:t + 1, :] += jnp.sum(dy_ext[8:8 + tr] * _shift(xv, t - kt // 2)[8:8 + tr],
                                           axis=0, keepdims=True)
        y = _conv_taps(xv, wv, tr)
        o_ref[:, 0:ca] = (dm_ref[...] * y).astype(BF16)
        o_ref[:, ca:2 * ca] = (dx * cv[8:8 + tr]).astype(BF16)
        o_ref[:, 2 * ca:w] = (dx * cc[8:8 + tr]).astype(BF16)

    return pl.pallas_call(
        body, grid=(nt,),
        in_specs=_halo_specs(tr, w, lay.cab // w, s) + _halo_specs(tr, ca, 0, s)
        + [_vec(kt, ca), pl.BlockSpec(memory_space=pl.ANY)],
        out_specs=[_row(tr, w, lay.cab // w), _vec(8, ca)],
        out_shape=[SDS(dproj.shape, dproj.dtype), SDS((8, ca), F32)], input_output_aliases={7: 0},
        compiler_params=_params("arbitrary"), name=name)(proj, proj, proj, dya, dya, dya, conv_a, dproj)


def _qkv_act(c, nh, tr_rows):
    sact = jax.nn.silu(c)
    outs, inv = [], []
    for hd in range(3 * nh):
        sl = sact[:, hd * HEAD:(hd + 1) * HEAD]
        if hd < 2 * nh:
            r = lax.rsqrt(jnp.sum(sl * sl, axis=-1, keepdims=True) + EPS)
            inv.append(r)
            outs.append(sl * (r * (HEAD ** -0.5 if hd < nh else 1.0)))
        else:
            outs.append(sl)
    return jnp.concatenate(outs, axis=-1), sact, inv


def _prep_b(name, proj, conv_dn, lay, tr=256):
    s, w, nh = proj.shape[0], lay.qkv, lay.nh
    tr = _tile(s, tr, 8)
    nt = s // tr

    def body(p_ref, m_ref, n_ref, w_ref, o_ref):
        x = _ext(p_ref, m_ref, n_ref, pl.program_id(0), nt)
        c = _conv_taps(x, w_ref[...], tr)
        o_ref[...] = _qkv_act(c, nh, tr)[0]

    return pl.pallas_call(
        body, grid=(nt,), in_specs=_halo_specs(tr, w, 0, s) + [_vec(conv_dn.shape[0], w)],
        out_specs=_row(tr, w), out_shape=SDS((s, w), F32),
        compiler_params=_params("parallel"), name=name)(proj, proj, proj, conv_dn)


def _prep_b_bwd(name, dq, dk, dv, proj, conv_dn, dproj, lay, tr=256):
    s, w, nh, qk = proj.shape[0], lay.qkv, lay.nh, lay.qk
    tr = _tile(s, tr, 8)
    nt, kt = s // tr, conv_dn.shape[0]
    n_ext = tr + 16

    def body(*refs):
        x_refs, g_refs = refs[0:3], refs[3:12]
        w_ref, o_ref, acc_ref = refs[12], refs[14], refs[15]
        i = pl.program_id(0)

        @pl.when(i == 0)
        def _():
            acc_ref[...] = jnp.zeros_like(acc_ref)

        x = _ext(*x_refs, i, nt)
        wv = w_ref[...]
        c = None
        for t in range(kt):
            term = _shift(x, t - kt // 2) * wv[t:t + 1, :]
            c = term if c is None else c + term
        sact = jax.nn.silu(c)
        ds = []
        for part in range(3):
            g = _ext(*g_refs[3 * part:3 * part + 3], i, nt)
            for hd in range(nh):
                sl = sact[:, part * qk + hd * HEAD:part * qk + (hd + 1) * HEAD]
                gh = g[:, hd * HEAD:(hd + 1) * HEAD]
                if part < 2:
                    r = lax.rsqrt(jnp.sum(sl * sl, axis=-1, keepdims=True) + EPS)
                    sc = HEAD ** -0.5 if part == 0 else 1.0
                    ds.append(sc * r * (gh - sl * (r * r) * jnp.sum(gh * sl, axis=-1, keepdims=True)))
                else:
                    ds.append(gh)
        dc = jnp.concatenate(ds, axis=-1) * _silu_grad(c)
        rows = lax.broadcasted_iota(jnp.int32, (n_ext, 1), 0)
        dc = jnp.where((rows >= 2) & (rows < n_ext - 2), dc, 0.0)
        dx = None
        for t in range(kt):
            term = _shift(dc, kt // 2 - t)[8:8 + tr] * wv[t:t + 1, :]
            dx = term if dx is None else dx + term
            acc_ref[t:t + 1, :] += jnp.sum(dc[8:8 + tr] * _shift(x, t - kt // 2)[8:8 + tr],
                                           axis=0, keepdims=True)
        o_ref[...] = dx.astype(BF16)

    return pl.pallas_call(
        body, grid=(nt,),
        in_specs=_halo_specs(tr, w, 0, s) + _halo_specs(tr, qk, 0, s) * 3
        + [_vec(kt, w), pl.BlockSpec(memory_space=pl.ANY)],
        out_specs=[_row(tr, w, 0), _vec(8, w)],
        out_shape=[SDS(dproj.shape, dproj.dtype), SDS((8, w), F32)], input_output_aliases={13: 0},
        compiler_params=_params("arbitrary"), name=name)(
            proj, proj, proj, dq, dq, dq, dk, dk, dk, dv, dv, dv, conv_dn, dproj)


def _softplus(x):
    return jnp.maximum(x, 0.0) + jnp.log(1.0 + jnp.exp(-jnp.abs(x)))


def _split3(x):
    hi = x.astype(BF16)
    r = x - hi.astype(F32)
    mid = r.astype(BF16)
    return hi, mid, (r - mid.astype(F32)).astype(BF16)


def _exact_nn(m, x):
    m = m.astype(BF16)
    hi, mid, lo = _split3(x)
    return _nn(m, hi) + _nn(m, mid) + _nn(m, lo)


def _chunk_cumsum_masks(tr):
    ri = lax.broadcasted_iota(jnp.int32, (tr, tr), 0)
    ci = lax.broadcasted_iota(jnp.int32, (tr, tr), 1)
    same = (ri // CHUNK) == (ci // CHUNK)
    return (same & (ci <= ri)).astype(F32), (same & (ci >= ri)).astype(F32)


def _prep_c(name, proj, pvec, lay, tr=512):
    s, nh = proj.shape[0], lay.nh
    tr = _tile(s, tr, CHUNK)
    assert 6 * nh <= LANES

    def body(x_ref, p_ref, o_ref):
        x = x_ref[...]
        lane = lax.broadcasted_iota(jnp.int32, x.shape, 1)
        is_g = (lane >= 2 * nh) & (lane < 4 * nh)
        g = jnp.where(is_g, -jnp.exp(p_ref[0:1, :]) * _softplus(x + p_ref[1:2, :]), 0.0)
        m_f, m_b = _chunk_cumsum_masks(tr)
        gc = jnp.where(lane < 3 * nh, _exact_nn(m_f, g), _exact_nn(m_b, g))
        gc = pltpu.roll(gc, 2 * nh, 1)
        o_ref[...] = jnp.where(lane < 2 * nh, jax.nn.sigmoid(x), jnp.where(lane < 4 * nh, g, gc))

    return pl.pallas_call(
        body, grid=(s // tr,), in_specs=[_row(tr, LANES, lay.ba // LANES), _vec(8, LANES)],
        out_specs=_row(tr, LANES), out_shape=SDS((s, LANES), F32),
        compiler_params=_params("parallel"), name=name)(proj, pvec)


def _prep_c_bwd(name, dbg_f, dbg_b, proj, pvec, dproj, lay, tr=512):
    s, nh, tail = proj.shape[0], lay.nh, lay.tail
    tr = _tile(s, tr, CHUNK)

    def body(x_ref, df_ref, db_ref, p_ref, _, o_ref, acc_ref):
        @pl.when(pl.program_id(0) == 0)
        def _():
            acc_ref[...] = jnp.zeros_like(acc_ref)

        x = x_ref[...]
        lane = lax.broadcasted_iota(jnp.int32, x.shape, 1)
        is_b, is_g = lane < 2 * nh, (lane >= 2 * nh) & (lane < 4 * nh)
        fwd_lane = (lane < nh) | ((lane >= 2 * nh) & (lane < 3 * nh))
        d = jnp.where(lane < 4 * nh, jnp.where(fwd_lane, df_ref[...], db_ref[...]), 0.0)
        m_f, m_b = _chunk_cumsum_masks(tr)
        dgc = jnp.where(is_g, d, 0.0)
        dg = jnp.where(fwd_lane, _exact_nn(m_b, dgc), _exact_nn(m_f, dgc))
        sb = jax.nn.sigmoid(x)
        na = -jnp.exp(p_ref[0:1, :])
        xs = x + p_ref[1:2, :]
        dsp = dg * na * jax.nn.sigmoid(xs)
        dx = jnp.where(is_b, d * sb * (1.0 - sb), jnp.where(is_g, dsp, 0.0))
        o_ref[...] = jnp.zeros_like(o_ref)
        o_ref[:, 0:LANES] = dx.astype(BF16)
        acc_ref[0:1, :] += jnp.sum(jnp.where(is_g, dg * na * _softplus(xs), 0.0), axis=0, keepdims=True)
        acc_ref[1:2, :] += jnp.sum(jnp.where(is_g, dsp, 0.0), axis=0, keepdims=True)

    return pl.pallas_call(
        body, grid=(s // tr,),
        in_specs=[_row(tr, LANES, lay.ba // LANES), _row(tr, LANES), _row(tr, LANES), _vec(8, LANES),
                  pl.BlockSpec(memory_space=pl.ANY)],
        out_specs=[_row(tr, tail, lay.ba // tail), _vec(8, LANES)],
        out_shape=[SDS(dproj.shape, dproj.dtype), SDS((8, LANES), F32)], input_output_aliases={4: 0},
        compiler_params=_params("arbitrary"), name=name)(proj, dbg_f, dbg_b, pvec, dproj)


def _post(name, o_f, o_b, proj, dn_w, lay, tr=256):
    s, qk, nh = o_f.shape[0], lay.qk, lay.nh
    tr = _tile(s, tr, 8)

    def body(f_ref, b_ref, z_ref, w_ref, o_ref):
        o = f_ref[...] + b_ref[...]
        gate = jax.nn.silu(z_ref[...])
        for hd in range(nh):
            sl = slice(hd * HEAD, (hd + 1) * HEAD)
            oh = o[:, sl]
            r = lax.rsqrt(jnp.mean(oh * oh, axis=-1, keepdims=True) + EPS)
            o_ref[:, sl] = (oh * r * w_ref[...] * gate[:, sl]).astype(BF16)

    return pl.pallas_call(
        body, grid=(s // tr,),
        in_specs=[_row(tr, qk), _row(tr, qk), _row(tr, qk, lay.z // qk), _vec(1, HEAD)],
        out_specs=_row(tr, qk), out_shape=SDS((s, qk), BF16),
        compiler_params=_params("parallel"), name=name)(o_f, o_b, proj, dn_w)


def _post_bwd(name, dyb, o_f, o_b, proj, dn_w, dproj, lay, tr=256):
    s, qk, nh = o_f.shape[0], lay.qk, lay.nh
    tr = _tile(s, tr, 8)

    def body(d_ref, f_ref, b_ref, z_ref, w_ref, _, do_ref, dz_ref, acc_ref):
        @pl.when(pl.program_id(0) == 0)
        def _():
            acc_ref[...] = jnp.zeros_like(acc_ref)

        o, z, d, wv = f_ref[...] + b_ref[...], z_ref[...], d_ref[...], w_ref[...]
        gate = jax.nn.silu(z)
        dgate = _silu_grad(z)
        for hd in range(nh):
            sl = slice(hd * HEAD, (hd + 1) * HEAD)
            oh, dh = o[:, sl], d[:, sl]
            r = lax.rsqrt(jnp.mean(oh * oh, axis=-1, keepdims=True) + EPS)
            n = oh * r
            dz_ref[:, sl] = (dh * n * wv * dgate[:, sl]).astype(BF16)
            don = dh * gate[:, sl]
            acc_ref[0:1, :] += jnp.sum(don * n, axis=0, keepdims=True)
            dn = don * wv
            do_ref[:, sl] = r * (dn - n * jnp.mean(dn * n, axis=-1, keepdims=True))

    return pl.pallas_call(
        body, grid=(s // tr,),
        in_specs=[_row(tr, qk), _row(tr, qk), _row(tr, qk), _row(tr, qk, lay.z // qk), _vec(1, HEAD),
                  pl.BlockSpec(memory_space=pl.ANY)],
        out_specs=[_row(tr, qk), _row(tr, qk, lay.z // qk), _vec(8, HEAD)],
        out_shape=[SDS((s, qk), F32), SDS(dproj.shape, dproj.dtype), SDS((8, HEAD), F32)],
        input_output_aliases={5: 1},
        compiler_params=_params("arbitrary"), name=name)(dyb, o_f, o_b, proj, dn_w, dproj)


def _merge(name, pa, pb, proj, lay, tr=512):
    s, d = pa.shape
    tr = _tile(s, tr, 8)

    def body(a_ref, b_ref, g_ref, o_ref):
        o_ref[...] = (jax.nn.sigmoid(g_ref[:, 0:d]) * a_ref[...]
                      + jax.nn.sigmoid(g_ref[:, d:2 * d]) * b_ref[...]).astype(BF16)

    return pl.pallas_call(
        body, grid=(s // tr,), in_specs=[_row(tr, d), _row(tr, d), _row(tr, 2 * d, lay.ga // (2 * d))],
        out_specs=_row(tr, d), out_shape=SDS((s, d), BF16),
        compiler_params=_params("parallel"), name=name)(pa, pb, proj)


def _merge_bwd(name, dmg, pa, pb, proj, lay, tr=512):
    s, d = pa.shape
    tr = _tile(s, tr, 8)

    def body(d_ref, a_ref, b_ref, g_ref, da_ref, db_ref, dg_ref):
        dm = d_ref[...]
        sa, sb = jax.nn.sigmoid(g_ref[:, 0:d]), jax.nn.sigmoid(g_ref[:, d:2 * d])
        da_ref[...] = (sa * dm).astype(BF16)
        db_ref[...] = (sb * dm).astype(BF16)
        dg_ref[:, 0:d] = (dm * a_ref[...] * sa * (1.0 - sa)).astype(BF16)
        dg_ref[:, d:2 * d] = (dm * b_ref[...] * sb * (1.0 - sb)).astype(BF16)

    return pl.pallas_call(
        body, grid=(s // tr,),
        in_specs=[_row(tr, d), _row(tr, d), _row(tr, d), _row(tr, 2 * d, lay.ga // (2 * d))],
        out_specs=[_row(tr, d), _row(tr, d), _row(tr, 2 * d, lay.ga // (2 * d))],
        out_shape=[SDS((s, d), BF16), SDS((s, d), BF16), SDS((s, lay.total), BF16)],
        compiler_params=_params("parallel"), name=name)(dmg, pa, pb, proj)


def _tri_inverse(a_mat, ri, ci):
    def same(shift):
        return (ri >> shift) == (ci >> shift)

    x = -jnp.where(same(3), a_mat, 0.0)
    t_mat = (ri == ci).astype(F32) + x
    for _ in range(2):
        x = _bnn(x, x)
        t_mat = t_mat + _bnn(t_mat, x)
    for shift in (3, 4, 5):
        b = jnp.where(same(shift + 1) & ~same(shift), a_mat, 0.0)
        t_mat = t_mat - _bnn(_bnn(t_mat, b), t_mat)
    return t_mat


def _chunk_terms(q, k, v, beta, gc, g_row, g_last, reverse, t_mat=None):
    c = CHUNK
    ri = lax.broadcasted_iota(jnp.int32, (c, c), 0)
    ci = lax.broadcasted_iota(jnp.int32, (c, c), 1)
    if reverse:
        incl, strict = ri <= ci, ri < ci
    else:
        incl, strict = ri >= ci, ri > ci
    decay = jnp.where(incl, jnp.exp(jnp.where(incl, gc - g_row, 0.0)), 0.0)
    e = jnp.exp(gc)
    ed = jnp.exp(g_last - gc)
    el = jnp.exp(g_last)
    kb = k * beta
    a_mat = jnp.where(strict, _bnt(kb, k) * decay, 0.0)
    if t_mat is None:
        t_mat = _tri_inverse(a_mat, ri, ci)
    u = _bnn(t_mat, v * beta)
    w = _bnn(t_mat, kb * e)
    p_mat = jnp.where(incl, _bnt(q, k) * decay, 0.0)
    return dict(incl=incl, strict=strict, decay=decay, e=e, ed=ed, el=el, kb=kb,
                a=a_mat, t=t_mat, u=u, w=w, p=p_mat)


def _delta_specs(nh, tb, nb, reverse):
    tok = (lambda i: nb - 1 - i) if reverse else (lambda i: i)
    hw = nh * HEAD
    qkv = [pl.BlockSpec((tb, hw), functools.partial(lambda i, part: (tok(i), part), part=p)) for p in range(3)]
    rows = pl.BlockSpec((tb, hw), lambda i: (tok(i), 0))
    bg = pl.BlockSpec((tb, LANES), lambda i: (tok(i), 0))
    gct = pl.BlockSpec((2 * nh, tb), lambda i: (0, tok(i)))
    st = pl.BlockSpec((nh, tb // CHUNK, HEAD, HEAD), lambda i: (0, tok(i), 0, 0))
    tri = pl.BlockSpec((nh, tb // CHUNK, CHUNK, CHUNK), lambda i: (0, tok(i), 0, 0))
    return qkv, rows, bg, gct, st, tri


def _heads(ref, rows, nh):
    return jnp.stack([ref[rows, hd * HEAD:(hd + 1) * HEAD] for hd in range(nh)])


def _chunk_scalars(bg_ref, gct_ref, cj, nh, tb, reverse):
    rows = pl.ds(cj * CHUNK, CHUNK)
    lb = nh if reverse else 0
    lc = 4 * nh + lb
    last = cj * CHUNK + (0 if reverse else CHUNK - 1)
    g_lanes = gct_ref[lb:lb + nh, :]
    if cj:
        g_lanes = pltpu.roll(g_lanes, tb - cj * CHUNK, 1)
    col = lambda l0, r: jnp.stack([bg_ref[r, l0 + hd:l0 + hd + 1] for hd in range(nh)])
    return col(lb, rows), col(lc, rows), g_lanes[:, 0:CHUNK][:, None, :], col(lc, pl.ds(last, 1))


def _delta_fwd(name, qkvn, bg, gct, nh, reverse, tb=128):
    s = qkvn.shape[0]
    tb = _tile(s, tb, LANES)
    nb, cpb = s // tb, tb // CHUNK
    qkv, rows_spec, bg_spec, gct_spec, st, tri = _delta_specs(nh, tb, nb, reverse)

    def body(q_ref, k_ref, v_ref, bg_ref, gct_ref, o_ref, st_ref, tri_ref, state):
        @pl.when(pl.program_id(0) == 0)
        def _():
            state[...] = jnp.zeros_like(state)

        for cj in (range(cpb - 1, -1, -1) if reverse else range(cpb)):
            rows = pl.ds(cj * CHUNK, CHUNK)
            q, k, v = _heads(q_ref, rows, nh), _heads(k_ref, rows, nh), _heads(v_ref, rows, nh)
            tm = _chunk_terms(q, k, v, *_chunk_scalars(bg_ref, gct_ref, cj, nh, tb, reverse), reverse)
            s_in = state[...]
            st_ref[:, cj] = s_in
            tri_ref[:, cj] = tm["t"]
            vn = tm["u"] - _bnn(tm["w"], s_in)
            o = _bnn(q * tm["e"], s_in) + _bnn(tm["p"], vn)
            for hd in range(nh):
                o_ref[rows, hd * HEAD:(hd + 1) * HEAD] = o[hd]
            state[...] = s_in * tm["el"] + _btn(k * tm["ed"], vn)

    return pl.pallas_call(
        body, grid=(nb,), in_specs=qkv + [bg_spec, gct_spec], out_specs=[rows_spec, st, tri],
        out_shape=[SDS((s, nh * HEAD), F32), SDS((nh, s // CHUNK, HEAD, HEAD), F32),
                   SDS((nh, s // CHUNK, CHUNK, CHUNK), F32)],
        scratch_shapes=[pltpu.VMEM((nh, HEAD, HEAD), F32)],
        compiler_params=_params("arbitrary"), name=name)(qkvn, qkvn, qkvn, bg, gct)


def _delta_bwd(name, qkvn, bg, gct, do, states, tris, nh, reverse, add=None, tb=128, ex=None):
    s = qkvn.shape[0]
    tb = _tile(s, tb, LANES)
    nb, cpb = s // tb, tb // CHUNK
    qkv, rows_spec, bg_spec, gct_spec, st, tri = _delta_specs(nh, tb, nb, not reverse)
    n_add = 0 if add is None else 3
    host = _Hosted(ex, name, (nb,))

    def body(*refs):
        q_ref, k_ref, v_ref, bg_ref, gct_ref, do_ref, st_ref, tri_ref = refs[0:8]
        add_refs = refs[8:8 + n_add]
        (dq_ref, dk_ref, dv_ref, dbg_ref), (dstate,) = host.split(refs[8 + n_add:], 4)
        host.start()

        @pl.when(pl.program_id(0) == 0)
        def _():
            dstate[...] = jnp.zeros_like(dstate)

        ones = jnp.ones((nh, CHUNK, HEAD), BF16)
        row_id = lax.broadcasted_iota(jnp.int32, (CHUNK, 1), 0)
        rsum = lambda x: jnp.sum(x, axis=2, keepdims=True)
        for cj in (range(cpb) if reverse else range(cpb - 1, -1, -1)):
            rows = pl.ds(cj * CHUNK, CHUNK)
            q, k, v, d_o = (_heads(r, rows, nh) for r in (q_ref, k_ref, v_ref, do_ref))
            beta, gc, g_row, g_last = _chunk_scalars(bg_ref, gct_ref, cj, nh, tb, reverse)
            tm = _chunk_terms(q, k, v, beta, gc, g_row, g_last, reverse, t_mat=tri_ref[:, cj])
            incl, strict, e, ed, el, kb = tm["incl"], tm["strict"], tm["e"], tm["ed"], tm["el"], tm["kb"]
            t_mat, u, w, p_mat, decay = tm["t"], tm["u"], tm["w"], tm["p"], tm["decay"]
            s_in, ds_out = st_ref[:, cj], dstate[...]
            vn = u - _bnn(w, s_in)
            qe, kd, ke = q * e, k * ed, kb * e
            dvn = _btn(p_mat, d_o) + _bnn(kd, ds_out)
            dqe = _bnt(d_o, s_in)
            dq = dqe * e
            dgc = rsum(dqe * qe)
            dp = jnp.where(incl, _bnt(d_o, vn), 0.0)
            dkd = _bnt(vn, ds_out)
            dk = dkd * ed
            r = rsum(dkd * kd)
            dgc = dgc - r
            dg_last = (jnp.sum(r, axis=1, keepdims=True)
                       + jnp.sum(rsum(ds_out * s_in), axis=1, keepdims=True) * el)
            dw = -_bnt(dvn, s_in)
            dbv = _btn(t_mat, dvn)
            dke = _btn(t_mat, dw)
            da = -jnp.where(strict, _bnt(dbv, u) + _bnt(dke, w), 0.0)
            m_mat, n_mat = da * decay, dp * decay
            dkb = _bnn(m_mat, k) + dke * e
            dk = dk + _btn(m_mat, kb) + _btn(n_mat, q)
            dq = dq + _bnn(n_mat, k)
            g_mat = da * tm["a"] + dp * p_mat
            g_hi, g_mid, g_lo = _split3(g_mat)
            col = (_btn(g_hi, ones) + _btn(g_mid, ones) + _btn(g_lo, ones))[:, :, 0:1]
            dgc = dgc + rsum(g_mat) - col + rsum(dke * ke)
            dgc = dgc + jnp.where(row_id == (0 if reverse else CHUNK - 1), dg_last, 0.0)
            dv = dbv * beta
            dbeta = rsum(dbv * v) + rsum(dkb * k)
            dk = dk + dkb * beta
            dstate[...] = el * ds_out + _btn(qe, d_o) - _btn(w, dvn)
            lb = nh if reverse else 0
            for hd in range(nh):
                cols = slice(hd * HEAD, (hd + 1) * HEAD)
                extra = [a[rows, cols] for a in add_refs] if n_add else [0.0, 0.0, 0.0]
                dq_ref[rows, cols] = dq[hd] + extra[0]
                dk_ref[rows, cols] = dk[hd] + extra[1]
                dv_ref[rows, cols] = dv[hd] + extra[2]
                dbg_ref[rows, lb + hd:lb + hd + 1] = dbeta[hd]
                dbg_ref[rows, 2 * nh + lb + hd:2 * nh + lb + hd + 1] = dgc[hd]
        host.wait()

    out3 = SDS((s, nh * HEAD), F32)
    n_in = 8 + n_add
    out = pl.pallas_call(
        body, grid=(nb,),
        in_specs=qkv + [bg_spec, gct_spec, rows_spec, st, tri] + [rows_spec] * n_add + host.in_specs,
        out_specs=[rows_spec, rows_spec, rows_spec, bg_spec] + host.out_specs,
        out_shape=[out3, out3, out3, SDS((s, LANES), F32)] + host.out_shapes,
        input_output_aliases=host.aliases(n_in, 4), scratch_shapes=[pltpu.VMEM((nh, HEAD, HEAD), F32)] + host.sems,
        compiler_params=_params("arbitrary"), name=name)(
            qkvn, qkvn, qkvn, bg, gct, do, states, tris, *(add or ()), *host.arrays)
    return host.finish(out, 4)


def _row_pieces(g):
    return g.reshape(N_CHIPS, g.shape[0] // N_CHIPS, g.shape[1])


def _ffn_fwd(tag, h, nw, sh, sc, g, w_up, w_down, ex=None):
    u = _norm_mod(tag + "_norm", h, nw, sh, sc)
    ab = _matmul(tag + "_up", u, w_up, "nn", out_dtype=BF16, tn=1408, ex=ex)
    hm = _swiglu(tag + "_act", ab)
    f = _matmul(tag + "_down", hm, w_down, "nn", ex=ex)
    return _resid(tag + "_res", h, f, g, 0.5), (h, u, ab, hm, f)


def _ffn_bwd(tag, dh, saved, nw, sc, g, w_up, w_down, ex=None):
    h, u, ab, hm, f = saved
    df, acc_g = _resid_bwd(tag + "_res_bwd", dh, f, g, 0.5)
    gw_down = _matmul(tag + "_gw_down", hm, df, "tn", out_dtype=BF16, tm=1408, ex=ex)
    dhm = _matmul(tag + "_dhm", df, w_down, "nt", out_dtype=BF16, tn=2816)
    dab = _swiglu_bwd(tag + "_act_bwd", ab, dhm)
    gw_up = _matmul(tag + "_gw_up", u, dab, "tn", out_dtype=BF16, tn=1408, out_pieces=N_CHIPS, ex=ex)
    du = _matmul(tag + "_du", dab, w_up, "nt", ex=ex)
    dh_in, acc = _norm_mod_bwd(tag + "_norm_bwd", h, du, dh, nw, sc)
    return dh_in, gw_up, _row_pieces(gw_down), (acc[0], acc[1], acc_g[0], acc[2])


def _mixer_fwd(h, nw, sh, sc, g, wt, lay, ex=None):
    nh = lay.nh
    u = _norm_mod("mix_norm", h, nw, sh, sc)
    proj = _matmul("mix_in", u, wt["w_in"], "nn", ex=ex)
    qkvn = _prep_b("mix_prep_b", proj, wt["conv_dn"], lay)
    ya = _prep_a("mix_prep_a", proj, wt["conv_a"], lay)
    bg = _prep_c("mix_prep_c", proj, wt["pvec"], lay)
    gct = bg[:, 4 * nh:6 * nh].T
    o_f, *st_f = _delta_fwd("delta_fwd_l2r", qkvn, bg, gct, nh, False)
    o_b, *st_b = _delta_fwd("delta_fwd_r2l", qkvn, bg, gct, nh, True)
    yb = _post("mix_post", o_f, o_b, proj, wt["dn_norm"], lay)
    pa = _matmul("mix_a_out", ya, wt["w_a_out"], "nn")
    pb = _matmul("mix_b_out", yb, wt["w_b_out"], "nn")
    mg = _merge("mix_merge", pa, pb, proj, lay)
    y = _matmul("mix_out", mg, wt["w_out"], "nn", ex=ex)
    h2 = _resid("mix_res", h, y, g, 1.0)
    return h2, (h, u, proj, qkvn, ya, bg, gct, o_f, o_b, st_f, st_b, yb, pa, pb, mg, y)


def _mixer_bwd(dh, saved, nw, sc, g, wt, lay, ex=None):
    h, u, proj, qkvn, ya, bg, gct, o_f, o_b, st_f, st_b, yb, pa, pb, mg, y = saved
    nh = lay.nh
    dy, acc_g = _resid_bwd("mix_res_bwd", dh, y, g, 1.0)
    gw_out = _matmul("mix_gw_out", mg, dy, "tn", out_dtype=BF16, ex=ex)
    dmg = _matmul("mix_dmg", dy, wt["w_out"], "nt")
    dpa, dpb, dproj = _merge_bwd("mix_merge_bwd", dmg, pa, pb, proj, lay)
    gw_a = _matmul("mix_gw_a", ya, dpa, "tn", out_dtype=BF16, out_pieces=N_CHIPS)
    gw_b = _matmul("mix_gw_b", yb, dpb, "tn", out_dtype=BF16)
    dya = _matmul("mix_dya", dpa, wt["w_a_out"], "nt")
    dyb = _matmul("mix_dyb", dpb, wt["w_b_out"], "nt")
    do, dproj, acc_dn = _post_bwd("mix_post_bwd", dyb, o_f, o_b, proj, wt["dn_norm"], dproj, lay)
    dq, dk, dv, dbg_f = _delta_bwd("delta_bwd_l2r", qkvn, bg, gct, do, *st_f, nh, False, ex=ex)
    dq, dk, dv, dbg_b = _delta_bwd("delta_bwd_r2l", qkvn, bg, gct, do, *st_b, nh, True, add=(dq, dk, dv), ex=ex)
    dproj, acc_ca = _prep_a_bwd("mix_prep_a_bwd", dya, proj, wt["conv_a"], dproj, lay)
    dproj, acc_cd = _prep_b_bwd("mix_prep_b_bwd", dq, dk, dv, proj, wt["conv_dn"], dproj, lay)
    dproj, acc_pc = _prep_c_bwd("mix_prep_c_bwd", dbg_f, dbg_b, proj, wt["pvec"], dproj, lay)
    gw_in = lay.unperm_cols(_matmul("mix_gw_in", u, dproj, "tn", out_dtype=BF16))
    gw_in = gw_in.reshape(gw_in.shape[0], N_CHIPS, -1).transpose(1, 0, 2)
    du = _matmul("mix_du", dproj, wt["w_in"], "nt")
    dh_in, acc = _norm_mod_bwd("mix_norm_bwd", h, du, dh, nw, sc)
    small = dict(conv_a=acc_ca[0:wt["conv_a"].shape[0]], conv_dn=acc_cd[0:wt["conv_dn"].shape[0]],
                 dn_norm=acc_dn[0:1], a_log=acc_pc[0], dt_bias=acc_pc[1])
    big = dict(w_in=gw_in, w_a_out=gw_a, w_b_out=_row_pieces(gw_b), w_out=_row_pieces(gw_out))
    return dh_in, big, small, (acc[0], acc[1], acc_g[0], acc[2])


def _as_operands(gathered, lay):
    wt = {}
    for n, g in gathered.items():
        if n == "w_in":
            wt[n] = lay.perm_cols(jnp.concatenate(list(g), axis=1))
        else:
            wt[n] = g if n in COL_SHARDED else g.reshape(-1, g.shape[-1])
    return wt


def _local_step(x, tgt, modv, wt, lay, ex=None):
    m = [modv[i:i + 1] for i in range(9)]
    h1, sv1 = _ffn_fwd("ffn1", x, wt["norm_ffn1"], m[0], m[1], m[2], wt["w_ffn1_up"], wt["w_ffn1_down"], ex)
    if ex:
        wt = dict(wt, **_as_operands(ex.weights("mixer"), lay))
    h2, sv2 = _mixer_fwd(h1, wt["norm_mix"], m[3], m[4], m[5], wt, lay, ex)
    if ex:
        wt = dict(wt, **_as_operands(ex.weights("ffn2"), lay))
    h3, sv3 = _ffn_fwd("ffn2", h2, wt["norm_ffn2"], m[6], m[7], m[8], wt["w_ffn2_up"], wt["w_ffn2_down"])
    dh3, acc_f = _final_loss("final_loss", h3, tgt, wt["norm_final"])
    loss = jnp.sum(acc_f[1])
    dh2, gu2, gd2, dm3 = _ffn_bwd("ffn2", dh3, sv3, wt["norm_ffn2"], m[7], m[8], wt["w_ffn2_up"], wt["w_ffn2_down"])
    if ex:
        ex.reduce("ffn2", dict(w_ffn2_up=gu2, w_ffn2_down=gd2))
    dh1, gmix, small, dm2 = _mixer_bwd(dh2, sv2, wt["norm_mix"], m[4], m[5], wt, lay, ex)
    if ex:
        ex.reduce("mixer", gmix)
    dx, gu1, gd1, dm1 = _ffn_bwd("ffn1", dh1, sv1, wt["norm_ffn1"], m[1], m[2], wt["w_ffn1_up"], wt["w_ffn1_down"], ex)
    dmod = jnp.stack([dm1[0], dm1[1], dm1[2], dm2[0], dm2[1], dm2[2], dm3[0], dm3[1], dm3[2]])
    big = dict(w_ffn1_up=gu1, w_ffn1_down=gd1, w_ffn2_up=gu2, w_ffn2_down=gd2, **gmix)
    small = dict(small, norm_ffn1=dm1[3], norm_mix=dm2[3], norm_ffn2=dm3[3], norm_final=acc_f[0])
    return loss, dx, dmod, big, small


def _position():
    return lax.axis_index("x"), lax.axis_index("y"), lax.axis_index("c")


_ANY = pl.BlockSpec(memory_space=pl.ANY)
_VMEM = pl.BlockSpec(memory_space=pltpu.VMEM)


def _allgather8(name, v):
    r = v.shape[0]

    def body(v_ref, out_ref, send_sems, recv_sems):
        x, y, c = _position()
        me = 4 * x + 2 * y + c
        out_ref[me] = v_ref[...]
        copies = []
        for mask in range(1, N_DEV):
            peer = tuple(1 - p if mask >> b & 1 else p for p, b in ((x, 2), (y, 1), (c, 0)))
            cp = pltpu.make_async_remote_copy(
                src_ref=v_ref, dst_ref=out_ref.at[me], send_sem=send_sems.at[mask - 1],
                recv_sem=recv_sems.at[mask - 1], device_id=peer, device_id_type=MESH)
            cp.start()
            copies.append(cp)
        for cp in copies:
            cp.wait()

    return pl.pallas_call(
        body, in_specs=[_VMEM], out_specs=_VMEM, out_shape=SDS((N_DEV, r, LANES), F32),
        scratch_shapes=[pltpu.SemaphoreType.DMA((N_DEV - 1,)), pltpu.SemaphoreType.DMA((N_DEV - 1,))],
        name=name)(v)


def _other_chips(x, y):
    return [(1 - x, y), (x, 1 - y), (1 - x, 1 - y)]


def _half_rows(c, rows):
    hr = rows // 2
    assert hr % 16 == 0
    return pl.ds(pl.multiple_of(c * hr, 16), hr)


class _Stage:
    def __init__(self, arrays, out_shapes, sems, plan, in_place=False):
        self.arrays, self.out_shapes, self.sems, self.plan, self.in_place = arrays, out_shapes, sems, plan, in_place

    def start(self, ins, outs, sems):
        for kind, cp in self.plan(ins, outs, sems):
            if kind != "recv":
                cp.start()

    def wait(self, ins, outs, sems):
        for kind, cp in self.plan(ins, outs, sems):
            {"local": cp.wait, "both": cp.wait, "send": cp.wait_send, "recv": cp.wait_recv}[kind]()

    def aliases(self, in_offset, out_offset):
        return {in_offset + i: out_offset + i for i in range(len(self.arrays))} if self.in_place else {}


def _run_stage(name, stage):
    n_in, n_out = len(stage.arrays), len(stage.out_shapes)

    def body(*refs):
        ins, outs, sems = refs[0:n_in], refs[n_in:n_in + n_out], refs[n_in + n_out:]
        stage.start(ins, outs, sems)
        stage.wait(ins, outs, sems)

    return pl.pallas_call(
        body, in_specs=[_ANY] * n_in, out_specs=[_ANY] * n_out, out_shape=stage.out_shapes,
        input_output_aliases=stage.aliases(0, 0), scratch_shapes=stage.sems, name=name)(*stage.arrays)


def _dma_sems(*counts):
    return [pltpu.SemaphoreType.DMA((n,)) for n in counts]


def _gather_send(shards):
    nw = len(shards)

    def plan(ins, outs, sems):
        send_sems, recv_sems, local_sems = sems
        x, y, c = _position()
        p = 2 * x + y
        todo = [("local", pltpu.make_async_copy(ins[w], outs[w].at[p], local_sems.at[w])) for w in range(nw)]
        for j, (cx, cy) in enumerate(_other_chips(x, y)):
            for w in range(nw):
                half = _half_rows(c, ins[w].shape[0])
                sem = dict(send_sem=send_sems.at[j * nw + w], recv_sem=recv_sems.at[j * nw + w], device_id_type=MESH)
                todo.append(("send", pltpu.make_async_remote_copy(
                    src_ref=ins[w].at[half], dst_ref=outs[w].at[p, half], device_id=(cx, cy, c), **sem)))
                landing = outs[w].at[2 * cx + cy, half]
                todo.append(("recv", pltpu.make_async_remote_copy(
                    src_ref=landing, dst_ref=landing, device_id=(x, y, c), **sem)))
        return todo

    return _Stage(shards, [SDS((N_CHIPS,) + v.shape, v.dtype) for v in shards], _dma_sems(3 * nw, 3 * nw, nw), plan)


def _gather_pass(gathered):
    nw = len(gathered)

    def plan(ins, outs, sems):
        send_sems, recv_sems = sems
        x, y, c = _position()
        todo = []
        for j, (cx, cy) in enumerate(_other_chips(x, y)):
            for w in range(nw):
                sem = dict(send_sem=send_sems.at[j * nw + w], recv_sem=recv_sems.at[j * nw + w], device_id_type=MESH)
                mine = outs[w].at[2 * cx + cy, _half_rows(c, outs[w].shape[1])]
                theirs = outs[w].at[2 * cx + cy, _half_rows(1 - c, outs[w].shape[1])]
                todo.append(("send", pltpu.make_async_remote_copy(
                    src_ref=mine, dst_ref=mine, device_id=(x, y, 1 - c), **sem)))
                todo.append(("recv", pltpu.make_async_remote_copy(
                    src_ref=theirs, dst_ref=theirs, device_id=(x, y, c), **sem)))
        return todo

    return _Stage(gathered, [SDS(g.shape, g.dtype) for g in gathered], _dma_sems(3 * nw, 3 * nw), plan, in_place=True)


def _swap_halves(gs):
    nw = len(gs)

    def plan(ins, outs, sems):
        x, y, c = _position()
        return [("both", pltpu.make_async_remote_copy(
            src_ref=ins[w].at[:, _half_rows(1 - c, ins[w].shape[1])], dst_ref=outs[w], send_sem=sems[0].at[w],
            recv_sem=sems[1].at[w], device_id=(x, y, 1 - c), device_id_type=MESH)) for w in range(nw)]

    return _Stage(gs, [SDS((g.shape[0], g.shape[1] // 2, g.shape[2]), g.dtype) for g in gs], _dma_sems(nw, nw), plan)


def _scatter_chips(vs):
    nw = len(vs)

    def plan(ins, outs, sems):
        x, y, c = _position()
        return [("both", pltpu.make_async_remote_copy(
            src_ref=ins[w].at[2 * cx + cy], dst_ref=outs[w].at[j], send_sem=sems[0].at[j * nw + w],
            recv_sem=sems[1].at[j * nw + w], device_id=(cx, cy, c), device_id_type=MESH))
            for j, (cx, cy) in enumerate(_other_chips(x, y)) for w in range(nw)]

    return _Stage(vs, [SDS((N_CHIPS - 1,) + v.shape[1:], v.dtype) for v in vs], _dma_sems(3 * nw, 3 * nw), plan)


def _share_halves(fulls):
    nw = len(fulls)

    def plan(ins, outs, sems):
        x, y, c = _position()
        todo = []
        for w in range(nw):
            rows = outs[w].at[_half_rows(c, outs[w].shape[0])]
            todo.append(("both", pltpu.make_async_remote_copy(
                src_ref=rows, dst_ref=rows, send_sem=sems[0].at[w], recv_sem=sems[1].at[w],
                device_id=(x, y, 1 - c), device_id_type=MESH)))
        return todo

    return _Stage(fulls, [SDS(f.shape, f.dtype) for f in fulls], _dma_sems(nw, nw), plan, in_place=True)


class _Hosted:
    def __init__(self, ex, name, grid):
        self.ex, self.name, self.grid = ex, name, grid
        self.stage = ex.host(name) if ex is not None else None
        st = self.stage
        self.arrays = list(st.arrays) if st else []
        self.out_shapes = list(st.out_shapes) if st else []
        self.sems = list(st.sems) if st else []
        self.in_specs, self.out_specs = [_ANY] * len(self.arrays), [_ANY] * len(self.out_shapes)

    def aliases(self, in_offset, out_offset):
        return self.stage.aliases(in_offset, out_offset) if self.stage else {}

    def split(self, rest, n_out):
        ni, no, ns = len(self.arrays), len(self.out_shapes), len(self.sems)
        self.ins, self.outs = rest[0:ni], rest[ni + n_out:ni + n_out + no]
        tail = rest[ni + n_out + no:]
        self.sem_refs = tail[len(tail) - ns:]
        return rest[ni:ni + n_out], tail[0:len(tail) - ns]

    def _at(self, last):
        conds = [pl.program_id(d) == (g - 1 if last else 0) for d, g in enumerate(self.grid)]
        return functools.reduce(lambda p, q: p & q, conds)

    def start(self):
        if self.stage:
            pl.when(self._at(False))(lambda: self.stage.start(self.ins, self.outs, self.sem_refs))

    def wait(self):
        if self.stage:
            pl.when(self._at(True))(lambda: self.stage.wait(self.ins, self.outs, self.sem_refs))

    def finish(self, out, n_out):
        out = list(out)
        if self.stage:
            self.ex.done(self.name, out[n_out:])
        return out[0:n_out]


GROUPS = {"ffn1": ("w_ffn1_up", "w_ffn1_down"), "mixer": ("w_in", "w_a_out", "w_b_out", "w_out"),
          "ffn2": ("w_ffn2_up", "w_ffn2_down")}
HOSTS = {"ffn1_up": ("gather_send", "mixer"), "ffn1_down": ("gather_pass", "mixer"),
         "mix_in": ("gather_send", "ffn2"), "mix_out": ("gather_pass", "ffn2"),
         "mix_gw_out": ("swap", "ffn2"), "delta_bwd_l2r": ("scatter", "ffn2"), "delta_bwd_r2l": ("share", "ffn2"),
         "ffn1_gw_down": ("swap", "mixer"), "ffn1_gw_up": ("scatter", "mixer"), "ffn1_du": ("share", "mixer")}


class _Exchange:
    def __init__(self, shards):
        self.shards = shards
        self.gathered, self.red = {}, {}

    def _stage(self, kind, group):
        if kind == "gather_send":
            return _gather_send([self.shards[n] for n in GROUPS[group]])
        if kind == "gather_pass":
            return _gather_pass(self.gathered[group])
        st = self.red[group]
        return {"swap": lambda: _swap_halves(st["parts"]), "scatter": lambda: _scatter_chips(st["chip_sums"]),
                "share": lambda: _share_halves(st["fulls"])}[kind]()

    def _done(self, kind, group, outs):
        names = GROUPS[group]
        if kind in ("gather_send", "gather_pass"):
            self.gathered[group] = list(outs)
            return
        st = self.red[group]
        if kind == "swap":
            st["from_sib"] = list(outs)
            st["chip_sums"] = [_chip_sum("chip_sum_" + n, g, f) for n, g, f in zip(names, st["parts"], outs)]
        elif kind == "scatter":
            st["fulls"] = [_total("total_" + n, g, f, r)
                           for n, g, f, r in zip(names, st["parts"], st["from_sib"], outs)]
        else:
            st["grads"] = dict(zip(names, outs))

    def host(self, kernel_name):
        return self._stage(*HOSTS[kernel_name]) if kernel_name in HOSTS else None

    def done(self, kernel_name, outs):
        self._done(*HOSTS[kernel_name], outs)

    def run(self, kind, group):
        self._done(kind, group, _run_stage(f"{kind}_{group}", self._stage(kind, group)))

    def weights(self, group):
        return dict(zip(GROUPS[group], self.gathered[group]))

    def reduce(self, group, parts):
        self.red[group] = dict(parts=[parts[n] for n in GROUPS[group]])

    def grads(self, group):
        return self.red[group]["grads"]


def _chip_sum(name, g, from_sib):
    _, r, cdim = g.shape
    hr = r // 2

    def body(g_ref, s_ref, o_ref):
        o_ref[...] = (g_ref[...].astype(F32) + s_ref[...].astype(F32)).astype(o_ref.dtype)

    blk = (None, hr, cdim)
    return pl.pallas_call(
        body, grid=(N_CHIPS,),
        in_specs=[pl.BlockSpec(blk, lambda j: (j, lax.axis_index("c"), 0)), pl.BlockSpec(blk, lambda j: (j, 0, 0))],
        out_specs=pl.BlockSpec(blk, lambda j: (j, 0, 0)),
        out_shape=SDS((N_CHIPS, hr, cdim), g.dtype), compiler_params=_params("parallel"), name=name)(g, from_sib)


def _total(name, g, from_sib, from_chips):
    _, r, cdim = g.shape
    hr = r // 2
    tr = _tile(hr, 256, 16)
    nt = hr // tr

    def body(g_ref, s_ref, rc_ref, o_ref):
        acc = g_ref[...].astype(F32) + s_ref[...].astype(F32)
        for j in range(N_CHIPS - 1):
            acc = acc + rc_ref[j].astype(F32)
        o_ref[...] = acc

    def chip():
        return 2 * lax.axis_index("x") + lax.axis_index("y")

    blk = (None, tr, cdim)
    return pl.pallas_call(
        body, grid=(nt,),
        in_specs=[pl.BlockSpec(blk, lambda i: (chip(), lax.axis_index("c") * nt + i, 0)),
                  pl.BlockSpec(blk, lambda i: (chip(), i, 0)),
                  pl.BlockSpec((N_CHIPS - 1, tr, cdim), lambda i: (0, i, 0))],
        out_specs=pl.BlockSpec((tr, cdim), lambda i: (lax.axis_index("c") * nt + i, 0)),
        out_shape=SDS((r, cdim), F32), compiler_params=_params("parallel"), name=name)(g, from_sib, from_chips)


def _sum8(name, v):
    _, r, w = v.shape

    def body(v_ref, o_ref):
        acc = v_ref[0]
        for j in range(1, N_DEV):
            acc = acc + v_ref[j]
        o_ref[...] = acc

    return pl.pallas_call(body, in_specs=[_VMEM], out_specs=_VMEM, out_shape=SDS((r, w), F32), name=name)(v)


def _adamw(name, w, g, m, v, tr=256):
    r, cdim = w.shape
    tr = _tile(r, tr, 8)
    bc1, bc2 = 1.0 - ADAM_B1 ** ADAM_STEP, 1.0 - ADAM_B2 ** ADAM_STEP

    def body(w_ref, g_ref, m_ref, v_ref, d_ref, nm_ref, nv_ref):
        g = g_ref[...]
        m2 = ADAM_B1 * m_ref[...] + (1.0 - ADAM_B1) * g
        v2 = ADAM_B2 * v_ref[...] + (1.0 - ADAM_B2) * (g * g)
        d_ref[...] = -ADAM_LR * ((m2 / bc1) / (jnp.sqrt(v2 / bc2) + ADAM_EPS) + ADAM_WD * w_ref[...])
        nm_ref[...] = m2
        nv_ref[...] = v2

    spec = _row(tr, cdim)
    out = SDS((r, cdim), F32)
    return pl.pallas_call(body, grid=(r // tr,), in_specs=[spec] * 4, out_specs=[spec] * 3, out_shape=[out] * 3,
                          compiler_params=_params("parallel"), name=name)(w, g, m, v)


def _pack_rows(arrays, width, row_mult, dtype):
    parts, spans, row = [], [], 0
    for a in arrays:
        n = a.size
        rows = -(-n // width)
        flat = a.reshape(-1).astype(dtype)
        if rows * width != n:
            flat = jnp.concatenate([flat, jnp.zeros((rows * width - n,), dtype)])
        parts.append(flat.reshape(rows, width))
        spans.append((row, rows, n, a.shape))
        row += rows
    pad = -row % row_mult
    if pad:
        parts.append(jnp.zeros((pad, width), dtype))
    return jnp.concatenate(parts, axis=0), spans


def _unpack_rows(packed, spans):
    return [packed[r0:r0 + rows].reshape(-1)[0:n].reshape(shape) for r0, rows, n, shape in spans]


BIG = ("w_ffn1_up", "w_ffn1_down", "w_in", "w_a_out", "w_b_out", "w_out", "w_ffn2_up", "w_ffn2_down")
COL_SHARDED = ("w_ffn1_up", "w_in", "w_a_out", "w_ffn2_up")
SMALL = ("b_ada", "norm_ffn1", "norm_mix", "conv_a", "conv_dn", "a_log_fwd", "dt_bias_fwd", "a_log_bwd",
         "dt_bias_bwd", "dn_norm", "norm_ffn2", "norm_final")
WEIGHTS = ("w_ada", "b_ada", "norm_ffn1", "w_ffn1_up", "w_ffn1_down", "norm_mix", "w_in", "conv_a", "conv_dn",
           "a_log_fwd", "dt_bias_fwd", "a_log_bwd", "dt_bias_bwd", "dn_norm", "w_a_out", "w_b_out", "w_out",
           "norm_ffn2", "w_ffn2_up", "w_ffn2_down", "norm_final")


def kernel(x, c, w_ada, b_ada, norm_ffn1, w_ffn1_up, w_ffn1_down, norm_mix, w_in, conv_a, conv_dn, a_log_fwd, dt_bias_fwd, a_log_bwd, dt_bias_bwd, dn_norm, w_a_out, w_b_out, w_out, norm_ffn2, w_ffn2_up, w_ffn2_down, norm_final, loss_target, m_w_ada, m_b_ada, m_norm_ffn1, m_w_ffn1_up, m_w_ffn1_down, m_norm_mix, m_w_in, m_conv_a, m_conv_dn, m_a_log_fwd, m_dt_bias_fwd, m_a_log_bwd, m_dt_bias_bwd, m_dn_norm, m_w_a_out, m_w_b_out, m_w_out, m_norm_ffn2, m_w_ffn2_up, m_w_ffn2_down, m_norm_final, v_w_ada, v_b_ada, v_norm_ffn1, v_w_ffn1_up, v_w_ffn1_down, v_norm_mix, v_w_in, v_conv_a, v_conv_dn, v_a_log_fwd, v_dt_bias_fwd, v_a_log_bwd, v_dt_bias_bwd, v_dn_norm, v_w_a_out, v_w_b_out, v_w_out, v_norm_ffn2, v_w_ffn2_up, v_w_ffn2_down, v_norm_final):
    given = dict(locals())
    wsh = {n: given[n] for n in WEIGHTS}
    msh = {n: given["m_" + n] for n in WEIGHTS}
    vsh = {n: given["v_" + n] for n in WEIGHTS}
    d = x.shape[-1]
    ca = conv_a.shape[-1] * N_CHIPS
    nh = conv_dn.shape[-1] * N_CHIPS // (3 * HEAD)
    lay = _Layout(d, ca, nh)
    xi, yi, ci = _position()
    chip = 2 * xi + yi
    me = 2 * chip + ci

    c_act = jax.nn.silu(c)
    g1, g1_spans = _pack_rows([c_act, conv_a[0], conv_dn[0]], LANES, 8, F32)
    g1_all = _allgather8("gather_cond", g1)
    per_dev = [_unpack_rows(g1_all[k], g1_spans) for k in range(N_DEV)]
    c_all = jnp.concatenate([p[0] for p in per_dev], axis=0)
    conv_a_full = jnp.concatenate([per_dev[2 * k][1] for k in range(N_CHIPS)], axis=1)
    conv_dn_full = jnp.concatenate([per_dev[2 * k][2] for k in range(N_CHIPS)], axis=1)

    mod_sh = _matmul("ada_mod", c_all, w_ada[0], "nn")
    b_sh = lax.dynamic_slice_in_dim(b_ada, chip * mod_sh.shape[1], mod_sh.shape[1], axis=1)
    g2, g2_spans = _pack_rows([mod_sh + b_sh], LANES, 8, F32)
    g2_all = _allgather8("gather_mod", g2)
    mod_all = jnp.concatenate([_unpack_rows(g2_all[2 * k], g2_spans)[0] for k in range(N_CHIPS)], axis=1)
    modv = lax.dynamic_index_in_dim(mod_all, me, 0, keepdims=False).reshape(9, d)

    ex = _Exchange({n: wsh[n][0].astype(BF16) for n in BIG})
    ex.run("gather_send", "ffn1")
    ex.run("gather_pass", "ffn1")
    wt = _as_operands(ex.weights("ffn1"), lay)
    lane_pad = (jnp.zeros((2 * nh,), F32), jnp.zeros((LANES - 4 * nh,), F32))
    pvec = jnp.stack([jnp.concatenate([lane_pad[0], a_log_fwd[0], a_log_bwd[0], lane_pad[1]]),
                      jnp.concatenate([lane_pad[0], dt_bias_fwd[0], dt_bias_bwd[0], lane_pad[1]])]
                     + [jnp.zeros((LANES,), F32)] * 6)
    wt.update(conv_a=conv_a_full, conv_dn=conv_dn_full, pvec=pvec, dn_norm=dn_norm, norm_ffn1=norm_ffn1,
              norm_mix=norm_mix, norm_ffn2=norm_ffn2, norm_final=norm_final.reshape(1, d))

    loss, dx, dmod, big, small = _local_step(x[0], loss_target[0], modv, wt, lay, ex)
    loss = lax.psum(loss, ("x", "y", "c"))

    small_list = [dmod.reshape(1, 9 * d), small["norm_ffn1"], small["norm_mix"], small["conv_a"], small["conv_dn"],
                  small["a_log"][2 * nh:3 * nh], small["dt_bias"][2 * nh:3 * nh], small["a_log"][3 * nh:4 * nh],
                  small["dt_bias"][3 * nh:4 * nh], small["dn_norm"], small["norm_ffn2"], small["norm_final"]]
    g3, g3_spans = _pack_rows(small_list, LANES, 8, F32)
    g3_all = _allgather8("gather_small_grads", g3)
    g_small = dict(zip(SMALL, _unpack_rows(_sum8("sum_small_grads", g3_all), g3_spans)))
    dmod_all = jnp.concatenate([_unpack_rows(g3_all[k], g3_spans)[0] for k in range(N_DEV)], axis=0)
    ncol = w_ada.shape[-1]
    dmod_sh = lax.dynamic_slice_in_dim(dmod_all, chip * ncol, ncol, axis=1)
    grads = {"w_ada": _matmul("ada_grad", c_all, dmod_sh, "tn")[None]}
    for n in SMALL:
        g = g_small[n]
        if n in ("conv_a", "conv_dn"):
            wloc = wsh[n].shape[-1]
            g = lax.dynamic_slice_in_dim(g, chip * wloc, wloc, axis=1)
        grads[n] = g.reshape(wsh[n].shape)

    ex.reduce("ffn1", big)
    for kind in ("swap", "scatter", "share"):
        ex.run(kind, "ffn1")
    for group in GROUPS:
        for n, g in ex.grads(group).items():
            grads[n] = g[None]

    delta, new_m, new_v = {}, {}, {}
    for n in ("w_ada",) + BIG:
        shp = wsh[n].shape
        outs = _adamw("adamw_" + n, *(t.reshape(shp[-2], shp[-1]) for t in (wsh[n], grads[n], msh[n], vsh[n])))
        delta[n], new_m[n], new_v[n] = (o.reshape(shp) for o in outs)
    packed = []
    for src in (wsh, grads, msh, vsh):
        pk, s_spans = _pack_rows([src[n] for n in SMALL], LANES, 8, F32)
        packed.append(pk)
    outs = _adamw("adamw_small", *packed)
    for dst, o in zip((delta, new_m, new_v), outs):
        dst.update(zip(SMALL, _unpack_rows(o, s_spans)))

    return (loss, dx[None], *[grads[n] for n in WEIGHTS], *[delta[n] for n in WEIGHTS],
            *[new_m[n] for n in WEIGHTS], *[new_v[n] for n in WEIGHTS])
```

```python
import functools

import jax
import jax.numpy as jnp
from jax import lax
from jax.experimental import pallas as pl
from jax.experimental.pallas import tpu as pltpu

F32 = jnp.float32
BF16 = jnp.bfloat16
SDS = jax.ShapeDtypeStruct
MESH = pl.DeviceIdType.MESH
HI = lax.Precision.HIGHEST

EPS = 1e-6
HEAD = 128
CHUNK = 64
LANES = 128
N_CHIPS = 4
N_DEV = 8
VMEM_LIMIT = 56 * 1024 * 1024

ADAM_LR = 0.001
ADAM_B1 = 0.9
ADAM_B2 = 0.999
ADAM_EPS = 1e-08
ADAM_WD = 0.01
ADAM_STEP = 10


def _params(*sem):
    return pltpu.CompilerParams(dimension_semantics=sem, vmem_limit_bytes=VMEM_LIMIT)


def _tile(n, cap, mult=LANES):
    t = min(n, cap) // mult * mult
    while t >= mult:
        if n % t == 0:
            return t
        t -= mult
    return n


def _row(tr, w, cb=0):
    return pl.BlockSpec((tr, w), lambda i: (i, cb))


def _vec(r, w):
    return pl.BlockSpec((r, w), lambda i: (0, 0))


def _nn(a, b, **kw):
    return jnp.dot(a, b, preferred_element_type=F32, **kw)


def _nt(a, b, **kw):
    return lax.dot_general(a, b, (((1,), (1,)), ((), ())), preferred_element_type=F32, **kw)


def _tn(a, b, **kw):
    return lax.dot_general(a, b, (((0,), (0,)), ((), ())), preferred_element_type=F32, **kw)


def _bnn(a, b):
    return lax.dot_general(a, b, (((2,), (1,)), ((0,), (0,))), preferred_element_type=F32)


def _bnt(a, b):
    return lax.dot_general(a, b, (((2,), (2,)), ((0,), (0,))), preferred_element_type=F32)


def _btn(a, b):
    return lax.dot_general(a, b, (((1,), (1,)), ((0,), (0,))), preferred_element_type=F32)


def _silu_grad(x):
    s = jax.nn.sigmoid(x)
    return s * (1.0 + x * (1.0 - s))


def _matmul(name, a, b, mode, out_dtype=F32, tm=1024, tn=1024, tk=2048, full_k=2816, out_pieces=0, ex=None):
    pieces_b = b.shape[0] if b.ndim == 3 else 0
    b2 = b.shape[1:] if pieces_b else b.shape
    if mode == "nn":
        (m, k), n = a.shape, b2[1] * max(pieces_b, 1)
    elif mode == "nt":
        (m, _), n, k = a.shape, b2[0], b2[1] * max(pieces_b, 1)
    else:
        (k, m), n = a.shape, b2[1] * max(pieces_b, 1)
    n_unit = n // max(out_pieces, 1) if mode == "nt" or not pieces_b else n // pieces_b
    if out_pieces and pieces_b and mode != "nt":
        assert out_pieces == pieces_b
    k_unit = k // pieces_b if (pieces_b and mode == "nt") else k
    tm, tn = _tile(m, tm), _tile(n_unit, tn)
    tk = k_unit if k_unit <= full_k else _tile(k_unit, tk)
    nk = k // tk
    n_per, k_per = n_unit // tn, k_unit // tk
    a_bytes, b_bytes = a.size * a.dtype.itemsize, b.size * b.dtype.itemsize
    j_outer = nk == 1 and b_bytes + a_bytes * (n // tn) < a_bytes + b_bytes * (m // tm)
    ij = (lambda g0, g1: (g1, g0)) if j_outer else (lambda g0, g1: (g0, g1))

    def spec(shape, pick):
        return pl.BlockSpec(shape, lambda g0, g1, l: pick(*ij(g0, g1), l))

    a_spec = {"nn": spec((tm, tk), lambda i, j, l: (i, l)), "nt": spec((tm, tk), lambda i, j, l: (i, l)),
              "tn": spec((tk, tm), lambda i, j, l: (l, i))}[mode]
    if not pieces_b:
        b_spec = {"nn": spec((tk, tn), lambda i, j, l: (l, j)), "nt": spec((tn, tk), lambda i, j, l: (j, l)),
                  "tn": spec((tk, tn), lambda i, j, l: (l, j))}[mode]
    elif mode == "nt":
        b_spec = spec((None, tn, tk), lambda i, j, l: (l // k_per, j, l % k_per))
    else:
        b_spec = spec((None, tk, tn), lambda i, j, l: (j // n_per, l, j % n_per))
    if out_pieces:
        o_spec = spec((None, tm, tn), lambda i, j, l: (j // n_per, i, j % n_per))
        o_shape = SDS((out_pieces, m, n // out_pieces), out_dtype)
    else:
        o_spec, o_shape = spec((tm, tn), lambda i, j, l: (i, j)), SDS((m, n), out_dtype)
    dot = {"nn": _nn, "nt": _nt, "tn": _tn}[mode]
    grid = (n // tn, m // tm, nk) if j_outer else (m // tm, n // tn, nk)
    host = _Hosted(ex, name, grid)

    def body(a_ref, b_ref, *rest):
        (o_ref,), scratch = host.split(rest, 1)
        host.start()
        part = dot(a_ref[...].astype(BF16), b_ref[...].astype(BF16))
        if nk == 1:
            o_ref[...] = part.astype(o_ref.dtype)
        else:
            l, acc = pl.program_id(2), scratch[0]

            @pl.when(l == 0)
            def _():
                acc[...] = part

            @pl.when((l > 0) & (l < nk - 1))
            def _():
                acc[...] += part

            @pl.when(l == nk - 1)
            def _():
                o_ref[...] = (acc[...] + part).astype(o_ref.dtype)
        host.wait()

    out = pl.pallas_call(
        body, grid=grid, in_specs=[a_spec, b_spec] + host.in_specs, out_specs=[o_spec] + host.out_specs,
        out_shape=[o_shape] + host.out_shapes, input_output_aliases=host.aliases(2, 1),
        scratch_shapes=([] if nk == 1 else [pltpu.VMEM((tm, tn), F32)]) + host.sems,
        compiler_params=_params(*(("arbitrary",) * 3 if host.stage else ("parallel", "parallel", "arbitrary"))),
        name=name)(a, b, *host.arrays)
    return host.finish(out, 1)[0]


def _norm_mod(name, h, nw, sh, sc, tr=512):
    s, d = h.shape
    tr = _tile(s, tr, 8)

    def body(h_ref, nw_ref, sh_ref, sc_ref, u_ref):
        x = h_ref[...]
        r = lax.rsqrt(jnp.mean(x * x, axis=-1, keepdims=True) + EPS)
        u_ref[...] = (x * r * nw_ref[...] * (1.0 + sc_ref[...]) + sh_ref[...]).astype(BF16)

    return pl.pallas_call(
        body, grid=(s // tr,), in_specs=[_row(tr, d), _vec(1, d), _vec(1, d), _vec(1, d)],
        out_specs=_row(tr, d), out_shape=SDS((s, d), BF16),
        compiler_params=_params("parallel"), name=name)(h, nw, sh, sc)


def _norm_mod_bwd(name, h, du, dh, nw, sc, tr=512):
    s, d = h.shape
    tr = _tile(s, tr, 8)

    def body(h_ref, du_ref, dh_ref, nw_ref, sc_ref, o_ref, acc_ref):
        @pl.when(pl.program_id(0) == 0)
        def _():
            acc_ref[...] = jnp.zeros_like(acc_ref)

        x, g = h_ref[...], du_ref[...]
        r = lax.rsqrt(jnp.mean(x * x, axis=-1, keepdims=True) + EPS)
        n = x * r
        nw, sc1 = nw_ref[...], 1.0 + sc_ref[...]
        dn = g * sc1 * nw
        o_ref[...] = dh_ref[...] + r * (dn - n * jnp.mean(dn * n, axis=-1, keepdims=True))
        gn = g * n
        acc_ref[0:1, :] += jnp.sum(g, axis=0, keepdims=True)
        acc_ref[1:2, :] += jnp.sum(gn * nw, axis=0, keepdims=True)
        acc_ref[2:3, :] += jnp.sum(gn * sc1, axis=0, keepdims=True)

    return pl.pallas_call(
        body, grid=(s // tr,),
        in_specs=[_row(tr, d), _row(tr, d), _row(tr, d), _vec(1, d), _vec(1, d)],
        out_specs=[_row(tr, d), _vec(8, d)], out_shape=[SDS((s, d), F32), SDS((8, d), F32)],
        compiler_params=_params("arbitrary"), name=name)(h, du, dh, nw, sc)


def _resid_bwd(name, dh, f, g, scale, tr=512):
    s, d = dh.shape
    tr = _tile(s, tr, 8)

    def body(dh_ref, f_ref, g_ref, o_ref, acc_ref):
        @pl.when(pl.program_id(0) == 0)
        def _():
            acc_ref[...] = jnp.zeros_like(acc_ref)

        x = dh_ref[...]
        o_ref[...] = ((scale * g_ref[...]) * x).astype(BF16)
        acc_ref[0:1, :] += jnp.sum(scale * x * f_ref[...], axis=0, keepdims=True)

    return pl.pallas_call(
        body, grid=(s // tr,), in_specs=[_row(tr, d), _row(tr, d), _vec(1, d)],
        out_specs=[_row(tr, d), _vec(8, d)], out_shape=[SDS((s, d), BF16), SDS((8, d), F32)],
        compiler_params=_params("arbitrary"), name=name)(dh, f, g)


def _final_loss(name, h, tgt, nw, tr=512):
    s, d = h.shape
    tr = _tile(s, tr, 8)

    def body(h_ref, t_ref, nw_ref, o_ref, acc_ref):
        @pl.when(pl.program_id(0) == 0)
        def _():
            acc_ref[...] = jnp.zeros_like(acc_ref)

        x, nw = h_ref[...], nw_ref[...]
        r = lax.rsqrt(jnp.mean(x * x, axis=-1, keepdims=True) + EPS)
        n = x * r
        diff = n * nw - t_ref[...]
        dy = diff * (1.0 / d)
        dn = dy * nw
        o_ref[...] = r * (dn - n * jnp.mean(dn * n, axis=-1, keepdims=True))
        acc_ref[0:1, :] += jnp.sum(dy * n, axis=0, keepdims=True)
        acc_ref[1:2, :] += jnp.sum(diff * diff, axis=0, keepdims=True) * (0.5 / d)

    return pl.pallas_call(
        body, grid=(s // tr,), in_specs=[_row(tr, d), _row(tr, d), _vec(1, d)],
        out_specs=[_row(tr, d), _vec(8, d)], out_shape=[SDS((s, d), F32), SDS((8, d), F32)],
        compiler_params=_params("arbitrary"), name=name)(h, tgt, nw)


class _Layout:
    def __init__(self, d, ca, nh):
        self.d, self.ca, self.nh = d, ca, nh
        self.qk = nh * HEAD
        self.qkv = 3 * self.qk
        self.z = self.qkv
        self.ga = self.z + self.qk
        self.cab = self.ga + 2 * d
        self.ba = self.cab + 3 * ca
        self.tail = _tile(self.ba, 512)
        self.total = self.ba + self.tail
        assert self.qkv % self.qk == 0 and self.ga % (2 * d) == 0 and self.cab % (3 * ca) == 0
        assert self.ba % self.tail == 0 and 4 * nh <= LANES

    def perm_cols(self, w):
        ca, qkv, qk, d, nh = self.ca, self.qkv, self.qk, self.d, self.nh
        o = [0, ca, 2 * ca, 3 * ca, 3 * ca + qkv, 3 * ca + qkv + qk, 3 * ca + qkv + qk + 4 * nh]
        cb, cc, cv = (w[..., o[i]:o[i + 1]] for i in range(3))
        x_qkv, x_z, x_ba = w[..., o[3]:o[4]], w[..., o[4]:o[5]], w[..., o[5]:o[6]]
        gates = w[..., o[6]:o[6] + 2 * d]
        pad = jnp.zeros(w.shape[:-1] + (self.tail - 4 * nh,), w.dtype)
        return jnp.concatenate([x_qkv, x_z, gates, cb, cc, cv, x_ba, pad], axis=-1)

    def unperm_cols(self, w):
        ca, nh = self.ca, self.nh
        cb, cc, cv = (w[..., self.cab + i * ca:self.cab + (i + 1) * ca] for i in range(3))
        return jnp.concatenate([cb, cc, cv, w[..., 0:self.qkv], w[..., self.z:self.ga],
                                w[..., self.ba:self.ba + 4 * nh], w[..., self.ga:self.cab]], axis=-1)


def _halo_specs(tr, w, cb, s):
    nb8 = s // 8
    return [pl.BlockSpec((8, w), lambda i: (jnp.maximum(i * (tr // 8) - 1, 0), cb)),
            pl.BlockSpec((tr, w), lambda i: (i, cb)),
            pl.BlockSpec((8, w), lambda i: (jnp.minimum((i + 1) * (tr // 8), nb8 - 1), cb))]


def _ext(prev_ref, main_ref, next_ref, i, nt):
    p = jnp.where(i > 0, prev_ref[...].astype(F32), 0.0)
    n = jnp.where(i < nt - 1, next_ref[...].astype(F32), 0.0)
    return jnp.concatenate([p, main_ref[...].astype(F32), n], axis=0)


def _shift(x, k):
    return x if k == 0 else pltpu.roll(x, (-k) % x.shape[0], 0)


def _conv_taps(x_ext, w, tr):
    kt = w.shape[0]
    acc = None
    for t in range(kt):
        term = _shift(x_ext, t - kt // 2)[8:8 + tr] * w[t:t + 1, :]
        acc = term if acc is None else acc + term
    return acc


def _prep_a(name, proj, conv_a, lay, tr=256):
    s, ca = proj.shape[0], lay.ca
    tr = _tile(s, tr, 8)
    nt, w = s // tr, 3 * ca

    def body(p_ref, m_ref, n_ref, w_ref, o_ref):
        x = _ext(p_ref, m_ref, n_ref, pl.program_id(0), nt)
        xv = x[:, ca:2 * ca] * x[:, 2 * ca:w]
        y = _conv_taps(xv, w_ref[...], tr)
        o_ref[...] = (m_ref[:, 0:ca] * y).astype(BF16)

    return pl.pallas_call(
        body, grid=(nt,), in_specs=_halo_specs(tr, w, lay.cab // w, s) + [_vec(conv_a.shape[0], ca)],
        out_specs=_row(tr, ca), out_shape=SDS((s, ca), BF16),
        compiler_params=_params("parallel"), name=name)(proj, proj, proj, conv_a)


def _prep_a_bwd(name, dya, proj, conv_a, dproj, lay, tr=256):
    s, ca = proj.shape[0], lay.ca
    tr = _tile(s, tr, 8)
    nt, w, kt = s // tr, 3 * ca, conv_a.shape[0]

    def body(p_ref, m_ref, n_ref, dp_ref, dm_ref, dn_ref, w_ref, _, o_ref, acc_ref):
        i = pl.program_id(0)

        @pl.when(i == 0)
        def _():
            acc_ref[...] = jnp.zeros_like(acc_ref)

        x = _ext(p_ref, m_ref, n_ref, i, nt)
        d_ext = _ext(dp_ref, dm_ref, dn_ref, i, nt)
        cb, cc, cv = x[:, 0:ca], x[:, ca:2 * ca], x[:, 2 * ca:w]
        xv = cc * cv
        wv = w_ref[...]
        dy_ext = d_ext * cb
        dx = None
        for t in range(kt):
            term = _shift(dy_ext, kt // 2 - t)[8:8 + tr] * wv[t:t + 1, :]
            dx = term if dx is None else dx + term
            acc_ref[t:t + 1, :] += jnp.sum(dy_ext[8:8 + tr] * _shift(xv, t - kt // 2)[8:8 + tr],
                                           axis=0, keepdims=True)
        y = _conv_taps(xv, wv, tr)
        o_ref[:, 0:ca] = (dm_ref[...] * y).astype(BF16)
        o_ref[:, ca:2 * ca] = (dx * cv[8:8 + tr]).astype(BF16)
        o_ref[:, 2 * ca:w] = (dx * cc[8:8 + tr]).astype(BF16)

    return pl.pallas_call(
        body, grid=(nt,),
        in_specs=_halo_specs(tr, w, lay.cab // w, s) + _halo_specs(tr, ca, 0, s)
        + [_vec(kt, ca), pl.BlockSpec(memory_space=pl.ANY)],
        out_specs=[_row(tr, w, lay.cab // w), _vec(8, ca)],
        out_shape=[SDS(dproj.shape, dproj.dtype), SDS((8, ca), F32)], input_output_aliases={7: 0},
        compiler_params=_params("arbitrary"), name=name)(proj, proj, proj, dya, dya, dya, conv_a, dproj)


def _qkv_act(c, nh, tr_rows):
    sact = jax.nn.silu(c)
    outs, inv = [], []
    for hd in range(3 * nh):
        sl = sact[:, hd * HEAD:(hd + 1) * HEAD]
        if hd < 2 * nh:
            r = lax.rsqrt(jnp.sum(sl * sl, axis=-1, keepdims=True) + EPS)
            inv.append(r)
            outs.append(sl * (r * (HEAD ** -0.5 if hd < nh else 1.0)))
        else:
            outs.append(sl)
    return jnp.concatenate(outs, axis=-1), sact, inv


def _prep_b(name, proj, conv_dn, lay, tr=256):
    s, w, nh = proj.shape[0], lay.qkv, lay.nh
    tr = _tile(s, tr, 8)
    nt = s // tr

    def body(p_ref, m_ref, n_ref, w_ref, o_ref):
        x = _ext(p_ref, m_ref, n_ref, pl.program_id(0), nt)
        c = _conv_taps(x, w_ref[...], tr)
        o_ref[...] = _qkv_act(c, nh, tr)[0]

    return pl.pallas_call(
        body, grid=(nt,), in_specs=_halo_specs(tr, w, 0, s) + [_vec(conv_dn.shape[0], w)],
        out_specs=_row(tr, w), out_shape=SDS((s, w), F32),
        compiler_params=_params("parallel"), name=name)(proj, proj, proj, conv_dn)


def _prep_b_bwd(name, dq, dk, dv, proj, conv_dn, dproj, lay, tr=256):
    s, w, nh, qk = proj.shape[0], lay.qkv, lay.nh, lay.qk
    tr = _tile(s, tr, 8)
    nt, kt = s // tr, conv_dn.shape[0]
    n_ext = tr + 16

    def body(*refs):
        x_refs, g_refs = refs[0:3], refs[3:12]
        w_ref, o_ref, acc_ref = refs[12], refs[14], refs[15]
        i = pl.program_id(0)

        @pl.when(i == 0)
        def _():
            acc_ref[...] = jnp.zeros_like(acc_ref)

        x = _ext(*x_refs, i, nt)
        wv = w_ref[...]
        c = None
        for t in range(kt):
            term = _shift(x, t - kt // 2) * wv[t:t + 1, :]
            c = term if c is None else c + term
        sact = jax.nn.silu(c)
        ds = []
        for part in range(3):
            g = _ext(*g_refs[3 * part:3 * part + 3], i, nt)
            for hd in range(nh):
                sl = sact[:, part * qk + hd * HEAD:part * qk + (hd + 1) * HEAD]
                gh = g[:, hd * HEAD:(hd + 1) * HEAD]
                if part < 2:
                    r = lax.rsqrt(jnp.sum(sl * sl, axis=-1, keepdims=True) + EPS)
                    sc = HEAD ** -0.5 if part == 0 else 1.0
                    ds.append(sc * r * (gh - sl * (r * r) * jnp.sum(gh * sl, axis=-1, keepdims=True)))
                else:
                    ds.append(gh)
        dc = jnp.concatenate(ds, axis=-1) * _silu_grad(c)
        rows = lax.broadcasted_iota(jnp.int32, (n_ext, 1), 0)
        dc = jnp.where((rows >= 2) & (rows < n_ext - 2), dc, 0.0)
        dx = None
        for t in range(kt):
            term = _shift(dc, kt // 2 - t)[8:8 + tr] * wv[t:t + 1, :]
            dx = term if dx is None else dx + term
            acc_ref[t:t + 1, :] += jnp.sum(dc[8:8 + tr] * _shift(x, t - kt // 2)[8:8 + tr],
                                           axis=0, keepdims=True)
        o_ref[...] = dx.astype(BF16)

    return pl.pallas_call(
        body, grid=(nt,),
        in_specs=_halo_specs(tr, w, 0, s) + _halo_specs(tr, qk, 0, s) * 3
        + [_vec(kt, w), pl.BlockSpec(memory_space=pl.ANY)],
        out_specs=[_row(tr, w, 0), _vec(8, w)],
        out_shape=[SDS(dproj.shape, dproj.dtype), SDS((8, w), F32)], input_output_aliases={13: 0},
        compiler_params=_params("arbitrary"), name=name)(
            proj, proj, proj, dq, dq, dq, dk, dk, dk, dv, dv, dv, conv_dn, dproj)


def _softplus(x):
    return jnp.maximum(x, 0.0) + jnp.log(1.0 + jnp.exp(-jnp.abs(x)))


def _split3(x):
    hi = x.astype(BF16)
    r = x - hi.astype(F32)
    mid = r.astype(BF16)
    return hi, mid, (r - mid.astype(F32)).astype(BF16)


def _exact_nn(m, x):
    m = m.astype(BF16)
    hi, mid, lo = _split3(x)
    return _nn(m, hi) + _nn(m, mid) + _nn(m, lo)


def _chunk_cumsum_masks(tr):
    ri = lax.broadcasted_iota(jnp.int32, (tr, tr), 0)
    ci = lax.broadcasted_iota(jnp.int32, (tr, tr), 1)
    same = (ri // CHUNK) == (ci // CHUNK)
    return (same & (ci <= ri)).astype(F32), (same & (ci >= ri)).astype(F32)


def _prep_c(name, proj, pvec, lay, tr=512):
    s, nh = proj.shape[0], lay.nh
    tr = _tile(s, tr, CHUNK)
    assert 6 * nh <= LANES

    def body(x_ref, p_ref, o_ref):
        x = x_ref[...]
        lane = lax.broadcasted_iota(jnp.int32, x.shape, 1)
        is_g = (lane >= 2 * nh) & (lane < 4 * nh)
        g = jnp.where(is_g, -jnp.exp(p_ref[0:1, :]) * _softplus(x + p_ref[1:2, :]), 0.0)
        m_f, m_b = _chunk_cumsum_masks(tr)
        gc = jnp.where(lane < 3 * nh, _exact_nn(m_f, g), _exact_nn(m_b, g))
        gc = pltpu.roll(gc, 2 * nh, 1)
        o_ref[...] = jnp.where(lane < 2 * nh, jax.nn.sigmoid(x), jnp.where(lane < 4 * nh, g, gc))

    return pl.pallas_call(
        body, grid=(s // tr,), in_specs=[_row(tr, LANES, lay.ba // LANES), _vec(8, LANES)],
        out_specs=_row(tr, LANES), out_shape=SDS((s, LANES), F32),
        compiler_params=_params("parallel"), name=name)(proj, pvec)


def _prep_c_bwd(name, dbg_f, dbg_b, proj, pvec, dproj, lay, tr=512):
    s, nh, tail = proj.shape[0], lay.nh, lay.tail
    tr = _tile(s, tr, CHUNK)

    def body(x_ref, df_ref, db_ref, p_ref, _, o_ref, acc_ref):
        @pl.when(pl.program_id(0) == 0)
        def _():
            acc_ref[...] = jnp.zeros_like(acc_ref)

        x = x_ref[...]
        lane = lax.broadcasted_iota(jnp.int32, x.shape, 1)
        is_b, is_g = lane < 2 * nh, (lane >= 2 * nh) & (lane < 4 * nh)
        fwd_lane = (lane < nh) | ((lane >= 2 * nh) & (lane < 3 * nh))
        d = jnp.where(lane < 4 * nh, jnp.where(fwd_lane, df_ref[...], db_ref[...]), 0.0)
        m_f, m_b = _chunk_cumsum_masks(tr)
        dgc = jnp.where(is_g, d, 0.0)
        dg = jnp.where(fwd_lane, _exact_nn(m_b, dgc), _exact_nn(m_f, dgc))
        sb = jax.nn.sigmoid(x)
        na = -jnp.exp(p_ref[0:1, :])
        xs = x + p_ref[1:2, :]
        dsp = dg * na * jax.nn.sigmoid(xs)
        dx = jnp.where(is_b, d * sb * (1.0 - sb), jnp.where(is_g, dsp, 0.0))
        o_ref[...] = jnp.zeros_like(o_ref)
        o_ref[:, 0:LANES] = dx.astype(BF16)
        acc_ref[0:1, :] += jnp.sum(jnp.where(is_g, dg * na * _softplus(xs), 0.0), axis=0, keepdims=True)
        acc_ref[1:2, :] += jnp.sum(jnp.where(is_g, dsp, 0.0), axis=0, keepdims=True)

    return pl.pallas_call(
        body, grid=(s // tr,),
        in_specs=[_row(tr, LANES, lay.ba // LANES), _row(tr, LANES), _row(tr, LANES), _vec(8, LANES),
                  pl.BlockSpec(memory_space=pl.ANY)],
        out_specs=[_row(tr, tail, lay.ba // tail), _vec(8, LANES)],
        out_shape=[SDS(dproj.shape, dproj.dtype), SDS((8, LANES), F32)], input_output_aliases={4: 0},
        compiler_params=_params("arbitrary"), name=name)(proj, dbg_f, dbg_b, pvec, dproj)


def _post(name, o_f, o_b, proj, dn_w, lay, tr=256):
    s, qk, nh = o_f.shape[0], lay.qk, lay.nh
    tr = _tile(s, tr, 8)

    def body(f_ref, b_ref, z_ref, w_ref, o_ref):
        o = f_ref[...] + b_ref[...]
        gate = jax.nn.silu(z_ref[...])
        for hd in range(nh):
            sl = slice(hd * HEAD, (hd + 1) * HEAD)
            oh = o[:, sl]
            r = lax.rsqrt(jnp.mean(oh * oh, axis=-1, keepdims=True) + EPS)
            o_ref[:, sl] = (oh * r * w_ref[...] * gate[:, sl]).astype(BF16)

    return pl.pallas_call(
        body, grid=(s // tr,),
        in_specs=[_row(tr, qk), _row(tr, qk), _row(tr, qk, lay.z // qk), _vec(1, HEAD)],
        out_specs=_row(tr, qk), out_shape=SDS((s, qk), BF16),
        compiler_params=_params("parallel"), name=name)(o_f, o_b, proj, dn_w)


def _post_bwd(name, dyb, o_f, o_b, proj, dn_w, dproj, lay, tr=256):
    s, qk, nh = o_f.shape[0], lay.qk, lay.nh
    tr = _tile(s, tr, 8)

    def body(d_ref, f_ref, b_ref, z_ref, w_ref, _, do_ref, dz_ref, acc_ref):
        @pl.when(pl.program_id(0) == 0)
        def _():
            acc_ref[...] = jnp.zeros_like(acc_ref)

        o, z, d, wv = f_ref[...] + b_ref[...], z_ref[...], d_ref[...], w_ref[...]
        gate = jax.nn.silu(z)
        dgate = _silu_grad(z)
        for hd in range(nh):
            sl = slice(hd * HEAD, (hd + 1) * HEAD)
            oh, dh = o[:, sl], d[:, sl]
            r = lax.rsqrt(jnp.mean(oh * oh, axis=-1, keepdims=True) + EPS)
            n = oh * r
            dz_ref[:, sl] = (dh * n * wv * dgate[:, sl]).astype(BF16)
            don = dh * gate[:, sl]
            acc_ref[0:1, :] += jnp.sum(don * n, axis=0, keepdims=True)
            dn = don * wv
            do_ref[:, sl] = r * (dn - n * jnp.mean(dn * n, axis=-1, keepdims=True))

    return pl.pallas_call(
        body, grid=(s // tr,),
        in_specs=[_row(tr, qk), _row(tr, qk), _row(tr, qk), _row(tr, qk, lay.z // qk), _vec(1, HEAD),
                  pl.BlockSpec(memory_space=pl.ANY)],
        out_specs=[_row(tr, qk), _row(tr, qk, lay.z // qk), _vec(8, HEAD)],
        out_shape=[SDS((s, qk), F32), SDS(dproj.shape, dproj.dtype), SDS((8, HEAD), F32)],
        input_output_aliases={5: 1},
        compiler_params=_params("arbitrary"), name=name)(dyb, o_f, o_b, proj, dn_w, dproj)


def _merge(name, pa, pb, proj, lay, tr=512):
    s, d = pa.shape
    tr = _tile(s, tr, 8)

    def body(a_ref, b_ref, g_ref, o_ref):
        o_ref[...] = (jax.nn.sigmoid(g_ref[:, 0:d]) * a_ref[...]
                      + jax.nn.sigmoid(g_ref[:, d:2 * d]) * b_ref[...]).astype(BF16)

    return pl.pallas_call(
        body, grid=(s // tr,), in_specs=[_row(tr, d), _row(tr, d), _row(tr, 2 * d, lay.ga // (2 * d))],
        out_specs=_row(tr, d), out_shape=SDS((s, d), BF16),
        compiler_params=_params("parallel"), name=name)(pa, pb, proj)


def _merge_bwd(name, dmg, pa, pb, proj, lay, tr=512):
    s, d = pa.shape
    tr = _tile(s, tr, 8)

    def body(d_ref, a_ref, b_ref, g_ref, da_ref, db_ref, dg_ref):
        dm = d_ref[...]
        sa, sb = jax.nn.sigmoid(g_ref[:, 0:d]), jax.nn.sigmoid(g_ref[:, d:2 * d])
        da_ref[...] = (sa * dm).astype(BF16)
        db_ref[...] = (sb * dm).astype(BF16)
        dg_ref[:, 0:d] = (dm * a_ref[...] * sa * (1.0 - sa)).astype(BF16)
        dg_ref[:, d:2 * d] = (dm * b_ref[...] * sb * (1.0 - sb)).astype(BF16)

    return pl.pallas_call(
        body, grid=(s // tr,),
        in_specs=[_row(tr, d), _row(tr, d), _row(tr, d), _row(tr, 2 * d, lay.ga // (2 * d))],
        out_specs=[_row(tr, d), _row(tr, d), _row(tr, 2 * d, lay.ga // (2 * d))],
        out_shape=[SDS((s, d), BF16), SDS((s, d), BF16), SDS((s, lay.total), BF16)],
        compiler_params=_params("parallel"), name=name)(dmg, pa, pb, proj)


def _tri_inverse(a_mat, ri, ci):
    def same(shift):
        return (ri >> shift) == (ci >> shift)

    x = -jnp.where(same(3), a_mat, 0.0)
    t_mat = (ri == ci).astype(F32) + x
    for _ in range(2):
        x = _bnn(x, x)
        t_mat = t_mat + _bnn(t_mat, x)
    for shift in (3, 4, 5):
        b = jnp.where(same(shift + 1) & ~same(shift), a_mat, 0.0)
        t_mat = t_mat - _bnn(_bnn(t_mat, b), t_mat)
    return t_mat


def _chunk_terms(q, k, v, beta, gc, g_row, g_last, reverse, t_mat=None):
    c = CHUNK
    ri = lax.broadcasted_iota(jnp.int32, (c, c), 0)
    ci = lax.broadcasted_iota(jnp.int32, (c, c), 1)
    if reverse:
        incl, strict = ri <= ci, ri < ci
    else:
        incl, strict = ri >= ci, ri > ci
    decay = jnp.where(incl, jnp.exp(jnp.where(incl, gc - g_row, 0.0)), 0.0)
    e = jnp.exp(gc)
    ed = jnp.exp(g_last - gc)
    el = jnp.exp(g_last)
    kb = k * beta
    a_mat = jnp.where(strict, _bnt(kb, k) * decay, 0.0)
    if t_mat is None:
        t_mat = _tri_inverse(a_mat, ri, ci)
    u = _bnn(t_mat, v * beta)
    w = _bnn(t_mat, kb * e)
    p_mat = jnp.where(incl, _bnt(q, k) * decay, 0.0)
    return dict(incl=incl, strict=strict, decay=decay, e=e, ed=ed, el=el, kb=kb,
                a=a_mat, t=t_mat, u=u, w=w, p=p_mat)


def _delta_specs(nh, tb, nb, reverse):
    tok = (lambda i: nb - 1 - i) if reverse else (lambda i: i)
    hw = nh * HEAD
    qkv = [pl.BlockSpec((tb, hw), functools.partial(lambda i, part: (tok(i), part), part=p)) for p in range(3)]
    rows = pl.BlockSpec((tb, hw), lambda i: (tok(i), 0))
    bg = pl.BlockSpec((tb, LANES), lambda i: (tok(i), 0))
    gct = pl.BlockSpec((2 * nh, tb), lambda i: (0, tok(i)))
    st = pl.BlockSpec((nh, tb // CHUNK, HEAD, HEAD), lambda i: (0, tok(i), 0, 0))
    tri = pl.BlockSpec((nh, tb // CHUNK, CHUNK, CHUNK), lambda i: (0, tok(i), 0, 0))
    return qkv, rows, bg, gct, st, tri


def _heads(ref, rows, nh):
    return jnp.stack([ref[rows, hd * HEAD:(hd + 1) * HEAD] for hd in range(nh)])


def _chunk_scalars(bg_ref, gct_ref, cj, nh, tb, reverse):
    rows = pl.ds(cj * CHUNK, CHUNK)
    lb = nh if reverse else 0
    lc = 4 * nh + lb
    last = cj * CHUNK + (0 if reverse else CHUNK - 1)
    g_lanes = gct_ref[lb:lb + nh, :]
    if cj:
        g_lanes = pltpu.roll(g_lanes, tb - cj * CHUNK, 1)
    col = lambda l0, r: jnp.stack([bg_ref[r, l0 + hd:l0 + hd + 1] for hd in range(nh)])
    return col(lb, rows), col(lc, rows), g_lanes[:, 0:CHUNK][:, None, :], col(lc, pl.ds(last, 1))


def _delta_fwd(name, qkvn, bg, gct, nh, reverse, tb=128):
    s = qkvn.shape[0]
    tb = _tile(s, tb, LANES)
    nb, cpb = s // tb, tb // CHUNK
    qkv, rows_spec, bg_spec, gct_spec, st, tri = _delta_specs(nh, tb, nb, reverse)

    def body(q_ref, k_ref, v_ref, bg_ref, gct_ref, o_ref, st_ref, tri_ref, state):
        @pl.when(pl.program_id(0) == 0)
        def _():
            state[...] = jnp.zeros_like(state)

        for cj in (range(cpb - 1, -1, -1) if reverse else range(cpb)):
            rows = pl.ds(cj * CHUNK, CHUNK)
            q, k, v = _heads(q_ref, rows, nh), _heads(k_ref, rows, nh), _heads(v_ref, rows, nh)
            tm = _chunk_terms(q, k, v, *_chunk_scalars(bg_ref, gct_ref, cj, nh, tb, reverse), reverse)
            s_in = state[...]
            st_ref[:, cj] = s_in
            tri_ref[:, cj] = tm["t"]
            vn = tm["u"] - _bnn(tm["w"], s_in)
            o = _bnn(q * tm["e"], s_in) + _bnn(tm["p"], vn)
            for hd in range(nh):
                o_ref[rows, hd * HEAD:(hd + 1) * HEAD] = o[hd]
            state[...] = s_in * tm["el"] + _btn(k * tm["ed"], vn)

    return pl.pallas_call(
        body, grid=(nb,), in_specs=qkv + [bg_spec, gct_spec], out_specs=[rows_spec, st, tri],
        out_shape=[SDS((s, nh * HEAD), F32), SDS((nh, s // CHUNK, HEAD, HEAD), F32),
                   SDS((nh, s // CHUNK, CHUNK, CHUNK), F32)],
        scratch_shapes=[pltpu.VMEM((nh, HEAD, HEAD), F32)],
        compiler_params=_params("arbitrary"), name=name)(qkvn, qkvn, qkvn, bg, gct)


def _delta_bwd(name, qkvn, bg, gct, do, states, tris, nh, reverse, add=None, tb=128, ex=None):
    s = qkvn.shape[0]
    tb = _tile(s, tb, LANES)
    nb, cpb = s // tb, tb // CHUNK
    qkv, rows_spec, bg_spec, gct_spec, st, tri = _delta_specs(nh, tb, nb, not reverse)
    n_add = 0 if add is None else 3
    host = _Hosted(ex, name, (nb,))

    def body(*refs):
        q_ref, k_ref, v_ref, bg_ref, gct_ref, do_ref, st_ref, tri_ref = refs[0:8]
        add_refs = refs[8:8 + n_add]
        (dq_ref, dk_ref, dv_ref, dbg_ref), (dstate,) = host.split(refs[8 + n_add:], 4)
        host.start()

        @pl.when(pl.program_id(0) == 0)
        def _():
            dstate[...] = jnp.zeros_like(dstate)

        ones = jnp.ones((nh, CHUNK, HEAD), BF16)
        row_id = lax.broadcasted_iota(jnp.int32, (CHUNK, 1), 0)
        rsum = lambda x: jnp.sum(x, axis=2, keepdims=True)
        for cj in (range(cpb) if reverse else range(cpb - 1, -1, -1)):
            rows = pl.ds(cj * CHUNK, CHUNK)
            q, k, v, d_o = (_heads(r, rows, nh) for r in (q_ref, k_ref, v_ref, do_ref))
            beta, gc, g_row, g_last = _chunk_scalars(bg_ref, gct_ref, cj, nh, tb, reverse)
            tm = _chunk_terms(q, k, v, beta, gc, g_row, g_last, reverse, t_mat=tri_ref[:, cj])
            incl, strict, e, ed, el, kb = tm["incl"], tm["strict"], tm["e"], tm["ed"], tm["el"], tm["kb"]
            t_mat, u, w, p_mat, decay = tm["t"], tm["u"], tm["w"], tm["p"], tm["decay"]
            s_in, ds_out = st_ref[:, cj], dstate[...]
            vn = u - _bnn(w, s_in)
            qe, kd, ke = q * e, k * ed, kb * e
            dvn = _btn(p_mat, d_o) + _bnn(kd, ds_out)
            dqe = _bnt(d_o, s_in)
            dq = dqe * e
            dgc = rsum(dqe * qe)
            dp = jnp.where(incl, _bnt(d_o, vn), 0.0)
            dkd = _bnt(vn, ds_out)
            dk = dkd * ed
            r = rsum(dkd * kd)
            dgc = dgc - r
            dg_last = (jnp.sum(r, axis=1, keepdims=True)
                       + jnp.sum(rsum(ds_out * s_in), axis=1, keepdims=True) * el)
            dw = -_bnt(dvn, s_in)
            dbv = _btn(t_mat, dvn)
            dke = _btn(t_mat, dw)
            da = -jnp.where(strict, _bnt(dbv, u) + _bnt(dke, w), 0.0)
            m_mat, n_mat = da * decay, dp * decay
            dkb = _bnn(m_mat, k) + dke * e
            dk = dk + _btn(m_mat, kb) + _btn(n_mat, q)
            dq = dq + _bnn(n_mat, k)
            g_mat = da * tm["a"] + dp * p_mat
            g_hi, g_mid, g_lo = _split3(g_mat)
            col = (_btn(g_hi, ones) + _btn(g_mid, ones) + _btn(g_lo, ones))[:, :, 0:1]
            dgc = dgc + rsum(g_mat) - col + rsum(dke * ke)
            dgc = dgc + jnp.where(row_id == (0 if reverse else CHUNK - 1), dg_last, 0.0)
            dv = dbv * beta
            dbeta = rsum(dbv * v) + rsum(dkb * k)
            dk = dk + dkb * beta
            dstate[...] = el * ds_out + _btn(qe, d_o) - _btn(w, dvn)
            lb = nh if reverse else 0
            for hd in range(nh):
                cols = slice(hd * HEAD, (hd + 1) * HEAD)
                extra = [a[rows, cols] for a in add_refs] if n_add else [0.0, 0.0, 0.0]
                dq_ref[rows, cols] = dq[hd] + extra[0]
                dk_ref[rows, cols] = dk[hd] + extra[1]
                dv_ref[rows, cols] = dv[hd] + extra[2]
                dbg_ref[rows, lb + hd:lb + hd + 1] = dbeta[hd]
                dbg_ref[rows, 2 * nh + lb + hd:2 * nh + lb + hd + 1] = dgc[hd]
        host.wait()

    out3 = SDS((s, nh * HEAD), F32)
    n_in = 8 + n_add
    out = pl.pallas_call(
        body, grid=(nb,),
        in_specs=qkv + [bg_spec, gct_spec, rows_spec, st, tri] + [rows_spec] * n_add + host.in_specs,
        out_specs=[rows_spec, rows_spec, rows_spec, bg_spec] + host.out_specs,
        out_shape=[out3, out3, out3, SDS((s, LANES), F32)] + host.out_shapes,
        input_output_aliases=host.aliases(n_in, 4), scratch_shapes=[pltpu.VMEM((nh, HEAD, HEAD), F32)] + host.sems,
        compiler_params=_params("arbitrary"), name=name)(
            qkvn, qkvn, qkvn, bg, gct, do, states, tris, *(add or ()), *host.arrays)
    return host.finish(out, 4)


def _row_pieces(g):
    return g.reshape(N_CHIPS, g.shape[0] // N_CHIPS, g.shape[1])


def _up_swiglu(name, u, w_up, tm=1024, ex=None):
    s, k = u.shape
    fh = w_up.shape[2]
    tm = _tile(s, tm, 8)
    grid = (2, s // tm)
    host = _Hosted(ex, name, grid)

    def body(u_ref, wa_ref, wb_ref, *rest):
        (a_ref, b_ref, o_ref), _ = host.split(rest, 3)
        host.start()
        x = u_ref[...]
        a, b = _nn(x, wa_ref[...]), _nn(x, wb_ref[...])
        a_ref[...], b_ref[...] = a.astype(BF16), b.astype(BF16)
        o_ref[...] = (jax.nn.silu(a) * b).astype(BF16)
        host.wait()

    tile = pl.BlockSpec((tm, fh), lambda j, i: (i, j))
    out = pl.pallas_call(
        body, grid=grid,
        in_specs=[pl.BlockSpec((tm, k), lambda j, i: (i, 0)), pl.BlockSpec((None, k, fh), lambda j, i: (j, 0, 0)),
                  pl.BlockSpec((None, k, fh), lambda j, i: (2 + j, 0, 0))] + host.in_specs,
        out_specs=[tile] * 3 + host.out_specs, out_shape=[SDS((s, 2 * fh), BF16)] * 3 + host.out_shapes,
        input_output_aliases=host.aliases(3, 3), scratch_shapes=host.sems,
        compiler_params=_params("arbitrary", "arbitrary"), name=name)(u, w_up, w_up, *host.arrays)
    return host.finish(out, 3)


def _matmul_resid(name, a, w, h, g, scale, tm=512, ex=None):
    s, k = a.shape
    d = w.shape[1]
    tm = _tile(s, tm, 16)
    grid = (s // tm,)
    host = _Hosted(ex, name, grid)

    def body(a_ref, w_ref, h_ref, g_ref, *rest):
        (o_ref, f_ref), _ = host.split(rest, 2)
        host.start()
        f = _nn(a_ref[...], w_ref[...])
        f_ref[...] = f.astype(BF16)
        o_ref[...] = h_ref[...] + (scale * g_ref[...]) * f
        host.wait()

    out = pl.pallas_call(
        body, grid=grid, in_specs=[_row(tm, k), _vec(k, d), _row(tm, d), _vec(1, d)] + host.in_specs,
        out_specs=[_row(tm, d), _row(tm, d)] + host.out_specs,
        out_shape=[SDS((s, d), F32), SDS((s, d), BF16)] + host.out_shapes,
        input_output_aliases=host.aliases(4, 2), scratch_shapes=host.sems,
        compiler_params=_params("arbitrary"), name=name)(a, w, h, g, *host.arrays)
    return host.finish(out, 2)


def _down_swiglu_bwd(name, df, w_down, a_pre, b_pre, tm=512):
    s, d = df.shape
    f = w_down.shape[0]
    tm = _tile(s, tm, 16)

    def body(df_ref, w_ref, a_ref, b_ref, o_ref):
        dhm = _nt(df_ref[...], w_ref[...])
        a = a_ref[...].astype(F32)
        o_ref[:, 0:f] = (dhm * b_ref[...].astype(F32) * _silu_grad(a)).astype(BF16)
        o_ref[:, f:2 * f] = (dhm * jax.nn.silu(a)).astype(BF16)

    return pl.pallas_call(
        body, grid=(s // tm,), in_specs=[_row(tm, d), _vec(f, d), _row(tm, f), _row(tm, f)],
        out_specs=_row(tm, 2 * f), out_shape=SDS((s, 2 * f), BF16),
        compiler_params=_params("parallel"), name=name)(df, w_down, a_pre, b_pre)


def _col_pieces(w):
    return w if w.ndim == 3 else w.reshape(w.shape[0], N_CHIPS, -1).transpose(1, 0, 2)


def _ffn_fwd(tag, h, nw, sh, sc, g, w_up, w_down, ex=None):
    u = _norm_mod(tag + "_norm", h, nw, sh, sc)
    a_pre, b_pre, hm = _up_swiglu(tag + "_up", u, _col_pieces(w_up), ex=ex)
    h_new, f = _matmul_resid(tag + "_down", hm, w_down, h, g, 0.5, ex=ex)
    return h_new, (h, u, a_pre, b_pre, hm, f)


def _ffn_bwd(tag, dh, saved, nw, sc, g, w_up, w_down, ex=None):
    h, u, a_pre, b_pre, hm, f = saved
    df, acc_g = _resid_bwd(tag + "_res_bwd", dh, f, g, 0.5)
    gw_down = _matmul(tag + "_gw_down", hm, df, "tn", out_dtype=BF16, tm=1408, ex=ex)
    dab = _down_swiglu_bwd(tag + "_dhm", df, w_down, a_pre, b_pre)
    gw_up = _matmul(tag + "_gw_up", u, dab, "tn", out_dtype=BF16, tn=1408, out_pieces=N_CHIPS, ex=ex)
    du = _matmul(tag + "_du", dab, w_up, "nt", ex=ex)
    dh_in, acc = _norm_mod_bwd(tag + "_norm_bwd", h, du, dh, nw, sc)
    return dh_in, gw_up, _row_pieces(gw_down), (acc[0], acc[1], acc_g[0], acc[2])


def _mixer_fwd(h, nw, sh, sc, g, wt, lay, ex=None):
    nh = lay.nh
    u = _norm_mod("mix_norm", h, nw, sh, sc)
    proj = _matmul("mix_in", u, wt["w_in"], "nn", ex=ex)
    qkvn = _prep_b("mix_prep_b", proj, wt["conv_dn"], lay)
    ya = _prep_a("mix_prep_a", proj, wt["conv_a"], lay)
    bg = _prep_c("mix_prep_c", proj, wt["pvec"], lay)
    gct = bg[:, 4 * nh:6 * nh].T
    o_f, *st_f = _delta_fwd("delta_fwd_l2r", qkvn, bg, gct, nh, False)
    o_b, *st_b = _delta_fwd("delta_fwd_r2l", qkvn, bg, gct, nh, True)
    yb = _post("mix_post", o_f, o_b, proj, wt["dn_norm"], lay)
    pa = _matmul("mix_a_out", ya, wt["w_a_out"], "nn")
    pb = _matmul("mix_b_out", yb, wt["w_b_out"], "nn")
    mg = _merge("mix_merge", pa, pb, proj, lay)
    h2, y = _matmul_resid("mix_out", mg, wt["w_out"], h, g, 1.0, ex=ex)
    return h2, (h, u, proj, qkvn, ya, bg, gct, o_f, o_b, st_f, st_b, yb, pa, pb, mg, y)


def _mixer_bwd(dh, saved, nw, sc, g, wt, lay, ex=None):
    h, u, proj, qkvn, ya, bg, gct, o_f, o_b, st_f, st_b, yb, pa, pb, mg, y = saved
    nh = lay.nh
    dy, acc_g = _resid_bwd("mix_res_bwd", dh, y, g, 1.0)
    gw_out = _matmul("mix_gw_out", mg, dy, "tn", out_dtype=BF16, ex=ex)
    dmg = _matmul("mix_dmg", dy, wt["w_out"], "nt")
    dpa, dpb, dproj = _merge_bwd("mix_merge_bwd", dmg, pa, pb, proj, lay)
    gw_a = _matmul("mix_gw_a", ya, dpa, "tn", out_dtype=BF16, out_pieces=N_CHIPS)
    gw_b = _matmul("mix_gw_b", yb, dpb, "tn", out_dtype=BF16)
    dya = _matmul("mix_dya", dpa, wt["w_a_out"], "nt")
    dyb = _matmul("mix_dyb", dpb, wt["w_b_out"], "nt")
    do, dproj, acc_dn = _post_bwd("mix_post_bwd", dyb, o_f, o_b, proj, wt["dn_norm"], dproj, lay)
    dq, dk, dv, dbg_f = _delta_bwd("delta_bwd_l2r", qkvn, bg, gct, do, *st_f, nh, False, ex=ex)
    dq, dk, dv, dbg_b = _delta_bwd("delta_bwd_r2l", qkvn, bg, gct, do, *st_b, nh, True, add=(dq, dk, dv), ex=ex)
    dproj, acc_ca = _prep_a_bwd("mix_prep_a_bwd", dya, proj, wt["conv_a"], dproj, lay)
    dproj, acc_cd = _prep_b_bwd("mix_prep_b_bwd", dq, dk, dv, proj, wt["conv_dn"], dproj, lay)
    dproj, acc_pc = _prep_c_bwd("mix_prep_c_bwd", dbg_f, dbg_b, proj, wt["pvec"], dproj, lay)
    gw_in = lay.unperm_cols(_matmul("mix_gw_in", u, dproj, "tn", out_dtype=BF16))
    gw_in = gw_in.reshape(gw_in.shape[0], N_CHIPS, -1).transpose(1, 0, 2)
    du = _matmul("mix_du", dproj, wt["w_in"], "nt")
    dh_in, acc = _norm_mod_bwd("mix_norm_bwd", h, du, dh, nw, sc)
    small = dict(conv_a=acc_ca[0:wt["conv_a"].shape[0]], conv_dn=acc_cd[0:wt["conv_dn"].shape[0]],
                 dn_norm=acc_dn[0:1], a_log=acc_pc[0], dt_bias=acc_pc[1])
    big = dict(w_in=gw_in, w_a_out=gw_a, w_b_out=_row_pieces(gw_b), w_out=_row_pieces(gw_out))
    return dh_in, big, small, (acc[0], acc[1], acc_g[0], acc[2])


def _as_operands(gathered, lay):
    wt = {}
    for n, g in gathered.items():
        if n == "w_in":
            wt[n] = lay.perm_cols(jnp.concatenate(list(g), axis=1))
        else:
            wt[n] = g if n in COL_SHARDED else g.reshape(-1, g.shape[-1])
    return wt


def _local_step(x, tgt, modv, wt, lay, ex=None):
    m = [modv[i:i + 1] for i in range(9)]
    h1, sv1 = _ffn_fwd("ffn1", x, wt["norm_ffn1"], m[0], m[1], m[2], wt["w_ffn1_up"], wt["w_ffn1_down"], ex)
    if ex:
        wt = dict(wt, **_as_operands(ex.weights("mixer"), lay))
    h2, sv2 = _mixer_fwd(h1, wt["norm_mix"], m[3], m[4], m[5], wt, lay, ex)
    if ex:
        wt = dict(wt, **_as_operands(ex.weights("ffn2"), lay))
    h3, sv3 = _ffn_fwd("ffn2", h2, wt["norm_ffn2"], m[6], m[7], m[8], wt["w_ffn2_up"], wt["w_ffn2_down"])
    dh3, acc_f = _final_loss("final_loss", h3, tgt, wt["norm_final"])
    loss = jnp.sum(acc_f[1])
    dh2, gu2, gd2, dm3 = _ffn_bwd("ffn2", dh3, sv3, wt["norm_ffn2"], m[7], m[8], wt["w_ffn2_up"], wt["w_ffn2_down"])
    if ex:
        ex.reduce("ffn2", dict(w_ffn2_up=gu2, w_ffn2_down=gd2))
    dh1, gmix, small, dm2 = _mixer_bwd(dh2, sv2, wt["norm_mix"], m[4], m[5], wt, lay, ex)
    if ex:
        ex.reduce("mixer", gmix)
    dx, gu1, gd1, dm1 = _ffn_bwd("ffn1", dh1, sv1, wt["norm_ffn1"], m[1], m[2], wt["w_ffn1_up"], wt["w_ffn1_down"], ex)
    dmod = jnp.stack([dm1[0], dm1[1], dm1[2], dm2[0], dm2[1], dm2[2], dm3[0], dm3[1], dm3[2]])
    big = dict(w_ffn1_up=gu1, w_ffn1_down=gd1, w_ffn2_up=gu2, w_ffn2_down=gd2, **gmix)
    small = dict(small, norm_ffn1=dm1[3], norm_mix=dm2[3], norm_ffn2=dm3[3], norm_final=acc_f[0])
    return loss, dx, dmod, big, small


def _position():
    return lax.axis_index("x"), lax.axis_index("y"), lax.axis_index("c")


_ANY = pl.BlockSpec(memory_space=pl.ANY)
_VMEM = pl.BlockSpec(memory_space=pltpu.VMEM)


def _allgather8(name, v):
    r = v.shape[0]

    def body(v_ref, out_ref, send_sems, recv_sems):
        x, y, c = _position()
        me = 4 * x + 2 * y + c
        out_ref[me] = v_ref[...]
        copies = []
        for mask in range(1, N_DEV):
            peer = tuple(1 - p if mask >> b & 1 else p for p, b in ((x, 2), (y, 1), (c, 0)))
            cp = pltpu.make_async_remote_copy(
                src_ref=v_ref, dst_ref=out_ref.at[me], send_sem=send_sems.at[mask - 1],
                recv_sem=recv_sems.at[mask - 1], device_id=peer, device_id_type=MESH)
            cp.start()
            copies.append(cp)
        for cp in copies:
            cp.wait()

    return pl.pallas_call(
        body, in_specs=[_VMEM], out_specs=_VMEM, out_shape=SDS((N_DEV, r, LANES), F32),
        scratch_shapes=[pltpu.SemaphoreType.DMA((N_DEV - 1,)), pltpu.SemaphoreType.DMA((N_DEV - 1,))],
        name=name)(v)


def _other_chips(x, y):
    return [(1 - x, y), (x, 1 - y), (1 - x, 1 - y)]


def _half_rows(c, rows):
    hr = rows // 2
    assert hr % 16 == 0
    return pl.ds(pl.multiple_of(c * hr, 16), hr)


class _Stage:
    def __init__(self, arrays, out_shapes, sems, plan, in_place=False):
        self.arrays, self.out_shapes, self.sems, self.plan, self.in_place = arrays, out_shapes, sems, plan, in_place

    def start(self, ins, outs, sems):
        for kind, cp in self.plan(ins, outs, sems):
            if kind != "recv":
                cp.start()

    def wait(self, ins, outs, sems):
        for kind, cp in self.plan(ins, outs, sems):
            {"local": cp.wait, "both": cp.wait, "send": cp.wait_send, "recv": cp.wait_recv}[kind]()

    def aliases(self, in_offset, out_offset):
        return {in_offset + i: out_offset + i for i in range(len(self.arrays))} if self.in_place else {}


def _run_stage(name, stage):
    n_in, n_out = len(stage.arrays), len(stage.out_shapes)

    def body(*refs):
        ins, outs, sems = refs[0:n_in], refs[n_in:n_in + n_out], refs[n_in + n_out:]
        stage.start(ins, outs, sems)
        stage.wait(ins, outs, sems)

    return pl.pallas_call(
        body, in_specs=[_ANY] * n_in, out_specs=[_ANY] * n_out, out_shape=stage.out_shapes,
        input_output_aliases=stage.aliases(0, 0), scratch_shapes=stage.sems, name=name)(*stage.arrays)


def _dma_sems(*counts):
    return [pltpu.SemaphoreType.DMA((n,)) for n in counts]


def _gather_send(shards):
    nw = len(shards)

    def plan(ins, outs, sems):
        send_sems, recv_sems, local_sems = sems
        x, y, c = _position()
        p = 2 * x + y
        todo = [("local", pltpu.make_async_copy(ins[w], outs[w].at[p], local_sems.at[w])) for w in range(nw)]
        for j, (cx, cy) in enumerate(_other_chips(x, y)):
            for w in range(nw):
                half = _half_rows(c, ins[w].shape[0])
                sem = dict(send_sem=send_sems.at[j * nw + w], recv_sem=recv_sems.at[j * nw + w], device_id_type=MESH)
                todo.append(("send", pltpu.make_async_remote_copy(
                    src_ref=ins[w].at[half], dst_ref=outs[w].at[p, half], device_id=(cx, cy, c), **sem)))
                landing = outs[w].at[2 * cx + cy, half]
                todo.append(("recv", pltpu.make_async_remote_copy(
                    src_ref=landing, dst_ref=landing, device_id=(x, y, c), **sem)))
        return todo

    return _Stage(shards, [SDS((N_CHIPS,) + v.shape, v.dtype) for v in shards], _dma_sems(3 * nw, 3 * nw, nw), plan)


def _gather_pass(gathered):
    nw = len(gathered)

    def plan(ins, outs, sems):
        send_sems, recv_sems = sems
        x, y, c = _position()
        todo = []
        for j, (cx, cy) in enumerate(_other_chips(x, y)):
            for w in range(nw):
                sem = dict(send_sem=send_sems.at[j * nw + w], recv_sem=recv_sems.at[j * nw + w], device_id_type=MESH)
                mine = outs[w].at[2 * cx + cy, _half_rows(c, outs[w].shape[1])]
                theirs = outs[w].at[2 * cx + cy, _half_rows(1 - c, outs[w].shape[1])]
                todo.append(("send", pltpu.make_async_remote_copy(
                    src_ref=mine, dst_ref=mine, device_id=(x, y, 1 - c), **sem)))
                todo.append(("recv", pltpu.make_async_remote_copy(
                    src_ref=theirs, dst_ref=theirs, device_id=(x, y, c), **sem)))
        return todo

    return _Stage(gathered, [SDS(g.shape, g.dtype) for g in gathered], _dma_sems(3 * nw, 3 * nw), plan, in_place=True)


def _swap_halves(gs):
    nw = len(gs)

    def plan(ins, outs, sems):
        x, y, c = _position()
        return [("both", pltpu.make_async_remote_copy(
            src_ref=ins[w].at[:, _half_rows(1 - c, ins[w].shape[1])], dst_ref=outs[w], send_sem=sems[0].at[w],
            recv_sem=sems[1].at[w], device_id=(x, y, 1 - c), device_id_type=MESH)) for w in range(nw)]

    return _Stage(gs, [SDS((g.shape[0], g.shape[1] // 2, g.shape[2]), g.dtype) for g in gs], _dma_sems(nw, nw), plan)


def _scatter_chips(vs):
    nw = len(vs)

    def plan(ins, outs, sems):
        x, y, c = _position()
        return [("both", pltpu.make_async_remote_copy(
            src_ref=ins[w].at[2 * cx + cy], dst_ref=outs[w].at[j], send_sem=sems[0].at[j * nw + w],
            recv_sem=sems[1].at[j * nw + w], device_id=(cx, cy, c), device_id_type=MESH))
            for j, (cx, cy) in enumerate(_other_chips(x, y)) for w in range(nw)]

    return _Stage(vs, [SDS((N_CHIPS - 1,) + v.shape[1:], v.dtype) for v in vs], _dma_sems(3 * nw, 3 * nw), plan)


def _share_halves(fulls):
    nw = len(fulls)

    def plan(ins, outs, sems):
        x, y, c = _position()
        todo = []
        for w in range(nw):
            rows = outs[w].at[_half_rows(c, outs[w].shape[0])]
            todo.append(("both", pltpu.make_async_remote_copy(
                src_ref=rows, dst_ref=rows, send_sem=sems[0].at[w], recv_sem=sems[1].at[w],
                device_id=(x, y, 1 - c), device_id_type=MESH)))
        return todo

    return _Stage(fulls, [SDS(f.shape, f.dtype) for f in fulls], _dma_sems(nw, nw), plan, in_place=True)


class _Hosted:
    def __init__(self, ex, name, grid):
        self.ex, self.name, self.grid = ex, name, grid
        self.stage = ex.host(name) if ex is not None else None
        st = self.stage
        self.arrays = list(st.arrays) if st else []
        self.out_shapes = list(st.out_shapes) if st else []
        self.sems = list(st.sems) if st else []
        self.in_specs, self.out_specs = [_ANY] * len(self.arrays), [_ANY] * len(self.out_shapes)

    def aliases(self, in_offset, out_offset):
        return self.stage.aliases(in_offset, out_offset) if self.stage else {}

    def split(self, rest, n_out):
        ni, no, ns = len(self.arrays), len(self.out_shapes), len(self.sems)
        self.ins, self.outs = rest[0:ni], rest[ni + n_out:ni + n_out + no]
        tail = rest[ni + n_out + no:]
        self.sem_refs = tail[len(tail) - ns:]
        return rest[ni:ni + n_out], tail[0:len(tail) - ns]

    def _at(self, last):
        conds = [pl.program_id(d) == (g - 1 if last else 0) for d, g in enumerate(self.grid)]
        return functools.reduce(lambda p, q: p & q, conds)

    def start(self):
        if self.stage:
            pl.when(self._at(False))(lambda: self.stage.start(self.ins, self.outs, self.sem_refs))

    def wait(self):
        if self.stage:
            pl.when(self._at(True))(lambda: self.stage.wait(self.ins, self.outs, self.sem_refs))

    def finish(self, out, n_out):
        out = list(out)
        if self.stage:
            self.ex.done(self.name, out[n_out:])
        return out[0:n_out]


GROUPS = {"ffn1": ("w_ffn1_up", "w_ffn1_down"), "mixer": ("w_in", "w_a_out", "w_b_out", "w_out"),
          "ffn2": ("w_ffn2_up", "w_ffn2_down")}
HOSTS = {"ffn1_up": ("gather_send", "mixer"), "ffn1_down": ("gather_pass", "mixer"),
         "mix_in": ("gather_send", "ffn2"), "mix_out": ("gather_pass", "ffn2"),
         "mix_gw_out": ("swap", "ffn2"), "delta_bwd_l2r": ("scatter", "ffn2"), "delta_bwd_r2l": ("share", "ffn2"),
         "ffn1_gw_down": ("swap", "mixer"), "ffn1_gw_up": ("scatter", "mixer"), "ffn1_du": ("share", "mixer")}


class _Exchange:
    def __init__(self, shards):
        self.shards = shards
        self.gathered, self.red = {}, {}

    def _stage(self, kind, group):
        if kind == "gather_send":
            return _gather_send([self.shards[n] for n in GROUPS[group]])
        if kind == "gather_pass":
            return _gather_pass(self.gathered[group])
        st = self.red[group]
        return {"swap": lambda: _swap_halves(st["parts"]), "scatter": lambda: _scatter_chips(st["chip_sums"]),
                "share": lambda: _share_halves(st["fulls"])}[kind]()

    def _done(self, kind, group, outs):
        names = GROUPS[group]
        if kind in ("gather_send", "gather_pass"):
            self.gathered[group] = list(outs)
            return
        st = self.red[group]
        if kind == "swap":
            st["from_sib"] = list(outs)
            st["chip_sums"] = [_chip_sum("chip_sum_" + n, g, f) for n, g, f in zip(names, st["parts"], outs)]
        elif kind == "scatter":
            st["fulls"] = [_total("total_" + n, g, f, r)
                           for n, g, f, r in zip(names, st["parts"], st["from_sib"], outs)]
        else:
            st["grads"] = dict(zip(names, outs))

    def host(self, kernel_name):
        return self._stage(*HOSTS[kernel_name]) if kernel_name in HOSTS else None

    def done(self, kernel_name, outs):
        self._done(*HOSTS[kernel_name], outs)

    def run(self, kind, group):
        self._done(kind, group, _run_stage(f"{kind}_{group}", self._stage(kind, group)))

    def weights(self, group):
        return dict(zip(GROUPS[group], self.gathered[group]))

    def reduce(self, group, parts):
        self.red[group] = dict(parts=[parts[n] for n in GROUPS[group]])

    def grads(self, group):
        return self.red[group]["grads"]


def _chip_sum(name, g, from_sib):
    _, r, cdim = g.shape
    hr = r // 2

    def body(g_ref, s_ref, o_ref):
        o_ref[...] = (g_ref[...].astype(F32) + s_ref[...].astype(F32)).astype(o_ref.dtype)

    blk = (None, hr, cdim)
    return pl.pallas_call(
        body, grid=(N_CHIPS,),
        in_specs=[pl.BlockSpec(blk, lambda j: (j, lax.axis_index("c"), 0)), pl.BlockSpec(blk, lambda j: (j, 0, 0))],
        out_specs=pl.BlockSpec(blk, lambda j: (j, 0, 0)),
        out_shape=SDS((N_CHIPS, hr, cdim), g.dtype), compiler_params=_params("parallel"), name=name)(g, from_sib)


def _total(name, g, from_sib, from_chips):
    _, r, cdim = g.shape
    hr = r // 2
    tr = _tile(hr, 256, 16)
    nt = hr // tr

    def body(g_ref, s_ref, rc_ref, o_ref):
        acc = g_ref[...].astype(F32) + s_ref[...].astype(F32)
        for j in range(N_CHIPS - 1):
            acc = acc + rc_ref[j].astype(F32)
        o_ref[...] = acc

    def chip():
        return 2 * lax.axis_index("x") + lax.axis_index("y")

    blk = (None, tr, cdim)
    return pl.pallas_call(
        body, grid=(nt,),
        in_specs=[pl.BlockSpec(blk, lambda i: (chip(), lax.axis_index("c") * nt + i, 0)),
                  pl.BlockSpec(blk, lambda i: (chip(), i, 0)),
                  pl.BlockSpec((N_CHIPS - 1, tr, cdim), lambda i: (0, i, 0))],
        out_specs=pl.BlockSpec((tr, cdim), lambda i: (lax.axis_index("c") * nt + i, 0)),
        out_shape=SDS((r, cdim), F32), compiler_params=_params("parallel"), name=name)(g, from_sib, from_chips)


def _sum8(name, v):
    _, r, w = v.shape

    def body(v_ref, o_ref):
        acc = v_ref[0]
        for j in range(1, N_DEV):
            acc = acc + v_ref[j]
        o_ref[...] = acc

    return pl.pallas_call(body, in_specs=[_VMEM], out_specs=_VMEM, out_shape=SDS((r, w), F32), name=name)(v)


def _adamw(name, w, g, m, v, tr=256):
    r, cdim = w.shape
    tr = _tile(r, tr, 8)
    bc1, bc2 = 1.0 - ADAM_B1 ** ADAM_STEP, 1.0 - ADAM_B2 ** ADAM_STEP

    def body(w_ref, g_ref, m_ref, v_ref, d_ref, nm_ref, nv_ref):
        g = g_ref[...]
        m2 = ADAM_B1 * m_ref[...] + (1.0 - ADAM_B1) * g
        v2 = ADAM_B2 * v_ref[...] + (1.0 - ADAM_B2) * (g * g)
        d_ref[...] = -ADAM_LR * ((m2 / bc1) / (jnp.sqrt(v2 / bc2) + ADAM_EPS) + ADAM_WD * w_ref[...])
        nm_ref[...] = m2
        nv_ref[...] = v2

    spec = _row(tr, cdim)
    out = SDS((r, cdim), F32)
    return pl.pallas_call(body, grid=(r // tr,), in_specs=[spec] * 4, out_specs=[spec] * 3, out_shape=[out] * 3,
                          compiler_params=_params("parallel"), name=name)(w, g, m, v)


def _pack_rows(arrays, width, row_mult, dtype):
    parts, spans, row = [], [], 0
    for a in arrays:
        n = a.size
        rows = -(-n // width)
        flat = a.reshape(-1).astype(dtype)
        if rows * width != n:
            flat = jnp.concatenate([flat, jnp.zeros((rows * width - n,), dtype)])
        parts.append(flat.reshape(rows, width))
        spans.append((row, rows, n, a.shape))
        row += rows
    pad = -row % row_mult
    if pad:
        parts.append(jnp.zeros((pad, width), dtype))
    return jnp.concatenate(parts, axis=0), spans


def _unpack_rows(packed, spans):
    return [packed[r0:r0 + rows].reshape(-1)[0:n].reshape(shape) for r0, rows, n, shape in spans]


BIG = ("w_ffn1_up", "w_ffn1_down", "w_in", "w_a_out", "w_b_out", "w_out", "w_ffn2_up", "w_ffn2_down")
COL_SHARDED = ("w_ffn1_up", "w_in", "w_a_out", "w_ffn2_up")
SMALL = ("b_ada", "norm_ffn1", "norm_mix", "conv_a", "conv_dn", "a_log_fwd", "dt_bias_fwd", "a_log_bwd",
         "dt_bias_bwd", "dn_norm", "norm_ffn2", "norm_final")
WEIGHTS = ("w_ada", "b_ada", "norm_ffn1", "w_ffn1_up", "w_ffn1_down", "norm_mix", "w_in", "conv_a", "conv_dn",
           "a_log_fwd", "dt_bias_fwd", "a_log_bwd", "dt_bias_bwd", "dn_norm", "w_a_out", "w_b_out", "w_out",
           "norm_ffn2", "w_ffn2_up", "w_ffn2_down", "norm_final")


def kernel(x, c, w_ada, b_ada, norm_ffn1, w_ffn1_up, w_ffn1_down, norm_mix, w_in, conv_a, conv_dn, a_log_fwd, dt_bias_fwd, a_log_bwd, dt_bias_bwd, dn_norm, w_a_out, w_b_out, w_out, norm_ffn2, w_ffn2_up, w_ffn2_down, norm_final, loss_target, m_w_ada, m_b_ada, m_norm_ffn1, m_w_ffn1_up, m_w_ffn1_down, m_norm_mix, m_w_in, m_conv_a, m_conv_dn, m_a_log_fwd, m_dt_bias_fwd, m_a_log_bwd, m_dt_bias_bwd, m_dn_norm, m_w_a_out, m_w_b_out, m_w_out, m_norm_ffn2, m_w_ffn2_up, m_w_ffn2_down, m_norm_final, v_w_ada, v_b_ada, v_norm_ffn1, v_w_ffn1_up, v_w_ffn1_down, v_norm_mix, v_w_in, v_conv_a, v_conv_dn, v_a_log_fwd, v_dt_bias_fwd, v_a_log_bwd, v_dt_bias_bwd, v_dn_norm, v_w_a_out, v_w_b_out, v_w_out, v_norm_ffn2, v_w_ffn2_up, v_w_ffn2_down, v_norm_final):
    given = dict(locals())
    wsh = {n: given[n] for n in WEIGHTS}
    msh = {n: given["m_" + n] for n in WEIGHTS}
    vsh = {n: given["v_" + n] for n in WEIGHTS}
    d = x.shape[-1]
    ca = conv_a.shape[-1] * N_CHIPS
    nh = conv_dn.shape[-1] * N_CHIPS // (3 * HEAD)
    lay = _Layout(d, ca, nh)
    xi, yi, ci = _position()
    chip = 2 * xi + yi
    me = 2 * chip + ci

    c_act = jax.nn.silu(c)
    g1, g1_spans = _pack_rows([c_act, conv_a[0], conv_dn[0]], LANES, 8, F32)
    g1_all = _allgather8("gather_cond", g1)
    per_dev = [_unpack_rows(g1_all[k], g1_spans) for k in range(N_DEV)]
    c_all = jnp.concatenate([p[0] for p in per_dev], axis=0)
    conv_a_full = jnp.concatenate([per_dev[2 * k][1] for k in range(N_CHIPS)], axis=1)
    conv_dn_full = jnp.concatenate([per_dev[2 * k][2] for k in range(N_CHIPS)], axis=1)

    mod_sh = _matmul("ada_mod", c_all, w_ada[0], "nn")
    b_sh = lax.dynamic_slice_in_dim(b_ada, chip * mod_sh.shape[1], mod_sh.shape[1], axis=1)
    g2, g2_spans = _pack_rows([mod_sh + b_sh], LANES, 8, F32)
    g2_all = _allgather8("gather_mod", g2)
    mod_all = jnp.concatenate([_unpack_rows(g2_all[2 * k], g2_spans)[0] for k in range(N_CHIPS)], axis=1)
    modv = lax.dynamic_index_in_dim(mod_all, me, 0, keepdims=False).reshape(9, d)

    ex = _Exchange({n: wsh[n][0].astype(BF16) for n in BIG})
    ex.run("gather_send", "ffn1")
    ex.run("gather_pass", "ffn1")
    wt = _as_operands(ex.weights("ffn1"), lay)
    lane_pad = (jnp.zeros((2 * nh,), F32), jnp.zeros((LANES - 4 * nh,), F32))
    pvec = jnp.stack([jnp.concatenate([lane_pad[0], a_log_fwd[0], a_log_bwd[0], lane_pad[1]]),
                      jnp.concatenate([lane_pad[0], dt_bias_fwd[0], dt_bias_bwd[0], lane_pad[1]])]
                     + [jnp.zeros((LANES,), F32)] * 6)
    wt.update(conv_a=conv_a_full, conv_dn=conv_dn_full, pvec=pvec, dn_norm=dn_norm, norm_ffn1=norm_ffn1,
              norm_mix=norm_mix, norm_ffn2=norm_ffn2, norm_final=norm_final.reshape(1, d))

    loss, dx, dmod, big, small = _local_step(x[0], loss_target[0], modv, wt, lay, ex)
    loss = lax.psum(loss, ("x", "y", "c"))

    small_list = [dmod.reshape(1, 9 * d), small["norm_ffn1"], small["norm_mix"], small["conv_a"], small["conv_dn"],
                  small["a_log"][2 * nh:3 * nh], small["dt_bias"][2 * nh:3 * nh], small["a_log"][3 * nh:4 * nh],
                  small["dt_bias"][3 * nh:4 * nh], small["dn_norm"], small["norm_ffn2"], small["norm_final"]]
    g3, g3_spans = _pack_rows(small_list, LANES, 8, F32)
    g3_all = _allgather8("gather_small_grads", g3)
    g_small = dict(zip(SMALL, _unpack_rows(_sum8("sum_small_grads", g3_all), g3_spans)))
    dmod_all = jnp.concatenate([_unpack_rows(g3_all[k], g3_spans)[0] for k in range(N_DEV)], axis=0)
    ncol = w_ada.shape[-1]
    dmod_sh = lax.dynamic_slice_in_dim(dmod_all, chip * ncol, ncol, axis=1)
    grads = {"w_ada": _matmul("ada_grad", c_all, dmod_sh, "tn")[None]}
    for n in SMALL:
        g = g_small[n]
        if n in ("conv_a", "conv_dn"):
            wloc = wsh[n].shape[-1]
            g = lax.dynamic_slice_in_dim(g, chip * wloc, wloc, axis=1)
        grads[n] = g.reshape(wsh[n].shape)

    ex.reduce("ffn1", big)
    for kind in ("swap", "scatter", "share"):
        ex.run(kind, "ffn1")
    for group in GROUPS:
        for n, g in ex.grads(group).items():
            grads[n] = g[None]

    delta, new_m, new_v = {}, {}, {}
    for n in ("w_ada",) + BIG:
        shp = wsh[n].shape
        outs = _adamw("adamw_" + n, *(t.reshape(shp[-2], shp[-1]) for t in (wsh[n], grads[n], msh[n], vsh[n])))
        delta[n], new_m[n], new_v[n] = (o.reshape(shp) for o in outs)
    packed = []
    for src in (wsh, grads, msh, vsh):
        pk, s_spans = _pack_rows([src[n] for n in SMALL], LANES, 8, F32)
        packed.append(pk)
    outs = _adamw("adamw_small", *packed)
    for dst, o in zip((delta, new_m, new_v), outs):
        dst.update(zip(SMALL, _unpack_rows(o, s_spans)))

    return (loss, dx[None], *[grads[n] for n in WEIGHTS], *[delta[n] for n in WEIGHTS],
            *[new_m[n] for n in WEIGHTS], *[new_v[n] for n in WEIGHTS])
```

```python
import functools

import jax
import jax.numpy as jnp
from jax import lax
from jax.experimental import pallas as pl
from jax.experimental.pallas import tpu as pltpu

F32 = jnp.float32
BF16 = jnp.bfloat16
SDS = jax.ShapeDtypeStruct
MESH = pl.DeviceIdType.MESH
HI = lax.Precision.HIGHEST

EPS = 1e-6
HEAD = 128
CHUNK = 64
LANES = 128
N_CHIPS = 4
N_DEV = 8
VMEM_LIMIT = 56 * 1024 * 1024

ADAM_LR = 0.001
ADAM_B1 = 0.9
ADAM_B2 = 0.999
ADAM_EPS = 1e-08
ADAM_WD = 0.01
ADAM_STEP = 10


def _params(*sem):
    return pltpu.CompilerParams(dimension_semantics=sem, vmem_limit_bytes=VMEM_LIMIT)


def _tile(n, cap, mult=LANES):
    t = min(n, cap) // mult * mult
    while t >= mult:
        if n % t == 0:
            return t
        t -= mult
    return n


def _row(tr, w, cb=0):
    return pl.BlockSpec((tr, w), lambda i: (i, cb))


def _vec(r, w):
    return pl.BlockSpec((r, w), lambda i: (0, 0))


def _nn(a, b, **kw):
    return jnp.dot(a, b, preferred_element_type=F32, **kw)


def _nt(a, b, **kw):
    return lax.dot_general(a, b, (((1,), (1,)), ((), ())), preferred_element_type=F32, **kw)


def _tn(a, b, **kw):
    return lax.dot_general(a, b, (((0,), (0,)), ((), ())), preferred_element_type=F32, **kw)


def _bnn(a, b):
    return lax.dot_general(a, b, (((2,), (1,)), ((0,), (0,))), preferred_element_type=F32)


def _bnt(a, b):
    return lax.dot_general(a, b, (((2,), (2,)), ((0,), (0,))), preferred_element_type=F32)


def _btn(a, b):
    return lax.dot_general(a, b, (((1,), (1,)), ((0,), (0,))), preferred_element_type=F32)


def _silu_grad(x):
    s = jax.nn.sigmoid(x)
    return s * (1.0 + x * (1.0 - s))


def _matmul(name, a, b, mode, out_dtype=F32, tm=1024, tn=1024, tk=2048, full_k=2816, out_pieces=0, ex=None):
    pieces_b = b.shape[0] if b.ndim == 3 else 0
    b2 = b.shape[1:] if pieces_b else b.shape
    if mode == "nn":
        (m, k), n = a.shape, b2[1] * max(pieces_b, 1)
    elif mode == "nt":
        (m, _), n, k = a.shape, b2[0], b2[1] * max(pieces_b, 1)
    else:
        (k, m), n = a.shape, b2[1] * max(pieces_b, 1)
    n_unit = n // max(out_pieces, 1) if mode == "nt" or not pieces_b else n // pieces_b
    if out_pieces and pieces_b and mode != "nt":
        assert out_pieces == pieces_b
    k_unit = k // pieces_b if (pieces_b and mode == "nt") else k
    tm, tn = _tile(m, tm), _tile(n_unit, tn)
    tk = k_unit if k_unit <= full_k else _tile(k_unit, tk)
    nk = k // tk
    n_per, k_per = n_unit // tn, k_unit // tk
    a_bytes, b_bytes = a.size * a.dtype.itemsize, b.size * b.dtype.itemsize
    j_outer = nk == 1 and b_bytes + a_bytes * (n // tn) < a_bytes + b_bytes * (m // tm)
    ij = (lambda g0, g1: (g1, g0)) if j_outer else (lambda g0, g1: (g0, g1))

    def spec(shape, pick):
        return pl.BlockSpec(shape, lambda g0, g1, l: pick(*ij(g0, g1), l))

    a_spec = {"nn": spec((tm, tk), lambda i, j, l: (i, l)), "nt": spec((tm, tk), lambda i, j, l: (i, l)),
              "tn": spec((tk, tm), lambda i, j, l: (l, i))}[mode]
    if not pieces_b:
        b_spec = {"nn": spec((tk, tn), lambda i, j, l: (l, j)), "nt": spec((tn, tk), lambda i, j, l: (j, l)),
                  "tn": spec((tk, tn), lambda i, j, l: (l, j))}[mode]
    elif mode == "nt":
        b_spec = spec((None, tn, tk), lambda i, j, l: (l // k_per, j, l % k_per))
    else:
        b_spec = spec((None, tk, tn), lambda i, j, l: (j // n_per, l, j % n_per))
    if out_pieces:
        o_spec = spec((None, tm, tn), lambda i, j, l: (j // n_per, i, j % n_per))
        o_shape = SDS((out_pieces, m, n // out_pieces), out_dtype)
    else:
        o_spec, o_shape = spec((tm, tn), lambda i, j, l: (i, j)), SDS((m, n), out_dtype)
    dot = {"nn": _nn, "nt": _nt, "tn": _tn}[mode]
    grid = (n // tn, m // tm, nk) if j_outer else (m // tm, n // tn, nk)
    host = _Hosted(ex, name, grid)

    def body(a_ref, b_ref, *rest):
        (o_ref,), scratch = host.split(rest, 1)
        host.start()
        part = dot(a_ref[...].astype(BF16), b_ref[...].astype(BF16))
        if nk == 1:
            o_ref[...] = part.astype(o_ref.dtype)
        else:
            l, acc = pl.program_id(2), scratch[0]

            @pl.when(l == 0)
            def _():
                acc[...] = part

            @pl.when((l > 0) & (l < nk - 1))
            def _():
                acc[...] += part

            @pl.when(l == nk - 1)
            def _():
                o_ref[...] = (acc[...] + part).astype(o_ref.dtype)
        host.wait()

    out = pl.pallas_call(
        body, grid=grid, in_specs=[a_spec, b_spec] + host.in_specs, out_specs=[o_spec] + host.out_specs,
        out_shape=[o_shape] + host.out_shapes, input_output_aliases=host.aliases(2, 1),
        scratch_shapes=([] if nk == 1 else [pltpu.VMEM((tm, tn), F32)]) + host.sems,
        compiler_params=_params(*(("arbitrary",) * 3 if host.stage else ("parallel", "parallel", "arbitrary"))),
        name=name)(a, b, *host.arrays)
    return host.finish(out, 1)[0]


def _norm_mod(name, h, nw, sh, sc, tr=512):
    s, d = h.shape
    tr = _tile(s, tr, 8)

    def body(h_ref, nw_ref, sh_ref, sc_ref, u_ref):
        x = h_ref[...]
        r = lax.rsqrt(jnp.mean(x * x, axis=-1, keepdims=True) + EPS)
        u_ref[...] = (x * r * nw_ref[...] * (1.0 + sc_ref[...]) + sh_ref[...]).astype(BF16)

    return pl.pallas_call(
        body, grid=(s // tr,), in_specs=[_row(tr, d), _vec(1, d), _vec(1, d), _vec(1, d)],
        out_specs=_row(tr, d), out_shape=SDS((s, d), BF16),
        compiler_params=_params("parallel"), name=name)(h, nw, sh, sc)


def _norm_mod_bwd(name, h, du, dh, nw, sc, tr=512):
    s, d = h.shape
    tr = _tile(s, tr, 8)

    def body(h_ref, du_ref, dh_ref, nw_ref, sc_ref, o_ref, acc_ref):
        @pl.when(pl.program_id(0) == 0)
        def _():
            acc_ref[...] = jnp.zeros_like(acc_ref)

        x, g = h_ref[...], du_ref[...]
        r = lax.rsqrt(jnp.mean(x * x, axis=-1, keepdims=True) + EPS)
        n = x * r
        nw, sc1 = nw_ref[...], 1.0 + sc_ref[...]
        dn = g * sc1 * nw
        o_ref[...] = dh_ref[...] + r * (dn - n * jnp.mean(dn * n, axis=-1, keepdims=True))
        gn = g * n
        acc_ref[0:1, :] += jnp.sum(g, axis=0, keepdims=True)
        acc_ref[1:2, :] += jnp.sum(gn * nw, axis=0, keepdims=True)
        acc_ref[2:3, :] += jnp.sum(gn * sc1, axis=0, keepdims=True)

    return pl.pallas_call(
        body, grid=(s // tr,),
        in_specs=[_row(tr, d), _row(tr, d), _row(tr, d), _vec(1, d), _vec(1, d)],
        out_specs=[_row(tr, d), _vec(8, d)], out_shape=[SDS((s, d), F32), SDS((8, d), F32)],
        compiler_params=_params("arbitrary"), name=name)(h, du, dh, nw, sc)


def _resid_bwd(name, dh, f, g, scale, tr=512):
    s, d = dh.shape
    tr = _tile(s, tr, 8)

    def body(dh_ref, f_ref, g_ref, o_ref, acc_ref):
        @pl.when(pl.program_id(0) == 0)
        def _():
            acc_ref[...] = jnp.zeros_like(acc_ref)

        x = dh_ref[...]
        o_ref[...] = ((scale * g_ref[...]) * x).astype(BF16)
        acc_ref[0:1, :] += jnp.sum(scale * x * f_ref[...], axis=0, keepdims=True)

    return pl.pallas_call(
        body, grid=(s // tr,), in_specs=[_row(tr, d), _row(tr, d), _vec(1, d)],
        out_specs=[_row(tr, d), _vec(8, d)], out_shape=[SDS((s, d), BF16), SDS((8, d), F32)],
        compiler_params=_params("arbitrary"), name=name)(dh, f, g)


def _final_loss(name, h, tgt, nw, tr=512):
    s, d = h.shape
    tr = _tile(s, tr, 8)

    def body(h_ref, t_ref, nw_ref, o_ref, acc_ref):
        @pl.when(pl.program_id(0) == 0)
        def _():
            acc_ref[...] = jnp.zeros_like(acc_ref)

        x, nw = h_ref[...], nw_ref[...]
        r = lax.rsqrt(jnp.mean(x * x, axis=-1, keepdims=True) + EPS)
        n = x * r
        diff = n * nw - t_ref[...]
        dy = diff * (1.0 / d)
        dn = dy * nw
        o_ref[...] = r * (dn - n * jnp.mean(dn * n, axis=-1, keepdims=True))
        acc_ref[0:1, :] += jnp.sum(dy * n, axis=0, keepdims=True)
        acc_ref[1:2, :] += jnp.sum(diff * diff, axis=0, keepdims=True) * (0.5 / d)

    return pl.pallas_call(
        body, grid=(s // tr,), in_specs=[_row(tr, d), _row(tr, d), _vec(1, d)],
        out_specs=[_row(tr, d), _vec(8, d)], out_shape=[SDS((s, d), F32), SDS((8, d), F32)],
        compiler_params=_params("arbitrary"), name=name)(h, tgt, nw)


class _Layout:
    def __init__(self, d, ca, nh):
        self.d, self.ca, self.nh = d, ca, nh
        self.qk = nh * HEAD
        self.qkv = 3 * self.qk
        self.z = self.qkv
        self.ga = self.z + self.qk
        self.cab = self.ga + 2 * d
        self.ba = self.cab + 3 * ca
        self.tail = _tile(self.ba, 512)
        self.total = self.ba + self.tail
        assert self.qkv % self.qk == 0 and self.ga % (2 * d) == 0 and self.cab % (3 * ca) == 0
        assert self.ba % self.tail == 0 and 4 * nh <= LANES

    def perm_cols(self, w):
        ca, qkv, qk, d, nh = self.ca, self.qkv, self.qk, self.d, self.nh
        o = [0, ca, 2 * ca, 3 * ca, 3 * ca + qkv, 3 * ca + qkv + qk, 3 * ca + qkv + qk + 4 * nh]
        cb, cc, cv = (w[..., o[i]:o[i + 1]] for i in range(3))
        x_qkv, x_z, x_ba = w[..., o[3]:o[4]], w[..., o[4]:o[5]], w[..., o[5]:o[6]]
        gates = w[..., o[6]:o[6] + 2 * d]
        pad = jnp.zeros(w.shape[:-1] + (self.tail - 4 * nh,), w.dtype)
        return jnp.concatenate([x_qkv, x_z, gates, cb, cc, cv, x_ba, pad], axis=-1)

    def unperm_cols(self, w):
        ca, nh = self.ca, self.nh
        cb, cc, cv = (w[..., self.cab + i * ca:self.cab + (i + 1) * ca] for i in range(3))
        return jnp.concatenate([cb, cc, cv, w[..., 0:self.qkv], w[..., self.z:self.ga],
                                w[..., self.ba:self.ba + 4 * nh], w[..., self.ga:self.cab]], axis=-1)


def _halo_specs(tr, w, cb, s):
    nb8 = s // 8
    return [pl.BlockSpec((8, w), lambda i: (jnp.maximum(i * (tr // 8) - 1, 0), cb)),
            pl.BlockSpec((tr, w), lambda i: (i, cb)),
            pl.BlockSpec((8, w), lambda i: (jnp.minimum((i + 1) * (tr // 8), nb8 - 1), cb))]


def _ext(prev_ref, main_ref, next_ref, i, nt):
    p = jnp.where(i > 0, prev_ref[...].astype(F32), 0.0)
    n = jnp.where(i < nt - 1, next_ref[...].astype(F32), 0.0)
    return jnp.concatenate([p, main_ref[...].astype(F32), n], axis=0)


def _shift(x, k):
    return x if k == 0 else pltpu.roll(x, (-k) % x.shape[0], 0)


def _conv_taps(x_ext, w, tr):
    kt = w.shape[0]
    acc = None
    for t in range(kt):
        term = _shift(x_ext, t - kt // 2)[8:8 + tr] * w[t:t + 1, :]
        acc = term if acc is None else acc + term
    return acc


def _prep_a(name, proj, conv_a, lay, tr=256):
    s, ca = proj.shape[0], lay.ca
    tr = _tile(s, tr, 8)
    nt, w = s // tr, 3 * ca

    def body(p_ref, m_ref, n_ref, w_ref, o_ref):
        x = _ext(p_ref, m_ref, n_ref, pl.program_id(0), nt)
        xv = x[:, ca:2 * ca] * x[:, 2 * ca:w]
        y = _conv_taps(xv, w_ref[...], tr)
        o_ref[...] = (m_ref[:, 0:ca] * y).astype(BF16)

    return pl.pallas_call(
        body, grid=(nt,), in_specs=_halo_specs(tr, w, lay.cab // w, s) + [_vec(conv_a.shape[0], ca)],
        out_specs=_row(tr, ca), out_shape=SDS((s, ca), BF16),
        compiler_params=_params("parallel"), name=name)(proj, proj, proj, conv_a)


def _prep_a_bwd(name, dya, proj, conv_a, dproj, lay, tr=256):
    s, ca = proj.shape[0], lay.ca
    tr = _tile(s, tr, 8)
    nt, w, kt = s // tr, 3 * ca, conv_a.shape[0]

    def body(p_ref, m_ref, n_ref, dp_ref, dm_ref, dn_ref, w_ref, _, o_ref, acc_ref):
        i = pl.program_id(0)

        @pl.when(i == 0)
        def _():
            acc_ref[...] = jnp.zeros_like(acc_ref)

        x = _ext(p_ref, m_ref, n_ref, i, nt)
        d_ext = _ext(dp_ref, dm_ref, dn_ref, i, nt)
        cb, cc, cv = x[:, 0:ca], x[:, ca:2 * ca], x[:, 2 * ca:w]
        xv = cc * cv
        wv = w_ref[...]
        dy_ext = d_ext * cb
        dx = None
        for t in range(kt):
            term = _shift(dy_ext, kt // 2 - t)[8:8 + tr] * wv[t:t + 1, :]
            dx = term if dx is None else dx + term
            acc_ref[t:t + 1, :] += jnp.sum(dy_ext[8:8 + tr] * _shift(xv, t - kt // 2)[8:8 + tr],
                                           axis=0, keepdims=True)
        y = _conv_taps(xv, wv, tr)
        o_ref[:, 0:ca] = (dm_ref[...] * y).astype(BF16)
        o_ref[:, ca:2 * ca] = (dx * cv[8:8 + tr]).astype(BF16)
        o_ref[:, 2 * ca:w] = (dx * cc[8:8 + tr]).astype(BF16)

    return pl.pallas_call(
        body, grid=(nt,),
        in_specs=_halo_specs(tr, w, lay.cab // w, s) + _halo_specs(tr, ca, 0, s)
        + [_vec(kt, ca), pl.BlockSpec(memory_space=pl.ANY)],
        out_specs=[_row(tr, w, lay.cab // w), _vec(8, ca)],
        out_shape=[SDS(dproj.shape, dproj.dtype), SDS((8, ca), F32)], input_output_aliases={7: 0},
        compiler_params=_params("arbitrary"), name=name)(proj, proj, proj, dya, dya, dya, conv_a, dproj)


def _qkv_act(c, nh, tr_rows):
    sact = jax.nn.silu(c)
    outs, inv = [], []
    for hd in range(3 * nh):
        sl = sact[:, hd * HEAD:(hd + 1) * HEAD]
        if hd < 2 * nh:
            r = lax.rsqrt(jnp.sum(sl * sl, axis=-1, keepdims=True) + EPS)
            inv.append(r)
            outs.append(sl * (r * (HEAD ** -0.5 if hd < nh else 1.0)))
        else:
            outs.append(sl)
    return jnp.concatenate(outs, axis=-1), sact, inv


def _prep_b(name, proj, conv_dn, lay, tr=256):
    s, w, nh = proj.shape[0], lay.qkv, lay.nh
    tr = _tile(s, tr, 8)
    nt = s // tr

    def body(p_ref, m_ref, n_ref, w_ref, o_ref):
        x = _ext(p_ref, m_ref, n_ref, pl.program_id(0), nt)
        c = _conv_taps(x, w_ref[...], tr)
        o_ref[...] = _qkv_act(c, nh, tr)[0]

    return pl.pallas_call(
        body, grid=(nt,), in_specs=_halo_specs(tr, w, 0, s) + [_vec(conv_dn.shape[0], w)],
        out_specs=_row(tr, w), out_shape=SDS((s, w), F32),
        compiler_params=_params("parallel"), name=name)(proj, proj, proj, conv_dn)


def _prep_b_bwd(name, dq, dk, dv, proj, conv_dn, dproj, lay, tr=256):
    s, w, nh, qk = proj.shape[0], lay.qkv, lay.nh, lay.qk
    tr = _tile(s, tr, 8)
    nt, kt = s // tr, conv_dn.shape[0]

    def body(*refs):
        x_refs, g_refs = refs[0:3], refs[3:12]
        w_ref, o_ref, acc_ref = refs[12], refs[14], refs[15]
        i = pl.program_id(0)

        @pl.when(i == 0)
        def _():
            acc_ref[...] = jnp.zeros_like(acc_ref)

        x = _ext(*x_refs, i, nt)
        wv = w_ref[...]
        c = None
        for t in range(kt):
            term = _shift(x, t - kt // 2) * wv[t:t + 1, :]
            c = term if c is None else c + term
        sig = jax.nn.sigmoid(c)
        sact = c * sig
        ds = []
        for part in range(3):
            g = _ext(*g_refs[3 * part:3 * part + 3], i, nt)
            for hd in range(nh):
                sl = sact[:, part * qk + hd * HEAD:part * qk + (hd + 1) * HEAD]
                gh = g[:, hd * HEAD:(hd + 1) * HEAD]
                if part < 2:
                    r = lax.rsqrt(jnp.sum(sl * sl, axis=-1, keepdims=True) + EPS)
                    sc = HEAD ** -0.5 if part == 0 else 1.0
                    ds.append(sc * r * (gh - sl * (r * r) * jnp.sum(gh * sl, axis=-1, keepdims=True)))
                else:
                    ds.append(gh)
        dc = jnp.concatenate(ds, axis=-1) * (sig * (1.0 + c * (1.0 - sig)))
        dx = None
        for t in range(kt):
            term = _shift(dc, kt // 2 - t)[8:8 + tr] * wv[t:t + 1, :]
            dx = term if dx is None else dx + term
            acc_ref[t:t + 1, :] += jnp.sum(dc[8:8 + tr] * _shift(x, t - kt // 2)[8:8 + tr],
                                           axis=0, keepdims=True)
        o_ref[...] = dx.astype(BF16)

    return pl.pallas_call(
        body, grid=(nt,),
        in_specs=_halo_specs(tr, w, 0, s) + _halo_specs(tr, qk, 0, s) * 3
        + [_vec(kt, w), pl.BlockSpec(memory_space=pl.ANY)],
        out_specs=[_row(tr, w, 0), _vec(8, w)],
        out_shape=[SDS(dproj.shape, dproj.dtype), SDS((8, w), F32)], input_output_aliases={13: 0},
        compiler_params=_params("arbitrary"), name=name)(
            proj, proj, proj, dq, dq, dq, dk, dk, dk, dv, dv, dv, conv_dn, dproj)


def _softplus(x):
    return jnp.maximum(x, 0.0) + jnp.log(1.0 + jnp.exp(-jnp.abs(x)))


def _split3(x):
    hi = x.astype(BF16)
    r = x - hi.astype(F32)
    mid = r.astype(BF16)
    return hi, mid, (r - mid.astype(F32)).astype(BF16)


def _exact_nn(m, x):
    m = m.astype(BF16)
    hi, mid, lo = _split3(x)
    return _nn(m, hi) + _nn(m, mid) + _nn(m, lo)


def _chunk_cumsum_masks(tr):
    ri = lax.broadcasted_iota(jnp.int32, (tr, tr), 0)
    ci = lax.broadcasted_iota(jnp.int32, (tr, tr), 1)
    same = (ri // CHUNK) == (ci // CHUNK)
    return (same & (ci <= ri)).astype(F32), (same & (ci >= ri)).astype(F32)


def _prep_c(name, proj, pvec, lay, tr=512):
    s, nh = proj.shape[0], lay.nh
    tr = _tile(s, tr, CHUNK)
    assert 6 * nh <= LANES

    def body(x_ref, p_ref, o_ref):
        x = x_ref[...]
        lane = lax.broadcasted_iota(jnp.int32, x.shape, 1)
        is_g = (lane >= 2 * nh) & (lane < 4 * nh)
        g = jnp.where(is_g, -jnp.exp(p_ref[0:1, :]) * _softplus(x + p_ref[1:2, :]), 0.0)
        m_f, m_b = _chunk_cumsum_masks(tr)
        gc = jnp.where(lane < 3 * nh, _exact_nn(m_f, g), _exact_nn(m_b, g))
        gc = pltpu.roll(gc, 2 * nh, 1)
        o_ref[...] = jnp.where(lane < 2 * nh, jax.nn.sigmoid(x), jnp.where(lane < 4 * nh, g, gc))

    return pl.pallas_call(
        body, grid=(s // tr,), in_specs=[_row(tr, LANES, lay.ba // LANES), _vec(8, LANES)],
        out_specs=_row(tr, LANES), out_shape=SDS((s, LANES), F32),
        compiler_params=_params("parallel"), name=name)(proj, pvec)


def _prep_c_bwd(name, dbg_f, dbg_b, proj, pvec, dproj, lay, tr=512):
    s, nh, tail = proj.shape[0], lay.nh, lay.tail
    tr = _tile(s, tr, CHUNK)

    def body(x_ref, df_ref, db_ref, p_ref, _, o_ref, acc_ref):
        @pl.when(pl.program_id(0) == 0)
        def _():
            acc_ref[...] = jnp.zeros_like(acc_ref)

        x = x_ref[...]
        lane = lax.broadcasted_iota(jnp.int32, x.shape, 1)
        is_b, is_g = lane < 2 * nh, (lane >= 2 * nh) & (lane < 4 * nh)
        fwd_lane = (lane < nh) | ((lane >= 2 * nh) & (lane < 3 * nh))
        d = jnp.where(lane < 4 * nh, jnp.where(fwd_lane, df_ref[...], db_ref[...]), 0.0)
        m_f, m_b = _chunk_cumsum_masks(tr)
        dgc = jnp.where(is_g, d, 0.0)
        dg = jnp.where(fwd_lane, _exact_nn(m_b, dgc), _exact_nn(m_f, dgc))
        sb = jax.nn.sigmoid(x)
        na = -jnp.exp(p_ref[0:1, :])
        xs = x + p_ref[1:2, :]
        dsp = dg * na * jax.nn.sigmoid(xs)
        dx = jnp.where(is_b, d * sb * (1.0 - sb), jnp.where(is_g, dsp, 0.0))
        o_ref[...] = jnp.zeros_like(o_ref)
        o_ref[:, 0:LANES] = dx.astype(BF16)
        acc_ref[0:1, :] += jnp.sum(jnp.where(is_g, dg * na * _softplus(xs), 0.0), axis=0, keepdims=True)
        acc_ref[1:2, :] += jnp.sum(jnp.where(is_g, dsp, 0.0), axis=0, keepdims=True)

    return pl.pallas_call(
        body, grid=(s // tr,),
        in_specs=[_row(tr, LANES, lay.ba // LANES), _row(tr, LANES), _row(tr, LANES), _vec(8, LANES),
                  pl.BlockSpec(memory_space=pl.ANY)],
        out_specs=[_row(tr, tail, lay.ba // tail), _vec(8, LANES)],
        out_shape=[SDS(dproj.shape, dproj.dtype), SDS((8, LANES), F32)], input_output_aliases={4: 0},
        compiler_params=_params("arbitrary"), name=name)(proj, dbg_f, dbg_b, pvec, dproj)


def _post(name, o_f, o_b, proj, dn_w, lay, tr=256):
    s, qk, nh = o_f.shape[0], lay.qk, lay.nh
    tr = _tile(s, tr, 8)

    def body(f_ref, b_ref, z_ref, w_ref, o_ref):
        o = f_ref[...] + b_ref[...]
        gate = jax.nn.silu(z_ref[...])
        for hd in range(nh):
            sl = slice(hd * HEAD, (hd + 1) * HEAD)
            oh = o[:, sl]
            r = lax.rsqrt(jnp.mean(oh * oh, axis=-1, keepdims=True) + EPS)
            o_ref[:, sl] = (oh * r * w_ref[...] * gate[:, sl]).astype(BF16)

    return pl.pallas_call(
        body, grid=(s // tr,),
        in_specs=[_row(tr, qk), _row(tr, qk), _row(tr, qk, lay.z // qk), _vec(1, HEAD)],
        out_specs=_row(tr, qk), out_shape=SDS((s, qk), BF16),
        compiler_params=_params("parallel"), name=name)(o_f, o_b, proj, dn_w)


def _post_bwd(name, dyb, o_f, o_b, proj, dn_w, dproj, lay, tr=256):
    s, qk, nh = o_f.shape[0], lay.qk, lay.nh
    tr = _tile(s, tr, 8)

    def body(d_ref, f_ref, b_ref, z_ref, w_ref, _, do_ref, dz_ref, acc_ref):
        @pl.when(pl.program_id(0) == 0)
        def _():
            acc_ref[...] = jnp.zeros_like(acc_ref)

        o, z, d, wv = f_ref[...] + b_ref[...], z_ref[...], d_ref[...], w_ref[...]
        gate = jax.nn.silu(z)
        dgate = _silu_grad(z)
        for hd in range(nh):
            sl = slice(hd * HEAD, (hd + 1) * HEAD)
            oh, dh = o[:, sl], d[:, sl]
            r = lax.rsqrt(jnp.mean(oh * oh, axis=-1, keepdims=True) + EPS)
            n = oh * r
            dz_ref[:, sl] = (dh * n * wv * dgate[:, sl]).astype(BF16)
            don = dh * gate[:, sl]
            acc_ref[0:1, :] += jnp.sum(don * n, axis=0, keepdims=True)
            dn = don * wv
            do_ref[:, sl] = r * (dn - n * jnp.mean(dn * n, axis=-1, keepdims=True))

    return pl.pallas_call(
        body, grid=(s // tr,),
        in_specs=[_row(tr, qk), _row(tr, qk), _row(tr, qk), _row(tr, qk, lay.z // qk), _vec(1, HEAD),
                  pl.BlockSpec(memory_space=pl.ANY)],
        out_specs=[_row(tr, qk), _row(tr, qk, lay.z // qk), _vec(8, HEAD)],
        out_shape=[SDS((s, qk), F32), SDS(dproj.shape, dproj.dtype), SDS((8, HEAD), F32)],
        input_output_aliases={5: 1},
        compiler_params=_params("arbitrary"), name=name)(dyb, o_f, o_b, proj, dn_w, dproj)


def _merge(name, pa, pb, proj, lay, tr=512):
    s, d = pa.shape
    tr = _tile(s, tr, 8)

    def body(a_ref, b_ref, g_ref, o_ref):
        o_ref[...] = (jax.nn.sigmoid(g_ref[:, 0:d]) * a_ref[...]
                      + jax.nn.sigmoid(g_ref[:, d:2 * d]) * b_ref[...]).astype(BF16)

    return pl.pallas_call(
        body, grid=(s // tr,), in_specs=[_row(tr, d), _row(tr, d), _row(tr, 2 * d, lay.ga // (2 * d))],
        out_specs=_row(tr, d), out_shape=SDS((s, d), BF16),
        compiler_params=_params("parallel"), name=name)(pa, pb, proj)


def _merge_bwd(name, dmg, pa, pb, proj, lay, tr=512):
    s, d = pa.shape
    tr = _tile(s, tr, 8)

    def body(d_ref, a_ref, b_ref, g_ref, da_ref, db_ref, dg_ref):
        dm = d_ref[...]
        sa, sb = jax.nn.sigmoid(g_ref[:, 0:d]), jax.nn.sigmoid(g_ref[:, d:2 * d])
        da_ref[...] = (sa * dm).astype(BF16)
        db_ref[...] = (sb * dm).astype(BF16)
        dg_ref[:, 0:d] = (dm * a_ref[...] * sa * (1.0 - sa)).astype(BF16)
        dg_ref[:, d:2 * d] = (dm * b_ref[...] * sb * (1.0 - sb)).astype(BF16)

    return pl.pallas_call(
        body, grid=(s // tr,),
        in_specs=[_row(tr, d), _row(tr, d), _row(tr, d), _row(tr, 2 * d, lay.ga // (2 * d))],
        out_specs=[_row(tr, d), _row(tr, d), _row(tr, 2 * d, lay.ga // (2 * d))],
        out_shape=[SDS((s, d), BF16), SDS((s, d), BF16), SDS((s, lay.total), BF16)],
        compiler_params=_params("parallel"), name=name)(dmg, pa, pb, proj)


def _tri_inverse(a_mat, ri, ci):
    def same(shift):
        return (ri >> shift) == (ci >> shift)

    x = -jnp.where(same(3), a_mat, 0.0)
    t_mat = (ri == ci).astype(F32) + x
    for _ in range(2):
        x = _bnn(x, x)
        t_mat = t_mat + _bnn(t_mat, x)
    for shift in (3, 4, 5):
        b = jnp.where(same(shift + 1) & ~same(shift), a_mat, 0.0)
        t_mat = t_mat - _bnn(_bnn(t_mat, b), t_mat)
    return t_mat


def _chunk_terms(q, k, v, beta, gc, g_row, g_last, reverse, t_mat=None):
    c = CHUNK
    ri = lax.broadcasted_iota(jnp.int32, (c, c), 0)
    ci = lax.broadcasted_iota(jnp.int32, (c, c), 1)
    if reverse:
        incl, strict = ri <= ci, ri < ci
    else:
        incl, strict = ri >= ci, ri > ci
    decay = jnp.where(incl, jnp.exp(jnp.where(incl, gc - g_row, 0.0)), 0.0)
    e = jnp.exp(gc)
    ed = jnp.exp(g_last - gc)
    el = jnp.exp(g_last)
    kb = k * beta
    kk_qk = _bnt(jnp.concatenate([kb, q], axis=1), k)
    a_mat = jnp.where(strict, kk_qk[:, 0:c] * decay, 0.0)
    p_mat = jnp.where(incl, kk_qk[:, c:2 * c] * decay, 0.0)
    if t_mat is None:
        t_mat = _tri_inverse(a_mat, ri, ci)
    uw = _bnn(t_mat, jnp.concatenate([v * beta, kb * e], axis=2))
    return dict(incl=incl, strict=strict, decay=decay, e=e, ed=ed, el=el, kb=kb,
                a=a_mat, t=t_mat, uw=uw, u=uw[:, :, 0:HEAD], w=uw[:, :, HEAD:2 * HEAD], p=p_mat)


def _delta_specs(nh, tb, nb, reverse):
    tok = (lambda i: nb - 1 - i) if reverse else (lambda i: i)
    hw = nh * HEAD
    qkv = [pl.BlockSpec((tb, hw), functools.partial(lambda i, part: (tok(i), part), part=p)) for p in range(3)]
    rows = pl.BlockSpec((tb, hw), lambda i: (tok(i), 0))
    bg = pl.BlockSpec((tb, LANES), lambda i: (tok(i), 0))
    gct = pl.BlockSpec((2 * nh, tb), lambda i: (0, tok(i)))
    st = pl.BlockSpec((nh, tb // CHUNK, HEAD, HEAD), lambda i: (0, tok(i), 0, 0))
    tri = pl.BlockSpec((nh, tb // CHUNK, CHUNK, CHUNK), lambda i: (0, tok(i), 0, 0))
    return qkv, rows, bg, gct, st, tri


def _heads(ref, rows, nh):
    return jnp.stack([ref[rows, hd * HEAD:(hd + 1) * HEAD] for hd in range(nh)])


def _chunk_scalars(bg_ref, gct_ref, cj, nh, tb, reverse):
    rows = pl.ds(cj * CHUNK, CHUNK)
    lb = nh if reverse else 0
    lc = 4 * nh + lb
    last = cj * CHUNK + (0 if reverse else CHUNK - 1)
    g_lanes = gct_ref[lb:lb + nh, :]
    if cj:
        g_lanes = pltpu.roll(g_lanes, tb - cj * CHUNK, 1)
    col = lambda l0, r: jnp.stack([bg_ref[r, l0 + hd:l0 + hd + 1] for hd in range(nh)])
    return col(lb, rows), col(lc, rows), g_lanes[:, 0:CHUNK][:, None, :], col(lc, pl.ds(last, 1))


def _delta_fwd(name, qkvn, bg, gct, nh, reverse, tb=256):
    s = qkvn.shape[0]
    tb = _tile(s, tb, LANES)
    nb, cpb = s // tb, tb // CHUNK
    qkv, rows_spec, bg_spec, gct_spec, st, tri = _delta_specs(nh, tb, nb, reverse)

    def body(q_ref, k_ref, v_ref, bg_ref, gct_ref, o_ref, st_ref, tri_ref, state):
        @pl.when(pl.program_id(0) == 0)
        def _():
            state[...] = jnp.zeros_like(state)

        for cj in (range(cpb - 1, -1, -1) if reverse else range(cpb)):
            rows = pl.ds(cj * CHUNK, CHUNK)
            q, k, v = _heads(q_ref, rows, nh), _heads(k_ref, rows, nh), _heads(v_ref, rows, nh)
            tm = _chunk_terms(q, k, v, *_chunk_scalars(bg_ref, gct_ref, cj, nh, tb, reverse), reverse)
            s_in = state[...]
            st_ref[:, cj] = s_in
            tri_ref[:, cj] = tm["t"]
            ws_qs = _bnn(jnp.concatenate([tm["w"], q * tm["e"]], axis=1), s_in)
            vn = tm["u"] - ws_qs[:, 0:CHUNK]
            o = ws_qs[:, CHUNK:2 * CHUNK] + _bnn(tm["p"], vn)
            for hd in range(nh):
                o_ref[rows, hd * HEAD:(hd + 1) * HEAD] = o[hd]
            state[...] = s_in * tm["el"] + _btn(k * tm["ed"], vn)

    return pl.pallas_call(
        body, grid=(nb,), in_specs=qkv + [bg_spec, gct_spec], out_specs=[rows_spec, st, tri],
        out_shape=[SDS((s, nh * HEAD), F32), SDS((nh, s // CHUNK, HEAD, HEAD), F32),
                   SDS((nh, s // CHUNK, CHUNK, CHUNK), F32)],
        scratch_shapes=[pltpu.VMEM((nh, HEAD, HEAD), F32)],
        compiler_params=_params("arbitrary"), name=name)(qkvn, qkvn, qkvn, bg, gct)


def _delta_bwd(name, qkvn, bg, gct, do, states, tris, nh, reverse, add=None, tb=128, ex=None):
    s = qkvn.shape[0]
    tb = _tile(s, tb, LANES)
    nb, cpb = s // tb, tb // CHUNK
    qkv, rows_spec, bg_spec, gct_spec, st, tri = _delta_specs(nh, tb, nb, not reverse)
    n_add = 0 if add is None else 3
    host = _Hosted(ex, name, (nb,))

    def body(*refs):
        q_ref, k_ref, v_ref, bg_ref, gct_ref, do_ref, st_ref, tri_ref = refs[0:8]
        add_refs = refs[8:8 + n_add]
        (dq_ref, dk_ref, dv_ref, dbg_ref), (dstate,) = host.split(refs[8 + n_add:], 4)
        host.start()

        @pl.when(pl.program_id(0) == 0)
        def _():
            dstate[...] = jnp.zeros_like(dstate)

        ones = jnp.ones((nh, 2 * CHUNK, HEAD), BF16)
        row_id = lax.broadcasted_iota(jnp.int32, (CHUNK, 1), 0)
        rsum = lambda x: jnp.sum(x, axis=2, keepdims=True)
        for cj in (range(cpb) if reverse else range(cpb - 1, -1, -1)):
            rows = pl.ds(cj * CHUNK, CHUNK)
            q, k, v, d_o = (_heads(r, rows, nh) for r in (q_ref, k_ref, v_ref, do_ref))
            beta, gc, g_row, g_last = _chunk_scalars(bg_ref, gct_ref, cj, nh, tb, reverse)
            tm = _chunk_terms(q, k, v, beta, gc, g_row, g_last, reverse, t_mat=tri_ref[:, cj])
            incl, strict, e, ed, el, kb = tm["incl"], tm["strict"], tm["e"], tm["ed"], tm["el"], tm["kb"]
            t_mat, u, w, p_mat, decay = tm["t"], tm["u"], tm["w"], tm["p"], tm["decay"]
            s_in, ds_out = st_ref[:, cj], dstate[...]
            cat_rows = lambda a, b: jnp.concatenate([a, b], axis=1)
            top, bot = slice(0, CHUNK), slice(CHUNK, 2 * CHUNK)
            vn = u - _bnn(w, s_in)
            qe, kd, ke = q * e, k * ed, kb * e
            dvn = _btn(p_mat, d_o) + _bnn(kd, ds_out)
            by_state = _bnt(cat_rows(d_o, dvn), s_in)
            dqe, dw = by_state[:, top], -by_state[:, bot]
            dq = dqe * e
            dgc = rsum(dqe * qe)
            dp = jnp.where(incl, _bnt(d_o, vn), 0.0)
            dkd = _bnt(vn, ds_out)
            dk = dkd * ed
            r = rsum(dkd * kd)
            dgc = dgc - r
            dg_last = (jnp.sum(r, axis=1, keepdims=True)
                       + jnp.sum(rsum(ds_out * s_in), axis=1, keepdims=True) * el)
            d_uw = _btn(t_mat, jnp.concatenate([dvn, dw], axis=2))
            dbv, dke = d_uw[:, :, 0:HEAD], d_uw[:, :, HEAD:2 * HEAD]
            da = -jnp.where(strict, _bnt(d_uw, tm["uw"]), 0.0)
            mn = cat_rows(da * decay, dp * decay)
            by_k = _bnn(mn, k)
            dkb = by_k[:, top] + dke * e
            dq = dq + by_k[:, bot]
            dk = dk + _btn(mn, cat_rows(kb, q))
            g_mat = da * tm["a"] + dp * p_mat
            g_hi, g_mid, _ = _split3(g_mat)
            col = _btn(cat_rows(g_hi, g_mid), ones)[:, :, 0:1]
            dgc = dgc + rsum(g_mat) - col + rsum(dke * ke)
            dgc = dgc + jnp.where(row_id == (0 if reverse else CHUNK - 1), dg_last, 0.0)
            dv = dbv * beta
            dbeta = rsum(dbv * v) + rsum(dkb * k)
            dk = dk + dkb * beta
            dstate[...] = el * ds_out + _btn(cat_rows(qe, w), cat_rows(d_o, -dvn))
            lb = nh if reverse else 0
            for hd in range(nh):
                cols = slice(hd * HEAD, (hd + 1) * HEAD)
                extra = [a[rows, cols] for a in add_refs] if n_add else [0.0, 0.0, 0.0]
                dq_ref[rows, cols] = dq[hd] + extra[0]
                dk_ref[rows, cols] = dk[hd] + extra[1]
                dv_ref[rows, cols] = dv[hd] + extra[2]
                dbg_ref[rows, lb + hd:lb + hd + 1] = dbeta[hd]
                dbg_ref[rows, 2 * nh + lb + hd:2 * nh + lb + hd + 1] = dgc[hd]
        host.wait()

    out3 = SDS((s, nh * HEAD), F32)
    n_in = 8 + n_add
    out = pl.pallas_call(
        body, grid=(nb,),
        in_specs=qkv + [bg_spec, gct_spec, rows_spec, st, tri] + [rows_spec] * n_add + host.in_specs,
        out_specs=[rows_spec, rows_spec, rows_spec, bg_spec] + host.out_specs,
        out_shape=[out3, out3, out3, SDS((s, LANES), F32)] + host.out_shapes,
        input_output_aliases=host.aliases(n_in, 4), scratch_shapes=[pltpu.VMEM((nh, HEAD, HEAD), F32)] + host.sems,
        compiler_params=_params("arbitrary"), name=name)(
            qkvn, qkvn, qkvn, bg, gct, do, states, tris, *(add or ()), *host.arrays)
    return host.finish(out, 4)


def _row_pieces(g):
    return g.reshape(N_CHIPS, g.shape[0] // N_CHIPS, g.shape[1])


def _up_swiglu(name, u, w_up, tm=1024, ex=None):
    s, k = u.shape
    fh = w_up.shape[2]
    tm = _tile(s, tm, 8)
    grid = (2, s // tm)
    host = _Hosted(ex, name, grid)

    def body(u_ref, wa_ref, wb_ref, *rest):
        (a_ref, b_ref, o_ref), _ = host.split(rest, 3)
        host.start()
        x = u_ref[...]
        a, b = _nn(x, wa_ref[...]), _nn(x, wb_ref[...])
        a_ref[...], b_ref[...] = a.astype(BF16), b.astype(BF16)
        o_ref[...] = (jax.nn.silu(a) * b).astype(BF16)
        host.wait()

    tile = pl.BlockSpec((tm, fh), lambda j, i: (i, j))
    out = pl.pallas_call(
        body, grid=grid,
        in_specs=[pl.BlockSpec((tm, k), lambda j, i: (i, 0)), pl.BlockSpec((None, k, fh), lambda j, i: (j, 0, 0)),
                  pl.BlockSpec((None, k, fh), lambda j, i: (2 + j, 0, 0))] + host.in_specs,
        out_specs=[tile] * 3 + host.out_specs, out_shape=[SDS((s, 2 * fh), BF16)] * 3 + host.out_shapes,
        input_output_aliases=host.aliases(3, 3), scratch_shapes=host.sems,
        compiler_params=_params("arbitrary", "arbitrary"), name=name)(u, w_up, w_up, *host.arrays)
    return host.finish(out, 3)


def _matmul_resid(name, a, w, h, g, scale, tm=512, ex=None):
    s, k = a.shape
    d = w.shape[1]
    tm = _tile(s, tm, 16)
    grid = (s // tm,)
    host = _Hosted(ex, name, grid)

    def body(a_ref, w_ref, h_ref, g_ref, *rest):
        (o_ref, f_ref), _ = host.split(rest, 2)
        host.start()
        f = _nn(a_ref[...], w_ref[...])
        f_ref[...] = f.astype(BF16)
        o_ref[...] = h_ref[...] + (scale * g_ref[...]) * f
        host.wait()

    out = pl.pallas_call(
        body, grid=grid, in_specs=[_row(tm, k), _vec(k, d), _row(tm, d), _vec(1, d)] + host.in_specs,
        out_specs=[_row(tm, d), _row(tm, d)] + host.out_specs,
        out_shape=[SDS((s, d), F32), SDS((s, d), BF16)] + host.out_shapes,
        input_output_aliases=host.aliases(4, 2), scratch_shapes=host.sems,
        compiler_params=_params("arbitrary"), name=name)(a, w, h, g, *host.arrays)
    return host.finish(out, 2)


def _down_swiglu_bwd(name, df, w_down, a_pre, b_pre, tm=512):
    s, d = df.shape
    f = w_down.shape[0]
    tm = _tile(s, tm, 16)

    def body(df_ref, w_ref, a_ref, b_ref, o_ref):
        dhm = _nt(df_ref[...], w_ref[...])
        a = a_ref[...].astype(F32)
        o_ref[:, 0:f] = (dhm * b_ref[...].astype(F32) * _silu_grad(a)).astype(BF16)
        o_ref[:, f:2 * f] = (dhm * jax.nn.silu(a)).astype(BF16)

    return pl.pallas_call(
        body, grid=(s // tm,), in_specs=[_row(tm, d), _vec(f, d), _row(tm, f), _row(tm, f)],
        out_specs=_row(tm, 2 * f), out_shape=SDS((s, 2 * f), BF16),
        compiler_params=_params("parallel"), name=name)(df, w_down, a_pre, b_pre)


def _col_pieces(w):
    return w if w.ndim == 3 else w.reshape(w.shape[0], N_CHIPS, -1).transpose(1, 0, 2)


def _ffn_fwd(tag, h, nw, sh, sc, g, w_up, w_down, ex=None):
    u = _norm_mod(tag + "_norm", h, nw, sh, sc)
    a_pre, b_pre, hm = _up_swiglu(tag + "_up", u, _col_pieces(w_up), ex=ex)
    h_new, f = _matmul_resid(tag + "_down", hm, w_down, h, g, 0.5, ex=ex)
    return h_new, (h, u, a_pre, b_pre, hm, f)


def _ffn_bwd(tag, dh, saved, nw, sc, g, w_up, w_down, ex=None):
    h, u, a_pre, b_pre, hm, f = saved
    df, acc_g = _resid_bwd(tag + "_res_bwd", dh, f, g, 0.5)
    gw_down = _matmul(tag + "_gw_down", hm, df, "tn", out_dtype=BF16, tm=1408, ex=ex)
    dab = _down_swiglu_bwd(tag + "_dhm", df, w_down, a_pre, b_pre)
    gw_up = _matmul(tag + "_gw_up", u, dab, "tn", out_dtype=BF16, tn=1408, out_pieces=N_CHIPS, ex=ex)
    du = _matmul(tag + "_du", dab, w_up, "nt", ex=ex)
    dh_in, acc = _norm_mod_bwd(tag + "_norm_bwd", h, du, dh, nw, sc)
    return dh_in, gw_up, _row_pieces(gw_down), (acc[0], acc[1], acc_g[0], acc[2])


def _mixer_fwd(h, nw, sh, sc, g, wt, lay, ex=None):
    nh = lay.nh
    u = _norm_mod("mix_norm", h, nw, sh, sc)
    proj = _matmul("mix_in", u, wt["w_in"], "nn", ex=ex)
    qkvn = _prep_b("mix_prep_b", proj, wt["conv_dn"], lay)
    ya = _prep_a("mix_prep_a", proj, wt["conv_a"], lay)
    bg = _prep_c("mix_prep_c", proj, wt["pvec"], lay)
    gct = bg[:, 4 * nh:6 * nh].T
    o_f, *st_f = _delta_fwd("delta_fwd_l2r", qkvn, bg, gct, nh, False)
    o_b, *st_b = _delta_fwd("delta_fwd_r2l", qkvn, bg, gct, nh, True)
    yb = _post("mix_post", o_f, o_b, proj, wt["dn_norm"], lay)
    pa = _matmul("mix_a_out", ya, wt["w_a_out"], "nn")
    pb = _matmul("mix_b_out", yb, wt["w_b_out"], "nn")
    mg = _merge("mix_merge", pa, pb, proj, lay)
    h2, y = _matmul_resid("mix_out", mg, wt["w_out"], h, g, 1.0, ex=ex)
    return h2, (h, u, proj, qkvn, ya, bg, gct, o_f, o_b, st_f, st_b, yb, pa, pb, mg, y)


def _mixer_bwd(dh, saved, nw, sc, g, wt, lay, ex=None):
    h, u, proj, qkvn, ya, bg, gct, o_f, o_b, st_f, st_b, yb, pa, pb, mg, y = saved
    nh = lay.nh
    dy, acc_g = _resid_bwd("mix_res_bwd", dh, y, g, 1.0)
    gw_out = _matmul("mix_gw_out", mg, dy, "tn", out_dtype=BF16, ex=ex)
    dmg = _matmul("mix_dmg", dy, wt["w_out"], "nt")
    dpa, dpb, dproj = _merge_bwd("mix_merge_bwd", dmg, pa, pb, proj, lay)
    gw_a = _matmul("mix_gw_a", ya, dpa, "tn", out_dtype=BF16, out_pieces=N_CHIPS)
    gw_b = _matmul("mix_gw_b", yb, dpb, "tn", out_dtype=BF16)
    dya = _matmul("mix_dya", dpa, wt["w_a_out"], "nt")
    dyb = _matmul("mix_dyb", dpb, wt["w_b_out"], "nt")
    do, dproj, acc_dn = _post_bwd("mix_post_bwd", dyb, o_f, o_b, proj, wt["dn_norm"], dproj, lay)
    dq, dk, dv, dbg_f = _delta_bwd("delta_bwd_l2r", qkvn, bg, gct, do, *st_f, nh, False, ex=ex)
    dq, dk, dv, dbg_b = _delta_bwd("delta_bwd_r2l", qkvn, bg, gct, do, *st_b, nh, True, add=(dq, dk, dv), ex=ex)
    dproj, acc_ca = _prep_a_bwd("mix_prep_a_bwd", dya, proj, wt["conv_a"], dproj, lay)
    dproj, acc_cd = _prep_b_bwd("mix_prep_b_bwd", dq, dk, dv, proj, wt["conv_dn"], dproj, lay)
    dproj, acc_pc = _prep_c_bwd("mix_prep_c_bwd", dbg_f, dbg_b, proj, wt["pvec"], dproj, lay)
    gw_in = lay.unperm_cols(_matmul("mix_gw_in", u, dproj, "tn", out_dtype=BF16))
    gw_in = gw_in.reshape(gw_in.shape[0], N_CHIPS, -1).transpose(1, 0, 2)
    du = _matmul("mix_du", dproj, wt["w_in"], "nt")
    dh_in, acc = _norm_mod_bwd("mix_norm_bwd", h, du, dh, nw, sc)
    small = dict(conv_a=acc_ca[0:wt["conv_a"].shape[0]], conv_dn=acc_cd[0:wt["conv_dn"].shape[0]],
                 dn_norm=acc_dn[0:1], a_log=acc_pc[0], dt_bias=acc_pc[1])
    big = dict(w_in=gw_in, w_a_out=gw_a, w_b_out=_row_pieces(gw_b), w_out=_row_pieces(gw_out))
    return dh_in, big, small, (acc[0], acc[1], acc_g[0], acc[2])


def _as_operands(gathered, lay):
    wt = {}
    for n, g in gathered.items():
        if n == "w_in":
            wt[n] = lay.perm_cols(jnp.concatenate(list(g), axis=1))
        else:
            wt[n] = g if n in COL_SHARDED else g.reshape(-1, g.shape[-1])
    return wt


def _local_step(x, tgt, modv, wt, lay, ex=None):
    m = [modv[i:i + 1] for i in range(9)]
    h1, sv1 = _ffn_fwd("ffn1", x, wt["norm_ffn1"], m[0], m[1], m[2], wt["w_ffn1_up"], wt["w_ffn1_down"], ex)
    if ex:
        wt = dict(wt, **_as_operands(ex.weights("mixer"), lay))
    h2, sv2 = _mixer_fwd(h1, wt["norm_mix"], m[3], m[4], m[5], wt, lay, ex)
    if ex:
        wt = dict(wt, **_as_operands(ex.weights("ffn2"), lay))
    h3, sv3 = _ffn_fwd("ffn2", h2, wt["norm_ffn2"], m[6], m[7], m[8], wt["w_ffn2_up"], wt["w_ffn2_down"])
    dh3, acc_f = _final_loss("final_loss", h3, tgt, wt["norm_final"])
    loss = jnp.sum(acc_f[1])
    dh2, gu2, gd2, dm3 = _ffn_bwd("ffn2", dh3, sv3, wt["norm_ffn2"], m[7], m[8], wt["w_ffn2_up"], wt["w_ffn2_down"])
    if ex:
        ex.reduce("ffn2", dict(w_ffn2_up=gu2, w_ffn2_down=gd2))
    dh1, gmix, small, dm2 = _mixer_bwd(dh2, sv2, wt["norm_mix"], m[4], m[5], wt, lay, ex)
    if ex:
        ex.reduce("mixer", gmix)
    dx, gu1, gd1, dm1 = _ffn_bwd("ffn1", dh1, sv1, wt["norm_ffn1"], m[1], m[2], wt["w_ffn1_up"], wt["w_ffn1_down"], ex)
    dmod = jnp.stack([dm1[0], dm1[1], dm1[2], dm2[0], dm2[1], dm2[2], dm3[0], dm3[1], dm3[2]])
    big = dict(w_ffn1_up=gu1, w_ffn1_down=gd1, w_ffn2_up=gu2, w_ffn2_down=gd2, **gmix)
    small = dict(small, norm_ffn1=dm1[3], norm_mix=dm2[3], norm_ffn2=dm3[3], norm_final=acc_f[0])
    return loss, dx, dmod, big, small


def _position():
    return lax.axis_index("x"), lax.axis_index("y"), lax.axis_index("c")


_ANY = pl.BlockSpec(memory_space=pl.ANY)
_VMEM = pl.BlockSpec(memory_space=pltpu.VMEM)


def _allgather8(name, v):
    r = v.shape[0]

    def body(v_ref, out_ref, send_sems, recv_sems):
        x, y, c = _position()
        me = 4 * x + 2 * y + c
        out_ref[me] = v_ref[...]
        copies = []
        for mask in range(1, N_DEV):
            peer = tuple(1 - p if mask >> b & 1 else p for p, b in ((x, 2), (y, 1), (c, 0)))
            cp = pltpu.make_async_remote_copy(
                src_ref=v_ref, dst_ref=out_ref.at[me], send_sem=send_sems.at[mask - 1],
                recv_sem=recv_sems.at[mask - 1], device_id=peer, device_id_type=MESH)
            cp.start()
            copies.append(cp)
        for cp in copies:
            cp.wait()

    return pl.pallas_call(
        body, in_specs=[_VMEM], out_specs=_VMEM, out_shape=SDS((N_DEV, r, LANES), F32),
        scratch_shapes=[pltpu.SemaphoreType.DMA((N_DEV - 1,)), pltpu.SemaphoreType.DMA((N_DEV - 1,))],
        name=name)(v)


def _other_chips(x, y):
    return [(1 - x, y), (x, 1 - y), (1 - x, 1 - y)]


def _half_rows(c, rows):
    hr = rows // 2
    assert hr % 16 == 0
    return pl.ds(pl.multiple_of(c * hr, 16), hr)


class _Stage:
    def __init__(self, arrays, out_shapes, sems, plan, in_place=False):
        self.arrays, self.out_shapes, self.sems, self.plan, self.in_place = arrays, out_shapes, sems, plan, in_place

    def start(self, ins, outs, sems):
        for kind, cp in self.plan(ins, outs, sems):
            if kind != "recv":
                cp.start()

    def wait(self, ins, outs, sems):
        for kind, cp in self.plan(ins, outs, sems):
            {"local": cp.wait, "both": cp.wait, "send": cp.wait_send, "recv": cp.wait_recv}[kind]()

    def aliases(self, in_offset, out_offset):
        return {in_offset + i: out_offset + i for i in range(len(self.arrays))} if self.in_place else {}


def _run_stage(name, stage):
    n_in, n_out = len(stage.arrays), len(stage.out_shapes)

    def body(*refs):
        ins, outs, sems = refs[0:n_in], refs[n_in:n_in + n_out], refs[n_in + n_out:]
        stage.start(ins, outs, sems)
        stage.wait(ins, outs, sems)

    return pl.pallas_call(
        body, in_specs=[_ANY] * n_in, out_specs=[_ANY] * n_out, out_shape=stage.out_shapes,
        input_output_aliases=stage.aliases(0, 0), scratch_shapes=stage.sems, name=name)(*stage.arrays)


def _dma_sems(*counts):
    return [pltpu.SemaphoreType.DMA((n,)) for n in counts]


def _gather_send(shards):
    nw = len(shards)

    def plan(ins, outs, sems):
        send_sems, recv_sems, local_sems = sems
        x, y, c = _position()
        p = 2 * x + y
        todo = [("local", pltpu.make_async_copy(ins[w], outs[w].at[p], local_sems.at[w])) for w in range(nw)]
        for j, (cx, cy) in enumerate(_other_chips(x, y)):
            for w in range(nw):
                half = _half_rows(c, ins[w].shape[0])
                sem = dict(send_sem=send_sems.at[j * nw + w], recv_sem=recv_sems.at[j * nw + w], device_id_type=MESH)
                todo.append(("send", pltpu.make_async_remote_copy(
                    src_ref=ins[w].at[half], dst_ref=outs[w].at[p, half], device_id=(cx, cy, c), **sem)))
                landing = outs[w].at[2 * cx + cy, half]
                todo.append(("recv", pltpu.make_async_remote_copy(
                    src_ref=landing, dst_ref=landing, device_id=(x, y, c), **sem)))
        return todo

    return _Stage(shards, [SDS((N_CHIPS,) + v.shape, v.dtype) for v in shards], _dma_sems(3 * nw, 3 * nw, nw), plan)


def _gather_pass(gathered):
    nw = len(gathered)

    def plan(ins, outs, sems):
        send_sems, recv_sems = sems
        x, y, c = _position()
        todo = []
        for j, (cx, cy) in enumerate(_other_chips(x, y)):
            for w in range(nw):
                sem = dict(send_sem=send_sems.at[j * nw + w], recv_sem=recv_sems.at[j * nw + w], device_id_type=MESH)
                mine = outs[w].at[2 * cx + cy, _half_rows(c, outs[w].shape[1])]
                theirs = outs[w].at[2 * cx + cy, _half_rows(1 - c, outs[w].shape[1])]
                todo.append(("send", pltpu.make_async_remote_copy(
                    src_ref=mine, dst_ref=mine, device_id=(x, y, 1 - c), **sem)))
                todo.append(("recv", pltpu.make_async_remote_copy(
                    src_ref=theirs, dst_ref=theirs, device_id=(x, y, c), **sem)))
        return todo

    return _Stage(gathered, [SDS(g.shape, g.dtype) for g in gathered], _dma_sems(3 * nw, 3 * nw), plan, in_place=True)


def _swap_halves(gs):
    nw = len(gs)

    def plan(ins, outs, sems):
        x, y, c = _position()
        return [("both", pltpu.make_async_remote_copy(
            src_ref=ins[w].at[:, _half_rows(1 - c, ins[w].shape[1])], dst_ref=outs[w], send_sem=sems[0].at[w],
            recv_sem=sems[1].at[w], device_id=(x, y, 1 - c), device_id_type=MESH)) for w in range(nw)]

    return _Stage(gs, [SDS((g.shape[0], g.shape[1] // 2, g.shape[2]), g.dtype) for g in gs], _dma_sems(nw, nw), plan)


def _scatter_chips(vs):
    nw = len(vs)

    def plan(ins, outs, sems):
        x, y, c = _position()
        return [("both", pltpu.make_async_remote_copy(
            src_ref=ins[w].at[2 * cx + cy], dst_ref=outs[w].at[j], send_sem=sems[0].at[j * nw + w],
            recv_sem=sems[1].at[j * nw + w], device_id=(cx, cy, c), device_id_type=MESH))
            for j, (cx, cy) in enumerate(_other_chips(x, y)) for w in range(nw)]

    return _Stage(vs, [SDS((N_CHIPS - 1,) + v.shape[1:], v.dtype) for v in vs], _dma_sems(3 * nw, 3 * nw), plan)


def _share_halves(fulls):
    nw = len(fulls)

    def plan(ins, outs, sems):
        x, y, c = _position()
        todo = []
        for w in range(nw):
            rows = outs[w].at[_half_rows(c, outs[w].shape[0])]
            todo.append(("both", pltpu.make_async_remote_copy(
                src_ref=rows, dst_ref=rows, send_sem=sems[0].at[w], recv_sem=sems[1].at[w],
                device_id=(x, y, 1 - c), device_id_type=MESH)))
        return todo

    return _Stage(fulls, [SDS(f.shape, f.dtype) for f in fulls], _dma_sems(nw, nw), plan, in_place=True)


class _Hosted:
    def __init__(self, ex, name, grid):
        self.ex, self.name, self.grid = ex, name, grid
        self.stage = ex.host(name) if ex is not None else None
        st = self.stage
        self.arrays = list(st.arrays) if st else []
        self.out_shapes = list(st.out_shapes) if st else []
        self.sems = list(st.sems) if st else []
        self.in_specs, self.out_specs = [_ANY] * len(self.arrays), [_ANY] * len(self.out_shapes)

    def aliases(self, in_offset, out_offset):
        return self.stage.aliases(in_offset, out_offset) if self.stage else {}

    def split(self, rest, n_out):
        ni, no, ns = len(self.arrays), len(self.out_shapes), len(self.sems)
        self.ins, self.outs = rest[0:ni], rest[ni + n_out:ni + n_out + no]
        tail = rest[ni + n_out + no:]
        self.sem_refs = tail[len(tail) - ns:]
        return rest[ni:ni + n_out], tail[0:len(tail) - ns]

    def _at(self, last):
        conds = [pl.program_id(d) == (g - 1 if last else 0) for d, g in enumerate(self.grid)]
        return functools.reduce(lambda p, q: p & q, conds)

    def start(self):
        if self.stage:
            pl.when(self._at(False))(lambda: self.stage.start(self.ins, self.outs, self.sem_refs))

    def wait(self):
        if self.stage:
            pl.when(self._at(True))(lambda: self.stage.wait(self.ins, self.outs, self.sem_refs))

    def finish(self, out, n_out):
        out = list(out)
        if self.stage:
            self.ex.done(self.name, out[n_out:])
        return out[0:n_out]


GROUPS = {"ffn1": ("w_ffn1_up", "w_ffn1_down"), "mixer": ("w_in", "w_a_out", "w_b_out", "w_out"),
          "ffn2": ("w_ffn2_up", "w_ffn2_down")}
HOSTS = {"ffn1_up": ("gather_send", "mixer"), "ffn1_down": ("gather_pass", "mixer"),
         "mix_in": ("gather_send", "ffn2"), "mix_out": ("gather_pass", "ffn2"),
         "mix_gw_out": ("swap", "ffn2"), "delta_bwd_l2r": ("scatter", "ffn2"), "delta_bwd_r2l": ("share", "ffn2"),
         "ffn1_gw_down": ("swap", "mixer"), "ffn1_gw_up": ("scatter", "mixer"), "ffn1_du": ("share", "mixer")}


class _Exchange:
    def __init__(self, shards):
        self.shards = shards
        self.gathered, self.red = {}, {}

    def _stage(self, kind, group):
        if kind == "gather_send":
            return _gather_send([self.shards[n] for n in GROUPS[group]])
        if kind == "gather_pass":
            return _gather_pass(self.gathered[group])
        st = self.red[group]
        return {"swap": lambda: _swap_halves(st["parts"]), "scatter": lambda: _scatter_chips(st["chip_sums"]),
                "share": lambda: _share_halves(st["fulls"])}[kind]()

    def _done(self, kind, group, outs):
        names = GROUPS[group]
        if kind in ("gather_send", "gather_pass"):
            self.gathered[group] = list(outs)
            return
        st = self.red[group]
        if kind == "swap":
            st["from_sib"] = list(outs)
            st["chip_sums"] = [_chip_sum("chip_sum_" + n, g, f) for n, g, f in zip(names, st["parts"], outs)]
        elif kind == "scatter":
            st["fulls"] = [_total("total_" + n, g, f, r)
                           for n, g, f, r in zip(names, st["parts"], st["from_sib"], outs)]
        else:
            st["grads"] = dict(zip(names, outs))

    def host(self, kernel_name):
        return self._stage(*HOSTS[kernel_name]) if kernel_name in HOSTS else None

    def done(self, kernel_name, outs):
        self._done(*HOSTS[kernel_name], outs)

    def run(self, kind, group):
        self._done(kind, group, _run_stage(f"{kind}_{group}", self._stage(kind, group)))

    def weights(self, group):
        return dict(zip(GROUPS[group], self.gathered[group]))

    def reduce(self, group, parts):
        self.red[group] = dict(parts=[parts[n] for n in GROUPS[group]])

    def grads(self, group):
        return self.red[group]["grads"]


def _chip_sum(name, g, from_sib):
    _, r, cdim = g.shape
    hr = r // 2

    def body(g_ref, s_ref, o_ref):
        o_ref[...] = (g_ref[...].astype(F32) + s_ref[...].astype(F32)).astype(o_ref.dtype)

    blk = (None, hr, cdim)
    return pl.pallas_call(
        body, grid=(N_CHIPS,),
        in_specs=[pl.BlockSpec(blk, lambda j: (j, lax.axis_index("c"), 0)), pl.BlockSpec(blk, lambda j: (j, 0, 0))],
        out_specs=pl.BlockSpec(blk, lambda j: (j, 0, 0)),
        out_shape=SDS((N_CHIPS, hr, cdim), g.dtype), compiler_params=_params("parallel"), name=name)(g, from_sib)


def _total(name, g, from_sib, from_chips):
    _, r, cdim = g.shape
    hr = r // 2
    tr = _tile(hr, 256, 16)
    nt = hr // tr

    def body(g_ref, s_ref, rc_ref, o_ref):
        acc = g_ref[...].astype(F32) + s_ref[...].astype(F32)
        for j in range(N_CHIPS - 1):
            acc = acc + rc_ref[j].astype(F32)
        o_ref[...] = acc

    def chip():
        return 2 * lax.axis_index("x") + lax.axis_index("y")

    blk = (None, tr, cdim)
    return pl.pallas_call(
        body, grid=(nt,),
        in_specs=[pl.BlockSpec(blk, lambda i: (chip(), lax.axis_index("c") * nt + i, 0)),
                  pl.BlockSpec(blk, lambda i: (chip(), i, 0)),
                  pl.BlockSpec((N_CHIPS - 1, tr, cdim), lambda i: (0, i, 0))],
        out_specs=pl.BlockSpec((tr, cdim), lambda i: (lax.axis_index("c") * nt + i, 0)),
        out_shape=SDS((r, cdim), F32), compiler_params=_params("parallel"), name=name)(g, from_sib, from_chips)


def _sum8(name, v):
    _, r, w = v.shape

    def body(v_ref, o_ref):
        acc = v_ref[0]
        for j in range(1, N_DEV):
            acc = acc + v_ref[j]
        o_ref[...] = acc

    return pl.pallas_call(body, in_specs=[_VMEM], out_specs=_VMEM, out_shape=SDS((r, w), F32), name=name)(v)


def _adamw(name, w, g, m, v, tr=256):
    r, cdim = w.shape
    tr = _tile(r, tr, 8)
    bc1, bc2 = 1.0 - ADAM_B1 ** ADAM_STEP, 1.0 - ADAM_B2 ** ADAM_STEP

    def body(w_ref, g_ref, m_ref, v_ref, d_ref, nm_ref, nv_ref):
        g = g_ref[...]
        m2 = ADAM_B1 * m_ref[...] + (1.0 - ADAM_B1) * g
        v2 = ADAM_B2 * v_ref[...] + (1.0 - ADAM_B2) * (g * g)
        d_ref[...] = -ADAM_LR * ((m2 / bc1) / (jnp.sqrt(v2 / bc2) + ADAM_EPS) + ADAM_WD * w_ref[...])
        nm_ref[...] = m2
        nv_ref[...] = v2

    spec = _row(tr, cdim)
    out = SDS((r, cdim), F32)
    return pl.pallas_call(body, grid=(r // tr,), in_specs=[spec] * 4, out_specs=[spec] * 3, out_shape=[out] * 3,
                          compiler_params=_params("parallel"), name=name)(w, g, m, v)


def _pack_rows(arrays, width, row_mult, dtype):
    parts, spans, row = [], [], 0
    for a in arrays:
        n = a.size
        rows = -(-n // width)
        flat = a.reshape(-1).astype(dtype)
        if rows * width != n:
            flat = jnp.concatenate([flat, jnp.zeros((rows * width - n,), dtype)])
        parts.append(flat.reshape(rows, width))
        spans.append((row, rows, n, a.shape))
        row += rows
    pad = -row % row_mult
    if pad:
        parts.append(jnp.zeros((pad, width), dtype))
    return jnp.concatenate(parts, axis=0), spans


def _unpack_rows(packed, spans):
    return [packed[r0:r0 + rows].reshape(-1)[0:n].reshape(shape) for r0, rows, n, shape in spans]


BIG = ("w_ffn1_up", "w_ffn1_down", "w_in", "w_a_out", "w_b_out", "w_out", "w_ffn2_up", "w_ffn2_down")
COL_SHARDED = ("w_ffn1_up", "w_in", "w_a_out", "w_ffn2_up")
SMALL = ("b_ada", "norm_ffn1", "norm_mix", "conv_a", "conv_dn", "a_log_fwd", "dt_bias_fwd", "a_log_bwd",
         "dt_bias_bwd", "dn_norm", "norm_ffn2", "norm_final")
WEIGHTS = ("w_ada", "b_ada", "norm_ffn1", "w_ffn1_up", "w_ffn1_down", "norm_mix", "w_in", "conv_a", "conv_dn",
           "a_log_fwd", "dt_bias_fwd", "a_log_bwd", "dt_bias_bwd", "dn_norm", "w_a_out", "w_b_out", "w_out",
           "norm_ffn2", "w_ffn2_up", "w_ffn2_down", "norm_final")


def kernel(x, c, w_ada, b_ada, norm_ffn1, w_ffn1_up, w_ffn1_down, norm_mix, w_in, conv_a, conv_dn, a_log_fwd, dt_bias_fwd, a_log_bwd, dt_bias_bwd, dn_norm, w_a_out, w_b_out, w_out, norm_ffn2, w_ffn2_up, w_ffn2_down, norm_final, loss_target, m_w_ada, m_b_ada, m_norm_ffn1, m_w_ffn1_up, m_w_ffn1_down, m_norm_mix, m_w_in, m_conv_a, m_conv_dn, m_a_log_fwd, m_dt_bias_fwd, m_a_log_bwd, m_dt_bias_bwd, m_dn_norm, m_w_a_out, m_w_b_out, m_w_out, m_norm_ffn2, m_w_ffn2_up, m_w_ffn2_down, m_norm_final, v_w_ada, v_b_ada, v_norm_ffn1, v_w_ffn1_up, v_w_ffn1_down, v_norm_mix, v_w_in, v_conv_a, v_conv_dn, v_a_log_fwd, v_dt_bias_fwd, v_a_log_bwd, v_dt_bias_bwd, v_dn_norm, v_w_a_out, v_w_b_out, v_w_out, v_norm_ffn2, v_w_ffn2_up, v_w_ffn2_down, v_norm_final):
    given = dict(locals())
    wsh = {n: given[n] for n in WEIGHTS}
    msh = {n: given["m_" + n] for n in WEIGHTS}
    vsh = {n: given["v_" + n] for n in WEIGHTS}
    d = x.shape[-1]
    ca = conv_a.shape[-1] * N_CHIPS
    nh = conv_dn.shape[-1] * N_CHIPS // (3 * HEAD)
    lay = _Layout(d, ca, nh)
    xi, yi, ci = _position()
    chip = 2 * xi + yi
    me = 2 * chip + ci

    c_act = jax.nn.silu(c)
    g1, g1_spans = _pack_rows([c_act, conv_a[0], conv_dn[0]], LANES, 8, F32)
    g1_all = _allgather8("gather_cond", g1)
    per_dev = [_unpack_rows(g1_all[k], g1_spans) for k in range(N_DEV)]
    c_all = jnp.concatenate([p[0] for p in per_dev], axis=0)
    conv_a_full = jnp.concatenate([per_dev[2 * k][1] for k in range(N_CHIPS)], axis=1)
    conv_dn_full = jnp.concatenate([per_dev[2 * k][2] for k in range(N_CHIPS)], axis=1)

    mod_sh = _matmul("ada_mod", c_all, w_ada[0], "nn")
    b_sh = lax.dynamic_slice_in_dim(b_ada, chip * mod_sh.shape[1], mod_sh.shape[1], axis=1)
    g2, g2_spans = _pack_rows([mod_sh + b_sh], LANES, 8, F32)
    g2_all = _allgather8("gather_mod", g2)
    mod_all = jnp.concatenate([_unpack_rows(g2_all[2 * k], g2_spans)[0] for k in range(N_CHIPS)], axis=1)
    modv = lax.dynamic_index_in_dim(mod_all, me, 0, keepdims=False).reshape(9, d)

    ex = _Exchange({n: wsh[n][0].astype(BF16) for n in BIG})
    ex.run("gather_send", "ffn1")
    ex.run("gather_pass", "ffn1")
    wt = _as_operands(ex.weights("ffn1"), lay)
    lane_pad = (jnp.zeros((2 * nh,), F32), jnp.zeros((LANES - 4 * nh,), F32))
    pvec = jnp.stack([jnp.concatenate([lane_pad[0], a_log_fwd[0], a_log_bwd[0], lane_pad[1]]),
                      jnp.concatenate([lane_pad[0], dt_bias_fwd[0], dt_bias_bwd[0], lane_pad[1]])]
                     + [jnp.zeros((LANES,), F32)] * 6)
    wt.update(conv_a=conv_a_full, conv_dn=conv_dn_full, pvec=pvec, dn_norm=dn_norm, norm_ffn1=norm_ffn1,
              norm_mix=norm_mix, norm_ffn2=norm_ffn2, norm_final=norm_final.reshape(1, d))

    loss, dx, dmod, big, small = _local_step(x[0], loss_target[0], modv, wt, lay, ex)
    loss = lax.psum(loss, ("x", "y", "c"))

    small_list = [dmod.reshape(1, 9 * d), small["norm_ffn1"], small["norm_mix"], small["conv_a"], small["conv_dn"],
                  small["a_log"][2 * nh:3 * nh], small["dt_bias"][2 * nh:3 * nh], small["a_log"][3 * nh:4 * nh],
                  small["dt_bias"][3 * nh:4 * nh], small["dn_norm"], small["norm_ffn2"], small["norm_final"]]
    g3, g3_spans = _pack_rows(small_list, LANES, 8, F32)
    g3_all = _allgather8("gather_small_grads", g3)
    g_small = dict(zip(SMALL, _unpack_rows(_sum8("sum_small_grads", g3_all), g3_spans)))
    dmod_all = jnp.concatenate([_unpack_rows(g3_all[k], g3_spans)[0] for k in range(N_DEV)], axis=0)
    ncol = w_ada.shape[-1]
    dmod_sh = lax.dynamic_slice_in_dim(dmod_all, chip * ncol, ncol, axis=1)
    grads = {"w_ada": _matmul("ada_grad", c_all, dmod_sh, "tn")[None]}
    for n in SMALL:
        g = g_small[n]
        if n in ("conv_a", "conv_dn"):
            wloc = wsh[n].shape[-1]
            g = lax.dynamic_slice_in_dim(g, chip * wloc, wloc, axis=1)
        grads[n] = g.reshape(wsh[n].shape)

    ex.reduce("ffn1", big)
    for kind in ("swap", "scatter", "share"):
        ex.run(kind, "ffn1")
    for group in GROUPS:
        for n, g in ex.grads(group).items():
            grads[n] = g[None]

    delta, new_m, new_v = {}, {}, {}
    for n in ("w_ada",) + BIG:
        shp = wsh[n].shape
        outs = _adamw("adamw_" + n, *(t.reshape(shp[-2], shp[-1]) for t in (wsh[n], grads[n], msh[n], vsh[n])))
        delta[n], new_m[n], new_v[n] = (o.reshape(shp) for o in outs)
    packed = []
    for src in (wsh, grads, msh, vsh):
        pk, s_spans = _pack_rows([src[n] for n in SMALL], LANES, 8, F32)
        packed.append(pk)
    outs = _adamw("adamw_small", *packed)
    for dst, o in zip((delta, new_m, new_v), outs):
        dst.update(zip(SMALL, _unpack_rows(o, s_spans)))

    return (loss, dx[None], *[grads[n] for n in WEIGHTS], *[delta[n] for n in WEIGHTS],
            *[new_m[n] for n in WEIGHTS], *[new_v[n] for n in WEIGHTS])
```

```python
import functools

import jax
import jax.numpy as jnp
from jax import lax
from jax.experimental import pallas as pl
from jax.experimental.pallas import tpu as pltpu

F32 = jnp.float32
BF16 = jnp.bfloat16
SDS = jax.ShapeDtypeStruct
MESH = pl.DeviceIdType.MESH
HI = lax.Precision.HIGHEST

EPS = 1e-6
HEAD = 128
CHUNK = 64
LANES = 128
N_CHIPS = 4
N_DEV = 8
VMEM_LIMIT = 56 * 1024 * 1024

ADAM_LR = 0.001
ADAM_B1 = 0.9
ADAM_B2 = 0.999
ADAM_EPS = 1e-08
ADAM_WD = 0.01
ADAM_STEP = 10


def _params(*sem):
    return pltpu.CompilerParams(dimension_semantics=sem, vmem_limit_bytes=VMEM_LIMIT)


def _tile(n, cap, mult=LANES):
    t = min(n, cap) // mult * mult
    while t >= mult:
        if n % t == 0:
            return t
        t -= mult
    return n


def _row(tr, w, cb=0):
    return pl.BlockSpec((tr, w), lambda i: (i, cb))


def _vec(r, w):
    return pl.BlockSpec((r, w), lambda i: (0, 0))


def _nn(a, b, **kw):
    return jnp.dot(a, b, preferred_element_type=F32, **kw)


def _nt(a, b, **kw):
    return lax.dot_general(a, b, (((1,), (1,)), ((), ())), preferred_element_type=F32, **kw)


def _tn(a, b, **kw):
    return lax.dot_general(a, b, (((0,), (0,)), ((), ())), preferred_element_type=F32, **kw)


def _bnn(a, b):
    return lax.dot_general(a, b, (((2,), (1,)), ((0,), (0,))), preferred_element_type=F32)


def _bnt(a, b):
    return lax.dot_general(a, b, (((2,), (2,)), ((0,), (0,))), preferred_element_type=F32)


def _btn(a, b):
    return lax.dot_general(a, b, (((1,), (1,)), ((0,), (0,))), preferred_element_type=F32)


def _silu_grad(x):
    s = jax.nn.sigmoid(x)
    return s * (1.0 + x * (1.0 - s))


def _matmul(name, a, b, mode, out_dtype=F32, tm=1024, tn=1024, tk=2048, full_k=2816, out_pieces=0, ex=None):
    pieces_b = b.shape[0] if b.ndim == 3 else 0
    b2 = b.shape[1:] if pieces_b else b.shape
    if mode == "nn":
        (m, k), n = a.shape, b2[1] * max(pieces_b, 1)
    elif mode == "nt":
        (m, _), n, k = a.shape, b2[0], b2[1] * max(pieces_b, 1)
    else:
        (k, m), n = a.shape, b2[1] * max(pieces_b, 1)
    n_unit = n // max(out_pieces, 1) if mode == "nt" or not pieces_b else n // pieces_b
    if out_pieces and pieces_b and mode != "nt":
        assert out_pieces == pieces_b
    k_unit = k // pieces_b if (pieces_b and mode == "nt") else k
    tm, tn = _tile(m, tm), _tile(n_unit, tn)
    tk = k_unit if k_unit <= full_k else _tile(k_unit, tk)
    nk = k // tk
    n_per, k_per = n_unit // tn, k_unit // tk
    a_bytes, b_bytes = a.size * a.dtype.itemsize, b.size * b.dtype.itemsize
    j_outer = nk == 1 and b_bytes + a_bytes * (n // tn) < a_bytes + b_bytes * (m // tm)
    ij = (lambda g0, g1: (g1, g0)) if j_outer else (lambda g0, g1: (g0, g1))

    def spec(shape, pick):
        return pl.BlockSpec(shape, lambda g0, g1, l: pick(*ij(g0, g1), l))

    a_spec = {"nn": spec((tm, tk), lambda i, j, l: (i, l)), "nt": spec((tm, tk), lambda i, j, l: (i, l)),
              "tn": spec((tk, tm), lambda i, j, l: (l, i))}[mode]
    if not pieces_b:
        b_spec = {"nn": spec((tk, tn), lambda i, j, l: (l, j)), "nt": spec((tn, tk), lambda i, j, l: (j, l)),
                  "tn": spec((tk, tn), lambda i, j, l: (l, j))}[mode]
    elif mode == "nt":
        b_spec = spec((None, tn, tk), lambda i, j, l: (l // k_per, j, l % k_per))
    else:
        b_spec = spec((None, tk, tn), lambda i, j, l: (j // n_per, l, j % n_per))
    if out_pieces:
        o_spec = spec((None, tm, tn), lambda i, j, l: (j // n_per, i, j % n_per))
        o_shape = SDS((out_pieces, m, n // out_pieces), out_dtype)
    else:
        o_spec, o_shape = spec((tm, tn), lambda i, j, l: (i, j)), SDS((m, n), out_dtype)
    dot = {"nn": _nn, "nt": _nt, "tn": _tn}[mode]
    grid = (n // tn, m // tm, nk) if j_outer else (m // tm, n // tn, nk)
    host = _Hosted(ex, name, grid)

    def body(a_ref, b_ref, *rest):
        (o_ref,), scratch = host.split(rest, 1)
        host.start()
        part = dot(a_ref[...].astype(BF16), b_ref[...].astype(BF16))
        if nk == 1:
            o_ref[...] = part.astype(o_ref.dtype)
        else:
            l, acc = pl.program_id(2), scratch[0]

            @pl.when(l == 0)
            def _():
                acc[...] = part

            @pl.when((l > 0) & (l < nk - 1))
            def _():
                acc[...] += part

            @pl.when(l == nk - 1)
            def _():
                o_ref[...] = (acc[...] + part).astype(o_ref.dtype)
        host.wait()

    out = pl.pallas_call(
        body, grid=grid, in_specs=[a_spec, b_spec] + host.in_specs, out_specs=[o_spec] + host.out_specs,
        out_shape=[o_shape] + host.out_shapes, input_output_aliases=host.aliases(2, 1),
        scratch_shapes=([] if nk == 1 else [pltpu.VMEM((tm, tn), F32)]) + host.sems,
        compiler_params=_params(*(("arbitrary",) * 3 if host.stage else ("parallel", "parallel", "arbitrary"))),
        name=name)(a, b, *host.arrays)
    return host.finish(out, 1)[0]


def _norm_mod(name, h, nw, sh, sc, tr=512):
    s, d = h.shape
    tr = _tile(s, tr, 8)

    def body(h_ref, nw_ref, sh_ref, sc_ref, u_ref):
        x = h_ref[...]
        r = lax.rsqrt(jnp.mean(x * x, axis=-1, keepdims=True) + EPS)
        u_ref[...] = (x * r * nw_ref[...] * (1.0 + sc_ref[...]) + sh_ref[...]).astype(BF16)

    return pl.pallas_call(
        body, grid=(s // tr,), in_specs=[_row(tr, d), _vec(1, d), _vec(1, d), _vec(1, d)],
        out_specs=_row(tr, d), out_shape=SDS((s, d), BF16),
        compiler_params=_params("parallel"), name=name)(h, nw, sh, sc)


def _norm_mod_bwd(name, h, du, dh, nw, sc, tr=512):
    s, d = h.shape
    tr = _tile(s, tr, 8)

    def body(h_ref, du_ref, dh_ref, nw_ref, sc_ref, o_ref, acc_ref):
        @pl.when(pl.program_id(0) == 0)
        def _():
            acc_ref[...] = jnp.zeros_like(acc_ref)

        x, g = h_ref[...], du_ref[...]
        r = lax.rsqrt(jnp.mean(x * x, axis=-1, keepdims=True) + EPS)
        n = x * r
        nw, sc1 = nw_ref[...], 1.0 + sc_ref[...]
        dn = g * sc1 * nw
        o_ref[...] = dh_ref[...] + r * (dn - n * jnp.mean(dn * n, axis=-1, keepdims=True))
        gn = g * n
        acc_ref[0:1, :] += jnp.sum(g, axis=0, keepdims=True)
        acc_ref[1:2, :] += jnp.sum(gn * nw, axis=0, keepdims=True)
        acc_ref[2:3, :] += jnp.sum(gn * sc1, axis=0, keepdims=True)

    return pl.pallas_call(
        body, grid=(s // tr,),
        in_specs=[_row(tr, d), _row(tr, d), _row(tr, d), _vec(1, d), _vec(1, d)],
        out_specs=[_row(tr, d), _vec(8, d)], out_shape=[SDS((s, d), F32), SDS((8, d), F32)],
        compiler_params=_params("arbitrary"), name=name)(h, du, dh, nw, sc)


def _resid_bwd(name, dh, f, g, scale, tr=512):
    s, d = dh.shape
    tr = _tile(s, tr, 8)

    def body(dh_ref, f_ref, g_ref, o_ref, acc_ref):
        @pl.when(pl.program_id(0) == 0)
        def _():
            acc_ref[...] = jnp.zeros_like(acc_ref)

        x = dh_ref[...]
        o_ref[...] = ((scale * g_ref[...]) * x).astype(BF16)
        acc_ref[0:1, :] += jnp.sum(scale * x * f_ref[...], axis=0, keepdims=True)

    return pl.pallas_call(
        body, grid=(s // tr,), in_specs=[_row(tr, d), _row(tr, d), _vec(1, d)],
        out_specs=[_row(tr, d), _vec(8, d)], out_shape=[SDS((s, d), BF16), SDS((8, d), F32)],
        compiler_params=_params("arbitrary"), name=name)(dh, f, g)


def _final_loss(name, h, tgt, nw, tr=512):
    s, d = h.shape
    tr = _tile(s, tr, 8)

    def body(h_ref, t_ref, nw_ref, o_ref, acc_ref):
        @pl.when(pl.program_id(0) == 0)
        def _():
            acc_ref[...] = jnp.zeros_like(acc_ref)

        x, nw = h_ref[...], nw_ref[...]
        r = lax.rsqrt(jnp.mean(x * x, axis=-1, keepdims=True) + EPS)
        n = x * r
        diff = n * nw - t_ref[...]
        dy = diff * (1.0 / d)
        dn = dy * nw
        o_ref[...] = r * (dn - n * jnp.mean(dn * n, axis=-1, keepdims=True))
        acc_ref[0:1, :] += jnp.sum(dy * n, axis=0, keepdims=True)
        acc_ref[1:2, :] += jnp.sum(diff * diff, axis=0, keepdims=True) * (0.5 / d)

    return pl.pallas_call(
        body, grid=(s // tr,), in_specs=[_row(tr, d), _row(tr, d), _vec(1, d)],
        out_specs=[_row(tr, d), _vec(8, d)], out_shape=[SDS((s, d), F32), SDS((8, d), F32)],
        compiler_params=_params("arbitrary"), name=name)(h, tgt, nw)


class _Layout:
    def __init__(self, d, ca, nh):
        self.d, self.ca, self.nh = d, ca, nh
        self.qk = nh * HEAD
        self.qkv = 3 * self.qk
        self.z = self.qkv
        self.ga = self.z + self.qk
        self.cab = self.ga + 2 * d
        self.ba = self.cab + 3 * ca
        self.tail = _tile(self.ba, 512)
        self.total = self.ba + self.tail
        assert self.qkv % self.qk == 0 and self.ga % (2 * d) == 0 and self.cab % (3 * ca) == 0
        assert self.ba % self.tail == 0 and 4 * nh <= LANES

    def perm_cols(self, w):
        ca, qkv, qk, d, nh = self.ca, self.qkv, self.qk, self.d, self.nh
        o = [0, ca, 2 * ca, 3 * ca, 3 * ca + qkv, 3 * ca + qkv + qk, 3 * ca + qkv + qk + 4 * nh]
        cb, cc, cv = (w[..., o[i]:o[i + 1]] for i in range(3))
        x_qkv, x_z, x_ba = w[..., o[3]:o[4]], w[..., o[4]:o[5]], w[..., o[5]:o[6]]
        gates = w[..., o[6]:o[6] + 2 * d]
        pad = jnp.zeros(w.shape[:-1] + (self.tail - 4 * nh,), w.dtype)
        return jnp.concatenate([x_qkv, x_z, gates, cb, cc, cv, x_ba, pad], axis=-1)

    def unperm_cols(self, w):
        ca, nh = self.ca, self.nh
        cb, cc, cv = (w[..., self.cab + i * ca:self.cab + (i + 1) * ca] for i in range(3))
        return jnp.concatenate([cb, cc, cv, w[..., 0:self.qkv], w[..., self.z:self.ga],
                                w[..., self.ba:self.ba + 4 * nh], w[..., self.ga:self.cab]], axis=-1)


def _halo_specs(tr, w, cb, s):
    nb8 = s // 8
    return [pl.BlockSpec((8, w), lambda i: (jnp.maximum(i * (tr // 8) - 1, 0), cb)),
            pl.BlockSpec((tr, w), lambda i: (i, cb)),
            pl.BlockSpec((8, w), lambda i: (jnp.minimum((i + 1) * (tr // 8), nb8 - 1), cb))]


def _ext(prev_ref, main_ref, next_ref, i, nt):
    p = jnp.where(i > 0, prev_ref[...].astype(F32), 0.0)
    n = jnp.where(i < nt - 1, next_ref[...].astype(F32), 0.0)
    return jnp.concatenate([p, main_ref[...].astype(F32), n], axis=0)


def _shift(x, k):
    return x if k == 0 else pltpu.roll(x, (-k) % x.shape[0], 0)


def _conv_taps(x_ext, w, tr):
    kt = w.shape[0]
    acc = None
    for t in range(kt):
        term = _shift(x_ext, t - kt // 2)[8:8 + tr] * w[t:t + 1, :]
        acc = term if acc is None else acc + term
    return acc


def _prep_a(name, proj, conv_a, lay, tr=256):
    s, ca = proj.shape[0], lay.ca
    tr = _tile(s, tr, 8)
    nt, w = s // tr, 3 * ca

    def body(p_ref, m_ref, n_ref, w_ref, o_ref):
        x = _ext(p_ref, m_ref, n_ref, pl.program_id(0), nt)
        xv = x[:, ca:2 * ca] * x[:, 2 * ca:w]
        y = _conv_taps(xv, w_ref[...], tr)
        o_ref[...] = (m_ref[:, 0:ca] * y).astype(BF16)

    return pl.pallas_call(
        body, grid=(nt,), in_specs=_halo_specs(tr, w, lay.cab // w, s) + [_vec(conv_a.shape[0], ca)],
        out_specs=_row(tr, ca), out_shape=SDS((s, ca), BF16),
        compiler_params=_params("parallel"), name=name)(proj, proj, proj, conv_a)


def _prep_a_bwd(name, dya, proj, conv_a, dproj, lay, tr=256):
    s, ca = proj.shape[0], lay.ca
    tr = _tile(s, tr, 8)
    nt, w, kt = s // tr, 3 * ca, conv_a.shape[0]

    def body(p_ref, m_ref, n_ref, dp_ref, dm_ref, dn_ref, w_ref, _, o_ref, acc_ref):
        i = pl.program_id(0)

        @pl.when(i == 0)
        def _():
            acc_ref[...] = jnp.zeros_like(acc_ref)

        x = _ext(p_ref, m_ref, n_ref, i, nt)
        d_ext = _ext(dp_ref, dm_ref, dn_ref, i, nt)
        cb, cc, cv = x[:, 0:ca], x[:, ca:2 * ca], x[:, 2 * ca:w]
        xv = cc * cv
        wv = w_ref[...]
        dy_ext = d_ext * cb
        dx = None
        for t in range(kt):
            term = _shift(dy_ext, kt // 2 - t)[8:8 + tr] * wv[t:t + 1, :]
            dx = term if dx is None else dx + term
            acc_ref[t:t + 1, :] += jnp.sum(dy_ext[8:8 + tr] * _shift(xv, t - kt // 2)[8:8 + tr],
                                           axis=0, keepdims=True)
        y = _conv_taps(xv, wv, tr)
        o_ref[:, 0:ca] = (dm_ref[...] * y).astype(BF16)
        o_ref[:, ca:2 * ca] = (dx * cv[8:8 + tr]).astype(BF16)
        o_ref[:, 2 * ca:w] = (dx * cc[8:8 + tr]).astype(BF16)

    return pl.pallas_call(
        body, grid=(nt,),
        in_specs=_halo_specs(tr, w, lay.cab // w, s) + _halo_specs(tr, ca, 0, s)
        + [_vec(kt, ca), pl.BlockSpec(memory_space=pl.ANY)],
        out_specs=[_row(tr, w, lay.cab // w), _vec(8, ca)],
        out_shape=[SDS(dproj.shape, dproj.dtype), SDS((8, ca), F32)], input_output_aliases={7: 0},
        compiler_params=_params("arbitrary"), name=name)(proj, proj, proj, dya, dya, dya, conv_a, dproj)


def _qkv_act(c, nh, tr_rows):
    sact = jax.nn.silu(c)
    outs, inv = [], []
    for hd in range(3 * nh):
        sl = sact[:, hd * HEAD:(hd + 1) * HEAD]
        if hd < 2 * nh:
            r = lax.rsqrt(jnp.sum(sl * sl, axis=-1, keepdims=True) + EPS)
            inv.append(r)
            outs.append(sl * (r * (HEAD ** -0.5 if hd < nh else 1.0)))
        else:
            outs.append(sl)
    return jnp.concatenate(outs, axis=-1), sact, inv


def _prep_b(name, proj, conv_dn, lay, tr=256):
    s, w, nh = proj.shape[0], lay.qkv, lay.nh
    tr = _tile(s, tr, 8)
    nt = s // tr

    def body(p_ref, m_ref, n_ref, w_ref, o_ref):
        x = _ext(p_ref, m_ref, n_ref, pl.program_id(0), nt)
        c = _conv_taps(x, w_ref[...], tr)
        o_ref[...] = _qkv_act(c, nh, tr)[0]

    return pl.pallas_call(
        body, grid=(nt,), in_specs=_halo_specs(tr, w, 0, s) + [_vec(conv_dn.shape[0], w)],
        out_specs=_row(tr, w), out_shape=SDS((s, w), F32),
        compiler_params=_params("parallel"), name=name)(proj, proj, proj, conv_dn)


def _prep_b_bwd(name, dq, dk, dv, proj, conv_dn, dproj, lay, tr=256):
    s, w, nh, qk = proj.shape[0], lay.qkv, lay.nh, lay.qk
    tr = _tile(s, tr, 8)
    nt, kt = s // tr, conv_dn.shape[0]

    def body(*refs):
        x_refs, g_refs = refs[0:3], refs[3:12]
        w_ref, o_ref, acc_ref = refs[12], refs[14], refs[15]
        i = pl.program_id(0)

        @pl.when(i == 0)
        def _():
            acc_ref[...] = jnp.zeros_like(acc_ref)

        x = _ext(*x_refs, i, nt)
        wv = w_ref[...]
        c = None
        for t in range(kt):
            term = _shift(x, t - kt // 2) * wv[t:t + 1, :]
            c = term if c is None else c + term
        sig = jax.nn.sigmoid(c)
        sact = c * sig
        ds = []
        for part in range(3):
            g = _ext(*g_refs[3 * part:3 * part + 3], i, nt)
            for hd in range(nh):
                sl = sact[:, part * qk + hd * HEAD:part * qk + (hd + 1) * HEAD]
                gh = g[:, hd * HEAD:(hd + 1) * HEAD]
                if part < 2:
                    r = lax.rsqrt(jnp.sum(sl * sl, axis=-1, keepdims=True) + EPS)
                    sc = HEAD ** -0.5 if part == 0 else 1.0
                    ds.append(sc * r * (gh - sl * (r * r) * jnp.sum(gh * sl, axis=-1, keepdims=True)))
                else:
                    ds.append(gh)
        dc = jnp.concatenate(ds, axis=-1) * (sig * (1.0 + c * (1.0 - sig)))
        dx = None
        for t in range(kt):
            term = _shift(dc, kt // 2 - t)[8:8 + tr] * wv[t:t + 1, :]
            dx = term if dx is None else dx + term
            acc_ref[t:t + 1, :] += jnp.sum(dc[8:8 + tr] * _shift(x, t - kt // 2)[8:8 + tr],
                                           axis=0, keepdims=True)
        o_ref[...] = dx.astype(BF16)

    return pl.pallas_call(
        body, grid=(nt,),
        in_specs=_halo_specs(tr, w, 0, s) + _halo_specs(tr, qk, 0, s) * 3
        + [_vec(kt, w), pl.BlockSpec(memory_space=pl.ANY)],
        out_specs=[_row(tr, w, 0), _vec(8, w)],
        out_shape=[SDS(dproj.shape, dproj.dtype), SDS((8, w), F32)], input_output_aliases={13: 0},
        compiler_params=_params("arbitrary"), name=name)(
            proj, proj, proj, dq, dq, dq, dk, dk, dk, dv, dv, dv, conv_dn, dproj)


def _softplus(x):
    return jnp.maximum(x, 0.0) + jnp.log(1.0 + jnp.exp(-jnp.abs(x)))


def _split3(x):
    hi = x.astype(BF16)
    r = x - hi.astype(F32)
    mid = r.astype(BF16)
    return hi, mid, (r - mid.astype(F32)).astype(BF16)


def _exact_nn(m, x):
    m = m.astype(BF16)
    hi, mid, lo = _split3(x)
    return _nn(m, hi) + _nn(m, mid) + _nn(m, lo)


def _chunk_cumsum_masks(tr):
    ri = lax.broadcasted_iota(jnp.int32, (tr, tr), 0)
    ci = lax.broadcasted_iota(jnp.int32, (tr, tr), 1)
    same = (ri // CHUNK) == (ci // CHUNK)
    return (same & (ci <= ri)).astype(F32), (same & (ci >= ri)).astype(F32)


def _prep_c(name, proj, pvec, lay, tr=512):
    s, nh = proj.shape[0], lay.nh
    tr = _tile(s, tr, CHUNK)
    assert 6 * nh <= LANES

    def body(x_ref, p_ref, o_ref):
        x = x_ref[...]
        lane = lax.broadcasted_iota(jnp.int32, x.shape, 1)
        is_g = (lane >= 2 * nh) & (lane < 4 * nh)
        g = jnp.where(is_g, -jnp.exp(p_ref[0:1, :]) * _softplus(x + p_ref[1:2, :]), 0.0)
        m_f, m_b = _chunk_cumsum_masks(tr)
        gc = jnp.where(lane < 3 * nh, _exact_nn(m_f, g), _exact_nn(m_b, g))
        gc = pltpu.roll(gc, 2 * nh, 1)
        o_ref[...] = jnp.where(lane < 2 * nh, jax.nn.sigmoid(x), jnp.where(lane < 4 * nh, g, gc))

    return pl.pallas_call(
        body, grid=(s // tr,), in_specs=[_row(tr, LANES, lay.ba // LANES), _vec(8, LANES)],
        out_specs=_row(tr, LANES), out_shape=SDS((s, LANES), F32),
        compiler_params=_params("parallel"), name=name)(proj, pvec)


def _prep_c_bwd(name, dbg_f, dbg_b, proj, pvec, dproj, lay, tr=512):
    s, nh, tail = proj.shape[0], lay.nh, lay.tail
    tr = _tile(s, tr, CHUNK)

    def body(x_ref, df_ref, db_ref, p_ref, _, o_ref, acc_ref):
        @pl.when(pl.program_id(0) == 0)
        def _():
            acc_ref[...] = jnp.zeros_like(acc_ref)

        x = x_ref[...]
        lane = lax.broadcasted_iota(jnp.int32, x.shape, 1)
        is_b, is_g = lane < 2 * nh, (lane >= 2 * nh) & (lane < 4 * nh)
        fwd_lane = (lane < nh) | ((lane >= 2 * nh) & (lane < 3 * nh))
        d = jnp.where(lane < 4 * nh, jnp.where(fwd_lane, df_ref[...], db_ref[...]), 0.0)
        m_f, m_b = _chunk_cumsum_masks(tr)
        dgc = jnp.where(is_g, d, 0.0)
        dg = jnp.where(fwd_lane, _exact_nn(m_b, dgc), _exact_nn(m_f, dgc))
        sb = jax.nn.sigmoid(x)
        na = -jnp.exp(p_ref[0:1, :])
        xs = x + p_ref[1:2, :]
        dsp = dg * na * jax.nn.sigmoid(xs)
        dx = jnp.where(is_b, d * sb * (1.0 - sb), jnp.where(is_g, dsp, 0.0))
        o_ref[...] = jnp.zeros_like(o_ref)
        o_ref[:, 0:LANES] = dx.astype(BF16)
        acc_ref[0:1, :] += jnp.sum(jnp.where(is_g, dg * na * _softplus(xs), 0.0), axis=0, keepdims=True)
        acc_ref[1:2, :] += jnp.sum(jnp.where(is_g, dsp, 0.0), axis=0, keepdims=True)

    return pl.pallas_call(
        body, grid=(s // tr,),
        in_specs=[_row(tr, LANES, lay.ba // LANES), _row(tr, LANES), _row(tr, LANES), _vec(8, LANES),
                  pl.BlockSpec(memory_space=pl.ANY)],
        out_specs=[_row(tr, tail, lay.ba // tail), _vec(8, LANES)],
        out_shape=[SDS(dproj.shape, dproj.dtype), SDS((8, LANES), F32)], input_output_aliases={4: 0},
        compiler_params=_params("arbitrary"), name=name)(proj, dbg_f, dbg_b, pvec, dproj)


def _post(name, o_f, o_b, proj, dn_w, lay, tr=256):
    s, qk, nh = o_f.shape[0], lay.qk, lay.nh
    tr = _tile(s, tr, 8)

    def body(f_ref, b_ref, z_ref, w_ref, o_ref):
        o = f_ref[...] + b_ref[...]
        gate = jax.nn.silu(z_ref[...])
        for hd in range(nh):
            sl = slice(hd * HEAD, (hd + 1) * HEAD)
            oh = o[:, sl]
            r = lax.rsqrt(jnp.mean(oh * oh, axis=-1, keepdims=True) + EPS)
            o_ref[:, sl] = (oh * r * w_ref[...] * gate[:, sl]).astype(BF16)

    return pl.pallas_call(
        body, grid=(s // tr,),
        in_specs=[_row(tr, qk), _row(tr, qk), _row(tr, qk, lay.z // qk), _vec(1, HEAD)],
        out_specs=_row(tr, qk), out_shape=SDS((s, qk), BF16),
        compiler_params=_params("parallel"), name=name)(o_f, o_b, proj, dn_w)


def _post_bwd(name, dyb, o_f, o_b, proj, dn_w, dproj, lay, tr=256):
    s, qk, nh = o_f.shape[0], lay.qk, lay.nh
    tr = _tile(s, tr, 8)

    def body(d_ref, f_ref, b_ref, z_ref, w_ref, _, do_ref, dz_ref, acc_ref):
        @pl.when(pl.program_id(0) == 0)
        def _():
            acc_ref[...] = jnp.zeros_like(acc_ref)

        o, z, d, wv = f_ref[...] + b_ref[...], z_ref[...], d_ref[...], w_ref[...]
        gate = jax.nn.silu(z)
        dgate = _silu_grad(z)
        for hd in range(nh):
            sl = slice(hd * HEAD, (hd + 1) * HEAD)
            oh, dh = o[:, sl], d[:, sl]
            r = lax.rsqrt(jnp.mean(oh * oh, axis=-1, keepdims=True) + EPS)
            n = oh * r
            dz_ref[:, sl] = (dh * n * wv * dgate[:, sl]).astype(BF16)
            don = dh * gate[:, sl]
            acc_ref[0:1, :] += jnp.sum(don * n, axis=0, keepdims=True)
            dn = don * wv
            do_ref[:, sl] = r * (dn - n * jnp.mean(dn * n, axis=-1, keepdims=True))

    return pl.pallas_call(
        body, grid=(s // tr,),
        in_specs=[_row(tr, qk), _row(tr, qk), _row(tr, qk), _row(tr, qk, lay.z // qk), _vec(1, HEAD),
                  pl.BlockSpec(memory_space=pl.ANY)],
        out_specs=[_row(tr, qk), _row(tr, qk, lay.z // qk), _vec(8, HEAD)],
        out_shape=[SDS((s, qk), F32), SDS(dproj.shape, dproj.dtype), SDS((8, HEAD), F32)],
        input_output_aliases={5: 1},
        compiler_params=_params("arbitrary"), name=name)(dyb, o_f, o_b, proj, dn_w, dproj)


def _merge(name, pa, pb, proj, lay, tr=512):
    s, d = pa.shape
    tr = _tile(s, tr, 8)

    def body(a_ref, b_ref, g_ref, o_ref):
        o_ref[...] = (jax.nn.sigmoid(g_ref[:, 0:d]) * a_ref[...]
                      + jax.nn.sigmoid(g_ref[:, d:2 * d]) * b_ref[...]).astype(BF16)

    return pl.pallas_call(
        body, grid=(s // tr,), in_specs=[_row(tr, d), _row(tr, d), _row(tr, 2 * d, lay.ga // (2 * d))],
        out_specs=_row(tr, d), out_shape=SDS((s, d), BF16),
        compiler_params=_params("parallel"), name=name)(pa, pb, proj)


def _merge_bwd(name, dmg, pa, pb, proj, lay, tr=512):
    s, d = pa.shape
    tr = _tile(s, tr, 8)

    def body(d_ref, a_ref, b_ref, g_ref, da_ref, db_ref, dg_ref):
        dm = d_ref[...]
        sa, sb = jax.nn.sigmoid(g_ref[:, 0:d]), jax.nn.sigmoid(g_ref[:, d:2 * d])
        da_ref[...] = (sa * dm).astype(BF16)
        db_ref[...] = (sb * dm).astype(BF16)
        dg_ref[:, 0:d] = (dm * a_ref[...] * sa * (1.0 - sa)).astype(BF16)
        dg_ref[:, d:2 * d] = (dm * b_ref[...] * sb * (1.0 - sb)).astype(BF16)

    return pl.pallas_call(
        body, grid=(s // tr,),
        in_specs=[_row(tr, d), _row(tr, d), _row(tr, d), _row(tr, 2 * d, lay.ga // (2 * d))],
        out_specs=[_row(tr, d), _row(tr, d), _row(tr, 2 * d, lay.ga // (2 * d))],
        out_shape=[SDS((s, d), BF16), SDS((s, d), BF16), SDS((s, lay.total), BF16)],
        compiler_params=_params("parallel"), name=name)(dmg, pa, pb, proj)


def _tri_inverse(a_mat, ri, ci):
    def same(shift):
        return (ri >> shift) == (ci >> shift)

    x = -jnp.where(same(3), a_mat, 0.0)
    t_mat = (ri == ci).astype(F32) + x
    for _ in range(2):
        x = _bnn(x, x)
        t_mat = t_mat + _bnn(t_mat, x)
    for shift in (3, 4, 5):
        b = jnp.where(same(shift + 1) & ~same(shift), a_mat, 0.0)
        t_mat = t_mat - _bnn(_bnn(t_mat, b), t_mat)
    return t_mat


def _chunk_terms(q, k, v, beta, gc, g_row, g_last, reverse, t_mat=None):
    c = CHUNK
    ri = lax.broadcasted_iota(jnp.int32, (c, c), 0)
    ci = lax.broadcasted_iota(jnp.int32, (c, c), 1)
    if reverse:
        incl, strict = ri <= ci, ri < ci
    else:
        incl, strict = ri >= ci, ri > ci
    decay = jnp.where(incl, jnp.exp(jnp.where(incl, gc - g_row, 0.0)), 0.0)
    e = jnp.exp(gc)
    ed = jnp.exp(g_last - gc)
    el = jnp.exp(g_last)
    kb = k * beta
    kk_qk = _bnt(jnp.concatenate([kb, q], axis=1), k)
    a_mat = jnp.where(strict, kk_qk[:, 0:c] * decay, 0.0)
    p_mat = jnp.where(incl, kk_qk[:, c:2 * c] * decay, 0.0)
    if t_mat is None:
        t_mat = _tri_inverse(a_mat, ri, ci)
    uw = _bnn(t_mat, jnp.concatenate([v * beta, kb * e], axis=2))
    return dict(incl=incl, strict=strict, decay=decay, e=e, ed=ed, el=el, kb=kb,
                a=a_mat, t=t_mat, uw=uw, u=uw[:, :, 0:HEAD], w=uw[:, :, HEAD:2 * HEAD], p=p_mat)


def _delta_specs(nh, tb, nb, reverse):
    tok = (lambda i: nb - 1 - i) if reverse else (lambda i: i)
    hw = nh * HEAD
    qkv = [pl.BlockSpec((tb, hw), functools.partial(lambda i, part: (tok(i), part), part=p)) for p in range(3)]
    rows = pl.BlockSpec((tb, hw), lambda i: (tok(i), 0))
    bg = pl.BlockSpec((tb, LANES), lambda i: (tok(i), 0))
    gct = pl.BlockSpec((2 * nh, tb), lambda i: (0, tok(i)))
    st = pl.BlockSpec((nh, tb // CHUNK, HEAD, HEAD), lambda i: (0, tok(i), 0, 0))
    tri = pl.BlockSpec((nh, tb // CHUNK, CHUNK, CHUNK), lambda i: (0, tok(i), 0, 0))
    return qkv, rows, bg, gct, st, tri


def _heads(ref, rows, nh):
    return jnp.stack([ref[rows, hd * HEAD:(hd + 1) * HEAD] for hd in range(nh)])


def _chunk_scalars(bg_ref, gct_ref, cj, nh, tb, reverse):
    rows = pl.ds(cj * CHUNK, CHUNK)
    lb = nh if reverse else 0
    lc = 4 * nh + lb
    last = cj * CHUNK + (0 if reverse else CHUNK - 1)
    g_lanes = gct_ref[lb:lb + nh, :]
    if cj:
        g_lanes = pltpu.roll(g_lanes, tb - cj * CHUNK, 1)
    col = lambda l0, r: jnp.stack([bg_ref[r, l0 + hd:l0 + hd + 1] for hd in range(nh)])
    return col(lb, rows), col(lc, rows), g_lanes[:, 0:CHUNK][:, None, :], col(lc, pl.ds(last, 1))


def _delta_fwd(name, qkvn, bg, gct, nh, reverse, tb=256):
    s = qkvn.shape[0]
    tb = _tile(s, tb, LANES)
    nb, cpb = s // tb, tb // CHUNK
    qkv, rows_spec, bg_spec, gct_spec, st, tri = _delta_specs(nh, tb, nb, reverse)

    def body(q_ref, k_ref, v_ref, bg_ref, gct_ref, o_ref, st_ref, tri_ref, state):
        @pl.when(pl.program_id(0) == 0)
        def _():
            state[...] = jnp.zeros_like(state)

        for cj in (range(cpb - 1, -1, -1) if reverse else range(cpb)):
            rows = pl.ds(cj * CHUNK, CHUNK)
            q, k, v = _heads(q_ref, rows, nh), _heads(k_ref, rows, nh), _heads(v_ref, rows, nh)
            tm = _chunk_terms(q, k, v, *_chunk_scalars(bg_ref, gct_ref, cj, nh, tb, reverse), reverse)
            s_in = state[...]
            st_ref[:, cj] = s_in
            tri_ref[:, cj] = tm["t"]
            ws_qs = _bnn(jnp.concatenate([tm["w"], q * tm["e"]], axis=1), s_in)
            vn = tm["u"] - ws_qs[:, 0:CHUNK]
            o = ws_qs[:, CHUNK:2 * CHUNK] + _bnn(tm["p"], vn)
            for hd in range(nh):
                o_ref[rows, hd * HEAD:(hd + 1) * HEAD] = o[hd]
            state[...] = s_in * tm["el"] + _btn(k * tm["ed"], vn)

    return pl.pallas_call(
        body, grid=(nb,), in_specs=qkv + [bg_spec, gct_spec], out_specs=[rows_spec, st, tri],
        out_shape=[SDS((s, nh * HEAD), F32), SDS((nh, s // CHUNK, HEAD, HEAD), F32),
                   SDS((nh, s // CHUNK, CHUNK, CHUNK), F32)],
        scratch_shapes=[pltpu.VMEM((nh, HEAD, HEAD), F32)],
        compiler_params=_params("arbitrary"), name=name)(qkvn, qkvn, qkvn, bg, gct)


def _delta_bwd(name, qkvn, bg, gct, do, states, tris, nh, reverse, add=None, tb=128, ex=None):
    s = qkvn.shape[0]
    tb = _tile(s, tb, LANES)
    nb, cpb = s // tb, tb // CHUNK
    qkv, rows_spec, bg_spec, gct_spec, st, tri = _delta_specs(nh, tb, nb, not reverse)
    n_add = 0 if add is None else 3
    host = _Hosted(ex, name, (nb,))

    def body(*refs):
        q_ref, k_ref, v_ref, bg_ref, gct_ref, do_ref, st_ref, tri_ref = refs[0:8]
        add_refs = refs[8:8 + n_add]
        (dq_ref, dk_ref, dv_ref, dbg_ref), (dstate,) = host.split(refs[8 + n_add:], 4)
        host.start()

        @pl.when(pl.program_id(0) == 0)
        def _():
            dstate[...] = jnp.zeros_like(dstate)

        ones = jnp.ones((nh, 2 * CHUNK, HEAD), BF16)
        row_id = lax.broadcasted_iota(jnp.int32, (CHUNK, 1), 0)
        rsum = lambda x: jnp.sum(x, axis=2, keepdims=True)
        for cj in (range(cpb) if reverse else range(cpb - 1, -1, -1)):
            rows = pl.ds(cj * CHUNK, CHUNK)
            q, k, v, d_o = (_heads(r, rows, nh) for r in (q_ref, k_ref, v_ref, do_ref))
            beta, gc, g_row, g_last = _chunk_scalars(bg_ref, gct_ref, cj, nh, tb, reverse)
            tm = _chunk_terms(q, k, v, beta, gc, g_row, g_last, reverse, t_mat=tri_ref[:, cj])
            incl, strict, e, ed, el, kb = tm["incl"], tm["strict"], tm["e"], tm["ed"], tm["el"], tm["kb"]
            t_mat, u, w, p_mat, decay = tm["t"], tm["u"], tm["w"], tm["p"], tm["decay"]
            s_in, ds_out = st_ref[:, cj], dstate[...]
            cat_rows = lambda a, b: jnp.concatenate([a, b], axis=1)
            top, bot = slice(0, CHUNK), slice(CHUNK, 2 * CHUNK)
            vn = u - _bnn(w, s_in)
            qe, kd, ke = q * e, k * ed, kb * e
            dvn = _btn(p_mat, d_o) + _bnn(kd, ds_out)
            by_state = _bnt(cat_rows(d_o, dvn), s_in)
            dqe, dw = by_state[:, top], -by_state[:, bot]
            dq = dqe * e
            dgc = rsum(dqe * qe)
            dp = jnp.where(incl, _bnt(d_o, vn), 0.0)
            dkd = _bnt(vn, ds_out)
            dk = dkd * ed
            r = rsum(dkd * kd)
            dgc = dgc - r
            dg_last = (jnp.sum(r, axis=1, keepdims=True)
                       + jnp.sum(rsum(ds_out * s_in), axis=1, keepdims=True) * el)
            d_uw = _btn(t_mat, jnp.concatenate([dvn, dw], axis=2))
            dbv, dke = d_uw[:, :, 0:HEAD], d_uw[:, :, HEAD:2 * HEAD]
            da = -jnp.where(strict, _bnt(d_uw, tm["uw"]), 0.0)
            mn = cat_rows(da * decay, dp * decay)
            by_k = _bnn(mn, k)
            dkb = by_k[:, top] + dke * e
            dq = dq + by_k[:, bot]
            dk = dk + _btn(mn, cat_rows(kb, q))
            g_mat = da * tm["a"] + dp * p_mat
            g_hi, g_mid, _ = _split3(g_mat)
            col = _btn(cat_rows(g_hi, g_mid), ones)[:, :, 0:1]
            dgc = dgc + rsum(g_mat) - col + rsum(dke * ke)
            dgc = dgc + jnp.where(row_id == (0 if reverse else CHUNK - 1), dg_last, 0.0)
            dv = dbv * beta
            dbeta = rsum(dbv * v) + rsum(dkb * k)
            dk = dk + dkb * beta
            dstate[...] = el * ds_out + _btn(cat_rows(qe, w), cat_rows(d_o, -dvn))
            lb = nh if reverse else 0
            for hd in range(nh):
                cols = slice(hd * HEAD, (hd + 1) * HEAD)
                extra = [a[rows, cols] for a in add_refs] if n_add else [0.0, 0.0, 0.0]
                dq_ref[rows, cols] = dq[hd] + extra[0]
                dk_ref[rows, cols] = dk[hd] + extra[1]
                dv_ref[rows, cols] = dv[hd] + extra[2]
                dbg_ref[rows, lb + hd:lb + hd + 1] = dbeta[hd]
                dbg_ref[rows, 2 * nh + lb + hd:2 * nh + lb + hd + 1] = dgc[hd]
        host.wait()

    out3 = SDS((s, nh * HEAD), F32)
    n_in = 8 + n_add
    out = pl.pallas_call(
        body, grid=(nb,),
        in_specs=qkv + [bg_spec, gct_spec, rows_spec, st, tri] + [rows_spec] * n_add + host.in_specs,
        out_specs=[rows_spec, rows_spec, rows_spec, bg_spec] + host.out_specs,
        out_shape=[out3, out3, out3, SDS((s, LANES), F32)] + host.out_shapes,
        input_output_aliases=host.aliases(n_in, 4), scratch_shapes=[pltpu.VMEM((nh, HEAD, HEAD), F32)] + host.sems,
        compiler_params=_params("arbitrary"), name=name)(
            qkvn, qkvn, qkvn, bg, gct, do, states, tris, *(add or ()), *host.arrays)
    return host.finish(out, 4)


def _row_pieces(g):
    return g.reshape(N_CHIPS, g.shape[0] // N_CHIPS, g.shape[1])


def _up_swiglu(name, u, w_up, tm=1024, ex=None):
    s, k = u.shape
    fh = w_up.shape[2]
    tm = _tile(s, tm, 8)
    grid = (2, s // tm)
    host = _Hosted(ex, name, grid)

    def body(u_ref, wa_ref, wb_ref, *rest):
        (a_ref, b_ref, o_ref), _ = host.split(rest, 3)
        host.start()
        x = u_ref[...]
        a, b = _nn(x, wa_ref[...]), _nn(x, wb_ref[...])
        a_ref[...], b_ref[...] = a.astype(BF16), b.astype(BF16)
        o_ref[...] = (jax.nn.silu(a) * b).astype(BF16)
        host.wait()

    tile = pl.BlockSpec((tm, fh), lambda j, i: (i, j))
    out = pl.pallas_call(
        body, grid=grid,
        in_specs=[pl.BlockSpec((tm, k), lambda j, i: (i, 0)), pl.BlockSpec((None, k, fh), lambda j, i: (j, 0, 0)),
                  pl.BlockSpec((None, k, fh), lambda j, i: (2 + j, 0, 0))] + host.in_specs,
        out_specs=[tile] * 3 + host.out_specs, out_shape=[SDS((s, 2 * fh), BF16)] * 3 + host.out_shapes,
        input_output_aliases=host.aliases(3, 3), scratch_shapes=host.sems,
        compiler_params=_params("arbitrary", "arbitrary"), name=name)(u, w_up, w_up, *host.arrays)
    return host.finish(out, 3)


def _matmul_resid(name, a, w, h, g, scale, tm=512, ex=None):
    s, k = a.shape
    d = w.shape[1]
    tm = _tile(s, tm, 16)
    grid = (s // tm,)
    host = _Hosted(ex, name, grid)

    def body(a_ref, w_ref, h_ref, g_ref, *rest):
        (o_ref, f_ref), _ = host.split(rest, 2)
        host.start()
        f = _nn(a_ref[...], w_ref[...])
        f_ref[...] = f.astype(BF16)
        o_ref[...] = h_ref[...] + (scale * g_ref[...]) * f
        host.wait()

    out = pl.pallas_call(
        body, grid=grid, in_specs=[_row(tm, k), _vec(k, d), _row(tm, d), _vec(1, d)] + host.in_specs,
        out_specs=[_row(tm, d), _row(tm, d)] + host.out_specs,
        out_shape=[SDS((s, d), F32), SDS((s, d), BF16)] + host.out_shapes,
        input_output_aliases=host.aliases(4, 2), scratch_shapes=host.sems,
        compiler_params=_params("arbitrary"), name=name)(a, w, h, g, *host.arrays)
    return host.finish(out, 2)


def _down_swiglu_bwd(name, df, w_down, a_pre, b_pre, tm=512):
    s, d = df.shape
    f = w_down.shape[0]
    tm = _tile(s, tm, 16)

    def body(df_ref, w_ref, a_ref, b_ref, o_ref):
        dhm = _nt(df_ref[...], w_ref[...])
        a = a_ref[...].astype(F32)
        o_ref[:, 0:f] = (dhm * b_ref[...].astype(F32) * _silu_grad(a)).astype(BF16)
        o_ref[:, f:2 * f] = (dhm * jax.nn.silu(a)).astype(BF16)

    return pl.pallas_call(
        body, grid=(s // tm,), in_specs=[_row(tm, d), _vec(f, d), _row(tm, f), _row(tm, f)],
        out_specs=_row(tm, 2 * f), out_shape=SDS((s, 2 * f), BF16),
        compiler_params=_params("parallel"), name=name)(df, w_down, a_pre, b_pre)


def _col_pieces(w):
    return w if w.ndim == 3 else w.reshape(w.shape[0], N_CHIPS, -1).transpose(1, 0, 2)


def _ffn_fwd(tag, h, nw, sh, sc, g, w_up, w_down, ex=None):
    u = _norm_mod(tag + "_norm", h, nw, sh, sc)
    a_pre, b_pre, hm = _up_swiglu(tag + "_up", u, _col_pieces(w_up), ex=ex)
    w_down = w_down() if callable(w_down) else w_down
    h_new, f = _matmul_resid(tag + "_down", hm, w_down, h, g, 0.5, ex=ex)
    return h_new, (h, u, a_pre, b_pre, hm, f, w_down)


def _ffn_bwd(tag, dh, saved, nw, sc, g, w_up, ex=None, on_gw_down=None):
    h, u, a_pre, b_pre, hm, f, w_down = saved
    df, acc_g = _resid_bwd(tag + "_res_bwd", dh, f, g, 0.5)
    gw_down = _row_pieces(_matmul(tag + "_gw_down", hm, df, "tn", out_dtype=BF16, tm=1408, ex=ex))
    if on_gw_down:
        on_gw_down(gw_down)
    dab = _down_swiglu_bwd(tag + "_dhm", df, w_down, a_pre, b_pre)
    gw_up = _matmul(tag + "_gw_up", u, dab, "tn", out_dtype=BF16, tn=1408, out_pieces=N_CHIPS, ex=ex)
    du = _matmul(tag + "_du", dab, w_up, "nt", ex=ex)
    dh_in, acc = _norm_mod_bwd(tag + "_norm_bwd", h, du, dh, nw, sc)
    return dh_in, gw_up, gw_down, (acc[0], acc[1], acc_g[0], acc[2])


def _mixer_fwd(h, nw, sh, sc, g, wt, lay, ex=None):
    nh = lay.nh
    u = _norm_mod("mix_norm", h, nw, sh, sc)
    proj = _matmul("mix_in", u, wt["w_in"], "nn", ex=ex)
    if ex:
        wt = dict(wt, **_as_operands(ex.weights("mix"), lay))
    qkvn = _prep_b("mix_prep_b", proj, wt["conv_dn"], lay)
    ya = _prep_a("mix_prep_a", proj, wt["conv_a"], lay)
    bg = _prep_c("mix_prep_c", proj, wt["pvec"], lay)
    gct = bg[:, 4 * nh:6 * nh].T
    o_f, *st_f = _delta_fwd("delta_fwd_l2r", qkvn, bg, gct, nh, False)
    o_b, *st_b = _delta_fwd("delta_fwd_r2l", qkvn, bg, gct, nh, True)
    yb = _post("mix_post", o_f, o_b, proj, wt["dn_norm"], lay)
    pa = _matmul("mix_a_out", ya, wt["w_a_out"], "nn")
    pb = _matmul("mix_b_out", yb, wt["w_b_out"], "nn")
    mg = _merge("mix_merge", pa, pb, proj, lay)
    h2, y = _matmul_resid("mix_out", mg, wt["w_out"], h, g, 1.0, ex=ex)
    return h2, (h, u, proj, qkvn, ya, bg, gct, o_f, o_b, st_f, st_b, yb, pa, pb, mg, y)


def _mixer_bwd(dh, saved, nw, sc, g, wt, lay, ex=None):
    h, u, proj, qkvn, ya, bg, gct, o_f, o_b, st_f, st_b, yb, pa, pb, mg, y = saved
    nh = lay.nh
    dy, acc_g = _resid_bwd("mix_res_bwd", dh, y, g, 1.0)
    gw_out = _matmul("mix_gw_out", mg, dy, "tn", out_dtype=BF16, ex=ex)
    dmg = _matmul("mix_dmg", dy, wt["w_out"], "nt")
    dpa, dpb, dproj = _merge_bwd("mix_merge_bwd", dmg, pa, pb, proj, lay)
    gw_a = _matmul("mix_gw_a", ya, dpa, "tn", out_dtype=BF16, out_pieces=N_CHIPS)
    gw_b = _matmul("mix_gw_b", yb, dpb, "tn", out_dtype=BF16)
    dya = _matmul("mix_dya", dpa, wt["w_a_out"], "nt")
    dyb = _matmul("mix_dyb", dpb, wt["w_b_out"], "nt")
    do, dproj, acc_dn = _post_bwd("mix_post_bwd", dyb, o_f, o_b, proj, wt["dn_norm"], dproj, lay)
    dq, dk, dv, dbg_f = _delta_bwd("delta_bwd_l2r", qkvn, bg, gct, do, *st_f, nh, False, ex=ex)
    dq, dk, dv, dbg_b = _delta_bwd("delta_bwd_r2l", qkvn, bg, gct, do, *st_b, nh, True, add=(dq, dk, dv), ex=ex)
    dproj, acc_ca = _prep_a_bwd("mix_prep_a_bwd", dya, proj, wt["conv_a"], dproj, lay)
    dproj, acc_cd = _prep_b_bwd("mix_prep_b_bwd", dq, dk, dv, proj, wt["conv_dn"], dproj, lay)
    dproj, acc_pc = _prep_c_bwd("mix_prep_c_bwd", dbg_f, dbg_b, proj, wt["pvec"], dproj, lay)
    gw_in = lay.unperm_cols(_matmul("mix_gw_in", u, dproj, "tn", out_dtype=BF16))
    gw_in = gw_in.reshape(gw_in.shape[0], N_CHIPS, -1).transpose(1, 0, 2)
    du = _matmul("mix_du", dproj, wt["w_in"], "nt")
    dh_in, acc = _norm_mod_bwd("mix_norm_bwd", h, du, dh, nw, sc)
    small = dict(conv_a=acc_ca[0:wt["conv_a"].shape[0]], conv_dn=acc_cd[0:wt["conv_dn"].shape[0]],
                 dn_norm=acc_dn[0:1], a_log=acc_pc[0], dt_bias=acc_pc[1])
    big = dict(w_in=gw_in, w_a_out=gw_a, w_b_out=_row_pieces(gw_b), w_out=_row_pieces(gw_out))
    return dh_in, big, small, (acc[0], acc[1], acc_g[0], acc[2])


def _as_operands(gathered, lay):
    wt = {}
    for n, g in gathered.items():
        if n == "w_in":
            wt[n] = lay.perm_cols(jnp.concatenate(list(g), axis=1))
        else:
            wt[n] = g if n in COL_SHARDED else g.reshape(-1, g.shape[-1])
    return wt


def _local_step(x, tgt, modv, wt, lay, ex=None):
    m = [modv[i:i + 1] for i in range(9)]

    def ffn1_down():
        ex.run(("gather_pass", "down1"))
        return _as_operands(ex.weights("down1"), lay)["w_ffn1_down"]

    h1, sv1 = _ffn_fwd("ffn1", x, wt["norm_ffn1"], m[0], m[1], m[2], wt["w_ffn1_up"],
                       ffn1_down if ex else wt["w_ffn1_down"], ex)
    if ex:
        wt = dict(wt, **_as_operands(ex.weights("in"), lay))
    h2, sv2 = _mixer_fwd(h1, wt["norm_mix"], m[3], m[4], m[5], wt, lay, ex)
    if ex:
        wt = dict(wt, **_as_operands(ex.weights("mix", "ffn2"), lay))
    h3, sv3 = _ffn_fwd("ffn2", h2, wt["norm_ffn2"], m[6], m[7], m[8], wt["w_ffn2_up"], wt["w_ffn2_down"])
    dh3, acc_f = _final_loss("final_loss", h3, tgt, wt["norm_final"])
    loss = jnp.sum(acc_f[1])
    dh2, gu2, gd2, dm3 = _ffn_bwd("ffn2", dh3, sv3, wt["norm_ffn2"], m[7], m[8], wt["w_ffn2_up"])
    if ex:
        ex.reduce("ffn2", dict(w_ffn2_up=gu2, w_ffn2_down=gd2))
    dh1, gmix, small, dm2 = _mixer_bwd(dh2, sv2, wt["norm_mix"], m[4], m[5], wt, lay, ex)
    if ex:
        ex.reduce("mixer", gmix)
    dx, gu1, gd1, dm1 = _ffn_bwd("ffn1", dh1, sv1, wt["norm_ffn1"], m[1], m[2], wt["w_ffn1_up"], ex,
                                 (lambda g: ex.reduce("down1", dict(w_ffn1_down=g))) if ex else None)
    dmod = jnp.stack([dm1[0], dm1[1], dm1[2], dm2[0], dm2[1], dm2[2], dm3[0], dm3[1], dm3[2]])
    big = dict(w_ffn1_up=gu1, w_ffn1_down=gd1, w_ffn2_up=gu2, w_ffn2_down=gd2, **gmix)
    small = dict(small, norm_ffn1=dm1[3], norm_mix=dm2[3], norm_ffn2=dm3[3], norm_final=acc_f[0])
    return loss, dx, dmod, big, small


def _position():
    return lax.axis_index("x"), lax.axis_index("y"), lax.axis_index("c")


_ANY = pl.BlockSpec(memory_space=pl.ANY)
_VMEM = pl.BlockSpec(memory_space=pltpu.VMEM)


def _allgather8(name, v):
    r = v.shape[0]

    def body(v_ref, out_ref, send_sems, recv_sems):
        x, y, c = _position()
        me = 4 * x + 2 * y + c
        out_ref[me] = v_ref[...]
        copies = []
        for mask in range(1, N_DEV):
            peer = tuple(1 - p if mask >> b & 1 else p for p, b in ((x, 2), (y, 1), (c, 0)))
            cp = pltpu.make_async_remote_copy(
                src_ref=v_ref, dst_ref=out_ref.at[me], send_sem=send_sems.at[mask - 1],
                recv_sem=recv_sems.at[mask - 1], device_id=peer, device_id_type=MESH)
            cp.start()
            copies.append(cp)
        for cp in copies:
            cp.wait()

    return pl.pallas_call(
        body, in_specs=[_VMEM], out_specs=_VMEM, out_shape=SDS((N_DEV, r, LANES), F32),
        scratch_shapes=[pltpu.SemaphoreType.DMA((N_DEV - 1,)), pltpu.SemaphoreType.DMA((N_DEV - 1,))],
        name=name)(v)


def _other_chips(x, y):
    return [(1 - x, y), (x, 1 - y), (1 - x, 1 - y)]


def _half_rows(c, rows):
    hr = rows // 2
    assert hr % 16 == 0
    return pl.ds(pl.multiple_of(c * hr, 16), hr)


class _Stage:
    def __init__(self, arrays, out_shapes, sems, plan, in_place=False):
        self.arrays, self.out_shapes, self.sems, self.plan = list(arrays), list(out_shapes), list(sems), plan
        self.alias_pairs = [(i, i) for i in range(len(self.arrays))] if in_place else []
        self.out_counts = [len(self.out_shapes)]

    def start(self, ins, outs, sems):
        for kind, cp in self.plan(ins, outs, sems):
            if kind != "recv":
                cp.start()

    def wait(self, ins, outs, sems):
        for kind, cp in self.plan(ins, outs, sems):
            {"local": cp.wait, "both": cp.wait, "send": cp.wait_send, "recv": cp.wait_recv}[kind]()

    def aliases(self, in_offset, out_offset):
        return {in_offset + i: out_offset + o for i, o in self.alias_pairs}


def _join(stages):
    ni = [0] + [len(st.arrays) for st in stages]
    no = [0] + [len(st.out_shapes) for st in stages]
    ns = [0] + [len(st.sems) for st in stages]
    for counts in (ni, no, ns):
        for k in range(1, len(counts)):
            counts[k] += counts[k - 1]

    def plan(ins, outs, sems):
        todo = []
        for k, st in enumerate(stages):
            todo += st.plan(ins[ni[k]:ni[k + 1]], outs[no[k]:no[k + 1]], sems[ns[k]:ns[k + 1]])
        return todo

    joined = _Stage([a for st in stages for a in st.arrays], [o for st in stages for o in st.out_shapes],
                    [m for st in stages for m in st.sems], plan)
    joined.alias_pairs = [(ni[k] + i, no[k] + o) for k, st in enumerate(stages) for i, o in st.alias_pairs]
    joined.out_counts = [len(st.out_shapes) for st in stages]
    return joined


def _run_stage(name, stage):
    n_in, n_out = len(stage.arrays), len(stage.out_shapes)

    def body(*refs):
        ins, outs, sems = refs[0:n_in], refs[n_in:n_in + n_out], refs[n_in + n_out:]
        stage.start(ins, outs, sems)
        stage.wait(ins, outs, sems)

    return pl.pallas_call(
        body, in_specs=[_ANY] * n_in, out_specs=[_ANY] * n_out, out_shape=stage.out_shapes,
        input_output_aliases=stage.aliases(0, 0), scratch_shapes=stage.sems, name=name)(*stage.arrays)


def _dma_sems(*counts):
    return [pltpu.SemaphoreType.DMA((n,)) for n in counts]


def _gather_send(shards):
    nw = len(shards)

    def plan(ins, outs, sems):
        send_sems, recv_sems, local_sems = sems
        x, y, c = _position()
        p = 2 * x + y
        todo = [("local", pltpu.make_async_copy(ins[w], outs[w].at[p], local_sems.at[w])) for w in range(nw)]
        for j, (cx, cy) in enumerate(_other_chips(x, y)):
            for w in range(nw):
                half = _half_rows(c, ins[w].shape[0])
                sem = dict(send_sem=send_sems.at[j * nw + w], recv_sem=recv_sems.at[j * nw + w], device_id_type=MESH)
                todo.append(("send", pltpu.make_async_remote_copy(
                    src_ref=ins[w].at[half], dst_ref=outs[w].at[p, half], device_id=(cx, cy, c), **sem)))
                landing = outs[w].at[2 * cx + cy, half]
                todo.append(("recv", pltpu.make_async_remote_copy(
                    src_ref=landing, dst_ref=landing, device_id=(x, y, c), **sem)))
        return todo

    return _Stage(shards, [SDS((N_CHIPS,) + v.shape, v.dtype) for v in shards], _dma_sems(3 * nw, 3 * nw, nw), plan)


def _gather_pass(gathered):
    nw = len(gathered)

    def plan(ins, outs, sems):
        send_sems, recv_sems = sems
        x, y, c = _position()
        todo = []
        for j, (cx, cy) in enumerate(_other_chips(x, y)):
            for w in range(nw):
                sem = dict(send_sem=send_sems.at[j * nw + w], recv_sem=recv_sems.at[j * nw + w], device_id_type=MESH)
                mine = outs[w].at[2 * cx + cy, _half_rows(c, outs[w].shape[1])]
                theirs = outs[w].at[2 * cx + cy, _half_rows(1 - c, outs[w].shape[1])]
                todo.append(("send", pltpu.make_async_remote_copy(
                    src_ref=mine, dst_ref=mine, device_id=(x, y, 1 - c), **sem)))
                todo.append(("recv", pltpu.make_async_remote_copy(
                    src_ref=theirs, dst_ref=theirs, device_id=(x, y, c), **sem)))
        return todo

    return _Stage(gathered, [SDS(g.shape, g.dtype) for g in gathered], _dma_sems(3 * nw, 3 * nw), plan, in_place=True)


def _swap_halves(gs):
    nw = len(gs)

    def plan(ins, outs, sems):
        x, y, c = _position()
        return [("both", pltpu.make_async_remote_copy(
            src_ref=ins[w].at[:, _half_rows(1 - c, ins[w].shape[1])], dst_ref=outs[w], send_sem=sems[0].at[w],
            recv_sem=sems[1].at[w], device_id=(x, y, 1 - c), device_id_type=MESH)) for w in range(nw)]

    return _Stage(gs, [SDS((g.shape[0], g.shape[1] // 2, g.shape[2]), g.dtype) for g in gs], _dma_sems(nw, nw), plan)


def _scatter_chips(vs):
    nw = len(vs)

    def plan(ins, outs, sems):
        x, y, c = _position()
        return [("both", pltpu.make_async_remote_copy(
            src_ref=ins[w].at[2 * cx + cy], dst_ref=outs[w].at[j], send_sem=sems[0].at[j * nw + w],
            recv_sem=sems[1].at[j * nw + w], device_id=(cx, cy, c), device_id_type=MESH))
            for j, (cx, cy) in enumerate(_other_chips(x, y)) for w in range(nw)]

    return _Stage(vs, [SDS((N_CHIPS - 1,) + v.shape[1:], v.dtype) for v in vs], _dma_sems(3 * nw, 3 * nw), plan)


def _share_halves(fulls):
    nw = len(fulls)

    def plan(ins, outs, sems):
        x, y, c = _position()
        todo = []
        for w in range(nw):
            rows = outs[w].at[_half_rows(c, outs[w].shape[0])]
            todo.append(("both", pltpu.make_async_remote_copy(
                src_ref=rows, dst_ref=rows, send_sem=sems[0].at[w], recv_sem=sems[1].at[w],
                device_id=(x, y, 1 - c), device_id_type=MESH)))
        return todo

    return _Stage(fulls, [SDS(f.shape, f.dtype) for f in fulls], _dma_sems(nw, nw), plan, in_place=True)


class _Hosted:
    def __init__(self, ex, name, grid):
        self.ex, self.name, self.grid = ex, name, grid
        self.stage = ex.host(name) if ex is not None else None
        st = self.stage
        self.arrays = list(st.arrays) if st else []
        self.out_shapes = list(st.out_shapes) if st else []
        self.sems = list(st.sems) if st else []
        self.in_specs, self.out_specs = [_ANY] * len(self.arrays), [_ANY] * len(self.out_shapes)

    def aliases(self, in_offset, out_offset):
        return self.stage.aliases(in_offset, out_offset) if self.stage else {}

    def split(self, rest, n_out):
        ni, no, ns = len(self.arrays), len(self.out_shapes), len(self.sems)
        self.ins, self.outs = rest[0:ni], rest[ni + n_out:ni + n_out + no]
        tail = rest[ni + n_out + no:]
        self.sem_refs = tail[len(tail) - ns:]
        return rest[ni:ni + n_out], tail[0:len(tail) - ns]

    def _at(self, last):
        conds = [pl.program_id(d) == (g - 1 if last else 0) for d, g in enumerate(self.grid)]
        return functools.reduce(lambda p, q: p & q, conds)

    def start(self):
        if self.stage:
            pl.when(self._at(False))(lambda: self.stage.start(self.ins, self.outs, self.sem_refs))

    def wait(self):
        if self.stage:
            pl.when(self._at(True))(lambda: self.stage.wait(self.ins, self.outs, self.sem_refs))

    def finish(self, out, n_out):
        out = list(out)
        if self.stage:
            self.ex.done(self.name, out[n_out:])
        return out[0:n_out]


GATHER = {"up1": ("w_ffn1_up",), "down1": ("w_ffn1_down",), "in": ("w_in",),
          "mix": ("w_a_out", "w_b_out", "w_out"), "ffn2": ("w_ffn2_up", "w_ffn2_down")}
REDUCE = {"ffn2": ("w_ffn2_up", "w_ffn2_down"), "mixer": ("w_in", "w_a_out", "w_b_out", "w_out"),
          "down1": ("w_ffn1_down",), "up1": ("w_ffn1_up",)}
HOSTS = {"ffn1_up": [("gather_send", "down1"), ("gather_send", "in")],
         "ffn1_down": [("gather_pass", "in"), ("gather_send", "mix")],
         "mix_in": [("gather_pass", "mix"), ("gather_send", "ffn2")],
         "mix_out": [("gather_pass", "ffn2")],
         "mix_gw_out": [("swap", "ffn2")], "delta_bwd_l2r": [("scatter", "ffn2")], "delta_bwd_r2l": [("share", "ffn2")],
         "ffn1_gw_down": [("swap", "mixer")], "ffn1_gw_up": [("scatter", "mixer"), ("swap", "down1")],
         "ffn1_du": [("share", "mixer"), ("scatter", "down1")]}


class _Exchange:
    def __init__(self, shards):
        self.shards = shards
        self.gathered, self.red, self.ready, self.reduced = {}, {}, {}, {}

    def _stage(self, kind, group):
        if kind == "gather_send":
            return _gather_send([self.shards[n] for n in GATHER[group]])
        if kind == "gather_pass":
            return _gather_pass(self.gathered[group])
        st = self.red[group]
        return {"swap": lambda: _swap_halves(st["parts"]), "scatter": lambda: _scatter_chips(st["chip_sums"]),
                "share": lambda: _share_halves(st["fulls"])}[kind]()

    def _done(self, kind, group, outs):
        if kind == "gather_send":
            self.gathered[group] = list(outs)
        elif kind == "gather_pass":
            self.ready.update(zip(GATHER[group], outs))
        elif kind == "swap":
            st = self.red[group]
            st["from_sib"] = list(outs)
            st["chip_sums"] = [_chip_sum("chip_sum_" + n, g, f) for n, g, f in zip(REDUCE[group], st["parts"], outs)]
        elif kind == "scatter":
            st = self.red[group]
            st["fulls"] = [_total("total_" + n, g, f, r)
                           for n, g, f, r in zip(REDUCE[group], st["parts"], st["from_sib"], outs)]
        else:
            self.reduced.update(zip(REDUCE[group], outs))

    def _all(self, steps, runner):
        stage = _join([self._stage(kind, group) for kind, group in steps])
        outs = list(runner(stage))
        for (kind, group), n in zip(steps, stage.out_counts):
            self._done(kind, group, outs[0:n])
            outs = outs[n:]

    def host(self, kernel_name):
        return _join([self._stage(*step) for step in HOSTS[kernel_name]]) if kernel_name in HOSTS else None

    def done(self, kernel_name, outs):
        self._all(HOSTS[kernel_name], lambda stage: outs)

    def run(self, *steps):
        name = "_".join(kind + "_" + group for kind, group in steps)
        self._all(steps, lambda stage: _run_stage(name, stage))

    def weights(self, *groups):
        return {n: self.ready[n] for g in groups for n in GATHER[g]}

    def reduce(self, group, parts):
        self.red[group] = dict(parts=[parts[n] for n in REDUCE[group]])


def _chip_sum(name, g, from_sib):
    _, r, cdim = g.shape
    hr = r // 2

    def body(g_ref, s_ref, o_ref):
        o_ref[...] = (g_ref[...].astype(F32) + s_ref[...].astype(F32)).astype(o_ref.dtype)

    blk = (None, hr, cdim)
    return pl.pallas_call(
        body, grid=(N_CHIPS,),
        in_specs=[pl.BlockSpec(blk, lambda j: (j, lax.axis_index("c"), 0)), pl.BlockSpec(blk, lambda j: (j, 0, 0))],
        out_specs=pl.BlockSpec(blk, lambda j: (j, 0, 0)),
        out_shape=SDS((N_CHIPS, hr, cdim), g.dtype), compiler_params=_params("parallel"), name=name)(g, from_sib)


def _total(name, g, from_sib, from_chips):
    _, r, cdim = g.shape
    hr = r // 2
    tr = _tile(hr, 256, 16)
    nt = hr // tr

    def body(g_ref, s_ref, rc_ref, o_ref):
        acc = g_ref[...].astype(F32) + s_ref[...].astype(F32)
        for j in range(N_CHIPS - 1):
            acc = acc + rc_ref[j].astype(F32)
        o_ref[...] = acc

    def chip():
        return 2 * lax.axis_index("x") + lax.axis_index("y")

    blk = (None, tr, cdim)
    return pl.pallas_call(
        body, grid=(nt,),
        in_specs=[pl.BlockSpec(blk, lambda i: (chip(), lax.axis_index("c") * nt + i, 0)),
                  pl.BlockSpec(blk, lambda i: (chip(), i, 0)),
                  pl.BlockSpec((N_CHIPS - 1, tr, cdim), lambda i: (0, i, 0))],
        out_specs=pl.BlockSpec((tr, cdim), lambda i: (lax.axis_index("c") * nt + i, 0)),
        out_shape=SDS((r, cdim), F32), compiler_params=_params("parallel"), name=name)(g, from_sib, from_chips)


def _sum8(name, v):
    _, r, w = v.shape

    def body(v_ref, o_ref):
        acc = v_ref[0]
        for j in range(1, N_DEV):
            acc = acc + v_ref[j]
        o_ref[...] = acc

    return pl.pallas_call(body, in_specs=[_VMEM], out_specs=_VMEM, out_shape=SDS((r, w), F32), name=name)(v)


def _adamw(name, w, g, m, v, tr=256):
    r, cdim = w.shape
    tr = _tile(r, tr, 8)
    bc1, bc2 = 1.0 - ADAM_B1 ** ADAM_STEP, 1.0 - ADAM_B2 ** ADAM_STEP

    def body(w_ref, g_ref, m_ref, v_ref, d_ref, nm_ref, nv_ref):
        g = g_ref[...]
        m2 = ADAM_B1 * m_ref[...] + (1.0 - ADAM_B1) * g
        v2 = ADAM_B2 * v_ref[...] + (1.0 - ADAM_B2) * (g * g)
        d_ref[...] = -ADAM_LR * ((m2 / bc1) / (jnp.sqrt(v2 / bc2) + ADAM_EPS) + ADAM_WD * w_ref[...])
        nm_ref[...] = m2
        nv_ref[...] = v2

    spec = _row(tr, cdim)
    out = SDS((r, cdim), F32)
    return pl.pallas_call(body, grid=(r // tr,), in_specs=[spec] * 4, out_specs=[spec] * 3, out_shape=[out] * 3,
                          compiler_params=_params("parallel"), name=name)(w, g, m, v)


def _pack_rows(arrays, width, row_mult, dtype):
    parts, spans, row = [], [], 0
    for a in arrays:
        n = a.size
        rows = -(-n // width)
        flat = a.reshape(-1).astype(dtype)
        if rows * width != n:
            flat = jnp.concatenate([flat, jnp.zeros((rows * width - n,), dtype)])
        parts.append(flat.reshape(rows, width))
        spans.append((row, rows, n, a.shape))
        row += rows
    pad = -row % row_mult
    if pad:
        parts.append(jnp.zeros((pad, width), dtype))
    return jnp.concatenate(parts, axis=0), spans


def _unpack_rows(packed, spans):
    return [packed[r0:r0 + rows].reshape(-1)[0:n].reshape(shape) for r0, rows, n, shape in spans]


BIG = ("w_ffn1_up", "w_ffn1_down", "w_in", "w_a_out", "w_b_out", "w_out", "w_ffn2_up", "w_ffn2_down")
COL_SHARDED = ("w_ffn1_up", "w_in", "w_a_out", "w_ffn2_up")
SMALL = ("b_ada", "norm_ffn1", "norm_mix", "conv_a", "conv_dn", "a_log_fwd", "dt_bias_fwd", "a_log_bwd",
         "dt_bias_bwd", "dn_norm", "norm_ffn2", "norm_final")
WEIGHTS = ("w_ada", "b_ada", "norm_ffn1", "w_ffn1_up", "w_ffn1_down", "norm_mix", "w_in", "conv_a", "conv_dn",
           "a_log_fwd", "dt_bias_fwd", "a_log_bwd", "dt_bias_bwd", "dn_norm", "w_a_out", "w_b_out", "w_out",
           "norm_ffn2", "w_ffn2_up", "w_ffn2_down", "norm_final")


def kernel(x, c, w_ada, b_ada, norm_ffn1, w_ffn1_up, w_ffn1_down, norm_mix, w_in, conv_a, conv_dn, a_log_fwd, dt_bias_fwd, a_log_bwd, dt_bias_bwd, dn_norm, w_a_out, w_b_out, w_out, norm_ffn2, w_ffn2_up, w_ffn2_down, norm_final, loss_target, m_w_ada, m_b_ada, m_norm_ffn1, m_w_ffn1_up, m_w_ffn1_down, m_norm_mix, m_w_in, m_conv_a, m_conv_dn, m_a_log_fwd, m_dt_bias_fwd, m_a_log_bwd, m_dt_bias_bwd, m_dn_norm, m_w_a_out, m_w_b_out, m_w_out, m_norm_ffn2, m_w_ffn2_up, m_w_ffn2_down, m_norm_final, v_w_ada, v_b_ada, v_norm_ffn1, v_w_ffn1_up, v_w_ffn1_down, v_norm_mix, v_w_in, v_conv_a, v_conv_dn, v_a_log_fwd, v_dt_bias_fwd, v_a_log_bwd, v_dt_bias_bwd, v_dn_norm, v_w_a_out, v_w_b_out, v_w_out, v_norm_ffn2, v_w_ffn2_up, v_w_ffn2_down, v_norm_final):
    given = dict(locals())
    wsh = {n: given[n] for n in WEIGHTS}
    msh = {n: given["m_" + n] for n in WEIGHTS}
    vsh = {n: given["v_" + n] for n in WEIGHTS}
    d = x.shape[-1]
    ca = conv_a.shape[-1] * N_CHIPS
    nh = conv_dn.shape[-1] * N_CHIPS // (3 * HEAD)
    lay = _Layout(d, ca, nh)
    xi, yi, ci = _position()
    chip = 2 * xi + yi
    me = 2 * chip + ci

    c_act = jax.nn.silu(c)
    g1, g1_spans = _pack_rows([c_act, conv_a[0], conv_dn[0]], LANES, 8, F32)
    g1_all = _allgather8("gather_cond", g1)
    per_dev = [_unpack_rows(g1_all[k], g1_spans) for k in range(N_DEV)]
    c_all = jnp.concatenate([p[0] for p in per_dev], axis=0)
    conv_a_full = jnp.concatenate([per_dev[2 * k][1] for k in range(N_CHIPS)], axis=1)
    conv_dn_full = jnp.concatenate([per_dev[2 * k][2] for k in range(N_CHIPS)], axis=1)

    mod_sh = _matmul("ada_mod", c_all, w_ada[0], "nn")
    b_sh = lax.dynamic_slice_in_dim(b_ada, chip * mod_sh.shape[1], mod_sh.shape[1], axis=1)
    g2, g2_spans = _pack_rows([mod_sh + b_sh], LANES, 8, F32)
    g2_all = _allgather8("gather_mod", g2)
    mod_all = jnp.concatenate([_unpack_rows(g2_all[2 * k], g2_spans)[0] for k in range(N_CHIPS)], axis=1)
    modv = lax.dynamic_index_in_dim(mod_all, me, 0, keepdims=False).reshape(9, d)

    ex = _Exchange({n: wsh[n][0].astype(BF16) for n in BIG})
    ex.run(("gather_send", "up1"))
    ex.run(("gather_pass", "up1"))
    wt = _as_operands(ex.weights("up1"), lay)
    lane_pad = (jnp.zeros((2 * nh,), F32), jnp.zeros((LANES - 4 * nh,), F32))
    pvec = jnp.stack([jnp.concatenate([lane_pad[0], a_log_fwd[0], a_log_bwd[0], lane_pad[1]]),
                      jnp.concatenate([lane_pad[0], dt_bias_fwd[0], dt_bias_bwd[0], lane_pad[1]])]
                     + [jnp.zeros((LANES,), F32)] * 6)
    wt.update(conv_a=conv_a_full, conv_dn=conv_dn_full, pvec=pvec, dn_norm=dn_norm, norm_ffn1=norm_ffn1,
              norm_mix=norm_mix, norm_ffn2=norm_ffn2, norm_final=norm_final.reshape(1, d))

    loss, dx, dmod, big, small = _local_step(x[0], loss_target[0], modv, wt, lay, ex)
    loss = lax.psum(loss, ("x", "y", "c"))

    small_list = [dmod.reshape(1, 9 * d), small["norm_ffn1"], small["norm_mix"], small["conv_a"], small["conv_dn"],
                  small["a_log"][2 * nh:3 * nh], small["dt_bias"][2 * nh:3 * nh], small["a_log"][3 * nh:4 * nh],
                  small["dt_bias"][3 * nh:4 * nh], small["dn_norm"], small["norm_ffn2"], small["norm_final"]]
    g3, g3_spans = _pack_rows(small_list, LANES, 8, F32)
    g3_all = _allgather8("gather_small_grads", g3)
    g_small = dict(zip(SMALL, _unpack_rows(_sum8("sum_small_grads", g3_all), g3_spans)))
    dmod_all = jnp.concatenate([_unpack_rows(g3_all[k], g3_spans)[0] for k in range(N_DEV)], axis=0)
    ncol = w_ada.shape[-1]
    dmod_sh = lax.dynamic_slice_in_dim(dmod_all, chip * ncol, ncol, axis=1)
    grads = {"w_ada": _matmul("ada_grad", c_all, dmod_sh, "tn")[None]}
    for n in SMALL:
        g = g_small[n]
        if n in ("conv_a", "conv_dn"):
            wloc = wsh[n].shape[-1]
            g = lax.dynamic_slice_in_dim(g, chip * wloc, wloc, axis=1)
        grads[n] = g.reshape(wsh[n].shape)

    ex.reduce("up1", big)
    ex.run(("share", "down1"), ("swap", "up1"))
    ex.run(("scatter", "up1"))
    ex.run(("share", "up1"))
    for n in BIG:
        grads[n] = ex.reduced[n][None]

    delta, new_m, new_v = {}, {}, {}
    for n in ("w_ada",) + BIG:
        shp = wsh[n].shape
        outs = _adamw("adamw_" + n, *(t.reshape(shp[-2], shp[-1]) for t in (wsh[n], grads[n], msh[n], vsh[n])))
        delta[n], new_m[n], new_v[n] = (o.reshape(shp) for o in outs)
    packed = []
    for src in (wsh, grads, msh, vsh):
        pk, s_spans = _pack_rows([src[n] for n in SMALL], LANES, 8, F32)
        packed.append(pk)
    outs = _adamw("adamw_small", *packed)
    for dst, o in zip((delta, new_m, new_v), outs):
        dst.update(zip(SMALL, _unpack_rows(o, s_spans)))

    return (loss, dx[None], *[grads[n] for n in WEIGHTS], *[delta[n] for n in WEIGHTS],
            *[new_m[n] for n in WEIGHTS], *[new_v[n] for n in WEIGHTS])
```

```python
import functools

import jax
import jax.numpy as jnp
from jax import lax
from jax.experimental import pallas as pl
from jax.experimental.pallas import tpu as pltpu

F32 = jnp.float32
BF16 = jnp.bfloat16
SDS = jax.ShapeDtypeStruct
MESH = pl.DeviceIdType.MESH
HI = lax.Precision.HIGHEST

EPS = 1e-6
HEAD = 128
CHUNK = 64
LANES = 128
N_CHIPS = 4
N_DEV = 8
VMEM_LIMIT = 56 * 1024 * 1024
FULL_K = 3072

ADAM_LR = 0.001
ADAM_B1 = 0.9
ADAM_B2 = 0.999
ADAM_EPS = 1e-08
ADAM_WD = 0.01
ADAM_STEP = 10


def _params(*sem):
    return pltpu.CompilerParams(dimension_semantics=sem, vmem_limit_bytes=VMEM_LIMIT)


def _tile(n, cap, mult=LANES):
    t = min(n, cap) // mult * mult
    while t >= mult:
        if n % t == 0:
            return t
        t -= mult
    return n


def _row(tr, w, cb=0):
    return pl.BlockSpec((tr, w), lambda i: (i, cb))


def _vec(r, w):
    return pl.BlockSpec((r, w), lambda i: (0, 0))


def _nn(a, b, **kw):
    return jnp.dot(a, b, preferred_element_type=F32, **kw)


def _nt(a, b, **kw):
    return lax.dot_general(a, b, (((1,), (1,)), ((), ())), preferred_element_type=F32, **kw)


def _tn(a, b, **kw):
    return lax.dot_general(a, b, (((0,), (0,)), ((), ())), preferred_element_type=F32, **kw)


def _bnn(a, b):
    return lax.dot_general(a, b, (((2,), (1,)), ((0,), (0,))), preferred_element_type=F32)


def _bnt(a, b):
    return lax.dot_general(a, b, (((2,), (2,)), ((0,), (0,))), preferred_element_type=F32)


def _btn(a, b):
    return lax.dot_general(a, b, (((1,), (1,)), ((0,), (0,))), preferred_element_type=F32)


def _silu_grad(x):
    s = jax.nn.sigmoid(x)
    return s * (1.0 + x * (1.0 - s))


def _matmul(name, a, b, mode, out_dtype=F32, tm=1024, tn=1024, tk=2048, out_pieces=0, ex=None):
    pieces_b = b.shape[0] if b.ndim == 3 else 0
    b2 = b.shape[1:] if pieces_b else b.shape
    if mode == "nn":
        (m, k), n = a.shape, b2[1] * max(pieces_b, 1)
    elif mode == "nt":
        (m, _), n, k = a.shape, b2[0], b2[1] * max(pieces_b, 1)
    else:
        (k, m), n = a.shape, b2[1] * max(pieces_b, 1)
    n_unit = n // max(out_pieces, 1) if mode == "nt" or not pieces_b else n // pieces_b
    if out_pieces and pieces_b and mode != "nt":
        assert out_pieces == pieces_b
    k_unit = k // pieces_b if (pieces_b and mode == "nt") else k
    tm, tn = _tile(m, tm), _tile(n_unit, tn)
    tk = k_unit if k_unit <= FULL_K else _tile(k_unit, tk)
    nk = k // tk
    n_per, k_per = n_unit // tn, k_unit // tk
    a_bytes, b_bytes = a.size * a.dtype.itemsize, b.size * b.dtype.itemsize
    j_outer = nk == 1 and b_bytes + a_bytes * (n // tn) < a_bytes + b_bytes * (m // tm)
    ij = (lambda g0, g1: (g1, g0)) if j_outer else (lambda g0, g1: (g0, g1))

    def spec(shape, pick):
        return pl.BlockSpec(shape, lambda g0, g1, l: pick(*ij(g0, g1), l))

    a_spec = {"nn": spec((tm, tk), lambda i, j, l: (i, l)), "nt": spec((tm, tk), lambda i, j, l: (i, l)),
              "tn": spec((tk, tm), lambda i, j, l: (l, i))}[mode]
    if not pieces_b:
        b_spec = {"nn": spec((tk, tn), lambda i, j, l: (l, j)), "nt": spec((tn, tk), lambda i, j, l: (j, l)),
                  "tn": spec((tk, tn), lambda i, j, l: (l, j))}[mode]
    elif mode == "nt":
        b_spec = spec((None, tn, tk), lambda i, j, l: (l // k_per, j, l % k_per))
    else:
        b_spec = spec((None, tk, tn), lambda i, j, l: (j // n_per, l, j % n_per))
    if out_pieces:
        o_spec = spec((None, tm, tn), lambda i, j, l: (j // n_per, i, j % n_per))
        o_shape = SDS((out_pieces, m, n // out_pieces), out_dtype)
    else:
        o_spec, o_shape = spec((tm, tn), lambda i, j, l: (i, j)), SDS((m, n), out_dtype)
    dot = {"nn": _nn, "nt": _nt, "tn": _tn}[mode]
    grid = (n // tn, m // tm, nk) if j_outer else (m // tm, n // tn, nk)
    host = _Hosted(ex, name, grid)

    def body(a_ref, b_ref, *rest):
        (o_ref,), scratch = host.split(rest, 1)
        host.start()
        part = dot(a_ref[...].astype(BF16), b_ref[...].astype(BF16))
        if nk == 1:
            o_ref[...] = part.astype(o_ref.dtype)
        else:
            l, acc = pl.program_id(2), scratch[0]

            @pl.when(l == 0)
            def _():
                acc[...] = part

            @pl.when((l > 0) & (l < nk - 1))
            def _():
                acc[...] += part

            @pl.when(l == nk - 1)
            def _():
                o_ref[...] = (acc[...] + part).astype(o_ref.dtype)
        host.wait()

    out = pl.pallas_call(
        body, grid=grid, in_specs=[a_spec, b_spec] + host.in_specs, out_specs=[o_spec] + host.out_specs,
        out_shape=[o_shape] + host.out_shapes, input_output_aliases=host.aliases(2, 1),
        scratch_shapes=([] if nk == 1 else [pltpu.VMEM((tm, tn), F32)]) + host.sems,
        compiler_params=_params(*(("arbitrary",) * 3 if host.stage else ("parallel", "parallel", "arbitrary"))),
        name=name)(a, b, *host.arrays)
    return host.finish(out, 1)[0]


def _norm_mod(name, h, nw, sh, sc, tr=512):
    s, d = h.shape
    tr = _tile(s, tr, 8)

    def body(h_ref, nw_ref, sh_ref, sc_ref, u_ref):
        x = h_ref[...]
        r = lax.rsqrt(jnp.mean(x * x, axis=-1, keepdims=True) + EPS)
        u_ref[...] = (x * r * nw_ref[...] * (1.0 + sc_ref[...]) + sh_ref[...]).astype(BF16)

    return pl.pallas_call(
        body, grid=(s // tr,), in_specs=[_row(tr, d), _vec(1, d), _vec(1, d), _vec(1, d)],
        out_specs=_row(tr, d), out_shape=SDS((s, d), BF16),
        compiler_params=_params("parallel"), name=name)(h, nw, sh, sc)


def _norm_mod_bwd(name, h, du, dh, nw, sc, tr=512):
    s, d = h.shape
    tr = _tile(s, tr, 8)

    def body(h_ref, du_ref, dh_ref, nw_ref, sc_ref, o_ref, acc_ref):
        @pl.when(pl.program_id(0) == 0)
        def _():
            acc_ref[...] = jnp.zeros_like(acc_ref)

        x, g = h_ref[...], du_ref[...]
        r = lax.rsqrt(jnp.mean(x * x, axis=-1, keepdims=True) + EPS)
        n = x * r
        nw, sc1 = nw_ref[...], 1.0 + sc_ref[...]
        dn = g * sc1 * nw
        o_ref[...] = dh_ref[...] + r * (dn - n * jnp.mean(dn * n, axis=-1, keepdims=True))
        gn = g * n
        acc_ref[0:1, :] += jnp.sum(g, axis=0, keepdims=True)
        acc_ref[1:2, :] += jnp.sum(gn * nw, axis=0, keepdims=True)
        acc_ref[2:3, :] += jnp.sum(gn * sc1, axis=0, keepdims=True)

    return pl.pallas_call(
        body, grid=(s // tr,),
        in_specs=[_row(tr, d), _row(tr, d), _row(tr, d), _vec(1, d), _vec(1, d)],
        out_specs=[_row(tr, d), _vec(8, d)], out_shape=[SDS((s, d), F32), SDS((8, d), F32)],
        compiler_params=_params("arbitrary"), name=name)(h, du, dh, nw, sc)


def _resid_bwd(name, dh, f, g, scale, tr=512):
    s, d = dh.shape
    tr = _tile(s, tr, 8)

    def body(dh_ref, f_ref, g_ref, o_ref, acc_ref):
        @pl.when(pl.program_id(0) == 0)
        def _():
            acc_ref[...] = jnp.zeros_like(acc_ref)

        x = dh_ref[...]
        o_ref[...] = ((scale * g_ref[...]) * x).astype(BF16)
        acc_ref[0:1, :] += jnp.sum(scale * x * f_ref[...], axis=0, keepdims=True)

    return pl.pallas_call(
        body, grid=(s // tr,), in_specs=[_row(tr, d), _row(tr, d), _vec(1, d)],
        out_specs=[_row(tr, d), _vec(8, d)], out_shape=[SDS((s, d), BF16), SDS((8, d), F32)],
        compiler_params=_params("arbitrary"), name=name)(dh, f, g)


def _final_loss(name, h, tgt, nw, tr=512):
    s, d = h.shape
    tr = _tile(s, tr, 8)

    def body(h_ref, t_ref, nw_ref, o_ref, acc_ref):
        @pl.when(pl.program_id(0) == 0)
        def _():
            acc_ref[...] = jnp.zeros_like(acc_ref)

        x, nw = h_ref[...], nw_ref[...]
        r = lax.rsqrt(jnp.mean(x * x, axis=-1, keepdims=True) + EPS)
        n = x * r
        diff = n * nw - t_ref[...]
        dy = diff * (1.0 / d)
        dn = dy * nw
        o_ref[...] = r * (dn - n * jnp.mean(dn * n, axis=-1, keepdims=True))
        acc_ref[0:1, :] += jnp.sum(dy * n, axis=0, keepdims=True)
        acc_ref[1:2, :] += jnp.sum(diff * diff, axis=0, keepdims=True) * (0.5 / d)

    return pl.pallas_call(
        body, grid=(s // tr,), in_specs=[_row(tr, d), _row(tr, d), _vec(1, d)],
        out_specs=[_row(tr, d), _vec(8, d)], out_shape=[SDS((s, d), F32), SDS((8, d), F32)],
        compiler_params=_params("arbitrary"), name=name)(h, tgt, nw)


class _Layout:
    def __init__(self, d, ca, nh):
        self.d, self.ca, self.nh = d, ca, nh
        self.qk = nh * HEAD
        self.qkv = 3 * self.qk
        self.z = self.qkv
        self.ga = self.z + self.qk
        self.cab = self.ga + 2 * d
        self.ba = self.cab + 3 * ca
        self.tail = _tile(self.ba, 512)
        self.total = self.ba + self.tail
        assert self.qkv % self.qk == 0 and self.ga % (2 * d) == 0 and self.cab % (3 * ca) == 0
        assert self.ba % self.tail == 0 and 4 * nh <= LANES

    def perm_cols(self, w):
        ca, qkv, qk, d, nh = self.ca, self.qkv, self.qk, self.d, self.nh
        o = [0, ca, 2 * ca, 3 * ca, 3 * ca + qkv, 3 * ca + qkv + qk, 3 * ca + qkv + qk + 4 * nh]
        cb, cc, cv = (w[..., o[i]:o[i + 1]] for i in range(3))
        x_qkv, x_z, x_ba = w[..., o[3]:o[4]], w[..., o[4]:o[5]], w[..., o[5]:o[6]]
        gates = w[..., o[6]:o[6] + 2 * d]
        pad = jnp.zeros(w.shape[:-1] + (self.tail - 4 * nh,), w.dtype)
        return jnp.concatenate([x_qkv, x_z, gates, cb, cc, cv, x_ba, pad], axis=-1)

    def unperm_cols(self, w):
        ca, nh = self.ca, self.nh
        cb, cc, cv = (w[..., self.cab + i * ca:self.cab + (i + 1) * ca] for i in range(3))
        return jnp.concatenate([cb, cc, cv, w[..., 0:self.qkv], w[..., self.z:self.ga],
                                w[..., self.ba:self.ba + 4 * nh], w[..., self.ga:self.cab]], axis=-1)


def _halo_specs(tr, w, cb, s):
    nb8 = s // 8
    return [pl.BlockSpec((8, w), lambda i: (jnp.maximum(i * (tr // 8) - 1, 0), cb)),
            pl.BlockSpec((tr, w), lambda i: (i, cb)),
            pl.BlockSpec((8, w), lambda i: (jnp.minimum((i + 1) * (tr // 8), nb8 - 1), cb))]


def _ext(prev_ref, main_ref, next_ref, i, nt):
    p = jnp.where(i > 0, prev_ref[...].astype(F32), 0.0)
    n = jnp.where(i < nt - 1, next_ref[...].astype(F32), 0.0)
    return jnp.concatenate([p, main_ref[...].astype(F32), n], axis=0)


def _shift(x, k):
    return x if k == 0 else pltpu.roll(x, (-k) % x.shape[0], 0)


def _conv_taps(x_ext, w, tr):
    kt = w.shape[0]
    acc = None
    for t in range(kt):
        term = _shift(x_ext, t - kt // 2)[8:8 + tr] * w[t:t + 1, :]
        acc = term if acc is None else acc + term
    return acc


def _prep_a(name, proj, conv_a, lay, tr=256):
    s, ca = proj.shape[0], lay.ca
    tr = _tile(s, tr, 8)
    nt, w = s // tr, 3 * ca

    def body(p_ref, m_ref, n_ref, w_ref, o_ref):
        x = _ext(p_ref, m_ref, n_ref, pl.program_id(0), nt)
        xv = x[:, ca:2 * ca] * x[:, 2 * ca:w]
        y = _conv_taps(xv, w_ref[...], tr)
        o_ref[...] = (m_ref[:, 0:ca] * y).astype(BF16)

    return pl.pallas_call(
        body, grid=(nt,), in_specs=_halo_specs(tr, w, lay.cab // w, s) + [_vec(conv_a.shape[0], ca)],
        out_specs=_row(tr, ca), out_shape=SDS((s, ca), BF16),
        compiler_params=_params("parallel"), name=name)(proj, proj, proj, conv_a)


def _prep_a_bwd(name, dya, proj, conv_a, dproj, lay, tr=256):
    s, ca = proj.shape[0], lay.ca
    tr = _tile(s, tr, 8)
    nt, w, kt = s // tr, 3 * ca, conv_a.shape[0]

    def body(p_ref, m_ref, n_ref, dp_ref, dm_ref, dn_ref, w_ref, _, o_ref, acc_ref):
        i = pl.program_id(0)

        @pl.when(i == 0)
        def _():
            acc_ref[...] = jnp.zeros_like(acc_ref)

        x = _ext(p_ref, m_ref, n_ref, i, nt)
        d_ext = _ext(dp_ref, dm_ref, dn_ref, i, nt)
        cb, cc, cv = x[:, 0:ca], x[:, ca:2 * ca], x[:, 2 * ca:w]
        xv = cc * cv
        wv = w_ref[...]
        dy_ext = d_ext * cb
        dx = None
        for t in range(kt):
            term = _shift(dy_ext, kt // 2 - t)[8:8 + tr] * wv[t:t + 1, :]
            dx = term if dx is None else dx + term
            acc_ref[t:t + 1, :] += jnp.sum(dy_ext[8:8 + tr] * _shift(xv, t - kt // 2)[8:8 + tr],
                                           axis=0, keepdims=True)
        y = _conv_taps(xv, wv, tr)
        o_ref[:, 0:ca] = (dm_ref[...] * y).astype(BF16)
        o_ref[:, ca:2 * ca] = (dx * cv[8:8 + tr]).astype(BF16)
        o_ref[:, 2 * ca:w] = (dx * cc[8:8 + tr]).astype(BF16)

    return pl.pallas_call(
        body, grid=(nt,),
        in_specs=_halo_specs(tr, w, lay.cab // w, s) + _halo_specs(tr, ca, 0, s)
        + [_vec(kt, ca), pl.BlockSpec(memory_space=pl.ANY)],
        out_specs=[_row(tr, w, lay.cab // w), _vec(8, ca)],
        out_shape=[SDS(dproj.shape, dproj.dtype), SDS((8, ca), F32)], input_output_aliases={7: 0},
        compiler_params=_params("arbitrary"), name=name)(proj, proj, proj, dya, dya, dya, conv_a, dproj)


def _qkv_act(c, nh, tr_rows):
    sact = jax.nn.silu(c)
    outs, inv = [], []
    for hd in range(3 * nh):
        sl = sact[:, hd * HEAD:(hd + 1) * HEAD]
        if hd < 2 * nh:
            r = lax.rsqrt(jnp.sum(sl * sl, axis=-1, keepdims=True) + EPS)
            inv.append(r)
            outs.append(sl * (r * (HEAD ** -0.5 if hd < nh else 1.0)))
        else:
            outs.append(sl)
    return jnp.concatenate(outs, axis=-1), sact, inv


def _prep_b(name, proj, conv_dn, lay, tr=256):
    s, w, nh = proj.shape[0], lay.qkv, lay.nh
    tr = _tile(s, tr, 8)
    nt = s // tr

    def body(p_ref, m_ref, n_ref, w_ref, o_ref):
        x = _ext(p_ref, m_ref, n_ref, pl.program_id(0), nt)
        c = _conv_taps(x, w_ref[...], tr)
        o_ref[...] = _qkv_act(c, nh, tr)[0]

    return pl.pallas_call(
        body, grid=(nt,), in_specs=_halo_specs(tr, w, 0, s) + [_vec(conv_dn.shape[0], w)],
        out_specs=_row(tr, w), out_shape=SDS((s, w), F32),
        compiler_params=_params("parallel"), name=name)(proj, proj, proj, conv_dn)


def _prep_b_bwd(name, dq, dk, dv, proj, conv_dn, dproj, lay, tr=256):
    s, w, nh, qk = proj.shape[0], lay.qkv, lay.nh, lay.qk
    tr = _tile(s, tr, 8)
    nt, kt = s // tr, conv_dn.shape[0]

    def body(*refs):
        x_refs, g_refs = refs[0:3], refs[3:12]
        w_ref, o_ref, acc_ref = refs[12], refs[14], refs[15]
        i = pl.program_id(0)

        @pl.when(i == 0)
        def _():
            acc_ref[...] = jnp.zeros_like(acc_ref)

        x = _ext(*x_refs, i, nt)
        wv = w_ref[...]
        c = None
        for t in range(kt):
            term = _shift(x, t - kt // 2) * wv[t:t + 1, :]
            c = term if c is None else c + term
        sig = jax.nn.sigmoid(c)
        sact = c * sig
        ds = []
        for part in range(3):
            g = _ext(*g_refs[3 * part:3 * part + 3], i, nt)
            for hd in range(nh):
                sl = sact[:, part * qk + hd * HEAD:part * qk + (hd + 1) * HEAD]
                gh = g[:, hd * HEAD:(hd + 1) * HEAD]
                if part < 2:
                    r = lax.rsqrt(jnp.sum(sl * sl, axis=-1, keepdims=True) + EPS)
                    sc = HEAD ** -0.5 if part == 0 else 1.0
                    ds.append(sc * r * (gh - sl * (r * r) * jnp.sum(gh * sl, axis=-1, keepdims=True)))
                else:
                    ds.append(gh)
        dc = jnp.concatenate(ds, axis=-1) * (sig * (1.0 + c * (1.0 - sig)))
        dx = None
        for t in range(kt):
            term = _shift(dc, kt // 2 - t)[8:8 + tr] * wv[t:t + 1, :]
            dx = term if dx is None else dx + term
            acc_ref[t:t + 1, :] += jnp.sum(dc[8:8 + tr] * _shift(x, t - kt // 2)[8:8 + tr],
                                           axis=0, keepdims=True)
        o_ref[...] = dx.astype(BF16)

    return pl.pallas_call(
        body, grid=(nt,),
        in_specs=_halo_specs(tr, w, 0, s) + _halo_specs(tr, qk, 0, s) * 3
        + [_vec(kt, w), pl.BlockSpec(memory_space=pl.ANY)],
        out_specs=[_row(tr, w, 0), _vec(8, w)],
        out_shape=[SDS(dproj.shape, dproj.dtype), SDS((8, w), F32)], input_output_aliases={13: 0},
        compiler_params=_params("arbitrary"), name=name)(
            proj, proj, proj, dq, dq, dq, dk, dk, dk, dv, dv, dv, conv_dn, dproj)


def _softplus(x):
    return jnp.maximum(x, 0.0) + jnp.log(1.0 + jnp.exp(-jnp.abs(x)))


def _split3(x):
    hi = x.astype(BF16)
    r = x - hi.astype(F32)
    mid = r.astype(BF16)
    return hi, mid, (r - mid.astype(F32)).astype(BF16)


def _exact_nn(m, x):
    m = m.astype(BF16)
    hi, mid, lo = _split3(x)
    return _nn(m, hi) + _nn(m, mid) + _nn(m, lo)


def _chunk_cumsum_masks(tr):
    ri = lax.broadcasted_iota(jnp.int32, (tr, tr), 0)
    ci = lax.broadcasted_iota(jnp.int32, (tr, tr), 1)
    same = (ri // CHUNK) == (ci // CHUNK)
    return (same & (ci <= ri)).astype(F32), (same & (ci >= ri)).astype(F32)


def _prep_c(name, proj, pvec, lay, tr=512):
    s, nh = proj.shape[0], lay.nh
    tr = _tile(s, tr, CHUNK)
    assert 6 * nh <= LANES

    def body(x_ref, p_ref, o_ref):
        x = x_ref[...]
        lane = lax.broadcasted_iota(jnp.int32, x.shape, 1)
        is_g = (lane >= 2 * nh) & (lane < 4 * nh)
        g = jnp.where(is_g, -jnp.exp(p_ref[0:1, :]) * _softplus(x + p_ref[1:2, :]), 0.0)
        m_f, m_b = _chunk_cumsum_masks(tr)
        gc = jnp.where(lane < 3 * nh, _exact_nn(m_f, g), _exact_nn(m_b, g))
        gc = pltpu.roll(gc, 2 * nh, 1)
        o_ref[...] = jnp.where(lane < 2 * nh, jax.nn.sigmoid(x), jnp.where(lane < 4 * nh, g, gc))

    return pl.pallas_call(
        body, grid=(s // tr,), in_specs=[_row(tr, LANES, lay.ba // LANES), _vec(8, LANES)],
        out_specs=_row(tr, LANES), out_shape=SDS((s, LANES), F32),
        compiler_params=_params("parallel"), name=name)(proj, pvec)


def _prep_c_bwd(name, dbg_f, dbg_b, proj, pvec, dproj, lay, tr=512):
    s, nh, tail = proj.shape[0], lay.nh, lay.tail
    tr = _tile(s, tr, CHUNK)

    def body(x_ref, df_ref, db_ref, p_ref, _, o_ref, acc_ref):
        @pl.when(pl.program_id(0) == 0)
        def _():
            acc_ref[...] = jnp.zeros_like(acc_ref)

        x = x_ref[...]
        lane = lax.broadcasted_iota(jnp.int32, x.shape, 1)
        is_b, is_g = lane < 2 * nh, (lane >= 2 * nh) & (lane < 4 * nh)
        fwd_lane = (lane < nh) | ((lane >= 2 * nh) & (lane < 3 * nh))
        d = jnp.where(lane < 4 * nh, jnp.where(fwd_lane, df_ref[...], db_ref[...]), 0.0)
        m_f, m_b = _chunk_cumsum_masks(tr)
        dgc = jnp.where(is_g, d, 0.0)
        dg = jnp.where(fwd_lane, _exact_nn(m_b, dgc), _exact_nn(m_f, dgc))
        sb = jax.nn.sigmoid(x)
        na = -jnp.exp(p_ref[0:1, :])
        xs = x + p_ref[1:2, :]
        dsp = dg * na * jax.nn.sigmoid(xs)
        dx = jnp.where(is_b, d * sb * (1.0 - sb), jnp.where(is_g, dsp, 0.0))
        o_ref[...] = jnp.zeros_like(o_ref)
        o_ref[:, 0:LANES] = dx.astype(BF16)
        acc_ref[0:1, :] += jnp.sum(jnp.where(is_g, dg * na * _softplus(xs), 0.0), axis=0, keepdims=True)
        acc_ref[1:2, :] += jnp.sum(jnp.where(is_g, dsp, 0.0), axis=0, keepdims=True)

    return pl.pallas_call(
        body, grid=(s // tr,),
        in_specs=[_row(tr, LANES, lay.ba // LANES), _row(tr, LANES), _row(tr, LANES), _vec(8, LANES),
                  pl.BlockSpec(memory_space=pl.ANY)],
        out_specs=[_row(tr, tail, lay.ba // tail), _vec(8, LANES)],
        out_shape=[SDS(dproj.shape, dproj.dtype), SDS((8, LANES), F32)], input_output_aliases={4: 0},
        compiler_params=_params("arbitrary"), name=name)(proj, dbg_f, dbg_b, pvec, dproj)


def _post(name, o_f, o_b, proj, dn_w, lay, tr=256):
    s, qk, nh = o_f.shape[0], lay.qk, lay.nh
    tr = _tile(s, tr, 8)

    def body(f_ref, b_ref, z_ref, w_ref, o_ref):
        o = f_ref[...] + b_ref[...]
        gate = jax.nn.silu(z_ref[...])
        for hd in range(nh):
            sl = slice(hd * HEAD, (hd + 1) * HEAD)
            oh = o[:, sl]
            r = lax.rsqrt(jnp.mean(oh * oh, axis=-1, keepdims=True) + EPS)
            o_ref[:, sl] = (oh * r * w_ref[...] * gate[:, sl]).astype(BF16)

    return pl.pallas_call(
        body, grid=(s // tr,),
        in_specs=[_row(tr, qk), _row(tr, qk), _row(tr, qk, lay.z // qk), _vec(1, HEAD)],
        out_specs=_row(tr, qk), out_shape=SDS((s, qk), BF16),
        compiler_params=_params("parallel"), name=name)(o_f, o_b, proj, dn_w)


def _post_bwd(name, dyb, o_f, o_b, proj, dn_w, dproj, lay, tr=256):
    s, qk, nh = o_f.shape[0], lay.qk, lay.nh
    tr = _tile(s, tr, 8)

    def body(d_ref, f_ref, b_ref, z_ref, w_ref, _, do_ref, dz_ref, acc_ref):
        @pl.when(pl.program_id(0) == 0)
        def _():
            acc_ref[...] = jnp.zeros_like(acc_ref)

        o, z, d, wv = f_ref[...] + b_ref[...], z_ref[...], d_ref[...], w_ref[...]
        gate = jax.nn.silu(z)
        dgate = _silu_grad(z)
        for hd in range(nh):
            sl = slice(hd * HEAD, (hd + 1) * HEAD)
            oh, dh = o[:, sl], d[:, sl]
            r = lax.rsqrt(jnp.mean(oh * oh, axis=-1, keepdims=True) + EPS)
            n = oh * r
            dz_ref[:, sl] = (dh * n * wv * dgate[:, sl]).astype(BF16)
            don = dh * gate[:, sl]
            acc_ref[0:1, :] += jnp.sum(don * n, axis=0, keepdims=True)
            dn = don * wv
            do_ref[:, sl] = r * (dn - n * jnp.mean(dn * n, axis=-1, keepdims=True))

    return pl.pallas_call(
        body, grid=(s // tr,),
        in_specs=[_row(tr, qk), _row(tr, qk), _row(tr, qk), _row(tr, qk, lay.z // qk), _vec(1, HEAD),
                  pl.BlockSpec(memory_space=pl.ANY)],
        out_specs=[_row(tr, qk), _row(tr, qk, lay.z // qk), _vec(8, HEAD)],
        out_shape=[SDS((s, qk), F32), SDS(dproj.shape, dproj.dtype), SDS((8, HEAD), F32)],
        input_output_aliases={5: 1},
        compiler_params=_params("arbitrary"), name=name)(dyb, o_f, o_b, proj, dn_w, dproj)


def _b_out_merge(name, yb, w_b, pa, proj, lay, tr=512):
    s, d = pa.shape
    k = yb.shape[1]
    tr = _tile(s, tr, 16)

    def body(y_ref, w_ref, a_ref, g_ref, pb_ref, o_ref):
        pb = _nn(y_ref[...], w_ref[...])
        pb_ref[...] = pb
        o_ref[...] = (jax.nn.sigmoid(g_ref[:, 0:d]) * a_ref[...] + jax.nn.sigmoid(g_ref[:, d:2 * d]) * pb).astype(BF16)

    return pl.pallas_call(
        body, grid=(s // tr,),
        in_specs=[_row(tr, k), _vec(k, d), _row(tr, d), _row(tr, 2 * d, lay.ga // (2 * d))],
        out_specs=[_row(tr, d), _row(tr, d)], out_shape=[SDS((s, d), F32), SDS((s, d), BF16)],
        compiler_params=_params("parallel"), name=name)(yb, w_b, pa, proj)


def _merge_bwd(name, dmg, pa, pb, proj, lay, tr=512):
    s, d = pa.shape
    tr = _tile(s, tr, 8)

    def body(d_ref, a_ref, b_ref, g_ref, da_ref, db_ref, dg_ref):
        dm = d_ref[...]
        sa, sb = jax.nn.sigmoid(g_ref[:, 0:d]), jax.nn.sigmoid(g_ref[:, d:2 * d])
        da_ref[...] = (sa * dm).astype(BF16)
        db_ref[...] = (sb * dm).astype(BF16)
        dg_ref[:, 0:d] = (dm * a_ref[...] * sa * (1.0 - sa)).astype(BF16)
        dg_ref[:, d:2 * d] = (dm * b_ref[...] * sb * (1.0 - sb)).astype(BF16)

    return pl.pallas_call(
        body, grid=(s // tr,),
        in_specs=[_row(tr, d), _row(tr, d), _row(tr, d), _row(tr, 2 * d, lay.ga // (2 * d))],
        out_specs=[_row(tr, d), _row(tr, d), _row(tr, 2 * d, lay.ga // (2 * d))],
        out_shape=[SDS((s, d), BF16), SDS((s, d), BF16), SDS((s, lay.total), BF16)],
        compiler_params=_params("parallel"), name=name)(dmg, pa, pb, proj)


def _tri_inverse(a_mat, ri, ci):
    def same(shift):
        return (ri >> shift) == (ci >> shift)

    x = -jnp.where(same(3), a_mat, 0.0)
    t_mat = (ri == ci).astype(F32) + x
    for _ in range(2):
        x = _bnn(x, x)
        t_mat = t_mat + _bnn(t_mat, x)
    for shift in (3, 4, 5):
        b = jnp.where(same(shift + 1) & ~same(shift), a_mat, 0.0)
        t_mat = t_mat - _bnn(_bnn(t_mat, b), t_mat)
    return t_mat


def _chunk_terms(q, k, v, beta, gc, g_row, g_last, reverse, t_mat=None):
    c = CHUNK
    ri = lax.broadcasted_iota(jnp.int32, (c, c), 0)
    ci = lax.broadcasted_iota(jnp.int32, (c, c), 1)
    if reverse:
        incl, strict = ri <= ci, ri < ci
    else:
        incl, strict = ri >= ci, ri > ci
    decay = jnp.where(incl, jnp.exp(jnp.where(incl, gc - g_row, 0.0)), 0.0)
    e = jnp.exp(gc)
    ed = jnp.exp(g_last - gc)
    el = jnp.exp(g_last)
    kb = k * beta
    kk_qk = _bnt(jnp.concatenate([kb, q], axis=1), k)
    a_mat = jnp.where(strict, kk_qk[:, 0:c] * decay, 0.0)
    p_mat = jnp.where(incl, kk_qk[:, c:2 * c] * decay, 0.0)
    if t_mat is None:
        t_mat = _tri_inverse(a_mat, ri, ci)
    uw = _bnn(t_mat, jnp.concatenate([v * beta, kb * e], axis=2))
    return dict(incl=incl, strict=strict, decay=decay, e=e, ed=ed, el=el, kb=kb,
                a=a_mat, t=t_mat, uw=uw, u=uw[:, :, 0:HEAD], w=uw[:, :, HEAD:2 * HEAD], p=p_mat)


def _delta_specs(nh, tb, nb, reverse):
    tok = (lambda i: nb - 1 - i) if reverse else (lambda i: i)
    hw = nh * HEAD
    qkv = [pl.BlockSpec((tb, hw), functools.partial(lambda i, part: (tok(i), part), part=p)) for p in range(3)]
    rows = pl.BlockSpec((tb, hw), lambda i: (tok(i), 0))
    bg = pl.BlockSpec((tb, LANES), lambda i: (tok(i), 0))
    gct = pl.BlockSpec((2 * nh, tb), lambda i: (0, tok(i)))
    st = pl.BlockSpec((nh, tb // CHUNK, HEAD, HEAD), lambda i: (0, tok(i), 0, 0))
    tri = pl.BlockSpec((nh, tb // CHUNK, CHUNK, CHUNK), lambda i: (0, tok(i), 0, 0))
    return qkv, rows, bg, gct, st, tri


def _heads(ref, rows, nh):
    return jnp.stack([ref[rows, hd * HEAD:(hd + 1) * HEAD] for hd in range(nh)])


def _chunk_scalars(bg_ref, gct_ref, cj, nh, tb, reverse):
    rows = pl.ds(cj * CHUNK, CHUNK)
    lb = nh if reverse else 0
    lc = 4 * nh + lb
    last = cj * CHUNK + (0 if reverse else CHUNK - 1)
    g_lanes = gct_ref[lb:lb + nh, :]
    if cj:
        g_lanes = pltpu.roll(g_lanes, tb - cj * CHUNK, 1)
    col = lambda l0, r: jnp.stack([bg_ref[r, l0 + hd:l0 + hd + 1] for hd in range(nh)])
    return col(lb, rows), col(lc, rows), g_lanes[:, 0:CHUNK][:, None, :], col(lc, pl.ds(last, 1))


def _delta_fwd(name, qkvn, bg, gct, nh, reverse, tb=256):
    s = qkvn.shape[0]
    tb = _tile(s, tb, LANES)
    nb, cpb = s // tb, tb // CHUNK
    qkv, rows_spec, bg_spec, gct_spec, st, tri = _delta_specs(nh, tb, nb, reverse)

    def body(q_ref, k_ref, v_ref, bg_ref, gct_ref, o_ref, st_ref, tri_ref, state):
        @pl.when(pl.program_id(0) == 0)
        def _():
            state[...] = jnp.zeros_like(state)

        for cj in (range(cpb - 1, -1, -1) if reverse else range(cpb)):
            rows = pl.ds(cj * CHUNK, CHUNK)
            q, k, v = _heads(q_ref, rows, nh), _heads(k_ref, rows, nh), _heads(v_ref, rows, nh)
            tm = _chunk_terms(q, k, v, *_chunk_scalars(bg_ref, gct_ref, cj, nh, tb, reverse), reverse)
            s_in = state[...]
            st_ref[:, cj] = s_in
            tri_ref[:, cj] = tm["t"]
            ws_qs = _bnn(jnp.concatenate([tm["w"], q * tm["e"]], axis=1), s_in)
            vn = tm["u"] - ws_qs[:, 0:CHUNK]
            o = ws_qs[:, CHUNK:2 * CHUNK] + _bnn(tm["p"], vn)
            for hd in range(nh):
                o_ref[rows, hd * HEAD:(hd + 1) * HEAD] = o[hd]
            state[...] = s_in * tm["el"] + _btn(k * tm["ed"], vn)

    return pl.pallas_call(
        body, grid=(nb,), in_specs=qkv + [bg_spec, gct_spec], out_specs=[rows_spec, st, tri],
        out_shape=[SDS((s, nh * HEAD), F32), SDS((nh, s // CHUNK, HEAD, HEAD), F32),
                   SDS((nh, s // CHUNK, CHUNK, CHUNK), F32)],
        scratch_shapes=[pltpu.VMEM((nh, HEAD, HEAD), F32)],
        compiler_params=_params("arbitrary"), name=name)(qkvn, qkvn, qkvn, bg, gct)


def _delta_bwd(name, qkvn, bg, gct, do, states, tris, nh, reverse, add=None, tb=128, ex=None):
    s = qkvn.shape[0]
    tb = _tile(s, tb, LANES)
    nb, cpb = s // tb, tb // CHUNK
    qkv, rows_spec, bg_spec, gct_spec, st, tri = _delta_specs(nh, tb, nb, not reverse)
    n_add = 0 if add is None else 3
    host = _Hosted(ex, name, (nb,))

    def body(*refs):
        q_ref, k_ref, v_ref, bg_ref, gct_ref, do_ref, st_ref, tri_ref = refs[0:8]
        add_refs = refs[8:8 + n_add]
        (dq_ref, dk_ref, dv_ref, dbg_ref), (dstate,) = host.split(refs[8 + n_add:], 4)
        host.start()

        @pl.when(pl.program_id(0) == 0)
        def _():
            dstate[...] = jnp.zeros_like(dstate)

        ones = jnp.ones((nh, 2 * CHUNK, HEAD), BF16)
        row_id = lax.broadcasted_iota(jnp.int32, (CHUNK, 1), 0)
        rsum = lambda x: jnp.sum(x, axis=2, keepdims=True)
        for cj in (range(cpb) if reverse else range(cpb - 1, -1, -1)):
            rows = pl.ds(cj * CHUNK, CHUNK)
            q, k, v, d_o = (_heads(r, rows, nh) for r in (q_ref, k_ref, v_ref, do_ref))
            beta, gc, g_row, g_last = _chunk_scalars(bg_ref, gct_ref, cj, nh, tb, reverse)
            tm = _chunk_terms(q, k, v, beta, gc, g_row, g_last, reverse, t_mat=tri_ref[:, cj])
            incl, strict, e, ed, el, kb = tm["incl"], tm["strict"], tm["e"], tm["ed"], tm["el"], tm["kb"]
            t_mat, u, w, p_mat, decay = tm["t"], tm["u"], tm["w"], tm["p"], tm["decay"]
            s_in, ds_out = st_ref[:, cj], dstate[...]
            cat_rows = lambda a, b: jnp.concatenate([a, b], axis=1)
            top, bot = slice(0, CHUNK), slice(CHUNK, 2 * CHUNK)
            vn = u - _bnn(w, s_in)
            qe, kd, ke = q * e, k * ed, kb * e
            dvn = _btn(p_mat, d_o) + _bnn(kd, ds_out)
            by_state = _bnt(cat_rows(d_o, dvn), s_in)
            dqe, dw = by_state[:, top], -by_state[:, bot]
            dq = dqe * e
            dgc = rsum(dqe * qe)
            dp = jnp.where(incl, _bnt(d_o, vn), 0.0)
            dkd = _bnt(vn, ds_out)
            dk = dkd * ed
            r = rsum(dkd * kd)
            dgc = dgc - r
            dg_last = (jnp.sum(r, axis=1, keepdims=True)
                       + jnp.sum(rsum(ds_out * s_in), axis=1, keepdims=True) * el)
            d_uw = _btn(t_mat, jnp.concatenate([dvn, dw], axis=2))
            dbv, dke = d_uw[:, :, 0:HEAD], d_uw[:, :, HEAD:2 * HEAD]
            da = -jnp.where(strict, _bnt(d_uw, tm["uw"]), 0.0)
            mn = cat_rows(da * decay, dp * decay)
            by_k = _bnn(mn, k)
            dkb = by_k[:, top] + dke * e
            dq = dq + by_k[:, bot]
            dk = dk + _btn(mn, cat_rows(kb, q))
            g_mat = da * tm["a"] + dp * p_mat
            g_hi, g_mid, _ = _split3(g_mat)
            col = _btn(cat_rows(g_hi, g_mid), ones)[:, :, 0:1]
            dgc = dgc + rsum(g_mat) - col + rsum(dke * ke)
            dgc = dgc + jnp.where(row_id == (0 if reverse else CHUNK - 1), dg_last, 0.0)
            dv = dbv * beta
            dbeta = rsum(dbv * v) + rsum(dkb * k)
            dk = dk + dkb * beta
            dstate[...] = el * ds_out + _btn(cat_rows(qe, w), cat_rows(d_o, -dvn))
            lb = nh if reverse else 0
            for hd in range(nh):
                cols = slice(hd * HEAD, (hd + 1) * HEAD)
                extra = [a[rows, cols] for a in add_refs] if n_add else [0.0, 0.0, 0.0]
                dq_ref[rows, cols] = dq[hd] + extra[0]
                dk_ref[rows, cols] = dk[hd] + extra[1]
                dv_ref[rows, cols] = dv[hd] + extra[2]
                dbg_ref[rows, lb + hd:lb + hd + 1] = dbeta[hd]
                dbg_ref[rows, 2 * nh + lb + hd:2 * nh + lb + hd + 1] = dgc[hd]
        host.wait()

    out3 = SDS((s, nh * HEAD), F32)
    n_in = 8 + n_add
    out = pl.pallas_call(
        body, grid=(nb,),
        in_specs=qkv + [bg_spec, gct_spec, rows_spec, st, tri] + [rows_spec] * n_add + host.in_specs,
        out_specs=[rows_spec, rows_spec, rows_spec, bg_spec] + host.out_specs,
        out_shape=[out3, out3, out3, SDS((s, LANES), F32)] + host.out_shapes,
        input_output_aliases=host.aliases(n_in, 4), scratch_shapes=[pltpu.VMEM((nh, HEAD, HEAD), F32)] + host.sems,
        compiler_params=_params("arbitrary"), name=name)(
            qkvn, qkvn, qkvn, bg, gct, do, states, tris, *(add or ()), *host.arrays)
    return host.finish(out, 4)


def _row_pieces(g):
    return g.reshape(N_CHIPS, g.shape[0] // N_CHIPS, g.shape[1])


def _up_swiglu(name, u, w_up, tm=1024, ex=None):
    s, k = u.shape
    fh = w_up.shape[2]
    tm = _tile(s, tm, 8)
    grid = (2, s // tm)
    host = _Hosted(ex, name, grid)

    def body(u_ref, wa_ref, wb_ref, *rest):
        (a_ref, b_ref, o_ref), _ = host.split(rest, 3)
        host.start()
        x = u_ref[...]
        a, b = _nn(x, wa_ref[...]), _nn(x, wb_ref[...])
        a_ref[...], b_ref[...] = a.astype(BF16), b.astype(BF16)
        o_ref[...] = (jax.nn.silu(a) * b).astype(BF16)
        host.wait()

    tile = pl.BlockSpec((tm, fh), lambda j, i: (i, j))
    out = pl.pallas_call(
        body, grid=grid,
        in_specs=[pl.BlockSpec((tm, k), lambda j, i: (i, 0)), pl.BlockSpec((None, k, fh), lambda j, i: (j, 0, 0)),
                  pl.BlockSpec((None, k, fh), lambda j, i: (2 + j, 0, 0))] + host.in_specs,
        out_specs=[tile] * 3 + host.out_specs, out_shape=[SDS((s, 2 * fh), BF16)] * 3 + host.out_shapes,
        input_output_aliases=host.aliases(3, 3), scratch_shapes=host.sems,
        compiler_params=_params("arbitrary", "arbitrary"), name=name)(u, w_up, w_up, *host.arrays)
    return host.finish(out, 3)


def _matmul_resid(name, a, w, h, g, scale, tm=512, ex=None):
    s, k = a.shape
    d = w.shape[1]
    tm = _tile(s, tm, 16)
    grid = (s // tm,)
    host = _Hosted(ex, name, grid)

    def body(a_ref, w_ref, h_ref, g_ref, *rest):
        (o_ref, f_ref), _ = host.split(rest, 2)
        host.start()
        f = _nn(a_ref[...], w_ref[...])
        f_ref[...] = f.astype(BF16)
        o_ref[...] = h_ref[...] + (scale * g_ref[...]) * f
        host.wait()

    out = pl.pallas_call(
        body, grid=grid, in_specs=[_row(tm, k), _vec(k, d), _row(tm, d), _vec(1, d)] + host.in_specs,
        out_specs=[_row(tm, d), _row(tm, d)] + host.out_specs,
        out_shape=[SDS((s, d), F32), SDS((s, d), BF16)] + host.out_shapes,
        input_output_aliases=host.aliases(4, 2), scratch_shapes=host.sems,
        compiler_params=_params("arbitrary"), name=name)(a, w, h, g, *host.arrays)
    return host.finish(out, 2)


def _down_swiglu_bwd(name, df, w_down, a_pre, b_pre, tm=512):
    s, d = df.shape
    f = w_down.shape[0]
    tm = _tile(s, tm, 16)

    def body(df_ref, w_ref, a_ref, b_ref, o_ref):
        dhm = _nt(df_ref[...], w_ref[...])
        a = a_ref[...].astype(F32)
        o_ref[:, 0:f] = (dhm * b_ref[...].astype(F32) * _silu_grad(a)).astype(BF16)
        o_ref[:, f:2 * f] = (dhm * jax.nn.silu(a)).astype(BF16)

    return pl.pallas_call(
        body, grid=(s // tm,), in_specs=[_row(tm, d), _vec(f, d), _row(tm, f), _row(tm, f)],
        out_specs=_row(tm, 2 * f), out_shape=SDS((s, 2 * f), BF16),
        compiler_params=_params("parallel"), name=name)(df, w_down, a_pre, b_pre)


def _col_pieces(w):
    return w if w.ndim == 3 else w.reshape(w.shape[0], N_CHIPS, -1).transpose(1, 0, 2)


def _ffn_fwd(tag, h, nw, sh, sc, g, w_up, w_down, ex=None):
    u = _norm_mod(tag + "_norm", h, nw, sh, sc)
    a_pre, b_pre, hm = _up_swiglu(tag + "_up", u, _col_pieces(w_up), ex=ex)
    w_down = w_down() if callable(w_down) else w_down
    h_new, f = _matmul_resid(tag + "_down", hm, w_down, h, g, 0.5, ex=ex)
    return h_new, (h, u, a_pre, b_pre, hm, f, w_down)


def _ffn_bwd(tag, dh, saved, nw, sc, g, w_up, ex=None, on_gw_down=None):
    h, u, a_pre, b_pre, hm, f, w_down = saved
    df, acc_g = _resid_bwd(tag + "_res_bwd", dh, f, g, 0.5)
    shard = hm.shape[1] // 2
    gw_down = _row_pieces(_matmul(tag + "_gw_down", hm, df, "tn", out_dtype=BF16, tm=shard, ex=ex))
    if on_gw_down:
        on_gw_down(gw_down)
    dab = _down_swiglu_bwd(tag + "_dhm", df, w_down, a_pre, b_pre)
    gw_up = _matmul(tag + "_gw_up", u, dab, "tn", out_dtype=BF16, tn=shard, out_pieces=N_CHIPS, ex=ex)
    du = _matmul(tag + "_du", dab, w_up, "nt", ex=ex)
    dh_in, acc = _norm_mod_bwd(tag + "_norm_bwd", h, du, dh, nw, sc)
    return dh_in, gw_up, gw_down, (acc[0], acc[1], acc_g[0], acc[2])


def _mixer_fwd(h, nw, sh, sc, g, wt, lay, ex=None):
    nh = lay.nh
    u = _norm_mod("mix_norm", h, nw, sh, sc)
    proj = _matmul("mix_in", u, wt["w_in"], "nn", ex=ex)
    if ex:
        wt = dict(wt, **_as_operands(ex.weights("mix"), lay))
    qkvn = _prep_b("mix_prep_b", proj, wt["conv_dn"], lay)
    ya = _prep_a("mix_prep_a", proj, wt["conv_a"], lay)
    bg = _prep_c("mix_prep_c", proj, wt["pvec"], lay)
    gct = bg[:, 4 * nh:6 * nh].T
    o_f, *st_f = _delta_fwd("delta_fwd_l2r", qkvn, bg, gct, nh, False)
    o_b, *st_b = _delta_fwd("delta_fwd_r2l", qkvn, bg, gct, nh, True)
    yb = _post("mix_post", o_f, o_b, proj, wt["dn_norm"], lay)
    pa = _matmul("mix_a_out", ya, wt["w_a_out"], "nn")
    pb, mg = _b_out_merge("mix_b_out", yb, wt["w_b_out"], pa, proj, lay)
    h2, y = _matmul_resid("mix_out", mg, wt["w_out"], h, g, 1.0, ex=ex)
    return h2, (h, u, proj, qkvn, ya, bg, gct, o_f, o_b, st_f, st_b, yb, pa, pb, mg, y)


def _mixer_bwd(dh, saved, nw, sc, g, wt, lay, ex=None):
    h, u, proj, qkvn, ya, bg, gct, o_f, o_b, st_f, st_b, yb, pa, pb, mg, y = saved
    nh = lay.nh
    dy, acc_g = _resid_bwd("mix_res_bwd", dh, y, g, 1.0)
    gw_out = _matmul("mix_gw_out", mg, dy, "tn", out_dtype=BF16, ex=ex)
    dmg = _matmul("mix_dmg", dy, wt["w_out"], "nt")
    dpa, dpb, dproj = _merge_bwd("mix_merge_bwd", dmg, pa, pb, proj, lay)
    gw_a = _matmul("mix_gw_a", ya, dpa, "tn", out_dtype=BF16, out_pieces=N_CHIPS)
    gw_b = _matmul("mix_gw_b", yb, dpb, "tn", out_dtype=BF16)
    dya = _matmul("mix_dya", dpa, wt["w_a_out"], "nt")
    dyb = _matmul("mix_dyb", dpb, wt["w_b_out"], "nt")
    do, dproj, acc_dn = _post_bwd("mix_post_bwd", dyb, o_f, o_b, proj, wt["dn_norm"], dproj, lay)
    dq, dk, dv, dbg_f = _delta_bwd("delta_bwd_l2r", qkvn, bg, gct, do, *st_f, nh, False, ex=ex)
    dq, dk, dv, dbg_b = _delta_bwd("delta_bwd_r2l", qkvn, bg, gct, do, *st_b, nh, True, add=(dq, dk, dv), ex=ex)
    dproj, acc_ca = _prep_a_bwd("mix_prep_a_bwd", dya, proj, wt["conv_a"], dproj, lay)
    dproj, acc_cd = _prep_b_bwd("mix_prep_b_bwd", dq, dk, dv, proj, wt["conv_dn"], dproj, lay)
    dproj, acc_pc = _prep_c_bwd("mix_prep_c_bwd", dbg_f, dbg_b, proj, wt["pvec"], dproj, lay)
    gw_in = lay.unperm_cols(_matmul("mix_gw_in", u, dproj, "tn", out_dtype=BF16))
    gw_in = gw_in.reshape(gw_in.shape[0], N_CHIPS, -1).transpose(1, 0, 2)
    du = _matmul("mix_du", dproj, wt["w_in"], "nt")
    dh_in, acc = _norm_mod_bwd("mix_norm_bwd", h, du, dh, nw, sc)
    small = dict(conv_a=acc_ca[0:wt["conv_a"].shape[0]], conv_dn=acc_cd[0:wt["conv_dn"].shape[0]],
                 dn_norm=acc_dn[0:1], a_log=acc_pc[0], dt_bias=acc_pc[1])
    big = dict(w_in=gw_in, w_a_out=gw_a, w_b_out=_row_pieces(gw_b), w_out=_row_pieces(gw_out))
    return dh_in, big, small, (acc[0], acc[1], acc_g[0], acc[2])


def _as_operands(gathered, lay):
    wt = {}
    for n, g in gathered.items():
        if n == "w_in":
            wt[n] = lay.perm_cols(jnp.concatenate(list(g), axis=1))
        else:
            wt[n] = g if n in COL_SHARDED else g.reshape(-1, g.shape[-1])
    return wt


def _local_step(x, tgt, modv, wt, lay, ex=None):
    m = [modv[i:i + 1] for i in range(9)]

    def ffn1_down():
        ex.run(("gather_pass", "down1"))
        return _as_operands(ex.weights("down1"), lay)["w_ffn1_down"]

    h1, sv1 = _ffn_fwd("ffn1", x, wt["norm_ffn1"], m[0], m[1], m[2], wt["w_ffn1_up"],
                       ffn1_down if ex else wt["w_ffn1_down"], ex)
    if ex:
        wt = dict(wt, **_as_operands(ex.weights("in"), lay))
    h2, sv2 = _mixer_fwd(h1, wt["norm_mix"], m[3], m[4], m[5], wt, lay, ex)
    if ex:
        wt = dict(wt, **_as_operands(ex.weights("mix", "ffn2"), lay))
    h3, sv3 = _ffn_fwd("ffn2", h2, wt["norm_ffn2"], m[6], m[7], m[8], wt["w_ffn2_up"], wt["w_ffn2_down"])
    dh3, acc_f = _final_loss("final_loss", h3, tgt, wt["norm_final"])
    loss = jnp.sum(acc_f[1])
    dh2, gu2, gd2, dm3 = _ffn_bwd("ffn2", dh3, sv3, wt["norm_ffn2"], m[7], m[8], wt["w_ffn2_up"])
    if ex:
        ex.reduce("ffn2", dict(w_ffn2_up=gu2, w_ffn2_down=gd2))
    dh1, gmix, small, dm2 = _mixer_bwd(dh2, sv2, wt["norm_mix"], m[4], m[5], wt, lay, ex)
    if ex:
        ex.reduce("mixer", gmix)
    dx, gu1, gd1, dm1 = _ffn_bwd("ffn1", dh1, sv1, wt["norm_ffn1"], m[1], m[2], wt["w_ffn1_up"], ex,
                                 (lambda g: ex.reduce("down1", dict(w_ffn1_down=g))) if ex else None)
    dmod = jnp.stack([dm1[0], dm1[1], dm1[2], dm2[0], dm2[1], dm2[2], dm3[0], dm3[1], dm3[2]])
    big = dict(w_ffn1_up=gu1, w_ffn1_down=gd1, w_ffn2_up=gu2, w_ffn2_down=gd2, **gmix)
    small = dict(small, norm_ffn1=dm1[3], norm_mix=dm2[3], norm_ffn2=dm3[3], norm_final=acc_f[0])
    return loss, dx, dmod, big, small


def _position():
    return lax.axis_index("x"), lax.axis_index("y"), lax.axis_index("c")


_ANY = pl.BlockSpec(memory_space=pl.ANY)
_VMEM = pl.BlockSpec(memory_space=pltpu.VMEM)


def _allgather8(name, v):
    r = v.shape[0]

    def body(v_ref, out_ref, send_sems, recv_sems):
        x, y, c = _position()
        me = 4 * x + 2 * y + c
        out_ref[me] = v_ref[...]
        copies = []
        for mask in range(1, N_DEV):
            peer = tuple(1 - p if mask >> b & 1 else p for p, b in ((x, 2), (y, 1), (c, 0)))
            cp = pltpu.make_async_remote_copy(
                src_ref=v_ref, dst_ref=out_ref.at[me], send_sem=send_sems.at[mask - 1],
                recv_sem=recv_sems.at[mask - 1], device_id=peer, device_id_type=MESH)
            cp.start()
            copies.append(cp)
        for cp in copies:
            cp.wait()

    return pl.pallas_call(
        body, in_specs=[_VMEM], out_specs=_VMEM, out_shape=SDS((N_DEV, r, LANES), F32),
        scratch_shapes=[pltpu.SemaphoreType.DMA((N_DEV - 1,)), pltpu.SemaphoreType.DMA((N_DEV - 1,))],
        name=name)(v)


def _other_chips(x, y):
    return [(1 - x, y), (x, 1 - y), (1 - x, 1 - y)]


def _half_rows(c, rows):
    hr = rows // 2
    assert hr % 16 == 0
    return pl.ds(pl.multiple_of(c * hr, 16), hr)


class _Stage:
    def __init__(self, arrays, out_shapes, sems, plan, in_place=False):
        self.arrays, self.out_shapes, self.sems, self.plan = list(arrays), list(out_shapes), list(sems), plan
        self.alias_pairs = [(i, i) for i in range(len(self.arrays))] if in_place else []
        self.out_counts = [len(self.out_shapes)]

    def start(self, ins, outs, sems):
        for kind, cp in self.plan(ins, outs, sems):
            if kind != "recv":
                cp.start()

    def wait(self, ins, outs, sems):
        for kind, cp in self.plan(ins, outs, sems):
            {"local": cp.wait, "both": cp.wait, "send": cp.wait_send, "recv": cp.wait_recv}[kind]()

    def aliases(self, in_offset, out_offset):
        return {in_offset + i: out_offset + o for i, o in self.alias_pairs}


def _join(stages):
    ni = [0] + [len(st.arrays) for st in stages]
    no = [0] + [len(st.out_shapes) for st in stages]
    ns = [0] + [len(st.sems) for st in stages]
    for counts in (ni, no, ns):
        for k in range(1, len(counts)):
            counts[k] += counts[k - 1]

    def plan(ins, outs, sems):
        todo = []
        for k, st in enumerate(stages):
            todo += st.plan(ins[ni[k]:ni[k + 1]], outs[no[k]:no[k + 1]], sems[ns[k]:ns[k + 1]])
        return todo

    joined = _Stage([a for st in stages for a in st.arrays], [o for st in stages for o in st.out_shapes],
                    [m for st in stages for m in st.sems], plan)
    joined.alias_pairs = [(ni[k] + i, no[k] + o) for k, st in enumerate(stages) for i, o in st.alias_pairs]
    joined.out_counts = [len(st.out_shapes) for st in stages]
    return joined


def _run_stage(name, stage):
    n_in, n_out = len(stage.arrays), len(stage.out_shapes)

    def body(*refs):
        ins, outs, sems = refs[0:n_in], refs[n_in:n_in + n_out], refs[n_in + n_out:]
        stage.start(ins, outs, sems)
        stage.wait(ins, outs, sems)

    return pl.pallas_call(
        body, in_specs=[_ANY] * n_in, out_specs=[_ANY] * n_out, out_shape=stage.out_shapes,
        input_output_aliases=stage.aliases(0, 0), scratch_shapes=stage.sems, name=name)(*stage.arrays)


def _dma_sems(*counts):
    return [pltpu.SemaphoreType.DMA((n,)) for n in counts]


def _gather_send(shards):
    nw = len(shards)

    def plan(ins, outs, sems):
        send_sems, recv_sems, local_sems = sems
        x, y, c = _position()
        p = 2 * x + y
        todo = [("local", pltpu.make_async_copy(ins[w], outs[w].at[p], local_sems.at[w])) for w in range(nw)]
        for j, (cx, cy) in enumerate(_other_chips(x, y)):
            for w in range(nw):
                half = _half_rows(c, ins[w].shape[0])
                sem = dict(send_sem=send_sems.at[j * nw + w], recv_sem=recv_sems.at[j * nw + w], device_id_type=MESH)
                todo.append(("send", pltpu.make_async_remote_copy(
                    src_ref=ins[w].at[half], dst_ref=outs[w].at[p, half], device_id=(cx, cy, c), **sem)))
                landing = outs[w].at[2 * cx + cy, half]
                todo.append(("recv", pltpu.make_async_remote_copy(
                    src_ref=landing, dst_ref=landing, device_id=(x, y, c), **sem)))
        return todo

    return _Stage(shards, [SDS((N_CHIPS,) + v.shape, v.dtype) for v in shards], _dma_sems(3 * nw, 3 * nw, nw), plan)


def _gather_pass(gathered):
    nw = len(gathered)

    def plan(ins, outs, sems):
        send_sems, recv_sems = sems
        x, y, c = _position()
        todo = []
        for j, (cx, cy) in enumerate(_other_chips(x, y)):
            for w in range(nw):
                sem = dict(send_sem=send_sems.at[j * nw + w], recv_sem=recv_sems.at[j * nw + w], device_id_type=MESH)
                mine = outs[w].at[2 * cx + cy, _half_rows(c, outs[w].shape[1])]
                theirs = outs[w].at[2 * cx + cy, _half_rows(1 - c, outs[w].shape[1])]
                todo.append(("send", pltpu.make_async_remote_copy(
                    src_ref=mine, dst_ref=mine, device_id=(x, y, 1 - c), **sem)))
                todo.append(("recv", pltpu.make_async_remote_copy(
                    src_ref=theirs, dst_ref=theirs, device_id=(x, y, c), **sem)))
        return todo

    return _Stage(gathered, [SDS(g.shape, g.dtype) for g in gathered], _dma_sems(3 * nw, 3 * nw), plan, in_place=True)


def _swap_halves(gs):
    nw = len(gs)

    def plan(ins, outs, sems):
        x, y, c = _position()
        return [("both", pltpu.make_async_remote_copy(
            src_ref=ins[w].at[:, _half_rows(1 - c, ins[w].shape[1])], dst_ref=outs[w], send_sem=sems[0].at[w],
            recv_sem=sems[1].at[w], device_id=(x, y, 1 - c), device_id_type=MESH)) for w in range(nw)]

    return _Stage(gs, [SDS((g.shape[0], g.shape[1] // 2, g.shape[2]), g.dtype) for g in gs], _dma_sems(nw, nw), plan)


def _scatter_chips(vs):
    nw = len(vs)

    def plan(ins, outs, sems):
        x, y, c = _position()
        return [("both", pltpu.make_async_remote_copy(
            src_ref=ins[w].at[2 * cx + cy], dst_ref=outs[w].at[j], send_sem=sems[0].at[j * nw + w],
            recv_sem=sems[1].at[j * nw + w], device_id=(cx, cy, c), device_id_type=MESH))
            for j, (cx, cy) in enumerate(_other_chips(x, y)) for w in range(nw)]

    return _Stage(vs, [SDS((N_CHIPS - 1,) + v.shape[1:], v.dtype) for v in vs], _dma_sems(3 * nw, 3 * nw), plan)


def _share_halves(fulls):
    nw = len(fulls)

    def plan(ins, outs, sems):
        x, y, c = _position()
        todo = []
        for w in range(nw):
            rows = outs[w].at[_half_rows(c, outs[w].shape[0])]
            todo.append(("both", pltpu.make_async_remote_copy(
                src_ref=rows, dst_ref=rows, send_sem=sems[0].at[w], recv_sem=sems[1].at[w],
                device_id=(x, y, 1 - c), device_id_type=MESH)))
        return todo

    return _Stage(fulls, [SDS(f.shape, f.dtype) for f in fulls], _dma_sems(nw, nw), plan, in_place=True)


class _Hosted:
    def __init__(self, ex, name, grid):
        self.ex, self.name, self.grid = ex, name, grid
        self.stage = ex.host(name) if ex is not None else None
        st = self.stage
        self.arrays = list(st.arrays) if st else []
        self.out_shapes = list(st.out_shapes) if st else []
        self.sems = list(st.sems) if st else []
        self.in_specs, self.out_specs = [_ANY] * len(self.arrays), [_ANY] * len(self.out_shapes)

    def aliases(self, in_offset, out_offset):
        return self.stage.aliases(in_offset, out_offset) if self.stage else {}

    def split(self, rest, n_out):
        ni, no, ns = len(self.arrays), len(self.out_shapes), len(self.sems)
        self.ins, self.outs = rest[0:ni], rest[ni + n_out:ni + n_out + no]
        tail = rest[ni + n_out + no:]
        self.sem_refs = tail[len(tail) - ns:]
        return rest[ni:ni + n_out], tail[0:len(tail) - ns]

    def _at(self, last):
        conds = [pl.program_id(d) == (g - 1 if last else 0) for d, g in enumerate(self.grid)]
        return functools.reduce(lambda p, q: p & q, conds)

    def start(self):
        if self.stage:
            pl.when(self._at(False))(lambda: self.stage.start(self.ins, self.outs, self.sem_refs))

    def wait(self):
        if self.stage:
            pl.when(self._at(True))(lambda: self.stage.wait(self.ins, self.outs, self.sem_refs))

    def finish(self, out, n_out):
        out = list(out)
        if self.stage:
            self.ex.done(self.name, out[n_out:])
        return out[0:n_out]


GATHER = {"up1": ("w_ffn1_up",), "down1": ("w_ffn1_down",), "in": ("w_in",),
          "mix": ("w_a_out", "w_b_out", "w_out"), "ffn2": ("w_ffn2_up", "w_ffn2_down")}
REDUCE = {"ffn2": ("w_ffn2_up", "w_ffn2_down"), "mixer": ("w_in", "w_a_out", "w_b_out", "w_out"),
          "down1": ("w_ffn1_down",), "up1": ("w_ffn1_up",)}
HOSTS = {"ffn1_up": [("gather_send", "down1"), ("gather_send", "in")],
         "ffn1_down": [("gather_pass", "in"), ("gather_send", "mix")],
         "mix_in": [("gather_pass", "mix"), ("gather_send", "ffn2")],
         "mix_out": [("gather_pass", "ffn2")],
         "mix_gw_out": [("swap", "ffn2")], "delta_bwd_l2r": [("scatter", "ffn2")], "delta_bwd_r2l": [("share", "ffn2")],
         "ffn1_gw_down": [("swap", "mixer")], "ffn1_gw_up": [("scatter", "mixer"), ("swap", "down1")],
         "ffn1_du": [("share", "mixer"), ("scatter", "down1")]}


class _Exchange:
    def __init__(self, shards):
        self.shards = shards
        self.gathered, self.red, self.ready, self.reduced = {}, {}, {}, {}

    def _stage(self, kind, group):
        if kind == "gather_send":
            return _gather_send([self.shards[n] for n in GATHER[group]])
        if kind == "gather_pass":
            return _gather_pass(self.gathered[group])
        st = self.red[group]
        return {"swap": lambda: _swap_halves(st["parts"]), "scatter": lambda: _scatter_chips(st["chip_sums"]),
                "share": lambda: _share_halves(st["fulls"])}[kind]()

    def _done(self, kind, group, outs):
        if kind == "gather_send":
            self.gathered[group] = list(outs)
        elif kind == "gather_pass":
            self.ready.update(zip(GATHER[group], outs))
        elif kind == "swap":
            st = self.red[group]
            st["from_sib"] = list(outs)
            st["chip_sums"] = [_chip_sum("chip_sum_" + n, g, f) for n, g, f in zip(REDUCE[group], st["parts"], outs)]
        elif kind == "scatter":
            st = self.red[group]
            st["fulls"] = [_total("total_" + n, g, f, r)
                           for n, g, f, r in zip(REDUCE[group], st["parts"], st["from_sib"], outs)]
        else:
            self.reduced.update(zip(REDUCE[group], outs))

    def _all(self, steps, runner):
        stage = _join([self._stage(kind, group) for kind, group in steps])
        outs = list(runner(stage))
        for (kind, group), n in zip(steps, stage.out_counts):
            self._done(kind, group, outs[0:n])
            outs = outs[n:]

    def host(self, kernel_name):
        return _join([self._stage(*step) for step in HOSTS[kernel_name]]) if kernel_name in HOSTS else None

    def done(self, kernel_name, outs):
        self._all(HOSTS[kernel_name], lambda stage: outs)

    def run(self, *steps):
        name = "_".join(kind + "_" + group for kind, group in steps)
        self._all(steps, lambda stage: _run_stage(name, stage))

    def weights(self, *groups):
        return {n: self.ready[n] for g in groups for n in GATHER[g]}

    def reduce(self, group, parts):
        self.red[group] = dict(parts=[parts[n] for n in REDUCE[group]])


def _chip_sum(name, g, from_sib):
    _, r, cdim = g.shape
    hr = r // 2

    def body(g_ref, s_ref, o_ref):
        o_ref[...] = (g_ref[...].astype(F32) + s_ref[...].astype(F32)).astype(o_ref.dtype)

    blk = (None, hr, cdim)
    return pl.pallas_call(
        body, grid=(N_CHIPS,),
        in_specs=[pl.BlockSpec(blk, lambda j: (j, lax.axis_index("c"), 0)), pl.BlockSpec(blk, lambda j: (j, 0, 0))],
        out_specs=pl.BlockSpec(blk, lambda j: (j, 0, 0)),
        out_shape=SDS((N_CHIPS, hr, cdim), g.dtype), compiler_params=_params("parallel"), name=name)(g, from_sib)


def _total(name, g, from_sib, from_chips):
    _, r, cdim = g.shape
    hr = r // 2
    tr = _tile(hr, 256, 16)
    nt = hr // tr

    def body(g_ref, s_ref, rc_ref, o_ref):
        acc = g_ref[...].astype(F32) + s_ref[...].astype(F32)
        for j in range(N_CHIPS - 1):
            acc = acc + rc_ref[j].astype(F32)
        o_ref[...] = acc

    def chip():
        return 2 * lax.axis_index("x") + lax.axis_index("y")

    blk = (None, tr, cdim)
    return pl.pallas_call(
        body, grid=(nt,),
        in_specs=[pl.BlockSpec(blk, lambda i: (chip(), lax.axis_index("c") * nt + i, 0)),
                  pl.BlockSpec(blk, lambda i: (chip(), i, 0)),
                  pl.BlockSpec((N_CHIPS - 1, tr, cdim), lambda i: (0, i, 0))],
        out_specs=pl.BlockSpec((tr, cdim), lambda i: (lax.axis_index("c") * nt + i, 0)),
        out_shape=SDS((r, cdim), F32), compiler_params=_params("parallel"), name=name)(g, from_sib, from_chips)


def _sum8(name, v):
    _, r, w = v.shape

    def body(v_ref, o_ref):
        acc = v_ref[0]
        for j in range(1, N_DEV):
            acc = acc + v_ref[j]
        o_ref[...] = acc

    return pl.pallas_call(body, in_specs=[_VMEM], out_specs=_VMEM, out_shape=SDS((r, w), F32), name=name)(v)


def _adamw(name, w, g, m, v, tr=256):
    r, cdim = w.shape
    tr = _tile(r, tr, 8)
    bc1, bc2 = 1.0 - ADAM_B1 ** ADAM_STEP, 1.0 - ADAM_B2 ** ADAM_STEP

    def body(w_ref, g_ref, m_ref, v_ref, d_ref, nm_ref, nv_ref):
        g = g_ref[...]
        m2 = ADAM_B1 * m_ref[...] + (1.0 - ADAM_B1) * g
        v2 = ADAM_B2 * v_ref[...] + (1.0 - ADAM_B2) * (g * g)
        d_ref[...] = -ADAM_LR * ((m2 / bc1) / (jnp.sqrt(v2 / bc2) + ADAM_EPS) + ADAM_WD * w_ref[...])
        nm_ref[...] = m2
        nv_ref[...] = v2

    spec = _row(tr, cdim)
    out = SDS((r, cdim), F32)
    return pl.pallas_call(body, grid=(r // tr,), in_specs=[spec] * 4, out_specs=[spec] * 3, out_shape=[out] * 3,
                          compiler_params=_params("parallel"), name=name)(w, g, m, v)


def _pack_rows(arrays, width, row_mult, dtype):
    parts, spans, row = [], [], 0
    for a in arrays:
        n = a.size
        rows = -(-n // width)
        flat = a.reshape(-1).astype(dtype)
        if rows * width != n:
            flat = jnp.concatenate([flat, jnp.zeros((rows * width - n,), dtype)])
        parts.append(flat.reshape(rows, width))
        spans.append((row, rows, n, a.shape))
        row += rows
    pad = -row % row_mult
    if pad:
        parts.append(jnp.zeros((pad, width), dtype))
    return jnp.concatenate(parts, axis=0), spans


def _unpack_rows(packed, spans):
    return [packed[r0:r0 + rows].reshape(-1)[0:n].reshape(shape) for r0, rows, n, shape in spans]


BIG = ("w_ffn1_up", "w_ffn1_down", "w_in", "w_a_out", "w_b_out", "w_out", "w_ffn2_up", "w_ffn2_down")
COL_SHARDED = ("w_ffn1_up", "w_in", "w_a_out", "w_ffn2_up")
SMALL = ("b_ada", "norm_ffn1", "norm_mix", "conv_a", "conv_dn", "a_log_fwd", "dt_bias_fwd", "a_log_bwd",
         "dt_bias_bwd", "dn_norm", "norm_ffn2", "norm_final")
WEIGHTS = ("w_ada", "b_ada", "norm_ffn1", "w_ffn1_up", "w_ffn1_down", "norm_mix", "w_in", "conv_a", "conv_dn",
           "a_log_fwd", "dt_bias_fwd", "a_log_bwd", "dt_bias_bwd", "dn_norm", "w_a_out", "w_b_out", "w_out",
           "norm_ffn2", "w_ffn2_up", "w_ffn2_down", "norm_final")


def kernel(x, c, w_ada, b_ada, norm_ffn1, w_ffn1_up, w_ffn1_down, norm_mix, w_in, conv_a, conv_dn, a_log_fwd, dt_bias_fwd, a_log_bwd, dt_bias_bwd, dn_norm, w_a_out, w_b_out, w_out, norm_ffn2, w_ffn2_up, w_ffn2_down, norm_final, loss_target, m_w_ada, m_b_ada, m_norm_ffn1, m_w_ffn1_up, m_w_ffn1_down, m_norm_mix, m_w_in, m_conv_a, m_conv_dn, m_a_log_fwd, m_dt_bias_fwd, m_a_log_bwd, m_dt_bias_bwd, m_dn_norm, m_w_a_out, m_w_b_out, m_w_out, m_norm_ffn2, m_w_ffn2_up, m_w_ffn2_down, m_norm_final, v_w_ada, v_b_ada, v_norm_ffn1, v_w_ffn1_up, v_w_ffn1_down, v_norm_mix, v_w_in, v_conv_a, v_conv_dn, v_a_log_fwd, v_dt_bias_fwd, v_a_log_bwd, v_dt_bias_bwd, v_dn_norm, v_w_a_out, v_w_b_out, v_w_out, v_norm_ffn2, v_w_ffn2_up, v_w_ffn2_down, v_norm_final):
    given = dict(locals())
    wsh = {n: given[n] for n in WEIGHTS}
    msh = {n: given["m_" + n] for n in WEIGHTS}
    vsh = {n: given["v_" + n] for n in WEIGHTS}
    d = x.shape[-1]
    ca = conv_a.shape[-1] * N_CHIPS
    nh = conv_dn.shape[-1] * N_CHIPS // (3 * HEAD)
    lay = _Layout(d, ca, nh)
    xi, yi, ci = _position()
    chip = 2 * xi + yi
    me = 2 * chip + ci

    c_act = jax.nn.silu(c)
    g1, g1_spans = _pack_rows([c_act, conv_a[0], conv_dn[0]], LANES, 8, F32)
    g1_all = _allgather8("gather_cond", g1)
    per_dev = [_unpack_rows(g1_all[k], g1_spans) for k in range(N_DEV)]
    c_all = jnp.concatenate([p[0] for p in per_dev], axis=0)
    conv_a_full = jnp.concatenate([per_dev[2 * k][1] for k in range(N_CHIPS)], axis=1)
    conv_dn_full = jnp.concatenate([per_dev[2 * k][2] for k in range(N_CHIPS)], axis=1)

    mod_sh = _matmul("ada_mod", c_all, w_ada[0], "nn")
    b_sh = lax.dynamic_slice_in_dim(b_ada, chip * mod_sh.shape[1], mod_sh.shape[1], axis=1)
    g2, g2_spans = _pack_rows([mod_sh + b_sh], LANES, 8, F32)
    g2_all = _allgather8("gather_mod", g2)
    mod_all = jnp.concatenate([_unpack_rows(g2_all[2 * k], g2_spans)[0] for k in range(N_CHIPS)], axis=1)
    modv = lax.dynamic_index_in_dim(mod_all, me, 0, keepdims=False).reshape(9, d)

    ex = _Exchange({n: wsh[n][0].astype(BF16) for n in BIG})
    ex.run(("gather_send", "up1"))
    ex.run(("gather_pass", "up1"))
    wt = _as_operands(ex.weights("up1"), lay)
    lane_pad = (jnp.zeros((2 * nh,), F32), jnp.zeros((LANES - 4 * nh,), F32))
    pvec = jnp.stack([jnp.concatenate([lane_pad[0], a_log_fwd[0], a_log_bwd[0], lane_pad[1]]),
                      jnp.concatenate([lane_pad[0], dt_bias_fwd[0], dt_bias_bwd[0], lane_pad[1]])]
                     + [jnp.zeros((LANES,), F32)] * 6)
    wt.update(conv_a=conv_a_full, conv_dn=conv_dn_full, pvec=pvec, dn_norm=dn_norm, norm_ffn1=norm_ffn1,
              norm_mix=norm_mix, norm_ffn2=norm_ffn2, norm_final=norm_final.reshape(1, d))

    loss, dx, dmod, big, small = _local_step(x[0], loss_target[0], modv, wt, lay, ex)

    small_list = [loss.reshape(1, 1), dmod.reshape(1, 9 * d), small["norm_ffn1"], small["norm_mix"], small["conv_a"],
                  small["conv_dn"], small["a_log"][2 * nh:3 * nh], small["dt_bias"][2 * nh:3 * nh],
                  small["a_log"][3 * nh:4 * nh], small["dt_bias"][3 * nh:4 * nh], small["dn_norm"], small["norm_ffn2"],
                  small["norm_final"]]
    g3, g3_spans = _pack_rows(small_list, LANES, 8, F32)
    g3_all = _allgather8("gather_small_grads", g3)
    g_small = dict(zip(("loss",) + SMALL, _unpack_rows(_sum8("sum_small_grads", g3_all), g3_spans)))
    loss = g_small["loss"].reshape(())
    dmod_all = jnp.concatenate([_unpack_rows(g3_all[k], g3_spans)[1] for k in range(N_DEV)], axis=0)
    ncol = w_ada.shape[-1]
    dmod_sh = lax.dynamic_slice_in_dim(dmod_all, chip * ncol, ncol, axis=1)
    grads = {"w_ada": _matmul("ada_grad", c_all, dmod_sh, "tn")[None]}
    for n in SMALL:
        g = g_small[n]
        if n in ("conv_a", "conv_dn"):
            wloc = wsh[n].shape[-1]
            g = lax.dynamic_slice_in_dim(g, chip * wloc, wloc, axis=1)
        grads[n] = g.reshape(wsh[n].shape)

    ex.reduce("up1", big)
    ex.run(("share", "down1"), ("swap", "up1"))
    ex.run(("scatter", "up1"))
    ex.run(("share", "up1"))
    for n in BIG:
        grads[n] = ex.reduced[n][None]

    delta, new_m, new_v = {}, {}, {}
    for n in ("w_ada",) + BIG:
        shp = wsh[n].shape
        outs = _adamw("adamw_" + n, *(t.reshape(shp[-2], shp[-1]) for t in (wsh[n], grads[n], msh[n], vsh[n])))
        delta[n], new_m[n], new_v[n] = (o.reshape(shp) for o in outs)
    packed = []
    for src in (wsh, grads, msh, vsh):
        pk, s_spans = _pack_rows([src[n] for n in SMALL], LANES, 8, F32)
        packed.append(pk)
    outs = _adamw("adamw_small", *packed)
    for dst, o in zip((delta, new_m, new_v), outs):
        dst.update(zip(SMALL, _unpack_rows(o, s_spans)))

    return (loss, dx[None], *[grads[n] for n in WEIGHTS], *[delta[n] for n in WEIGHTS],
            *[new_m[n] for n in WEIGHTS], *[new_v[n] for n in WEIGHTS])
```

```python
import functools

import jax
import jax.numpy as jnp
from jax import lax
from jax.experimental import pallas as pl
from jax.experimental.pallas import tpu as pltpu

F32 = jnp.float32
BF16 = jnp.bfloat16
SDS = jax.ShapeDtypeStruct
MESH = pl.DeviceIdType.MESH
HI = lax.Precision.HIGHEST

EPS = 1e-6
HEAD = 128
CHUNK = 64
LANES = 128
N_CHIPS = 4
N_DEV = 8
VMEM_LIMIT = 56 * 1024 * 1024
FULL_K = 3072

ADAM_LR = 0.001
ADAM_B1 = 0.9
ADAM_B2 = 0.999
ADAM_EPS = 1e-08
ADAM_WD = 0.01
ADAM_STEP = 10


def _params(*sem):
    return pltpu.CompilerParams(dimension_semantics=sem, vmem_limit_bytes=VMEM_LIMIT)


def _tile(n, cap, mult=LANES):
    t = min(n, cap) // mult * mult
    while t >= mult:
        if n % t == 0:
            return t
        t -= mult
    return n


def _row(tr, w, cb=0):
    return pl.BlockSpec((tr, w), lambda i: (i, cb))


def _vec(r, w):
    return pl.BlockSpec((r, w), lambda i: (0, 0))


def _nn(a, b, **kw):
    return jnp.dot(a, b, preferred_element_type=F32, **kw)


def _nt(a, b, **kw):
    return lax.dot_general(a, b, (((1,), (1,)), ((), ())), preferred_element_type=F32, **kw)


def _tn(a, b, **kw):
    return lax.dot_general(a, b, (((0,), (0,)), ((), ())), preferred_element_type=F32, **kw)


def _bnn(a, b):
    return lax.dot_general(a, b, (((2,), (1,)), ((0,), (0,))), preferred_element_type=F32)


def _bnt(a, b):
    return lax.dot_general(a, b, (((2,), (2,)), ((0,), (0,))), preferred_element_type=F32)


def _btn(a, b):
    return lax.dot_general(a, b, (((1,), (1,)), ((0,), (0,))), preferred_element_type=F32)


def _silu_grad(x):
    s = jax.nn.sigmoid(x)
    return s * (1.0 + x * (1.0 - s))


def _matmul(name, a, b, mode, out_dtype=F32, tm=1024, tn=1024, tk=2048, out_pieces=0, ex=None):
    pieces_b = b.shape[0] if b.ndim == 3 else 0
    b2 = b.shape[1:] if pieces_b else b.shape
    if mode == "nn":
        (m, k), n = a.shape, b2[1] * max(pieces_b, 1)
    elif mode == "nt":
        (m, _), n, k = a.shape, b2[0], b2[1] * max(pieces_b, 1)
    else:
        (k, m), n = a.shape, b2[1] * max(pieces_b, 1)
    n_unit = n // max(out_pieces, 1) if mode == "nt" or not pieces_b else n // pieces_b
    if out_pieces and pieces_b and mode != "nt":
        assert out_pieces == pieces_b
    k_unit = k // pieces_b if (pieces_b and mode == "nt") else k
    tm, tn = _tile(m, tm), _tile(n_unit, tn)
    tk = k_unit if k_unit <= FULL_K else _tile(k_unit, tk)
    nk = k // tk
    n_per, k_per = n_unit // tn, k_unit // tk
    a_bytes, b_bytes = a.size * a.dtype.itemsize, b.size * b.dtype.itemsize
    j_outer = nk == 1 and b_bytes + a_bytes * (n // tn) < a_bytes + b_bytes * (m // tm)
    ij = (lambda g0, g1: (g1, g0)) if j_outer else (lambda g0, g1: (g0, g1))

    def spec(shape, pick):
        return pl.BlockSpec(shape, lambda g0, g1, l: pick(*ij(g0, g1), l))

    a_spec = {"nn": spec((tm, tk), lambda i, j, l: (i, l)), "nt": spec((tm, tk), lambda i, j, l: (i, l)),
              "tn": spec((tk, tm), lambda i, j, l: (l, i))}[mode]
    if not pieces_b:
        b_spec = {"nn": spec((tk, tn), lambda i, j, l: (l, j)), "nt": spec((tn, tk), lambda i, j, l: (j, l)),
                  "tn": spec((tk, tn), lambda i, j, l: (l, j))}[mode]
    elif mode == "nt":
        b_spec = spec((None, tn, tk), lambda i, j, l: (l // k_per, j, l % k_per))
    else:
        b_spec = spec((None, tk, tn), lambda i, j, l: (j // n_per, l, j % n_per))
    if out_pieces:
        o_spec = spec((None, tm, tn), lambda i, j, l: (j // n_per, i, j % n_per))
        o_shape = SDS((out_pieces, m, n // out_pieces), out_dtype)
    else:
        o_spec, o_shape = spec((tm, tn), lambda i, j, l: (i, j)), SDS((m, n), out_dtype)
    dot = {"nn": _nn, "nt": _nt, "tn": _tn}[mode]
    grid = (n // tn, m // tm, nk) if j_outer else (m // tm, n // tn, nk)
    host = _Hosted(ex, name, grid)

    def body(a_ref, b_ref, *rest):
        (o_ref,), scratch = host.split(rest, 1)
        host.start()
        part = dot(a_ref[...].astype(BF16), b_ref[...].astype(BF16))
        if nk == 1:
            o_ref[...] = part.astype(o_ref.dtype)
        else:
            l, acc = pl.program_id(2), scratch[0]

            @pl.when(l == 0)
            def _():
                acc[...] = part

            @pl.when((l > 0) & (l < nk - 1))
            def _():
                acc[...] += part

            @pl.when(l == nk - 1)
            def _():
                o_ref[...] = (acc[...] + part).astype(o_ref.dtype)
        host.wait()

    out = pl.pallas_call(
        body, grid=grid, in_specs=[a_spec, b_spec] + host.in_specs, out_specs=[o_spec] + host.out_specs,
        out_shape=[o_shape] + host.out_shapes, input_output_aliases=host.aliases(2, 1),
        scratch_shapes=([] if nk == 1 else [pltpu.VMEM((tm, tn), F32)]) + host.sems,
        compiler_params=_params(*(("arbitrary",) * 3 if host.stage else ("parallel", "parallel", "arbitrary"))),
        name=name)(a, b, *host.arrays)
    return host.finish(out, 1)[0]


def _norm_mod(name, h, nw, sh, sc, tr=512):
    s, d = h.shape
    tr = _tile(s, tr, 8)

    def body(h_ref, nw_ref, sh_ref, sc_ref, u_ref):
        x = h_ref[...]
        r = lax.rsqrt(jnp.mean(x * x, axis=-1, keepdims=True) + EPS)
        u_ref[...] = (x * r * nw_ref[...] * (1.0 + sc_ref[...]) + sh_ref[...]).astype(BF16)

    return pl.pallas_call(
        body, grid=(s // tr,), in_specs=[_row(tr, d), _vec(1, d), _vec(1, d), _vec(1, d)],
        out_specs=_row(tr, d), out_shape=SDS((s, d), BF16),
        compiler_params=_params("parallel"), name=name)(h, nw, sh, sc)


def _norm_mod_bwd(name, h, du, dh, nw, sc, tr=512):
    s, d = h.shape
    tr = _tile(s, tr, 8)

    def body(h_ref, du_ref, dh_ref, nw_ref, sc_ref, o_ref, acc_ref):
        @pl.when(pl.program_id(0) == 0)
        def _():
            acc_ref[...] = jnp.zeros_like(acc_ref)

        x, g = h_ref[...], du_ref[...]
        r = lax.rsqrt(jnp.mean(x * x, axis=-1, keepdims=True) + EPS)
        n = x * r
        nw, sc1 = nw_ref[...], 1.0 + sc_ref[...]
        dn = g * sc1 * nw
        o_ref[...] = dh_ref[...] + r * (dn - n * jnp.mean(dn * n, axis=-1, keepdims=True))
        gn = g * n
        acc_ref[0:1, :] += jnp.sum(g, axis=0, keepdims=True)
        acc_ref[1:2, :] += jnp.sum(gn * nw, axis=0, keepdims=True)
        acc_ref[2:3, :] += jnp.sum(gn * sc1, axis=0, keepdims=True)

    return pl.pallas_call(
        body, grid=(s // tr,),
        in_specs=[_row(tr, d), _row(tr, d), _row(tr, d), _vec(1, d), _vec(1, d)],
        out_specs=[_row(tr, d), _vec(8, d)], out_shape=[SDS((s, d), F32), SDS((8, d), F32)],
        compiler_params=_params("arbitrary"), name=name)(h, du, dh, nw, sc)


def _resid_bwd(name, dh, f, g, scale, tr=512):
    s, d = dh.shape
    tr = _tile(s, tr, 8)

    def body(dh_ref, f_ref, g_ref, o_ref, acc_ref):
        @pl.when(pl.program_id(0) == 0)
        def _():
            acc_ref[...] = jnp.zeros_like(acc_ref)

        x = dh_ref[...]
        o_ref[...] = ((scale * g_ref[...]) * x).astype(BF16)
        acc_ref[0:1, :] += jnp.sum(scale * x * f_ref[...], axis=0, keepdims=True)

    return pl.pallas_call(
        body, grid=(s // tr,), in_specs=[_row(tr, d), _row(tr, d), _vec(1, d)],
        out_specs=[_row(tr, d), _vec(8, d)], out_shape=[SDS((s, d), BF16), SDS((8, d), F32)],
        compiler_params=_params("arbitrary"), name=name)(dh, f, g)


def _final_loss(name, h, tgt, nw, tr=512):
    s, d = h.shape
    tr = _tile(s, tr, 8)

    def body(h_ref, t_ref, nw_ref, o_ref, acc_ref):
        @pl.when(pl.program_id(0) == 0)
        def _():
            acc_ref[...] = jnp.zeros_like(acc_ref)

        x, nw = h_ref[...], nw_ref[...]
        r = lax.rsqrt(jnp.mean(x * x, axis=-1, keepdims=True) + EPS)
        n = x * r
        diff = n * nw - t_ref[...]
        dy = diff * (1.0 / d)
        dn = dy * nw
        o_ref[...] = r * (dn - n * jnp.mean(dn * n, axis=-1, keepdims=True))
        acc_ref[0:1, :] += jnp.sum(dy * n, axis=0, keepdims=True)
        acc_ref[1:2, :] += jnp.sum(diff * diff, axis=0, keepdims=True) * (0.5 / d)

    return pl.pallas_call(
        body, grid=(s // tr,), in_specs=[_row(tr, d), _row(tr, d), _vec(1, d)],
        out_specs=[_row(tr, d), _vec(8, d)], out_shape=[SDS((s, d), F32), SDS((8, d), F32)],
        compiler_params=_params("arbitrary"), name=name)(h, tgt, nw)


class _Layout:
    def __init__(self, d, ca, nh):
        self.d, self.ca, self.nh = d, ca, nh
        self.qk = nh * HEAD
        self.qkv = 3 * self.qk
        self.z = self.qkv
        self.ga = self.z + self.qk
        self.cab = self.ga + 2 * d
        self.ba = self.cab + 3 * ca
        self.tail = _tile(self.ba, 512)
        self.total = self.ba + self.tail
        assert self.qkv % self.qk == 0 and self.ga % (2 * d) == 0 and self.cab % (3 * ca) == 0
        assert self.ba % self.tail == 0 and 4 * nh <= LANES

    def perm_cols(self, w):
        ca, qkv, qk, d, nh = self.ca, self.qkv, self.qk, self.d, self.nh
        o = [0, ca, 2 * ca, 3 * ca, 3 * ca + qkv, 3 * ca + qkv + qk, 3 * ca + qkv + qk + 4 * nh]
        cb, cc, cv = (w[..., o[i]:o[i + 1]] for i in range(3))
        x_qkv, x_z, x_ba = w[..., o[3]:o[4]], w[..., o[4]:o[5]], w[..., o[5]:o[6]]
        gates = w[..., o[6]:o[6] + 2 * d]
        pad = jnp.zeros(w.shape[:-1] + (self.tail - 4 * nh,), w.dtype)
        return jnp.concatenate([x_qkv, x_z, gates, cb, cc, cv, x_ba, pad], axis=-1)

    def unperm_cols(self, w):
        ca, nh = self.ca, self.nh
        cb, cc, cv = (w[..., self.cab + i * ca:self.cab + (i + 1) * ca] for i in range(3))
        return jnp.concatenate([cb, cc, cv, w[..., 0:self.qkv], w[..., self.z:self.ga],
                                w[..., self.ba:self.ba + 4 * nh], w[..., self.ga:self.cab]], axis=-1)


def _halo_specs(tr, w, cb, s):
    nb8 = s // 8
    return [pl.BlockSpec((8, w), lambda i: (jnp.maximum(i * (tr // 8) - 1, 0), cb)),
            pl.BlockSpec((tr, w), lambda i: (i, cb)),
            pl.BlockSpec((8, w), lambda i: (jnp.minimum((i + 1) * (tr // 8), nb8 - 1), cb))]


def _ext(prev_ref, main_ref, next_ref, i, nt):
    p = jnp.where(i > 0, prev_ref[...].astype(F32), 0.0)
    n = jnp.where(i < nt - 1, next_ref[...].astype(F32), 0.0)
    return jnp.concatenate([p, main_ref[...].astype(F32), n], axis=0)


def _shift(x, k):
    return x if k == 0 else pltpu.roll(x, (-k) % x.shape[0], 0)


def _conv_taps(x_ext, w, tr):
    kt = w.shape[0]
    acc = None
    for t in range(kt):
        term = _shift(x_ext, t - kt // 2)[8:8 + tr] * w[t:t + 1, :]
        acc = term if acc is None else acc + term
    return acc


def _prep_a(name, proj, conv_a, lay, tr=256):
    s, ca = proj.shape[0], lay.ca
    tr = _tile(s, tr, 8)
    nt, w = s // tr, 3 * ca

    def body(p_ref, m_ref, n_ref, w_ref, o_ref):
        x = _ext(p_ref, m_ref, n_ref, pl.program_id(0), nt)
        xv = x[:, ca:2 * ca] * x[:, 2 * ca:w]
        y = _conv_taps(xv, w_ref[...], tr)
        o_ref[...] = (m_ref[:, 0:ca] * y).astype(BF16)

    return pl.pallas_call(
        body, grid=(nt,), in_specs=_halo_specs(tr, w, lay.cab // w, s) + [_vec(conv_a.shape[0], ca)],
        out_specs=_row(tr, ca), out_shape=SDS((s, ca), BF16),
        compiler_params=_params("parallel"), name=name)(proj, proj, proj, conv_a)


def _prep_a_bwd(name, dya, proj, conv_a, dproj, lay, tr=256):
    s, ca = proj.shape[0], lay.ca
    tr = _tile(s, tr, 8)
    nt, w, kt = s // tr, 3 * ca, conv_a.shape[0]

    def body(p_ref, m_ref, n_ref, dp_ref, dm_ref, dn_ref, w_ref, _, o_ref, acc_ref):
        i = pl.program_id(0)

        @pl.when(i == 0)
        def _():
            acc_ref[...] = jnp.zeros_like(acc_ref)

        x = _ext(p_ref, m_ref, n_ref, i, nt)
        d_ext = _ext(dp_ref, dm_ref, dn_ref, i, nt)
        cb, cc, cv = x[:, 0:ca], x[:, ca:2 * ca], x[:, 2 * ca:w]
        xv = cc * cv
        wv = w_ref[...]
        dy_ext = d_ext * cb
        dx = None
        for t in range(kt):
            term = _shift(dy_ext, kt // 2 - t)[8:8 + tr] * wv[t:t + 1, :]
            dx = term if dx is None else dx + term
            acc_ref[t:t + 1, :] += jnp.sum(dy_ext[8:8 + tr] * _shift(xv, t - kt // 2)[8:8 + tr],
                                           axis=0, keepdims=True)
        y = _conv_taps(xv, wv, tr)
        o_ref[:, 0:ca] = (dm_ref[...] * y).astype(BF16)
        o_ref[:, ca:2 * ca] = (dx * cv[8:8 + tr]).astype(BF16)
        o_ref[:, 2 * ca:w] = (dx * cc[8:8 + tr]).astype(BF16)

    return pl.pallas_call(
        body, grid=(nt,),
        in_specs=_halo_specs(tr, w, lay.cab // w, s) + _halo_specs(tr, ca, 0, s)
        + [_vec(kt, ca), pl.BlockSpec(memory_space=pl.ANY)],
        out_specs=[_row(tr, w, lay.cab // w), _vec(8, ca)],
        out_shape=[SDS(dproj.shape, dproj.dtype), SDS((8, ca), F32)], input_output_aliases={7: 0},
        compiler_params=_params("arbitrary"), name=name)(proj, proj, proj, dya, dya, dya, conv_a, dproj)


def _qkv_act(c, nh, tr_rows):
    sact = jax.nn.silu(c)
    outs, inv = [], []
    for hd in range(3 * nh):
        sl = sact[:, hd * HEAD:(hd + 1) * HEAD]
        if hd < 2 * nh:
            r = lax.rsqrt(jnp.sum(sl * sl, axis=-1, keepdims=True) + EPS)
            inv.append(r)
            outs.append(sl * (r * (HEAD ** -0.5 if hd < nh else 1.0)))
        else:
            outs.append(sl)
    return jnp.concatenate(outs, axis=-1), sact, inv


def _prep_b(name, proj, conv_dn, lay, tr=256):
    s, w, nh = proj.shape[0], lay.qkv, lay.nh
    tr = _tile(s, tr, 8)
    nt = s // tr

    def body(p_ref, m_ref, n_ref, w_ref, o_ref):
        x = _ext(p_ref, m_ref, n_ref, pl.program_id(0), nt)
        c = _conv_taps(x, w_ref[...], tr)
        o_ref[...] = _qkv_act(c, nh, tr)[0]

    return pl.pallas_call(
        body, grid=(nt,), in_specs=_halo_specs(tr, w, 0, s) + [_vec(conv_dn.shape[0], w)],
        out_specs=_row(tr, w), out_shape=SDS((s, w), F32),
        compiler_params=_params("parallel"), name=name)(proj, proj, proj, conv_dn)


def _prep_b_bwd(name, dq, dk, dv, proj, conv_dn, dproj, lay, tr=256):
    s, w, nh, qk = proj.shape[0], lay.qkv, lay.nh, lay.qk
    tr = _tile(s, tr, 8)
    nt, kt = s // tr, conv_dn.shape[0]

    def body(*refs):
        x_refs, g_refs = refs[0:3], refs[3:12]
        w_ref, o_ref, acc_ref = refs[12], refs[14], refs[15]
        i = pl.program_id(0)

        @pl.when(i == 0)
        def _():
            acc_ref[...] = jnp.zeros_like(acc_ref)

        x = _ext(*x_refs, i, nt)
        wv = w_ref[...]
        c = None
        for t in range(kt):
            term = _shift(x, t - kt // 2) * wv[t:t + 1, :]
            c = term if c is None else c + term
        sig = jax.nn.sigmoid(c)
        sact = c * sig
        ds = []
        for part in range(3):
            g = _ext(*g_refs[3 * part:3 * part + 3], i, nt)
            for hd in range(nh):
                sl = sact[:, part * qk + hd * HEAD:part * qk + (hd + 1) * HEAD]
                gh = g[:, hd * HEAD:(hd + 1) * HEAD]
                if part < 2:
                    r = lax.rsqrt(jnp.sum(sl * sl, axis=-1, keepdims=True) + EPS)
                    sc = HEAD ** -0.5 if part == 0 else 1.0
                    ds.append(sc * r * (gh - sl * (r * r) * jnp.sum(gh * sl, axis=-1, keepdims=True)))
                else:
                    ds.append(gh)
        dc = jnp.concatenate(ds, axis=-1) * (sig * (1.0 + c * (1.0 - sig)))
        dx = None
        for t in range(kt):
            term = _shift(dc, kt // 2 - t)[8:8 + tr] * wv[t:t + 1, :]
            dx = term if dx is None else dx + term
            acc_ref[t:t + 1, :] += jnp.sum(dc[8:8 + tr] * _shift(x, t - kt // 2)[8:8 + tr],
                                           axis=0, keepdims=True)
        o_ref[...] = dx.astype(BF16)

    return pl.pallas_call(
        body, grid=(nt,),
        in_specs=_halo_specs(tr, w, 0, s) + _halo_specs(tr, qk, 0, s) * 3
        + [_vec(kt, w), pl.BlockSpec(memory_space=pl.ANY)],
        out_specs=[_row(tr, w, 0), _vec(8, w)],
        out_shape=[SDS(dproj.shape, dproj.dtype), SDS((8, w), F32)], input_output_aliases={13: 0},
        compiler_params=_params("arbitrary"), name=name)(
            proj, proj, proj, dq, dq, dq, dk, dk, dk, dv, dv, dv, conv_dn, dproj)


def _softplus(x):
    return jnp.maximum(x, 0.0) + jnp.log(1.0 + jnp.exp(-jnp.abs(x)))


def _split3(x):
    hi = x.astype(BF16)
    r = x - hi.astype(F32)
    mid = r.astype(BF16)
    return hi, mid, (r - mid.astype(F32)).astype(BF16)


def _exact_nn(m, x):
    m = m.astype(BF16)
    hi, mid, lo = _split3(x)
    return _nn(m, hi) + _nn(m, mid) + _nn(m, lo)


def _chunk_cumsum_masks(tr):
    ri = lax.broadcasted_iota(jnp.int32, (tr, tr), 0)
    ci = lax.broadcasted_iota(jnp.int32, (tr, tr), 1)
    same = (ri // CHUNK) == (ci // CHUNK)
    return (same & (ci <= ri)).astype(F32), (same & (ci >= ri)).astype(F32)


def _prep_c(name, proj, pvec, lay, tr=512):
    s, nh = proj.shape[0], lay.nh
    tr = _tile(s, tr, CHUNK)
    assert 6 * nh <= LANES

    def body(x_ref, p_ref, o_ref):
        x = x_ref[...]
        lane = lax.broadcasted_iota(jnp.int32, x.shape, 1)
        is_g = (lane >= 2 * nh) & (lane < 4 * nh)
        g = jnp.where(is_g, -jnp.exp(p_ref[0:1, :]) * _softplus(x + p_ref[1:2, :]), 0.0)
        m_f, m_b = _chunk_cumsum_masks(tr)
        gc = jnp.where(lane < 3 * nh, _exact_nn(m_f, g), _exact_nn(m_b, g))
        gc = pltpu.roll(gc, 2 * nh, 1)
        o_ref[...] = jnp.where(lane < 2 * nh, jax.nn.sigmoid(x), jnp.where(lane < 4 * nh, g, gc))

    return pl.pallas_call(
        body, grid=(s // tr,), in_specs=[_row(tr, LANES, lay.ba // LANES), _vec(8, LANES)],
        out_specs=_row(tr, LANES), out_shape=SDS((s, LANES), F32),
        compiler_params=_params("parallel"), name=name)(proj, pvec)


def _prep_c_bwd(name, dbg_f, dbg_b, proj, pvec, dproj, lay, tr=512):
    s, nh, tail = proj.shape[0], lay.nh, lay.tail
    tr = _tile(s, tr, CHUNK)

    def body(x_ref, df_ref, db_ref, p_ref, _, o_ref, acc_ref):
        @pl.when(pl.program_id(0) == 0)
        def _():
            acc_ref[...] = jnp.zeros_like(acc_ref)

        x = x_ref[...]
        lane = lax.broadcasted_iota(jnp.int32, x.shape, 1)
        is_b, is_g = lane < 2 * nh, (lane >= 2 * nh) & (lane < 4 * nh)
        fwd_lane = (lane < nh) | ((lane >= 2 * nh) & (lane < 3 * nh))
        d = jnp.where(lane < 4 * nh, jnp.where(fwd_lane, df_ref[...], db_ref[...]), 0.0)
        m_f, m_b = _chunk_cumsum_masks(tr)
        dgc = jnp.where(is_g, d, 0.0)
        dg = jnp.where(fwd_lane, _exact_nn(m_b, dgc), _exact_nn(m_f, dgc))
        sb = jax.nn.sigmoid(x)
        na = -jnp.exp(p_ref[0:1, :])
        xs = x + p_ref[1:2, :]
        dsp = dg * na * jax.nn.sigmoid(xs)
        dx = jnp.where(is_b, d * sb * (1.0 - sb), jnp.where(is_g, dsp, 0.0))
        o_ref[...] = jnp.zeros_like(o_ref)
        o_ref[:, 0:LANES] = dx.astype(BF16)
        acc_ref[0:1, :] += jnp.sum(jnp.where(is_g, dg * na * _softplus(xs), 0.0), axis=0, keepdims=True)
        acc_ref[1:2, :] += jnp.sum(jnp.where(is_g, dsp, 0.0), axis=0, keepdims=True)

    return pl.pallas_call(
        body, grid=(s // tr,),
        in_specs=[_row(tr, LANES, lay.ba // LANES), _row(tr, LANES), _row(tr, LANES), _vec(8, LANES),
                  pl.BlockSpec(memory_space=pl.ANY)],
        out_specs=[_row(tr, tail, lay.ba // tail), _vec(8, LANES)],
        out_shape=[SDS(dproj.shape, dproj.dtype), SDS((8, LANES), F32)], input_output_aliases={4: 0},
        compiler_params=_params("arbitrary"), name=name)(proj, dbg_f, dbg_b, pvec, dproj)


def _post(name, o_f, o_b, proj, dn_w, lay, tr=256):
    s, qk, nh = o_f.shape[0], lay.qk, lay.nh
    tr = _tile(s, tr, 8)

    def body(f_ref, b_ref, z_ref, w_ref, o_ref):
        o = f_ref[...] + b_ref[...]
        gate = jax.nn.silu(z_ref[...])
        for hd in range(nh):
            sl = slice(hd * HEAD, (hd + 1) * HEAD)
            oh = o[:, sl]
            r = lax.rsqrt(jnp.mean(oh * oh, axis=-1, keepdims=True) + EPS)
            o_ref[:, sl] = (oh * r * w_ref[...] * gate[:, sl]).astype(BF16)

    return pl.pallas_call(
        body, grid=(s // tr,),
        in_specs=[_row(tr, qk), _row(tr, qk), _row(tr, qk, lay.z // qk), _vec(1, HEAD)],
        out_specs=_row(tr, qk), out_shape=SDS((s, qk), BF16),
        compiler_params=_params("parallel"), name=name)(o_f, o_b, proj, dn_w)


def _post_bwd(name, dyb, o_f, o_b, proj, dn_w, dproj, lay, tr=256):
    s, qk, nh = o_f.shape[0], lay.qk, lay.nh
    tr = _tile(s, tr, 8)

    def body(d_ref, f_ref, b_ref, z_ref, w_ref, _, do_ref, dz_ref, acc_ref):
        @pl.when(pl.program_id(0) == 0)
        def _():
            acc_ref[...] = jnp.zeros_like(acc_ref)

        o, z, d, wv = f_ref[...] + b_ref[...], z_ref[...], d_ref[...], w_ref[...]
        gate = jax.nn.silu(z)
        dgate = _silu_grad(z)
        for hd in range(nh):
            sl = slice(hd * HEAD, (hd + 1) * HEAD)
            oh, dh = o[:, sl], d[:, sl]
            r = lax.rsqrt(jnp.mean(oh * oh, axis=-1, keepdims=True) + EPS)
            n = oh * r
            dz_ref[:, sl] = (dh * n * wv * dgate[:, sl]).astype(BF16)
            don = dh * gate[:, sl]
            acc_ref[0:1, :] += jnp.sum(don * n, axis=0, keepdims=True)
            dn = don * wv
            do_ref[:, sl] = r * (dn - n * jnp.mean(dn * n, axis=-1, keepdims=True))

    return pl.pallas_call(
        body, grid=(s // tr,),
        in_specs=[_row(tr, qk), _row(tr, qk), _row(tr, qk), _row(tr, qk, lay.z // qk), _vec(1, HEAD),
                  pl.BlockSpec(memory_space=pl.ANY)],
        out_specs=[_row(tr, qk), _row(tr, qk, lay.z // qk), _vec(8, HEAD)],
        out_shape=[SDS((s, qk), F32), SDS(dproj.shape, dproj.dtype), SDS((8, HEAD), F32)],
        input_output_aliases={5: 1},
        compiler_params=_params("arbitrary"), name=name)(dyb, o_f, o_b, proj, dn_w, dproj)


def _b_out_merge(name, yb, w_b, pa, proj, lay, tr=512):
    s, d = pa.shape
    k = yb.shape[1]
    tr = _tile(s, tr, 16)

    def body(y_ref, w_ref, a_ref, g_ref, pb_ref, o_ref):
        pb = _nn(y_ref[...], w_ref[...])
        pb_ref[...] = pb
        o_ref[...] = (jax.nn.sigmoid(g_ref[:, 0:d]) * a_ref[...] + jax.nn.sigmoid(g_ref[:, d:2 * d]) * pb).astype(BF16)

    return pl.pallas_call(
        body, grid=(s // tr,),
        in_specs=[_row(tr, k), _vec(k, d), _row(tr, d), _row(tr, 2 * d, lay.ga // (2 * d))],
        out_specs=[_row(tr, d), _row(tr, d)], out_shape=[SDS((s, d), F32), SDS((s, d), BF16)],
        compiler_params=_params("parallel"), name=name)(yb, w_b, pa, proj)


def _merge_bwd(name, dmg, pa, pb, proj, lay, tr=512):
    s, d = pa.shape
    tr = _tile(s, tr, 8)

    def body(d_ref, a_ref, b_ref, g_ref, da_ref, db_ref, dg_ref):
        dm = d_ref[...]
        sa, sb = jax.nn.sigmoid(g_ref[:, 0:d]), jax.nn.sigmoid(g_ref[:, d:2 * d])
        da_ref[...] = (sa * dm).astype(BF16)
        db_ref[...] = (sb * dm).astype(BF16)
        dg_ref[:, 0:d] = (dm * a_ref[...] * sa * (1.0 - sa)).astype(BF16)
        dg_ref[:, d:2 * d] = (dm * b_ref[...] * sb * (1.0 - sb)).astype(BF16)

    return pl.pallas_call(
        body, grid=(s // tr,),
        in_specs=[_row(tr, d), _row(tr, d), _row(tr, d), _row(tr, 2 * d, lay.ga // (2 * d))],
        out_specs=[_row(tr, d), _row(tr, d), _row(tr, 2 * d, lay.ga // (2 * d))],
        out_shape=[SDS((s, d), BF16), SDS((s, d), BF16), SDS((s, lay.total), BF16)],
        compiler_params=_params("parallel"), name=name)(dmg, pa, pb, proj)


def _tri_inverse(a_mat, ri, ci):
    def same(shift):
        return (ri >> shift) == (ci >> shift)

    x = -jnp.where(same(3), a_mat, 0.0)
    t_mat = (ri == ci).astype(F32) + x
    for _ in range(2):
        x = _bnn(x, x)
        t_mat = t_mat + _bnn(t_mat, x)
    for shift in (3, 4, 5):
        b = jnp.where(same(shift + 1) & ~same(shift), a_mat, 0.0)
        t_mat = t_mat - _bnn(_bnn(t_mat, b), t_mat)
    return t_mat


def _chunk_terms(q, k, v, beta, gc, g_row, g_last, reverse, t_mat=None):
    c = CHUNK
    ri = lax.broadcasted_iota(jnp.int32, (c, c), 0)
    ci = lax.broadcasted_iota(jnp.int32, (c, c), 1)
    if reverse:
        incl, strict = ri <= ci, ri < ci
    else:
        incl, strict = ri >= ci, ri > ci
    decay = jnp.where(incl, jnp.exp(jnp.where(incl, gc - g_row, 0.0)), 0.0)
    e = jnp.exp(gc)
    ed = jnp.exp(g_last - gc)
    el = jnp.exp(g_last)
    kb = k * beta
    kk_qk = _bnt(jnp.concatenate([kb, q], axis=1), k)
    a_mat = jnp.where(strict, kk_qk[:, 0:c] * decay, 0.0)
    p_mat = jnp.where(incl, kk_qk[:, c:2 * c] * decay, 0.0)
    if t_mat is None:
        t_mat = _tri_inverse(a_mat, ri, ci)
    uw = _bnn(t_mat, jnp.concatenate([v * beta, kb * e], axis=2))
    return dict(incl=incl, strict=strict, decay=decay, e=e, ed=ed, el=el, kb=kb,
                a=a_mat, t=t_mat, uw=uw, u=uw[:, :, 0:HEAD], w=uw[:, :, HEAD:2 * HEAD], p=p_mat)


def _delta_specs(nh, tb, nb, reverse):
    tok = (lambda i: nb - 1 - i) if reverse else (lambda i: i)
    hw = nh * HEAD
    qkv = [pl.BlockSpec((tb, hw), functools.partial(lambda i, part: (tok(i), part), part=p)) for p in range(3)]
    rows = pl.BlockSpec((tb, hw), lambda i: (tok(i), 0))
    bg = pl.BlockSpec((tb, LANES), lambda i: (tok(i), 0))
    gct = pl.BlockSpec((2 * nh, tb), lambda i: (0, tok(i)))
    st = pl.BlockSpec((nh, tb // CHUNK, HEAD, HEAD), lambda i: (0, tok(i), 0, 0))
    tri = pl.BlockSpec((nh, tb // CHUNK, CHUNK, CHUNK), lambda i: (0, tok(i), 0, 0))
    return qkv, rows, bg, gct, st, tri


def _heads(ref, rows, nh):
    return jnp.stack([ref[rows, hd * HEAD:(hd + 1) * HEAD] for hd in range(nh)])


def _chunk_scalars(bg_ref, gct_ref, cj, nh, tb, reverse):
    rows = pl.ds(cj * CHUNK, CHUNK)
    lb = nh if reverse else 0
    lc = 4 * nh + lb
    last = cj * CHUNK + (0 if reverse else CHUNK - 1)
    g_lanes = gct_ref[lb:lb + nh, :]
    if cj:
        g_lanes = pltpu.roll(g_lanes, tb - cj * CHUNK, 1)
    col = lambda l0, r: jnp.stack([bg_ref[r, l0 + hd:l0 + hd + 1] for hd in range(nh)])
    return col(lb, rows), col(lc, rows), g_lanes[:, 0:CHUNK][:, None, :], col(lc, pl.ds(last, 1))


def _delta_fwd(name, qkvn, bg, gct, nh, reverse, tb=256):
    s = qkvn.shape[0]
    tb = _tile(s, tb, LANES)
    nb, cpb = s // tb, tb // CHUNK
    qkv, rows_spec, bg_spec, gct_spec, st, tri = _delta_specs(nh, tb, nb, reverse)

    def body(q_ref, k_ref, v_ref, bg_ref, gct_ref, o_ref, st_ref, tri_ref, state):
        @pl.when(pl.program_id(0) == 0)
        def _():
            state[...] = jnp.zeros_like(state)

        for cj in (range(cpb - 1, -1, -1) if reverse else range(cpb)):
            rows = pl.ds(cj * CHUNK, CHUNK)
            q, k, v = _heads(q_ref, rows, nh), _heads(k_ref, rows, nh), _heads(v_ref, rows, nh)
            tm = _chunk_terms(q, k, v, *_chunk_scalars(bg_ref, gct_ref, cj, nh, tb, reverse), reverse)
            s_in = state[...]
            st_ref[:, cj] = s_in
            tri_ref[:, cj] = tm["t"]
            ws_qs = _bnn(jnp.concatenate([tm["w"], q * tm["e"]], axis=1), s_in)
            vn = tm["u"] - ws_qs[:, 0:CHUNK]
            o = ws_qs[:, CHUNK:2 * CHUNK] + _bnn(tm["p"], vn)
            for hd in range(nh):
                o_ref[rows, hd * HEAD:(hd + 1) * HEAD] = o[hd]
            state[...] = s_in * tm["el"] + _btn(k * tm["ed"], vn)

    return pl.pallas_call(
        body, grid=(nb,), in_specs=qkv + [bg_spec, gct_spec], out_specs=[rows_spec, st, tri],
        out_shape=[SDS((s, nh * HEAD), F32), SDS((nh, s // CHUNK, HEAD, HEAD), F32),
                   SDS((nh, s // CHUNK, CHUNK, CHUNK), F32)],
        scratch_shapes=[pltpu.VMEM((nh, HEAD, HEAD), F32)],
        compiler_params=_params("arbitrary"), name=name)(qkvn, qkvn, qkvn, bg, gct)


def _delta_bwd(name, qkvn, bg, gct, do, states, tris, nh, reverse, add=None, tb=128, ex=None):
    s = qkvn.shape[0]
    tb = _tile(s, tb, LANES)
    nb, cpb = s // tb, tb // CHUNK
    qkv, rows_spec, bg_spec, gct_spec, st, tri = _delta_specs(nh, tb, nb, not reverse)
    n_add = 0 if add is None else 3
    host = _Hosted(ex, name, (nb,))

    def body(*refs):
        q_ref, k_ref, v_ref, bg_ref, gct_ref, do_ref, st_ref, tri_ref = refs[0:8]
        add_refs = refs[8:8 + n_add]
        (dq_ref, dk_ref, dv_ref, dbg_ref), (dstate,) = host.split(refs[8 + n_add:], 4)
        host.start()

        @pl.when(pl.program_id(0) == 0)
        def _():
            dstate[...] = jnp.zeros_like(dstate)

        ones = jnp.ones((nh, 2 * CHUNK, HEAD), BF16)
        row_id = lax.broadcasted_iota(jnp.int32, (CHUNK, 1), 0)
        rsum = lambda x: jnp.sum(x, axis=2, keepdims=True)
        for cj in (range(cpb) if reverse else range(cpb - 1, -1, -1)):
            rows = pl.ds(cj * CHUNK, CHUNK)
            q, k, v, d_o = (_heads(r, rows, nh) for r in (q_ref, k_ref, v_ref, do_ref))
            beta, gc, g_row, g_last = _chunk_scalars(bg_ref, gct_ref, cj, nh, tb, reverse)
            tm = _chunk_terms(q, k, v, beta, gc, g_row, g_last, reverse, t_mat=tri_ref[:, cj])
            incl, strict, e, ed, el, kb = tm["incl"], tm["strict"], tm["e"], tm["ed"], tm["el"], tm["kb"]
            t_mat, u, w, p_mat, decay = tm["t"], tm["u"], tm["w"], tm["p"], tm["decay"]
            s_in, ds_out = st_ref[:, cj], dstate[...]
            cat_rows = lambda a, b: jnp.concatenate([a, b], axis=1)
            top, bot = slice(0, CHUNK), slice(CHUNK, 2 * CHUNK)
            vn = u - _bnn(w, s_in)
            qe, kd, ke = q * e, k * ed, kb * e
            dvn = _btn(p_mat, d_o) + _bnn(kd, ds_out)
            by_state = _bnt(cat_rows(d_o, dvn), s_in)
            dqe, dw = by_state[:, top], -by_state[:, bot]
            dq = dqe * e
            dgc = rsum(dqe * qe)
            dp = jnp.where(incl, _bnt(d_o, vn), 0.0)
            dkd = _bnt(vn, ds_out)
            dk = dkd * ed
            r = rsum(dkd * kd)
            dgc = dgc - r
            dg_last = (jnp.sum(r, axis=1, keepdims=True)
                       + jnp.sum(rsum(ds_out * s_in), axis=1, keepdims=True) * el)
            d_uw = _btn(t_mat, jnp.concatenate([dvn, dw], axis=2))
            dbv, dke = d_uw[:, :, 0:HEAD], d_uw[:, :, HEAD:2 * HEAD]
            da = -jnp.where(strict, _bnt(d_uw, tm["uw"]), 0.0)
            mn = cat_rows(da * decay, dp * decay)
            by_k = _bnn(mn, k)
            dkb = by_k[:, top] + dke * e
            dq = dq + by_k[:, bot]
            dk = dk + _btn(mn, cat_rows(kb, q))
            g_mat = da * tm["a"] + dp * p_mat
            g_hi, g_mid, _ = _split3(g_mat)
            col = _btn(cat_rows(g_hi, g_mid), ones)[:, :, 0:1]
            dgc = dgc + rsum(g_mat) - col + rsum(dke * ke)
            dgc = dgc + jnp.where(row_id == (0 if reverse else CHUNK - 1), dg_last, 0.0)
            dv = dbv * beta
            dbeta = rsum(dbv * v) + rsum(dkb * k)
            dk = dk + dkb * beta
            dstate[...] = el * ds_out + _btn(cat_rows(qe, w), cat_rows(d_o, -dvn))
            lb = nh if reverse else 0
            for hd in range(nh):
                cols = slice(hd * HEAD, (hd + 1) * HEAD)
                extra = [a[rows, cols] for a in add_refs] if n_add else [0.0, 0.0, 0.0]
                dq_ref[rows, cols] = dq[hd] + extra[0]
                dk_ref[rows, cols] = dk[hd] + extra[1]
                dv_ref[rows, cols] = dv[hd] + extra[2]
                dbg_ref[rows, lb + hd:lb + hd + 1] = dbeta[hd]
                dbg_ref[rows, 2 * nh + lb + hd:2 * nh + lb + hd + 1] = dgc[hd]
        host.wait()

    out3 = SDS((s, nh * HEAD), F32)
    n_in = 8 + n_add
    out = pl.pallas_call(
        body, grid=(nb,),
        in_specs=qkv + [bg_spec, gct_spec, rows_spec, st, tri] + [rows_spec] * n_add + host.in_specs,
        out_specs=[rows_spec, rows_spec, rows_spec, bg_spec] + host.out_specs,
        out_shape=[out3, out3, out3, SDS((s, LANES), F32)] + host.out_shapes,
        input_output_aliases=host.aliases(n_in, 4), scratch_shapes=[pltpu.VMEM((nh, HEAD, HEAD), F32)] + host.sems,
        compiler_params=_params("arbitrary"), name=name)(
            qkvn, qkvn, qkvn, bg, gct, do, states, tris, *(add or ()), *host.arrays)
    return host.finish(out, 4)


def _row_pieces(g):
    return g.reshape(N_CHIPS, g.shape[0] // N_CHIPS, g.shape[1])


def _up_swiglu(name, u, w_up, tm=1024, ex=None):
    s, k = u.shape
    fh = w_up.shape[2]
    tm = _tile(s, tm, 8)
    grid = (2, s // tm)
    host = _Hosted(ex, name, grid)

    def body(u_ref, wa_ref, wb_ref, *rest):
        (a_ref, b_ref, o_ref), _ = host.split(rest, 3)
        host.start()
        x = u_ref[...]
        a, b = _nn(x, wa_ref[...]), _nn(x, wb_ref[...])
        a_ref[...], b_ref[...] = a.astype(BF16), b.astype(BF16)
        o_ref[...] = (jax.nn.silu(a) * b).astype(BF16)
        host.wait()

    tile = pl.BlockSpec((tm, fh), lambda j, i: (i, j))
    out = pl.pallas_call(
        body, grid=grid,
        in_specs=[pl.BlockSpec((tm, k), lambda j, i: (i, 0)), pl.BlockSpec((None, k, fh), lambda j, i: (j, 0, 0)),
                  pl.BlockSpec((None, k, fh), lambda j, i: (2 + j, 0, 0))] + host.in_specs,
        out_specs=[tile] * 3 + host.out_specs, out_shape=[SDS((s, 2 * fh), BF16)] * 3 + host.out_shapes,
        input_output_aliases=host.aliases(3, 3), scratch_shapes=host.sems,
        compiler_params=_params("arbitrary", "arbitrary"), name=name)(u, w_up, w_up, *host.arrays)
    return host.finish(out, 3)


def _matmul_resid(name, a, w, h, g, scale, tm=512, ex=None):
    s, k = a.shape
    d = w.shape[1]
    tm = _tile(s, tm, 16)
    grid = (s // tm,)
    host = _Hosted(ex, name, grid)

    def body(a_ref, w_ref, h_ref, g_ref, *rest):
        (o_ref, f_ref), _ = host.split(rest, 2)
        host.start()
        f = _nn(a_ref[...], w_ref[...])
        f_ref[...] = f.astype(BF16)
        o_ref[...] = h_ref[...] + (scale * g_ref[...]) * f
        host.wait()

    out = pl.pallas_call(
        body, grid=grid, in_specs=[_row(tm, k), _vec(k, d), _row(tm, d), _vec(1, d)] + host.in_specs,
        out_specs=[_row(tm, d), _row(tm, d)] + host.out_specs,
        out_shape=[SDS((s, d), F32), SDS((s, d), BF16)] + host.out_shapes,
        input_output_aliases=host.aliases(4, 2), scratch_shapes=host.sems,
        compiler_params=_params("arbitrary"), name=name)(a, w, h, g, *host.arrays)
    return host.finish(out, 2)


def _down_swiglu_bwd(name, df, w_down, a_pre, b_pre, tm=512):
    s, d = df.shape
    f = w_down.shape[0]
    tm = _tile(s, tm, 16)

    def body(df_ref, w_ref, a_ref, b_ref, o_ref):
        dhm = _nt(df_ref[...], w_ref[...])
        a = a_ref[...].astype(F32)
        o_ref[:, 0:f] = (dhm * b_ref[...].astype(F32) * _silu_grad(a)).astype(BF16)
        o_ref[:, f:2 * f] = (dhm * jax.nn.silu(a)).astype(BF16)

    return pl.pallas_call(
        body, grid=(s // tm,), in_specs=[_row(tm, d), _vec(f, d), _row(tm, f), _row(tm, f)],
        out_specs=_row(tm, 2 * f), out_shape=SDS((s, 2 * f), BF16),
        compiler_params=_params("parallel"), name=name)(df, w_down, a_pre, b_pre)


def _col_pieces(w):
    return w if w.ndim == 3 else w.reshape(w.shape[0], N_CHIPS, -1).transpose(1, 0, 2)


def _ffn_fwd(tag, h, nw, sh, sc, g, w_up, w_down, ex=None):
    u = _norm_mod(tag + "_norm", h, nw, sh, sc)
    a_pre, b_pre, hm = _up_swiglu(tag + "_up", u, _col_pieces(w_up), ex=ex)
    w_down = w_down() if callable(w_down) else w_down
    h_new, f = _matmul_resid(tag + "_down", hm, w_down, h, g, 0.5, ex=ex)
    return h_new, (h, u, a_pre, b_pre, hm, f, w_down)


def _ffn_bwd(tag, dh, saved, nw, sc, g, w_up, ex=None, on_gw_down=None):
    h, u, a_pre, b_pre, hm, f, w_down = saved
    df, acc_g = _resid_bwd(tag + "_res_bwd", dh, f, g, 0.5)
    shard = hm.shape[1] // 2
    gw_down = _row_pieces(_matmul(tag + "_gw_down", hm, df, "tn", out_dtype=BF16, tm=shard, ex=ex))
    if on_gw_down:
        on_gw_down(gw_down)
    dab = _down_swiglu_bwd(tag + "_dhm", df, w_down, a_pre, b_pre)
    gw_up = _matmul(tag + "_gw_up", u, dab, "tn", out_dtype=BF16, tn=shard, out_pieces=N_CHIPS, ex=ex)
    du = _matmul(tag + "_du", dab, w_up, "nt", ex=ex)
    dh_in, acc = _norm_mod_bwd(tag + "_norm_bwd", h, du, dh, nw, sc)
    return dh_in, gw_up, gw_down, (acc[0], acc[1], acc_g[0], acc[2])


def _mixer_fwd(h, nw, sh, sc, g, wt, lay, ex=None):
    nh = lay.nh
    u = _norm_mod("mix_norm", h, nw, sh, sc)
    proj = _matmul("mix_in", u, wt["w_in"], "nn", ex=ex)
    if ex:
        wt = dict(wt, **_as_operands(ex.weights("mix"), lay))
    qkvn = _prep_b("mix_prep_b", proj, wt["conv_dn"], lay)
    ya = _prep_a("mix_prep_a", proj, wt["conv_a"], lay)
    bg = _prep_c("mix_prep_c", proj, wt["pvec"], lay)
    gct = bg[:, 4 * nh:6 * nh].T
    o_f, *st_f = _delta_fwd("delta_fwd_l2r", qkvn, bg, gct, nh, False)
    o_b, *st_b = _delta_fwd("delta_fwd_r2l", qkvn, bg, gct, nh, True)
    yb = _post("mix_post", o_f, o_b, proj, wt["dn_norm"], lay)
    pa = _matmul("mix_a_out", ya, wt["w_a_out"], "nn")
    pb, mg = _b_out_merge("mix_b_out", yb, wt["w_b_out"], pa, proj, lay)
    h2, y = _matmul_resid("mix_out", mg, wt["w_out"], h, g, 1.0, ex=ex)
    return h2, (h, u, proj, qkvn, ya, bg, gct, o_f, o_b, st_f, st_b, yb, pa, pb, mg, y)


def _mixer_bwd(dh, saved, nw, sc, g, wt, lay, ex=None):
    h, u, proj, qkvn, ya, bg, gct, o_f, o_b, st_f, st_b, yb, pa, pb, mg, y = saved
    nh = lay.nh
    dy, acc_g = _resid_bwd("mix_res_bwd", dh, y, g, 1.0)
    gw_out = _matmul("mix_gw_out", mg, dy, "tn", out_dtype=BF16, ex=ex)
    dmg = _matmul("mix_dmg", dy, wt["w_out"], "nt")
    dpa, dpb, dproj = _merge_bwd("mix_merge_bwd", dmg, pa, pb, proj, lay)
    gw_a = _matmul("mix_gw_a", ya, dpa, "tn", out_dtype=BF16, out_pieces=N_CHIPS)
    gw_b = _matmul("mix_gw_b", yb, dpb, "tn", out_dtype=BF16)
    dya = _matmul("mix_dya", dpa, wt["w_a_out"], "nt")
    dyb = _matmul("mix_dyb", dpb, wt["w_b_out"], "nt")
    do, dproj, acc_dn = _post_bwd("mix_post_bwd", dyb, o_f, o_b, proj, wt["dn_norm"], dproj, lay)
    dq, dk, dv, dbg_f = _delta_bwd("delta_bwd_l2r", qkvn, bg, gct, do, *st_f, nh, False, ex=ex)
    dq, dk, dv, dbg_b = _delta_bwd("delta_bwd_r2l", qkvn, bg, gct, do, *st_b, nh, True, add=(dq, dk, dv), ex=ex)
    dproj, acc_ca = _prep_a_bwd("mix_prep_a_bwd", dya, proj, wt["conv_a"], dproj, lay)
    dproj, acc_cd = _prep_b_bwd("mix_prep_b_bwd", dq, dk, dv, proj, wt["conv_dn"], dproj, lay)
    dproj, acc_pc = _prep_c_bwd("mix_prep_c_bwd", dbg_f, dbg_b, proj, wt["pvec"], dproj, lay)
    gw_in = lay.unperm_cols(_matmul("mix_gw_in", u, dproj, "tn", out_dtype=BF16))
    gw_in = gw_in.reshape(gw_in.shape[0], N_CHIPS, -1).transpose(1, 0, 2)
    du = _matmul("mix_du", dproj, wt["w_in"], "nt")
    dh_in, acc = _norm_mod_bwd("mix_norm_bwd", h, du, dh, nw, sc)
    small = dict(conv_a=acc_ca[0:wt["conv_a"].shape[0]], conv_dn=acc_cd[0:wt["conv_dn"].shape[0]],
                 dn_norm=acc_dn[0:1], a_log=acc_pc[0], dt_bias=acc_pc[1])
    big = dict(w_in=gw_in, w_a_out=gw_a, w_b_out=_row_pieces(gw_b), w_out=_row_pieces(gw_out))
    return dh_in, big, small, (acc[0], acc[1], acc_g[0], acc[2])


def _as_operands(gathered, lay):
    wt = {}
    for n, g in gathered.items():
        if n == "w_in":
            wt[n] = lay.perm_cols(jnp.concatenate(list(g), axis=1))
        else:
            wt[n] = g if n in COL_SHARDED else g.reshape(-1, g.shape[-1])
    return wt


def _local_step(x, tgt, modv, wt, lay, ex=None):
    m = [modv[i:i + 1] for i in range(9)]

    def ffn1_down():
        ex.run(("gather_pass", "down1"))
        return _as_operands(ex.weights("down1"), lay)["w_ffn1_down"]

    h1, sv1 = _ffn_fwd("ffn1", x, wt["norm_ffn1"], m[0], m[1], m[2], wt["w_ffn1_up"],
                       ffn1_down if ex else wt["w_ffn1_down"], ex)
    if ex:
        wt = dict(wt, **_as_operands(ex.weights("in"), lay))
    h2, sv2 = _mixer_fwd(h1, wt["norm_mix"], m[3], m[4], m[5], wt, lay, ex)
    if ex:
        wt = dict(wt, **_as_operands(ex.weights("mix", "ffn2"), lay))
    h3, sv3 = _ffn_fwd("ffn2", h2, wt["norm_ffn2"], m[6], m[7], m[8], wt["w_ffn2_up"], wt["w_ffn2_down"])
    dh3, acc_f = _final_loss("final_loss", h3, tgt, wt["norm_final"])
    loss = jnp.sum(acc_f[1])
    dh2, gu2, gd2, dm3 = _ffn_bwd("ffn2", dh3, sv3, wt["norm_ffn2"], m[7], m[8], wt["w_ffn2_up"])
    if ex:
        ex.reduce("ffn2", dict(w_ffn2_up=gu2, w_ffn2_down=gd2))
    dh1, gmix, small, dm2 = _mixer_bwd(dh2, sv2, wt["norm_mix"], m[4], m[5], wt, lay, ex)
    if ex:
        ex.reduce("mixer", gmix)
    dx, gu1, gd1, dm1 = _ffn_bwd("ffn1", dh1, sv1, wt["norm_ffn1"], m[1], m[2], wt["w_ffn1_up"], ex,
                                 (lambda g: ex.reduce("down1", dict(w_ffn1_down=g))) if ex else None)
    dmod = jnp.stack([dm1[0], dm1[1], dm1[2], dm2[0], dm2[1], dm2[2], dm3[0], dm3[1], dm3[2]])
    big = dict(w_ffn1_up=gu1, w_ffn1_down=gd1, w_ffn2_up=gu2, w_ffn2_down=gd2, **gmix)
    small = dict(small, norm_ffn1=dm1[3], norm_mix=dm2[3], norm_ffn2=dm3[3], norm_final=acc_f[0])
    return loss, dx, dmod, big, small


def _position():
    return lax.axis_index("x"), lax.axis_index("y"), lax.axis_index("c")


_ANY = pl.BlockSpec(memory_space=pl.ANY)
_VMEM = pl.BlockSpec(memory_space=pltpu.VMEM)


def _allgather8(name, v):
    r = v.shape[0]

    def body(v_ref, out_ref, send_sems, recv_sems):
        x, y, c = _position()
        me = 4 * x + 2 * y + c
        out_ref[me] = v_ref[...]
        copies = []
        for mask in range(1, N_DEV):
            peer = tuple(1 - p if mask >> b & 1 else p for p, b in ((x, 2), (y, 1), (c, 0)))
            cp = pltpu.make_async_remote_copy(
                src_ref=v_ref, dst_ref=out_ref.at[me], send_sem=send_sems.at[mask - 1],
                recv_sem=recv_sems.at[mask - 1], device_id=peer, device_id_type=MESH)
            cp.start()
            copies.append(cp)
        for cp in copies:
            cp.wait()

    return pl.pallas_call(
        body, in_specs=[_VMEM], out_specs=_VMEM, out_shape=SDS((N_DEV, r, LANES), F32),
        scratch_shapes=[pltpu.SemaphoreType.DMA((N_DEV - 1,)), pltpu.SemaphoreType.DMA((N_DEV - 1,))],
        name=name)(v)


def _other_chips(x, y):
    return [(1 - x, y), (x, 1 - y), (1 - x, 1 - y)]


def _half_rows(c, rows):
    hr = rows // 2
    assert hr % 16 == 0
    return pl.ds(pl.multiple_of(c * hr, 16), hr)


class _Stage:
    def __init__(self, arrays, out_shapes, sems, plan, in_place=False):
        self.arrays, self.out_shapes, self.sems, self.plan = list(arrays), list(out_shapes), list(sems), plan
        self.alias_pairs = [(i, i) for i in range(len(self.arrays))] if in_place else []
        self.out_counts = [len(self.out_shapes)]

    def start(self, ins, outs, sems):
        for kind, cp in self.plan(ins, outs, sems):
            if kind != "recv":
                cp.start()

    def wait(self, ins, outs, sems):
        for kind, cp in self.plan(ins, outs, sems):
            {"local": cp.wait, "both": cp.wait, "send": cp.wait_send, "recv": cp.wait_recv}[kind]()

    def aliases(self, in_offset, out_offset):
        return {in_offset + i: out_offset + o for i, o in self.alias_pairs}


def _join(stages):
    ni = [0] + [len(st.arrays) for st in stages]
    no = [0] + [len(st.out_shapes) for st in stages]
    ns = [0] + [len(st.sems) for st in stages]
    for counts in (ni, no, ns):
        for k in range(1, len(counts)):
            counts[k] += counts[k - 1]

    def plan(ins, outs, sems):
        todo = []
        for k, st in enumerate(stages):
            todo += st.plan(ins[ni[k]:ni[k + 1]], outs[no[k]:no[k + 1]], sems[ns[k]:ns[k + 1]])
        return todo

    joined = _Stage([a for st in stages for a in st.arrays], [o for st in stages for o in st.out_shapes],
                    [m for st in stages for m in st.sems], plan)
    joined.alias_pairs = [(ni[k] + i, no[k] + o) for k, st in enumerate(stages) for i, o in st.alias_pairs]
    joined.out_counts = [len(st.out_shapes) for st in stages]
    return joined


def _run_stage(name, stage):
    n_in, n_out = len(stage.arrays), len(stage.out_shapes)

    def body(*refs):
        ins, outs, sems = refs[0:n_in], refs[n_in:n_in + n_out], refs[n_in + n_out:]
        stage.start(ins, outs, sems)
        stage.wait(ins, outs, sems)

    return pl.pallas_call(
        body, in_specs=[_ANY] * n_in, out_specs=[_ANY] * n_out, out_shape=stage.out_shapes,
        input_output_aliases=stage.aliases(0, 0), scratch_shapes=stage.sems, name=name)(*stage.arrays)


def _dma_sems(*counts):
    return [pltpu.SemaphoreType.DMA((n,)) for n in counts]


def _gather_send(shards):
    nw = len(shards)

    def plan(ins, outs, sems):
        send_sems, recv_sems, local_sems = sems
        x, y, c = _position()
        p = 2 * x + y
        todo = [("local", pltpu.make_async_copy(ins[w], outs[w].at[p], local_sems.at[w])) for w in range(nw)]
        for j, (cx, cy) in enumerate(_other_chips(x, y)):
            for w in range(nw):
                half = _half_rows(c, ins[w].shape[0])
                sem = dict(send_sem=send_sems.at[j * nw + w], recv_sem=recv_sems.at[j * nw + w], device_id_type=MESH)
                todo.append(("send", pltpu.make_async_remote_copy(
                    src_ref=ins[w].at[half], dst_ref=outs[w].at[p, half], device_id=(cx, cy, c), **sem)))
                landing = outs[w].at[2 * cx + cy, half]
                todo.append(("recv", pltpu.make_async_remote_copy(
                    src_ref=landing, dst_ref=landing, device_id=(x, y, c), **sem)))
        return todo

    return _Stage(shards, [SDS((N_CHIPS,) + v.shape, v.dtype) for v in shards], _dma_sems(3 * nw, 3 * nw, nw), plan)


def _gather_pass(gathered):
    nw = len(gathered)

    def plan(ins, outs, sems):
        send_sems, recv_sems = sems
        x, y, c = _position()
        todo = []
        for j, (cx, cy) in enumerate(_other_chips(x, y)):
            for w in range(nw):
                sem = dict(send_sem=send_sems.at[j * nw + w], recv_sem=recv_sems.at[j * nw + w], device_id_type=MESH)
                mine = outs[w].at[2 * cx + cy, _half_rows(c, outs[w].shape[1])]
                theirs = outs[w].at[2 * cx + cy, _half_rows(1 - c, outs[w].shape[1])]
                todo.append(("send", pltpu.make_async_remote_copy(
                    src_ref=mine, dst_ref=mine, device_id=(x, y, 1 - c), **sem)))
                todo.append(("recv", pltpu.make_async_remote_copy(
                    src_ref=theirs, dst_ref=theirs, device_id=(x, y, c), **sem)))
        return todo

    return _Stage(gathered, [SDS(g.shape, g.dtype) for g in gathered], _dma_sems(3 * nw, 3 * nw), plan, in_place=True)


def _swap_halves(gs):
    nw = len(gs)

    def plan(ins, outs, sems):
        x, y, c = _position()
        return [("both", pltpu.make_async_remote_copy(
            src_ref=ins[w].at[:, _half_rows(1 - c, ins[w].shape[1])], dst_ref=outs[w], send_sem=sems[0].at[w],
            recv_sem=sems[1].at[w], device_id=(x, y, 1 - c), device_id_type=MESH)) for w in range(nw)]

    return _Stage(gs, [SDS((g.shape[0], g.shape[1] // 2, g.shape[2]), g.dtype) for g in gs], _dma_sems(nw, nw), plan)


def _scatter_chips(vs):
    nw = len(vs)

    def plan(ins, outs, sems):
        x, y, c = _position()
        return [("both", pltpu.make_async_remote_copy(
            src_ref=ins[w].at[2 * cx + cy], dst_ref=outs[w].at[j], send_sem=sems[0].at[j * nw + w],
            recv_sem=sems[1].at[j * nw + w], device_id=(cx, cy, c), device_id_type=MESH))
            for j, (cx, cy) in enumerate(_other_chips(x, y)) for w in range(nw)]

    return _Stage(vs, [SDS((N_CHIPS - 1,) + v.shape[1:], v.dtype) for v in vs], _dma_sems(3 * nw, 3 * nw), plan)


def _share_halves(fulls):
    nw = len(fulls)

    def plan(ins, outs, sems):
        x, y, c = _position()
        todo = []
        for w in range(nw):
            rows = outs[w].at[_half_rows(c, outs[w].shape[0])]
            todo.append(("both", pltpu.make_async_remote_copy(
                src_ref=rows, dst_ref=rows, send_sem=sems[0].at[w], recv_sem=sems[1].at[w],
                device_id=(x, y, 1 - c), device_id_type=MESH)))
        return todo

    return _Stage(fulls, [SDS(f.shape, f.dtype) for f in fulls], _dma_sems(nw, nw), plan, in_place=True)


class _Hosted:
    def __init__(self, ex, name, grid):
        self.ex, self.name, self.grid = ex, name, grid
        self.stage = ex.host(name) if ex is not None else None
        st = self.stage
        self.arrays = list(st.arrays) if st else []
        self.out_shapes = list(st.out_shapes) if st else []
        self.sems = list(st.sems) if st else []
        self.in_specs, self.out_specs = [_ANY] * len(self.arrays), [_ANY] * len(self.out_shapes)

    def aliases(self, in_offset, out_offset):
        return self.stage.aliases(in_offset, out_offset) if self.stage else {}

    def split(self, rest, n_out):
        ni, no, ns = len(self.arrays), len(self.out_shapes), len(self.sems)
        self.ins, self.outs = rest[0:ni], rest[ni + n_out:ni + n_out + no]
        tail = rest[ni + n_out + no:]
        self.sem_refs = tail[len(tail) - ns:]
        return rest[ni:ni + n_out], tail[0:len(tail) - ns]

    def _at(self, last):
        conds = [pl.program_id(d) == (g - 1 if last else 0) for d, g in enumerate(self.grid)]
        return functools.reduce(lambda p, q: p & q, conds)

    def start(self):
        if self.stage:
            pl.when(self._at(False))(lambda: self.stage.start(self.ins, self.outs, self.sem_refs))

    def wait(self):
        if self.stage:
            pl.when(self._at(True))(lambda: self.stage.wait(self.ins, self.outs, self.sem_refs))

    def finish(self, out, n_out):
        out = list(out)
        if self.stage:
            self.ex.done(self.name, out[n_out:])
        return out[0:n_out]


GATHER = {"up1": ("w_ffn1_up",), "down1": ("w_ffn1_down",), "in": ("w_in",),
          "mix": ("w_a_out", "w_b_out", "w_out"), "ffn2": ("w_ffn2_up", "w_ffn2_down")}
REDUCE = {"ffn2": ("w_ffn2_up", "w_ffn2_down"), "mixer": ("w_in", "w_a_out", "w_b_out", "w_out"),
          "down1": ("w_ffn1_down",), "up1": ("w_ffn1_up",)}
HOSTS = {"ffn1_up": [("gather_send", "in")],
         "adamw_w_ada": [("share", "down1"), ("swap", "up1")], "adamw_w_in": [("scatter", "up1")],
         "adamw_w_ffn2_up": [("share", "up1")],
         "ffn1_down": [("gather_pass", "in"), ("gather_send", "mix")],
         "mix_in": [("gather_pass", "mix"), ("gather_send", "ffn2")],
         "mix_out": [("gather_pass", "ffn2")],
         "mix_gw_out": [("swap", "ffn2")], "delta_bwd_l2r": [("scatter", "ffn2")], "delta_bwd_r2l": [("share", "ffn2")],
         "ffn1_gw_down": [("swap", "mixer")], "ffn1_gw_up": [("scatter", "mixer"), ("swap", "down1")],
         "ffn1_du": [("share", "mixer"), ("scatter", "down1")]}


class _Exchange:
    def __init__(self, shards):
        self.shards = shards
        self.gathered, self.red, self.ready, self.reduced = {}, {}, {}, {}

    def _stage(self, kind, group):
        if kind == "gather_send":
            return _gather_send([self.shards[n] for n in GATHER[group]])
        if kind == "gather_pass":
            return _gather_pass(self.gathered[group])
        st = self.red[group]
        return {"swap": lambda: _swap_halves(st["parts"]), "scatter": lambda: _scatter_chips(st["chip_sums"]),
                "share": lambda: _share_halves(st["fulls"])}[kind]()

    def _done(self, kind, group, outs):
        if kind == "gather_send":
            self.gathered[group] = list(outs)
        elif kind == "gather_pass":
            self.ready.update(zip(GATHER[group], outs))
        elif kind == "swap":
            st = self.red[group]
            st["from_sib"] = list(outs)
            st["chip_sums"] = [_chip_sum("chip_sum_" + n, g, f) for n, g, f in zip(REDUCE[group], st["parts"], outs)]
        elif kind == "scatter":
            st = self.red[group]
            st["fulls"] = [_total("total_" + n, g, f, r)
                           for n, g, f, r in zip(REDUCE[group], st["parts"], st["from_sib"], outs)]
        else:
            self.reduced.update(zip(REDUCE[group], outs))

    def _all(self, steps, runner):
        stage = _join([self._stage(kind, group) for kind, group in steps])
        outs = list(runner(stage))
        for (kind, group), n in zip(steps, stage.out_counts):
            self._done(kind, group, outs[0:n])
            outs = outs[n:]

    def host(self, kernel_name):
        return _join([self._stage(*step) for step in HOSTS[kernel_name]]) if kernel_name in HOSTS else None

    def done(self, kernel_name, outs):
        self._all(HOSTS[kernel_name], lambda stage: outs)

    def run(self, *steps):
        name = "_".join(kind + "_" + group for kind, group in steps)
        self._all(steps, lambda stage: _run_stage(name, stage))

    def weights(self, *groups):
        return {n: self.ready[n] for g in groups for n in GATHER[g]}

    def reduce(self, group, parts):
        self.red[group] = dict(parts=[parts[n] for n in REDUCE[group]])


def _chip_sum(name, g, from_sib):
    _, r, cdim = g.shape
    hr = r // 2

    def body(g_ref, s_ref, o_ref):
        o_ref[...] = (g_ref[...].astype(F32) + s_ref[...].astype(F32)).astype(o_ref.dtype)

    blk = (None, hr, cdim)
    return pl.pallas_call(
        body, grid=(N_CHIPS,),
        in_specs=[pl.BlockSpec(blk, lambda j: (j, lax.axis_index("c"), 0)), pl.BlockSpec(blk, lambda j: (j, 0, 0))],
        out_specs=pl.BlockSpec(blk, lambda j: (j, 0, 0)),
        out_shape=SDS((N_CHIPS, hr, cdim), g.dtype), compiler_params=_params("parallel"), name=name)(g, from_sib)


def _total(name, g, from_sib, from_chips):
    _, r, cdim = g.shape
    hr = r // 2
    tr = _tile(hr, 256, 16)
    nt = hr // tr

    def body(g_ref, s_ref, rc_ref, o_ref):
        acc = g_ref[...].astype(F32) + s_ref[...].astype(F32)
        for j in range(N_CHIPS - 1):
            acc = acc + rc_ref[j].astype(F32)
        o_ref[...] = acc

    def chip():
        return 2 * lax.axis_index("x") + lax.axis_index("y")

    blk = (None, tr, cdim)
    return pl.pallas_call(
        body, grid=(nt,),
        in_specs=[pl.BlockSpec(blk, lambda i: (chip(), lax.axis_index("c") * nt + i, 0)),
                  pl.BlockSpec(blk, lambda i: (chip(), i, 0)),
                  pl.BlockSpec((N_CHIPS - 1, tr, cdim), lambda i: (0, i, 0))],
        out_specs=pl.BlockSpec((tr, cdim), lambda i: (lax.axis_index("c") * nt + i, 0)),
        out_shape=SDS((r, cdim), F32), compiler_params=_params("parallel"), name=name)(g, from_sib, from_chips)


def _sum8(name, v):
    _, r, w = v.shape

    def body(v_ref, o_ref):
        acc = v_ref[0]
        for j in range(1, N_DEV):
            acc = acc + v_ref[j]
        o_ref[...] = acc

    return pl.pallas_call(body, in_specs=[_VMEM], out_specs=_VMEM, out_shape=SDS((r, w), F32), name=name)(v)


def _adamw(name, w, g, m, v, tr=256, ex=None):
    r, cdim = w.shape
    tr = _tile(r, tr, 8)
    bc1, bc2 = 1.0 - ADAM_B1 ** ADAM_STEP, 1.0 - ADAM_B2 ** ADAM_STEP
    grid = (r // tr,)
    host = _Hosted(ex, name, grid)

    def body(w_ref, g_ref, m_ref, v_ref, *rest):
        (d_ref, nm_ref, nv_ref), _ = host.split(rest, 3)
        host.start()
        g = g_ref[...]
        m2 = ADAM_B1 * m_ref[...] + (1.0 - ADAM_B1) * g
        v2 = ADAM_B2 * v_ref[...] + (1.0 - ADAM_B2) * (g * g)
        d_ref[...] = -ADAM_LR * ((m2 / bc1) / (jnp.sqrt(v2 / bc2) + ADAM_EPS) + ADAM_WD * w_ref[...])
        nm_ref[...] = m2
        nv_ref[...] = v2
        host.wait()

    spec = _row(tr, cdim)
    out = SDS((r, cdim), F32)
    res = pl.pallas_call(
        body, grid=grid, in_specs=[spec] * 4 + host.in_specs, out_specs=[spec] * 3 + host.out_specs,
        out_shape=[out] * 3 + host.out_shapes, input_output_aliases=host.aliases(4, 3), scratch_shapes=host.sems,
        compiler_params=_params("arbitrary" if host.stage else "parallel"), name=name)(w, g, m, v, *host.arrays)
    return host.finish(res, 3)


def _pack_rows(arrays, width, row_mult, dtype):
    parts, spans, row = [], [], 0
    for a in arrays:
        n = a.size
        rows = -(-n // width)
        flat = a.reshape(-1).astype(dtype)
        if rows * width != n:
            flat = jnp.concatenate([flat, jnp.zeros((rows * width - n,), dtype)])
        parts.append(flat.reshape(rows, width))
        spans.append((row, rows, n, a.shape))
        row += rows
    pad = -row % row_mult
    if pad:
        parts.append(jnp.zeros((pad, width), dtype))
    return jnp.concatenate(parts, axis=0), spans


def _unpack_rows(packed, spans):
    return [packed[r0:r0 + rows].reshape(-1)[0:n].reshape(shape) for r0, rows, n, shape in spans]


BIG = ("w_ffn1_up", "w_ffn1_down", "w_in", "w_a_out", "w_b_out", "w_out", "w_ffn2_up", "w_ffn2_down")
COL_SHARDED = ("w_ffn1_up", "w_in", "w_a_out", "w_ffn2_up")
ADAM_ORDER = ("w_ada", "w_in", "w_ffn2_up", "w_a_out", "w_b_out", "w_out", "w_ffn2_down", "w_ffn1_down", "w_ffn1_up")
SMALL = ("b_ada", "norm_ffn1", "norm_mix", "conv_a", "conv_dn", "a_log_fwd", "dt_bias_fwd", "a_log_bwd",
         "dt_bias_bwd", "dn_norm", "norm_ffn2", "norm_final")
WEIGHTS = ("w_ada", "b_ada", "norm_ffn1", "w_ffn1_up", "w_ffn1_down", "norm_mix", "w_in", "conv_a", "conv_dn",
           "a_log_fwd", "dt_bias_fwd", "a_log_bwd", "dt_bias_bwd", "dn_norm", "w_a_out", "w_b_out", "w_out",
           "norm_ffn2", "w_ffn2_up", "w_ffn2_down", "norm_final")


def kernel(x, c, w_ada, b_ada, norm_ffn1, w_ffn1_up, w_ffn1_down, norm_mix, w_in, conv_a, conv_dn, a_log_fwd, dt_bias_fwd, a_log_bwd, dt_bias_bwd, dn_norm, w_a_out, w_b_out, w_out, norm_ffn2, w_ffn2_up, w_ffn2_down, norm_final, loss_target, m_w_ada, m_b_ada, m_norm_ffn1, m_w_ffn1_up, m_w_ffn1_down, m_norm_mix, m_w_in, m_conv_a, m_conv_dn, m_a_log_fwd, m_dt_bias_fwd, m_a_log_bwd, m_dt_bias_bwd, m_dn_norm, m_w_a_out, m_w_b_out, m_w_out, m_norm_ffn2, m_w_ffn2_up, m_w_ffn2_down, m_norm_final, v_w_ada, v_b_ada, v_norm_ffn1, v_w_ffn1_up, v_w_ffn1_down, v_norm_mix, v_w_in, v_conv_a, v_conv_dn, v_a_log_fwd, v_dt_bias_fwd, v_a_log_bwd, v_dt_bias_bwd, v_dn_norm, v_w_a_out, v_w_b_out, v_w_out, v_norm_ffn2, v_w_ffn2_up, v_w_ffn2_down, v_norm_final):
    given = dict(locals())
    wsh = {n: given[n] for n in WEIGHTS}
    msh = {n: given["m_" + n] for n in WEIGHTS}
    vsh = {n: given["v_" + n] for n in WEIGHTS}
    d = x.shape[-1]
    ca = conv_a.shape[-1] * N_CHIPS
    nh = conv_dn.shape[-1] * N_CHIPS // (3 * HEAD)
    lay = _Layout(d, ca, nh)
    xi, yi, ci = _position()
    chip = 2 * xi + yi
    me = 2 * chip + ci

    c_act = jax.nn.silu(c)
    g1, g1_spans = _pack_rows([c_act, conv_a[0], conv_dn[0]], LANES, 8, F32)
    g1_all = _allgather8("gather_cond", g1)
    per_dev = [_unpack_rows(g1_all[k], g1_spans) for k in range(N_DEV)]
    c_all = jnp.concatenate([p[0] for p in per_dev], axis=0)
    conv_a_full = jnp.concatenate([per_dev[2 * k][1] for k in range(N_CHIPS)], axis=1)
    conv_dn_full = jnp.concatenate([per_dev[2 * k][2] for k in range(N_CHIPS)], axis=1)

    mod_sh = _matmul("ada_mod", c_all, w_ada[0], "nn")
    b_sh = lax.dynamic_slice_in_dim(b_ada, chip * mod_sh.shape[1], mod_sh.shape[1], axis=1)
    g2, g2_spans = _pack_rows([mod_sh + b_sh], LANES, 8, F32)
    g2_all = _allgather8("gather_mod", g2)
    mod_all = jnp.concatenate([_unpack_rows(g2_all[2 * k], g2_spans)[0] for k in range(N_CHIPS)], axis=1)
    modv = lax.dynamic_index_in_dim(mod_all, me, 0, keepdims=False).reshape(9, d)

    ex = _Exchange({n: wsh[n][0].astype(BF16) for n in BIG})
    ex.run(("gather_send", "up1"))
    ex.run(("gather_pass", "up1"), ("gather_send", "down1"))
    wt = _as_operands(ex.weights("up1"), lay)
    lane_pad = (jnp.zeros((2 * nh,), F32), jnp.zeros((LANES - 4 * nh,), F32))
    pvec = jnp.stack([jnp.concatenate([lane_pad[0], a_log_fwd[0], a_log_bwd[0], lane_pad[1]]),
                      jnp.concatenate([lane_pad[0], dt_bias_fwd[0], dt_bias_bwd[0], lane_pad[1]])]
                     + [jnp.zeros((LANES,), F32)] * 6)
    wt.update(conv_a=conv_a_full, conv_dn=conv_dn_full, pvec=pvec, dn_norm=dn_norm, norm_ffn1=norm_ffn1,
              norm_mix=norm_mix, norm_ffn2=norm_ffn2, norm_final=norm_final.reshape(1, d))

    loss, dx, dmod, big, small = _local_step(x[0], loss_target[0], modv, wt, lay, ex)

    small_list = [loss.reshape(1, 1), dmod.reshape(1, 9 * d), small["norm_ffn1"], small["norm_mix"], small["conv_a"],
                  small["conv_dn"], small["a_log"][2 * nh:3 * nh], small["dt_bias"][2 * nh:3 * nh],
                  small["a_log"][3 * nh:4 * nh], small["dt_bias"][3 * nh:4 * nh], small["dn_norm"], small["norm_ffn2"],
                  small["norm_final"]]
    g3, g3_spans = _pack_rows(small_list, LANES, 8, F32)
    g3_all = _allgather8("gather_small_grads", g3)
    g_small = dict(zip(("loss",) + SMALL, _unpack_rows(_sum8("sum_small_grads", g3_all), g3_spans)))
    loss = g_small["loss"].reshape(())
    dmod_all = jnp.concatenate([_unpack_rows(g3_all[k], g3_spans)[1] for k in range(N_DEV)], axis=0)
    ncol = w_ada.shape[-1]
    dmod_sh = lax.dynamic_slice_in_dim(dmod_all, chip * ncol, ncol, axis=1)
    grads = {"w_ada": _matmul("ada_grad", c_all, dmod_sh, "tn")[None]}
    for n in SMALL:
        g = g_small[n]
        if n in ("conv_a", "conv_dn"):
            wloc = wsh[n].shape[-1]
            g = lax.dynamic_slice_in_dim(g, chip * wloc, wloc, axis=1)
        grads[n] = g.reshape(wsh[n].shape)

    ex.reduce("up1", big)

    delta, new_m, new_v = {}, {}, {}
    for n in ADAM_ORDER:
        if n in BIG:
            grads[n] = ex.reduced[n][None]
        shp = wsh[n].shape
        outs = _adamw("adamw_" + n, *(t.reshape(shp[-2], shp[-1]) for t in (wsh[n], grads[n], msh[n], vsh[n])), ex=ex)
        delta[n], new_m[n], new_v[n] = (o.reshape(shp) for o in outs)
    packed = []
    for src in (wsh, grads, msh, vsh):
        pk, s_spans = _pack_rows([src[n] for n in SMALL], LANES, 8, F32)
        packed.append(pk)
    outs = _adamw("adamw_small", *packed)
    for dst, o in zip((delta, new_m, new_v), outs):
        dst.update(zip(SMALL, _unpack_rows(o, s_spans)))

    return (loss, dx[None], *[grads[n] for n in WEIGHTS], *[delta[n] for n in WEIGHTS],
            *[new_m[n] for n in WEIGHTS], *[new_v[n] for n in WEIGHTS])
```

```python
import functools

import jax
import jax.numpy as jnp
from jax import lax
from jax.experimental import pallas as pl
from jax.experimental.pallas import tpu as pltpu

F32 = jnp.float32
BF16 = jnp.bfloat16
SDS = jax.ShapeDtypeStruct
MESH = pl.DeviceIdType.MESH
HI = lax.Precision.HIGHEST

EPS = 1e-6
HEAD = 128
CHUNK = 64
LANES = 128
N_CHIPS = 4
N_DEV = 8
VMEM_LIMIT = 56 * 1024 * 1024
FULL_K = 3072

ADAM_LR = 0.001
ADAM_B1 = 0.9
ADAM_B2 = 0.999
ADAM_EPS = 1e-08
ADAM_WD = 0.01
ADAM_STEP = 10


def _params(*sem):
    return pltpu.CompilerParams(dimension_semantics=sem, vmem_limit_bytes=VMEM_LIMIT)


def _tile(n, cap, mult=LANES):
    t = min(n, cap) // mult * mult
    while t >= mult:
        if n % t == 0:
            return t
        t -= mult
    return n


def _row(tr, w, cb=0):
    return pl.BlockSpec((tr, w), lambda i: (i, cb))


def _vec(r, w):
    return pl.BlockSpec((r, w), lambda i: (0, 0))


def _nn(a, b, **kw):
    return jnp.dot(a, b, preferred_element_type=F32, **kw)


def _nt(a, b, **kw):
    return lax.dot_general(a, b, (((1,), (1,)), ((), ())), preferred_element_type=F32, **kw)


def _tn(a, b, **kw):
    return lax.dot_general(a, b, (((0,), (0,)), ((), ())), preferred_element_type=F32, **kw)


def _bnn(a, b):
    return lax.dot_general(a, b, (((2,), (1,)), ((0,), (0,))), preferred_element_type=F32)


def _bnt(a, b):
    return lax.dot_general(a, b, (((2,), (2,)), ((0,), (0,))), preferred_element_type=F32)


def _btn(a, b):
    return lax.dot_general(a, b, (((1,), (1,)), ((0,), (0,))), preferred_element_type=F32)


def _silu_grad(x):
    s = jax.nn.sigmoid(x)
    return s * (1.0 + x * (1.0 - s))


def _matmul(name, a, b, mode, out_dtype=F32, tm=1024, tn=1024, tk=2048, out_pieces=0, ex=None):
    pieces_b = b.shape[0] if b.ndim == 3 else 0
    b2 = b.shape[1:] if pieces_b else b.shape
    if mode == "nn":
        (m, k), n = a.shape, b2[1] * max(pieces_b, 1)
    elif mode == "nt":
        (m, _), n, k = a.shape, b2[0], b2[1] * max(pieces_b, 1)
    else:
        (k, m), n = a.shape, b2[1] * max(pieces_b, 1)
    n_unit = n // max(out_pieces, 1) if mode == "nt" or not pieces_b else n // pieces_b
    if out_pieces and pieces_b and mode != "nt":
        assert out_pieces == pieces_b
    k_unit = k // pieces_b if (pieces_b and mode == "nt") else k
    tm, tn = _tile(m, tm), _tile(n_unit, tn)
    tk = k_unit if k_unit <= FULL_K else _tile(k_unit, tk)
    nk = k // tk
    n_per, k_per = n_unit // tn, k_unit // tk
    a_bytes, b_bytes = a.size * a.dtype.itemsize, b.size * b.dtype.itemsize
    j_outer = nk == 1 and b_bytes + a_bytes * (n // tn) < a_bytes + b_bytes * (m // tm)
    ij = (lambda g0, g1: (g1, g0)) if j_outer else (lambda g0, g1: (g0, g1))

    def spec(shape, pick):
        return pl.BlockSpec(shape, lambda g0, g1, l: pick(*ij(g0, g1), l))

    a_spec = {"nn": spec((tm, tk), lambda i, j, l: (i, l)), "nt": spec((tm, tk), lambda i, j, l: (i, l)),
              "tn": spec((tk, tm), lambda i, j, l: (l, i))}[mode]
    if not pieces_b:
        b_spec = {"nn": spec((tk, tn), lambda i, j, l: (l, j)), "nt": spec((tn, tk), lambda i, j, l: (j, l)),
                  "tn": spec((tk, tn), lambda i, j, l: (l, j))}[mode]
    elif mode == "nt":
        b_spec = spec((None, tn, tk), lambda i, j, l: (l // k_per, j, l % k_per))
    else:
        b_spec = spec((None, tk, tn), lambda i, j, l: (j // n_per, l, j % n_per))
    if out_pieces:
        o_spec = spec((None, tm, tn), lambda i, j, l: (j // n_per, i, j % n_per))
        o_shape = SDS((out_pieces, m, n // out_pieces), out_dtype)
    else:
        o_spec, o_shape = spec((tm, tn), lambda i, j, l: (i, j)), SDS((m, n), out_dtype)
    dot = {"nn": _nn, "nt": _nt, "tn": _tn}[mode]
    grid = (n // tn, m // tm, nk) if j_outer else (m // tm, n // tn, nk)
    host = _Hosted(ex, name, grid)

    def body(a_ref, b_ref, *rest):
        (o_ref,), scratch = host.split(rest, 1)
        host.start()
        part = dot(a_ref[...].astype(BF16), b_ref[...].astype(BF16))
        if nk == 1:
            o_ref[...] = part.astype(o_ref.dtype)
        else:
            l, acc = pl.program_id(2), scratch[0]

            @pl.when(l == 0)
            def _():
                acc[...] = part

            @pl.when((l > 0) & (l < nk - 1))
            def _():
                acc[...] += part

            @pl.when(l == nk - 1)
            def _():
                o_ref[...] = (acc[...] + part).astype(o_ref.dtype)
        host.wait()

    out = pl.pallas_call(
        body, grid=grid, in_specs=[a_spec, b_spec] + host.in_specs, out_specs=[o_spec] + host.out_specs,
        out_shape=[o_shape] + host.out_shapes, input_output_aliases=host.aliases(2, 1),
        scratch_shapes=([] if nk == 1 else [pltpu.VMEM((tm, tn), F32)]) + host.sems,
        compiler_params=_params(*(("arbitrary",) * 3 if host.stage else ("parallel", "parallel", "arbitrary"))),
        name=name)(a, b, *host.arrays)
    return host.finish(out, 1)[0]


def _norm_mod(name, h, nw, sh, sc, tr=512):
    s, d = h.shape
    tr = _tile(s, tr, 8)

    def body(h_ref, nw_ref, sh_ref, sc_ref, u_ref):
        x = h_ref[...]
        r = lax.rsqrt(jnp.mean(x * x, axis=-1, keepdims=True) + EPS)
        u_ref[...] = (x * r * nw_ref[...] * (1.0 + sc_ref[...]) + sh_ref[...]).astype(BF16)

    return pl.pallas_call(
        body, grid=(s // tr,), in_specs=[_row(tr, d), _vec(1, d), _vec(1, d), _vec(1, d)],
        out_specs=_row(tr, d), out_shape=SDS((s, d), BF16),
        compiler_params=_params("parallel"), name=name)(h, nw, sh, sc)


def _norm_mod_bwd(name, h, du, dh, nw, sc, tr=512):
    s, d = h.shape
    tr = _tile(s, tr, 8)

    def body(h_ref, du_ref, dh_ref, nw_ref, sc_ref, o_ref, acc_ref):
        @pl.when(pl.program_id(0) == 0)
        def _():
            acc_ref[...] = jnp.zeros_like(acc_ref)

        x, g = h_ref[...], du_ref[...]
        r = lax.rsqrt(jnp.mean(x * x, axis=-1, keepdims=True) + EPS)
        n = x * r
        nw, sc1 = nw_ref[...], 1.0 + sc_ref[...]
        dn = g * sc1 * nw
        o_ref[...] = dh_ref[...] + r * (dn - n * jnp.mean(dn * n, axis=-1, keepdims=True))
        gn = g * n
        acc_ref[0:1, :] += jnp.sum(g, axis=0, keepdims=True)
        acc_ref[1:2, :] += jnp.sum(gn * nw, axis=0, keepdims=True)
        acc_ref[2:3, :] += jnp.sum(gn * sc1, axis=0, keepdims=True)

    return pl.pallas_call(
        body, grid=(s // tr,),
        in_specs=[_row(tr, d), _row(tr, d), _row(tr, d), _vec(1, d), _vec(1, d)],
        out_specs=[_row(tr, d), _vec(8, d)], out_shape=[SDS((s, d), F32), SDS((8, d), F32)],
        compiler_params=_params("arbitrary"), name=name)(h, du, dh, nw, sc)


def _resid_bwd(name, dh, f, g, scale, tr=512):
    s, d = dh.shape
    tr = _tile(s, tr, 8)

    def body(dh_ref, f_ref, g_ref, o_ref, acc_ref):
        @pl.when(pl.program_id(0) == 0)
        def _():
            acc_ref[...] = jnp.zeros_like(acc_ref)

        x = dh_ref[...]
        o_ref[...] = ((scale * g_ref[...]) * x).astype(BF16)
        acc_ref[0:1, :] += jnp.sum(scale * x * f_ref[...], axis=0, keepdims=True)

    return pl.pallas_call(
        body, grid=(s // tr,), in_specs=[_row(tr, d), _row(tr, d), _vec(1, d)],
        out_specs=[_row(tr, d), _vec(8, d)], out_shape=[SDS((s, d), BF16), SDS((8, d), F32)],
        compiler_params=_params("arbitrary"), name=name)(dh, f, g)


def _final_loss(name, h, tgt, nw, tr=512):
    s, d = h.shape
    tr = _tile(s, tr, 8)

    def body(h_ref, t_ref, nw_ref, o_ref, acc_ref):
        @pl.when(pl.program_id(0) == 0)
        def _():
            acc_ref[...] = jnp.zeros_like(acc_ref)

        x, nw = h_ref[...], nw_ref[...]
        r = lax.rsqrt(jnp.mean(x * x, axis=-1, keepdims=True) + EPS)
        n = x * r
        diff = n * nw - t_ref[...]
        dy = diff * (1.0 / d)
        dn = dy * nw
        o_ref[...] = r * (dn - n * jnp.mean(dn * n, axis=-1, keepdims=True))
        acc_ref[0:1, :] += jnp.sum(dy * n, axis=0, keepdims=True)
        acc_ref[1:2, :] += jnp.sum(diff * diff, axis=0, keepdims=True) * (0.5 / d)

    return pl.pallas_call(
        body, grid=(s // tr,), in_specs=[_row(tr, d), _row(tr, d), _vec(1, d)],
        out_specs=[_row(tr, d), _vec(8, d)], out_shape=[SDS((s, d), F32), SDS((8, d), F32)],
        compiler_params=_params("arbitrary"), name=name)(h, tgt, nw)


class _Layout:
    def __init__(self, d, ca, nh):
        self.d, self.ca, self.nh = d, ca, nh
        self.qk = nh * HEAD
        self.qkv = 3 * self.qk
        self.z = self.qkv
        self.ga = self.z + self.qk
        self.cab = self.ga + 2 * d
        self.ba = self.cab + 3 * ca
        self.tail = _tile(self.ba, 512)
        self.total = self.ba + self.tail
        assert self.qkv % self.qk == 0 and self.ga % (2 * d) == 0 and self.cab % (3 * ca) == 0
        assert self.ba % self.tail == 0 and 4 * nh <= LANES

    def perm_cols(self, w):
        ca, qkv, qk, d, nh = self.ca, self.qkv, self.qk, self.d, self.nh
        o = [0, ca, 2 * ca, 3 * ca, 3 * ca + qkv, 3 * ca + qkv + qk, 3 * ca + qkv + qk + 4 * nh]
        cb, cc, cv = (w[..., o[i]:o[i + 1]] for i in range(3))
        x_qkv, x_z, x_ba = w[..., o[3]:o[4]], w[..., o[4]:o[5]], w[..., o[5]:o[6]]
        gates = w[..., o[6]:o[6] + 2 * d]
        pad = jnp.zeros(w.shape[:-1] + (self.tail - 4 * nh,), w.dtype)
        return jnp.concatenate([x_qkv, x_z, gates, cb, cc, cv, x_ba, pad], axis=-1)

    def unperm_cols(self, w):
        ca, nh = self.ca, self.nh
        cb, cc, cv = (w[..., self.cab + i * ca:self.cab + (i + 1) * ca] for i in range(3))
        return jnp.concatenate([cb, cc, cv, w[..., 0:self.qkv], w[..., self.z:self.ga],
                                w[..., self.ba:self.ba + 4 * nh], w[..., self.ga:self.cab]], axis=-1)


def _halo_specs(tr, w, cb, s):
    nb8 = s // 8
    return [pl.BlockSpec((8, w), lambda i: (jnp.maximum(i * (tr // 8) - 1, 0), cb)),
            pl.BlockSpec((tr, w), lambda i: (i, cb)),
            pl.BlockSpec((8, w), lambda i: (jnp.minimum((i + 1) * (tr // 8), nb8 - 1), cb))]


def _ext(prev_ref, main_ref, next_ref, i, nt):
    p = jnp.where(i > 0, prev_ref[...].astype(F32), 0.0)
    n = jnp.where(i < nt - 1, next_ref[...].astype(F32), 0.0)
    return jnp.concatenate([p, main_ref[...].astype(F32), n], axis=0)


def _shift(x, k):
    return x if k == 0 else pltpu.roll(x, (-k) % x.shape[0], 0)


def _conv_taps(x_ext, w, tr):
    kt = w.shape[0]
    acc = None
    for t in range(kt):
        term = _shift(x_ext, t - kt // 2)[8:8 + tr] * w[t:t + 1, :]
        acc = term if acc is None else acc + term
    return acc


def _prep_a(name, proj, conv_a, lay, tr=256):
    s, ca = proj.shape[0], lay.ca
    tr = _tile(s, tr, 8)
    nt, w = s // tr, 3 * ca

    def body(p_ref, m_ref, n_ref, w_ref, o_ref):
        x = _ext(p_ref, m_ref, n_ref, pl.program_id(0), nt)
        xv = x[:, ca:2 * ca] * x[:, 2 * ca:w]
        y = _conv_taps(xv, w_ref[...], tr)
        o_ref[...] = (m_ref[:, 0:ca] * y).astype(BF16)

    return pl.pallas_call(
        body, grid=(nt,), in_specs=_halo_specs(tr, w, lay.cab // w, s) + [_vec(conv_a.shape[0], ca)],
        out_specs=_row(tr, ca), out_shape=SDS((s, ca), BF16),
        compiler_params=_params("parallel"), name=name)(proj, proj, proj, conv_a)


def _prep_a_bwd(name, dya, proj, conv_a, dproj, lay, tr=256):
    s, ca = proj.shape[0], lay.ca
    tr = _tile(s, tr, 8)
    nt, w, kt = s // tr, 3 * ca, conv_a.shape[0]

    def body(p_ref, m_ref, n_ref, dp_ref, dm_ref, dn_ref, w_ref, _, o_ref, acc_ref):
        i = pl.program_id(0)

        @pl.when(i == 0)
        def _():
            acc_ref[...] = jnp.zeros_like(acc_ref)

        x = _ext(p_ref, m_ref, n_ref, i, nt)
        d_ext = _ext(dp_ref, dm_ref, dn_ref, i, nt)
        cb, cc, cv = x[:, 0:ca], x[:, ca:2 * ca], x[:, 2 * ca:w]
        xv = cc * cv
        wv = w_ref[...]
        dy_ext = d_ext * cb
        dx = None
        for t in range(kt):
            term = _shift(dy_ext, kt // 2 - t)[8:8 + tr] * wv[t:t + 1, :]
            dx = term if dx is None else dx + term
            acc_ref[t:t + 1, :] += jnp.sum(dy_ext[8:8 + tr] * _shift(xv, t - kt // 2)[8:8 + tr],
                                           axis=0, keepdims=True)
        y = _conv_taps(xv, wv, tr)
        o_ref[:, 0:ca] = (dm_ref[...] * y).astype(BF16)
        o_ref[:, ca:2 * ca] = (dx * cv[8:8 + tr]).astype(BF16)
        o_ref[:, 2 * ca:w] = (dx * cc[8:8 + tr]).astype(BF16)

    return pl.pallas_call(
        body, grid=(nt,),
        in_specs=_halo_specs(tr, w, lay.cab // w, s) + _halo_specs(tr, ca, 0, s)
        + [_vec(kt, ca), pl.BlockSpec(memory_space=pl.ANY)],
        out_specs=[_row(tr, w, lay.cab // w), _vec(8, ca)],
        out_shape=[SDS(dproj.shape, dproj.dtype), SDS((8, ca), F32)], input_output_aliases={7: 0},
        compiler_params=_params("arbitrary"), name=name)(proj, proj, proj, dya, dya, dya, conv_a, dproj)


def _qkv_act(c, nh, tr_rows):
    sact = jax.nn.silu(c)
    outs, inv = [], []
    for hd in range(3 * nh):
        sl = sact[:, hd * HEAD:(hd + 1) * HEAD]
        if hd < 2 * nh:
            r = lax.rsqrt(jnp.sum(sl * sl, axis=-1, keepdims=True) + EPS)
            inv.append(r)
            outs.append(sl * (r * (HEAD ** -0.5 if hd < nh else 1.0)))
        else:
            outs.append(sl)
    return jnp.concatenate(outs, axis=-1), sact, inv


def _prep_b(name, proj, conv_dn, lay, tr=256):
    s, w, nh = proj.shape[0], lay.qkv, lay.nh
    tr = _tile(s, tr, 8)
    nt = s // tr

    def body(p_ref, m_ref, n_ref, w_ref, o_ref):
        x = _ext(p_ref, m_ref, n_ref, pl.program_id(0), nt)
        c = _conv_taps(x, w_ref[...], tr)
        o_ref[...] = _qkv_act(c, nh, tr)[0]

    return pl.pallas_call(
        body, grid=(nt,), in_specs=_halo_specs(tr, w, 0, s) + [_vec(conv_dn.shape[0], w)],
        out_specs=_row(tr, w), out_shape=SDS((s, w), F32),
        compiler_params=_params("parallel"), name=name)(proj, proj, proj, conv_dn)


def _prep_b_bwd(name, dq, dk, dv, proj, conv_dn, dproj, lay, tr=256):
    s, w, nh, qk = proj.shape[0], lay.qkv, lay.nh, lay.qk
    tr = _tile(s, tr, 8)
    nt, kt = s // tr, conv_dn.shape[0]

    def body(*refs):
        x_refs, g_refs = refs[0:3], refs[3:12]
        w_ref, o_ref, acc_ref = refs[12], refs[14], refs[15]
        i = pl.program_id(0)

        @pl.when(i == 0)
        def _():
            acc_ref[...] = jnp.zeros_like(acc_ref)

        x = _ext(*x_refs, i, nt)
        wv = w_ref[...]
        c = None
        for t in range(kt):
            term = _shift(x, t - kt // 2) * wv[t:t + 1, :]
            c = term if c is None else c + term
        sig = jax.nn.sigmoid(c)
        sact = c * sig
        ds = []
        for part in range(3):
            g = _ext(*g_refs[3 * part:3 * part + 3], i, nt)
            for hd in range(nh):
                sl = sact[:, part * qk + hd * HEAD:part * qk + (hd + 1) * HEAD]
                gh = g[:, hd * HEAD:(hd + 1) * HEAD]
                if part < 2:
                    r = lax.rsqrt(jnp.sum(sl * sl, axis=-1, keepdims=True) + EPS)
                    sc = HEAD ** -0.5 if part == 0 else 1.0
                    ds.append(sc * r * (gh - sl * (r * r) * jnp.sum(gh * sl, axis=-1, keepdims=True)))
                else:
                    ds.append(gh)
        dc = jnp.concatenate(ds, axis=-1) * (sig * (1.0 + c * (1.0 - sig)))
        dx = None
        for t in range(kt):
            term = _shift(dc, kt // 2 - t)[8:8 + tr] * wv[t:t + 1, :]
            dx = term if dx is None else dx + term
            acc_ref[t:t + 1, :] += jnp.sum(dc[8:8 + tr] * _shift(x, t - kt // 2)[8:8 + tr],
                                           axis=0, keepdims=True)
        o_ref[...] = dx.astype(BF16)

    return pl.pallas_call(
        body, grid=(nt,),
        in_specs=_halo_specs(tr, w, 0, s) + _halo_specs(tr, qk, 0, s) * 3
        + [_vec(kt, w), pl.BlockSpec(memory_space=pl.ANY)],
        out_specs=[_row(tr, w, 0), _vec(8, w)],
        out_shape=[SDS(dproj.shape, dproj.dtype), SDS((8, w), F32)], input_output_aliases={13: 0},
        compiler_params=_params("arbitrary"), name=name)(
            proj, proj, proj, dq, dq, dq, dk, dk, dk, dv, dv, dv, conv_dn, dproj)


def _softplus(x):
    return jnp.maximum(x, 0.0) + jnp.log(1.0 + jnp.exp(-jnp.abs(x)))


def _split3(x):
    hi = x.astype(BF16)
    r = x - hi.astype(F32)
    mid = r.astype(BF16)
    return hi, mid, (r - mid.astype(F32)).astype(BF16)


def _exact_nn(m, x):
    m = m.astype(BF16)
    hi, mid, lo = _split3(x)
    return _nn(m, hi) + _nn(m, mid) + _nn(m, lo)


def _chunk_cumsum_masks(tr):
    ri = lax.broadcasted_iota(jnp.int32, (tr, tr), 0)
    ci = lax.broadcasted_iota(jnp.int32, (tr, tr), 1)
    same = (ri // CHUNK) == (ci // CHUNK)
    return (same & (ci <= ri)).astype(F32), (same & (ci >= ri)).astype(F32)


def _prep_c(name, proj, pvec, lay, tr=512):
    s, nh = proj.shape[0], lay.nh
    tr = _tile(s, tr, CHUNK)
    assert 6 * nh <= LANES

    def body(x_ref, p_ref, o_ref):
        x = x_ref[...]
        lane = lax.broadcasted_iota(jnp.int32, x.shape, 1)
        is_g = (lane >= 2 * nh) & (lane < 4 * nh)
        g = jnp.where(is_g, -jnp.exp(p_ref[0:1, :]) * _softplus(x + p_ref[1:2, :]), 0.0)
        m_f, m_b = _chunk_cumsum_masks(tr)
        gc = jnp.where(lane < 3 * nh, _exact_nn(m_f, g), _exact_nn(m_b, g))
        gc = pltpu.roll(gc, 2 * nh, 1)
        o_ref[...] = jnp.where(lane < 2 * nh, jax.nn.sigmoid(x), jnp.where(lane < 4 * nh, g, gc))

    return pl.pallas_call(
        body, grid=(s // tr,), in_specs=[_row(tr, LANES, lay.ba // LANES), _vec(8, LANES)],
        out_specs=_row(tr, LANES), out_shape=SDS((s, LANES), F32),
        compiler_params=_params("parallel"), name=name)(proj, pvec)


def _prep_c_bwd(name, dbg_f, dbg_b, proj, pvec, dproj, lay, tr=512):
    s, nh, tail = proj.shape[0], lay.nh, lay.tail
    tr = _tile(s, tr, CHUNK)

    def body(x_ref, df_ref, db_ref, p_ref, _, o_ref, acc_ref):
        @pl.when(pl.program_id(0) == 0)
        def _():
            acc_ref[...] = jnp.zeros_like(acc_ref)

        x = x_ref[...]
        lane = lax.broadcasted_iota(jnp.int32, x.shape, 1)
        is_b, is_g = lane < 2 * nh, (lane >= 2 * nh) & (lane < 4 * nh)
        fwd_lane = (lane < nh) | ((lane >= 2 * nh) & (lane < 3 * nh))
        d = jnp.where(lane < 4 * nh, jnp.where(fwd_lane, df_ref[...], db_ref[...]), 0.0)
        m_f, m_b = _chunk_cumsum_masks(tr)
        dgc = jnp.where(is_g, d, 0.0)
        dg = jnp.where(fwd_lane, _exact_nn(m_b, dgc), _exact_nn(m_f, dgc))
        sb = jax.nn.sigmoid(x)
        na = -jnp.exp(p_ref[0:1, :])
        xs = x + p_ref[1:2, :]
        dsp = dg * na * jax.nn.sigmoid(xs)
        dx = jnp.where(is_b, d * sb * (1.0 - sb), jnp.where(is_g, dsp, 0.0))
        o_ref[...] = jnp.zeros_like(o_ref)
        o_ref[:, 0:LANES] = dx.astype(BF16)
        acc_ref[0:1, :] += jnp.sum(jnp.where(is_g, dg * na * _softplus(xs), 0.0), axis=0, keepdims=True)
        acc_ref[1:2, :] += jnp.sum(jnp.where(is_g, dsp, 0.0), axis=0, keepdims=True)

    return pl.pallas_call(
        body, grid=(s // tr,),
        in_specs=[_row(tr, LANES, lay.ba // LANES), _row(tr, LANES), _row(tr, LANES), _vec(8, LANES),
                  pl.BlockSpec(memory_space=pl.ANY)],
        out_specs=[_row(tr, tail, lay.ba // tail), _vec(8, LANES)],
        out_shape=[SDS(dproj.shape, dproj.dtype), SDS((8, LANES), F32)], input_output_aliases={4: 0},
        compiler_params=_params("arbitrary"), name=name)(proj, dbg_f, dbg_b, pvec, dproj)


def _post(name, o_f, o_b, proj, dn_w, lay, tr=256):
    s, qk, nh = o_f.shape[0], lay.qk, lay.nh
    tr = _tile(s, tr, 8)

    def body(f_ref, b_ref, z_ref, w_ref, o_ref):
        o = f_ref[...] + b_ref[...]
        gate = jax.nn.silu(z_ref[...])
        for hd in range(nh):
            sl = slice(hd * HEAD, (hd + 1) * HEAD)
            oh = o[:, sl]
            r = lax.rsqrt(jnp.mean(oh * oh, axis=-1, keepdims=True) + EPS)
            o_ref[:, sl] = (oh * r * w_ref[...] * gate[:, sl]).astype(BF16)

    return pl.pallas_call(
        body, grid=(s // tr,),
        in_specs=[_row(tr, qk), _row(tr, qk), _row(tr, qk, lay.z // qk), _vec(1, HEAD)],
        out_specs=_row(tr, qk), out_shape=SDS((s, qk), BF16),
        compiler_params=_params("parallel"), name=name)(o_f, o_b, proj, dn_w)


def _post_bwd(name, dyb, o_f, o_b, proj, dn_w, dproj, lay, tr=256):
    s, qk, nh = o_f.shape[0], lay.qk, lay.nh
    tr = _tile(s, tr, 8)

    def body(d_ref, f_ref, b_ref, z_ref, w_ref, _, do_ref, dz_ref, acc_ref):
        @pl.when(pl.program_id(0) == 0)
        def _():
            acc_ref[...] = jnp.zeros_like(acc_ref)

        o, z, d, wv = f_ref[...] + b_ref[...], z_ref[...], d_ref[...], w_ref[...]
        gate = jax.nn.silu(z)
        dgate = _silu_grad(z)
        for hd in range(nh):
            sl = slice(hd * HEAD, (hd + 1) * HEAD)
            oh, dh = o[:, sl], d[:, sl]
            r = lax.rsqrt(jnp.mean(oh * oh, axis=-1, keepdims=True) + EPS)
            n = oh * r
            dz_ref[:, sl] = (dh * n * wv * dgate[:, sl]).astype(BF16)
            don = dh * gate[:, sl]
            acc_ref[0:1, :] += jnp.sum(don * n, axis=0, keepdims=True)
            dn = don * wv
            do_ref[:, sl] = r * (dn - n * jnp.mean(dn * n, axis=-1, keepdims=True))

    return pl.pallas_call(
        body, grid=(s // tr,),
        in_specs=[_row(tr, qk), _row(tr, qk), _row(tr, qk), _row(tr, qk, lay.z // qk), _vec(1, HEAD),
                  pl.BlockSpec(memory_space=pl.ANY)],
        out_specs=[_row(tr, qk), _row(tr, qk, lay.z // qk), _vec(8, HEAD)],
        out_shape=[SDS((s, qk), F32), SDS(dproj.shape, dproj.dtype), SDS((8, HEAD), F32)],
        input_output_aliases={5: 1},
        compiler_params=_params("arbitrary"), name=name)(dyb, o_f, o_b, proj, dn_w, dproj)


def _b_out_merge(name, yb, w_b, pa, proj, lay, tr=512):
    s, d = pa.shape
    k = yb.shape[1]
    tr = _tile(s, tr, 16)

    def body(y_ref, w_ref, a_ref, g_ref, pb_ref, o_ref):
        pb = _nn(y_ref[...], w_ref[...])
        pb_ref[...] = pb
        o_ref[...] = (jax.nn.sigmoid(g_ref[:, 0:d]) * a_ref[...] + jax.nn.sigmoid(g_ref[:, d:2 * d]) * pb).astype(BF16)

    return pl.pallas_call(
        body, grid=(s // tr,),
        in_specs=[_row(tr, k), _vec(k, d), _row(tr, d), _row(tr, 2 * d, lay.ga // (2 * d))],
        out_specs=[_row(tr, d), _row(tr, d)], out_shape=[SDS((s, d), F32), SDS((s, d), BF16)],
        compiler_params=_params("parallel"), name=name)(yb, w_b, pa, proj)


def _merge_bwd(name, dmg, pa, pb, proj, lay, tr=512):
    s, d = pa.shape
    tr = _tile(s, tr, 8)

    def body(d_ref, a_ref, b_ref, g_ref, da_ref, db_ref, dg_ref):
        dm = d_ref[...]
        sa, sb = jax.nn.sigmoid(g_ref[:, 0:d]), jax.nn.sigmoid(g_ref[:, d:2 * d])
        da_ref[...] = (sa * dm).astype(BF16)
        db_ref[...] = (sb * dm).astype(BF16)
        dg_ref[:, 0:d] = (dm * a_ref[...] * sa * (1.0 - sa)).astype(BF16)
        dg_ref[:, d:2 * d] = (dm * b_ref[...] * sb * (1.0 - sb)).astype(BF16)

    return pl.pallas_call(
        body, grid=(s // tr,),
        in_specs=[_row(tr, d), _row(tr, d), _row(tr, d), _row(tr, 2 * d, lay.ga // (2 * d))],
        out_specs=[_row(tr, d), _row(tr, d), _row(tr, 2 * d, lay.ga // (2 * d))],
        out_shape=[SDS((s, d), BF16), SDS((s, d), BF16), SDS((s, lay.total), BF16)],
        compiler_params=_params("parallel"), name=name)(dmg, pa, pb, proj)


def _tri_inverse(a_mat, ri, ci):
    def same(shift):
        return (ri >> shift) == (ci >> shift)

    x = -jnp.where(same(3), a_mat, 0.0)
    t_mat = (ri == ci).astype(F32) + x
    for _ in range(2):
        x = _bnn(x, x)
        t_mat = t_mat + _bnn(t_mat, x)
    for shift in (3, 4, 5):
        b = jnp.where(same(shift + 1) & ~same(shift), a_mat, 0.0)
        t_mat = t_mat - _bnn(_bnn(t_mat, b), t_mat)
    return t_mat


def _chunk_terms(q, k, v, beta, gc, g_row, g_last, reverse, t_mat=None):
    c = CHUNK
    ri = lax.broadcasted_iota(jnp.int32, (c, c), 0)
    ci = lax.broadcasted_iota(jnp.int32, (c, c), 1)
    if reverse:
        incl, strict = ri <= ci, ri < ci
    else:
        incl, strict = ri >= ci, ri > ci
    decay = jnp.where(incl, jnp.exp(jnp.where(incl, gc - g_row, 0.0)), 0.0)
    e = jnp.exp(gc)
    ed = jnp.exp(g_last - gc)
    el = jnp.exp(g_last)
    kb = k * beta
    kk_qk = _bnt(jnp.concatenate([kb, q], axis=1), k)
    a_mat = jnp.where(strict, kk_qk[:, 0:c] * decay, 0.0)
    p_mat = jnp.where(incl, kk_qk[:, c:2 * c] * decay, 0.0)
    if t_mat is None:
        t_mat = _tri_inverse(a_mat, ri, ci)
    uw = _bnn(t_mat, jnp.concatenate([v * beta, kb * e], axis=2))
    return dict(incl=incl, strict=strict, decay=decay, e=e, ed=ed, el=el, kb=kb,
                a=a_mat, t=t_mat, uw=uw, u=uw[:, :, 0:HEAD], w=uw[:, :, HEAD:2 * HEAD], p=p_mat)


def _delta_specs(nh, tb, nb, reverse):
    tok = (lambda i: nb - 1 - i) if reverse else (lambda i: i)
    hw = nh * HEAD
    qkv = [pl.BlockSpec((tb, hw), functools.partial(lambda i, part: (tok(i), part), part=p)) for p in range(3)]
    rows = pl.BlockSpec((tb, hw), lambda i: (tok(i), 0))
    bg = pl.BlockSpec((tb, LANES), lambda i: (tok(i), 0))
    gct = pl.BlockSpec((2 * nh, tb), lambda i: (0, tok(i)))
    st = pl.BlockSpec((nh, tb // CHUNK, HEAD, HEAD), lambda i: (0, tok(i), 0, 0))
    tri = pl.BlockSpec((nh, tb // CHUNK, CHUNK, CHUNK), lambda i: (0, tok(i), 0, 0))
    return qkv, rows, bg, gct, st, tri


def _heads(ref, rows, nh):
    return jnp.stack([ref[rows, hd * HEAD:(hd + 1) * HEAD] for hd in range(nh)])


def _chunk_scalars(bg_ref, gct_ref, cj, nh, tb, reverse):
    rows = pl.ds(cj * CHUNK, CHUNK)
    lb = nh if reverse else 0
    lc = 4 * nh + lb
    last = cj * CHUNK + (0 if reverse else CHUNK - 1)
    g_lanes = gct_ref[lb:lb + nh, :]
    if cj:
        g_lanes = pltpu.roll(g_lanes, tb - cj * CHUNK, 1)
    col = lambda l0, r: jnp.stack([bg_ref[r, l0 + hd:l0 + hd + 1] for hd in range(nh)])
    return col(lb, rows), col(lc, rows), g_lanes[:, 0:CHUNK][:, None, :], col(lc, pl.ds(last, 1))


def _delta_fwd(name, qkvn, bg, gct, nh, reverse, tb=256):
    s = qkvn.shape[0]
    tb = _tile(s, tb, LANES)
    nb, cpb = s // tb, tb // CHUNK
    qkv, rows_spec, bg_spec, gct_spec, st, tri = _delta_specs(nh, tb, nb, reverse)

    def body(q_ref, k_ref, v_ref, bg_ref, gct_ref, o_ref, st_ref, tri_ref, state):
        @pl.when(pl.program_id(0) == 0)
        def _():
            state[...] = jnp.zeros_like(state)

        for cj in (range(cpb - 1, -1, -1) if reverse else range(cpb)):
            rows = pl.ds(cj * CHUNK, CHUNK)
            q, k, v = _heads(q_ref, rows, nh), _heads(k_ref, rows, nh), _heads(v_ref, rows, nh)
            tm = _chunk_terms(q, k, v, *_chunk_scalars(bg_ref, gct_ref, cj, nh, tb, reverse), reverse)
            s_in = state[...]
            st_ref[:, cj] = s_in
            tri_ref[:, cj] = tm["t"]
            ws_qs = _bnn(jnp.concatenate([tm["w"], q * tm["e"]], axis=1), s_in)
            vn = tm["u"] - ws_qs[:, 0:CHUNK]
            o = ws_qs[:, CHUNK:2 * CHUNK] + _bnn(tm["p"], vn)
            for hd in range(nh):
                o_ref[rows, hd * HEAD:(hd + 1) * HEAD] = o[hd]
            state[...] = s_in * tm["el"] + _btn(k * tm["ed"], vn)

    return pl.pallas_call(
        body, grid=(nb,), in_specs=qkv + [bg_spec, gct_spec], out_specs=[rows_spec, st, tri],
        out_shape=[SDS((s, nh * HEAD), F32), SDS((nh, s // CHUNK, HEAD, HEAD), F32),
                   SDS((nh, s // CHUNK, CHUNK, CHUNK), F32)],
        scratch_shapes=[pltpu.VMEM((nh, HEAD, HEAD), F32)],
        compiler_params=_params("arbitrary"), name=name)(qkvn, qkvn, qkvn, bg, gct)


def _delta_bwd(name, qkvn, bg, gct, do, states, tris, nh, reverse, add=None, tb=128, ex=None):
    s = qkvn.shape[0]
    tb = _tile(s, tb, LANES)
    nb, cpb = s // tb, tb // CHUNK
    qkv, rows_spec, bg_spec, gct_spec, st, tri = _delta_specs(nh, tb, nb, not reverse)
    n_add = 0 if add is None else 3
    host = _Hosted(ex, name, (nb,))

    def body(*refs):
        q_ref, k_ref, v_ref, bg_ref, gct_ref, do_ref, st_ref, tri_ref = refs[0:8]
        add_refs = refs[8:8 + n_add]
        (dq_ref, dk_ref, dv_ref, dbg_ref), (dstate,) = host.split(refs[8 + n_add:], 4)
        host.start()

        @pl.when(pl.program_id(0) == 0)
        def _():
            dstate[...] = jnp.zeros_like(dstate)

        ones = jnp.ones((nh, 2 * CHUNK, HEAD), BF16)
        row_id = lax.broadcasted_iota(jnp.int32, (CHUNK, 1), 0)
        rsum = lambda x: jnp.sum(x, axis=2, keepdims=True)
        for cj in (range(cpb) if reverse else range(cpb - 1, -1, -1)):
            rows = pl.ds(cj * CHUNK, CHUNK)
            q, k, v, d_o = (_heads(r, rows, nh) for r in (q_ref, k_ref, v_ref, do_ref))
            beta, gc, g_row, g_last = _chunk_scalars(bg_ref, gct_ref, cj, nh, tb, reverse)
            tm = _chunk_terms(q, k, v, beta, gc, g_row, g_last, reverse, t_mat=tri_ref[:, cj])
            incl, strict, e, ed, el, kb = tm["incl"], tm["strict"], tm["e"], tm["ed"], tm["el"], tm["kb"]
            t_mat, u, w, p_mat, decay = tm["t"], tm["u"], tm["w"], tm["p"], tm["decay"]
            s_in, ds_out = st_ref[:, cj], dstate[...]
            cat_rows = lambda a, b: jnp.concatenate([a, b], axis=1)
            top, bot = slice(0, CHUNK), slice(CHUNK, 2 * CHUNK)
            vn = u - _bnn(w, s_in)
            qe, kd, ke = q * e, k * ed, kb * e
            dvn = _btn(p_mat, d_o) + _bnn(kd, ds_out)
            by_state = _bnt(cat_rows(d_o, dvn), s_in)
            dqe, dw = by_state[:, top], -by_state[:, bot]
            dq = dqe * e
            dgc = rsum(dqe * qe)
            dp = jnp.where(incl, _bnt(d_o, vn), 0.0)
            dkd = _bnt(vn, ds_out)
            dk = dkd * ed
            r = rsum(dkd * kd)
            dgc = dgc - r
            dg_last = (jnp.sum(r, axis=1, keepdims=True)
                       + jnp.sum(rsum(ds_out * s_in), axis=1, keepdims=True) * el)
            d_uw = _btn(t_mat, jnp.concatenate([dvn, dw], axis=2))
            dbv, dke = d_uw[:, :, 0:HEAD], d_uw[:, :, HEAD:2 * HEAD]
            da = -jnp.where(strict, _bnt(d_uw, tm["uw"]), 0.0)
            mn = cat_rows(da * decay, dp * decay)
            by_k = _bnn(mn, k)
            dkb = by_k[:, top] + dke * e
            dq = dq + by_k[:, bot]
            dk = dk + _btn(mn, cat_rows(kb, q))
            g_mat = da * tm["a"] + dp * p_mat
            g_hi, g_mid, _ = _split3(g_mat)
            col = _btn(cat_rows(g_hi, g_mid), ones)[:, :, 0:1]
            dgc = dgc + rsum(g_mat) - col + rsum(dke * ke)
            dgc = dgc + jnp.where(row_id == (0 if reverse else CHUNK - 1), dg_last, 0.0)
            dv = dbv * beta
            dbeta = rsum(dbv * v) + rsum(dkb * k)
            dk = dk + dkb * beta
            dstate[...] = el * ds_out + _btn(cat_rows(qe, w), cat_rows(d_o, -dvn))
            lb = nh if reverse else 0
            for hd in range(nh):
                cols = slice(hd * HEAD, (hd + 1) * HEAD)
                extra = [a[rows, cols] for a in add_refs] if n_add else [0.0, 0.0, 0.0]
                dq_ref[rows, cols] = dq[hd] + extra[0]
                dk_ref[rows, cols] = dk[hd] + extra[1]
                dv_ref[rows, cols] = dv[hd] + extra[2]
                dbg_ref[rows, lb + hd:lb + hd + 1] = dbeta[hd]
                dbg_ref[rows, 2 * nh + lb + hd:2 * nh + lb + hd + 1] = dgc[hd]
        host.wait()

    out3 = SDS((s, nh * HEAD), F32)
    n_in = 8 + n_add
    out = pl.pallas_call(
        body, grid=(nb,),
        in_specs=qkv + [bg_spec, gct_spec, rows_spec, st, tri] + [rows_spec] * n_add + host.in_specs,
        out_specs=[rows_spec, rows_spec, rows_spec, bg_spec] + host.out_specs,
        out_shape=[out3, out3, out3, SDS((s, LANES), F32)] + host.out_shapes,
        input_output_aliases=host.aliases(n_in, 4), scratch_shapes=[pltpu.VMEM((nh, HEAD, HEAD), F32)] + host.sems,
        compiler_params=_params("arbitrary"), name=name)(
            qkvn, qkvn, qkvn, bg, gct, do, states, tris, *(add or ()), *host.arrays)
    return host.finish(out, 4)


def _row_pieces(g):
    return g.reshape(N_CHIPS, g.shape[0] // N_CHIPS, g.shape[1])


def _up_swiglu(name, u, w_up, tm=1024, ex=None):
    s, k = u.shape
    fh = w_up.shape[2]
    tm = _tile(s, tm, 8)
    grid = (2, s // tm)
    host = _Hosted(ex, name, grid)

    def body(u_ref, wa_ref, wb_ref, *rest):
        (a_ref, b_ref, o_ref), _ = host.split(rest, 3)
        host.start()
        x = u_ref[...]
        a, b = _nn(x, wa_ref[...]), _nn(x, wb_ref[...])
        a_ref[...], b_ref[...] = a.astype(BF16), b.astype(BF16)
        o_ref[...] = (jax.nn.silu(a) * b).astype(BF16)
        host.wait()

    tile = pl.BlockSpec((tm, fh), lambda j, i: (i, j))
    out = pl.pallas_call(
        body, grid=grid,
        in_specs=[pl.BlockSpec((tm, k), lambda j, i: (i, 0)), pl.BlockSpec((None, k, fh), lambda j, i: (j, 0, 0)),
                  pl.BlockSpec((None, k, fh), lambda j, i: (2 + j, 0, 0))] + host.in_specs,
        out_specs=[tile] * 3 + host.out_specs, out_shape=[SDS((s, 2 * fh), BF16)] * 3 + host.out_shapes,
        input_output_aliases=host.aliases(3, 3), scratch_shapes=host.sems,
        compiler_params=_params("arbitrary", "arbitrary"), name=name)(u, w_up, w_up, *host.arrays)
    return host.finish(out, 3)


def _matmul_resid(name, a, w, h, g, scale, tm=512, ex=None):
    s, k = a.shape
    d = w.shape[1]
    tm = _tile(s, tm, 16)
    grid = (s // tm,)
    host = _Hosted(ex, name, grid)

    def body(a_ref, w_ref, h_ref, g_ref, *rest):
        (o_ref, f_ref), _ = host.split(rest, 2)
        host.start()
        f = _nn(a_ref[...], w_ref[...])
        f_ref[...] = f.astype(BF16)
        o_ref[...] = h_ref[...] + (scale * g_ref[...]) * f
        host.wait()

    out = pl.pallas_call(
        body, grid=grid, in_specs=[_row(tm, k), _vec(k, d), _row(tm, d), _vec(1, d)] + host.in_specs,
        out_specs=[_row(tm, d), _row(tm, d)] + host.out_specs,
        out_shape=[SDS((s, d), F32), SDS((s, d), BF16)] + host.out_shapes,
        input_output_aliases=host.aliases(4, 2), scratch_shapes=host.sems,
        compiler_params=_params("arbitrary"), name=name)(a, w, h, g, *host.arrays)
    return host.finish(out, 2)


def _down_swiglu_bwd(name, df, w_down, a_pre, b_pre, tm=512):
    s, d = df.shape
    f = w_down.shape[0]
    tm = _tile(s, tm, 16)

    def body(df_ref, w_ref, a_ref, b_ref, o_ref):
        dhm = _nt(df_ref[...], w_ref[...])
        a = a_ref[...].astype(F32)
        o_ref[:, 0:f] = (dhm * b_ref[...].astype(F32) * _silu_grad(a)).astype(BF16)
        o_ref[:, f:2 * f] = (dhm * jax.nn.silu(a)).astype(BF16)

    return pl.pallas_call(
        body, grid=(s // tm,), in_specs=[_row(tm, d), _vec(f, d), _row(tm, f), _row(tm, f)],
        out_specs=_row(tm, 2 * f), out_shape=SDS((s, 2 * f), BF16),
        compiler_params=_params("parallel"), name=name)(df, w_down, a_pre, b_pre)


def _col_pieces(w):
    return w if w.ndim == 3 else w.reshape(w.shape[0], N_CHIPS, -1).transpose(1, 0, 2)


def _ffn_fwd(tag, h, nw, sh, sc, g, w_up, w_down, ex=None):
    u = _norm_mod(tag + "_norm", h, nw, sh, sc)
    a_pre, b_pre, hm = _up_swiglu(tag + "_up", u, _col_pieces(w_up), ex=ex)
    w_down = w_down() if callable(w_down) else w_down
    h_new, f = _matmul_resid(tag + "_down", hm, w_down, h, g, 0.5, ex=ex)
    return h_new, (h, u, a_pre, b_pre, hm, f, w_down)


def _ffn_bwd(tag, dh, saved, nw, sc, g, w_up, ex=None, on_gw_down=None):
    h, u, a_pre, b_pre, hm, f, w_down = saved
    df, acc_g = _resid_bwd(tag + "_res_bwd", dh, f, g, 0.5)
    shard = hm.shape[1] // 2
    gw_down = _row_pieces(_matmul(tag + "_gw_down", hm, df, "tn", out_dtype=BF16, tm=shard, ex=ex))
    if on_gw_down:
        on_gw_down(gw_down)
    dab = _down_swiglu_bwd(tag + "_dhm", df, w_down, a_pre, b_pre)
    gw_up = _matmul(tag + "_gw_up", u, dab, "tn", out_dtype=BF16, tn=shard, out_pieces=N_CHIPS, ex=ex)
    du = _matmul(tag + "_du", dab, w_up, "nt", ex=ex)
    dh_in, acc = _norm_mod_bwd(tag + "_norm_bwd", h, du, dh, nw, sc)
    return dh_in, gw_up, gw_down, (acc[0], acc[1], acc_g[0], acc[2])


def _mixer_fwd(h, nw, sh, sc, g, wt, lay, ex=None):
    nh = lay.nh
    u = _norm_mod("mix_norm", h, nw, sh, sc)
    proj = _matmul("mix_in", u, wt["w_in"], "nn", ex=ex)
    if ex:
        wt = dict(wt, **_as_operands(ex.weights("mix"), lay))
    qkvn = _prep_b("mix_prep_b", proj, wt["conv_dn"], lay)
    ya = _prep_a("mix_prep_a", proj, wt["conv_a"], lay)
    bg = _prep_c("mix_prep_c", proj, wt["pvec"], lay)
    gct = bg[:, 4 * nh:6 * nh].T
    o_f, *st_f = _delta_fwd("delta_fwd_l2r", qkvn, bg, gct, nh, False)
    o_b, *st_b = _delta_fwd("delta_fwd_r2l", qkvn, bg, gct, nh, True)
    yb = _post("mix_post", o_f, o_b, proj, wt["dn_norm"], lay)
    pa = _matmul("mix_a_out", ya, wt["w_a_out"], "nn")
    pb, mg = _b_out_merge("mix_b_out", yb, wt["w_b_out"], pa, proj, lay)
    h2, y = _matmul_resid("mix_out", mg, wt["w_out"], h, g, 1.0, ex=ex)
    return h2, (h, u, proj, qkvn, ya, bg, gct, o_f, o_b, st_f, st_b, yb, pa, pb, mg, y)


def _mixer_bwd(dh, saved, nw, sc, g, wt, lay, ex=None):
    h, u, proj, qkvn, ya, bg, gct, o_f, o_b, st_f, st_b, yb, pa, pb, mg, y = saved
    nh = lay.nh
    dy, acc_g = _resid_bwd("mix_res_bwd", dh, y, g, 1.0)
    gw_out = _matmul("mix_gw_out", mg, dy, "tn", out_dtype=BF16, ex=ex)
    dmg = _matmul("mix_dmg", dy, wt["w_out"], "nt")
    dpa, dpb, dproj = _merge_bwd("mix_merge_bwd", dmg, pa, pb, proj, lay)
    gw_a = _matmul("mix_gw_a", ya, dpa, "tn", out_dtype=BF16, out_pieces=N_CHIPS)
    gw_b = _matmul("mix_gw_b", yb, dpb, "tn", out_dtype=BF16)
    dya = _matmul("mix_dya", dpa, wt["w_a_out"], "nt")
    dyb = _matmul("mix_dyb", dpb, wt["w_b_out"], "nt")
    do, dproj, acc_dn = _post_bwd("mix_post_bwd", dyb, o_f, o_b, proj, wt["dn_norm"], dproj, lay)
    dq, dk, dv, dbg_f = _delta_bwd("delta_bwd_l2r", qkvn, bg, gct, do, *st_f, nh, False, ex=ex)
    dq, dk, dv, dbg_b = _delta_bwd("delta_bwd_r2l", qkvn, bg, gct, do, *st_b, nh, True, add=(dq, dk, dv), ex=ex)
    dproj, acc_ca = _prep_a_bwd("mix_prep_a_bwd", dya, proj, wt["conv_a"], dproj, lay)
    dproj, acc_cd = _prep_b_bwd("mix_prep_b_bwd", dq, dk, dv, proj, wt["conv_dn"], dproj, lay)
    dproj, acc_pc = _prep_c_bwd("mix_prep_c_bwd", dbg_f, dbg_b, proj, wt["pvec"], dproj, lay)
    gw_in = lay.unperm_cols(_matmul("mix_gw_in", u, dproj, "tn", out_dtype=BF16))
    gw_in = gw_in.reshape(gw_in.shape[0], N_CHIPS, -1).transpose(1, 0, 2)
    du = _matmul("mix_du", dproj, wt["w_in"], "nt")
    dh_in, acc = _norm_mod_bwd("mix_norm_bwd", h, du, dh, nw, sc)
    small = dict(conv_a=acc_ca[0:wt["conv_a"].shape[0]], conv_dn=acc_cd[0:wt["conv_dn"].shape[0]],
                 dn_norm=acc_dn[0:1], a_log=acc_pc[0], dt_bias=acc_pc[1])
    big = dict(w_in=gw_in, w_a_out=gw_a, w_b_out=_row_pieces(gw_b), w_out=_row_pieces(gw_out))
    return dh_in, big, small, (acc[0], acc[1], acc_g[0], acc[2])


def _as_operands(gathered, lay):
    wt = {}
    for n, g in gathered.items():
        if n == "w_in":
            wt[n] = lay.perm_cols(jnp.concatenate(list(g), axis=1))
        else:
            wt[n] = g if n in COL_SHARDED else g.reshape(-1, g.shape[-1])
    return wt


def _local_step(x, tgt, modv, wt, lay, ex=None):
    m = [modv[i:i + 1] for i in range(9)]

    def ffn1_down():
        ex.run(("gather_pass", "down1"))
        return _as_operands(ex.weights("down1"), lay)["w_ffn1_down"]

    h1, sv1 = _ffn_fwd("ffn1", x, wt["norm_ffn1"], m[0], m[1], m[2], wt["w_ffn1_up"],
                       ffn1_down if ex else wt["w_ffn1_down"], ex)
    if ex:
        wt = dict(wt, **_as_operands(ex.weights("in"), lay))
    h2, sv2 = _mixer_fwd(h1, wt["norm_mix"], m[3], m[4], m[5], wt, lay, ex)
    if ex:
        wt = dict(wt, **_as_operands(ex.weights("mix", "ffn2"), lay))
    h3, sv3 = _ffn_fwd("ffn2", h2, wt["norm_ffn2"], m[6], m[7], m[8], wt["w_ffn2_up"], wt["w_ffn2_down"])
    dh3, acc_f = _final_loss("final_loss", h3, tgt, wt["norm_final"])
    loss = jnp.sum(acc_f[1])
    dh2, gu2, gd2, dm3 = _ffn_bwd("ffn2", dh3, sv3, wt["norm_ffn2"], m[7], m[8], wt["w_ffn2_up"])
    if ex:
        ex.reduce("ffn2", dict(w_ffn2_up=gu2, w_ffn2_down=gd2))
    dh1, gmix, small, dm2 = _mixer_bwd(dh2, sv2, wt["norm_mix"], m[4], m[5], wt, lay, ex)
    if ex:
        ex.reduce("mixer", gmix)
    dx, gu1, gd1, dm1 = _ffn_bwd("ffn1", dh1, sv1, wt["norm_ffn1"], m[1], m[2], wt["w_ffn1_up"], ex,
                                 (lambda g: ex.reduce("down1", dict(w_ffn1_down=g))) if ex else None)
    dmod = jnp.stack([dm1[0], dm1[1], dm1[2], dm2[0], dm2[1], dm2[2], dm3[0], dm3[1], dm3[2]])
    big = dict(w_ffn1_up=gu1, w_ffn1_down=gd1, w_ffn2_up=gu2, w_ffn2_down=gd2, **gmix)
    small = dict(small, norm_ffn1=dm1[3], norm_mix=dm2[3], norm_ffn2=dm3[3], norm_final=acc_f[0])
    return loss, dx, dmod, big, small


def _position():
    return lax.axis_index("x"), lax.axis_index("y"), lax.axis_index("c")


_ANY = pl.BlockSpec(memory_space=pl.ANY)
_VMEM = pl.BlockSpec(memory_space=pltpu.VMEM)


def _allgather8(name, v):
    r = v.shape[0]

    def body(v_ref, out_ref, send_sems, recv_sems):
        x, y, c = _position()
        me = 4 * x + 2 * y + c
        out_ref[me] = v_ref[...]
        copies = []
        for mask in range(1, N_DEV):
            peer = tuple(1 - p if mask >> b & 1 else p for p, b in ((x, 2), (y, 1), (c, 0)))
            cp = pltpu.make_async_remote_copy(
                src_ref=v_ref, dst_ref=out_ref.at[me], send_sem=send_sems.at[mask - 1],
                recv_sem=recv_sems.at[mask - 1], device_id=peer, device_id_type=MESH)
            cp.start()
            copies.append(cp)
        for cp in copies:
            cp.wait()

    return pl.pallas_call(
        body, in_specs=[_VMEM], out_specs=_VMEM, out_shape=SDS((N_DEV, r, LANES), F32),
        scratch_shapes=[pltpu.SemaphoreType.DMA((N_DEV - 1,)), pltpu.SemaphoreType.DMA((N_DEV - 1,))],
        name=name)(v)


def _other_chips(x, y):
    return [(1 - x, y), (x, 1 - y), (1 - x, 1 - y)]


def _half_rows(c, rows):
    hr = rows // 2
    assert hr % 16 == 0
    return pl.ds(pl.multiple_of(c * hr, 16), hr)


class _Stage:
    def __init__(self, arrays, out_shapes, sems, plan, in_place=False):
        self.arrays, self.out_shapes, self.sems, self.plan = list(arrays), list(out_shapes), list(sems), plan
        self.alias_pairs = [(i, i) for i in range(len(self.arrays))] if in_place else []
        self.out_counts = [len(self.out_shapes)]

    def start(self, ins, outs, sems):
        for kind, cp in self.plan(ins, outs, sems):
            if kind != "recv":
                cp.start()

    def wait(self, ins, outs, sems):
        for kind, cp in self.plan(ins, outs, sems):
            {"local": cp.wait, "both": cp.wait, "send": cp.wait_send, "recv": cp.wait_recv}[kind]()

    def aliases(self, in_offset, out_offset):
        return {in_offset + i: out_offset + o for i, o in self.alias_pairs}


def _join(stages):
    ni = [0] + [len(st.arrays) for st in stages]
    no = [0] + [len(st.out_shapes) for st in stages]
    ns = [0] + [len(st.sems) for st in stages]
    for counts in (ni, no, ns):
        for k in range(1, len(counts)):
            counts[k] += counts[k - 1]

    def plan(ins, outs, sems):
        todo = []
        for k, st in enumerate(stages):
            todo += st.plan(ins[ni[k]:ni[k + 1]], outs[no[k]:no[k + 1]], sems[ns[k]:ns[k + 1]])
        return todo

    joined = _Stage([a for st in stages for a in st.arrays], [o for st in stages for o in st.out_shapes],
                    [m for st in stages for m in st.sems], plan)
    joined.alias_pairs = [(ni[k] + i, no[k] + o) for k, st in enumerate(stages) for i, o in st.alias_pairs]
    joined.out_counts = [len(st.out_shapes) for st in stages]
    return joined


def _run_stage(name, stage):
    n_in, n_out = len(stage.arrays), len(stage.out_shapes)

    def body(*refs):
        ins, outs, sems = refs[0:n_in], refs[n_in:n_in + n_out], refs[n_in + n_out:]
        stage.start(ins, outs, sems)
        stage.wait(ins, outs, sems)

    return pl.pallas_call(
        body, in_specs=[_ANY] * n_in, out_specs=[_ANY] * n_out, out_shape=stage.out_shapes,
        input_output_aliases=stage.aliases(0, 0), scratch_shapes=stage.sems, name=name)(*stage.arrays)


def _dma_sems(*counts):
    return [pltpu.SemaphoreType.DMA((n,)) for n in counts]


def _gather_send(shards):
    nw = len(shards)

    def plan(ins, outs, sems):
        send_sems, recv_sems, local_sems = sems
        x, y, c = _position()
        p = 2 * x + y
        todo = [("local", pltpu.make_async_copy(ins[w], outs[w].at[p], local_sems.at[w])) for w in range(nw)]
        for j, (cx, cy) in enumerate(_other_chips(x, y)):
            for w in range(nw):
                half = _half_rows(c, ins[w].shape[0])
                sem = dict(send_sem=send_sems.at[j * nw + w], recv_sem=recv_sems.at[j * nw + w], device_id_type=MESH)
                todo.append(("send", pltpu.make_async_remote_copy(
                    src_ref=ins[w].at[half], dst_ref=outs[w].at[p, half], device_id=(cx, cy, c), **sem)))
                landing = outs[w].at[2 * cx + cy, half]
                todo.append(("recv", pltpu.make_async_remote_copy(
                    src_ref=landing, dst_ref=landing, device_id=(x, y, c), **sem)))
        return todo

    return _Stage(shards, [SDS((N_CHIPS,) + v.shape, v.dtype) for v in shards], _dma_sems(3 * nw, 3 * nw, nw), plan)


def _gather_pass(gathered):
    nw = len(gathered)

    def plan(ins, outs, sems):
        send_sems, recv_sems = sems
        x, y, c = _position()
        todo = []
        for j, (cx, cy) in enumerate(_other_chips(x, y)):
            for w in range(nw):
                sem = dict(send_sem=send_sems.at[j * nw + w], recv_sem=recv_sems.at[j * nw + w], device_id_type=MESH)
                mine = outs[w].at[2 * cx + cy, _half_rows(c, outs[w].shape[1])]
                theirs = outs[w].at[2 * cx + cy, _half_rows(1 - c, outs[w].shape[1])]
                todo.append(("send", pltpu.make_async_remote_copy(
                    src_ref=mine, dst_ref=mine, device_id=(x, y, 1 - c), **sem)))
                todo.append(("recv", pltpu.make_async_remote_copy(
                    src_ref=theirs, dst_ref=theirs, device_id=(x, y, c), **sem)))
        return todo

    return _Stage(gathered, [SDS(g.shape, g.dtype) for g in gathered], _dma_sems(3 * nw, 3 * nw), plan, in_place=True)


def _swap_halves(gs):
    nw = len(gs)

    def plan(ins, outs, sems):
        x, y, c = _position()
        return [("both", pltpu.make_async_remote_copy(
            src_ref=ins[w].at[:, _half_rows(1 - c, ins[w].shape[1])], dst_ref=outs[w], send_sem=sems[0].at[w],
            recv_sem=sems[1].at[w], device_id=(x, y, 1 - c), device_id_type=MESH)) for w in range(nw)]

    return _Stage(gs, [SDS((g.shape[0], g.shape[1] // 2, g.shape[2]), g.dtype) for g in gs], _dma_sems(nw, nw), plan)


def _scatter_chips(vs):
    nw = len(vs)

    def plan(ins, outs, sems):
        x, y, c = _position()
        return [("both", pltpu.make_async_remote_copy(
            src_ref=ins[w].at[2 * cx + cy], dst_ref=outs[w].at[j], send_sem=sems[0].at[j * nw + w],
            recv_sem=sems[1].at[j * nw + w], device_id=(cx, cy, c), device_id_type=MESH))
            for j, (cx, cy) in enumerate(_other_chips(x, y)) for w in range(nw)]

    return _Stage(vs, [SDS((N_CHIPS - 1,) + v.shape[1:], v.dtype) for v in vs], _dma_sems(3 * nw, 3 * nw), plan)


def _share_halves(fulls):
    nw = len(fulls)

    def plan(ins, outs, sems):
        x, y, c = _position()
        todo = []
        for w in range(nw):
            rows = outs[w].at[_half_rows(c, outs[w].shape[0])]
            todo.append(("both", pltpu.make_async_remote_copy(
                src_ref=rows, dst_ref=rows, send_sem=sems[0].at[w], recv_sem=sems[1].at[w],
                device_id=(x, y, 1 - c), device_id_type=MESH)))
        return todo

    return _Stage(fulls, [SDS(f.shape, f.dtype) for f in fulls], _dma_sems(nw, nw), plan, in_place=True)


class _Hosted:
    def __init__(self, ex, name, grid):
        self.ex, self.name, self.grid = ex, name, grid
        self.stage = ex.host(name) if ex is not None else None
        st = self.stage
        self.arrays = list(st.arrays) if st else []
        self.out_shapes = list(st.out_shapes) if st else []
        self.sems = list(st.sems) if st else []
        self.in_specs, self.out_specs = [_ANY] * len(self.arrays), [_ANY] * len(self.out_shapes)

    def aliases(self, in_offset, out_offset):
        return self.stage.aliases(in_offset, out_offset) if self.stage else {}

    def split(self, rest, n_out):
        ni, no, ns = len(self.arrays), len(self.out_shapes), len(self.sems)
        self.ins, self.outs = rest[0:ni], rest[ni + n_out:ni + n_out + no]
        tail = rest[ni + n_out + no:]
        self.sem_refs = tail[len(tail) - ns:]
        return rest[ni:ni + n_out], tail[0:len(tail) - ns]

    def _at(self, last):
        conds = [pl.program_id(d) == (g - 1 if last else 0) for d, g in enumerate(self.grid)]
        return functools.reduce(lambda p, q: p & q, conds)

    def start(self):
        if self.stage:
            pl.when(self._at(False))(lambda: self.stage.start(self.ins, self.outs, self.sem_refs))

    def wait(self):
        if self.stage:
            pl.when(self._at(True))(lambda: self.stage.wait(self.ins, self.outs, self.sem_refs))

    def finish(self, out, n_out):
        out = list(out)
        if self.stage:
            self.ex.done(self.name, out[n_out:])
        return out[0:n_out]


GATHER = {"up1": ("w_ffn1_up",), "down1": ("w_ffn1_down",), "in": ("w_in",),
          "mix": ("w_a_out", "w_b_out", "w_out"), "ffn2": ("w_ffn2_up", "w_ffn2_down")}
REDUCE = {"ffn2": ("w_ffn2_up", "w_ffn2_down"), "mixer": ("w_in", "w_a_out", "w_b_out", "w_out"),
          "down1": ("w_ffn1_down",), "up1": ("w_ffn1_up",)}
HOSTS = {"ffn1_up": [("gather_send", "in")],
         "ffn1_down": [("gather_pass", "in"), ("gather_send", "mix")],
         "mix_in": [("gather_pass", "mix"), ("gather_send", "ffn2")],
         "mix_out": [("gather_pass", "ffn2")],
         "mix_gw_out": [("swap", "ffn2")], "delta_bwd_l2r": [("scatter", "ffn2")], "delta_bwd_r2l": [("share", "ffn2")],
         "ffn1_gw_down": [("swap", "mixer")], "ffn1_gw_up": [("scatter", "mixer"), ("swap", "down1")],
         "ffn1_du": [("share", "mixer"), ("scatter", "down1")]}


class _Exchange:
    def __init__(self, shards):
        self.shards = shards
        self.gathered, self.red, self.ready, self.reduced = {}, {}, {}, {}

    def _stage(self, kind, group):
        if kind == "gather_send":
            return _gather_send([self.shards[n] for n in GATHER[group]])
        if kind == "gather_pass":
            return _gather_pass(self.gathered[group])
        st = self.red[group]
        return {"swap": lambda: _swap_halves(st["parts"]), "scatter": lambda: _scatter_chips(st["chip_sums"]),
                "share": lambda: _share_halves(st["fulls"])}[kind]()

    def _done(self, kind, group, outs):
        if kind == "gather_send":
            self.gathered[group] = list(outs)
        elif kind == "gather_pass":
            self.ready.update(zip(GATHER[group], outs))
        elif kind == "swap":
            st = self.red[group]
            st["from_sib"] = list(outs)
            st["chip_sums"] = [_chip_sum("chip_sum_" + n, g, f) for n, g, f in zip(REDUCE[group], st["parts"], outs)]
        elif kind == "scatter":
            st = self.red[group]
            st["fulls"] = [_total("total_" + n, g, f, r)
                           for n, g, f, r in zip(REDUCE[group], st["parts"], st["from_sib"], outs)]
        else:
            self.reduced.update(zip(REDUCE[group], outs))

    def _all(self, steps, runner):
        stage = _join([self._stage(kind, group) for kind, group in steps])
        outs = list(runner(stage))
        for (kind, group), n in zip(steps, stage.out_counts):
            self._done(kind, group, outs[0:n])
            outs = outs[n:]

    def host(self, kernel_name):
        return _join([self._stage(*step) for step in HOSTS[kernel_name]]) if kernel_name in HOSTS else None

    def done(self, kernel_name, outs):
        self._all(HOSTS[kernel_name], lambda stage: outs)

    def run(self, *steps):
        name = "_".join(kind + "_" + group for kind, group in steps)
        self._all(steps, lambda stage: _run_stage(name, stage))

    def weights(self, *groups):
        return {n: self.ready[n] for g in groups for n in GATHER[g]}

    def reduce(self, group, parts):
        self.red[group] = dict(parts=[parts[n] for n in REDUCE[group]])


def _chip_sum(name, g, from_sib):
    _, r, cdim = g.shape
    hr = r // 2

    def body(g_ref, s_ref, o_ref):
        o_ref[...] = (g_ref[...].astype(F32) + s_ref[...].astype(F32)).astype(o_ref.dtype)

    blk = (None, hr, cdim)
    return pl.pallas_call(
        body, grid=(N_CHIPS,),
        in_specs=[pl.BlockSpec(blk, lambda j: (j, lax.axis_index("c"), 0)), pl.BlockSpec(blk, lambda j: (j, 0, 0))],
        out_specs=pl.BlockSpec(blk, lambda j: (j, 0, 0)),
        out_shape=SDS((N_CHIPS, hr, cdim), g.dtype), compiler_params=_params("parallel"), name=name)(g, from_sib)


def _total(name, g, from_sib, from_chips):
    _, r, cdim = g.shape
    hr = r // 2
    tr = _tile(hr, 256, 16)
    nt = hr // tr

    def body(g_ref, s_ref, rc_ref, o_ref):
        acc = g_ref[...].astype(F32) + s_ref[...].astype(F32)
        for j in range(N_CHIPS - 1):
            acc = acc + rc_ref[j].astype(F32)
        o_ref[...] = acc

    def chip():
        return 2 * lax.axis_index("x") + lax.axis_index("y")

    blk = (None, tr, cdim)
    return pl.pallas_call(
        body, grid=(nt,),
        in_specs=[pl.BlockSpec(blk, lambda i: (chip(), lax.axis_index("c") * nt + i, 0)),
                  pl.BlockSpec(blk, lambda i: (chip(), i, 0)),
                  pl.BlockSpec((N_CHIPS - 1, tr, cdim), lambda i: (0, i, 0))],
        out_specs=pl.BlockSpec((tr, cdim), lambda i: (lax.axis_index("c") * nt + i, 0)),
        out_shape=SDS((r, cdim), F32), compiler_params=_params("parallel"), name=name)(g, from_sib, from_chips)


def _sum8(name, v):
    _, r, w = v.shape

    def body(v_ref, o_ref):
        acc = v_ref[0]
        for j in range(1, N_DEV):
            acc = acc + v_ref[j]
        o_ref[...] = acc

    return pl.pallas_call(body, in_specs=[_VMEM], out_specs=_VMEM, out_shape=SDS((r, w), F32), name=name)(v)


def _adamw(name, w, g, m, v, tr=256):
    r, cdim = w.shape
    tr = _tile(r, tr, 8)
    bc1, bc2 = 1.0 - ADAM_B1 ** ADAM_STEP, 1.0 - ADAM_B2 ** ADAM_STEP

    def body(w_ref, g_ref, m_ref, v_ref, d_ref, nm_ref, nv_ref):
        g = g_ref[...]
        m2 = ADAM_B1 * m_ref[...] + (1.0 - ADAM_B1) * g
        v2 = ADAM_B2 * v_ref[...] + (1.0 - ADAM_B2) * (g * g)
        d_ref[...] = -ADAM_LR * ((m2 / bc1) / (jnp.sqrt(v2 / bc2) + ADAM_EPS) + ADAM_WD * w_ref[...])
        nm_ref[...] = m2
        nv_ref[...] = v2

    spec = _row(tr, cdim)
    out = SDS((r, cdim), F32)
    return pl.pallas_call(body, grid=(r // tr,), in_specs=[spec] * 4, out_specs=[spec] * 3, out_shape=[out] * 3,
                          compiler_params=_params("parallel"), name=name)(w, g, m, v)


def _pack_rows(arrays, width, row_mult, dtype):
    parts, spans, row = [], [], 0
    for a in arrays:
        n = a.size
        rows = -(-n // width)
        flat = a.reshape(-1).astype(dtype)
        if rows * width != n:
            flat = jnp.concatenate([flat, jnp.zeros((rows * width - n,), dtype)])
        parts.append(flat.reshape(rows, width))
        spans.append((row, rows, n, a.shape))
        row += rows
    pad = -row % row_mult
    if pad:
        parts.append(jnp.zeros((pad, width), dtype))
    return jnp.concatenate(parts, axis=0), spans


def _unpack_rows(packed, spans):
    return [packed[r0:r0 + rows].reshape(-1)[0:n].reshape(shape) for r0, rows, n, shape in spans]


BIG = ("w_ffn1_up", "w_ffn1_down", "w_in", "w_a_out", "w_b_out", "w_out", "w_ffn2_up", "w_ffn2_down")
COL_SHARDED = ("w_ffn1_up", "w_in", "w_a_out", "w_ffn2_up")
SMALL = ("b_ada", "norm_ffn1", "norm_mix", "conv_a", "conv_dn", "a_log_fwd", "dt_bias_fwd", "a_log_bwd",
         "dt_bias_bwd", "dn_norm", "norm_ffn2", "norm_final")
WEIGHTS = ("w_ada", "b_ada", "norm_ffn1", "w_ffn1_up", "w_ffn1_down", "norm_mix", "w_in", "conv_a", "conv_dn",
           "a_log_fwd", "dt_bias_fwd", "a_log_bwd", "dt_bias_bwd", "dn_norm", "w_a_out", "w_b_out", "w_out",
           "norm_ffn2", "w_ffn2_up", "w_ffn2_down", "norm_final")


def kernel(x, c, w_ada, b_ada, norm_ffn1, w_ffn1_up, w_ffn1_down, norm_mix, w_in, conv_a, conv_dn, a_log_fwd, dt_bias_fwd, a_log_bwd, dt_bias_bwd, dn_norm, w_a_out, w_b_out, w_out, norm_ffn2, w_ffn2_up, w_ffn2_down, norm_final, loss_target, m_w_ada, m_b_ada, m_norm_ffn1, m_w_ffn1_up, m_w_ffn1_down, m_norm_mix, m_w_in, m_conv_a, m_conv_dn, m_a_log_fwd, m_dt_bias_fwd, m_a_log_bwd, m_dt_bias_bwd, m_dn_norm, m_w_a_out, m_w_b_out, m_w_out, m_norm_ffn2, m_w_ffn2_up, m_w_ffn2_down, m_norm_final, v_w_ada, v_b_ada, v_norm_ffn1, v_w_ffn1_up, v_w_ffn1_down, v_norm_mix, v_w_in, v_conv_a, v_conv_dn, v_a_log_fwd, v_dt_bias_fwd, v_a_log_bwd, v_dt_bias_bwd, v_dn_norm, v_w_a_out, v_w_b_out, v_w_out, v_norm_ffn2, v_w_ffn2_up, v_w_ffn2_down, v_norm_final):
    given = dict(locals())
    wsh = {n: given[n] for n in WEIGHTS}
    msh = {n: given["m_" + n] for n in WEIGHTS}
    vsh = {n: given["v_" + n] for n in WEIGHTS}
    d = x.shape[-1]
    ca = conv_a.shape[-1] * N_CHIPS
    nh = conv_dn.shape[-1] * N_CHIPS // (3 * HEAD)
    lay = _Layout(d, ca, nh)
    xi, yi, ci = _position()
    chip = 2 * xi + yi
    me = 2 * chip + ci

    c_act = jax.nn.silu(c)
    g1, g1_spans = _pack_rows([c_act, conv_a[0], conv_dn[0]], LANES, 8, F32)
    g1_all = _allgather8("gather_cond", g1)
    per_dev = [_unpack_rows(g1_all[k], g1_spans) for k in range(N_DEV)]
    c_all = jnp.concatenate([p[0] for p in per_dev], axis=0)
    conv_a_full = jnp.concatenate([per_dev[2 * k][1] for k in range(N_CHIPS)], axis=1)
    conv_dn_full = jnp.concatenate([per_dev[2 * k][2] for k in range(N_CHIPS)], axis=1)

    mod_sh = _matmul("ada_mod", c_all, w_ada[0], "nn")
    b_sh = lax.dynamic_slice_in_dim(b_ada, chip * mod_sh.shape[1], mod_sh.shape[1], axis=1)
    g2, g2_spans = _pack_rows([mod_sh + b_sh], LANES, 8, F32)
    g2_all = _allgather8("gather_mod", g2)
    mod_all = jnp.concatenate([_unpack_rows(g2_all[2 * k], g2_spans)[0] for k in range(N_CHIPS)], axis=1)
    modv = lax.dynamic_index_in_dim(mod_all, me, 0, keepdims=False).reshape(9, d)

    ex = _Exchange({n: wsh[n][0].astype(BF16) for n in BIG})
    ex.run(("gather_send", "up1"))
    ex.run(("gather_pass", "up1"), ("gather_send", "down1"))
    wt = _as_operands(ex.weights("up1"), lay)
    lane_pad = (jnp.zeros((2 * nh,), F32), jnp.zeros((LANES - 4 * nh,), F32))
    pvec = jnp.stack([jnp.concatenate([lane_pad[0], a_log_fwd[0], a_log_bwd[0], lane_pad[1]]),
                      jnp.concatenate([lane_pad[0], dt_bias_fwd[0], dt_bias_bwd[0], lane_pad[1]])]
                     + [jnp.zeros((LANES,), F32)] * 6)
    wt.update(conv_a=conv_a_full, conv_dn=conv_dn_full, pvec=pvec, dn_norm=dn_norm, norm_ffn1=norm_ffn1,
              norm_mix=norm_mix, norm_ffn2=norm_ffn2, norm_final=norm_final.reshape(1, d))

    loss, dx, dmod, big, small = _local_step(x[0], loss_target[0], modv, wt, lay, ex)

    small_list = [loss.reshape(1, 1), dmod.reshape(1, 9 * d), small["norm_ffn1"], small["norm_mix"], small["conv_a"],
                  small["conv_dn"], small["a_log"][2 * nh:3 * nh], small["dt_bias"][2 * nh:3 * nh],
                  small["a_log"][3 * nh:4 * nh], small["dt_bias"][3 * nh:4 * nh], small["dn_norm"], small["norm_ffn2"],
                  small["norm_final"]]
    g3, g3_spans = _pack_rows(small_list, LANES, 8, F32)
    g3_all = _allgather8("gather_small_grads", g3)
    g_small = dict(zip(("loss",) + SMALL, _unpack_rows(_sum8("sum_small_grads", g3_all), g3_spans)))
    loss = g_small["loss"].reshape(())
    dmod_all = jnp.concatenate([_unpack_rows(g3_all[k], g3_spans)[1] for k in range(N_DEV)], axis=0)
    ncol = w_ada.shape[-1]
    dmod_sh = lax.dynamic_slice_in_dim(dmod_all, chip * ncol, ncol, axis=1)
    grads = {"w_ada": _matmul("ada_grad", c_all, dmod_sh, "tn")[None]}
    for n in SMALL:
        g = g_small[n]
        if n in ("conv_a", "conv_dn"):
            wloc = wsh[n].shape[-1]
            g = lax.dynamic_slice_in_dim(g, chip * wloc, wloc, axis=1)
        grads[n] = g.reshape(wsh[n].shape)

    ex.reduce("up1", big)
    ex.run(("share", "down1"), ("swap", "up1"))
    ex.run(("scatter", "up1"))
    ex.run(("share", "up1"))
    for n in BIG:
        grads[n] = ex.reduced[n][None]

    delta, new_m, new_v = {}, {}, {}
    for n in ("w_ada",) + BIG:
        shp = wsh[n].shape
        outs = _adamw("adamw_" + n, *(t.reshape(shp[-2], shp[-1]) for t in (wsh[n], grads[n], msh[n], vsh[n])))
        delta[n], new_m[n], new_v[n] = (o.reshape(shp) for o in outs)
    packed = []
    for src in (wsh, grads, msh, vsh):
        pk, s_spans = _pack_rows([src[n] for n in SMALL], LANES, 8, F32)
        packed.append(pk)
    outs = _adamw("adamw_small", *packed)
    for dst, o in zip((delta, new_m, new_v), outs):
        dst.update(zip(SMALL, _unpack_rows(o, s_spans)))

    return (loss, dx[None], *[grads[n] for n in WEIGHTS], *[delta[n] for n in WEIGHTS],
            *[new_m[n] for n in WEIGHTS], *[new_v[n] for n in WEIGHTS])
```

```python
import functools

import jax
import jax.numpy as jnp
from jax import lax
from jax.experimental import pallas as pl
from jax.experimental.pallas import tpu as pltpu

F32 = jnp.float32
BF16 = jnp.bfloat16
SDS = jax.ShapeDtypeStruct
MESH = pl.DeviceIdType.MESH
HI = lax.Precision.HIGHEST

EPS = 1e-6
HEAD = 128
CHUNK = 64
LANES = 128
N_CHIPS = 4
N_DEV = 8
VMEM_LIMIT = 56 * 1024 * 1024
FULL_K = 3072

ADAM_LR = 0.001
ADAM_B1 = 0.9
ADAM_B2 = 0.999
ADAM_EPS = 1e-08
ADAM_WD = 0.01
ADAM_STEP = 10


def _params(*sem):
    return pltpu.CompilerParams(dimension_semantics=sem, vmem_limit_bytes=VMEM_LIMIT)


def _tile(n, cap, mult=LANES):
    t = min(n, cap) // mult * mult
    while t >= mult:
        if n % t == 0:
            return t
        t -= mult
    return n


def _row(tr, w, cb=0):
    return pl.BlockSpec((tr, w), lambda i: (i, cb))


def _vec(r, w):
    return pl.BlockSpec((r, w), lambda i: (0, 0))


def _nn(a, b, **kw):
    return jnp.dot(a, b, preferred_element_type=F32, **kw)


def _nt(a, b, **kw):
    return lax.dot_general(a, b, (((1,), (1,)), ((), ())), preferred_element_type=F32, **kw)


def _tn(a, b, **kw):
    return lax.dot_general(a, b, (((0,), (0,)), ((), ())), preferred_element_type=F32, **kw)


def _bnn(a, b):
    return lax.dot_general(a, b, (((2,), (1,)), ((0,), (0,))), preferred_element_type=F32)


def _bnt(a, b):
    return lax.dot_general(a, b, (((2,), (2,)), ((0,), (0,))), preferred_element_type=F32)


def _btn(a, b):
    return lax.dot_general(a, b, (((1,), (1,)), ((0,), (0,))), preferred_element_type=F32)


def _silu_grad(x):
    s = jax.nn.sigmoid(x)
    return s * (1.0 + x * (1.0 - s))


def _matmul(name, a, b, mode, out_dtype=F32, tm=1024, tn=1024, tk=2048, out_pieces=0, ex=None):
    pieces_b = b.shape[0] if b.ndim == 3 else 0
    b2 = b.shape[1:] if pieces_b else b.shape
    if mode == "nn":
        (m, k), n = a.shape, b2[1] * max(pieces_b, 1)
    elif mode == "nt":
        (m, _), n, k = a.shape, b2[0], b2[1] * max(pieces_b, 1)
    else:
        (k, m), n = a.shape, b2[1] * max(pieces_b, 1)
    n_unit = n // max(out_pieces, 1) if mode == "nt" or not pieces_b else n // pieces_b
    if out_pieces and pieces_b and mode != "nt":
        assert out_pieces == pieces_b
    k_unit = k // pieces_b if (pieces_b and mode == "nt") else k
    tm, tn = _tile(m, tm), _tile(n_unit, tn)
    tk = k_unit if k_unit <= FULL_K else _tile(k_unit, tk)
    nk = k // tk
    n_per, k_per = n_unit // tn, k_unit // tk
    a_bytes, b_bytes = a.size * a.dtype.itemsize, b.size * b.dtype.itemsize
    j_outer = nk == 1 and b_bytes + a_bytes * (n // tn) < a_bytes + b_bytes * (m // tm)
    ij = (lambda g0, g1: (g1, g0)) if j_outer else (lambda g0, g1: (g0, g1))

    def spec(shape, pick):
        return pl.BlockSpec(shape, lambda g0, g1, l: pick(*ij(g0, g1), l))

    a_spec = {"nn": spec((tm, tk), lambda i, j, l: (i, l)), "nt": spec((tm, tk), lambda i, j, l: (i, l)),
              "tn": spec((tk, tm), lambda i, j, l: (l, i))}[mode]
    if not pieces_b:
        b_spec = {"nn": spec((tk, tn), lambda i, j, l: (l, j)), "nt": spec((tn, tk), lambda i, j, l: (j, l)),
                  "tn": spec((tk, tn), lambda i, j, l: (l, j))}[mode]
    elif mode == "nt":
        b_spec = spec((None, tn, tk), lambda i, j, l: (l // k_per, j, l % k_per))
    else:
        b_spec = spec((None, tk, tn), lambda i, j, l: (j // n_per, l, j % n_per))
    if out_pieces:
        o_spec = spec((None, tm, tn), lambda i, j, l: (j // n_per, i, j % n_per))
        o_shape = SDS((out_pieces, m, n // out_pieces), out_dtype)
    else:
        o_spec, o_shape = spec((tm, tn), lambda i, j, l: (i, j)), SDS((m, n), out_dtype)
    dot = {"nn": _nn, "nt": _nt, "tn": _tn}[mode]
    grid = (n // tn, m // tm, nk) if j_outer else (m // tm, n // tn, nk)
    host = _Hosted(ex, name, grid)

    def body(a_ref, b_ref, *rest):
        (o_ref,), scratch = host.split(rest, 1)
        host.start()
        part = dot(a_ref[...].astype(BF16), b_ref[...].astype(BF16))
        if nk == 1:
            o_ref[...] = part.astype(o_ref.dtype)
        else:
            l, acc = pl.program_id(2), scratch[0]

            @pl.when(l == 0)
            def _():
                acc[...] = part

            @pl.when((l > 0) & (l < nk - 1))
            def _():
                acc[...] += part

            @pl.when(l == nk - 1)
            def _():
                o_ref[...] = (acc[...] + part).astype(o_ref.dtype)
        host.wait()

    out = pl.pallas_call(
        body, grid=grid, in_specs=[a_spec, b_spec] + host.in_specs, out_specs=[o_spec] + host.out_specs,
        out_shape=[o_shape] + host.out_shapes, input_output_aliases=host.aliases(2, 1),
        scratch_shapes=([] if nk == 1 else [pltpu.VMEM((tm, tn), F32)]) + host.sems,
        compiler_params=_params(*(("arbitrary",) * 3 if host.stage else ("parallel", "parallel", "arbitrary"))),
        name=name)(a, b, *host.arrays)
    return host.finish(out, 1)[0]


def _norm_mod(name, h, nw, sh, sc, tr=1024):
    s, d = h.shape
    tr = _tile(s, tr, 8)

    def body(h_ref, nw_ref, sh_ref, sc_ref, u_ref):
        x = h_ref[...]
        r = lax.rsqrt(jnp.mean(x * x, axis=-1, keepdims=True) + EPS)
        u_ref[...] = (x * r * nw_ref[...] * (1.0 + sc_ref[...]) + sh_ref[...]).astype(BF16)

    return pl.pallas_call(
        body, grid=(s // tr,), in_specs=[_row(tr, d), _vec(1, d), _vec(1, d), _vec(1, d)],
        out_specs=_row(tr, d), out_shape=SDS((s, d), BF16),
        compiler_params=_params("parallel"), name=name)(h, nw, sh, sc)


def _norm_mod_bwd(name, h, du, dh, nw, sc, tr=512):
    s, d = h.shape
    tr = _tile(s, tr, 8)

    def body(h_ref, du_ref, dh_ref, nw_ref, sc_ref, o_ref, acc_ref):
        @pl.when(pl.program_id(0) == 0)
        def _():
            acc_ref[...] = jnp.zeros_like(acc_ref)

        x, g = h_ref[...], du_ref[...]
        r = lax.rsqrt(jnp.mean(x * x, axis=-1, keepdims=True) + EPS)
        n = x * r
        nw, sc1 = nw_ref[...], 1.0 + sc_ref[...]
        dn = g * sc1 * nw
        o_ref[...] = dh_ref[...] + r * (dn - n * jnp.mean(dn * n, axis=-1, keepdims=True))
        gn = g * n
        acc_ref[0:1, :] += jnp.sum(g, axis=0, keepdims=True)
        acc_ref[1:2, :] += jnp.sum(gn * nw, axis=0, keepdims=True)
        acc_ref[2:3, :] += jnp.sum(gn * sc1, axis=0, keepdims=True)

    return pl.pallas_call(
        body, grid=(s // tr,),
        in_specs=[_row(tr, d), _row(tr, d), _row(tr, d), _vec(1, d), _vec(1, d)],
        out_specs=[_row(tr, d), _vec(8, d)], out_shape=[SDS((s, d), F32), SDS((8, d), F32)],
        compiler_params=_params("arbitrary"), name=name)(h, du, dh, nw, sc)


def _resid_bwd(name, dh, f, g, scale, tr=1024):
    s, d = dh.shape
    tr = _tile(s, tr, 8)

    def body(dh_ref, f_ref, g_ref, o_ref, acc_ref):
        @pl.when(pl.program_id(0) == 0)
        def _():
            acc_ref[...] = jnp.zeros_like(acc_ref)

        x = dh_ref[...]
        o_ref[...] = ((scale * g_ref[...]) * x).astype(BF16)
        acc_ref[0:1, :] += jnp.sum(scale * x * f_ref[...], axis=0, keepdims=True)

    return pl.pallas_call(
        body, grid=(s // tr,), in_specs=[_row(tr, d), _row(tr, d), _vec(1, d)],
        out_specs=[_row(tr, d), _vec(8, d)], out_shape=[SDS((s, d), BF16), SDS((8, d), F32)],
        compiler_params=_params("arbitrary"), name=name)(dh, f, g)


def _final_loss(name, h, tgt, nw, tr=512):
    s, d = h.shape
    tr = _tile(s, tr, 8)

    def body(h_ref, t_ref, nw_ref, o_ref, acc_ref):
        @pl.when(pl.program_id(0) == 0)
        def _():
            acc_ref[...] = jnp.zeros_like(acc_ref)

        x, nw = h_ref[...], nw_ref[...]
        r = lax.rsqrt(jnp.mean(x * x, axis=-1, keepdims=True) + EPS)
        n = x * r
        diff = n * nw - t_ref[...]
        dy = diff * (1.0 / d)
        dn = dy * nw
        o_ref[...] = r * (dn - n * jnp.mean(dn * n, axis=-1, keepdims=True))
        acc_ref[0:1, :] += jnp.sum(dy * n, axis=0, keepdims=True)
        acc_ref[1:2, :] += jnp.sum(diff * diff, axis=0, keepdims=True) * (0.5 / d)

    return pl.pallas_call(
        body, grid=(s // tr,), in_specs=[_row(tr, d), _row(tr, d), _vec(1, d)],
        out_specs=[_row(tr, d), _vec(8, d)], out_shape=[SDS((s, d), F32), SDS((8, d), F32)],
        compiler_params=_params("arbitrary"), name=name)(h, tgt, nw)


class _Layout:
    def __init__(self, d, ca, nh):
        self.d, self.ca, self.nh = d, ca, nh
        self.qk = nh * HEAD
        self.qkv = 3 * self.qk
        self.z = self.qkv
        self.ga = self.z + self.qk
        self.cab = self.ga + 2 * d
        self.ba = self.cab + 3 * ca
        self.tail = _tile(self.ba, 512)
        self.total = self.ba + self.tail
        assert self.qkv % self.qk == 0 and self.ga % (2 * d) == 0 and self.cab % (3 * ca) == 0
        assert self.ba % self.tail == 0 and 4 * nh <= LANES

    def perm_cols(self, w):
        ca, qkv, qk, d, nh = self.ca, self.qkv, self.qk, self.d, self.nh
        o = [0, ca, 2 * ca, 3 * ca, 3 * ca + qkv, 3 * ca + qkv + qk, 3 * ca + qkv + qk + 4 * nh]
        cb, cc, cv = (w[..., o[i]:o[i + 1]] for i in range(3))
        x_qkv, x_z, x_ba = w[..., o[3]:o[4]], w[..., o[4]:o[5]], w[..., o[5]:o[6]]
        gates = w[..., o[6]:o[6] + 2 * d]
        pad = jnp.zeros(w.shape[:-1] + (self.tail - 4 * nh,), w.dtype)
        return jnp.concatenate([x_qkv, x_z, gates, cb, cc, cv, x_ba, pad], axis=-1)

    def unperm_cols(self, w):
        ca, nh = self.ca, self.nh
        cb, cc, cv = (w[..., self.cab + i * ca:self.cab + (i + 1) * ca] for i in range(3))
        return jnp.concatenate([cb, cc, cv, w[..., 0:self.qkv], w[..., self.z:self.ga],
                                w[..., self.ba:self.ba + 4 * nh], w[..., self.ga:self.cab]], axis=-1)


def _halo_specs(tr, w, cb, s):
    nb8 = s // 8
    return [pl.BlockSpec((8, w), lambda i: (jnp.maximum(i * (tr // 8) - 1, 0), cb)),
            pl.BlockSpec((tr, w), lambda i: (i, cb)),
            pl.BlockSpec((8, w), lambda i: (jnp.minimum((i + 1) * (tr // 8), nb8 - 1), cb))]


def _ext(prev_ref, main_ref, next_ref, i, nt):
    p = jnp.where(i > 0, prev_ref[...].astype(F32), 0.0)
    n = jnp.where(i < nt - 1, next_ref[...].astype(F32), 0.0)
    return jnp.concatenate([p, main_ref[...].astype(F32), n], axis=0)


def _shift(x, k):
    return x if k == 0 else pltpu.roll(x, (-k) % x.shape[0], 0)


def _conv_taps(x_ext, w, tr):
    kt = w.shape[0]
    acc = None
    for t in range(kt):
        term = _shift(x_ext, t - kt // 2)[8:8 + tr] * w[t:t + 1, :]
        acc = term if acc is None else acc + term
    return acc


def _prep_a(name, proj, conv_a, lay, tr=256):
    s, ca = proj.shape[0], lay.ca
    tr = _tile(s, tr, 8)
    nt, w = s // tr, 3 * ca

    def body(p_ref, m_ref, n_ref, w_ref, o_ref):
        x = _ext(p_ref, m_ref, n_ref, pl.program_id(0), nt)
        xv = x[:, ca:2 * ca] * x[:, 2 * ca:w]
        y = _conv_taps(xv, w_ref[...], tr)
        o_ref[...] = (m_ref[:, 0:ca] * y).astype(BF16)

    return pl.pallas_call(
        body, grid=(nt,), in_specs=_halo_specs(tr, w, lay.cab // w, s) + [_vec(conv_a.shape[0], ca)],
        out_specs=_row(tr, ca), out_shape=SDS((s, ca), BF16),
        compiler_params=_params("parallel"), name=name)(proj, proj, proj, conv_a)


def _prep_a_bwd(name, dya, proj, conv_a, dproj, lay, tr=256):
    s, ca = proj.shape[0], lay.ca
    tr = _tile(s, tr, 8)
    nt, w, kt = s // tr, 3 * ca, conv_a.shape[0]

    def body(p_ref, m_ref, n_ref, dp_ref, dm_ref, dn_ref, w_ref, _, o_ref, acc_ref):
        i = pl.program_id(0)

        @pl.when(i == 0)
        def _():
            acc_ref[...] = jnp.zeros_like(acc_ref)

        x = _ext(p_ref, m_ref, n_ref, i, nt)
        d_ext = _ext(dp_ref, dm_ref, dn_ref, i, nt)
        cb, cc, cv = x[:, 0:ca], x[:, ca:2 * ca], x[:, 2 * ca:w]
        xv = cc * cv
        wv = w_ref[...]
        dy_ext = d_ext * cb
        dx = None
        for t in range(kt):
            term = _shift(dy_ext, kt // 2 - t)[8:8 + tr] * wv[t:t + 1, :]
            dx = term if dx is None else dx + term
            acc_ref[t:t + 1, :] += jnp.sum(dy_ext[8:8 + tr] * _shift(xv, t - kt // 2)[8:8 + tr],
                                           axis=0, keepdims=True)
        y = _conv_taps(xv, wv, tr)
        o_ref[:, 0:ca] = (dm_ref[...] * y).astype(BF16)
        o_ref[:, ca:2 * ca] = (dx * cv[8:8 + tr]).astype(BF16)
        o_ref[:, 2 * ca:w] = (dx * cc[8:8 + tr]).astype(BF16)

    return pl.pallas_call(
        body, grid=(nt,),
        in_specs=_halo_specs(tr, w, lay.cab // w, s) + _halo_specs(tr, ca, 0, s)
        + [_vec(kt, ca), pl.BlockSpec(memory_space=pl.ANY)],
        out_specs=[_row(tr, w, lay.cab // w), _vec(8, ca)],
        out_shape=[SDS(dproj.shape, dproj.dtype), SDS((8, ca), F32)], input_output_aliases={7: 0},
        compiler_params=_params("arbitrary"), name=name)(proj, proj, proj, dya, dya, dya, conv_a, dproj)


def _qkv_act(c, nh, tr_rows):
    sact = jax.nn.silu(c)
    outs, inv = [], []
    for hd in range(3 * nh):
        sl = sact[:, hd * HEAD:(hd + 1) * HEAD]
        if hd < 2 * nh:
            r = lax.rsqrt(jnp.sum(sl * sl, axis=-1, keepdims=True) + EPS)
            inv.append(r)
            outs.append(sl * (r * (HEAD ** -0.5 if hd < nh else 1.0)))
        else:
            outs.append(sl)
    return jnp.concatenate(outs, axis=-1), sact, inv


def _prep_b(name, proj, conv_dn, lay, tr=256):
    s, w, nh = proj.shape[0], lay.qkv, lay.nh
    tr = _tile(s, tr, 8)
    nt = s // tr

    def body(p_ref, m_ref, n_ref, w_ref, o_ref):
        x = _ext(p_ref, m_ref, n_ref, pl.program_id(0), nt)
        c = _conv_taps(x, w_ref[...], tr)
        o_ref[...] = _qkv_act(c, nh, tr)[0]

    return pl.pallas_call(
        body, grid=(nt,), in_specs=_halo_specs(tr, w, 0, s) + [_vec(conv_dn.shape[0], w)],
        out_specs=_row(tr, w), out_shape=SDS((s, w), F32),
        compiler_params=_params("parallel"), name=name)(proj, proj, proj, conv_dn)


def _prep_b_bwd(name, dq, dk, dv, proj, conv_dn, dproj, lay, tr=256):
    s, w, nh, qk = proj.shape[0], lay.qkv, lay.nh, lay.qk
    tr = _tile(s, tr, 8)
    nt, kt = s // tr, conv_dn.shape[0]

    def body(*refs):
        x_refs, g_refs = refs[0:3], refs[3:12]
        w_ref, o_ref, acc_ref = refs[12], refs[14], refs[15]
        i = pl.program_id(0)

        @pl.when(i == 0)
        def _():
            acc_ref[...] = jnp.zeros_like(acc_ref)

        x = _ext(*x_refs, i, nt)
        wv = w_ref[...]
        c = None
        for t in range(kt):
            term = _shift(x, t - kt // 2) * wv[t:t + 1, :]
            c = term if c is None else c + term
        sig = jax.nn.sigmoid(c)
        sact = c * sig
        ds = []
        for part in range(3):
            g = _ext(*g_refs[3 * part:3 * part + 3], i, nt)
            for hd in range(nh):
                sl = sact[:, part * qk + hd * HEAD:part * qk + (hd + 1) * HEAD]
                gh = g[:, hd * HEAD:(hd + 1) * HEAD]
                if part < 2:
                    r = lax.rsqrt(jnp.sum(sl * sl, axis=-1, keepdims=True) + EPS)
                    sc = HEAD ** -0.5 if part == 0 else 1.0
                    ds.append(sc * r * (gh - sl * (r * r) * jnp.sum(gh * sl, axis=-1, keepdims=True)))
                else:
                    ds.append(gh)
        dc = jnp.concatenate(ds, axis=-1) * (sig * (1.0 + c * (1.0 - sig)))
        dx = None
        for t in range(kt):
            term = _shift(dc, kt // 2 - t)[8:8 + tr] * wv[t:t + 1, :]
            dx = term if dx is None else dx + term
            acc_ref[t:t + 1, :] += jnp.sum(dc[8:8 + tr] * _shift(x, t - kt // 2)[8:8 + tr],
                                           axis=0, keepdims=True)
        o_ref[...] = dx.astype(BF16)

    return pl.pallas_call(
        body, grid=(nt,),
        in_specs=_halo_specs(tr, w, 0, s) + _halo_specs(tr, qk, 0, s) * 3
        + [_vec(kt, w), pl.BlockSpec(memory_space=pl.ANY)],
        out_specs=[_row(tr, w, 0), _vec(8, w)],
        out_shape=[SDS(dproj.shape, dproj.dtype), SDS((8, w), F32)], input_output_aliases={13: 0},
        compiler_params=_params("arbitrary"), name=name)(
            proj, proj, proj, dq, dq, dq, dk, dk, dk, dv, dv, dv, conv_dn, dproj)


def _softplus(x):
    return jnp.maximum(x, 0.0) + jnp.log(1.0 + jnp.exp(-jnp.abs(x)))


def _split3(x):
    hi = x.astype(BF16)
    r = x - hi.astype(F32)
    mid = r.astype(BF16)
    return hi, mid, (r - mid.astype(F32)).astype(BF16)


def _exact_nn(m, x):
    m = m.astype(BF16)
    hi, mid, lo = _split3(x)
    return _nn(m, hi) + _nn(m, mid) + _nn(m, lo)


def _chunk_cumsum_masks(tr):
    ri = lax.broadcasted_iota(jnp.int32, (tr, tr), 0)
    ci = lax.broadcasted_iota(jnp.int32, (tr, tr), 1)
    same = (ri // CHUNK) == (ci // CHUNK)
    return (same & (ci <= ri)).astype(F32), (same & (ci >= ri)).astype(F32)


def _prep_c(name, proj, pvec, lay, tr=512):
    s, nh = proj.shape[0], lay.nh
    tr = _tile(s, tr, CHUNK)
    assert 6 * nh <= LANES

    def body(x_ref, p_ref, o_ref):
        x = x_ref[...]
        lane = lax.broadcasted_iota(jnp.int32, x.shape, 1)
        is_g = (lane >= 2 * nh) & (lane < 4 * nh)
        g = jnp.where(is_g, -jnp.exp(p_ref[0:1, :]) * _softplus(x + p_ref[1:2, :]), 0.0)
        m_f, m_b = _chunk_cumsum_masks(tr)
        gc = jnp.where(lane < 3 * nh, _exact_nn(m_f, g), _exact_nn(m_b, g))
        gc = pltpu.roll(gc, 2 * nh, 1)
        o_ref[...] = jnp.where(lane < 2 * nh, jax.nn.sigmoid(x), jnp.where(lane < 4 * nh, g, gc))

    return pl.pallas_call(
        body, grid=(s // tr,), in_specs=[_row(tr, LANES, lay.ba // LANES), _vec(8, LANES)],
        out_specs=_row(tr, LANES), out_shape=SDS((s, LANES), F32),
        compiler_params=_params("parallel"), name=name)(proj, pvec)


def _prep_c_bwd(name, dbg_f, dbg_b, proj, pvec, dproj, lay, tr=512):
    s, nh, tail = proj.shape[0], lay.nh, lay.tail
    tr = _tile(s, tr, CHUNK)

    def body(x_ref, df_ref, db_ref, p_ref, _, o_ref, acc_ref):
        @pl.when(pl.program_id(0) == 0)
        def _():
            acc_ref[...] = jnp.zeros_like(acc_ref)

        x = x_ref[...]
        lane = lax.broadcasted_iota(jnp.int32, x.shape, 1)
        is_b, is_g = lane < 2 * nh, (lane >= 2 * nh) & (lane < 4 * nh)
        fwd_lane = (lane < nh) | ((lane >= 2 * nh) & (lane < 3 * nh))
        d = jnp.where(lane < 4 * nh, jnp.where(fwd_lane, df_ref[...], db_ref[...]), 0.0)
        m_f, m_b = _chunk_cumsum_masks(tr)
        dgc = jnp.where(is_g, d, 0.0)
        dg = jnp.where(fwd_lane, _exact_nn(m_b, dgc), _exact_nn(m_f, dgc))
        sb = jax.nn.sigmoid(x)
        na = -jnp.exp(p_ref[0:1, :])
        xs = x + p_ref[1:2, :]
        dsp = dg * na * jax.nn.sigmoid(xs)
        dx = jnp.where(is_b, d * sb * (1.0 - sb), jnp.where(is_g, dsp, 0.0))
        o_ref[...] = jnp.zeros_like(o_ref)
        o_ref[:, 0:LANES] = dx.astype(BF16)
        acc_ref[0:1, :] += jnp.sum(jnp.where(is_g, dg * na * _softplus(xs), 0.0), axis=0, keepdims=True)
        acc_ref[1:2, :] += jnp.sum(jnp.where(is_g, dsp, 0.0), axis=0, keepdims=True)

    return pl.pallas_call(
        body, grid=(s // tr,),
        in_specs=[_row(tr, LANES, lay.ba // LANES), _row(tr, LANES), _row(tr, LANES), _vec(8, LANES),
                  pl.BlockSpec(memory_space=pl.ANY)],
        out_specs=[_row(tr, tail, lay.ba // tail), _vec(8, LANES)],
        out_shape=[SDS(dproj.shape, dproj.dtype), SDS((8, LANES), F32)], input_output_aliases={4: 0},
        compiler_params=_params("arbitrary"), name=name)(proj, dbg_f, dbg_b, pvec, dproj)


def _post(name, o_f, o_b, proj, dn_w, lay, tr=256):
    s, qk, nh = o_f.shape[0], lay.qk, lay.nh
    tr = _tile(s, tr, 8)

    def body(f_ref, b_ref, z_ref, w_ref, o_ref):
        o = f_ref[...] + b_ref[...]
        gate = jax.nn.silu(z_ref[...])
        for hd in range(nh):
            sl = slice(hd * HEAD, (hd + 1) * HEAD)
            oh = o[:, sl]
            r = lax.rsqrt(jnp.mean(oh * oh, axis=-1, keepdims=True) + EPS)
            o_ref[:, sl] = (oh * r * w_ref[...] * gate[:, sl]).astype(BF16)

    return pl.pallas_call(
        body, grid=(s // tr,),
        in_specs=[_row(tr, qk), _row(tr, qk), _row(tr, qk, lay.z // qk), _vec(1, HEAD)],
        out_specs=_row(tr, qk), out_shape=SDS((s, qk), BF16),
        compiler_params=_params("parallel"), name=name)(o_f, o_b, proj, dn_w)


def _post_bwd(name, dyb, o_f, o_b, proj, dn_w, dproj, lay, tr=256):
    s, qk, nh = o_f.shape[0], lay.qk, lay.nh
    tr = _tile(s, tr, 8)

    def body(d_ref, f_ref, b_ref, z_ref, w_ref, _, do_ref, dz_ref, acc_ref):
        @pl.when(pl.program_id(0) == 0)
        def _():
            acc_ref[...] = jnp.zeros_like(acc_ref)

        o, z, d, wv = f_ref[...] + b_ref[...], z_ref[...], d_ref[...], w_ref[...]
        gate = jax.nn.silu(z)
        dgate = _silu_grad(z)
        for hd in range(nh):
            sl = slice(hd * HEAD, (hd + 1) * HEAD)
            oh, dh = o[:, sl], d[:, sl]
            r = lax.rsqrt(jnp.mean(oh * oh, axis=-1, keepdims=True) + EPS)
            n = oh * r
            dz_ref[:, sl] = (dh * n * wv * dgate[:, sl]).astype(BF16)
            don = dh * gate[:, sl]
            acc_ref[0:1, :] += jnp.sum(don * n, axis=0, keepdims=True)
            dn = don * wv
            do_ref[:, sl] = r * (dn - n * jnp.mean(dn * n, axis=-1, keepdims=True))

    return pl.pallas_call(
        body, grid=(s // tr,),
        in_specs=[_row(tr, qk), _row(tr, qk), _row(tr, qk), _row(tr, qk, lay.z // qk), _vec(1, HEAD),
                  pl.BlockSpec(memory_space=pl.ANY)],
        out_specs=[_row(tr, qk), _row(tr, qk, lay.z // qk), _vec(8, HEAD)],
        out_shape=[SDS((s, qk), F32), SDS(dproj.shape, dproj.dtype), SDS((8, HEAD), F32)],
        input_output_aliases={5: 1},
        compiler_params=_params("arbitrary"), name=name)(dyb, o_f, o_b, proj, dn_w, dproj)


def _b_out_merge(name, yb, w_b, pa, proj, lay, tr=512):
    s, d = pa.shape
    k = yb.shape[1]
    tr = _tile(s, tr, 16)

    def body(y_ref, w_ref, a_ref, g_ref, pb_ref, o_ref):
        pb = _nn(y_ref[...], w_ref[...])
        pb_ref[...] = pb
        o_ref[...] = (jax.nn.sigmoid(g_ref[:, 0:d]) * a_ref[...] + jax.nn.sigmoid(g_ref[:, d:2 * d]) * pb).astype(BF16)

    return pl.pallas_call(
        body, grid=(s // tr,),
        in_specs=[_row(tr, k), _vec(k, d), _row(tr, d), _row(tr, 2 * d, lay.ga // (2 * d))],
        out_specs=[_row(tr, d), _row(tr, d)], out_shape=[SDS((s, d), F32), SDS((s, d), BF16)],
        compiler_params=_params("parallel"), name=name)(yb, w_b, pa, proj)


def _merge_bwd(name, dmg, pa, pb, proj, lay, tr=512):
    s, d = pa.shape
    tr = _tile(s, tr, 8)

    def body(d_ref, a_ref, b_ref, g_ref, da_ref, db_ref, dg_ref):
        dm = d_ref[...]
        sa, sb = jax.nn.sigmoid(g_ref[:, 0:d]), jax.nn.sigmoid(g_ref[:, d:2 * d])
        da_ref[...] = (sa * dm).astype(BF16)
        db_ref[...] = (sb * dm).astype(BF16)
        dg_ref[:, 0:d] = (dm * a_ref[...] * sa * (1.0 - sa)).astype(BF16)
        dg_ref[:, d:2 * d] = (dm * b_ref[...] * sb * (1.0 - sb)).astype(BF16)

    return pl.pallas_call(
        body, grid=(s // tr,),
        in_specs=[_row(tr, d), _row(tr, d), _row(tr, d), _row(tr, 2 * d, lay.ga // (2 * d))],
        out_specs=[_row(tr, d), _row(tr, d), _row(tr, 2 * d, lay.ga // (2 * d))],
        out_shape=[SDS((s, d), BF16), SDS((s, d), BF16), SDS((s, lay.total), BF16)],
        compiler_params=_params("parallel"), name=name)(dmg, pa, pb, proj)


def _tri_inverse(a_mat, ri, ci):
    def same(shift):
        return (ri >> shift) == (ci >> shift)

    x = -jnp.where(same(3), a_mat, 0.0)
    t_mat = (ri == ci).astype(F32) + x
    for _ in range(2):
        x = _bnn(x, x)
        t_mat = t_mat + _bnn(t_mat, x)
    for shift in (3, 4, 5):
        b = jnp.where(same(shift + 1) & ~same(shift), a_mat, 0.0)
        t_mat = t_mat - _bnn(_bnn(t_mat, b), t_mat)
    return t_mat


def _chunk_terms(q, k, v, beta, gc, g_row, g_last, reverse, t_mat=None):
    c = CHUNK
    ri = lax.broadcasted_iota(jnp.int32, (c, c), 0)
    ci = lax.broadcasted_iota(jnp.int32, (c, c), 1)
    if reverse:
        incl, strict = ri <= ci, ri < ci
    else:
        incl, strict = ri >= ci, ri > ci
    decay = jnp.where(incl, jnp.exp(jnp.where(incl, gc - g_row, 0.0)), 0.0)
    e = jnp.exp(gc)
    ed = jnp.exp(g_last - gc)
    el = jnp.exp(g_last)
    kb = k * beta
    kk_qk = _bnt(jnp.concatenate([kb, q], axis=1), k)
    a_mat = jnp.where(strict, kk_qk[:, 0:c] * decay, 0.0)
    p_mat = jnp.where(incl, kk_qk[:, c:2 * c] * decay, 0.0)
    if t_mat is None:
        t_mat = _tri_inverse(a_mat, ri, ci)
    uw = _bnn(t_mat, jnp.concatenate([v * beta, kb * e], axis=2))
    return dict(incl=incl, strict=strict, decay=decay, e=e, ed=ed, el=el, kb=kb,
                a=a_mat, t=t_mat, uw=uw, u=uw[:, :, 0:HEAD], w=uw[:, :, HEAD:2 * HEAD], p=p_mat)


def _delta_specs(nh, tb, nb, reverse):
    tok = (lambda i: nb - 1 - i) if reverse else (lambda i: i)
    hw = nh * HEAD
    qkv = [pl.BlockSpec((tb, hw), functools.partial(lambda i, part: (tok(i), part), part=p)) for p in range(3)]
    rows = pl.BlockSpec((tb, hw), lambda i: (tok(i), 0))
    bg = pl.BlockSpec((tb, LANES), lambda i: (tok(i), 0))
    gct = pl.BlockSpec((2 * nh, tb), lambda i: (0, tok(i)))
    st = pl.BlockSpec((nh, tb // CHUNK, HEAD, HEAD), lambda i: (0, tok(i), 0, 0))
    tri = pl.BlockSpec((nh, tb // CHUNK, CHUNK, CHUNK), lambda i: (0, tok(i), 0, 0))
    return qkv, rows, bg, gct, st, tri


def _heads(ref, rows, nh):
    return jnp.stack([ref[rows, hd * HEAD:(hd + 1) * HEAD] for hd in range(nh)])


def _chunk_scalars(bg_ref, gct_ref, cj, nh, tb, reverse):
    rows = pl.ds(cj * CHUNK, CHUNK)
    lb = nh if reverse else 0
    lc = 4 * nh + lb
    last = cj * CHUNK + (0 if reverse else CHUNK - 1)
    g_lanes = gct_ref[lb:lb + nh, :]
    if cj:
        g_lanes = pltpu.roll(g_lanes, tb - cj * CHUNK, 1)
    col = lambda l0, r: jnp.stack([bg_ref[r, l0 + hd:l0 + hd + 1] for hd in range(nh)])
    return col(lb, rows), col(lc, rows), g_lanes[:, 0:CHUNK][:, None, :], col(lc, pl.ds(last, 1))


def _delta_fwd(name, qkvn, bg, gct, nh, reverse, tb=256):
    s = qkvn.shape[0]
    tb = _tile(s, tb, LANES)
    nb, cpb = s // tb, tb // CHUNK
    qkv, rows_spec, bg_spec, gct_spec, st, tri = _delta_specs(nh, tb, nb, reverse)

    def body(q_ref, k_ref, v_ref, bg_ref, gct_ref, o_ref, st_ref, tri_ref, state):
        @pl.when(pl.program_id(0) == 0)
        def _():
            state[...] = jnp.zeros_like(state)

        for cj in (range(cpb - 1, -1, -1) if reverse else range(cpb)):
            rows = pl.ds(cj * CHUNK, CHUNK)
            q, k, v = _heads(q_ref, rows, nh), _heads(k_ref, rows, nh), _heads(v_ref, rows, nh)
            tm = _chunk_terms(q, k, v, *_chunk_scalars(bg_ref, gct_ref, cj, nh, tb, reverse), reverse)
            s_in = state[...]
            st_ref[:, cj] = s_in
            tri_ref[:, cj] = tm["t"]
            ws_qs = _bnn(jnp.concatenate([tm["w"], q * tm["e"]], axis=1), s_in)
            vn = tm["u"] - ws_qs[:, 0:CHUNK]
            o = ws_qs[:, CHUNK:2 * CHUNK] + _bnn(tm["p"], vn)
            for hd in range(nh):
                o_ref[rows, hd * HEAD:(hd + 1) * HEAD] = o[hd]
            state[...] = s_in * tm["el"] + _btn(k * tm["ed"], vn)

    return pl.pallas_call(
        body, grid=(nb,), in_specs=qkv + [bg_spec, gct_spec], out_specs=[rows_spec, st, tri],
        out_shape=[SDS((s, nh * HEAD), F32), SDS((nh, s // CHUNK, HEAD, HEAD), F32),
                   SDS((nh, s // CHUNK, CHUNK, CHUNK), F32)],
        scratch_shapes=[pltpu.VMEM((nh, HEAD, HEAD), F32)],
        compiler_params=_params("arbitrary"), name=name)(qkvn, qkvn, qkvn, bg, gct)


def _delta_bwd(name, qkvn, bg, gct, do, states, tris, nh, reverse, add=None, tb=128, ex=None):
    s = qkvn.shape[0]
    tb = _tile(s, tb, LANES)
    nb, cpb = s // tb, tb // CHUNK
    qkv, rows_spec, bg_spec, gct_spec, st, tri = _delta_specs(nh, tb, nb, not reverse)
    n_add = 0 if add is None else 3
    host = _Hosted(ex, name, (nb,))

    def body(*refs):
        q_ref, k_ref, v_ref, bg_ref, gct_ref, do_ref, st_ref, tri_ref = refs[0:8]
        add_refs = refs[8:8 + n_add]
        (dq_ref, dk_ref, dv_ref, dbg_ref), (dstate,) = host.split(refs[8 + n_add:], 4)
        host.start()

        @pl.when(pl.program_id(0) == 0)
        def _():
            dstate[...] = jnp.zeros_like(dstate)

        ones = jnp.ones((nh, 2 * CHUNK, HEAD), BF16)
        row_id = lax.broadcasted_iota(jnp.int32, (CHUNK, 1), 0)
        rsum = lambda x: jnp.sum(x, axis=2, keepdims=True)
        for cj in (range(cpb) if reverse else range(cpb - 1, -1, -1)):
            rows = pl.ds(cj * CHUNK, CHUNK)
            q, k, v, d_o = (_heads(r, rows, nh) for r in (q_ref, k_ref, v_ref, do_ref))
            beta, gc, g_row, g_last = _chunk_scalars(bg_ref, gct_ref, cj, nh, tb, reverse)
            tm = _chunk_terms(q, k, v, beta, gc, g_row, g_last, reverse, t_mat=tri_ref[:, cj])
            incl, strict, e, ed, el, kb = tm["incl"], tm["strict"], tm["e"], tm["ed"], tm["el"], tm["kb"]
            t_mat, u, w, p_mat, decay = tm["t"], tm["u"], tm["w"], tm["p"], tm["decay"]
            s_in, ds_out = st_ref[:, cj], dstate[...]
            cat_rows = lambda a, b: jnp.concatenate([a, b], axis=1)
            top, bot = slice(0, CHUNK), slice(CHUNK, 2 * CHUNK)
            vn = u - _bnn(w, s_in)
            qe, kd, ke = q * e, k * ed, kb * e
            dvn = _btn(p_mat, d_o) + _bnn(kd, ds_out)
            by_state = _bnt(cat_rows(d_o, dvn), s_in)
            dqe, dw = by_state[:, top], -by_state[:, bot]
            dq = dqe * e
            dgc = rsum(dqe * qe)
            dp = jnp.where(incl, _bnt(d_o, vn), 0.0)
            dkd = _bnt(vn, ds_out)
            dk = dkd * ed
            r = rsum(dkd * kd)
            dgc = dgc - r
            dg_last = (jnp.sum(r, axis=1, keepdims=True)
                       + jnp.sum(rsum(ds_out * s_in), axis=1, keepdims=True) * el)
            d_uw = _btn(t_mat, jnp.concatenate([dvn, dw], axis=2))
            dbv, dke = d_uw[:, :, 0:HEAD], d_uw[:, :, HEAD:2 * HEAD]
            da = -jnp.where(strict, _bnt(d_uw, tm["uw"]), 0.0)
            mn = cat_rows(da * decay, dp * decay)
            by_k = _bnn(mn, k)
            dkb = by_k[:, top] + dke * e
            dq = dq + by_k[:, bot]
            dk = dk + _btn(mn, cat_rows(kb, q))
            g_mat = da * tm["a"] + dp * p_mat
            g_hi, g_mid, _ = _split3(g_mat)
            col = _btn(cat_rows(g_hi, g_mid), ones)[:, :, 0:1]
            dgc = dgc + rsum(g_mat) - col + rsum(dke * ke)
            dgc = dgc + jnp.where(row_id == (0 if reverse else CHUNK - 1), dg_last, 0.0)
            dv = dbv * beta
            dbeta = rsum(dbv * v) + rsum(dkb * k)
            dk = dk + dkb * beta
            dstate[...] = el * ds_out + _btn(cat_rows(qe, w), cat_rows(d_o, -dvn))
            lb = nh if reverse else 0
            for hd in range(nh):
                cols = slice(hd * HEAD, (hd + 1) * HEAD)
                extra = [a[rows, cols] for a in add_refs] if n_add else [0.0, 0.0, 0.0]
                dq_ref[rows, cols] = dq[hd] + extra[0]
                dk_ref[rows, cols] = dk[hd] + extra[1]
                dv_ref[rows, cols] = dv[hd] + extra[2]
                dbg_ref[rows, lb + hd:lb + hd + 1] = dbeta[hd]
                dbg_ref[rows, 2 * nh + lb + hd:2 * nh + lb + hd + 1] = dgc[hd]
        host.wait()

    out3 = SDS((s, nh * HEAD), F32)
    n_in = 8 + n_add
    out = pl.pallas_call(
        body, grid=(nb,),
        in_specs=qkv + [bg_spec, gct_spec, rows_spec, st, tri] + [rows_spec] * n_add + host.in_specs,
        out_specs=[rows_spec, rows_spec, rows_spec, bg_spec] + host.out_specs,
        out_shape=[out3, out3, out3, SDS((s, LANES), F32)] + host.out_shapes,
        input_output_aliases=host.aliases(n_in, 4), scratch_shapes=[pltpu.VMEM((nh, HEAD, HEAD), F32)] + host.sems,
        compiler_params=_params("arbitrary"), name=name)(
            qkvn, qkvn, qkvn, bg, gct, do, states, tris, *(add or ()), *host.arrays)
    return host.finish(out, 4)


def _row_pieces(g):
    return g.reshape(N_CHIPS, g.shape[0] // N_CHIPS, g.shape[1])


def _up_swiglu(name, u, w_up, tm=1024, ex=None):
    s, k = u.shape
    fh = w_up.shape[2]
    tm = _tile(s, tm, 8)
    grid = (2, s // tm)
    host = _Hosted(ex, name, grid)

    def body(u_ref, wa_ref, wb_ref, *rest):
        (a_ref, b_ref, o_ref), _ = host.split(rest, 3)
        host.start()
        x = u_ref[...]
        a, b = _nn(x, wa_ref[...]), _nn(x, wb_ref[...])
        a_ref[...], b_ref[...] = a.astype(BF16), b.astype(BF16)
        o_ref[...] = (jax.nn.silu(a) * b).astype(BF16)
        host.wait()

    tile = pl.BlockSpec((tm, fh), lambda j, i: (i, j))
    out = pl.pallas_call(
        body, grid=grid,
        in_specs=[pl.BlockSpec((tm, k), lambda j, i: (i, 0)), pl.BlockSpec((None, k, fh), lambda j, i: (j, 0, 0)),
                  pl.BlockSpec((None, k, fh), lambda j, i: (2 + j, 0, 0))] + host.in_specs,
        out_specs=[tile] * 3 + host.out_specs, out_shape=[SDS((s, 2 * fh), BF16)] * 3 + host.out_shapes,
        input_output_aliases=host.aliases(3, 3), scratch_shapes=host.sems,
        compiler_params=_params("arbitrary", "arbitrary"), name=name)(u, w_up, w_up, *host.arrays)
    return host.finish(out, 3)


def _matmul_resid(name, a, w, h, g, scale, tm=1024, ex=None):
    s, k = a.shape
    d = w.shape[1]
    tm = _tile(s, tm, 16)
    grid = (s // tm,)
    host = _Hosted(ex, name, grid)

    def body(a_ref, w_ref, h_ref, g_ref, *rest):
        (o_ref, f_ref), _ = host.split(rest, 2)
        host.start()
        f = _nn(a_ref[...], w_ref[...])
        f_ref[...] = f.astype(BF16)
        o_ref[...] = h_ref[...] + (scale * g_ref[...]) * f
        host.wait()

    out = pl.pallas_call(
        body, grid=grid, in_specs=[_row(tm, k), _vec(k, d), _row(tm, d), _vec(1, d)] + host.in_specs,
        out_specs=[_row(tm, d), _row(tm, d)] + host.out_specs,
        out_shape=[SDS((s, d), F32), SDS((s, d), BF16)] + host.out_shapes,
        input_output_aliases=host.aliases(4, 2), scratch_shapes=host.sems,
        compiler_params=_params("arbitrary"), name=name)(a, w, h, g, *host.arrays)
    return host.finish(out, 2)


def _down_swiglu_bwd(name, df, w_down, a_pre, b_pre, tm=512):
    s, d = df.shape
    f = w_down.shape[0]
    tm = _tile(s, tm, 16)

    def body(df_ref, w_ref, a_ref, b_ref, o_ref):
        dhm = _nt(df_ref[...], w_ref[...])
        a = a_ref[...].astype(F32)
        o_ref[:, 0:f] = (dhm * b_ref[...].astype(F32) * _silu_grad(a)).astype(BF16)
        o_ref[:, f:2 * f] = (dhm * jax.nn.silu(a)).astype(BF16)

    return pl.pallas_call(
        body, grid=(s // tm,), in_specs=[_row(tm, d), _vec(f, d), _row(tm, f), _row(tm, f)],
        out_specs=_row(tm, 2 * f), out_shape=SDS((s, 2 * f), BF16),
        compiler_params=_params("parallel"), name=name)(df, w_down, a_pre, b_pre)


def _col_pieces(w):
    return w if w.ndim == 3 else w.reshape(w.shape[0], N_CHIPS, -1).transpose(1, 0, 2)


def _ffn_fwd(tag, h, nw, sh, sc, g, w_up, w_down, ex=None):
    u = _norm_mod(tag + "_norm", h, nw, sh, sc)
    a_pre, b_pre, hm = _up_swiglu(tag + "_up", u, _col_pieces(w_up), ex=ex)
    w_down = w_down() if callable(w_down) else w_down
    h_new, f = _matmul_resid(tag + "_down", hm, w_down, h, g, 0.5, ex=ex)
    return h_new, (h, u, a_pre, b_pre, hm, f, w_down)


def _ffn_bwd(tag, dh, saved, nw, sc, g, w_up, ex=None, on_gw_down=None):
    h, u, a_pre, b_pre, hm, f, w_down = saved
    df, acc_g = _resid_bwd(tag + "_res_bwd", dh, f, g, 0.5)
    shard = hm.shape[1] // 2
    gw_down = _row_pieces(_matmul(tag + "_gw_down", hm, df, "tn", out_dtype=BF16, tm=shard, ex=ex))
    if on_gw_down:
        on_gw_down(gw_down)
    dab = _down_swiglu_bwd(tag + "_dhm", df, w_down, a_pre, b_pre)
    gw_up = _matmul(tag + "_gw_up", u, dab, "tn", out_dtype=BF16, tn=shard, out_pieces=N_CHIPS, ex=ex)
    du = _matmul(tag + "_du", dab, w_up, "nt", ex=ex)
    dh_in, acc = _norm_mod_bwd(tag + "_norm_bwd", h, du, dh, nw, sc)
    return dh_in, gw_up, gw_down, (acc[0], acc[1], acc_g[0], acc[2])


def _mixer_fwd(h, nw, sh, sc, g, wt, lay, ex=None):
    nh = lay.nh
    u = _norm_mod("mix_norm", h, nw, sh, sc)
    proj = _matmul("mix_in", u, wt["w_in"], "nn", tn=2048, ex=ex)
    if ex:
        wt = dict(wt, **_as_operands(ex.weights("mix"), lay))
    qkvn = _prep_b("mix_prep_b", proj, wt["conv_dn"], lay)
    ya = _prep_a("mix_prep_a", proj, wt["conv_a"], lay)
    bg = _prep_c("mix_prep_c", proj, wt["pvec"], lay)
    gct = bg[:, 4 * nh:6 * nh].T
    o_f, *st_f = _delta_fwd("delta_fwd_l2r", qkvn, bg, gct, nh, False)
    o_b, *st_b = _delta_fwd("delta_fwd_r2l", qkvn, bg, gct, nh, True)
    yb = _post("mix_post", o_f, o_b, proj, wt["dn_norm"], lay)
    pa = _matmul("mix_a_out", ya, wt["w_a_out"], "nn")
    pb, mg = _b_out_merge("mix_b_out", yb, wt["w_b_out"], pa, proj, lay)
    h2, y = _matmul_resid("mix_out", mg, wt["w_out"], h, g, 1.0, ex=ex)
    return h2, (h, u, proj, qkvn, ya, bg, gct, o_f, o_b, st_f, st_b, yb, pa, pb, mg, y)


def _mixer_bwd(dh, saved, nw, sc, g, wt, lay, ex=None):
    h, u, proj, qkvn, ya, bg, gct, o_f, o_b, st_f, st_b, yb, pa, pb, mg, y = saved
    nh = lay.nh
    dy, acc_g = _resid_bwd("mix_res_bwd", dh, y, g, 1.0)
    gw_out = _matmul("mix_gw_out", mg, dy, "tn", out_dtype=BF16, ex=ex)
    dmg = _matmul("mix_dmg", dy, wt["w_out"], "nt")
    dpa, dpb, dproj = _merge_bwd("mix_merge_bwd", dmg, pa, pb, proj, lay)
    gw_a = _matmul("mix_gw_a", ya, dpa, "tn", out_dtype=BF16, out_pieces=N_CHIPS)
    gw_b = _matmul("mix_gw_b", yb, dpb, "tn", out_dtype=BF16)
    dya = _matmul("mix_dya", dpa, wt["w_a_out"], "nt")
    dyb = _matmul("mix_dyb", dpb, wt["w_b_out"], "nt")
    do, dproj, acc_dn = _post_bwd("mix_post_bwd", dyb, o_f, o_b, proj, wt["dn_norm"], dproj, lay)
    dq, dk, dv, dbg_f = _delta_bwd("delta_bwd_l2r", qkvn, bg, gct, do, *st_f, nh, False, ex=ex)
    dq, dk, dv, dbg_b = _delta_bwd("delta_bwd_r2l", qkvn, bg, gct, do, *st_b, nh, True, add=(dq, dk, dv), ex=ex)
    dproj, acc_ca = _prep_a_bwd("mix_prep_a_bwd", dya, proj, wt["conv_a"], dproj, lay)
    dproj, acc_cd = _prep_b_bwd("mix_prep_b_bwd", dq, dk, dv, proj, wt["conv_dn"], dproj, lay)
    dproj, acc_pc = _prep_c_bwd("mix_prep_c_bwd", dbg_f, dbg_b, proj, wt["pvec"], dproj, lay)
    gw_in = lay.unperm_cols(_matmul("mix_gw_in", u, dproj, "tn", out_dtype=BF16))
    gw_in = gw_in.reshape(gw_in.shape[0], N_CHIPS, -1).transpose(1, 0, 2)
    du = _matmul("mix_du", dproj, wt["w_in"], "nt")
    dh_in, acc = _norm_mod_bwd("mix_norm_bwd", h, du, dh, nw, sc)
    small = dict(conv_a=acc_ca[0:wt["conv_a"].shape[0]], conv_dn=acc_cd[0:wt["conv_dn"].shape[0]],
                 dn_norm=acc_dn[0:1], a_log=acc_pc[0], dt_bias=acc_pc[1])
    big = dict(w_in=gw_in, w_a_out=gw_a, w_b_out=_row_pieces(gw_b), w_out=_row_pieces(gw_out))
    return dh_in, big, small, (acc[0], acc[1], acc_g[0], acc[2])


def _as_operands(gathered, lay):
    wt = {}
    for n, g in gathered.items():
        if n == "w_in":
            wt[n] = lay.perm_cols(jnp.concatenate(list(g), axis=1))
        else:
            wt[n] = g if n in COL_SHARDED else g.reshape(-1, g.shape[-1])
    return wt


def _local_step(x, tgt, modv, wt, lay, ex=None):
    m = [modv[i:i + 1] for i in range(9)]

    def ffn1_down():
        ex.run(("gather_pass", "down1"))
        return _as_operands(ex.weights("down1"), lay)["w_ffn1_down"]

    h1, sv1 = _ffn_fwd("ffn1", x, wt["norm_ffn1"], m[0], m[1], m[2], wt["w_ffn1_up"],
                       ffn1_down if ex else wt["w_ffn1_down"], ex)
    if ex:
        wt = dict(wt, **_as_operands(ex.weights("in"), lay))
    h2, sv2 = _mixer_fwd(h1, wt["norm_mix"], m[3], m[4], m[5], wt, lay, ex)
    if ex:
        wt = dict(wt, **_as_operands(ex.weights("mix", "ffn2"), lay))
    h3, sv3 = _ffn_fwd("ffn2", h2, wt["norm_ffn2"], m[6], m[7], m[8], wt["w_ffn2_up"], wt["w_ffn2_down"])
    dh3, acc_f = _final_loss("final_loss", h3, tgt, wt["norm_final"])
    loss = jnp.sum(acc_f[1])
    dh2, gu2, gd2, dm3 = _ffn_bwd("ffn2", dh3, sv3, wt["norm_ffn2"], m[7], m[8], wt["w_ffn2_up"])
    if ex:
        ex.reduce("ffn2", dict(w_ffn2_up=gu2, w_ffn2_down=gd2))
    dh1, gmix, small, dm2 = _mixer_bwd(dh2, sv2, wt["norm_mix"], m[4], m[5], wt, lay, ex)
    if ex:
        ex.reduce("mixer", gmix)
    dx, gu1, gd1, dm1 = _ffn_bwd("ffn1", dh1, sv1, wt["norm_ffn1"], m[1], m[2], wt["w_ffn1_up"], ex,
                                 (lambda g: ex.reduce("down1", dict(w_ffn1_down=g))) if ex else None)
    dmod = jnp.stack([dm1[0], dm1[1], dm1[2], dm2[0], dm2[1], dm2[2], dm3[0], dm3[1], dm3[2]])
    big = dict(w_ffn1_up=gu1, w_ffn1_down=gd1, w_ffn2_up=gu2, w_ffn2_down=gd2, **gmix)
    small = dict(small, norm_ffn1=dm1[3], norm_mix=dm2[3], norm_ffn2=dm3[3], norm_final=acc_f[0])
    return loss, dx, dmod, big, small


def _position():
    return lax.axis_index("x"), lax.axis_index("y"), lax.axis_index("c")


_ANY = pl.BlockSpec(memory_space=pl.ANY)
_VMEM = pl.BlockSpec(memory_space=pltpu.VMEM)


def _allgather8(name, v):
    r = v.shape[0]

    def body(v_ref, out_ref, send_sems, recv_sems):
        x, y, c = _position()
        me = 4 * x + 2 * y + c
        out_ref[me] = v_ref[...]
        copies = []
        for mask in range(1, N_DEV):
            peer = tuple(1 - p if mask >> b & 1 else p for p, b in ((x, 2), (y, 1), (c, 0)))
            cp = pltpu.make_async_remote_copy(
                src_ref=v_ref, dst_ref=out_ref.at[me], send_sem=send_sems.at[mask - 1],
                recv_sem=recv_sems.at[mask - 1], device_id=peer, device_id_type=MESH)
            cp.start()
            copies.append(cp)
        for cp in copies:
            cp.wait()

    return pl.pallas_call(
        body, in_specs=[_VMEM], out_specs=_VMEM, out_shape=SDS((N_DEV, r, LANES), F32),
        scratch_shapes=[pltpu.SemaphoreType.DMA((N_DEV - 1,)), pltpu.SemaphoreType.DMA((N_DEV - 1,))],
        name=name)(v)


def _other_chips(x, y):
    return [(1 - x, y), (x, 1 - y), (1 - x, 1 - y)]


def _half_rows(c, rows):
    hr = rows // 2
    assert hr % 16 == 0
    return pl.ds(pl.multiple_of(c * hr, 16), hr)


class _Stage:
    def __init__(self, arrays, out_shapes, sems, plan, in_place=False):
        self.arrays, self.out_shapes, self.sems, self.plan = list(arrays), list(out_shapes), list(sems), plan
        self.alias_pairs = [(i, i) for i in range(len(self.arrays))] if in_place else []
        self.out_counts = [len(self.out_shapes)]

    def start(self, ins, outs, sems):
        for kind, cp in self.plan(ins, outs, sems):
            if kind != "recv":
                cp.start()

    def wait(self, ins, outs, sems):
        for kind, cp in self.plan(ins, outs, sems):
            {"local": cp.wait, "both": cp.wait, "send": cp.wait_send, "recv": cp.wait_recv}[kind]()

    def aliases(self, in_offset, out_offset):
        return {in_offset + i: out_offset + o for i, o in self.alias_pairs}


def _join(stages):
    ni = [0] + [len(st.arrays) for st in stages]
    no = [0] + [len(st.out_shapes) for st in stages]
    ns = [0] + [len(st.sems) for st in stages]
    for counts in (ni, no, ns):
        for k in range(1, len(counts)):
            counts[k] += counts[k - 1]

    def plan(ins, outs, sems):
        todo = []
        for k, st in enumerate(stages):
            todo += st.plan(ins[ni[k]:ni[k + 1]], outs[no[k]:no[k + 1]], sems[ns[k]:ns[k + 1]])
        return todo

    joined = _Stage([a for st in stages for a in st.arrays], [o for st in stages for o in st.out_shapes],
                    [m for st in stages for m in st.sems], plan)
    joined.alias_pairs = [(ni[k] + i, no[k] + o) for k, st in enumerate(stages) for i, o in st.alias_pairs]
    joined.out_counts = [len(st.out_shapes) for st in stages]
    return joined


def _run_stage(name, stage):
    n_in, n_out = len(stage.arrays), len(stage.out_shapes)

    def body(*refs):
        ins, outs, sems = refs[0:n_in], refs[n_in:n_in + n_out], refs[n_in + n_out:]
        stage.start(ins, outs, sems)
        stage.wait(ins, outs, sems)

    return pl.pallas_call(
        body, in_specs=[_ANY] * n_in, out_specs=[_ANY] * n_out, out_shape=stage.out_shapes,
        input_output_aliases=stage.aliases(0, 0), scratch_shapes=stage.sems, name=name)(*stage.arrays)


def _dma_sems(*counts):
    return [pltpu.SemaphoreType.DMA((n,)) for n in counts]


def _gather_send(shards):
    nw = len(shards)

    def plan(ins, outs, sems):
        send_sems, recv_sems, local_sems = sems
        x, y, c = _position()
        p = 2 * x + y
        todo = [("local", pltpu.make_async_copy(ins[w], outs[w].at[p], local_sems.at[w])) for w in range(nw)]
        for j, (cx, cy) in enumerate(_other_chips(x, y)):
            for w in range(nw):
                half = _half_rows(c, ins[w].shape[0])
                sem = dict(send_sem=send_sems.at[j * nw + w], recv_sem=recv_sems.at[j * nw + w], device_id_type=MESH)
                todo.append(("send", pltpu.make_async_remote_copy(
                    src_ref=ins[w].at[half], dst_ref=outs[w].at[p, half], device_id=(cx, cy, c), **sem)))
                landing = outs[w].at[2 * cx + cy, half]
                todo.append(("recv", pltpu.make_async_remote_copy(
                    src_ref=landing, dst_ref=landing, device_id=(x, y, c), **sem)))
        return todo

    return _Stage(shards, [SDS((N_CHIPS,) + v.shape, v.dtype) for v in shards], _dma_sems(3 * nw, 3 * nw, nw), plan)


def _gather_pass(gathered):
    nw = len(gathered)

    def plan(ins, outs, sems):
        send_sems, recv_sems = sems
        x, y, c = _position()
        todo = []
        for j, (cx, cy) in enumerate(_other_chips(x, y)):
            for w in range(nw):
                sem = dict(send_sem=send_sems.at[j * nw + w], recv_sem=recv_sems.at[j * nw + w], device_id_type=MESH)
                mine = outs[w].at[2 * cx + cy, _half_rows(c, outs[w].shape[1])]
                theirs = outs[w].at[2 * cx + cy, _half_rows(1 - c, outs[w].shape[1])]
                todo.append(("send", pltpu.make_async_remote_copy(
                    src_ref=mine, dst_ref=mine, device_id=(x, y, 1 - c), **sem)))
                todo.append(("recv", pltpu.make_async_remote_copy(
                    src_ref=theirs, dst_ref=theirs, device_id=(x, y, c), **sem)))
        return todo

    return _Stage(gathered, [SDS(g.shape, g.dtype) for g in gathered], _dma_sems(3 * nw, 3 * nw), plan, in_place=True)


def _swap_halves(gs):
    nw = len(gs)

    def plan(ins, outs, sems):
        x, y, c = _position()
        return [("both", pltpu.make_async_remote_copy(
            src_ref=ins[w].at[:, _half_rows(1 - c, ins[w].shape[1])], dst_ref=outs[w], send_sem=sems[0].at[w],
            recv_sem=sems[1].at[w], device_id=(x, y, 1 - c), device_id_type=MESH)) for w in range(nw)]

    return _Stage(gs, [SDS((g.shape[0], g.shape[1] // 2, g.shape[2]), g.dtype) for g in gs], _dma_sems(nw, nw), plan)


def _scatter_chips(vs):
    nw = len(vs)

    def plan(ins, outs, sems):
        x, y, c = _position()
        return [("both", pltpu.make_async_remote_copy(
            src_ref=ins[w].at[2 * cx + cy], dst_ref=outs[w].at[j], send_sem=sems[0].at[j * nw + w],
            recv_sem=sems[1].at[j * nw + w], device_id=(cx, cy, c), device_id_type=MESH))
            for j, (cx, cy) in enumerate(_other_chips(x, y)) for w in range(nw)]

    return _Stage(vs, [SDS((N_CHIPS - 1,) + v.shape[1:], v.dtype) for v in vs], _dma_sems(3 * nw, 3 * nw), plan)


def _share_halves(fulls):
    nw = len(fulls)

    def plan(ins, outs, sems):
        x, y, c = _position()
        todo = []
        for w in range(nw):
            rows = outs[w].at[_half_rows(c, outs[w].shape[0])]
            todo.append(("both", pltpu.make_async_remote_copy(
                src_ref=rows, dst_ref=rows, send_sem=sems[0].at[w], recv_sem=sems[1].at[w],
                device_id=(x, y, 1 - c), device_id_type=MESH)))
        return todo

    return _Stage(fulls, [SDS(f.shape, f.dtype) for f in fulls], _dma_sems(nw, nw), plan, in_place=True)


class _Hosted:
    def __init__(self, ex, name, grid):
        self.ex, self.name, self.grid = ex, name, grid
        self.stage = ex.host(name) if ex is not None else None
        st = self.stage
        self.arrays = list(st.arrays) if st else []
        self.out_shapes = list(st.out_shapes) if st else []
        self.sems = list(st.sems) if st else []
        self.in_specs, self.out_specs = [_ANY] * len(self.arrays), [_ANY] * len(self.out_shapes)

    def aliases(self, in_offset, out_offset):
        return self.stage.aliases(in_offset, out_offset) if self.stage else {}

    def split(self, rest, n_out):
        ni, no, ns = len(self.arrays), len(self.out_shapes), len(self.sems)
        self.ins, self.outs = rest[0:ni], rest[ni + n_out:ni + n_out + no]
        tail = rest[ni + n_out + no:]
        self.sem_refs = tail[len(tail) - ns:]
        return rest[ni:ni + n_out], tail[0:len(tail) - ns]

    def _at(self, last):
        conds = [pl.program_id(d) == (g - 1 if last else 0) for d, g in enumerate(self.grid)]
        return functools.reduce(lambda p, q: p & q, conds)

    def start(self):
        if self.stage:
            pl.when(self._at(False))(lambda: self.stage.start(self.ins, self.outs, self.sem_refs))

    def wait(self):
        if self.stage:
            pl.when(self._at(True))(lambda: self.stage.wait(self.ins, self.outs, self.sem_refs))

    def finish(self, out, n_out):
        out = list(out)
        if self.stage:
            self.ex.done(self.name, out[n_out:])
        return out[0:n_out]


GATHER = {"up1": ("w_ffn1_up",), "down1": ("w_ffn1_down",), "in": ("w_in",),
          "mix": ("w_a_out", "w_b_out", "w_out"), "ffn2": ("w_ffn2_up", "w_ffn2_down")}
REDUCE = {"ffn2": ("w_ffn2_up", "w_ffn2_down"), "mixer": ("w_in", "w_a_out", "w_b_out", "w_out"),
          "down1": ("w_ffn1_down",), "up1": ("w_ffn1_up",)}
HOSTS = {"ffn1_up": [("gather_send", "down1"), ("gather_send", "in")],
         "ffn1_down": [("gather_pass", "in"), ("gather_send", "mix")],
         "mix_in": [("gather_pass", "mix"), ("gather_send", "ffn2")],
         "mix_out": [("gather_pass", "ffn2")],
         "mix_gw_out": [("swap", "ffn2")], "delta_bwd_l2r": [("scatter", "ffn2")], "delta_bwd_r2l": [("share", "ffn2")],
         "ffn1_gw_down": [("swap", "mixer")], "ffn1_gw_up": [("scatter", "mixer"), ("swap", "down1")],
         "ffn1_du": [("share", "mixer"), ("scatter", "down1")]}


class _Exchange:
    def __init__(self, shards):
        self.shards = shards
        self.gathered, self.red, self.ready, self.reduced = {}, {}, {}, {}

    def _stage(self, kind, group):
        if kind == "gather_send":
            return _gather_send([self.shards[n] for n in GATHER[group]])
        if kind == "gather_pass":
            return _gather_pass(self.gathered[group])
        st = self.red[group]
        return {"swap": lambda: _swap_halves(st["parts"]), "scatter": lambda: _scatter_chips(st["chip_sums"]),
                "share": lambda: _share_halves(st["fulls"])}[kind]()

    def _done(self, kind, group, outs):
        if kind == "gather_send":
            self.gathered[group] = list(outs)
        elif kind == "gather_pass":
            self.ready.update(zip(GATHER[group], outs))
        elif kind == "swap":
            st = self.red[group]
            st["from_sib"] = list(outs)
            st["chip_sums"] = [_chip_sum("chip_sum_" + n, g, f) for n, g, f in zip(REDUCE[group], st["parts"], outs)]
        elif kind == "scatter":
            st = self.red[group]
            st["fulls"] = [_total("total_" + n, g, f, r)
                           for n, g, f, r in zip(REDUCE[group], st["parts"], st["from_sib"], outs)]
        else:
            self.reduced.update(zip(REDUCE[group], outs))

    def _all(self, steps, runner):
        stage = _join([self._stage(kind, group) for kind, group in steps])
        outs = list(runner(stage))
        for (kind, group), n in zip(steps, stage.out_counts):
            self._done(kind, group, outs[0:n])
            outs = outs[n:]

    def host(self, kernel_name):
        return _join([self._stage(*step) for step in HOSTS[kernel_name]]) if kernel_name in HOSTS else None

    def done(self, kernel_name, outs):
        self._all(HOSTS[kernel_name], lambda stage: outs)

    def run(self, *steps):
        name = "_".join(kind + "_" + group for kind, group in steps)
        self._all(steps, lambda stage: _run_stage(name, stage))

    def weights(self, *groups):
        return {n: self.ready[n] for g in groups for n in GATHER[g]}

    def reduce(self, group, parts):
        self.red[group] = dict(parts=[parts[n] for n in REDUCE[group]])


def _chip_sum(name, g, from_sib):
    _, r, cdim = g.shape
    hr = r // 2

    def body(g_ref, s_ref, o_ref):
        o_ref[...] = (g_ref[...].astype(F32) + s_ref[...].astype(F32)).astype(o_ref.dtype)

    blk = (None, hr, cdim)
    return pl.pallas_call(
        body, grid=(N_CHIPS,),
        in_specs=[pl.BlockSpec(blk, lambda j: (j, lax.axis_index("c"), 0)), pl.BlockSpec(blk, lambda j: (j, 0, 0))],
        out_specs=pl.BlockSpec(blk, lambda j: (j, 0, 0)),
        out_shape=SDS((N_CHIPS, hr, cdim), g.dtype), compiler_params=_params("parallel"), name=name)(g, from_sib)


def _total(name, g, from_sib, from_chips):
    _, r, cdim = g.shape
    hr = r // 2
    tr = _tile(hr, 256, 16)
    nt = hr // tr

    def body(g_ref, s_ref, rc_ref, o_ref):
        acc = g_ref[...].astype(F32) + s_ref[...].astype(F32)
        for j in range(N_CHIPS - 1):
            acc = acc + rc_ref[j].astype(F32)
        o_ref[...] = acc

    def chip():
        return 2 * lax.axis_index("x") + lax.axis_index("y")

    blk = (None, tr, cdim)
    return pl.pallas_call(
        body, grid=(nt,),
        in_specs=[pl.BlockSpec(blk, lambda i: (chip(), lax.axis_index("c") * nt + i, 0)),
                  pl.BlockSpec(blk, lambda i: (chip(), i, 0)),
                  pl.BlockSpec((N_CHIPS - 1, tr, cdim), lambda i: (0, i, 0))],
        out_specs=pl.BlockSpec((tr, cdim), lambda i: (lax.axis_index("c") * nt + i, 0)),
        out_shape=SDS((r, cdim), F32), compiler_params=_params("parallel"), name=name)(g, from_sib, from_chips)


def _sum8(name, v):
    _, r, w = v.shape

    def body(v_ref, o_ref):
        acc = v_ref[0]
        for j in range(1, N_DEV):
            acc = acc + v_ref[j]
        o_ref[...] = acc

    return pl.pallas_call(body, in_specs=[_VMEM], out_specs=_VMEM, out_shape=SDS((r, w), F32), name=name)(v)


def _adamw(name, w, g, m, v, tr=256):
    r, cdim = w.shape
    tr = _tile(r, tr, 8)
    bc1, bc2 = 1.0 - ADAM_B1 ** ADAM_STEP, 1.0 - ADAM_B2 ** ADAM_STEP

    def body(w_ref, g_ref, m_ref, v_ref, d_ref, nm_ref, nv_ref):
        g = g_ref[...]
        m2 = ADAM_B1 * m_ref[...] + (1.0 - ADAM_B1) * g
        v2 = ADAM_B2 * v_ref[...] + (1.0 - ADAM_B2) * (g * g)
        d_ref[...] = -ADAM_LR * ((m2 / bc1) / (jnp.sqrt(v2 / bc2) + ADAM_EPS) + ADAM_WD * w_ref[...])
        nm_ref[...] = m2
        nv_ref[...] = v2

    spec = _row(tr, cdim)
    out = SDS((r, cdim), F32)
    return pl.pallas_call(body, grid=(r // tr,), in_specs=[spec] * 4, out_specs=[spec] * 3, out_shape=[out] * 3,
                          compiler_params=_params("parallel"), name=name)(w, g, m, v)


def _pack_rows(arrays, width, row_mult, dtype):
    parts, spans, row = [], [], 0
    for a in arrays:
        n = a.size
        rows = -(-n // width)
        flat = a.reshape(-1).astype(dtype)
        if rows * width != n:
            flat = jnp.concatenate([flat, jnp.zeros((rows * width - n,), dtype)])
        parts.append(flat.reshape(rows, width))
        spans.append((row, rows, n, a.shape))
        row += rows
    pad = -row % row_mult
    if pad:
        parts.append(jnp.zeros((pad, width), dtype))
    return jnp.concatenate(parts, axis=0), spans


def _unpack_rows(packed, spans):
    return [packed[r0:r0 + rows].reshape(-1)[0:n].reshape(shape) for r0, rows, n, shape in spans]


BIG = ("w_ffn1_up", "w_ffn1_down", "w_in", "w_a_out", "w_b_out", "w_out", "w_ffn2_up", "w_ffn2_down")
COL_SHARDED = ("w_ffn1_up", "w_in", "w_a_out", "w_ffn2_up")
SMALL = ("b_ada", "norm_ffn1", "norm_mix", "conv_a", "conv_dn", "a_log_fwd", "dt_bias_fwd", "a_log_bwd",
         "dt_bias_bwd", "dn_norm", "norm_ffn2", "norm_final")
WEIGHTS = ("w_ada", "b_ada", "norm_ffn1", "w_ffn1_up", "w_ffn1_down", "norm_mix", "w_in", "conv_a", "conv_dn",
           "a_log_fwd", "dt_bias_fwd", "a_log_bwd", "dt_bias_bwd", "dn_norm", "w_a_out", "w_b_out", "w_out",
           "norm_ffn2", "w_ffn2_up", "w_ffn2_down", "norm_final")


def kernel(x, c, w_ada, b_ada, norm_ffn1, w_ffn1_up, w_ffn1_down, norm_mix, w_in, conv_a, conv_dn, a_log_fwd, dt_bias_fwd, a_log_bwd, dt_bias_bwd, dn_norm, w_a_out, w_b_out, w_out, norm_ffn2, w_ffn2_up, w_ffn2_down, norm_final, loss_target, m_w_ada, m_b_ada, m_norm_ffn1, m_w_ffn1_up, m_w_ffn1_down, m_norm_mix, m_w_in, m_conv_a, m_conv_dn, m_a_log_fwd, m_dt_bias_fwd, m_a_log_bwd, m_dt_bias_bwd, m_dn_norm, m_w_a_out, m_w_b_out, m_w_out, m_norm_ffn2, m_w_ffn2_up, m_w_ffn2_down, m_norm_final, v_w_ada, v_b_ada, v_norm_ffn1, v_w_ffn1_up, v_w_ffn1_down, v_norm_mix, v_w_in, v_conv_a, v_conv_dn, v_a_log_fwd, v_dt_bias_fwd, v_a_log_bwd, v_dt_bias_bwd, v_dn_norm, v_w_a_out, v_w_b_out, v_w_out, v_norm_ffn2, v_w_ffn2_up, v_w_ffn2_down, v_norm_final):
    given = dict(locals())
    wsh = {n: given[n] for n in WEIGHTS}
    msh = {n: given["m_" + n] for n in WEIGHTS}
    vsh = {n: given["v_" + n] for n in WEIGHTS}
    d = x.shape[-1]
    ca = conv_a.shape[-1] * N_CHIPS
    nh = conv_dn.shape[-1] * N_CHIPS // (3 * HEAD)
    lay = _Layout(d, ca, nh)
    xi, yi, ci = _position()
    chip = 2 * xi + yi
    me = 2 * chip + ci

    c_act = jax.nn.silu(c)
    g1, g1_spans = _pack_rows([c_act, conv_a[0], conv_dn[0]], LANES, 8, F32)
    g1_all = _allgather8("gather_cond", g1)
    per_dev = [_unpack_rows(g1_all[k], g1_spans) for k in range(N_DEV)]
    c_all = jnp.concatenate([p[0] for p in per_dev], axis=0)
    conv_a_full = jnp.concatenate([per_dev[2 * k][1] for k in range(N_CHIPS)], axis=1)
    conv_dn_full = jnp.concatenate([per_dev[2 * k][2] for k in range(N_CHIPS)], axis=1)

    mod_sh = _matmul("ada_mod", c_all, w_ada[0], "nn")
    b_sh = lax.dynamic_slice_in_dim(b_ada, chip * mod_sh.shape[1], mod_sh.shape[1], axis=1)
    g2, g2_spans = _pack_rows([mod_sh + b_sh], LANES, 8, F32)
    g2_all = _allgather8("gather_mod", g2)
    mod_all = jnp.concatenate([_unpack_rows(g2_all[2 * k], g2_spans)[0] for k in range(N_CHIPS)], axis=1)
    modv = lax.dynamic_index_in_dim(mod_all, me, 0, keepdims=False).reshape(9, d)

    ex = _Exchange({n: wsh[n][0].astype(BF16) for n in BIG})
    ex.run(("gather_send", "up1"))
    ex.run(("gather_pass", "up1"))
    wt = _as_operands(ex.weights("up1"), lay)
    lane_pad = (jnp.zeros((2 * nh,), F32), jnp.zeros((LANES - 4 * nh,), F32))
    pvec = jnp.stack([jnp.concatenate([lane_pad[0], a_log_fwd[0], a_log_bwd[0], lane_pad[1]]),
                      jnp.concatenate([lane_pad[0], dt_bias_fwd[0], dt_bias_bwd[0], lane_pad[1]])]
                     + [jnp.zeros((LANES,), F32)] * 6)
    wt.update(conv_a=conv_a_full, conv_dn=conv_dn_full, pvec=pvec, dn_norm=dn_norm, norm_ffn1=norm_ffn1,
              norm_mix=norm_mix, norm_ffn2=norm_ffn2, norm_final=norm_final.reshape(1, d))

    loss, dx, dmod, big, small = _local_step(x[0], loss_target[0], modv, wt, lay, ex)

    small_list = [loss.reshape(1, 1), dmod.reshape(1, 9 * d), small["norm_ffn1"], small["norm_mix"], small["conv_a"],
                  small["conv_dn"], small["a_log"][2 * nh:3 * nh], small["dt_bias"][2 * nh:3 * nh],
                  small["a_log"][3 * nh:4 * nh], small["dt_bias"][3 * nh:4 * nh], small["dn_norm"], small["norm_ffn2"],
                  small["norm_final"]]
    g3, g3_spans = _pack_rows(small_list, LANES, 8, F32)
    g3_all = _allgather8("gather_small_grads", g3)
    g_small = dict(zip(("loss",) + SMALL, _unpack_rows(_sum8("sum_small_grads", g3_all), g3_spans)))
    loss = g_small["loss"].reshape(())
    dmod_all = jnp.concatenate([_unpack_rows(g3_all[k], g3_spans)[1] for k in range(N_DEV)], axis=0)
    ncol = w_ada.shape[-1]
    dmod_sh = lax.dynamic_slice_in_dim(dmod_all, chip * ncol, ncol, axis=1)
    grads = {"w_ada": _matmul("ada_grad", c_all, dmod_sh, "tn")[None]}
    for n in SMALL:
        g = g_small[n]
        if n in ("conv_a", "conv_dn"):
            wloc = wsh[n].shape[-1]
            g = lax.dynamic_slice_in_dim(g, chip * wloc, wloc, axis=1)
        grads[n] = g.reshape(wsh[n].shape)

    ex.reduce("up1", big)
    ex.run(("share", "down1"), ("swap", "up1"))
    ex.run(("scatter", "up1"))
    ex.run(("share", "up1"))
    for n in BIG:
        grads[n] = ex.reduced[n][None]

    delta, new_m, new_v = {}, {}, {}
    for n in ("w_ada",) + BIG:
        shp = wsh[n].shape
        outs = _adamw("adamw_" + n, *(t.reshape(shp[-2], shp[-1]) for t in (wsh[n], grads[n], msh[n], vsh[n])))
        delta[n], new_m[n], new_v[n] = (o.reshape(shp) for o in outs)
    packed = []
    for src in (wsh, grads, msh, vsh):
        pk, s_spans = _pack_rows([src[n] for n in SMALL], LANES, 8, F32)
        packed.append(pk)
    outs = _adamw("adamw_small", *packed)
    for dst, o in zip((delta, new_m, new_v), outs):
        dst.update(zip(SMALL, _unpack_rows(o, s_spans)))

    return (loss, dx[None], *[grads[n] for n in WEIGHTS], *[delta[n] for n in WEIGHTS],
            *[new_m[n] for n in WEIGHTS], *[new_v[n] for n in WEIGHTS])
```
